```python
import jax, jax.numpy as jnp
from jax import lax
import numpy as np

D_MODEL = 1024
BATCH = 8
SEQ = 8192
DEPTH = 1

CHUNK = 64
Q_BLOCK = 128
FOX_HEAD_DIM = 128
N_FOX_HEADS = D_MODEL // FOX_HEAD_DIM
FOX_WIDTH = N_FOX_HEADS * FOX_HEAD_DIM
SGU_GROUP_DIM = 128
N_SGU_GROUPS = D_MODEL // SGU_GROUP_DIM
SGU_WIDTH = N_SGU_GROUPS * SGU_GROUP_DIM
SGU_LEN = 128
D_FF = 4 * D_MODEL
RMS_EPS = 1e-6
LN_EPS = 1e-5

COL_Q = 0
COL_K = COL_Q + FOX_WIDTH
COL_V = COL_K + FOX_WIDTH
COL_F = COL_V + FOX_WIDTH
COL_U = COL_F + N_FOX_HEADS
COL_SV = COL_U + SGU_WIDTH
COL_GA = COL_SV + SGU_WIDTH
COL_GB = COL_GA + D_MODEL
IN_WIDTH = COL_GB + D_MODEL

kernel_name = "fox_gmlp_gated_macaron_block"


def rmsnorm(x, g):
    xf = x.astype(jnp.float32)
    y = xf * lax.rsqrt(jnp.mean(xf * xf, axis=-1, keepdims=True) + RMS_EPS)
    return (y * g.astype(jnp.float32)).astype(x.dtype)


def swiglu(h, w_gate, w_up, w_down):
    return (jax.nn.silu(h @ w_gate) * (h @ w_up)) @ w_down


def forgetting_attention(q, k, v, f_logit, b_forget):
    B, S, H, D = q.shape
    nb = S // Q_BLOCK
    scale = 1.0 / np.sqrt(D).astype(np.float32)
    log_f = jax.nn.log_sigmoid(f_logit.astype(jnp.float32) + b_forget.astype(jnp.float32))
    c = jnp.cumsum(log_f, axis=1).transpose(0, 2, 1)
    kh = k.transpose(0, 2, 1, 3)
    vh = v.transpose(0, 2, 1, 3)
    qb = q.transpose(0, 2, 1, 3).reshape(B, H, nb, Q_BLOCK, D).transpose(2, 0, 1, 3, 4)
    cb = c.reshape(B, H, nb, Q_BLOCK).transpose(2, 0, 1, 3)
    k_pos = jnp.arange(S)

    def block(args):
        q_blk, c_blk, idx = args
        s = jnp.einsum('bhqd,bhkd->bhqk', q_blk, kh).astype(jnp.float32) * scale
        s = s + c_blk[..., :, None] - c[:, :, None, :]
        q_pos = idx * Q_BLOCK + jnp.arange(Q_BLOCK)
        s = jnp.where(k_pos[None, :] <= q_pos[:, None], s, -jnp.inf)
        p = jax.nn.softmax(s, axis=-1).astype(vh.dtype)
        return jnp.einsum('bhqk,bhkd->bhqd', p, vh)

    out = lax.map(block, (qb, cb, jnp.arange(nb)))
    return out.transpose(1, 0, 3, 2, 4).reshape(B, S, H * D)


def spatial_gating(u, v, ln_g, ln_b, w_s, b_s):
    B, S, W = v.shape
    G, C, L = N_SGU_GROUPS, SGU_GROUP_DIM, SGU_LEN
    vf = v.astype(jnp.float32).reshape(B, S, G, C)
    mu = jnp.mean(vf, axis=-1, keepdims=True)
    var = jnp.mean(jnp.square(vf - mu), axis=-1, keepdims=True)
    vn = ((vf - mu) * lax.rsqrt(var + LN_EPS)).reshape(B, S, W)
    vn = (vn * ln_g.astype(jnp.float32) + ln_b.astype(jnp.float32)).reshape(B, S // L, L, G, C)
    pos = jnp.arange(L)
    mask = (pos[None, :] // CHUNK) <= (pos[:, None] // CHUNK)
    w = jnp.where(mask[None], w_s.astype(jnp.float32), 0.0)
    mixed = jnp.einsum('gts,bnsgc->bntgc', w, vn) + b_s.astype(jnp.float32).T[None, None, :, :, None]
    return u * mixed.reshape(B, S, W).astype(u.dtype)


def _fwd_setup_inputs(seed: int = 0) -> dict:
    key = jax.random.key(seed)
    ks = jax.random.split(key, 24)
    L, D, F = DEPTH, D_MODEL, D_FF
    nrm = lambda k, shape, fan_in: jax.random.normal(k, shape, jnp.float32) * (fan_in ** -0.5)
    gain = lambda k, shape: 1.0 + 0.05 * jax.random.normal(k, shape, jnp.float32)
    return {
        "x": jax.random.normal(ks[0], (BATCH, SEQ, D), jnp.float32),
        "ffn1_pre_g": gain(ks[1], (L, D)),
        "ffn1_w_gate": nrm(ks[2], (L, D, F), D),
        "ffn1_w_up": nrm(ks[3], (L, D, F), D),
        "ffn1_w_down": nrm(ks[4], (L, F, D), F),
        "ffn1_post_g": gain(ks[5], (L, D)),
        "mix_pre_g": gain(ks[6], (L, D)),
        "w_in": nrm(ks[7], (L, D, IN_WIDTH), D),
        "b_forget": jax.random.uniform(ks[8], (L, N_FOX_HEADS), jnp.float32, 2.0, 6.0),
        "sgu_ln_g": gain(ks[9], (L, SGU_WIDTH)),
        "sgu_ln_b": 0.02 * jax.random.normal(ks[10], (L, SGU_WIDTH), jnp.float32),
        "sgu_w_s": nrm(ks[11], (L, N_SGU_GROUPS, SGU_LEN, SGU_LEN), SGU_LEN),
        "sgu_b_s": 1.0 + 0.02 * jax.random.normal(ks[12], (L, N_SGU_GROUPS, SGU_LEN), jnp.float32),
        "w_out": nrm(ks[13], (L, D, D), D),
        "mix_post_g": gain(ks[14], (L, D)),
        "ffn2_pre_g": gain(ks[15], (L, D)),
        "ffn2_w_gate": nrm(ks[16], (L, D, F), D),
        "ffn2_w_up": nrm(ks[17], (L, D, F), D),
        "ffn2_w_down": nrm(ks[18], (L, F, D), F),
        "ffn2_post_g": gain(ks[19], (L, D)),
    }


def _fwd_reference(x, ffn1_pre_g, ffn1_w_gate, ffn1_w_up, ffn1_w_down, ffn1_post_g,
              mix_pre_g, w_in, b_forget, sgu_ln_g, sgu_ln_b, sgu_w_s, sgu_b_s,
              w_out, mix_post_g, ffn2_pre_g, ffn2_w_gate, ffn2_w_up, ffn2_w_down,
              ffn2_post_g):
    B, S, D = x.shape
    H, HD = N_FOX_HEADS, FOX_HEAD_DIM
    for l in range(DEPTH):
        h = rmsnorm(x, ffn1_pre_g[l])
        x = x + 0.5 * rmsnorm(swiglu(h, ffn1_w_gate[l], ffn1_w_up[l], ffn1_w_down[l]), ffn1_post_g[l])

        h = rmsnorm(x, mix_pre_g[l])
        z = h @ w_in[l]
        q = z[..., COL_Q:COL_K].reshape(B, S, H, HD)
        k = z[..., COL_K:COL_V].reshape(B, S, H, HD)
        v = z[..., COL_V:COL_F].reshape(B, S, H, HD)
        f_logit = z[..., COL_F:COL_U]
        u_s = jax.nn.gelu(z[..., COL_U:COL_SV], approximate=False)
        v_s = jax.nn.gelu(z[..., COL_SV:COL_GA], approximate=False)
        gate_a = jax.nn.sigmoid(z[..., COL_GA:COL_GB])
        gate_b = jax.nn.sigmoid(z[..., COL_GB:IN_WIDTH])

        o_a = forgetting_attention(q, k, v, f_logit, b_forget[l])
        o_b = spatial_gating(u_s, v_s, sgu_ln_g[l], sgu_ln_b[l], sgu_w_s[l], sgu_b_s[l])
        merged = gate_a * o_a + gate_b * o_b
        x = x + rmsnorm(merged @ w_out[l], mix_post_g[l])

        h = rmsnorm(x, ffn2_pre_g[l])
        x = x + 0.5 * rmsnorm(swiglu(h, ffn2_w_gate[l], ffn2_w_up[l], ffn2_w_down[l]), ffn2_post_g[l])
    return x


import jax as _jax
import jax.numpy as _jnp

TWIN_FORMAT = 'train_step'
FWD_PARAMS = ['x', 'ffn1_pre_g', 'ffn1_w_gate', 'ffn1_w_up', 'ffn1_w_down', 'ffn1_post_g', 'mix_pre_g', 'w_in', 'b_forget', 'sgu_ln_g', 'sgu_ln_b', 'sgu_w_s', 'sgu_b_s', 'w_out', 'mix_post_g', 'ffn2_pre_g', 'ffn2_w_gate', 'ffn2_w_up', 'ffn2_w_down', 'ffn2_post_g']
TWIN_WEIGHTS = ['ffn1_pre_g', 'ffn1_w_gate', 'ffn1_w_up', 'ffn1_w_down', 'ffn1_post_g', 'mix_pre_g', 'w_in', 'b_forget', 'sgu_ln_g', 'sgu_ln_b', 'sgu_w_s', 'sgu_b_s', 'w_out', 'mix_post_g', 'ffn2_pre_g', 'ffn2_w_gate', 'ffn2_w_up', 'ffn2_w_down', 'ffn2_post_g']
TWIN_DIFF_INPUT = 'x'
TWIN_INPUTS = ['x', 'ffn1_pre_g', 'ffn1_w_gate', 'ffn1_w_up', 'ffn1_w_down', 'ffn1_post_g', 'mix_pre_g', 'w_in', 'b_forget', 'sgu_ln_g', 'sgu_ln_b', 'sgu_w_s', 'sgu_b_s', 'w_out', 'mix_post_g', 'ffn2_pre_g', 'ffn2_w_gate', 'ffn2_w_up', 'ffn2_w_down', 'ffn2_post_g', 'loss_target', 'm_ffn1_pre_g', 'm_ffn1_w_gate', 'm_ffn1_w_up', 'm_ffn1_w_down', 'm_ffn1_post_g', 'm_mix_pre_g', 'm_w_in', 'm_b_forget', 'm_sgu_ln_g', 'm_sgu_ln_b', 'm_sgu_w_s', 'm_sgu_b_s', 'm_w_out', 'm_mix_post_g', 'm_ffn2_pre_g', 'm_ffn2_w_gate', 'm_ffn2_w_up', 'm_ffn2_w_down', 'm_ffn2_post_g', 'v_ffn1_pre_g', 'v_ffn1_w_gate', 'v_ffn1_w_up', 'v_ffn1_w_down', 'v_ffn1_post_g', 'v_mix_pre_g', 'v_w_in', 'v_b_forget', 'v_sgu_ln_g', 'v_sgu_ln_b', 'v_sgu_w_s', 'v_sgu_b_s', 'v_w_out', 'v_mix_post_g', 'v_ffn2_pre_g', 'v_ffn2_w_gate', 'v_ffn2_w_up', 'v_ffn2_w_down', 'v_ffn2_post_g']
TWIN_OUTPUTS = ['loss', 'grad_x', 'grad_ffn1_pre_g', 'grad_ffn1_w_gate', 'grad_ffn1_w_up', 'grad_ffn1_w_down', 'grad_ffn1_post_g', 'grad_mix_pre_g', 'grad_w_in', 'grad_b_forget', 'grad_sgu_ln_g', 'grad_sgu_ln_b', 'grad_sgu_w_s', 'grad_sgu_b_s', 'grad_w_out', 'grad_mix_post_g', 'grad_ffn2_pre_g', 'grad_ffn2_w_gate', 'grad_ffn2_w_up', 'grad_ffn2_w_down', 'grad_ffn2_post_g', 'delta_ffn1_pre_g', 'delta_ffn1_w_gate', 'delta_ffn1_w_up', 'delta_ffn1_w_down', 'delta_ffn1_post_g', 'delta_mix_pre_g', 'delta_w_in', 'delta_b_forget', 'delta_sgu_ln_g', 'delta_sgu_ln_b', 'delta_sgu_w_s', 'delta_sgu_b_s', 'delta_w_out', 'delta_mix_post_g', 'delta_ffn2_pre_g', 'delta_ffn2_w_gate', 'delta_ffn2_w_up', 'delta_ffn2_w_down', 'delta_ffn2_post_g', 'new_m_ffn1_pre_g', 'new_m_ffn1_w_gate', 'new_m_ffn1_w_up', 'new_m_ffn1_w_down', 'new_m_ffn1_post_g', 'new_m_mix_pre_g', 'new_m_w_in', 'new_m_b_forget', 'new_m_sgu_ln_g', 'new_m_sgu_ln_b', 'new_m_sgu_w_s', 'new_m_sgu_b_s', 'new_m_w_out', 'new_m_mix_post_g', 'new_m_ffn2_pre_g', 'new_m_ffn2_w_gate', 'new_m_ffn2_w_up', 'new_m_ffn2_w_down', 'new_m_ffn2_post_g', 'new_v_ffn1_pre_g', 'new_v_ffn1_w_gate', 'new_v_ffn1_w_up', 'new_v_ffn1_w_down', 'new_v_ffn1_post_g', 'new_v_mix_pre_g', 'new_v_w_in', 'new_v_b_forget', 'new_v_sgu_ln_g', 'new_v_sgu_ln_b', 'new_v_sgu_w_s', 'new_v_sgu_b_s', 'new_v_w_out', 'new_v_mix_post_g', 'new_v_ffn2_pre_g', 'new_v_ffn2_w_gate', 'new_v_ffn2_w_up', 'new_v_ffn2_w_down', 'new_v_ffn2_post_g']
TWIN_LEAF_KINDS = {'loss': 'loss', 'grad_x': 'grad_x', 'grad_ffn1_pre_g': 'grad_w', 'grad_ffn1_w_gate': 'grad_w', 'grad_ffn1_w_up': 'grad_w', 'grad_ffn1_w_down': 'grad_w', 'grad_ffn1_post_g': 'grad_w', 'grad_mix_pre_g': 'grad_w', 'grad_w_in': 'grad_w', 'grad_b_forget': 'grad_w', 'grad_sgu_ln_g': 'grad_w', 'grad_sgu_ln_b': 'grad_w', 'grad_sgu_w_s': 'grad_w', 'grad_sgu_b_s': 'grad_w', 'grad_w_out': 'grad_w', 'grad_mix_post_g': 'grad_w', 'grad_ffn2_pre_g': 'grad_w', 'grad_ffn2_w_gate': 'grad_w', 'grad_ffn2_w_up': 'grad_w', 'grad_ffn2_w_down': 'grad_w', 'grad_ffn2_post_g': 'grad_w', 'delta_ffn1_pre_g': 'delta_w', 'delta_ffn1_w_gate': 'delta_w', 'delta_ffn1_w_up': 'delta_w', 'delta_ffn1_w_down': 'delta_w', 'delta_ffn1_post_g': 'delta_w', 'delta_mix_pre_g': 'delta_w', 'delta_w_in': 'delta_w', 'delta_b_forget': 'delta_w', 'delta_sgu_ln_g': 'delta_w', 'delta_sgu_ln_b': 'delta_w', 'delta_sgu_w_s': 'delta_w', 'delta_sgu_b_s': 'delta_w', 'delta_w_out': 'delta_w', 'delta_mix_post_g': 'delta_w', 'delta_ffn2_pre_g': 'delta_w', 'delta_ffn2_w_gate': 'delta_w', 'delta_ffn2_w_up': 'delta_w', 'delta_ffn2_w_down': 'delta_w', 'delta_ffn2_post_g': 'delta_w', 'new_m_ffn1_pre_g': 'new_m', 'new_m_ffn1_w_gate': 'new_m', 'new_m_ffn1_w_up': 'new_m', 'new_m_ffn1_w_down': 'new_m', 'new_m_ffn1_post_g': 'new_m', 'new_m_mix_pre_g': 'new_m', 'new_m_w_in': 'new_m', 'new_m_b_forget': 'new_m', 'new_m_sgu_ln_g': 'new_m', 'new_m_sgu_ln_b': 'new_m', 'new_m_sgu_w_s': 'new_m', 'new_m_sgu_b_s': 'new_m', 'new_m_w_out': 'new_m', 'new_m_mix_post_g': 'new_m', 'new_m_ffn2_pre_g': 'new_m', 'new_m_ffn2_w_gate': 'new_m', 'new_m_ffn2_w_up': 'new_m', 'new_m_ffn2_w_down': 'new_m', 'new_m_ffn2_post_g': 'new_m', 'new_v_ffn1_pre_g': 'new_v', 'new_v_ffn1_w_gate': 'new_v', 'new_v_ffn1_w_up': 'new_v', 'new_v_ffn1_w_down': 'new_v', 'new_v_ffn1_post_g': 'new_v', 'new_v_mix_pre_g': 'new_v', 'new_v_w_in': 'new_v', 'new_v_b_forget': 'new_v', 'new_v_sgu_ln_g': 'new_v', 'new_v_sgu_ln_b': 'new_v', 'new_v_sgu_w_s': 'new_v', 'new_v_sgu_b_s': 'new_v', 'new_v_w_out': 'new_v', 'new_v_mix_post_g': 'new_v', 'new_v_ffn2_pre_g': 'new_v', 'new_v_ffn2_w_gate': 'new_v', 'new_v_ffn2_w_up': 'new_v', 'new_v_ffn2_w_down': 'new_v', 'new_v_ffn2_post_g': 'new_v'}


def _forward(args):
    return _fwd_reference(*[args[k] for k in FWD_PARAMS])


def _output_shape():
    def fwd():
        inp = _fwd_setup_inputs(0)
        return _fwd_reference(*[inp[k] for k in FWD_PARAMS])
    out = _jax.eval_shape(fwd)
    return out.shape, out.dtype

N_MICROBATCH = 1
ADAM_LR = 0.001
ADAM_B1 = 0.9
ADAM_B2 = 0.999
ADAM_EPS = 1e-08
ADAM_WD = 0.01
ADAM_STEP = 10
PER_EXAMPLE_BATCH_AXIS = {'x': 0, 'loss_target': 0}
SHARED_INPUTS = []
_WEIGHT_DTYPES = {'ffn1_pre_g': _jnp.float32, 'ffn1_w_gate': _jnp.float32, 'ffn1_w_up': _jnp.float32, 'ffn1_w_down': _jnp.float32, 'ffn1_post_g': _jnp.float32, 'mix_pre_g': _jnp.float32, 'w_in': _jnp.float32, 'b_forget': _jnp.float32, 'sgu_ln_g': _jnp.float32, 'sgu_ln_b': _jnp.float32, 'sgu_w_s': _jnp.float32, 'sgu_b_s': _jnp.float32, 'w_out': _jnp.float32, 'mix_post_g': _jnp.float32, 'ffn2_pre_g': _jnp.float32, 'ffn2_w_gate': _jnp.float32, 'ffn2_w_up': _jnp.float32, 'ffn2_w_down': _jnp.float32, 'ffn2_post_g': _jnp.float32}
MOMENT_SCALE = {'ffn1_pre_g': 6.314054e-01, 'ffn1_w_gate': 2.055821e-01, 'ffn1_w_up': 2.316024e-01, 'ffn1_w_down': 4.751644e-01, 'ffn1_post_g': 1.587600e+01, 'mix_pre_g': 9.129140e-01, 'w_in': 3.598785e-01, 'b_forget': 7.668932e-01, 'sgu_ln_g': 2.886826e-01, 'sgu_ln_b': 2.756536e-01, 'sgu_w_s': 2.794501e-01, 'sgu_b_s': 3.136055e-01, 'w_out': 1.387536e+00, 'mix_post_g': 6.457044e+01, 'ffn2_pre_g': 9.974419e-01, 'ffn2_w_gate': 2.168574e-01, 'ffn2_w_up': 4.067648e-01, 'ffn2_w_down': 8.218684e-01, 'ffn2_post_g': 1.599039e+01}


def _to_microbatches(a, axis):
    t = _jnp.moveaxis(a, axis, 0)
    t = t.reshape((N_MICROBATCH, t.shape[0] // N_MICROBATCH) + t.shape[1:])
    return _jnp.moveaxis(t, 1, axis + 1)


def setup_inputs(seed: int = 0) -> dict:
    inp = _fwd_setup_inputs(seed)
    key = _jax.random.fold_in(_jax.random.key(seed), 7919)
    shape, _ = _output_shape()
    out = dict(inp)
    out["loss_target"] = _jax.random.normal(_jax.random.fold_in(key, 0), shape, _jnp.float32)
    for i, name in enumerate(TWIN_WEIGHTS):
        w = inp[name].astype(_jnp.float32)
        if MOMENT_SCALE is None:
            s = _jnp.sqrt(_jnp.mean(_jnp.square(w)) + 1e-30)
        else:
            s = MOMENT_SCALE[name]
        km, kv = _jax.random.split(_jax.random.fold_in(key, i + 1))
        out[name] = w
        out["m_" + name] = s * _jax.random.normal(km, w.shape, _jnp.float32)
        out["v_" + name] = (s * s) * _jax.random.uniform(kv, w.shape, _jnp.float32, 0.5, 1.5)
    if N_MICROBATCH > 1:
        for name, axis in PER_EXAMPLE_BATCH_AXIS.items():
            out[name] = _to_microbatches(out[name], axis)
    return {'x': out['x'], 'ffn1_pre_g': out['ffn1_pre_g'], 'ffn1_w_gate': out['ffn1_w_gate'], 'ffn1_w_up': out['ffn1_w_up'], 'ffn1_w_down': out['ffn1_w_down'], 'ffn1_post_g': out['ffn1_post_g'], 'mix_pre_g': out['mix_pre_g'], 'w_in': out['w_in'], 'b_forget': out['b_forget'], 'sgu_ln_g': out['sgu_ln_g'], 'sgu_ln_b': out['sgu_ln_b'], 'sgu_w_s': out['sgu_w_s'], 'sgu_b_s': out['sgu_b_s'], 'w_out': out['w_out'], 'mix_post_g': out['mix_post_g'], 'ffn2_pre_g': out['ffn2_pre_g'], 'ffn2_w_gate': out['ffn2_w_gate'], 'ffn2_w_up': out['ffn2_w_up'], 'ffn2_w_down': out['ffn2_w_down'], 'ffn2_post_g': out['ffn2_post_g'], 'loss_target': out['loss_target'], 'm_ffn1_pre_g': out['m_ffn1_pre_g'], 'm_ffn1_w_gate': out['m_ffn1_w_gate'], 'm_ffn1_w_up': out['m_ffn1_w_up'], 'm_ffn1_w_down': out['m_ffn1_w_down'], 'm_ffn1_post_g': out['m_ffn1_post_g'], 'm_mix_pre_g': out['m_mix_pre_g'], 'm_w_in': out['m_w_in'], 'm_b_forget': out['m_b_forget'], 'm_sgu_ln_g': out['m_sgu_ln_g'], 'm_sgu_ln_b': out['m_sgu_ln_b'], 'm_sgu_w_s': out['m_sgu_w_s'], 'm_sgu_b_s': out['m_sgu_b_s'], 'm_w_out': out['m_w_out'], 'm_mix_post_g': out['m_mix_post_g'], 'm_ffn2_pre_g': out['m_ffn2_pre_g'], 'm_ffn2_w_gate': out['m_ffn2_w_gate'], 'm_ffn2_w_up': out['m_ffn2_w_up'], 'm_ffn2_w_down': out['m_ffn2_w_down'], 'm_ffn2_post_g': out['m_ffn2_post_g'], 'v_ffn1_pre_g': out['v_ffn1_pre_g'], 'v_ffn1_w_gate': out['v_ffn1_w_gate'], 'v_ffn1_w_up': out['v_ffn1_w_up'], 'v_ffn1_w_down': out['v_ffn1_w_down'], 'v_ffn1_post_g': out['v_ffn1_post_g'], 'v_mix_pre_g': out['v_mix_pre_g'], 'v_w_in': out['v_w_in'], 'v_b_forget': out['v_b_forget'], 'v_sgu_ln_g': out['v_sgu_ln_g'], 'v_sgu_ln_b': out['v_sgu_ln_b'], 'v_sgu_w_s': out['v_sgu_w_s'], 'v_sgu_b_s': out['v_sgu_b_s'], 'v_w_out': out['v_w_out'], 'v_mix_post_g': out['v_mix_post_g'], 'v_ffn2_pre_g': out['v_ffn2_pre_g'], 'v_ffn2_w_gate': out['v_ffn2_w_gate'], 'v_ffn2_w_up': out['v_ffn2_w_up'], 'v_ffn2_w_down': out['v_ffn2_w_down'], 'v_ffn2_post_g': out['v_ffn2_post_g']}


def _loss(weights, diff, rest, loss_target):
    with _jax.named_scope("forward"):
        args = {**rest, TWIN_DIFF_INPUT: diff, **{k: w.astype(_WEIGHT_DTYPES[k]) for k, w in weights.items()}}
        y = _forward(args)
    with _jax.named_scope("loss_head"):
        err = _jnp.square(y.astype(_jnp.float32) - loss_target)
        return 0.5 * _jnp.sum(_jnp.mean(err, axis=-1)) if err.ndim else 0.5 * err


def _adamw(w, g, m, v):
    m = ADAM_B1 * m + (1.0 - ADAM_B1) * g
    v = ADAM_B2 * v + (1.0 - ADAM_B2) * _jnp.square(g)
    m_hat = m / (1.0 - ADAM_B1 ** ADAM_STEP)
    v_hat = v / (1.0 - ADAM_B2 ** ADAM_STEP)
    delta = -ADAM_LR * (m_hat / (_jnp.sqrt(v_hat) + ADAM_EPS) + ADAM_WD * w)
    return delta, m, v


def reference(x, ffn1_pre_g, ffn1_w_gate, ffn1_w_up, ffn1_w_down, ffn1_post_g, mix_pre_g, w_in, b_forget, sgu_ln_g, sgu_ln_b, sgu_w_s, sgu_b_s, w_out, mix_post_g, ffn2_pre_g, ffn2_w_gate, ffn2_w_up, ffn2_w_down, ffn2_post_g, loss_target, m_ffn1_pre_g, m_ffn1_w_gate, m_ffn1_w_up, m_ffn1_w_down, m_ffn1_post_g, m_mix_pre_g, m_w_in, m_b_forget, m_sgu_ln_g, m_sgu_ln_b, m_sgu_w_s, m_sgu_b_s, m_w_out, m_mix_post_g, m_ffn2_pre_g, m_ffn2_w_gate, m_ffn2_w_up, m_ffn2_w_down, m_ffn2_post_g, v_ffn1_pre_g, v_ffn1_w_gate, v_ffn1_w_up, v_ffn1_w_down, v_ffn1_post_g, v_mix_pre_g, v_w_in, v_b_forget, v_sgu_ln_g, v_sgu_ln_b, v_sgu_w_s, v_sgu_b_s, v_w_out, v_mix_post_g, v_ffn2_pre_g, v_ffn2_w_gate, v_ffn2_w_up, v_ffn2_w_down, v_ffn2_post_g):
    given = dict(x=x, ffn1_pre_g=ffn1_pre_g, ffn1_w_gate=ffn1_w_gate, ffn1_w_up=ffn1_w_up, ffn1_w_down=ffn1_w_down, ffn1_post_g=ffn1_post_g, mix_pre_g=mix_pre_g, w_in=w_in, b_forget=b_forget, sgu_ln_g=sgu_ln_g, sgu_ln_b=sgu_ln_b, sgu_w_s=sgu_w_s, sgu_b_s=sgu_b_s, w_out=w_out, mix_post_g=mix_post_g, ffn2_pre_g=ffn2_pre_g, ffn2_w_gate=ffn2_w_gate, ffn2_w_up=ffn2_w_up, ffn2_w_down=ffn2_w_down, ffn2_post_g=ffn2_post_g, loss_target=loss_target, m_ffn1_pre_g=m_ffn1_pre_g, m_ffn1_w_gate=m_ffn1_w_gate, m_ffn1_w_up=m_ffn1_w_up, m_ffn1_w_down=m_ffn1_w_down, m_ffn1_post_g=m_ffn1_post_g, m_mix_pre_g=m_mix_pre_g, m_w_in=m_w_in, m_b_forget=m_b_forget, m_sgu_ln_g=m_sgu_ln_g, m_sgu_ln_b=m_sgu_ln_b, m_sgu_w_s=m_sgu_w_s, m_sgu_b_s=m_sgu_b_s, m_w_out=m_w_out, m_mix_post_g=m_mix_post_g, m_ffn2_pre_g=m_ffn2_pre_g, m_ffn2_w_gate=m_ffn2_w_gate, m_ffn2_w_up=m_ffn2_w_up, m_ffn2_w_down=m_ffn2_w_down, m_ffn2_post_g=m_ffn2_post_g, v_ffn1_pre_g=v_ffn1_pre_g, v_ffn1_w_gate=v_ffn1_w_gate, v_ffn1_w_up=v_ffn1_w_up, v_ffn1_w_down=v_ffn1_w_down, v_ffn1_post_g=v_ffn1_post_g, v_mix_pre_g=v_mix_pre_g, v_w_in=v_w_in, v_b_forget=v_b_forget, v_sgu_ln_g=v_sgu_ln_g, v_sgu_ln_b=v_sgu_ln_b, v_sgu_w_s=v_sgu_w_s, v_sgu_b_s=v_sgu_b_s, v_w_out=v_w_out, v_mix_post_g=v_mix_post_g, v_ffn2_pre_g=v_ffn2_pre_g, v_ffn2_w_gate=v_ffn2_w_gate, v_ffn2_w_up=v_ffn2_w_up, v_ffn2_w_down=v_ffn2_w_down, v_ffn2_post_g=v_ffn2_post_g)
    weights = {n: given[n] for n in TWIN_WEIGHTS}
    shared = {n: given[n] for n in SHARED_INPUTS}
    per_example = {n: given[n] for n in ['x']}
    grad_fn = _jax.value_and_grad(_loss, argnums=(0, 1))

    def one_microbatch(ex, loss_target):
        ex = dict(ex)
        diff = ex.pop(TWIN_DIFF_INPUT)
        return grad_fn(weights, diff, {**shared, **ex}, loss_target)

    if N_MICROBATCH == 1:
        loss, (grad_w, grad_x) = one_microbatch(per_example, given["loss_target"])
    else:
        def body(carry, xs):
            loss_sum, grad_sum = carry
            l_k, (gw_k, gx_k) = one_microbatch(xs[0], xs[1])
            with _jax.named_scope("update"):
                return (loss_sum + l_k, _jax.tree.map(_jnp.add, grad_sum, gw_k)), gx_k

        init = (_jnp.zeros((), _jnp.float32), _jax.tree.map(_jnp.zeros_like, weights))
        (loss, grad_w), grad_x = _jax.lax.scan(body, init, (per_example, given["loss_target"]))
    with _jax.named_scope("update"):
        delta_w, new_m, new_v = {}, {}, {}
        for n in TWIN_WEIGHTS:
            delta_w[n], new_m[n], new_v[n] = _adamw(weights[n], grad_w[n], given["m_" + n], given["v_" + n])
    return (loss, grad_x, *[grad_w[n] for n in TWIN_WEIGHTS], *[delta_w[n] for n in TWIN_WEIGHTS],
            *[new_m[n] for n in TWIN_WEIGHTS], *[new_v[n] for n in TWIN_WEIGHTS])
```

```python
import functools
import math

import jax
import jax.numpy as jnp
from jax import lax
from jax.experimental import pallas as pl
from jax.experimental.pallas import tpu as pltpu

D = 1024
F = 4096
H = 8
HD = 128
G = 8
L = 128
CHUNK = 64
NSH = 4
NDEV = 8
ZW = 7 * D
RMS_EPS = 1e-6
LN_EPS = 1e-5
NEG = -1e30
SCALE = 1.0 / math.sqrt(HD)

ADAM_LR = 0.001
ADAM_B1 = 0.9
ADAM_B2 = 0.999
ADAM_EPS = 1e-08
ADAM_WD = 0.01
ADAM_STEP = 10

VMEM_LIMIT_BYTES = 56 * 1024 * 1024

ROW_FFN1 = 0
ROW_FFN2 = 3 * D
ROW_WOUT = 6 * D
ROW_WIN = 6 * D + 256
WIN_SH = 1794
PACK_ROWS = 8320
HALF_ROWS = PACK_ROWS // 2
SMALL_STRIDE = 8
SMALL_ROWS = 10 * SMALL_STRIDE + L

BF = jnp.bfloat16
F32 = jnp.float32
MESH = pl.DeviceIdType.MESH


def _params(n_grid):
    return pltpu.CompilerParams(dimension_semantics=("arbitrary",) * n_grid,
                                vmem_limit_bytes=VMEM_LIMIT_BYTES)


def _dot(a, b):
    return jnp.dot(a, b, preferred_element_type=F32)


def _dot_nt(a, b):
    return lax.dot_general(a, b, (((1,), (1,)), ((), ())), preferred_element_type=F32)


def _dot_tn(a, b):
    return lax.dot_general(a, b, (((0,), (0,)), ((), ())), preferred_element_type=F32)


def _rms(x, g):
    r = lax.rsqrt(jnp.mean(x * x, axis=-1, keepdims=True) + RMS_EPS)
    return x * r * g


def _rms_bwd(dn, x, g):
    r = lax.rsqrt(jnp.mean(x * x, axis=-1, keepdims=True) + RMS_EPS)
    xr = x * r
    dg = jnp.sum(dn * xr, axis=0, keepdims=True)
    t = dn * g
    dx = r * (t - xr * jnp.mean(t * xr, axis=-1, keepdims=True))
    return dx, dg


def _gelu_parts(x):
    cdf = 0.5 * (1.0 + lax.erf(x * (1.0 / math.sqrt(2.0))))
    pdf = jnp.exp(-0.5 * x * x) * (1.0 / math.sqrt(2.0 * math.pi))
    return x * cdf, cdf + x * pdf


def _gelu(x):
    return x * (0.5 * (1.0 + lax.erf(x * (1.0 / math.sqrt(2.0)))))


def _sigmoid(x):
    return 1.0 / (1.0 + jnp.exp(-x))


def _ffn_fwd(x, g_pre, wpack, row0, g_post, tm):
    S = x.shape[0]
    nt, nf, tf = S // tm, NSH, D
    rb = row0 // D

    def body(x_ref, gpre_ref, wg_ref, wu_ref, wd_ref, gpost_ref,
             h_ref, a_ref, b_ref, y_ref, xo_ref, h_s, acc):
        j = pl.program_id(1)

        @pl.when(j == 0)
        def _():
            h = _rms(x_ref[...], gpre_ref[...]).astype(BF)
            h_s[...] = h
            h_ref[...] = h
            acc[...] = jnp.zeros_like(acc)

        h = h_s[...]
        a = _dot(h, wg_ref[...])
        b = _dot(h, wu_ref[...])
        a_ref[...] = a.astype(BF)
        b_ref[...] = b.astype(BF)
        act = (a * _sigmoid(a)) * b
        acc[...] += _dot(act.astype(BF), wd_ref[...])

        @pl.when(j == nf - 1)
        def _():
            y = acc[...]
            y_ref[...] = y
            xo_ref[...] = x_ref[...] + 0.5 * _rms(y, gpost_ref[...])

    row = pl.BlockSpec((tm, D), lambda i, j: (i, 0))
    vec = pl.BlockSpec((1, D), lambda i, j: (0, 0))
    return pl.pallas_call(
        body, name="ffn_fwd",
        grid=(nt, nf),
        in_specs=[row, vec,
                  pl.BlockSpec((None, D, tf), lambda i, j: (j, rb, 0)),
                  pl.BlockSpec((None, D, tf), lambda i, j: (j, rb + 1, 0)),
                  pl.BlockSpec((None, tf, D), lambda i, j: (j, rb + 2, 0)),
                  vec],
        out_specs=[row,
                   pl.BlockSpec((tm, tf), lambda i, j: (i, j)),
                   pl.BlockSpec((tm, tf), lambda i, j: (i, j)),
                   row, row],
        out_shape=[jax.ShapeDtypeStruct((S, D), BF),
                   jax.ShapeDtypeStruct((S, F), BF),
                   jax.ShapeDtypeStruct((S, F), BF),
                   jax.ShapeDtypeStruct((S, D), F32),
                   jax.ShapeDtypeStruct((S, D), F32)],
        scratch_shapes=[pltpu.VMEM((tm, D), BF), pltpu.VMEM((tm, D), F32)],
        compiler_params=_params(2),
    )(x, g_pre, wpack, wpack, wpack, g_post)


def _ffn_bwd(dxo, y, g_post, a, b, wpack, row0, x_in, g_pre, tm):
    S = dxo.shape[0]
    nt, nf, tf = S // tm, NSH, D
    rb = row0 // D

    def body(dxo_ref, y_ref, gpost_ref, a_ref, b_ref, wg_ref, wu_ref, wd_ref, xin_ref, gpre_ref,
             dy_ref, da_ref, db_ref, act_ref, dxin_ref, dgpost_ref, dgpre_ref, dy_s, acc):
        i = pl.program_id(0)
        j = pl.program_id(1)

        @pl.when((i == 0) & (j == 0))
        def _():
            dgpost_ref[...] = jnp.zeros_like(dgpost_ref)
            dgpre_ref[...] = jnp.zeros_like(dgpre_ref)

        @pl.when(j == 0)
        def _():
            dy, dg = _rms_bwd(0.5 * dxo_ref[...], y_ref[...], gpost_ref[...])
            dyb = dy.astype(BF)
            dy_s[...] = dyb
            dy_ref[...] = dyb
            dgpost_ref[...] += dg
            acc[...] = jnp.zeros_like(acc)

        dact = _dot_nt(dy_s[...], wd_ref[...])
        av = a_ref[...].astype(F32)
        bv = b_ref[...].astype(F32)
        sig = _sigmoid(av)
        sl = av * sig
        act_ref[...] = (sl * bv).astype(BF)
        dbb = (dact * sl).astype(BF)
        dab = (dact * bv * (sig * (1.0 + av * (1.0 - sig)))).astype(BF)
        da_ref[...] = dab
        db_ref[...] = dbb
        acc[...] += _dot_nt(dab, wg_ref[...]) + _dot_nt(dbb, wu_ref[...])

        @pl.when(j == nf - 1)
        def _():
            dx, dg = _rms_bwd(acc[...], xin_ref[...], gpre_ref[...])
            dxin_ref[...] = dxo_ref[...] + dx
            dgpre_ref[...] += dg

    row = pl.BlockSpec((tm, D), lambda i, j: (i, 0))
    vec = pl.BlockSpec((1, D), lambda i, j: (0, 0))
    ff = pl.BlockSpec((tm, tf), lambda i, j: (i, j))
    return pl.pallas_call(
        body, name="ffn_bwd",
        grid=(nt, nf),
        in_specs=[row, row, vec, ff, ff,
                  pl.BlockSpec((None, D, tf), lambda i, j: (j, rb, 0)),
                  pl.BlockSpec((None, D, tf), lambda i, j: (j, rb + 1, 0)),
                  pl.BlockSpec((None, tf, D), lambda i, j: (j, rb + 2, 0)),
                  row, vec],
        out_specs=[row, ff, ff, ff, row, vec, vec],
        out_shape=[jax.ShapeDtypeStruct((S, D), BF),
                   jax.ShapeDtypeStruct((S, F), BF),
                   jax.ShapeDtypeStruct((S, F), BF),
                   jax.ShapeDtypeStruct((S, F), BF),
                   jax.ShapeDtypeStruct((S, D), F32),
                   jax.ShapeDtypeStruct((1, D), F32),
                   jax.ShapeDtypeStruct((1, D), F32)],
        scratch_shapes=[pltpu.VMEM((tm, D), BF), pltpu.VMEM((tm, D), F32)],
        compiler_params=_params(2),
    )(dxo, y, g_post, a, b, wpack, wpack, wpack, x_in, g_pre)


def _mm_tn(a, b, bm, bn, bt, col_blocked=False):
    S, M = a.shape
    N = b.shape[1]
    nt = S // bt

    def body(a_ref, b_ref, o_ref):
        t = pl.program_id(2)

        @pl.when(t == 0)
        def _():
            o_ref[...] = jnp.zeros_like(o_ref)

        o_ref[...] += _dot_tn(a_ref[...], b_ref[...])

    if col_blocked:
        out_spec = pl.BlockSpec((None, bm, bn), lambda m, n, t: (n, m, 0))
        out_shape = jax.ShapeDtypeStruct((N // bn, M, bn), F32)
    else:
        out_spec = pl.BlockSpec((bm, bn), lambda m, n, t: (m, n))
        out_shape = jax.ShapeDtypeStruct((M, N), F32)
    return pl.pallas_call(
        body, name="mm_tn",
        grid=(M // bm, N // bn, nt),
        in_specs=[pl.BlockSpec((bt, bm), lambda m, n, t: (t, m)),
                  pl.BlockSpec((bt, bn), lambda m, n, t: (t, n))],
        out_specs=out_spec, out_shape=out_shape,
        compiler_params=_params(3),
    )(a, b)


def _mm(a, w, tm, tn, out_dtype):
    S, K = a.shape
    N = w.shape[1]

    def body(a_ref, w_ref, o_ref):
        o_ref[...] = _dot(a_ref[...], w_ref[...]).astype(out_dtype)

    return pl.pallas_call(
        body, name="mm",
        grid=(S // tm, N // tn),
        in_specs=[pl.BlockSpec((tm, K), lambda i, j: (i, 0)),
                  pl.BlockSpec((K, tn), lambda i, j: (0, j))],
        out_specs=pl.BlockSpec((tm, tn), lambda i, j: (i, j)),
        out_shape=jax.ShapeDtypeStruct((S, N), out_dtype),
        compiler_params=_params(2),
    )(a, w)


def _norm_fwd(x, g, tm):
    S = x.shape[0]

    def body(x_ref, g_ref, h_ref):
        h_ref[...] = _rms(x_ref[...], g_ref[...]).astype(BF)

    return pl.pallas_call(
        body, name="norm_fwd",
        grid=(S // tm,),
        in_specs=[pl.BlockSpec((tm, D), lambda i: (i, 0)), pl.BlockSpec((1, D), lambda i: (0, 0))],
        out_specs=pl.BlockSpec((tm, D), lambda i: (i, 0)),
        out_shape=jax.ShapeDtypeStruct((S, D), BF),
        compiler_params=_params(1),
    )(x, g)


def _mix_in_bwd(dz, wcat, dzf, wf, x1, g, dx2, tm):
    S = dz.shape[0]
    nk = ZW // D

    def body(dz_ref, w_ref, dzf_ref, wf_ref, x_ref, g_ref, dx2_ref, dx1_ref, dg_ref, acc):
        i = pl.program_id(0)
        k = pl.program_id(1)

        @pl.when((i == 0) & (k == 0))
        def _():
            dg_ref[...] = jnp.zeros_like(dg_ref)

        @pl.when(k == 0)
        def _():
            acc[...] = _dot_nt(dzf_ref[...], wf_ref[...])

        acc[...] += _dot_nt(dz_ref[...], w_ref[...])

        @pl.when(k == nk - 1)
        def _():
            dx, dg = _rms_bwd(acc[...], x_ref[...], g_ref[...])
            dx1_ref[...] = dx2_ref[...] + dx
            dg_ref[...] += dg

    row = pl.BlockSpec((tm, D), lambda i, k: (i, 0))
    vec = pl.BlockSpec((1, D), lambda i, k: (0, 0))
    return pl.pallas_call(
        body, name="mix_in_bwd",
        grid=(S // tm, nk),
        in_specs=[pl.BlockSpec((tm, D), lambda i, k: (i, k)),
                  pl.BlockSpec((D, D), lambda i, k: (0, k)),
                  pl.BlockSpec((tm, HD), lambda i, k: (i, 0)),
                  pl.BlockSpec((D, HD), lambda i, k: (0, 0)),
                  row, vec, row],
        out_specs=[row, vec],
        out_shape=[jax.ShapeDtypeStruct((S, D), F32), jax.ShapeDtypeStruct((1, D), F32)],
        scratch_shapes=[pltpu.VMEM((tm, D), F32)],
        compiler_params=_params(2),
    )(dz, wcat, dzf, wf, x1, g, dx2)


def _scan_rows(blk, reverse):
    n = blk.shape[0]
    row = lax.broadcasted_iota(jnp.int32, blk.shape, 0)
    d = 1
    while d < n:
        if reverse:
            blk = blk + jnp.where(row < n - d, pltpu.roll(blk, n - d, 0), 0.0)
        else:
            blk = blk + jnp.where(row >= d, pltpu.roll(blk, d, 0), 0.0)
        d *= 2
    return blk


def _forget_cumsum(zf, bf, cs):
    S = zf.shape[0]

    def body(zf_ref, bf_ref, c_ref, cb_ref, carry):
        @pl.when(pl.program_id(0) == 0)
        def _():
            carry[...] = jnp.zeros_like(carry)

        x = zf_ref[...] + bf_ref[...]
        logf = jnp.minimum(x, 0.0) - jnp.log1p(jnp.exp(-jnp.abs(x)))
        sc = _scan_rows(logf, False) + carry[...]
        c_ref[...] = sc
        carry[...] = sc[cs - 1:cs, :]
        for h in range(H):
            cb_ref[h] = jnp.broadcast_to(sc[:, h:h + 1], (cs, HD))

    return pl.pallas_call(
        body, name="forget_cumsum",
        grid=(S // cs,),
        in_specs=[pl.BlockSpec((cs, HD), lambda i: (i, 0)), pl.BlockSpec((1, HD), lambda i: (0, 0))],
        out_specs=[pl.BlockSpec((cs, HD), lambda i: (i, 0)),
                   pl.BlockSpec((H, cs, HD), lambda i: (0, i, 0))],
        out_shape=[jax.ShapeDtypeStruct((S, HD), F32), jax.ShapeDtypeStruct((H, S, HD), F32)],
        scratch_shapes=[pltpu.VMEM((1, HD), F32)],
        compiler_params=_params(1),
    )(zf, bf)


def _forget_bwd(dc, zf, bf, cs):
    S = dc.shape[0]
    nc = S // cs

    def body(dc_ref, zf_ref, bf_ref, dzf_ref, dbf_ref, carry):
        @pl.when(pl.program_id(0) == 0)
        def _():
            carry[...] = jnp.zeros_like(carry)
            dbf_ref[...] = jnp.zeros_like(dbf_ref)

        sc = _scan_rows(dc_ref[...], True) + carry[...]
        carry[...] = sc[0:1, :]
        x = zf_ref[...] + bf_ref[...]
        dz = sc * _sigmoid(-x)
        dzf_ref[...] = dz.astype(BF)
        dbf_ref[...] += jnp.sum(dz, axis=0, keepdims=True)

    rev = pl.BlockSpec((cs, HD), lambda i: (nc - 1 - i, 0))
    vec = pl.BlockSpec((1, HD), lambda i: (0, 0))
    return pl.pallas_call(
        body, name="forget_bwd",
        grid=(nc,),
        in_specs=[rev, rev, vec],
        out_specs=[rev, vec],
        out_shape=[jax.ShapeDtypeStruct((S, HD), BF), jax.ShapeDtypeStruct((1, HD), F32)],
        scratch_shapes=[pltpu.VMEM((1, HD), F32)],
        compiler_params=_params(1),
    )(dc, zf, bf)


def _causal_mask(i, j, t, rows_are_queries):
    r = lax.broadcasted_iota(jnp.int32, (t, t), 0)
    c = lax.broadcasted_iota(jnp.int32, (t, t), 1)
    if rows_are_queries:
        return (j * t + c) <= (i * t + r)
    return (j * t + r) <= (i * t + c)


def _fox_fwd(z, ccol_b, crow, t):
    S = z.shape[0]
    nq = S // t

    def body(q_ref, kv_ref, cc_ref, cr_ref, o_ref, lse_ref, m_s, l_s, acc_s):
        i = pl.program_id(1)
        q = q_ref[...]
        ct = cc_ref[:, :1]
        m_s[...] = jnp.full_like(m_s, NEG)
        l_s[...] = jnp.zeros_like(l_s)
        acc_s[...] = jnp.zeros_like(acc_s)

        def step(j, masked):
            off = pl.multiple_of(j * t, t)
            k = kv_ref[pl.ds(off, t), :HD]
            v = kv_ref[pl.ds(off, t), HD:]
            s = _dot_nt(q, k) * SCALE + (ct - cr_ref[pl.ds(j, 1), :])
            if masked:
                s = jnp.where(_causal_mask(i, j, t, True), s, NEG)
            m_old = m_s[...]
            m_new = jnp.maximum(m_old, jnp.max(s, axis=1, keepdims=True))
            p = jnp.exp(s - m_new)
            alpha = jnp.exp(m_old - m_new)
            l_s[...] = alpha * l_s[...] + jnp.sum(p, axis=1, keepdims=True)
            acc_s[...] = alpha * acc_s[...] + _dot(p.astype(BF), v)
            m_s[...] = m_new

        def full_step(j, carry):
            step(j, False)
            return carry

        lax.fori_loop(0, i, full_step, 0)
        step(i, True)
        l = l_s[...]
        o_ref[...] = (acc_s[...] / l).astype(BF)
        lse_ref[...] = jnp.broadcast_to(m_s[...] + jnp.log(l), (t, HD))

    return pl.pallas_call(
        body, name="fox_fwd",
        grid=(H, nq),
        in_specs=[pl.BlockSpec((t, HD), lambda h, i: (i, h)),
                  pl.BlockSpec((S, 2 * HD), lambda h, i: (0, 4 + h)),
                  pl.BlockSpec((None, t, HD), lambda h, i: (h, i, 0)),
                  pl.BlockSpec((None, nq, t), lambda h, i: (h, 0, 0))],
        out_specs=[pl.BlockSpec((t, HD), lambda h, i: (i, h)),
                   pl.BlockSpec((None, t, HD), lambda h, i: (h, i, 0))],
        out_shape=[jax.ShapeDtypeStruct((S, D), BF), jax.ShapeDtypeStruct((H, S, HD), F32)],
        scratch_shapes=[pltpu.VMEM((t, 1), F32), pltpu.VMEM((t, 1), F32), pltpu.VMEM((t, HD), F32)],
        compiler_params=_params(2),
    )(z, z, ccol_b, crow)


def _fox_bwd_dq(z, do, o, lse_b, ccol_b, crow, dz, t):
    S = z.shape[0]
    nq = S // t

    def body(q_ref, kv_ref, do_ref, o_ref, lse_ref, cc_ref, cr_ref, dz_in, dq_ref, delta_ref, dr_ref, acc_s, dr_s):
        del dz_in
        dr_s[...] = jnp.zeros_like(dr_s)
        i = pl.program_id(1)
        q = q_ref[...]
        dout = do_ref[...]
        ct = cc_ref[:, :1]
        lse = lse_ref[:, :1]
        delta = jnp.sum(dout.astype(F32) * o_ref[...].astype(F32), axis=1, keepdims=True)
        delta_ref[...] = jnp.broadcast_to(delta, (t, HD))
        acc_s[...] = jnp.zeros_like(acc_s)

        def step(j, masked):
            off = pl.multiple_of(j * t, t)
            k = kv_ref[pl.ds(off, t), :HD]
            v = kv_ref[pl.ds(off, t), HD:]
            s = _dot_nt(q, k) * SCALE + (ct - cr_ref[pl.ds(j, 1), :])
            p = jnp.exp(s - lse)
            if masked:
                p = jnp.where(_causal_mask(i, j, t, True), p, 0.0)
            ds = p * (_dot_nt(dout, v) - delta)
            acc_s[...] += _dot(ds.astype(BF), k)
            dr_s[...] += jnp.sum(ds, axis=1, keepdims=True)

        def full_step(j, carry):
            step(j, False)
            return carry

        lax.fori_loop(0, i, full_step, 0)
        step(i, True)
        dq_ref[...] = (acc_s[...] * SCALE).astype(BF)
        dr_ref[...] = jnp.broadcast_to(dr_s[...], (t, HD))

    qspec = pl.BlockSpec((t, HD), lambda h, i: (i, h))
    bspec = pl.BlockSpec((None, t, HD), lambda h, i: (h, i, 0))
    return pl.pallas_call(
        body, name="fox_bwd_dq",
        grid=(H, nq),
        in_specs=[qspec,
                  pl.BlockSpec((S, 2 * HD), lambda h, i: (0, 4 + h)),
                  qspec, qspec, bspec, bspec,
                  pl.BlockSpec((None, nq, t), lambda h, i: (h, 0, 0)),
                  pl.BlockSpec(memory_space=pl.ANY)],
        out_specs=[qspec, bspec, bspec],
        out_shape=[jax.ShapeDtypeStruct((S, ZW), BF), jax.ShapeDtypeStruct((H, S, HD), F32),
                   jax.ShapeDtypeStruct((H, S, HD), F32)],
        scratch_shapes=[pltpu.VMEM((t, HD), F32), pltpu.VMEM((t, 1), F32)],
        input_output_aliases={7: 0},
        compiler_params=_params(2),
    )(z, z, do, o, lse_b, ccol_b, crow, dz)


def _fox_bwd_dkv(z, do, lse_r, delta_r, ccol_b, crow, dz, t):
    S = z.shape[0]
    nq = S // t

    def body(q_ref, kv_ref, do_ref, lse_ref, delta_ref, cc_ref, cr_ref, dz_in, dkv_ref, dc_ref,
             dk_s, dv_s, dc_s):
        del dz_in
        j = pl.program_id(1)
        k = kv_ref[:, :HD]
        v = kv_ref[:, HD:]
        cs = cc_ref[:, :1]
        dk_s[...] = jnp.zeros_like(dk_s)
        dv_s[...] = jnp.zeros_like(dv_s)
        dc_s[...] = jnp.zeros_like(dc_s)

        def step(i, masked):
            off = pl.multiple_of(i * t, t)
            q = q_ref[pl.ds(off, t), :]
            dout = do_ref[pl.ds(off, t), :]
            st = _dot_nt(k, q) * SCALE + (cr_ref[pl.ds(i, 1), :] - cs)
            pt = jnp.exp(st - lse_ref[pl.ds(i, 1), :])
            if masked:
                pt = jnp.where(_causal_mask(i, j, t, False), pt, 0.0)
            dv_s[...] += _dot(pt.astype(BF), dout)
            dst = pt * (_dot_nt(v, dout) - delta_ref[pl.ds(i, 1), :])
            dk_s[...] += _dot(dst.astype(BF), q)
            dc_s[...] += jnp.sum(dst, axis=1, keepdims=True)

        step(j, True)

        def full_step(i, carry):
            step(i, False)
            return carry

        lax.fori_loop(j + 1, nq, full_step, 0)
        dkv_ref[:, :HD] = (dk_s[...] * SCALE).astype(BF)
        dkv_ref[:, HD:] = dv_s[...].astype(BF)
        dc_ref[...] = jnp.broadcast_to(-dc_s[...], (t, HD))

    whole = pl.BlockSpec((S, HD), lambda h, j: (0, h))
    rows = pl.BlockSpec((None, nq, t), lambda h, j: (h, 0, 0))
    return pl.pallas_call(
        body, name="fox_bwd_dkv",
        grid=(H, nq),
        in_specs=[whole,
                  pl.BlockSpec((t, 2 * HD), lambda h, j: (j, 4 + h)),
                  whole, rows, rows,
                  pl.BlockSpec((None, t, HD), lambda h, j: (h, j, 0)),
                  rows,
                  pl.BlockSpec(memory_space=pl.ANY)],
        out_specs=[pl.BlockSpec((t, 2 * HD), lambda h, j: (j, 4 + h)),
                   pl.BlockSpec((None, t, HD), lambda h, j: (h, j, 0))],
        out_shape=[jax.ShapeDtypeStruct((S, ZW), BF), jax.ShapeDtypeStruct((H, S, HD), F32)],
        scratch_shapes=[pltpu.VMEM((t, HD), F32), pltpu.VMEM((t, HD), F32), pltpu.VMEM((t, 1), F32)],
        input_output_aliases={7: 0},
        compiler_params=_params(2),
    )(z, z, do, lse_r, delta_r, ccol_b, crow, dz)


def _sgu_mask(transposed):
    r = lax.broadcasted_iota(jnp.int32, (L, L), 0)
    c = lax.broadcasted_iota(jnp.int32, (L, L), 1)
    if transposed:
        return (r // CHUNK) <= (c // CHUNK)
    return (c // CHUNK) <= (r // CHUNK)


def _ln_group(vs, lng, lnb):
    mu = jnp.mean(vs, axis=-1, keepdims=True)
    xc = vs - mu
    rstd = lax.rsqrt(jnp.mean(xc * xc, axis=-1, keepdims=True) + LN_EPS)
    xhat = xc * rstd
    return xhat, rstd, xhat * lng + lnb


def _mix_out_fwd(z, oa, ln_g, ln_b, ws, bst, wout, x1, g_post, tm):
    S = z.shape[0]
    nw = tm // L

    def body(u_ref, sv_ref, ga_ref, gb_ref, oa_ref, lng_ref, lnb_ref, ws_ref, bst_ref, wout_ref, x1_ref, gp_ref,
             mg_ref, y_ref, x2_ref, mg_s):
        mask = _sgu_mask(False)
        for g in range(G):
            cols = slice(g * L, (g + 1) * L)
            wm = jnp.where(mask, ws_ref[g], 0.0).astype(BF)
            bcol = bst_ref[:, g:g + 1]
            lng = lng_ref[:, cols]
            lnb = lnb_ref[:, cols]
            for w in range(nw):
                rows = slice(w * L, (w + 1) * L)
                vs = _gelu(sv_ref[rows, cols].astype(F32))
                _, _, vn = _ln_group(vs, lng, lnb)
                mixed = _dot(wm, vn.astype(BF)) + bcol
                ob = _gelu(u_ref[rows, cols].astype(F32)) * mixed
                mg = (_sigmoid(ga_ref[rows, cols].astype(F32)) * oa_ref[rows, cols].astype(F32)
                      + _sigmoid(gb_ref[rows, cols].astype(F32)) * ob)
                mg_s[rows, cols] = mg.astype(BF)
        mgb = mg_s[...]
        mg_ref[...] = mgb
        y = _dot(mgb, wout_ref[...])
        y_ref[...] = y
        x2_ref[...] = x1_ref[...] + _rms(y, gp_ref[...])

    row = pl.BlockSpec((tm, D), lambda i: (i, 0))
    vec = pl.BlockSpec((1, D), lambda i: (0, 0))

    def zcol(kb):
        return pl.BlockSpec((tm, D), lambda i: (i, kb))

    return pl.pallas_call(
        body, name="mix_out_fwd",
        grid=(S // tm,),
        in_specs=[zcol(3), zcol(4), zcol(5), zcol(6), row, vec, vec,
                  pl.BlockSpec((G, L, L), lambda i: (0, 0, 0)),
                  pl.BlockSpec((L, HD), lambda i: (0, 0)),
                  pl.BlockSpec((D, D), lambda i: (0, 0)),
                  row, vec],
        out_specs=[row, row, row],
        out_shape=[jax.ShapeDtypeStruct((S, D), BF),
                   jax.ShapeDtypeStruct((S, D), F32),
                   jax.ShapeDtypeStruct((S, D), F32)],
        scratch_shapes=[pltpu.VMEM((tm, D), BF)],
        compiler_params=_params(1),
    )(z, z, z, z, oa, ln_g, ln_b, ws, bst, wout, x1, g_post)


def _mix_out_bwd(dx2, y2, g_post, wout, z, oa, ln_g, ln_b, ws, wst, bst, tm):
    S = z.shape[0]
    nw = tm // L

    def body(dx2_ref, y_ref, gp_ref, wout_ref, u_ref, sv_ref, ga_ref, gb_ref, oa_ref, lng_ref, lnb_ref,
             ws_ref, wst_ref, bst_ref,
             dz_ref, dy_ref, doa_ref, dgp_ref, dlng_ref, dlnb_ref, dws_ref, dbst_ref, dzg_s, dm_s):
        i = pl.program_id(0)
        c = pl.program_id(1)

        @pl.when((i == 0) & (c == 0))
        def _():
            dgp_ref[...] = jnp.zeros_like(dgp_ref)
            dlng_ref[...] = jnp.zeros_like(dlng_ref)
            dlnb_ref[...] = jnp.zeros_like(dlnb_ref)
            dws_ref[...] = jnp.zeros_like(dws_ref)
            dbst_ref[...] = jnp.zeros_like(dbst_ref)

        @pl.when(c == 0)
        def _():
            dy, dg = _rms_bwd(dx2_ref[...], y_ref[...], gp_ref[...])
            dyb = dy.astype(BF)
            dy_ref[...] = dyb
            dgp_ref[...] += dg
            dm_s[...] = _dot_nt(dyb, wout_ref[...])
            mask = _sgu_mask(False)
            mask_t = _sgu_mask(True)
            lane = lax.broadcasted_iota(jnp.int32, (L, HD), 1)
            for g in range(G):
                cols = slice(g * L, (g + 1) * L)
                wm = jnp.where(mask, ws_ref[g], 0.0).astype(BF)
                wmt = jnp.where(mask_t, wst_ref[g], 0.0).astype(BF)
                bcol = bst_ref[:, g:g + 1]
                lng = lng_ref[:, cols]
                lnb = lnb_ref[:, cols]
                dws_g = jnp.zeros((L, L), F32)
                dbs_g = jnp.zeros((L, 1), F32)
                dlng_g = jnp.zeros((1, L), F32)
                dlnb_g = jnp.zeros((1, L), F32)
                for w in range(nw):
                    rows = slice(w * L, (w + 1) * L)
                    dm = dm_s[rows, cols]
                    vs, dvs_dz = _gelu_parts(sv_ref[rows, cols].astype(F32))
                    xhat, rstd, vn = _ln_group(vs, lng, lnb)
                    vnb = vn.astype(BF)
                    mixed = _dot(wm, vnb) + bcol
                    u, du_dz = _gelu_parts(u_ref[rows, cols].astype(F32))
                    sga = _sigmoid(ga_ref[rows, cols].astype(F32))
                    sgb = _sigmoid(gb_ref[rows, cols].astype(F32))
                    oav = oa_ref[rows, cols].astype(F32)
                    ob = u * mixed
                    doa_ref[rows, cols] = (dm * sga).astype(BF)
                    dzg_s[2, rows, cols] = (dm * oav * sga * (1.0 - sga)).astype(BF)
                    dzg_s[3, rows, cols] = (dm * ob * sgb * (1.0 - sgb)).astype(BF)
                    dob = dm * sgb
                    dzg_s[0, rows, cols] = (dob * mixed * du_dz).astype(BF)
                    dmixed = dob * u
                    dmb = dmixed.astype(BF)
                    dbs_g += jnp.sum(dmixed, axis=1, keepdims=True)
                    dws_g += _dot_nt(dmb, vnb)
                    dvn = _dot(wmt, dmb)
                    dlng_g += jnp.sum(dvn * xhat, axis=0, keepdims=True)
                    dlnb_g += jnp.sum(dvn, axis=0, keepdims=True)
                    dxh = dvn * lng
                    dvs = rstd * (dxh - jnp.mean(dxh, axis=-1, keepdims=True)
                                  - xhat * jnp.mean(dxh * xhat, axis=-1, keepdims=True))
                    dzg_s[1, rows, cols] = (dvs * dvs_dz).astype(BF)
                dws_ref[g] += jnp.where(mask, dws_g, 0.0)
                dbst_ref[...] += jnp.where(lane == g, dbs_g, 0.0)
                dlng_ref[:, cols] += dlng_g
                dlnb_ref[:, cols] += dlnb_g

        dz_ref[...] = dzg_s[c]

    row = pl.BlockSpec((tm, D), lambda i, c: (i, 0))
    vec = pl.BlockSpec((1, D), lambda i, c: (0, 0))
    wsspec = pl.BlockSpec((G, L, L), lambda i, c: (0, 0, 0))
    bspec = pl.BlockSpec((L, HD), lambda i, c: (0, 0))

    def zcol(kb):
        return pl.BlockSpec((tm, D), lambda i, c: (i, kb))

    return pl.pallas_call(
        body, name="mix_out_bwd",
        grid=(S // tm, 4),
        in_specs=[row, row, vec, pl.BlockSpec((D, D), lambda i, c: (0, 0)),
                  zcol(3), zcol(4), zcol(5), zcol(6), row, vec, vec, wsspec, wsspec, bspec],
        out_specs=[pl.BlockSpec((tm, D), lambda i, c: (i, 3 + c)),
                   row, row, vec, vec, vec, wsspec, bspec],
        out_shape=[jax.ShapeDtypeStruct((S, ZW), BF),
                   jax.ShapeDtypeStruct((S, D), BF),
                   jax.ShapeDtypeStruct((S, D), BF),
                   jax.ShapeDtypeStruct((1, D), F32),
                   jax.ShapeDtypeStruct((1, D), F32),
                   jax.ShapeDtypeStruct((1, D), F32),
                   jax.ShapeDtypeStruct((G, L, L), F32),
                   jax.ShapeDtypeStruct((L, HD), F32)],
        scratch_shapes=[pltpu.VMEM((4, tm, D), BF), pltpu.VMEM((tm, D), F32)],
        compiler_params=_params(2),
    )(dx2, y2, g_post, wout, z, z, z, z, oa, ln_g, ln_b, ws, wst, bst)


def _loss_head(x3, target, tm):
    S = x3.shape[0]

    def body(x_ref, t_ref, dx_ref, loss_ref):
        @pl.when(pl.program_id(0) == 0)
        def _():
            loss_ref[...] = jnp.zeros_like(loss_ref)

        e = x_ref[...] - t_ref[...]
        dx_ref[...] = e * (1.0 / D)
        loss_ref[...] += jnp.sum(e * e) * (0.5 / D)

    row = pl.BlockSpec((tm, D), lambda i: (i, 0))
    return pl.pallas_call(
        body, name="loss_head",
        grid=(S // tm,),
        in_specs=[row, row],
        out_specs=[row, pl.BlockSpec((8, HD), lambda i: (0, 0))],
        out_shape=[jax.ShapeDtypeStruct((S, D), F32), jax.ShapeDtypeStruct((8, HD), F32)],
        compiler_params=_params(1),
    )(x3, target)


def _adamw(w, g, m, v, tr):
    R, C = w.shape

    def body(w_ref, g_ref, m_ref, v_ref, d_ref, nm_ref, nv_ref):
        gv = g_ref[...]
        m_new = ADAM_B1 * m_ref[...] + (1.0 - ADAM_B1) * gv
        v_new = ADAM_B2 * v_ref[...] + (1.0 - ADAM_B2) * (gv * gv)
        m_hat = m_new / (1.0 - ADAM_B1 ** ADAM_STEP)
        v_hat = v_new / (1.0 - ADAM_B2 ** ADAM_STEP)
        d_ref[...] = -ADAM_LR * (m_hat / (jnp.sqrt(v_hat) + ADAM_EPS) + ADAM_WD * w_ref[...])
        nm_ref[...] = m_new
        nv_ref[...] = v_new

    spec = pl.BlockSpec((tr, C), lambda i: (i, 0))
    shp = jax.ShapeDtypeStruct((R, C), F32)
    return pl.pallas_call(
        body, name="adamw",
        grid=(R // tr,),
        in_specs=[spec] * 4, out_specs=[spec] * 3, out_shape=[shp] * 3,
        compiler_params=_params(1),
    )(w, g, m, v)


def _mesh_pos():
    return lax.axis_index("x"), lax.axis_index("y"), lax.axis_index("c")


def _half(c):
    return pl.ds(pl.multiple_of(c * HALF_ROWS, 16), HALF_ROWS)


def _all_gather_weights(wp):
    def body(wp_ref, g_ref, send_sems, recv_sems, local_sem):
        x, y, c = _mesh_pos()
        chips = [(1 - x, y), (x, 1 - y), (1 - x, 1 - y)]
        sibling = (x, y, 1 - c)
        j_me = 2 * x + y
        mine, other = _half(c), _half(1 - c)

        def copy(k, src, dst, to):
            return pltpu.make_async_remote_copy(src_ref=src, dst_ref=dst, send_sem=send_sems.at[k],
                                                recv_sem=recv_sems.at[k], device_id=to, device_id_type=MESH)

        own = pltpu.make_async_copy(wp_ref, g_ref.at[j_me], local_sem)
        own.start()
        first = [copy(k, wp_ref.at[mine], g_ref.at[j_me, mine], (px, py, c)) for k, (px, py) in enumerate(chips)]
        for cp in first:
            cp.start()
        passed = []
        for k, (px, py) in enumerate(chips):
            land = g_ref.at[2 * px + py, mine]
            copy(k, land, land, (px, py, c)).wait_recv()
            fwd = copy(3 + k, land, land, sibling)
            fwd.start()
            passed.append(fwd)
        for k, (px, py) in enumerate(chips):
            land = g_ref.at[2 * px + py, other]
            copy(3 + k, land, land, sibling).wait_recv()
        for cp in first + passed:
            cp.wait_send()
        own.wait()

    return pl.pallas_call(
        body, name="all_gather_weights",
        in_specs=[pl.BlockSpec(memory_space=pl.ANY)],
        out_specs=pl.BlockSpec(memory_space=pl.ANY),
        out_shape=jax.ShapeDtypeStruct((NSH, PACK_ROWS, D), BF),
        scratch_shapes=[pltpu.SemaphoreType.DMA((6,)), pltpu.SemaphoreType.DMA((6,)), pltpu.SemaphoreType.DMA],
        compiler_params=pltpu.CompilerParams(has_side_effects=True),
    )(wp)


def _pair_exchange(p):
    def body(p_ref, r_ref, send_sem, recv_sem):
        x, y, c = _mesh_pos()
        cp = pltpu.make_async_remote_copy(src_ref=p_ref.at[:, _half(1 - c)], dst_ref=r_ref, send_sem=send_sem,
                                          recv_sem=recv_sem, device_id=(x, y, 1 - c), device_id_type=MESH)
        cp.start()
        cp.wait()

    return pl.pallas_call(
        body, name="pair_exchange",
        in_specs=[pl.BlockSpec(memory_space=pl.ANY)],
        out_specs=pl.BlockSpec(memory_space=pl.ANY),
        out_shape=jax.ShapeDtypeStruct((NSH, HALF_ROWS, D), F32),
        scratch_shapes=[pltpu.SemaphoreType.DMA, pltpu.SemaphoreType.DMA],
        compiler_params=pltpu.CompilerParams(has_side_effects=True),
    )(p)


def _pair_add(p, r, c_arr, tr):
    nb = HALF_ROWS // tr

    def body(c_ref, p_ref, r_ref, q_ref):
        del c_ref
        q_ref[...] = (p_ref[...] + r_ref[...]).astype(BF)

    grid_spec = pltpu.PrefetchScalarGridSpec(
        num_scalar_prefetch=1, grid=(NSH, nb),
        in_specs=[pl.BlockSpec((None, tr, D), lambda j, i, c: (j, c[0] * nb + i, 0)),
                  pl.BlockSpec((None, tr, D), lambda j, i, c: (j, i, 0))],
        out_specs=pl.BlockSpec((None, tr, D), lambda j, i, c: (j, i, 0)))
    return pl.pallas_call(
        body, name="pair_add", grid_spec=grid_spec,
        out_shape=jax.ShapeDtypeStruct((NSH, HALF_ROWS, D), BF),
        compiler_params=_params(2),
    )(c_arr, p, r)


def _chip_exchange(q, small):
    def body(q_ref, s_ref, t_ref, sm_ref, send_sems, recv_sems, ssend_sems, srecv_sems, local_sem):
        x, y, c = _mesh_pos()
        me = 4 * x + 2 * y + c
        chips = [(1 - x, y), (x, 1 - y), (1 - x, 1 - y)]
        own = pltpu.make_async_copy(s_ref, sm_ref.at[me], local_sem)
        own.start()
        big = []
        for k, (px, py) in enumerate(chips):
            cp = pltpu.make_async_remote_copy(src_ref=q_ref.at[2 * px + py], dst_ref=t_ref.at[k],
                                              send_sem=send_sems.at[k], recv_sem=recv_sems.at[k],
                                              device_id=(px, py, c), device_id_type=MESH)
            cp.start()
            big.append(cp)
        flips = [(fx, fy, fc) for fx in (0, 1) for fy in (0, 1) for fc in (0, 1)][1:]
        smalls = []
        for k, (fx, fy, fc) in enumerate(flips):
            cp = pltpu.make_async_remote_copy(src_ref=s_ref, dst_ref=sm_ref.at[me],
                                              send_sem=ssend_sems.at[k], recv_sem=srecv_sems.at[k],
                                              device_id=(x ^ fx, y ^ fy, c ^ fc), device_id_type=MESH)
            cp.start()
            smalls.append(cp)
        for cp in big + smalls:
            cp.wait()
        own.wait()

    return pl.pallas_call(
        body, name="chip_exchange",
        in_specs=[pl.BlockSpec(memory_space=pl.ANY), pl.BlockSpec(memory_space=pl.ANY)],
        out_specs=[pl.BlockSpec(memory_space=pl.ANY), pl.BlockSpec(memory_space=pl.ANY)],
        out_shape=[jax.ShapeDtypeStruct((3, HALF_ROWS, D), BF),
                   jax.ShapeDtypeStruct((NDEV, SMALL_ROWS, D), F32)],
        scratch_shapes=[pltpu.SemaphoreType.DMA((3,)), pltpu.SemaphoreType.DMA((3,)),
                        pltpu.SemaphoreType.DMA((7,)), pltpu.SemaphoreType.DMA((7,)),
                        pltpu.SemaphoreType.DMA],
        compiler_params=pltpu.CompilerParams(has_side_effects=True),
    )(q, small)


def _shard_sum(p, r, t, jc_arr, tr):
    nb = HALF_ROWS // tr

    def body(jc_ref, p_ref, r_ref, t_ref, o_ref):
        del jc_ref
        s = p_ref[...] + r_ref[...]
        for k in range(3):
            s = s + t_ref[k].astype(F32)
        o_ref[...] = s

    grid_spec = pltpu.PrefetchScalarGridSpec(
        num_scalar_prefetch=1, grid=(nb,),
        in_specs=[pl.BlockSpec((None, tr, D), lambda i, jc: (jc[0], jc[1] * nb + i, 0)),
                  pl.BlockSpec((None, tr, D), lambda i, jc: (jc[0], i, 0)),
                  pl.BlockSpec((3, tr, D), lambda i, jc: (0, i, 0))],
        out_specs=pl.BlockSpec((tr, D), lambda i, jc: (i, 0)))
    return pl.pallas_call(
        body, name="shard_sum", grid_spec=grid_spec,
        out_shape=jax.ShapeDtypeStruct((HALF_ROWS, D), F32),
        compiler_params=_params(1),
    )(jc_arr, p, r, t)


def _small_sum(sm):
    def body(sm_ref, o_ref):
        s = sm_ref[0]
        for k in range(1, NDEV):
            s = s + sm_ref[k]
        o_ref[...] = s

    return pl.pallas_call(
        body, name="small_sum",
        in_specs=[pl.BlockSpec(memory_space=pltpu.VMEM)],
        out_specs=pl.BlockSpec(memory_space=pltpu.VMEM),
        out_shape=jax.ShapeDtypeStruct((SMALL_ROWS, D), F32),
    )(sm)


def _pair_gather(gh):
    def body(gh_ref, o_ref, send_sem, recv_sem, local_sem):
        x, y, c = _mesh_pos()
        own = pltpu.make_async_copy(gh_ref, o_ref.at[_half(c)], local_sem)
        own.start()
        cp = pltpu.make_async_remote_copy(src_ref=gh_ref, dst_ref=o_ref.at[_half(c)], send_sem=send_sem,
                                          recv_sem=recv_sem, device_id=(x, y, 1 - c), device_id_type=MESH)
        cp.start()
        cp.wait_send()
        pltpu.make_async_remote_copy(src_ref=gh_ref, dst_ref=o_ref.at[_half(1 - c)], send_sem=send_sem,
                                     recv_sem=recv_sem, device_id=(x, y, 1 - c), device_id_type=MESH).wait_recv()
        own.wait()

    return pl.pallas_call(
        body, name="pair_gather",
        in_specs=[pl.BlockSpec(memory_space=pl.ANY)],
        out_specs=pl.BlockSpec(memory_space=pl.ANY),
        out_shape=jax.ShapeDtypeStruct((PACK_ROWS, D), F32),
        scratch_shapes=[pltpu.SemaphoreType.DMA, pltpu.SemaphoreType.DMA, pltpu.SemaphoreType.DMA],
        compiler_params=pltpu.CompilerParams(has_side_effects=True),
    )(gh)


def _pad_cols(a, n):
    return jnp.pad(a, ((0, 0), (0, n - a.shape[1])))


def _split_w_in(w_in_full):
    q, k, v = w_in_full[:, :D], w_in_full[:, D:2 * D], w_in_full[:, 2 * D:3 * D]
    f = w_in_full[:, 3 * D:3 * D + H]
    gates = w_in_full[:, 3 * D + H:]
    kv = jnp.stack([k.reshape(D, H, HD), v.reshape(D, H, HD)], axis=2).reshape(D, 2 * D)
    return jnp.concatenate([q, kv, gates], axis=1), _pad_cols(f, HD)


def _merge_w_in_grad(dwcat, dwf):
    kv = dwcat[:, D:3 * D].reshape(D, H, 2, HD)
    return jnp.concatenate([dwcat[:, :D], kv[:, :, 0].reshape(D, D), kv[:, :, 1].reshape(D, D),
                            dwf[:, :H], dwcat[:, 3 * D:]], axis=1)


def _local_step(x, target, wpack, small, tm, t_attn):
    S = x.shape[0]
    g1pre, g1post = small["ffn1_pre_g"], small["ffn1_post_g"]
    gmpre, gmpost = small["mix_pre_g"], small["mix_post_g"]
    g2pre, g2post = small["ffn2_pre_g"], small["ffn2_post_g"]
    ln_g, ln_b = small["sgu_ln_g"], small["sgu_ln_b"]
    ws = small["sgu_w_s"][0]
    wst = jnp.swapaxes(ws, 1, 2)
    bst = _pad_cols(small["sgu_b_s"][0].T, HD)
    bf = _pad_cols(small["b_forget"], HD)

    wout = wpack[:, ROW_WOUT:ROW_WOUT + 256, :].reshape(D, D)
    w_in_full = jnp.transpose(wpack[:, ROW_WIN:ROW_WIN + WIN_SH, :].reshape(NSH, D, WIN_SH), (1, 0, 2)).reshape(D, NSH * WIN_SH)
    wcat, wf = _split_w_in(w_in_full)

    h1, a1, b1, y1, x1 = _ffn_fwd(x, g1pre, wpack, ROW_FFN1, g1post, tm)
    h2 = _norm_fwd(x1, gmpre, tm)
    z = _mm(h2, wcat, tm, D, BF)
    zf = _mm(h2, wf, tm, HD, F32)
    cs = min(512, S)
    c, ccol_b = _forget_cumsum(zf, bf, cs)
    crow = jnp.transpose(c[:, :H]).reshape(H, S // t_attn, t_attn)
    oa, lse_b = _fox_fwd(z, ccol_b, crow, t_attn)
    merged, y2, x2 = _mix_out_fwd(z, oa, ln_g, ln_b, ws, bst, wout, x1, gmpost, tm)
    h3, a3, b3, y3, x3 = _ffn_fwd(x2, g2pre, wpack, ROW_FFN2, g2post, tm)
    dx3, loss_acc = _loss_head(x3, target, tm)
    loss = loss_acc[0, 0]

    dy3, da3, db3, act3, dx2, dg2post, dg2pre = _ffn_bwd(dx3, y3, g2post, a3, b3, wpack, ROW_FFN2, x2, g2pre, tm)
    bt = min(512, S)
    dwg2 = _mm_tn(h3, da3, D, D, bt, col_blocked=True)
    dwu2 = _mm_tn(h3, db3, D, D, bt, col_blocked=True)
    dwd2 = _mm_tn(act3, dy3, D, D, bt).reshape(NSH, D, D)

    dz, dy2, doa, dgmpost, dlng, dlnb, dws, dbst = _mix_out_bwd(dx2, y2, gmpost, wout, z, oa, ln_g, ln_b, ws, wst, bst, tm)
    dwout = _mm_tn(merged, dy2, D, D, bt).reshape(NSH, 256, D)
    dz, delta_b, dr_b = _fox_bwd_dq(z, doa, oa, lse_b, ccol_b, crow, dz, t_attn)
    lse_r = lse_b[:, :, 0].reshape(H, S // t_attn, t_attn)
    delta_r = delta_b[:, :, 0].reshape(H, S // t_attn, t_attn)
    dz, dc_b = _fox_bwd_dkv(z, doa, lse_r, delta_r, ccol_b, crow, dz, t_attn)
    dc = _pad_cols(jnp.transpose(dr_b[:, :, 0] + dc_b[:, :, 0]), HD)
    dzf, dbf = _forget_bwd(dc, zf, bf, cs)
    dwcat = _mm_tn(h2, dz, D, D, bt)
    dwf = _mm_tn(h2, dzf, D, HD, bt)
    dx1, dgmpre = _mix_in_bwd(dz, wcat, dzf, wf, x1, gmpre, dx2, tm)
    dwin = _merge_w_in_grad(dwcat, dwf)
    dwin = jnp.transpose(dwin.reshape(D, NSH, WIN_SH), (1, 0, 2)).reshape(NSH, WIN_SH, D)

    dy1, da1, db1, act1, dx, dg1post, dg1pre = _ffn_bwd(dx1, y1, g1post, a1, b1, wpack, ROW_FFN1, x, g1pre, tm)
    dwg1 = _mm_tn(h1, da1, D, D, bt, col_blocked=True)
    dwu1 = _mm_tn(h1, db1, D, D, bt, col_blocked=True)
    dwd1 = _mm_tn(act1, dy1, D, D, bt).reshape(NSH, D, D)

    dwin = jnp.pad(dwin, ((0, 0), (0, PACK_ROWS - ROW_WIN - WIN_SH), (0, 0)))
    gpack = jnp.concatenate([dwg1, dwu1, dwd1, dwg2, dwu2, dwd2, dwout, dwin], axis=1)
    gsmall = {
        "ffn1_pre_g": dg1pre, "ffn1_post_g": dg1post, "mix_pre_g": dgmpre, "mix_post_g": dgmpost,
        "ffn2_pre_g": dg2pre, "ffn2_post_g": dg2post, "sgu_ln_g": dlng, "sgu_ln_b": dlnb,
        "sgu_w_s": dws[None], "sgu_b_s": jnp.transpose(dbst[:, :G])[None], "b_forget": dbf[:, :H],
    }
    return loss, dx, gpack, gsmall


_SMALL_NAMES = ["ffn1_pre_g", "ffn1_post_g", "mix_pre_g", "mix_post_g", "ffn2_pre_g", "ffn2_post_g",
                "sgu_ln_g", "sgu_ln_b", "sgu_b_s", "b_forget", "sgu_w_s"]
_SMALL_SHAPES = {"sgu_b_s": (1, G, L), "b_forget": (1, H), "sgu_w_s": (1, G, L, L)}


def _pack_small(d):
    rows = []
    for n in _SMALL_NAMES:
        a = d[n].astype(F32)
        if n == "b_forget":
            a = _pad_cols(a, D)
        a = a.reshape(-1, D)
        rows.append(jnp.pad(a, ((0, -a.shape[0] % SMALL_STRIDE), (0, 0))))
    return jnp.concatenate(rows, axis=0)


def _unpack_small(p):
    out, r = {}, 0
    for n in _SMALL_NAMES:
        if n == "sgu_w_s":
            out[n] = p[r:r + L].reshape(1, G, L, L)
            r += L
        elif n == "b_forget":
            out[n] = p[r:r + 1, :H]
            r += SMALL_STRIDE
        elif n == "sgu_b_s":
            out[n] = p[r:r + 1].reshape(1, G, L)
            r += SMALL_STRIDE
        else:
            out[n] = p[r:r + 1]
            r += SMALL_STRIDE
    return out


_BIG_NAMES = ["ffn1_w_gate", "ffn1_w_up", "ffn1_w_down", "ffn2_w_gate", "ffn2_w_up", "ffn2_w_down", "w_out", "w_in"]
_WEIGHT_ORDER = ['ffn1_pre_g', 'ffn1_w_gate', 'ffn1_w_up', 'ffn1_w_down', 'ffn1_post_g', 'mix_pre_g', 'w_in', 'b_forget',
                 'sgu_ln_g', 'sgu_ln_b', 'sgu_w_s', 'sgu_b_s', 'w_out', 'mix_post_g', 'ffn2_pre_g', 'ffn2_w_gate',
                 'ffn2_w_up', 'ffn2_w_down', 'ffn2_post_g']


def _pack_big(w):
    parts = [w[n][0] for n in _BIG_NAMES[:7]]
    parts.append(jnp.pad(w["w_in"][0].reshape(WIN_SH, D), ((0, PACK_ROWS - ROW_WIN - WIN_SH), (0, 0))))
    return jnp.concatenate(parts, axis=0)


def _unpack_big(p):
    out = {}
    for k, n in enumerate(_BIG_NAMES[:6]):
        out[n] = p[k * D:(k + 1) * D][None]
    out["w_out"] = p[ROW_WOUT:ROW_WOUT + 256][None]
    out["w_in"] = p[ROW_WIN:ROW_WIN + WIN_SH].reshape(1, D, WIN_SH)
    return out


def _step(args, tm, t_attn):
    S = args["x"].shape[1]
    x = args["x"][0]
    target = args["loss_target"][0]
    weights = {n: args[n] for n in _WEIGHT_ORDER}
    small = {n: weights[n] for n in _SMALL_NAMES}

    xi, yi, ci = _mesh_pos()
    wpack = _all_gather_weights(_pack_big({n: weights[n].astype(BF) for n in _BIG_NAMES}))
    loss_local, dx, gpack, gsmall = _local_step(x, target, wpack, small, tm, t_attn)
    loss = lax.psum(loss_local, ("x", "y", "c"))

    c_arr = jnp.reshape(ci, (1,)).astype(jnp.int32)
    jc_arr = jnp.stack([2 * xi + yi, ci]).astype(jnp.int32)
    tr = HALF_ROWS // 13
    r = _pair_exchange(gpack)
    q = _pair_add(gpack, r, c_arr, tr)
    t, sm = _chip_exchange(q, _pack_small(gsmall))
    gh = _shard_sum(gpack, r, t, jc_arr, tr)
    gbig = _unpack_big(_pair_gather(gh))
    gsm = _small_sum(sm)
    grads = {**gbig, **_unpack_small(gsm)}

    delta, new_m, new_v = {}, {}, {}
    for n in _BIG_NAMES:
        shp = weights[n].shape
        w2 = weights[n].reshape(-1, shp[-1])
        rows = w2.shape[0]
        d, nm, nv = _adamw(w2, grads[n].reshape(w2.shape), args["m_" + n].reshape(w2.shape),
                           args["v_" + n].reshape(w2.shape), rows // 4)
        delta[n], new_m[n], new_v[n] = d.reshape(shp), nm.reshape(shp), nv.reshape(shp)
    ds, nms, nvs = _adamw(_pack_small(small), gsm, _pack_small({n: args["m_" + n] for n in _SMALL_NAMES}),
                          _pack_small({n: args["v_" + n] for n in _SMALL_NAMES}), SMALL_ROWS)
    delta.update(_unpack_small(ds))
    new_m.update(_unpack_small(nms))
    new_v.update(_unpack_small(nvs))

    return (loss, dx[None], *[grads[n] for n in _WEIGHT_ORDER], *[delta[n] for n in _WEIGHT_ORDER],
            *[new_m[n] for n in _WEIGHT_ORDER], *[new_v[n] for n in _WEIGHT_ORDER])


_ARG_NAMES = (["x"] + _WEIGHT_ORDER + ["loss_target"] + ["m_" + n for n in _WEIGHT_ORDER]
              + ["v_" + n for n in _WEIGHT_ORDER])


def kernel(x, ffn1_pre_g, ffn1_w_gate, ffn1_w_up, ffn1_w_down, ffn1_post_g, mix_pre_g, w_in, b_forget, sgu_ln_g, sgu_ln_b, sgu_w_s, sgu_b_s, w_out, mix_post_g, ffn2_pre_g, ffn2_w_gate, ffn2_w_up, ffn2_w_down, ffn2_post_g, loss_target, m_ffn1_pre_g, m_ffn1_w_gate, m_ffn1_w_up, m_ffn1_w_down, m_ffn1_post_g, m_mix_pre_g, m_w_in, m_b_forget, m_sgu_ln_g, m_sgu_ln_b, m_sgu_w_s, m_sgu_b_s, m_w_out, m_mix_post_g, m_ffn2_pre_g, m_ffn2_w_gate, m_ffn2_w_up, m_ffn2_w_down, m_ffn2_post_g, v_ffn1_pre_g, v_ffn1_w_gate, v_ffn1_w_up, v_ffn1_w_down, v_ffn1_post_g, v_mix_pre_g, v_w_in, v_b_forget, v_sgu_ln_g, v_sgu_ln_b, v_sgu_w_s, v_sgu_b_s, v_w_out, v_mix_post_g, v_ffn2_pre_g, v_ffn2_w_gate, v_ffn2_w_up, v_ffn2_w_down, v_ffn2_post_g):
    args = (x, ffn1_pre_g, ffn1_w_gate, ffn1_w_up, ffn1_w_down, ffn1_post_g, mix_pre_g, w_in, b_forget, sgu_ln_g, sgu_ln_b, sgu_w_s, sgu_b_s, w_out, mix_post_g, ffn2_pre_g, ffn2_w_gate, ffn2_w_up, ffn2_w_down, ffn2_post_g, loss_target, m_ffn1_pre_g, m_ffn1_w_gate, m_ffn1_w_up, m_ffn1_w_down, m_ffn1_post_g, m_mix_pre_g, m_w_in, m_b_forget, m_sgu_ln_g, m_sgu_ln_b, m_sgu_w_s, m_sgu_b_s, m_w_out, m_mix_post_g, m_ffn2_pre_g, m_ffn2_w_gate, m_ffn2_w_up, m_ffn2_w_down, m_ffn2_post_g, v_ffn1_pre_g, v_ffn1_w_gate, v_ffn1_w_up, v_ffn1_w_down, v_ffn1_post_g, v_mix_pre_g, v_w_in, v_b_forget, v_sgu_ln_g, v_sgu_ln_b, v_sgu_w_s, v_sgu_b_s, v_w_out, v_mix_post_g, v_ffn2_pre_g, v_ffn2_w_gate, v_ffn2_w_up, v_ffn2_w_down, v_ffn2_post_g)
    named = dict(zip(_ARG_NAMES, args))
    tile = min(512, x.shape[1])
    return _step(named, tile, tile)
```

```python
import functools
import math

import jax
import jax.numpy as jnp
from jax import lax
from jax.experimental import pallas as pl
from jax.experimental.pallas import tpu as pltpu

D = 1024
F = 4096
H = 8
HD = 128
G = 8
L = 128
CHUNK = 64
NSH = 4
NDEV = 8
ZW = 7 * D
RMS_EPS = 1e-6
LN_EPS = 1e-5
NEG = -1e30
SCALE = 1.0 / math.sqrt(HD)
LOG2E = math.log2(math.e)
LN2 = math.log(2.0)

ADAM_LR = 0.001
ADAM_B1 = 0.9
ADAM_B2 = 0.999
ADAM_EPS = 1e-08
ADAM_WD = 0.01
ADAM_STEP = 10

VMEM_LIMIT_BYTES = 56 * 1024 * 1024

ROW_FFN1 = 0
ROW_FFN2 = 3 * D
ROW_WOUT = 6 * D
ROW_WIN = 6 * D + 256
WIN_SH = 1794
PACK_ROWS = 8320
HALF_ROWS = PACK_ROWS // 2
SMALL_STRIDE = 8
SMALL_ROWS = 10 * SMALL_STRIDE + L

BF = jnp.bfloat16
F32 = jnp.float32
MESH = pl.DeviceIdType.MESH


def _params(n_grid):
    return pltpu.CompilerParams(dimension_semantics=("arbitrary",) * n_grid,
                                vmem_limit_bytes=VMEM_LIMIT_BYTES)


def _dot(a, b):
    return jnp.dot(a, b, preferred_element_type=F32)


def _dot_nt(a, b):
    return lax.dot_general(a, b, (((1,), (1,)), ((), ())), preferred_element_type=F32)


def _dot_tn(a, b):
    return lax.dot_general(a, b, (((0,), (0,)), ((), ())), preferred_element_type=F32)


def _rms(x, g):
    r = lax.rsqrt(jnp.mean(x * x, axis=-1, keepdims=True) + RMS_EPS)
    return x * r * g


def _rms_bwd(dn, x, g):
    r = lax.rsqrt(jnp.mean(x * x, axis=-1, keepdims=True) + RMS_EPS)
    xr = x * r
    dg = jnp.sum(dn * xr, axis=0, keepdims=True)
    t = dn * g
    dx = r * (t - xr * jnp.mean(t * xr, axis=-1, keepdims=True))
    return dx, dg


def _gelu_parts(x):
    cdf = 0.5 * (1.0 + lax.erf(x * (1.0 / math.sqrt(2.0))))
    pdf = jnp.exp(-0.5 * x * x) * (1.0 / math.sqrt(2.0 * math.pi))
    return x * cdf, cdf + x * pdf


def _gelu(x):
    return x * (0.5 * (1.0 + lax.erf(x * (1.0 / math.sqrt(2.0)))))


def _sigmoid(x):
    return 1.0 / (1.0 + jnp.exp(-x))


def _ffn_fwd(x, g_pre, wpack, row0, g_post, tm):
    S = x.shape[0]
    nt, nf, tf = S // tm, NSH, D
    rb = row0 // D

    def body(x_ref, gpre_ref, wg_ref, wu_ref, wd_ref, gpost_ref,
             h_ref, a_ref, b_ref, y_ref, xo_ref, h_s, acc):
        j = pl.program_id(1)

        @pl.when(j == 0)
        def _():
            h = _rms(x_ref[...], gpre_ref[...]).astype(BF)
            h_s[...] = h
            h_ref[...] = h
            acc[...] = jnp.zeros_like(acc)

        h = h_s[...]
        a = _dot(h, wg_ref[...])
        b = _dot(h, wu_ref[...])
        a_ref[...] = a.astype(BF)
        b_ref[...] = b.astype(BF)
        act = (a * _sigmoid(a)) * b
        acc[...] += _dot(act.astype(BF), wd_ref[...])

        @pl.when(j == nf - 1)
        def _():
            y = acc[...]
            y_ref[...] = y
            xo_ref[...] = x_ref[...] + 0.5 * _rms(y, gpost_ref[...])

    row = pl.BlockSpec((tm, D), lambda i, j: (i, 0))
    vec = pl.BlockSpec((1, D), lambda i, j: (0, 0))
    return pl.pallas_call(
        body, name="ffn_fwd",
        grid=(nt, nf),
        in_specs=[row, vec,
                  pl.BlockSpec((None, D, tf), lambda i, j: (j, rb, 0)),
                  pl.BlockSpec((None, D, tf), lambda i, j: (j, rb + 1, 0)),
                  pl.BlockSpec((None, tf, D), lambda i, j: (j, rb + 2, 0)),
                  vec],
        out_specs=[row,
                   pl.BlockSpec((tm, tf), lambda i, j: (i, j)),
                   pl.BlockSpec((tm, tf), lambda i, j: (i, j)),
                   row, row],
        out_shape=[jax.ShapeDtypeStruct((S, D), BF),
                   jax.ShapeDtypeStruct((S, F), BF),
                   jax.ShapeDtypeStruct((S, F), BF),
                   jax.ShapeDtypeStruct((S, D), F32),
                   jax.ShapeDtypeStruct((S, D), F32)],
        scratch_shapes=[pltpu.VMEM((tm, D), BF), pltpu.VMEM((tm, D), F32)],
        compiler_params=_params(2),
    )(x, g_pre, wpack, wpack, wpack, g_post)


def _ffn_bwd(dxo, y, g_post, a, b, wpack, row0, x_in, g_pre, tm):
    S = dxo.shape[0]
    nt, nf, tf = S // tm, NSH, D
    rb = row0 // D

    def body(dxo_ref, y_ref, gpost_ref, a_ref, b_ref, wg_ref, wu_ref, wd_ref, xin_ref, gpre_ref,
             dy_ref, da_ref, db_ref, act_ref, dxin_ref, dgpost_ref, dgpre_ref, dy_s, acc):
        i = pl.program_id(0)
        j = pl.program_id(1)

        @pl.when((i == 0) & (j == 0))
        def _():
            dgpost_ref[...] = jnp.zeros_like(dgpost_ref)
            dgpre_ref[...] = jnp.zeros_like(dgpre_ref)

        @pl.when(j == 0)
        def _():
            dy, dg = _rms_bwd(0.5 * dxo_ref[...], y_ref[...], gpost_ref[...])
            dyb = dy.astype(BF)
            dy_s[...] = dyb
            dy_ref[...] = dyb
            dgpost_ref[...] += dg
            acc[...] = jnp.zeros_like(acc)

        dact = _dot_nt(dy_s[...], wd_ref[...])
        av = a_ref[...].astype(F32)
        bv = b_ref[...].astype(F32)
        sig = _sigmoid(av)
        sl = av * sig
        act_ref[...] = (sl * bv).astype(BF)
        dbb = (dact * sl).astype(BF)
        dab = (dact * bv * (sig * (1.0 + av * (1.0 - sig)))).astype(BF)
        da_ref[...] = dab
        db_ref[...] = dbb
        acc[...] += _dot_nt(dab, wg_ref[...]) + _dot_nt(dbb, wu_ref[...])

        @pl.when(j == nf - 1)
        def _():
            dx, dg = _rms_bwd(acc[...], xin_ref[...], gpre_ref[...])
            dxin_ref[...] = dxo_ref[...] + dx
            dgpre_ref[...] += dg

    row = pl.BlockSpec((tm, D), lambda i, j: (i, 0))
    vec = pl.BlockSpec((1, D), lambda i, j: (0, 0))
    ff = pl.BlockSpec((tm, tf), lambda i, j: (i, j))
    return pl.pallas_call(
        body, name="ffn_bwd",
        grid=(nt, nf),
        in_specs=[row, row, vec, ff, ff,
                  pl.BlockSpec((None, D, tf), lambda i, j: (j, rb, 0)),
                  pl.BlockSpec((None, D, tf), lambda i, j: (j, rb + 1, 0)),
                  pl.BlockSpec((None, tf, D), lambda i, j: (j, rb + 2, 0)),
                  row, vec],
        out_specs=[row, ff, ff, ff, row, vec, vec],
        out_shape=[jax.ShapeDtypeStruct((S, D), BF),
                   jax.ShapeDtypeStruct((S, F), BF),
                   jax.ShapeDtypeStruct((S, F), BF),
                   jax.ShapeDtypeStruct((S, F), BF),
                   jax.ShapeDtypeStruct((S, D), F32),
                   jax.ShapeDtypeStruct((1, D), F32),
                   jax.ShapeDtypeStruct((1, D), F32)],
        scratch_shapes=[pltpu.VMEM((tm, D), BF), pltpu.VMEM((tm, D), F32)],
        compiler_params=_params(2),
    )(dxo, y, g_post, a, b, wpack, wpack, wpack, x_in, g_pre)


def _mm_tn(a, b, bm, bn, bt, col_blocked=False):
    S, M = a.shape
    N = b.shape[1]
    nt = S // bt

    def body(a_ref, b_ref, o_ref):
        t = pl.program_id(2)

        @pl.when(t == 0)
        def _():
            o_ref[...] = jnp.zeros_like(o_ref)

        o_ref[...] += _dot_tn(a_ref[...], b_ref[...])

    if col_blocked:
        out_spec = pl.BlockSpec((None, bm, bn), lambda m, n, t: (n, m, 0))
        out_shape = jax.ShapeDtypeStruct((N // bn, M, bn), F32)
    else:
        out_spec = pl.BlockSpec((bm, bn), lambda m, n, t: (m, n))
        out_shape = jax.ShapeDtypeStruct((M, N), F32)
    return pl.pallas_call(
        body, name="mm_tn",
        grid=(M // bm, N // bn, nt),
        in_specs=[pl.BlockSpec((bt, bm), lambda m, n, t: (t, m)),
                  pl.BlockSpec((bt, bn), lambda m, n, t: (t, n))],
        out_specs=out_spec, out_shape=out_shape,
        compiler_params=_params(3),
    )(a, b)


def _mm(a, w, tm, tn, out_dtype, first_block_scale=1.0):
    S, K = a.shape
    N = w.shape[1]

    def body(a_ref, w_ref, o_ref):
        r = _dot(a_ref[...], w_ref[...])
        if first_block_scale != 1.0:
            r = r * jnp.where(pl.program_id(1) == 0, first_block_scale, 1.0)
        o_ref[...] = r.astype(out_dtype)

    return pl.pallas_call(
        body, name="mm",
        grid=(S // tm, N // tn),
        in_specs=[pl.BlockSpec((tm, K), lambda i, j: (i, 0)),
                  pl.BlockSpec((K, tn), lambda i, j: (0, j))],
        out_specs=pl.BlockSpec((tm, tn), lambda i, j: (i, j)),
        out_shape=jax.ShapeDtypeStruct((S, N), out_dtype),
        compiler_params=_params(2),
    )(a, w)


def _norm_fwd(x, g, tm):
    S = x.shape[0]

    def body(x_ref, g_ref, h_ref):
        h_ref[...] = _rms(x_ref[...], g_ref[...]).astype(BF)

    return pl.pallas_call(
        body, name="norm_fwd",
        grid=(S // tm,),
        in_specs=[pl.BlockSpec((tm, D), lambda i: (i, 0)), pl.BlockSpec((1, D), lambda i: (0, 0))],
        out_specs=pl.BlockSpec((tm, D), lambda i: (i, 0)),
        out_shape=jax.ShapeDtypeStruct((S, D), BF),
        compiler_params=_params(1),
    )(x, g)


def _mix_in_bwd(dz, wcat, dzf, wf, x1, g, dx2, tm):
    S = dz.shape[0]
    nk = ZW // D

    def body(dz_ref, w_ref, dzf_ref, wf_ref, x_ref, g_ref, dx2_ref, dx1_ref, dg_ref, acc):
        i = pl.program_id(0)
        k = pl.program_id(1)

        @pl.when((i == 0) & (k == 0))
        def _():
            dg_ref[...] = jnp.zeros_like(dg_ref)

        @pl.when(k == 0)
        def _():
            acc[...] = _dot_nt(dzf_ref[...], wf_ref[...])

        acc[...] += _dot_nt(dz_ref[...], w_ref[...])

        @pl.when(k == nk - 1)
        def _():
            dx, dg = _rms_bwd(acc[...], x_ref[...], g_ref[...])
            dx1_ref[...] = dx2_ref[...] + dx
            dg_ref[...] += dg

    row = pl.BlockSpec((tm, D), lambda i, k: (i, 0))
    vec = pl.BlockSpec((1, D), lambda i, k: (0, 0))
    return pl.pallas_call(
        body, name="mix_in_bwd",
        grid=(S // tm, nk),
        in_specs=[pl.BlockSpec((tm, D), lambda i, k: (i, k)),
                  pl.BlockSpec((D, D), lambda i, k: (0, k)),
                  pl.BlockSpec((tm, HD), lambda i, k: (i, 0)),
                  pl.BlockSpec((D, HD), lambda i, k: (0, 0)),
                  row, vec, row],
        out_specs=[row, vec],
        out_shape=[jax.ShapeDtypeStruct((S, D), F32), jax.ShapeDtypeStruct((1, D), F32)],
        scratch_shapes=[pltpu.VMEM((tm, D), F32)],
        compiler_params=_params(2),
    )(dz, wcat, dzf, wf, x1, g, dx2)


def _scan_rows(blk, reverse):
    n = blk.shape[0]
    row = lax.broadcasted_iota(jnp.int32, blk.shape, 0)
    d = 1
    while d < n:
        if reverse:
            blk = blk + jnp.where(row < n - d, pltpu.roll(blk, n - d, 0), 0.0)
        else:
            blk = blk + jnp.where(row >= d, pltpu.roll(blk, d, 0), 0.0)
        d *= 2
    return blk


def _forget_cumsum(zf, bf, cs):
    S = zf.shape[0]

    def body(zf_ref, bf_ref, c_ref, cb_ref, carry):
        @pl.when(pl.program_id(0) == 0)
        def _():
            carry[...] = jnp.zeros_like(carry)

        x = zf_ref[...] + bf_ref[...]
        logf = jnp.minimum(x, 0.0) - jnp.log1p(jnp.exp(-jnp.abs(x)))
        sc = _scan_rows(logf, False) + carry[...]
        carry[...] = sc[cs - 1:cs, :]
        sc = sc * LOG2E
        c_ref[...] = sc
        for h in range(H):
            cb_ref[h] = jnp.broadcast_to(sc[:, h:h + 1], (cs, HD))

    return pl.pallas_call(
        body, name="forget_cumsum",
        grid=(S // cs,),
        in_specs=[pl.BlockSpec((cs, HD), lambda i: (i, 0)), pl.BlockSpec((1, HD), lambda i: (0, 0))],
        out_specs=[pl.BlockSpec((cs, HD), lambda i: (i, 0)),
                   pl.BlockSpec((H, cs, HD), lambda i: (0, i, 0))],
        out_shape=[jax.ShapeDtypeStruct((S, HD), F32), jax.ShapeDtypeStruct((H, S, HD), F32)],
        scratch_shapes=[pltpu.VMEM((1, HD), F32)],
        compiler_params=_params(1),
    )(zf, bf)


def _forget_bwd(dc, zf, bf, cs):
    S = dc.shape[0]
    nc = S // cs

    def body(dc_ref, zf_ref, bf_ref, dzf_ref, dbf_ref, carry):
        @pl.when(pl.program_id(0) == 0)
        def _():
            carry[...] = jnp.zeros_like(carry)
            dbf_ref[...] = jnp.zeros_like(dbf_ref)

        sc = _scan_rows(dc_ref[...], True) + carry[...]
        carry[...] = sc[0:1, :]
        x = zf_ref[...] + bf_ref[...]
        dz = sc * _sigmoid(-x)
        dzf_ref[...] = dz.astype(BF)
        dbf_ref[...] += jnp.sum(dz, axis=0, keepdims=True)

    rev = pl.BlockSpec((cs, HD), lambda i: (nc - 1 - i, 0))
    vec = pl.BlockSpec((1, HD), lambda i: (0, 0))
    return pl.pallas_call(
        body, name="forget_bwd",
        grid=(nc,),
        in_specs=[rev, rev, vec],
        out_specs=[rev, vec],
        out_shape=[jax.ShapeDtypeStruct((S, HD), BF), jax.ShapeDtypeStruct((1, HD), F32)],
        scratch_shapes=[pltpu.VMEM((1, HD), F32)],
        compiler_params=_params(1),
    )(dc, zf, bf)


def _lanes(x, n):
    return x if n == HD else jnp.concatenate([x] * (n // HD), axis=1)


def _causal_mask(i, j, t, rows_are_queries):
    r = lax.broadcasted_iota(jnp.int32, (t, t), 0)
    c = lax.broadcasted_iota(jnp.int32, (t, t), 1)
    if rows_are_queries:
        return (j * t + c) <= (i * t + r)
    return (j * t + r) <= (i * t + c)


def _fox_fwd(z, ccol_b, crow, t):
    S = z.shape[0]
    nq = S // t

    def body(q_ref, kv_ref, cc_ref, cr_ref, o_ref, lse_ref, lser_ref, m_s, acc_s):
        i = pl.program_id(1)
        ct = cc_ref[...]
        ones = jnp.ones((t, HD), BF)
        m_s[...] = jnp.full_like(m_s, NEG)
        acc_s[...] = jnp.zeros_like(acc_s)
        n_sub = 2 if t >= 256 else 1
        ts = t // n_sub

        def step(j, masked):
            off = pl.multiple_of(j * t, t)
            k = kv_ref[pl.ds(off, t), :HD]
            v1 = jnp.concatenate([kv_ref[pl.ds(off, t), HD:], ones], axis=1)
            cs = cr_ref[pl.ds(j, 1), :]
            mask = _causal_mask(i, j, t, True) if masked else None
            for r in range(n_sub):
                rows = slice(r * ts, (r + 1) * ts)
                s = _dot_nt(q_ref[rows, :], k) - cs
                if masked:
                    s = jnp.where(mask[rows, :], s, NEG)
                m_old = m_s[rows, :]
                m_new = jnp.maximum(m_old, jnp.max(s, axis=1, keepdims=True))
                p = jnp.exp2(s - _lanes(m_new, t))
                alpha = jnp.exp2(m_old - m_new)
                acc_s[rows, :] = _lanes(alpha, 2 * HD) * acc_s[rows, :] + _dot(p.astype(BF), v1)
                m_s[rows, :] = m_new

        def full_step(j, carry):
            step(j, False)
            return carry

        lax.fori_loop(0, i, full_step, 0)
        step(i, True)
        l = acc_s[:, HD:]
        o_ref[...] = (acc_s[:, :HD] / l).astype(BF)
        lse = m_s[...] + ct + jnp.log2(l)
        lse_ref[...] = lse
        lser_ref[...] = jnp.transpose(lse)[0:1, :]

    return pl.pallas_call(
        body, name="fox_fwd",
        grid=(H, nq),
        in_specs=[pl.BlockSpec((t, HD), lambda h, i: (i, h)),
                  pl.BlockSpec((S, 2 * HD), lambda h, i: (0, 4 + h)),
                  pl.BlockSpec((None, t, HD), lambda h, i: (h, i, 0)),
                  pl.BlockSpec((None, nq, t), lambda h, i: (h, 0, 0))],
        out_specs=[pl.BlockSpec((t, HD), lambda h, i: (i, h)),
                   pl.BlockSpec((None, t, HD), lambda h, i: (h, i, 0)),
                   pl.BlockSpec((None, None, 1, t), lambda h, i: (h, i, 0, 0))],
        out_shape=[jax.ShapeDtypeStruct((S, D), BF), jax.ShapeDtypeStruct((H, S, HD), F32),
                   jax.ShapeDtypeStruct((H, nq, 1, t), F32)],
        scratch_shapes=[pltpu.VMEM((t, HD), F32), pltpu.VMEM((t, 2 * HD), F32)],
        compiler_params=_params(2),
    )(z, z, ccol_b, crow)


def _fox_bwd_dq(z, do, o, lse_b, ccol_b, crow, dz, t):
    S = z.shape[0]
    nq = S // t

    def body(q_ref, kv_ref, do_ref, o_ref, lse_ref, cc_ref, cr_ref, dz_in, dq_ref, deltar_ref, dck_ref, acc_s):
        del dz_in
        i = pl.program_id(1)

        @pl.when(i == 0)
        def _():
            dck_ref[...] = jnp.zeros_like(dck_ref)

        q = q_ref[...]
        dout = do_ref[...]
        off_t = _lanes(lse_ref[...] - cc_ref[...], t)
        delta = jnp.sum(dout.astype(F32) * o_ref[...].astype(F32), axis=1, keepdims=True)
        delta = jnp.broadcast_to(delta, (t, HD))
        deltar_ref[...] = jnp.transpose(delta)[0:1, :]
        delta = _lanes(delta, t)
        acc_s[...] = jnp.zeros_like(acc_s)

        def step(j, masked):
            off = pl.multiple_of(j * t, t)
            k = kv_ref[pl.ds(off, t), :HD]
            v = kv_ref[pl.ds(off, t), HD:]
            p = jnp.exp2(_dot_nt(q, k) - cr_ref[pl.ds(j, 1), :] - off_t)
            if masked:
                p = jnp.where(_causal_mask(i, j, t, True), p, 0.0)
            ds = p * (_dot_nt(dout, v) - delta)
            acc_s[...] += _dot(ds.astype(BF), k)
            dck_ref[pl.ds(j, 1), :] += jnp.sum(ds, axis=0, keepdims=True)

        def full_step(j, carry):
            step(j, False)
            return carry

        lax.fori_loop(0, i, full_step, 0)
        step(i, True)
        dq_ref[...] = (acc_s[...] * SCALE).astype(BF)

    qspec = pl.BlockSpec((t, HD), lambda h, i: (i, h))
    bspec = pl.BlockSpec((None, t, HD), lambda h, i: (h, i, 0))
    rows = pl.BlockSpec((None, nq, t), lambda h, i: (h, 0, 0))
    return pl.pallas_call(
        body, name="fox_bwd_dq",
        grid=(H, nq),
        in_specs=[qspec,
                  pl.BlockSpec((S, 2 * HD), lambda h, i: (0, 4 + h)),
                  qspec, qspec, bspec, bspec, rows,
                  pl.BlockSpec(memory_space=pl.ANY)],
        out_specs=[qspec, pl.BlockSpec((None, None, 1, t), lambda h, i: (h, i, 0, 0)), rows],
        out_shape=[jax.ShapeDtypeStruct((S, ZW), BF), jax.ShapeDtypeStruct((H, nq, 1, t), F32),
                   jax.ShapeDtypeStruct((H, nq, t), F32)],
        scratch_shapes=[pltpu.VMEM((t, HD), F32)],
        input_output_aliases={7: 0},
        compiler_params=_params(2),
    )(z, z, do, o, lse_b, ccol_b, crow, dz)


def _fox_bwd_dkv(z, do, lse_r, delta_r, ccol_b, crow, dz, t):
    S = z.shape[0]
    nq = S // t

    def body(q_ref, kv_ref, do_ref, lse_ref, delta_ref, cc_ref, cr_ref, dz_in, dkv_ref, dcq_ref, dk_s, dv_s):
        del dz_in
        j = pl.program_id(1)

        @pl.when(j == 0)
        def _():
            dcq_ref[...] = jnp.zeros_like(dcq_ref)

        k = kv_ref[:, :HD]
        v = kv_ref[:, HD:]
        cs = _lanes(cc_ref[...], t)
        dk_s[...] = jnp.zeros_like(dk_s)
        dv_s[...] = jnp.zeros_like(dv_s)

        def step(i, masked):
            off = pl.multiple_of(i * t, t)
            q = q_ref[pl.ds(off, t), :]
            dout = do_ref[pl.ds(off, t), :]
            pt = jnp.exp2(_dot_nt(k, q) + (cr_ref[pl.ds(i, 1), :] - lse_ref[i]) - cs)
            if masked:
                pt = jnp.where(_causal_mask(i, j, t, False), pt, 0.0)
            dv_s[...] += _dot(pt.astype(BF), dout)
            dst = pt * (_dot_nt(v, dout) - delta_ref[i])
            dk_s[...] += _dot(dst.astype(BF), q)
            dcq_ref[pl.ds(i, 1), :] += jnp.sum(dst, axis=0, keepdims=True)

        step(j, True)

        def full_step(i, carry):
            step(i, False)
            return carry

        lax.fori_loop(j + 1, nq, full_step, 0)
        dkv_ref[:, :HD] = (dk_s[...] * LN2).astype(BF)
        dkv_ref[:, HD:] = dv_s[...].astype(BF)

    whole = pl.BlockSpec((S, HD), lambda h, j: (0, h))
    rows = pl.BlockSpec((None, nq, t), lambda h, j: (h, 0, 0))
    rows4 = pl.BlockSpec((None, nq, 1, t), lambda h, j: (h, 0, 0, 0))
    return pl.pallas_call(
        body, name="fox_bwd_dkv",
        grid=(H, nq),
        in_specs=[whole,
                  pl.BlockSpec((t, 2 * HD), lambda h, j: (j, 4 + h)),
                  whole, rows4, rows4,
                  pl.BlockSpec((None, t, HD), lambda h, j: (h, j, 0)),
                  rows,
                  pl.BlockSpec(memory_space=pl.ANY)],
        out_specs=[pl.BlockSpec((t, 2 * HD), lambda h, j: (j, 4 + h)), rows],
        out_shape=[jax.ShapeDtypeStruct((S, ZW), BF), jax.ShapeDtypeStruct((H, nq, t), F32)],
        scratch_shapes=[pltpu.VMEM((t, HD), F32), pltpu.VMEM((t, HD), F32)],
        input_output_aliases={7: 0},
        compiler_params=_params(2),
    )(z, z, do, lse_r, delta_r, ccol_b, crow, dz)


def _sgu_mask(transposed):
    r = lax.broadcasted_iota(jnp.int32, (L, L), 0)
    c = lax.broadcasted_iota(jnp.int32, (L, L), 1)
    if transposed:
        return (r // CHUNK) <= (c // CHUNK)
    return (c // CHUNK) <= (r // CHUNK)


def _ln_group(vs, lng, lnb):
    mu = jnp.mean(vs, axis=-1, keepdims=True)
    xc = vs - mu
    rstd = lax.rsqrt(jnp.mean(xc * xc, axis=-1, keepdims=True) + LN_EPS)
    xhat = xc * rstd
    return xhat, rstd, xhat * lng + lnb


def _mix_out_fwd(z, oa, ln_g, ln_b, ws, bst, wout, x1, g_post, tm):
    S = z.shape[0]
    nw = tm // L

    def body(u_ref, sv_ref, ga_ref, gb_ref, oa_ref, lng_ref, lnb_ref, ws_ref, bst_ref, wout_ref, x1_ref, gp_ref,
             mg_ref, y_ref, x2_ref, mg_s):
        mask = _sgu_mask(False)
        for g in range(G):
            cols = slice(g * L, (g + 1) * L)
            wm = jnp.where(mask, ws_ref[g], 0.0).astype(BF)
            bcol = bst_ref[:, g:g + 1]
            lng = lng_ref[:, cols]
            lnb = lnb_ref[:, cols]
            for w in range(nw):
                rows = slice(w * L, (w + 1) * L)
                vs = _gelu(sv_ref[rows, cols].astype(F32))
                _, _, vn = _ln_group(vs, lng, lnb)
                mixed = _dot(wm, vn.astype(BF)) + bcol
                ob = _gelu(u_ref[rows, cols].astype(F32)) * mixed
                mg = (_sigmoid(ga_ref[rows, cols].astype(F32)) * oa_ref[rows, cols].astype(F32)
                      + _sigmoid(gb_ref[rows, cols].astype(F32)) * ob)
                mg_s[rows, cols] = mg.astype(BF)
        mgb = mg_s[...]
        mg_ref[...] = mgb
        y = _dot(mgb, wout_ref[...])
        y_ref[...] = y
        x2_ref[...] = x1_ref[...] + _rms(y, gp_ref[...])

    row = pl.BlockSpec((tm, D), lambda i: (i, 0))
    vec = pl.BlockSpec((1, D), lambda i: (0, 0))

    def zcol(kb):
        return pl.BlockSpec((tm, D), lambda i: (i, kb))

    return pl.pallas_call(
        body, name="mix_out_fwd",
        grid=(S // tm,),
        in_specs=[zcol(3), zcol(4), zcol(5), zcol(6), row, vec, vec,
                  pl.BlockSpec((G, L, L), lambda i: (0, 0, 0)),
                  pl.BlockSpec((L, HD), lambda i: (0, 0)),
                  pl.BlockSpec((D, D), lambda i: (0, 0)),
                  row, vec],
        out_specs=[row, row, row],
        out_shape=[jax.ShapeDtypeStruct((S, D), BF),
                   jax.ShapeDtypeStruct((S, D), F32),
                   jax.ShapeDtypeStruct((S, D), F32)],
        scratch_shapes=[pltpu.VMEM((tm, D), BF)],
        compiler_params=_params(1),
    )(z, z, z, z, oa, ln_g, ln_b, ws, bst, wout, x1, g_post)


def _mix_out_bwd(dx2, y2, g_post, wout, z, oa, ln_g, ln_b, ws, wst, bst, tm):
    S = z.shape[0]
    nw = tm // L

    def body(dx2_ref, y_ref, gp_ref, wout_ref, u_ref, sv_ref, ga_ref, gb_ref, oa_ref, lng_ref, lnb_ref,
             ws_ref, wst_ref, bst_ref,
             dz_ref, dy_ref, doa_ref, dgp_ref, dlng_ref, dlnb_ref, dws_ref, dbst_ref, dzg_s, dm_s):
        i = pl.program_id(0)
        c = pl.program_id(1)

        @pl.when((i == 0) & (c == 0))
        def _():
            dgp_ref[...] = jnp.zeros_like(dgp_ref)
            dlng_ref[...] = jnp.zeros_like(dlng_ref)
            dlnb_ref[...] = jnp.zeros_like(dlnb_ref)
            dws_ref[...] = jnp.zeros_like(dws_ref)
            dbst_ref[...] = jnp.zeros_like(dbst_ref)

        @pl.when(c == 0)
        def _():
            dy, dg = _rms_bwd(dx2_ref[...], y_ref[...], gp_ref[...])
            dyb = dy.astype(BF)
            dy_ref[...] = dyb
            dgp_ref[...] += dg
            dm_s[...] = _dot_nt(dyb, wout_ref[...])
            mask = _sgu_mask(False)
            mask_t = _sgu_mask(True)
            lane = lax.broadcasted_iota(jnp.int32, (L, HD), 1)
            for g in range(G):
                cols = slice(g * L, (g + 1) * L)
                wm = jnp.where(mask, ws_ref[g], 0.0).astype(BF)
                wmt = jnp.where(mask_t, wst_ref[g], 0.0).astype(BF)
                bcol = bst_ref[:, g:g + 1]
                lng = lng_ref[:, cols]
                lnb = lnb_ref[:, cols]
                dws_g = jnp.zeros((L, L), F32)
                dbs_g = jnp.zeros((L, 1), F32)
                dlng_g = jnp.zeros((1, L), F32)
                dlnb_g = jnp.zeros((1, L), F32)
                for w in range(nw):
                    rows = slice(w * L, (w + 1) * L)
                    dm = dm_s[rows, cols]
                    vs, dvs_dz = _gelu_parts(sv_ref[rows, cols].astype(F32))
                    xhat, rstd, vn = _ln_group(vs, lng, lnb)
                    vnb = vn.astype(BF)
                    mixed = _dot(wm, vnb) + bcol
                    u, du_dz = _gelu_parts(u_ref[rows, cols].astype(F32))
                    sga = _sigmoid(ga_ref[rows, cols].astype(F32))
                    sgb = _sigmoid(gb_ref[rows, cols].astype(F32))
                    oav = oa_ref[rows, cols].astype(F32)
                    ob = u * mixed
                    doa_ref[rows, cols] = (dm * sga).astype(BF)
                    dzg_s[2, rows, cols] = (dm * oav * sga * (1.0 - sga)).astype(BF)
                    dzg_s[3, rows, cols] = (dm * ob * sgb * (1.0 - sgb)).astype(BF)
                    dob = dm * sgb
                    dzg_s[0, rows, cols] = (dob * mixed * du_dz).astype(BF)
                    dmixed = dob * u
                    dmb = dmixed.astype(BF)
                    dbs_g += jnp.sum(dmixed, axis=1, keepdims=True)
                    dws_g += _dot_nt(dmb, vnb)
                    dvn = _dot(wmt, dmb)
                    dlng_g += jnp.sum(dvn * xhat, axis=0, keepdims=True)
                    dlnb_g += jnp.sum(dvn, axis=0, keepdims=True)
                    dxh = dvn * lng
                    dvs = rstd * (dxh - jnp.mean(dxh, axis=-1, keepdims=True)
                                  - xhat * jnp.mean(dxh * xhat, axis=-1, keepdims=True))
                    dzg_s[1, rows, cols] = (dvs * dvs_dz).astype(BF)
                dws_ref[g] += jnp.where(mask, dws_g, 0.0)
                dbst_ref[...] += jnp.where(lane == g, dbs_g, 0.0)
                dlng_ref[:, cols] += dlng_g
                dlnb_ref[:, cols] += dlnb_g

        dz_ref[...] = dzg_s[c]

    row = pl.BlockSpec((tm, D), lambda i, c: (i, 0))
    vec = pl.BlockSpec((1, D), lambda i, c: (0, 0))
    wsspec = pl.BlockSpec((G, L, L), lambda i, c: (0, 0, 0))
    bspec = pl.BlockSpec((L, HD), lambda i, c: (0, 0))

    def zcol(kb):
        return pl.BlockSpec((tm, D), lambda i, c: (i, kb))

    return pl.pallas_call(
        body, name="mix_out_bwd",
        grid=(S // tm, 4),
        in_specs=[row, row, vec, pl.BlockSpec((D, D), lambda i, c: (0, 0)),
                  zcol(3), zcol(4), zcol(5), zcol(6), row, vec, vec, wsspec, wsspec, bspec],
        out_specs=[pl.BlockSpec((tm, D), lambda i, c: (i, 3 + c)),
                   row, row, vec, vec, vec, wsspec, bspec],
        out_shape=[jax.ShapeDtypeStruct((S, ZW), BF),
                   jax.ShapeDtypeStruct((S, D), BF),
                   jax.ShapeDtypeStruct((S, D), BF),
                   jax.ShapeDtypeStruct((1, D), F32),
                   jax.ShapeDtypeStruct((1, D), F32),
                   jax.ShapeDtypeStruct((1, D), F32),
                   jax.ShapeDtypeStruct((G, L, L), F32),
                   jax.ShapeDtypeStruct((L, HD), F32)],
        scratch_shapes=[pltpu.VMEM((4, tm, D), BF), pltpu.VMEM((tm, D), F32)],
        compiler_params=_params(2),
    )(dx2, y2, g_post, wout, z, z, z, z, oa, ln_g, ln_b, ws, wst, bst)


def _loss_head(x3, target, tm):
    S = x3.shape[0]

    def body(x_ref, t_ref, dx_ref, loss_ref):
        @pl.when(pl.program_id(0) == 0)
        def _():
            loss_ref[...] = jnp.zeros_like(loss_ref)

        e = x_ref[...] - t_ref[...]
        dx_ref[...] = e * (1.0 / D)
        loss_ref[...] += jnp.sum(e * e) * (0.5 / D)

    row = pl.BlockSpec((tm, D), lambda i: (i, 0))
    return pl.pallas_call(
        body, name="loss_head",
        grid=(S // tm,),
        in_specs=[row, row],
        out_specs=[row, pl.BlockSpec((8, HD), lambda i: (0, 0))],
        out_shape=[jax.ShapeDtypeStruct((S, D), F32), jax.ShapeDtypeStruct((8, HD), F32)],
        compiler_params=_params(1),
    )(x3, target)


def _adamw(w, g, m, v, tr):
    R, C = w.shape

    def body(w_ref, g_ref, m_ref, v_ref, d_ref, nm_ref, nv_ref):
        gv = g_ref[...]
        m_new = ADAM_B1 * m_ref[...] + (1.0 - ADAM_B1) * gv
        v_new = ADAM_B2 * v_ref[...] + (1.0 - ADAM_B2) * (gv * gv)
        m_hat = m_new / (1.0 - ADAM_B1 ** ADAM_STEP)
        v_hat = v_new / (1.0 - ADAM_B2 ** ADAM_STEP)
        d_ref[...] = -ADAM_LR * (m_hat / (jnp.sqrt(v_hat) + ADAM_EPS) + ADAM_WD * w_ref[...])
        nm_ref[...] = m_new
        nv_ref[...] = v_new

    spec = pl.BlockSpec((tr, C), lambda i: (i, 0))
    shp = jax.ShapeDtypeStruct((R, C), F32)
    return pl.pallas_call(
        body, name="adamw",
        grid=(R // tr,),
        in_specs=[spec] * 4, out_specs=[spec] * 3, out_shape=[shp] * 3,
        compiler_params=_params(1),
    )(w, g, m, v)


def _mesh_pos():
    return lax.axis_index("x"), lax.axis_index("y"), lax.axis_index("c")


def _half(c):
    return pl.ds(pl.multiple_of(c * HALF_ROWS, 16), HALF_ROWS)


def _all_gather_weights(wp):
    def body(wp_ref, g_ref, send_sems, recv_sems):
        x, y, c = _mesh_pos()
        chips = [(1 - x, y), (x, 1 - y), (1 - x, 1 - y)]
        sibling = (x, y, 1 - c)
        j_me = 2 * x + y
        mine, other = _half(c), _half(1 - c)

        def copy(k, src, dst, to):
            return pltpu.make_async_remote_copy(src_ref=src, dst_ref=dst, send_sem=send_sems.at[k],
                                                recv_sem=recv_sems.at[k], device_id=to, device_id_type=MESH)

        first = [copy(k, wp_ref.at[mine], g_ref.at[j_me, mine], (px, py, c)) for k, (px, py) in enumerate(chips)]
        for cp in first:
            cp.start()
        passed = []
        for k, (px, py) in enumerate(chips):
            land = g_ref.at[2 * px + py, mine]
            copy(k, land, land, (px, py, c)).wait_recv()
            fwd = copy(3 + k, land, land, sibling)
            fwd.start()
            passed.append(fwd)
        for k, (px, py) in enumerate(chips):
            land = g_ref.at[2 * px + py, other]
            copy(3 + k, land, land, sibling).wait_recv()
        for cp in first + passed:
            cp.wait_send()

    g = pl.pallas_call(
        body, name="all_gather_weights",
        in_specs=[pl.BlockSpec(memory_space=pl.ANY)],
        out_specs=pl.BlockSpec(memory_space=pl.ANY),
        out_shape=jax.ShapeDtypeStruct((NSH, PACK_ROWS, D), BF),
        scratch_shapes=[pltpu.SemaphoreType.DMA((6,)), pltpu.SemaphoreType.DMA((6,))],
        compiler_params=pltpu.CompilerParams(has_side_effects=True),
    )(wp)
    x, y, _ = _mesh_pos()
    return lax.dynamic_update_index_in_dim(g, wp, 2 * x + y, 0)


def _pair_exchange(p):
    def body(p_ref, r_ref, send_sem, recv_sem):
        x, y, c = _mesh_pos()
        cp = pltpu.make_async_remote_copy(src_ref=p_ref.at[:, _half(1 - c)], dst_ref=r_ref, send_sem=send_sem,
                                          recv_sem=recv_sem, device_id=(x, y, 1 - c), device_id_type=MESH)
        cp.start()
        cp.wait()

    return pl.pallas_call(
        body, name="pair_exchange",
        in_specs=[pl.BlockSpec(memory_space=pl.ANY)],
        out_specs=pl.BlockSpec(memory_space=pl.ANY),
        out_shape=jax.ShapeDtypeStruct((NSH, HALF_ROWS, D), F32),
        scratch_shapes=[pltpu.SemaphoreType.DMA, pltpu.SemaphoreType.DMA],
        compiler_params=pltpu.CompilerParams(has_side_effects=True),
    )(p)


def _pair_add(p, r, tr):
    nb = HALF_ROWS // tr

    def body(p_ref, r_ref, q_ref):
        q_ref[...] = (p_ref[...] + r_ref[...]).astype(BF)

    return pl.pallas_call(
        body, name="pair_add", grid=(NSH, nb),
        in_specs=[pl.BlockSpec((None, tr, D), lambda j, i: (j, lax.axis_index("c") * nb + i, 0)),
                  pl.BlockSpec((None, tr, D), lambda j, i: (j, i, 0))],
        out_specs=pl.BlockSpec((None, tr, D), lambda j, i: (j, i, 0)),
        out_shape=jax.ShapeDtypeStruct((NSH, HALF_ROWS, D), BF),
        compiler_params=_params(2),
    )(p, r)


def _chip_exchange(q, small):
    def body(q_ref, s_ref, t_ref, sm_ref, send_sems, recv_sems, ssend_sems, srecv_sems):
        x, y, c = _mesh_pos()
        me = 4 * x + 2 * y + c
        chips = [(1 - x, y), (x, 1 - y), (1 - x, 1 - y)]
        big = []
        for k, (px, py) in enumerate(chips):
            cp = pltpu.make_async_remote_copy(src_ref=q_ref.at[2 * px + py], dst_ref=t_ref.at[k],
                                              send_sem=send_sems.at[k], recv_sem=recv_sems.at[k],
                                              device_id=(px, py, c), device_id_type=MESH)
            cp.start()
            big.append(cp)
        flips = [(fx, fy, fc) for fx in (0, 1) for fy in (0, 1) for fc in (0, 1)][1:]
        smalls = []
        for k, (fx, fy, fc) in enumerate(flips):
            cp = pltpu.make_async_remote_copy(src_ref=s_ref, dst_ref=sm_ref.at[me],
                                              send_sem=ssend_sems.at[k], recv_sem=srecv_sems.at[k],
                                              device_id=(x ^ fx, y ^ fy, c ^ fc), device_id_type=MESH)
            cp.start()
            smalls.append(cp)
        for cp in big + smalls:
            cp.wait()

    t, sm = pl.pallas_call(
        body, name="chip_exchange",
        in_specs=[pl.BlockSpec(memory_space=pl.ANY), pl.BlockSpec(memory_space=pl.ANY)],
        out_specs=[pl.BlockSpec(memory_space=pl.ANY), pl.BlockSpec(memory_space=pl.ANY)],
        out_shape=[jax.ShapeDtypeStruct((3, HALF_ROWS, D), BF),
                   jax.ShapeDtypeStruct((NDEV, SMALL_ROWS, D), F32)],
        scratch_shapes=[pltpu.SemaphoreType.DMA((3,)), pltpu.SemaphoreType.DMA((3,)),
                        pltpu.SemaphoreType.DMA((7,)), pltpu.SemaphoreType.DMA((7,))],
        compiler_params=pltpu.CompilerParams(has_side_effects=True),
    )(q, small)
    x, y, c = _mesh_pos()
    return t, lax.dynamic_update_index_in_dim(sm, small, 4 * x + 2 * y + c, 0)


def _shard_sum(p, r, t, tr):
    nb = HALF_ROWS // tr

    def shard():
        return 2 * lax.axis_index("x") + lax.axis_index("y")

    def body(p_ref, r_ref, t_ref, o_ref):
        s = p_ref[...] + r_ref[...]
        for k in range(3):
            s = s + t_ref[k].astype(F32)
        o_ref[...] = s

    return pl.pallas_call(
        body, name="shard_sum", grid=(nb,),
        in_specs=[pl.BlockSpec((None, tr, D), lambda i: (shard(), lax.axis_index("c") * nb + i, 0)),
                  pl.BlockSpec((None, tr, D), lambda i: (shard(), i, 0)),
                  pl.BlockSpec((3, tr, D), lambda i: (0, i, 0))],
        out_specs=pl.BlockSpec((tr, D), lambda i: (i, 0)),
        out_shape=jax.ShapeDtypeStruct((HALF_ROWS, D), F32),
        compiler_params=_params(1),
    )(p, r, t)


def _small_sum(sm):
    def body(sm_ref, o_ref):
        s = sm_ref[0]
        for k in range(1, NDEV):
            s = s + sm_ref[k]
        o_ref[...] = s

    return pl.pallas_call(
        body, name="small_sum",
        in_specs=[pl.BlockSpec(memory_space=pltpu.VMEM)],
        out_specs=pl.BlockSpec(memory_space=pltpu.VMEM),
        out_shape=jax.ShapeDtypeStruct((SMALL_ROWS, D), F32),
    )(sm)


def _pair_gather(gh):
    def body(gh_ref, o_ref, send_sem, recv_sem):
        x, y, c = _mesh_pos()
        cp = pltpu.make_async_remote_copy(src_ref=gh_ref, dst_ref=o_ref.at[_half(c)], send_sem=send_sem,
                                          recv_sem=recv_sem, device_id=(x, y, 1 - c), device_id_type=MESH)
        cp.start()
        cp.wait_send()
        pltpu.make_async_remote_copy(src_ref=gh_ref, dst_ref=o_ref.at[_half(1 - c)], send_sem=send_sem,
                                     recv_sem=recv_sem, device_id=(x, y, 1 - c), device_id_type=MESH).wait_recv()

    o = pl.pallas_call(
        body, name="pair_gather",
        in_specs=[pl.BlockSpec(memory_space=pl.ANY)],
        out_specs=pl.BlockSpec(memory_space=pl.ANY),
        out_shape=jax.ShapeDtypeStruct((PACK_ROWS, D), F32),
        scratch_shapes=[pltpu.SemaphoreType.DMA, pltpu.SemaphoreType.DMA],
        compiler_params=pltpu.CompilerParams(has_side_effects=True),
    )(gh)
    return lax.dynamic_update_slice_in_dim(o, gh, lax.axis_index("c") * HALF_ROWS, 0)


def _pad_cols(a, n):
    return jnp.pad(a, ((0, 0), (0, n - a.shape[1])))


def _split_w_in(w_in_full):
    q, k, v = w_in_full[:, :D], w_in_full[:, D:2 * D], w_in_full[:, 2 * D:3 * D]
    f = w_in_full[:, 3 * D:3 * D + H]
    gates = w_in_full[:, 3 * D + H:]
    kv = jnp.stack([k.reshape(D, H, HD), v.reshape(D, H, HD)], axis=2).reshape(D, 2 * D)
    return jnp.concatenate([q, kv, gates], axis=1), _pad_cols(f, HD)


def _merge_w_in_grad(dwcat, dwf):
    kv = dwcat[:, D:3 * D].reshape(D, H, 2, HD)
    return jnp.concatenate([dwcat[:, :D], kv[:, :, 0].reshape(D, D), kv[:, :, 1].reshape(D, D),
                            dwf[:, :H], dwcat[:, 3 * D:]], axis=1)


def _local_step(x, target, wpack, small, tm, t_attn):
    S = x.shape[0]
    g1pre, g1post = small["ffn1_pre_g"], small["ffn1_post_g"]
    gmpre, gmpost = small["mix_pre_g"], small["mix_post_g"]
    g2pre, g2post = small["ffn2_pre_g"], small["ffn2_post_g"]
    ln_g, ln_b = small["sgu_ln_g"], small["sgu_ln_b"]
    ws = small["sgu_w_s"][0]
    wst = jnp.swapaxes(ws, 1, 2)
    bst = _pad_cols(small["sgu_b_s"][0].T, HD)
    bf = _pad_cols(small["b_forget"], HD)

    wout = wpack[:, ROW_WOUT:ROW_WOUT + 256, :].reshape(D, D)
    w_in_full = jnp.transpose(wpack[:, ROW_WIN:ROW_WIN + WIN_SH, :].reshape(NSH, D, WIN_SH), (1, 0, 2)).reshape(D, NSH * WIN_SH)
    wcat, wf = _split_w_in(w_in_full)

    h1, a1, b1, y1, x1 = _ffn_fwd(x, g1pre, wpack, ROW_FFN1, g1post, tm)
    h2 = _norm_fwd(x1, gmpre, tm)
    z = _mm(h2, wcat, tm, D, BF, first_block_scale=SCALE * LOG2E)
    zf = _mm(h2, wf, tm, HD, F32)
    cs = min(512, S)
    c, ccol_b = _forget_cumsum(zf, bf, cs)
    crow = jnp.transpose(c[:, :H]).reshape(H, S // t_attn, t_attn)
    oa, lse_b, lse_r = _fox_fwd(z, ccol_b, crow, t_attn)
    merged, y2, x2 = _mix_out_fwd(z, oa, ln_g, ln_b, ws, bst, wout, x1, gmpost, tm)
    h3, a3, b3, y3, x3 = _ffn_fwd(x2, g2pre, wpack, ROW_FFN2, g2post, tm)
    dx3, loss_acc = _loss_head(x3, target, tm)
    loss = loss_acc[0, 0]

    dy3, da3, db3, act3, dx2, dg2post, dg2pre = _ffn_bwd(dx3, y3, g2post, a3, b3, wpack, ROW_FFN2, x2, g2pre, tm)
    bt = min(512, S)
    dwg2 = _mm_tn(h3, da3, D, D, bt, col_blocked=True)
    dwu2 = _mm_tn(h3, db3, D, D, bt, col_blocked=True)
    dwd2 = _mm_tn(act3, dy3, D, D, bt).reshape(NSH, D, D)

    dz, dy2, doa, dgmpost, dlng, dlnb, dws, dbst = _mix_out_bwd(dx2, y2, gmpost, wout, z, oa, ln_g, ln_b, ws, wst, bst, tm)
    dwout = _mm_tn(merged, dy2, D, D, bt).reshape(NSH, 256, D)
    dz, delta_r, dc_keys = _fox_bwd_dq(z, doa, oa, lse_b, ccol_b, crow, dz, t_attn)
    dz, dc_queries = _fox_bwd_dkv(z, doa, lse_r, delta_r, ccol_b, crow, dz, t_attn)
    dc = _pad_cols(jnp.transpose((dc_queries - dc_keys).reshape(H, S)), HD)
    dzf, dbf = _forget_bwd(dc, zf, bf, cs)
    dwcat = _mm_tn(h2, dz, D, D, bt)
    dwf = _mm_tn(h2, dzf, D, HD, bt)
    dx1, dgmpre = _mix_in_bwd(dz, wcat, dzf, wf, x1, gmpre, dx2, tm)
    dwin = _merge_w_in_grad(dwcat, dwf)
    dwin = jnp.transpose(dwin.reshape(D, NSH, WIN_SH), (1, 0, 2)).reshape(NSH, WIN_SH, D)

    dy1, da1, db1, act1, dx, dg1post, dg1pre = _ffn_bwd(dx1, y1, g1post, a1, b1, wpack, ROW_FFN1, x, g1pre, tm)
    dwg1 = _mm_tn(h1, da1, D, D, bt, col_blocked=True)
    dwu1 = _mm_tn(h1, db1, D, D, bt, col_blocked=True)
    dwd1 = _mm_tn(act1, dy1, D, D, bt).reshape(NSH, D, D)

    dwin = jnp.pad(dwin, ((0, 0), (0, PACK_ROWS - ROW_WIN - WIN_SH), (0, 0)))
    gpack = jnp.concatenate([dwg1, dwu1, dwd1, dwg2, dwu2, dwd2, dwout, dwin], axis=1)
    gsmall = {
        "ffn1_pre_g": dg1pre, "ffn1_post_g": dg1post, "mix_pre_g": dgmpre, "mix_post_g": dgmpost,
        "ffn2_pre_g": dg2pre, "ffn2_post_g": dg2post, "sgu_ln_g": dlng, "sgu_ln_b": dlnb,
        "sgu_w_s": dws[None], "sgu_b_s": jnp.transpose(dbst[:, :G])[None], "b_forget": dbf[:, :H],
    }
    return loss, dx, gpack, gsmall


_SMALL_NAMES = ["ffn1_pre_g", "ffn1_post_g", "mix_pre_g", "mix_post_g", "ffn2_pre_g", "ffn2_post_g",
                "sgu_ln_g", "sgu_ln_b", "sgu_b_s", "b_forget", "sgu_w_s"]
_SMALL_SHAPES = {"sgu_b_s": (1, G, L), "b_forget": (1, H), "sgu_w_s": (1, G, L, L)}


def _pack_small(d):
    rows = []
    for n in _SMALL_NAMES:
        a = d[n].astype(F32)
        if n == "b_forget":
            a = _pad_cols(a, D)
        a = a.reshape(-1, D)
        rows.append(jnp.pad(a, ((0, -a.shape[0] % SMALL_STRIDE), (0, 0))))
    return jnp.concatenate(rows, axis=0)


def _unpack_small(p):
    out, r = {}, 0
    for n in _SMALL_NAMES:
        if n == "sgu_w_s":
            out[n] = p[r:r + L].reshape(1, G, L, L)
            r += L
        elif n == "b_forget":
            out[n] = p[r:r + 1, :H]
            r += SMALL_STRIDE
        elif n == "sgu_b_s":
            out[n] = p[r:r + 1].reshape(1, G, L)
            r += SMALL_STRIDE
        else:
            out[n] = p[r:r + 1]
            r += SMALL_STRIDE
    return out


_BIG_NAMES = ["ffn1_w_gate", "ffn1_w_up", "ffn1_w_down", "ffn2_w_gate", "ffn2_w_up", "ffn2_w_down", "w_out", "w_in"]
_WEIGHT_ORDER = ['ffn1_pre_g', 'ffn1_w_gate', 'ffn1_w_up', 'ffn1_w_down', 'ffn1_post_g', 'mix_pre_g', 'w_in', 'b_forget',
                 'sgu_ln_g', 'sgu_ln_b', 'sgu_w_s', 'sgu_b_s', 'w_out', 'mix_post_g', 'ffn2_pre_g', 'ffn2_w_gate',
                 'ffn2_w_up', 'ffn2_w_down', 'ffn2_post_g']


def _pack_big(w):
    parts = [w[n][0] for n in _BIG_NAMES[:7]]
    parts.append(jnp.pad(w["w_in"][0].reshape(WIN_SH, D), ((0, PACK_ROWS - ROW_WIN - WIN_SH), (0, 0))))
    return jnp.concatenate(parts, axis=0)


def _unpack_big(p):
    out = {}
    for k, n in enumerate(_BIG_NAMES[:6]):
        out[n] = p[k * D:(k + 1) * D][None]
    out["w_out"] = p[ROW_WOUT:ROW_WOUT + 256][None]
    out["w_in"] = p[ROW_WIN:ROW_WIN + WIN_SH].reshape(1, D, WIN_SH)
    return out


def _step(args, tm, t_attn):
    x = args["x"][0]
    target = args["loss_target"][0]
    weights = {n: args[n] for n in _WEIGHT_ORDER}
    small = {n: weights[n] for n in _SMALL_NAMES}

    wpack = _all_gather_weights(_pack_big({n: weights[n].astype(BF) for n in _BIG_NAMES}))
    loss_local, dx, gpack, gsmall = _local_step(x, target, wpack, small, tm, t_attn)
    loss = lax.psum(loss_local, ("x", "y", "c"))

    tr = HALF_ROWS // 13
    r = _pair_exchange(gpack)
    q = _pair_add(gpack, r, tr)
    t, sm = _chip_exchange(q, _pack_small(gsmall))
    gh = _shard_sum(gpack, r, t, tr)
    gbig = _unpack_big(_pair_gather(gh))
    gsm = _small_sum(sm)
    grads = {**gbig, **_unpack_small(gsm)}

    delta, new_m, new_v = {}, {}, {}
    for n in _BIG_NAMES:
        shp = weights[n].shape
        w2 = weights[n].reshape(-1, shp[-1])
        rows = w2.shape[0]
        d, nm, nv = _adamw(w2, grads[n].reshape(w2.shape), args["m_" + n].reshape(w2.shape),
                           args["v_" + n].reshape(w2.shape), rows // 4)
        delta[n], new_m[n], new_v[n] = d.reshape(shp), nm.reshape(shp), nv.reshape(shp)
    ds, nms, nvs = _adamw(_pack_small(small), gsm, _pack_small({n: args["m_" + n] for n in _SMALL_NAMES}),
                          _pack_small({n: args["v_" + n] for n in _SMALL_NAMES}), SMALL_ROWS)
    delta.update(_unpack_small(ds))
    new_m.update(_unpack_small(nms))
    new_v.update(_unpack_small(nvs))

    return (loss, dx[None], *[grads[n] for n in _WEIGHT_ORDER], *[delta[n] for n in _WEIGHT_ORDER],
            *[new_m[n] for n in _WEIGHT_ORDER], *[new_v[n] for n in _WEIGHT_ORDER])


_ARG_NAMES = (["x"] + _WEIGHT_ORDER + ["loss_target"] + ["m_" + n for n in _WEIGHT_ORDER]
              + ["v_" + n for n in _WEIGHT_ORDER])


def kernel(x, ffn1_pre_g, ffn1_w_gate, ffn1_w_up, ffn1_w_down, ffn1_post_g, mix_pre_g, w_in, b_forget, sgu_ln_g, sgu_ln_b, sgu_w_s, sgu_b_s, w_out, mix_post_g, ffn2_pre_g, ffn2_w_gate, ffn2_w_up, ffn2_w_down, ffn2_post_g, loss_target, m_ffn1_pre_g, m_ffn1_w_gate, m_ffn1_w_up, m_ffn1_w_down, m_ffn1_post_g, m_mix_pre_g, m_w_in, m_b_forget, m_sgu_ln_g, m_sgu_ln_b, m_sgu_w_s, m_sgu_b_s, m_w_out, m_mix_post_g, m_ffn2_pre_g, m_ffn2_w_gate, m_ffn2_w_up, m_ffn2_w_down, m_ffn2_post_g, v_ffn1_pre_g, v_ffn1_w_gate, v_ffn1_w_up, v_ffn1_w_down, v_ffn1_post_g, v_mix_pre_g, v_w_in, v_b_forget, v_sgu_ln_g, v_sgu_ln_b, v_sgu_w_s, v_sgu_b_s, v_w_out, v_mix_post_g, v_ffn2_pre_g, v_ffn2_w_gate, v_ffn2_w_up, v_ffn2_w_down, v_ffn2_post_g):
    args = (x, ffn1_pre_g, ffn1_w_gate, ffn1_w_up, ffn1_w_down, ffn1_post_g, mix_pre_g, w_in, b_forget, sgu_ln_g, sgu_ln_b, sgu_w_s, sgu_b_s, w_out, mix_post_g, ffn2_pre_g, ffn2_w_gate, ffn2_w_up, ffn2_w_down, ffn2_post_g, loss_target, m_ffn1_pre_g, m_ffn1_w_gate, m_ffn1_w_up, m_ffn1_w_down, m_ffn1_post_g, m_mix_pre_g, m_w_in, m_b_forget, m_sgu_ln_g, m_sgu_ln_b, m_sgu_w_s, m_sgu_b_s, m_w_out, m_mix_post_g, m_ffn2_pre_g, m_ffn2_w_gate, m_ffn2_w_up, m_ffn2_w_down, m_ffn2_post_g, v_ffn1_pre_g, v_ffn1_w_gate, v_ffn1_w_up, v_ffn1_w_down, v_ffn1_post_g, v_mix_pre_g, v_w_in, v_b_forget, v_sgu_ln_g, v_sgu_ln_b, v_sgu_w_s, v_sgu_b_s, v_w_out, v_mix_post_g, v_ffn2_pre_g, v_ffn2_w_gate, v_ffn2_w_up, v_ffn2_w_down, v_ffn2_post_g)
    named = dict(zip(_ARG_NAMES, args))
    tile = min(512, x.shape[1])
    return _step(named, tile, tile)
```

```python
import functools
import math

import jax
import jax.numpy as jnp
from jax import lax
from jax.experimental import pallas as pl
from jax.experimental.pallas import tpu as pltpu

D = 1024
F = 4096
H = 8
HD = 128
G = 8
L = 128
CHUNK = 64
NSH = 4
NDEV = 8
ZW = 7 * D
RMS_EPS = 1e-6
LN_EPS = 1e-5
NEG = -1e30
SCALE = 1.0 / math.sqrt(HD)
LOG2E = math.log2(math.e)
LN2 = math.log(2.0)

ADAM_LR = 0.001
ADAM_B1 = 0.9
ADAM_B2 = 0.999
ADAM_EPS = 1e-08
ADAM_WD = 0.01
ADAM_STEP = 10

VMEM_LIMIT_BYTES = 56 * 1024 * 1024

ROW_FFN1 = 0
ROW_FFN2 = 3 * D
ROW_WOUT = 6 * D
ROW_WIN = 6 * D + 256
WIN_SH = 1794
PACK_ROWS = 8320
HALF_ROWS = PACK_ROWS // 2
SMALL_STRIDE = 8
SMALL_ROWS = 10 * SMALL_STRIDE + L

BF = jnp.bfloat16
F32 = jnp.float32
MESH = pl.DeviceIdType.MESH


def _params(n_grid):
    return pltpu.CompilerParams(dimension_semantics=("arbitrary",) * n_grid,
                                vmem_limit_bytes=VMEM_LIMIT_BYTES)


def _dot(a, b):
    return jnp.dot(a, b, preferred_element_type=F32)


def _dot_nt(a, b):
    return lax.dot_general(a, b, (((1,), (1,)), ((), ())), preferred_element_type=F32)


def _dot_tn(a, b):
    return lax.dot_general(a, b, (((0,), (0,)), ((), ())), preferred_element_type=F32)


def _rms(x, g):
    r = lax.rsqrt(jnp.mean(x * x, axis=-1, keepdims=True) + RMS_EPS)
    return x * r * g


def _rms_bwd(dn, x, g):
    r = lax.rsqrt(jnp.mean(x * x, axis=-1, keepdims=True) + RMS_EPS)
    xr = x * r
    dg = jnp.sum(dn * xr, axis=0, keepdims=True)
    t = dn * g
    dx = r * (t - xr * jnp.mean(t * xr, axis=-1, keepdims=True))
    return dx, dg


def _gelu_parts(x):
    cdf = 0.5 * (1.0 + lax.erf(x * (1.0 / math.sqrt(2.0))))
    pdf = jnp.exp(-0.5 * x * x) * (1.0 / math.sqrt(2.0 * math.pi))
    return x * cdf, cdf + x * pdf


def _gelu(x):
    return x * (0.5 * (1.0 + lax.erf(x * (1.0 / math.sqrt(2.0)))))


def _sigmoid(x):
    return 1.0 / (1.0 + jnp.exp(-x))


def _ffn_fwd(x, g_pre, wpack, row0, g_post, tm):
    S = x.shape[0]
    nt, nf, tf = S // tm, NSH, D
    rb = row0 // D

    def body(x_ref, gpre_ref, wg_ref, wu_ref, wd_ref, gpost_ref,
             h_ref, a_ref, b_ref, y_ref, xo_ref, h_s, acc):
        j = pl.program_id(1)

        @pl.when(j == 0)
        def _():
            h = _rms(x_ref[...], gpre_ref[...]).astype(BF)
            h_s[...] = h
            h_ref[...] = h
            acc[...] = jnp.zeros_like(acc)

        h = h_s[...]
        a = _dot(h, wg_ref[...])
        b = _dot(h, wu_ref[...])
        a_ref[...] = a.astype(BF)
        b_ref[...] = b.astype(BF)
        act = (a * _sigmoid(a)) * b
        acc[...] += _dot(act.astype(BF), wd_ref[...])

        @pl.when(j == nf - 1)
        def _():
            y = acc[...]
            y_ref[...] = y
            xo_ref[...] = x_ref[...] + 0.5 * _rms(y, gpost_ref[...])

    row = pl.BlockSpec((tm, D), lambda i, j: (i, 0))
    vec = pl.BlockSpec((1, D), lambda i, j: (0, 0))
    return pl.pallas_call(
        body, name="ffn_fwd",
        grid=(nt, nf),
        in_specs=[row, vec,
                  pl.BlockSpec((None, D, tf), lambda i, j: (j, rb, 0)),
                  pl.BlockSpec((None, D, tf), lambda i, j: (j, rb + 1, 0)),
                  pl.BlockSpec((None, tf, D), lambda i, j: (j, rb + 2, 0)),
                  vec],
        out_specs=[row,
                   pl.BlockSpec((tm, tf), lambda i, j: (i, j)),
                   pl.BlockSpec((tm, tf), lambda i, j: (i, j)),
                   row, row],
        out_shape=[jax.ShapeDtypeStruct((S, D), BF),
                   jax.ShapeDtypeStruct((S, F), BF),
                   jax.ShapeDtypeStruct((S, F), BF),
                   jax.ShapeDtypeStruct((S, D), F32),
                   jax.ShapeDtypeStruct((S, D), F32)],
        scratch_shapes=[pltpu.VMEM((tm, D), BF), pltpu.VMEM((tm, D), F32)],
        compiler_params=_params(2),
    )(x, g_pre, wpack, wpack, wpack, g_post)


def _ffn_bwd(dxo, y, g_post, a, b, wpack, row0, x_in, g_pre, tm):
    S = dxo.shape[0]
    nt, nf, tf = S // tm, NSH, D
    rb = row0 // D

    def body(dxo_ref, y_ref, gpost_ref, a_ref, b_ref, wg_ref, wu_ref, wd_ref, xin_ref, gpre_ref,
             dy_ref, da_ref, db_ref, act_ref, dxin_ref, dgpost_ref, dgpre_ref, dy_s, acc):
        i = pl.program_id(0)
        j = pl.program_id(1)

        @pl.when((i == 0) & (j == 0))
        def _():
            dgpost_ref[...] = jnp.zeros_like(dgpost_ref)
            dgpre_ref[...] = jnp.zeros_like(dgpre_ref)

        @pl.when(j == 0)
        def _():
            dy, dg = _rms_bwd(0.5 * dxo_ref[...], y_ref[...], gpost_ref[...])
            dyb = dy.astype(BF)
            dy_s[...] = dyb
            dy_ref[...] = dyb
            dgpost_ref[...] += dg
            acc[...] = jnp.zeros_like(acc)

        dact = _dot_nt(dy_s[...], wd_ref[...])
        av = a_ref[...].astype(F32)
        bv = b_ref[...].astype(F32)
        sig = _sigmoid(av)
        sl = av * sig
        act_ref[...] = (sl * bv).astype(BF)
        dbb = (dact * sl).astype(BF)
        dab = (dact * bv * (sig * (1.0 + av * (1.0 - sig)))).astype(BF)
        da_ref[...] = dab
        db_ref[...] = dbb
        acc[...] += _dot_nt(dab, wg_ref[...]) + _dot_nt(dbb, wu_ref[...])

        @pl.when(j == nf - 1)
        def _():
            dx, dg = _rms_bwd(acc[...], xin_ref[...], gpre_ref[...])
            dxin_ref[...] = dxo_ref[...] + dx
            dgpre_ref[...] += dg

    row = pl.BlockSpec((tm, D), lambda i, j: (i, 0))
    vec = pl.BlockSpec((1, D), lambda i, j: (0, 0))
    ff = pl.BlockSpec((tm, tf), lambda i, j: (i, j))
    return pl.pallas_call(
        body, name="ffn_bwd",
        grid=(nt, nf),
        in_specs=[row, row, vec, ff, ff,
                  pl.BlockSpec((None, D, tf), lambda i, j: (j, rb, 0)),
                  pl.BlockSpec((None, D, tf), lambda i, j: (j, rb + 1, 0)),
                  pl.BlockSpec((None, tf, D), lambda i, j: (j, rb + 2, 0)),
                  row, vec],
        out_specs=[row, ff, ff, ff, row, vec, vec],
        out_shape=[jax.ShapeDtypeStruct((S, D), BF),
                   jax.ShapeDtypeStruct((S, F), BF),
                   jax.ShapeDtypeStruct((S, F), BF),
                   jax.ShapeDtypeStruct((S, F), BF),
                   jax.ShapeDtypeStruct((S, D), F32),
                   jax.ShapeDtypeStruct((1, D), F32),
                   jax.ShapeDtypeStruct((1, D), F32)],
        scratch_shapes=[pltpu.VMEM((tm, D), BF), pltpu.VMEM((tm, D), F32)],
        compiler_params=_params(2),
    )(dxo, y, g_post, a, b, wpack, wpack, wpack, x_in, g_pre)


def _mm_tn(a, b, bm, bn, bt, col_blocked=False):
    S, M = a.shape
    N = b.shape[1]
    nt = S // bt

    def body(a_ref, b_ref, o_ref):
        t = pl.program_id(2)

        @pl.when(t == 0)
        def _():
            o_ref[...] = jnp.zeros_like(o_ref)

        o_ref[...] += _dot_tn(a_ref[...], b_ref[...])

    if col_blocked:
        out_spec = pl.BlockSpec((None, bm, bn), lambda m, n, t: (n, m, 0))
        out_shape = jax.ShapeDtypeStruct((N // bn, M, bn), F32)
    else:
        out_spec = pl.BlockSpec((bm, bn), lambda m, n, t: (m, n))
        out_shape = jax.ShapeDtypeStruct((M, N), F32)
    return pl.pallas_call(
        body, name="mm_tn",
        grid=(M // bm, N // bn, nt),
        in_specs=[pl.BlockSpec((bt, bm), lambda m, n, t: (t, m)),
                  pl.BlockSpec((bt, bn), lambda m, n, t: (t, n))],
        out_specs=out_spec, out_shape=out_shape,
        compiler_params=_params(3),
    )(a, b)


def _mm(a, w, tm, tn, out_dtype, first_block_scale=1.0):
    S, K = a.shape
    N = w.shape[1]

    def body(a_ref, w_ref, o_ref):
        r = _dot(a_ref[...], w_ref[...])
        if first_block_scale != 1.0:
            r = r * jnp.where(pl.program_id(1) == 0, first_block_scale, 1.0)
        o_ref[...] = r.astype(out_dtype)

    return pl.pallas_call(
        body, name="mm",
        grid=(S // tm, N // tn),
        in_specs=[pl.BlockSpec((tm, K), lambda i, j: (i, 0)),
                  pl.BlockSpec((K, tn), lambda i, j: (0, j))],
        out_specs=pl.BlockSpec((tm, tn), lambda i, j: (i, j)),
        out_shape=jax.ShapeDtypeStruct((S, N), out_dtype),
        compiler_params=_params(2),
    )(a, w)


def _norm_fwd(x, g, tm):
    S = x.shape[0]

    def body(x_ref, g_ref, h_ref):
        h_ref[...] = _rms(x_ref[...], g_ref[...]).astype(BF)

    return pl.pallas_call(
        body, name="norm_fwd",
        grid=(S // tm,),
        in_specs=[pl.BlockSpec((tm, D), lambda i: (i, 0)), pl.BlockSpec((1, D), lambda i: (0, 0))],
        out_specs=pl.BlockSpec((tm, D), lambda i: (i, 0)),
        out_shape=jax.ShapeDtypeStruct((S, D), BF),
        compiler_params=_params(1),
    )(x, g)


def _mix_in_bwd(dz, wcat, dzf, wf, x1, g, dx2, tm):
    S = dz.shape[0]
    nk = ZW // D

    def body(dz_ref, w_ref, dzf_ref, wf_ref, x_ref, g_ref, dx2_ref, dx1_ref, dg_ref, acc):
        i = pl.program_id(0)
        k = pl.program_id(1)

        @pl.when((i == 0) & (k == 0))
        def _():
            dg_ref[...] = jnp.zeros_like(dg_ref)

        @pl.when(k == 0)
        def _():
            acc[...] = _dot_nt(dzf_ref[...], wf_ref[...])

        acc[...] += _dot_nt(dz_ref[...], w_ref[...])

        @pl.when(k == nk - 1)
        def _():
            dx, dg = _rms_bwd(acc[...], x_ref[...], g_ref[...])
            dx1_ref[...] = dx2_ref[...] + dx
            dg_ref[...] += dg

    row = pl.BlockSpec((tm, D), lambda i, k: (i, 0))
    vec = pl.BlockSpec((1, D), lambda i, k: (0, 0))
    return pl.pallas_call(
        body, name="mix_in_bwd",
        grid=(S // tm, nk),
        in_specs=[pl.BlockSpec((tm, D), lambda i, k: (i, k)),
                  pl.BlockSpec((D, D), lambda i, k: (0, k)),
                  pl.BlockSpec((tm, HD), lambda i, k: (i, 0)),
                  pl.BlockSpec((D, HD), lambda i, k: (0, 0)),
                  row, vec, row],
        out_specs=[row, vec],
        out_shape=[jax.ShapeDtypeStruct((S, D), F32), jax.ShapeDtypeStruct((1, D), F32)],
        scratch_shapes=[pltpu.VMEM((tm, D), F32)],
        compiler_params=_params(2),
    )(dz, wcat, dzf, wf, x1, g, dx2)


def _scan_rows(blk, reverse):
    n = blk.shape[0]
    row = lax.broadcasted_iota(jnp.int32, blk.shape, 0)
    d = 1
    while d < n:
        if reverse:
            blk = blk + jnp.where(row < n - d, pltpu.roll(blk, n - d, 0), 0.0)
        else:
            blk = blk + jnp.where(row >= d, pltpu.roll(blk, d, 0), 0.0)
        d *= 2
    return blk


def _forget_cumsum(zf, bf, cs):
    S = zf.shape[0]

    def body(zf_ref, bf_ref, c_ref, cb_ref, carry):
        @pl.when(pl.program_id(0) == 0)
        def _():
            carry[...] = jnp.zeros_like(carry)

        x = zf_ref[...] + bf_ref[...]
        logf = jnp.minimum(x, 0.0) - jnp.log1p(jnp.exp(-jnp.abs(x)))
        sc = _scan_rows(logf, False) + carry[...]
        carry[...] = sc[cs - 1:cs, :]
        sc = sc * LOG2E
        c_ref[...] = sc
        for h in range(H):
            cb_ref[h] = jnp.broadcast_to(sc[:, h:h + 1], (cs, HD))

    return pl.pallas_call(
        body, name="forget_cumsum",
        grid=(S // cs,),
        in_specs=[pl.BlockSpec((cs, HD), lambda i: (i, 0)), pl.BlockSpec((1, HD), lambda i: (0, 0))],
        out_specs=[pl.BlockSpec((cs, HD), lambda i: (i, 0)),
                   pl.BlockSpec((H, cs, HD), lambda i: (0, i, 0))],
        out_shape=[jax.ShapeDtypeStruct((S, HD), F32), jax.ShapeDtypeStruct((H, S, HD), F32)],
        scratch_shapes=[pltpu.VMEM((1, HD), F32)],
        compiler_params=_params(1),
    )(zf, bf)


def _forget_bwd(dc, zf, bf, cs):
    S = dc.shape[0]
    nc = S // cs

    def body(dc_ref, zf_ref, bf_ref, dzf_ref, dbf_ref, carry):
        @pl.when(pl.program_id(0) == 0)
        def _():
            carry[...] = jnp.zeros_like(carry)
            dbf_ref[...] = jnp.zeros_like(dbf_ref)

        sc = _scan_rows(dc_ref[...], True) + carry[...]
        carry[...] = sc[0:1, :]
        x = zf_ref[...] + bf_ref[...]
        dz = sc * _sigmoid(-x)
        dzf_ref[...] = dz.astype(BF)
        dbf_ref[...] += jnp.sum(dz, axis=0, keepdims=True)

    rev = pl.BlockSpec((cs, HD), lambda i: (nc - 1 - i, 0))
    vec = pl.BlockSpec((1, HD), lambda i: (0, 0))
    return pl.pallas_call(
        body, name="forget_bwd",
        grid=(nc,),
        in_specs=[rev, rev, vec],
        out_specs=[rev, vec],
        out_shape=[jax.ShapeDtypeStruct((S, HD), BF), jax.ShapeDtypeStruct((1, HD), F32)],
        scratch_shapes=[pltpu.VMEM((1, HD), F32)],
        compiler_params=_params(1),
    )(dc, zf, bf)


def _lanes(x, n):
    return x if n == HD else jnp.concatenate([x] * (n // HD), axis=1)


def _causal_mask(i, j, t, rows_are_queries):
    r = lax.broadcasted_iota(jnp.int32, (t, t), 0)
    c = lax.broadcasted_iota(jnp.int32, (t, t), 1)
    if rows_are_queries:
        return (j * t + c) <= (i * t + r)
    return (j * t + r) <= (i * t + c)


def _fox_fwd(z, ccol_b, crow, t):
    S = z.shape[0]
    nq = S // t

    def body(q_ref, kv_ref, cc_ref, cr_ref, o_ref, lse_ref, m_s, acc_s):
        i = pl.program_id(1)
        ct = cc_ref[...]
        ones = jnp.ones((t, HD), BF)
        m_s[...] = jnp.full_like(m_s, NEG)
        acc_s[...] = jnp.zeros_like(acc_s)
        n_sub = 2 if t >= 256 else 1
        ts = t // n_sub

        def step(j, masked):
            off = pl.multiple_of(j * t, t)
            k = kv_ref[pl.ds(off, t), :HD]
            v1 = jnp.concatenate([kv_ref[pl.ds(off, t), HD:], ones], axis=1)
            cs = cr_ref[pl.ds(j, 1), :]
            mask = _causal_mask(i, j, t, True) if masked else None
            for r in range(n_sub):
                rows = slice(r * ts, (r + 1) * ts)
                s = _dot_nt(q_ref[rows, :], k) - cs
                if masked:
                    s = jnp.where(mask[rows, :], s, NEG)
                m_old = m_s[rows, :]
                m_new = jnp.maximum(m_old, jnp.max(s, axis=1, keepdims=True))
                p = jnp.exp2(s - _lanes(m_new, t))
                alpha = jnp.exp2(m_old - m_new)
                acc_s[rows, :] = _lanes(alpha, 2 * HD) * acc_s[rows, :] + _dot(p.astype(BF), v1)
                m_s[rows, :] = m_new

        def full_step(j, carry):
            step(j, False)
            return carry

        lax.fori_loop(0, i, full_step, 0)
        step(i, True)
        l = acc_s[:, HD:]
        o_ref[...] = (acc_s[:, :HD] / l).astype(BF)
        lse_ref[...] = m_s[...] + ct + jnp.log2(l)

    return pl.pallas_call(
        body, name="fox_fwd",
        grid=(H, nq),
        in_specs=[pl.BlockSpec((t, HD), lambda h, i: (i, h)),
                  pl.BlockSpec((S, 2 * HD), lambda h, i: (0, 4 + h)),
                  pl.BlockSpec((None, t, HD), lambda h, i: (h, i, 0)),
                  pl.BlockSpec((None, nq, t), lambda h, i: (h, 0, 0))],
        out_specs=[pl.BlockSpec((t, HD), lambda h, i: (i, h)),
                   pl.BlockSpec((None, t, HD), lambda h, i: (h, i, 0))],
        out_shape=[jax.ShapeDtypeStruct((S, D), BF), jax.ShapeDtypeStruct((H, S, HD), F32)],
        scratch_shapes=[pltpu.VMEM((t, HD), F32), pltpu.VMEM((t, 2 * HD), F32)],
        compiler_params=_params(2),
    )(z, z, ccol_b, crow)


def _fox_bwd(z, do, o, lse_b, ccol_b, crow, dz, t):
    S = z.shape[0]
    nq = S // t

    def body(q_ref, kv_ref, do_ref, o_ref, lse_ref, cc_ref, cr_ref, dz_in,
             dq_ref, dkt_ref, dvt_ref, dck_ref, dcq_ref, acc_s, r_s):
        del dz_in
        i = pl.program_id(1)

        @pl.when(i == 0)
        def _():
            dkt_ref[...] = jnp.zeros_like(dkt_ref)
            dvt_ref[...] = jnp.zeros_like(dvt_ref)
            dck_ref[...] = jnp.zeros_like(dck_ref)

        q = q_ref[...]
        dout = do_ref[...]
        qt = jnp.transpose(q.astype(F32)).astype(BF)
        dot_ = jnp.transpose(dout.astype(F32)).astype(BF)
        off_t = _lanes(lse_ref[...] - cc_ref[...], t)
        delta = jnp.sum(dout.astype(F32) * o_ref[...].astype(F32), axis=1, keepdims=True)
        delta = _lanes(jnp.broadcast_to(delta, (t, HD)), t)
        acc_s[...] = jnp.zeros_like(acc_s)
        r_s[...] = jnp.zeros_like(r_s)

        def step(j, masked):
            off = pl.multiple_of(j * t, t)
            k = kv_ref[pl.ds(off, t), :HD]
            v = kv_ref[pl.ds(off, t), HD:]
            p = jnp.exp2(_dot_nt(q, k) - cr_ref[pl.ds(j, 1), :] - off_t)
            if masked:
                p = jnp.where(_causal_mask(i, j, t, True), p, 0.0)
            ds = p * (_dot_nt(dout, v) - delta)
            dsb = ds.astype(BF)
            acc_s[...] += _dot(dsb, k)
            dkt_ref[j] += _dot(qt, dsb)
            dvt_ref[j] += _dot(dot_, p.astype(BF))
            dck_ref[pl.ds(j, 1), :] += jnp.sum(ds, axis=0, keepdims=True)
            r_s[...] += jnp.sum(ds, axis=1, keepdims=True)

        def full_step(j, carry):
            step(j, False)
            return carry

        lax.fori_loop(0, i, full_step, 0)
        step(i, True)
        dq_ref[...] = (acc_s[...] * SCALE).astype(BF)
        dcq_ref[...] = jnp.transpose(r_s[...])[0:1, :]

    qspec = pl.BlockSpec((t, HD), lambda h, i: (i, h))
    bspec = pl.BlockSpec((None, t, HD), lambda h, i: (h, i, 0))
    rows = pl.BlockSpec((None, nq, t), lambda h, i: (h, 0, 0))
    tspec = pl.BlockSpec((None, nq, HD, t), lambda h, i: (h, 0, 0, 0))
    tshape = jax.ShapeDtypeStruct((H, nq, HD, t), F32)
    return pl.pallas_call(
        body, name="fox_bwd",
        grid=(H, nq),
        in_specs=[qspec,
                  pl.BlockSpec((S, 2 * HD), lambda h, i: (0, 4 + h)),
                  qspec, qspec, bspec, bspec, rows,
                  pl.BlockSpec(memory_space=pl.ANY)],
        out_specs=[qspec, tspec, tspec, rows, pl.BlockSpec((None, None, 1, t), lambda h, i: (h, i, 0, 0))],
        out_shape=[jax.ShapeDtypeStruct((S, ZW), BF), tshape, tshape,
                   jax.ShapeDtypeStruct((H, nq, t), F32), jax.ShapeDtypeStruct((H, nq, 1, t), F32)],
        scratch_shapes=[pltpu.VMEM((t, HD), F32), pltpu.VMEM((t, HD), F32)],
        input_output_aliases={7: 0},
        compiler_params=_params(2),
    )(z, z, do, o, lse_b, ccol_b, crow, dz)


def _fox_bwd_finish(dkt, dvt, dz, t):
    nq = dkt.shape[1]
    S = nq * t

    def body(dkt_ref, dvt_ref, dz_in, dkv_ref):
        del dz_in
        dkv_ref[:, :HD] = (jnp.transpose(dkt_ref[...]) * LN2).astype(BF)
        dkv_ref[:, HD:] = jnp.transpose(dvt_ref[...]).astype(BF)

    tspec = pl.BlockSpec((None, None, HD, t), lambda h, j: (h, j, 0, 0))
    return pl.pallas_call(
        body, name="fox_bwd_finish",
        grid=(H, nq),
        in_specs=[tspec, tspec, pl.BlockSpec(memory_space=pl.ANY)],
        out_specs=pl.BlockSpec((t, 2 * HD), lambda h, j: (j, 4 + h)),
        out_shape=jax.ShapeDtypeStruct((S, ZW), BF),
        input_output_aliases={2: 0},
        compiler_params=_params(2),
    )(dkt, dvt, dz)


def _sgu_mask(transposed):
    r = lax.broadcasted_iota(jnp.int32, (L, L), 0)
    c = lax.broadcasted_iota(jnp.int32, (L, L), 1)
    if transposed:
        return (r // CHUNK) <= (c // CHUNK)
    return (c // CHUNK) <= (r // CHUNK)


def _ln_group(vs, lng, lnb):
    mu = jnp.mean(vs, axis=-1, keepdims=True)
    xc = vs - mu
    rstd = lax.rsqrt(jnp.mean(xc * xc, axis=-1, keepdims=True) + LN_EPS)
    xhat = xc * rstd
    return xhat, rstd, xhat * lng + lnb


def _mix_out_fwd(z, oa, ln_g, ln_b, ws, bst, wout, x1, g_post, tm):
    S = z.shape[0]
    nw = tm // L

    def body(u_ref, sv_ref, ga_ref, gb_ref, oa_ref, lng_ref, lnb_ref, ws_ref, bst_ref, wout_ref, x1_ref, gp_ref,
             mg_ref, y_ref, x2_ref, mg_s):
        mask = _sgu_mask(False)
        for g in range(G):
            cols = slice(g * L, (g + 1) * L)
            wm = jnp.where(mask, ws_ref[g], 0.0).astype(BF)
            bcol = bst_ref[:, g:g + 1]
            lng = lng_ref[:, cols]
            lnb = lnb_ref[:, cols]
            for w in range(nw):
                rows = slice(w * L, (w + 1) * L)
                vs = _gelu(sv_ref[rows, cols].astype(F32))
                _, _, vn = _ln_group(vs, lng, lnb)
                mixed = _dot(wm, vn.astype(BF)) + bcol
                ob = _gelu(u_ref[rows, cols].astype(F32)) * mixed
                mg = (_sigmoid(ga_ref[rows, cols].astype(F32)) * oa_ref[rows, cols].astype(F32)
                      + _sigmoid(gb_ref[rows, cols].astype(F32)) * ob)
                mg_s[rows, cols] = mg.astype(BF)
        mgb = mg_s[...]
        mg_ref[...] = mgb
        y = _dot(mgb, wout_ref[...])
        y_ref[...] = y
        x2_ref[...] = x1_ref[...] + _rms(y, gp_ref[...])

    row = pl.BlockSpec((tm, D), lambda i: (i, 0))
    vec = pl.BlockSpec((1, D), lambda i: (0, 0))

    def zcol(kb):
        return pl.BlockSpec((tm, D), lambda i: (i, kb))

    return pl.pallas_call(
        body, name="mix_out_fwd",
        grid=(S // tm,),
        in_specs=[zcol(3), zcol(4), zcol(5), zcol(6), row, vec, vec,
                  pl.BlockSpec((G, L, L), lambda i: (0, 0, 0)),
                  pl.BlockSpec((L, HD), lambda i: (0, 0)),
                  pl.BlockSpec((D, D), lambda i: (0, 0)),
                  row, vec],
        out_specs=[row, row, row],
        out_shape=[jax.ShapeDtypeStruct((S, D), BF),
                   jax.ShapeDtypeStruct((S, D), F32),
                   jax.ShapeDtypeStruct((S, D), F32)],
        scratch_shapes=[pltpu.VMEM((tm, D), BF)],
        compiler_params=_params(1),
    )(z, z, z, z, oa, ln_g, ln_b, ws, bst, wout, x1, g_post)


def _mix_out_bwd(dx2, y2, g_post, wout, z, oa, ln_g, ln_b, ws, wst, bst, tm):
    S = z.shape[0]
    nw = tm // L

    def body(dx2_ref, y_ref, gp_ref, wout_ref, u_ref, sv_ref, ga_ref, gb_ref, oa_ref, lng_ref, lnb_ref,
             ws_ref, wst_ref, bst_ref,
             dz_ref, dy_ref, doa_ref, dgp_ref, dlng_ref, dlnb_ref, dws_ref, dbst_ref, dzg_s, dm_s):
        i = pl.program_id(0)
        c = pl.program_id(1)

        @pl.when((i == 0) & (c == 0))
        def _():
            dgp_ref[...] = jnp.zeros_like(dgp_ref)
            dlng_ref[...] = jnp.zeros_like(dlng_ref)
            dlnb_ref[...] = jnp.zeros_like(dlnb_ref)
            dws_ref[...] = jnp.zeros_like(dws_ref)
            dbst_ref[...] = jnp.zeros_like(dbst_ref)

        @pl.when(c == 0)
        def _():
            dy, dg = _rms_bwd(dx2_ref[...], y_ref[...], gp_ref[...])
            dyb = dy.astype(BF)
            dy_ref[...] = dyb
            dgp_ref[...] += dg
            dm_s[...] = _dot_nt(dyb, wout_ref[...])
            mask = _sgu_mask(False)
            mask_t = _sgu_mask(True)
            lane = lax.broadcasted_iota(jnp.int32, (L, HD), 1)
            for g in range(G):
                cols = slice(g * L, (g + 1) * L)
                wm = jnp.where(mask, ws_ref[g], 0.0).astype(BF)
                wmt = jnp.where(mask_t, wst_ref[g], 0.0).astype(BF)
                bcol = bst_ref[:, g:g + 1]
                lng = lng_ref[:, cols]
                lnb = lnb_ref[:, cols]
                dws_g = jnp.zeros((L, L), F32)
                dbs_g = jnp.zeros((L, 1), F32)
                dlng_g = jnp.zeros((1, L), F32)
                dlnb_g = jnp.zeros((1, L), F32)
                for w in range(nw):
                    rows = slice(w * L, (w + 1) * L)
                    dm = dm_s[rows, cols]
                    vs, dvs_dz = _gelu_parts(sv_ref[rows, cols].astype(F32))
                    xhat, rstd, vn = _ln_group(vs, lng, lnb)
                    vnb = vn.astype(BF)
                    mixed = _dot(wm, vnb) + bcol
                    u, du_dz = _gelu_parts(u_ref[rows, cols].astype(F32))
                    sga = _sigmoid(ga_ref[rows, cols].astype(F32))
                    sgb = _sigmoid(gb_ref[rows, cols].astype(F32))
                    oav = oa_ref[rows, cols].astype(F32)
                    ob = u * mixed
                    doa_ref[rows, cols] = (dm * sga).astype(BF)
                    dzg_s[2, rows, cols] = (dm * oav * sga * (1.0 - sga)).astype(BF)
                    dzg_s[3, rows, cols] = (dm * ob * sgb * (1.0 - sgb)).astype(BF)
                    dob = dm * sgb
                    dzg_s[0, rows, cols] = (dob * mixed * du_dz).astype(BF)
                    dmixed = dob * u
                    dmb = dmixed.astype(BF)
                    dbs_g += jnp.sum(dmixed, axis=1, keepdims=True)
                    dws_g += _dot_nt(dmb, vnb)
                    dvn = _dot(wmt, dmb)
                    dlng_g += jnp.sum(dvn * xhat, axis=0, keepdims=True)
                    dlnb_g += jnp.sum(dvn, axis=0, keepdims=True)
                    dxh = dvn * lng
                    dvs = rstd * (dxh - jnp.mean(dxh, axis=-1, keepdims=True)
                                  - xhat * jnp.mean(dxh * xhat, axis=-1, keepdims=True))
                    dzg_s[1, rows, cols] = (dvs * dvs_dz).astype(BF)
                dws_ref[g] += jnp.where(mask, dws_g, 0.0)
                dbst_ref[...] += jnp.where(lane == g, dbs_g, 0.0)
                dlng_ref[:, cols] += dlng_g
                dlnb_ref[:, cols] += dlnb_g

        dz_ref[...] = dzg_s[c]

    row = pl.BlockSpec((tm, D), lambda i, c: (i, 0))
    vec = pl.BlockSpec((1, D), lambda i, c: (0, 0))
    wsspec = pl.BlockSpec((G, L, L), lambda i, c: (0, 0, 0))
    bspec = pl.BlockSpec((L, HD), lambda i, c: (0, 0))

    def zcol(kb):
        return pl.BlockSpec((tm, D), lambda i, c: (i, kb))

    return pl.pallas_call(
        body, name="mix_out_bwd",
        grid=(S // tm, 4),
        in_specs=[row, row, vec, pl.BlockSpec((D, D), lambda i, c: (0, 0)),
                  zcol(3), zcol(4), zcol(5), zcol(6), row, vec, vec, wsspec, wsspec, bspec],
        out_specs=[pl.BlockSpec((tm, D), lambda i, c: (i, 3 + c)),
                   row, row, vec, vec, vec, wsspec, bspec],
        out_shape=[jax.ShapeDtypeStruct((S, ZW), BF),
                   jax.ShapeDtypeStruct((S, D), BF),
                   jax.ShapeDtypeStruct((S, D), BF),
                   jax.ShapeDtypeStruct((1, D), F32),
                   jax.ShapeDtypeStruct((1, D), F32),
                   jax.ShapeDtypeStruct((1, D), F32),
                   jax.ShapeDtypeStruct((G, L, L), F32),
                   jax.ShapeDtypeStruct((L, HD), F32)],
        scratch_shapes=[pltpu.VMEM((4, tm, D), BF), pltpu.VMEM((tm, D), F32)],
        compiler_params=_params(2),
    )(dx2, y2, g_post, wout, z, z, z, z, oa, ln_g, ln_b, ws, wst, bst)


def _loss_head(x3, target, tm):
    S = x3.shape[0]

    def body(x_ref, t_ref, dx_ref, loss_ref):
        @pl.when(pl.program_id(0) == 0)
        def _():
            loss_ref[...] = jnp.zeros_like(loss_ref)

        e = x_ref[...] - t_ref[...]
        dx_ref[...] = e * (1.0 / D)
        loss_ref[...] += jnp.sum(e * e) * (0.5 / D)

    row = pl.BlockSpec((tm, D), lambda i: (i, 0))
    return pl.pallas_call(
        body, name="loss_head",
        grid=(S // tm,),
        in_specs=[row, row],
        out_specs=[row, pl.BlockSpec((8, HD), lambda i: (0, 0))],
        out_shape=[jax.ShapeDtypeStruct((S, D), F32), jax.ShapeDtypeStruct((8, HD), F32)],
        compiler_params=_params(1),
    )(x3, target)


def _adamw(w, g, m, v, tr):
    R, C = w.shape

    def body(w_ref, g_ref, m_ref, v_ref, d_ref, nm_ref, nv_ref):
        gv = g_ref[...]
        m_new = ADAM_B1 * m_ref[...] + (1.0 - ADAM_B1) * gv
        v_new = ADAM_B2 * v_ref[...] + (1.0 - ADAM_B2) * (gv * gv)
        m_hat = m_new / (1.0 - ADAM_B1 ** ADAM_STEP)
        v_hat = v_new / (1.0 - ADAM_B2 ** ADAM_STEP)
        d_ref[...] = -ADAM_LR * (m_hat / (jnp.sqrt(v_hat) + ADAM_EPS) + ADAM_WD * w_ref[...])
        nm_ref[...] = m_new
        nv_ref[...] = v_new

    spec = pl.BlockSpec((tr, C), lambda i: (i, 0))
    shp = jax.ShapeDtypeStruct((R, C), F32)
    return pl.pallas_call(
        body, name="adamw",
        grid=(R // tr,),
        in_specs=[spec] * 4, out_specs=[spec] * 3, out_shape=[shp] * 3,
        compiler_params=_params(1),
    )(w, g, m, v)


def _mesh_pos():
    return lax.axis_index("x"), lax.axis_index("y"), lax.axis_index("c")


def _half(c):
    return pl.ds(pl.multiple_of(c * HALF_ROWS, 16), HALF_ROWS)


def _all_gather_weights(wp):
    def body(wp_ref, g_ref, send_sems, recv_sems):
        x, y, c = _mesh_pos()
        chips = [(1 - x, y), (x, 1 - y), (1 - x, 1 - y)]
        sibling = (x, y, 1 - c)
        j_me = 2 * x + y
        mine, other = _half(c), _half(1 - c)

        def copy(k, src, dst, to):
            return pltpu.make_async_remote_copy(src_ref=src, dst_ref=dst, send_sem=send_sems.at[k],
                                                recv_sem=recv_sems.at[k], device_id=to, device_id_type=MESH)

        first = [copy(k, wp_ref.at[mine], g_ref.at[j_me, mine], (px, py, c)) for k, (px, py) in enumerate(chips)]
        for cp in first:
            cp.start()
        passed = []
        for k, (px, py) in enumerate(chips):
            land = g_ref.at[2 * px + py, mine]
            copy(k, land, land, (px, py, c)).wait_recv()
            fwd = copy(3 + k, land, land, sibling)
            fwd.start()
            passed.append(fwd)
        for k, (px, py) in enumerate(chips):
            land = g_ref.at[2 * px + py, other]
            copy(3 + k, land, land, sibling).wait_recv()
        for cp in first + passed:
            cp.wait_send()

    g = pl.pallas_call(
        body, name="all_gather_weights",
        in_specs=[pl.BlockSpec(memory_space=pl.ANY)],
        out_specs=pl.BlockSpec(memory_space=pl.ANY),
        out_shape=jax.ShapeDtypeStruct((NSH, PACK_ROWS, D), BF),
        scratch_shapes=[pltpu.SemaphoreType.DMA((6,)), pltpu.SemaphoreType.DMA((6,))],
        compiler_params=pltpu.CompilerParams(has_side_effects=True),
    )(wp)
    x, y, _ = _mesh_pos()
    return lax.dynamic_update_index_in_dim(g, wp, 2 * x + y, 0)


def _pair_exchange(p):
    def body(p_ref, r_ref, send_sem, recv_sem):
        x, y, c = _mesh_pos()
        cp = pltpu.make_async_remote_copy(src_ref=p_ref.at[:, _half(1 - c)], dst_ref=r_ref, send_sem=send_sem,
                                          recv_sem=recv_sem, device_id=(x, y, 1 - c), device_id_type=MESH)
        cp.start()
        cp.wait()

    return pl.pallas_call(
        body, name="pair_exchange",
        in_specs=[pl.BlockSpec(memory_space=pl.ANY)],
        out_specs=pl.BlockSpec(memory_space=pl.ANY),
        out_shape=jax.ShapeDtypeStruct((NSH, HALF_ROWS, D), F32),
        scratch_shapes=[pltpu.SemaphoreType.DMA, pltpu.SemaphoreType.DMA],
        compiler_params=pltpu.CompilerParams(has_side_effects=True),
    )(p)


def _pair_add(p, r, tr):
    nb = HALF_ROWS // tr

    def body(p_ref, r_ref, q_ref):
        q_ref[...] = (p_ref[...] + r_ref[...]).astype(BF)

    return pl.pallas_call(
        body, name="pair_add", grid=(NSH, nb),
        in_specs=[pl.BlockSpec((None, tr, D), lambda j, i: (j, lax.axis_index("c") * nb + i, 0)),
                  pl.BlockSpec((None, tr, D), lambda j, i: (j, i, 0))],
        out_specs=pl.BlockSpec((None, tr, D), lambda j, i: (j, i, 0)),
        out_shape=jax.ShapeDtypeStruct((NSH, HALF_ROWS, D), BF),
        compiler_params=_params(2),
    )(p, r)


def _chip_exchange(q, small):
    def body(q_ref, s_ref, t_ref, sm_ref, send_sems, recv_sems, ssend_sems, srecv_sems):
        x, y, c = _mesh_pos()
        me = 4 * x + 2 * y + c
        chips = [(1 - x, y), (x, 1 - y), (1 - x, 1 - y)]
        big = []
        for k, (px, py) in enumerate(chips):
            cp = pltpu.make_async_remote_copy(src_ref=q_ref.at[2 * px + py], dst_ref=t_ref.at[k],
                                              send_sem=send_sems.at[k], recv_sem=recv_sems.at[k],
                                              device_id=(px, py, c), device_id_type=MESH)
            cp.start()
            big.append(cp)
        flips = [(fx, fy, fc) for fx in (0, 1) for fy in (0, 1) for fc in (0, 1)][1:]
        smalls = []
        for k, (fx, fy, fc) in enumerate(flips):
            cp = pltpu.make_async_remote_copy(src_ref=s_ref, dst_ref=sm_ref.at[me],
                                              send_sem=ssend_sems.at[k], recv_sem=srecv_sems.at[k],
                                              device_id=(x ^ fx, y ^ fy, c ^ fc), device_id_type=MESH)
            cp.start()
            smalls.append(cp)
        for cp in big + smalls:
            cp.wait()

    t, sm = pl.pallas_call(
        body, name="chip_exchange",
        in_specs=[pl.BlockSpec(memory_space=pl.ANY), pl.BlockSpec(memory_space=pl.ANY)],
        out_specs=[pl.BlockSpec(memory_space=pl.ANY), pl.BlockSpec(memory_space=pl.ANY)],
        out_shape=[jax.ShapeDtypeStruct((3, HALF_ROWS, D), BF),
                   jax.ShapeDtypeStruct((NDEV, SMALL_ROWS, D), F32)],
        scratch_shapes=[pltpu.SemaphoreType.DMA((3,)), pltpu.SemaphoreType.DMA((3,)),
                        pltpu.SemaphoreType.DMA((7,)), pltpu.SemaphoreType.DMA((7,))],
        compiler_params=pltpu.CompilerParams(has_side_effects=True),
    )(q, small)
    x, y, c = _mesh_pos()
    return t, lax.dynamic_update_index_in_dim(sm, small, 4 * x + 2 * y + c, 0)


def _shard_sum(p, r, t, tr):
    nb = HALF_ROWS // tr

    def shard():
        return 2 * lax.axis_index("x") + lax.axis_index("y")

    def body(p_ref, r_ref, t_ref, o_ref):
        s = p_ref[...] + r_ref[...]
        for k in range(3):
            s = s + t_ref[k].astype(F32)
        o_ref[...] = s

    return pl.pallas_call(
        body, name="shard_sum", grid=(nb,),
        in_specs=[pl.BlockSpec((None, tr, D), lambda i: (shard(), lax.axis_index("c") * nb + i, 0)),
                  pl.BlockSpec((None, tr, D), lambda i: (shard(), i, 0)),
                  pl.BlockSpec((3, tr, D), lambda i: (0, i, 0))],
        out_specs=pl.BlockSpec((tr, D), lambda i: (i, 0)),
        out_shape=jax.ShapeDtypeStruct((HALF_ROWS, D), F32),
        compiler_params=_params(1),
    )(p, r, t)


def _small_sum(sm):
    def body(sm_ref, o_ref):
        s = sm_ref[0]
        for k in range(1, NDEV):
            s = s + sm_ref[k]
        o_ref[...] = s

    return pl.pallas_call(
        body, name="small_sum",
        in_specs=[pl.BlockSpec(memory_space=pltpu.VMEM)],
        out_specs=pl.BlockSpec(memory_space=pltpu.VMEM),
        out_shape=jax.ShapeDtypeStruct((SMALL_ROWS, D), F32),
    )(sm)


def _pair_gather(gh):
    def body(gh_ref, o_ref, send_sem, recv_sem):
        x, y, c = _mesh_pos()
        cp = pltpu.make_async_remote_copy(src_ref=gh_ref, dst_ref=o_ref.at[_half(c)], send_sem=send_sem,
                                          recv_sem=recv_sem, device_id=(x, y, 1 - c), device_id_type=MESH)
        cp.start()
        cp.wait_send()
        pltpu.make_async_remote_copy(src_ref=gh_ref, dst_ref=o_ref.at[_half(1 - c)], send_sem=send_sem,
                                     recv_sem=recv_sem, device_id=(x, y, 1 - c), device_id_type=MESH).wait_recv()

    o = pl.pallas_call(
        body, name="pair_gather",
        in_specs=[pl.BlockSpec(memory_space=pl.ANY)],
        out_specs=pl.BlockSpec(memory_space=pl.ANY),
        out_shape=jax.ShapeDtypeStruct((PACK_ROWS, D), F32),
        scratch_shapes=[pltpu.SemaphoreType.DMA, pltpu.SemaphoreType.DMA],
        compiler_params=pltpu.CompilerParams(has_side_effects=True),
    )(gh)
    return lax.dynamic_update_slice_in_dim(o, gh, lax.axis_index("c") * HALF_ROWS, 0)


def _pad_cols(a, n):
    return jnp.pad(a, ((0, 0), (0, n - a.shape[1])))


def _split_w_in(w_in_full):
    q, k, v = w_in_full[:, :D], w_in_full[:, D:2 * D], w_in_full[:, 2 * D:3 * D]
    f = w_in_full[:, 3 * D:3 * D + H]
    gates = w_in_full[:, 3 * D + H:]
    kv = jnp.stack([k.reshape(D, H, HD), v.reshape(D, H, HD)], axis=2).reshape(D, 2 * D)
    return jnp.concatenate([q, kv, gates], axis=1), _pad_cols(f, HD)


def _merge_w_in_grad(dwcat, dwf):
    kv = dwcat[:, D:3 * D].reshape(D, H, 2, HD)
    return jnp.concatenate([dwcat[:, :D], kv[:, :, 0].reshape(D, D), kv[:, :, 1].reshape(D, D),
                            dwf[:, :H], dwcat[:, 3 * D:]], axis=1)


def _local_step(x, target, wpack, small, tm, t_attn):
    S = x.shape[0]
    g1pre, g1post = small["ffn1_pre_g"], small["ffn1_post_g"]
    gmpre, gmpost = small["mix_pre_g"], small["mix_post_g"]
    g2pre, g2post = small["ffn2_pre_g"], small["ffn2_post_g"]
    ln_g, ln_b = small["sgu_ln_g"], small["sgu_ln_b"]
    ws = small["sgu_w_s"][0]
    wst = jnp.swapaxes(ws, 1, 2)
    bst = _pad_cols(small["sgu_b_s"][0].T, HD)
    bf = _pad_cols(small["b_forget"], HD)

    wout = wpack[:, ROW_WOUT:ROW_WOUT + 256, :].reshape(D, D)
    w_in_full = jnp.transpose(wpack[:, ROW_WIN:ROW_WIN + WIN_SH, :].reshape(NSH, D, WIN_SH), (1, 0, 2)).reshape(D, NSH * WIN_SH)
    wcat, wf = _split_w_in(w_in_full)

    h1, a1, b1, y1, x1 = _ffn_fwd(x, g1pre, wpack, ROW_FFN1, g1post, tm)
    h2 = _norm_fwd(x1, gmpre, tm)
    z = _mm(h2, wcat, tm, D, BF, first_block_scale=SCALE * LOG2E)
    zf = _mm(h2, wf, tm, HD, F32)
    cs = min(512, S)
    c, ccol_b = _forget_cumsum(zf, bf, cs)
    crow = jnp.transpose(c[:, :H]).reshape(H, S // t_attn, t_attn)
    oa, lse_b = _fox_fwd(z, ccol_b, crow, t_attn)
    merged, y2, x2 = _mix_out_fwd(z, oa, ln_g, ln_b, ws, bst, wout, x1, gmpost, tm)
    h3, a3, b3, y3, x3 = _ffn_fwd(x2, g2pre, wpack, ROW_FFN2, g2post, tm)
    dx3, loss_acc = _loss_head(x3, target, tm)
    loss = loss_acc[0, 0]

    dy3, da3, db3, act3, dx2, dg2post, dg2pre = _ffn_bwd(dx3, y3, g2post, a3, b3, wpack, ROW_FFN2, x2, g2pre, tm)
    bt = min(512, S)
    dwg2 = _mm_tn(h3, da3, D, D, bt, col_blocked=True)
    dwu2 = _mm_tn(h3, db3, D, D, bt, col_blocked=True)
    dwd2 = _mm_tn(act3, dy3, D, D, bt).reshape(NSH, D, D)

    dz, dy2, doa, dgmpost, dlng, dlnb, dws, dbst = _mix_out_bwd(dx2, y2, gmpost, wout, z, oa, ln_g, ln_b, ws, wst, bst, tm)
    dwout = _mm_tn(merged, dy2, D, D, bt).reshape(NSH, 256, D)
    dz, dkt, dvt, dc_keys, dc_queries = _fox_bwd(z, doa, oa, lse_b, ccol_b, crow, dz, t_attn)
    dz = _fox_bwd_finish(dkt, dvt, dz, t_attn)
    dc = _pad_cols(jnp.transpose(dc_queries.reshape(H, S) - dc_keys.reshape(H, S)), HD)
    dzf, dbf = _forget_bwd(dc, zf, bf, cs)
    dwcat = _mm_tn(h2, dz, D, D, bt)
    dwf = _mm_tn(h2, dzf, D, HD, bt)
    dx1, dgmpre = _mix_in_bwd(dz, wcat, dzf, wf, x1, gmpre, dx2, tm)
    dwin = _merge_w_in_grad(dwcat, dwf)
    dwin = jnp.transpose(dwin.reshape(D, NSH, WIN_SH), (1, 0, 2)).reshape(NSH, WIN_SH, D)

    dy1, da1, db1, act1, dx, dg1post, dg1pre = _ffn_bwd(dx1, y1, g1post, a1, b1, wpack, ROW_FFN1, x, g1pre, tm)
    dwg1 = _mm_tn(h1, da1, D, D, bt, col_blocked=True)
    dwu1 = _mm_tn(h1, db1, D, D, bt, col_blocked=True)
    dwd1 = _mm_tn(act1, dy1, D, D, bt).reshape(NSH, D, D)

    dwin = jnp.pad(dwin, ((0, 0), (0, PACK_ROWS - ROW_WIN - WIN_SH), (0, 0)))
    gpack = jnp.concatenate([dwg1, dwu1, dwd1, dwg2, dwu2, dwd2, dwout, dwin], axis=1)
    gsmall = {
        "ffn1_pre_g": dg1pre, "ffn1_post_g": dg1post, "mix_pre_g": dgmpre, "mix_post_g": dgmpost,
        "ffn2_pre_g": dg2pre, "ffn2_post_g": dg2post, "sgu_ln_g": dlng, "sgu_ln_b": dlnb,
        "sgu_w_s": dws[None], "sgu_b_s": jnp.transpose(dbst[:, :G])[None], "b_forget": dbf[:, :H],
    }
    return loss, dx, gpack, gsmall


_SMALL_NAMES = ["ffn1_pre_g", "ffn1_post_g", "mix_pre_g", "mix_post_g", "ffn2_pre_g", "ffn2_post_g",
                "sgu_ln_g", "sgu_ln_b", "sgu_b_s", "b_forget", "sgu_w_s"]
_SMALL_SHAPES = {"sgu_b_s": (1, G, L), "b_forget": (1, H), "sgu_w_s": (1, G, L, L)}


def _pack_small(d):
    rows = []
    for n in _SMALL_NAMES:
        a = d[n].astype(F32)
        if n == "b_forget":
            a = _pad_cols(a, D)
        a = a.reshape(-1, D)
        rows.append(jnp.pad(a, ((0, -a.shape[0] % SMALL_STRIDE), (0, 0))))
    return jnp.concatenate(rows, axis=0)


def _unpack_small(p):
    out, r = {}, 0
    for n in _SMALL_NAMES:
        if n == "sgu_w_s":
            out[n] = p[r:r + L].reshape(1, G, L, L)
            r += L
        elif n == "b_forget":
            out[n] = p[r:r + 1, :H]
            r += SMALL_STRIDE
        elif n == "sgu_b_s":
            out[n] = p[r:r + 1].reshape(1, G, L)
            r += SMALL_STRIDE
        else:
            out[n] = p[r:r + 1]
            r += SMALL_STRIDE
    return out


_BIG_NAMES = ["ffn1_w_gate", "ffn1_w_up", "ffn1_w_down", "ffn2_w_gate", "ffn2_w_up", "ffn2_w_down", "w_out", "w_in"]
_WEIGHT_ORDER = ['ffn1_pre_g', 'ffn1_w_gate', 'ffn1_w_up', 'ffn1_w_down', 'ffn1_post_g', 'mix_pre_g', 'w_in', 'b_forget',
                 'sgu_ln_g', 'sgu_ln_b', 'sgu_w_s', 'sgu_b_s', 'w_out', 'mix_post_g', 'ffn2_pre_g', 'ffn2_w_gate',
                 'ffn2_w_up', 'ffn2_w_down', 'ffn2_post_g']


def _pack_big(w):
    parts = [w[n][0] for n in _BIG_NAMES[:7]]
    parts.append(jnp.pad(w["w_in"][0].reshape(WIN_SH, D), ((0, PACK_ROWS - ROW_WIN - WIN_SH), (0, 0))))
    return jnp.concatenate(parts, axis=0)


def _unpack_big(p):
    out = {}
    for k, n in enumerate(_BIG_NAMES[:6]):
        out[n] = p[k * D:(k + 1) * D][None]
    out["w_out"] = p[ROW_WOUT:ROW_WOUT + 256][None]
    out["w_in"] = p[ROW_WIN:ROW_WIN + WIN_SH].reshape(1, D, WIN_SH)
    return out


def _step(args, tm, t_attn):
    x = args["x"][0]
    target = args["loss_target"][0]
    weights = {n: args[n] for n in _WEIGHT_ORDER}
    small = {n: weights[n] for n in _SMALL_NAMES}

    wpack = _all_gather_weights(_pack_big({n: weights[n].astype(BF) for n in _BIG_NAMES}))
    loss_local, dx, gpack, gsmall = _local_step(x, target, wpack, small, tm, t_attn)
    loss = lax.psum(loss_local, ("x", "y", "c"))

    tr = HALF_ROWS // 13
    r = _pair_exchange(gpack)
    q = _pair_add(gpack, r, tr)
    t, sm = _chip_exchange(q, _pack_small(gsmall))
    gh = _shard_sum(gpack, r, t, tr)
    gbig = _unpack_big(_pair_gather(gh))
    gsm = _small_sum(sm)
    grads = {**gbig, **_unpack_small(gsm)}

    delta, new_m, new_v = {}, {}, {}
    for n in _BIG_NAMES:
        shp = weights[n].shape
        w2 = weights[n].reshape(-1, shp[-1])
        rows = w2.shape[0]
        d, nm, nv = _adamw(w2, grads[n].reshape(w2.shape), args["m_" + n].reshape(w2.shape),
                           args["v_" + n].reshape(w2.shape), rows // 4)
        delta[n], new_m[n], new_v[n] = d.reshape(shp), nm.reshape(shp), nv.reshape(shp)
    ds, nms, nvs = _adamw(_pack_small(small), gsm, _pack_small({n: args["m_" + n] for n in _SMALL_NAMES}),
                          _pack_small({n: args["v_" + n] for n in _SMALL_NAMES}), SMALL_ROWS)
    delta.update(_unpack_small(ds))
    new_m.update(_unpack_small(nms))
    new_v.update(_unpack_small(nvs))

    return (loss, dx[None], *[grads[n] for n in _WEIGHT_ORDER], *[delta[n] for n in _WEIGHT_ORDER],
            *[new_m[n] for n in _WEIGHT_ORDER], *[new_v[n] for n in _WEIGHT_ORDER])


_ARG_NAMES = (["x"] + _WEIGHT_ORDER + ["loss_target"] + ["m_" + n for n in _WEIGHT_ORDER]
              + ["v_" + n for n in _WEIGHT_ORDER])


def kernel(x, ffn1_pre_g, ffn1_w_gate, ffn1_w_up, ffn1_w_down, ffn1_post_g, mix_pre_g, w_in, b_forget, sgu_ln_g, sgu_ln_b, sgu_w_s, sgu_b_s, w_out, mix_post_g, ffn2_pre_g, ffn2_w_gate, ffn2_w_up, ffn2_w_down, ffn2_post_g, loss_target, m_ffn1_pre_g, m_ffn1_w_gate, m_ffn1_w_up, m_ffn1_w_down, m_ffn1_post_g, m_mix_pre_g, m_w_in, m_b_forget, m_sgu_ln_g, m_sgu_ln_b, m_sgu_w_s, m_sgu_b_s, m_w_out, m_mix_post_g, m_ffn2_pre_g, m_ffn2_w_gate, m_ffn2_w_up, m_ffn2_w_down, m_ffn2_post_g, v_ffn1_pre_g, v_ffn1_w_gate, v_ffn1_w_up, v_ffn1_w_down, v_ffn1_post_g, v_mix_pre_g, v_w_in, v_b_forget, v_sgu_ln_g, v_sgu_ln_b, v_sgu_w_s, v_sgu_b_s, v_w_out, v_mix_post_g, v_ffn2_pre_g, v_ffn2_w_gate, v_ffn2_w_up, v_ffn2_w_down, v_ffn2_post_g):
    args = (x, ffn1_pre_g, ffn1_w_gate, ffn1_w_up, ffn1_w_down, ffn1_post_g, mix_pre_g, w_in, b_forget, sgu_ln_g, sgu_ln_b, sgu_w_s, sgu_b_s, w_out, mix_post_g, ffn2_pre_g, ffn2_w_gate, ffn2_w_up, ffn2_w_down, ffn2_post_g, loss_target, m_ffn1_pre_g, m_ffn1_w_gate, m_ffn1_w_up, m_ffn1_w_down, m_ffn1_post_g, m_mix_pre_g, m_w_in, m_b_forget, m_sgu_ln_g, m_sgu_ln_b, m_sgu_w_s, m_sgu_b_s, m_w_out, m_mix_post_g, m_ffn2_pre_g, m_ffn2_w_gate, m_ffn2_w_up, m_ffn2_w_down, m_ffn2_post_g, v_ffn1_pre_g, v_ffn1_w_gate, v_ffn1_w_up, v_ffn1_w_down, v_ffn1_post_g, v_mix_pre_g, v_w_in, v_b_forget, v_sgu_ln_g, v_sgu_ln_b, v_sgu_w_s, v_sgu_b_s, v_w_out, v_mix_post_g, v_ffn2_pre_g, v_ffn2_w_gate, v_ffn2_w_up, v_ffn2_w_down, v_ffn2_post_g)
    named = dict(zip(_ARG_NAMES, args))
    tile = min(512, x.shape[1])
    return _step(named, tile, tile)
```

```python
import functools
import math

import jax
import jax.numpy as jnp
from jax import lax
from jax.experimental import pallas as pl
from jax.experimental.pallas import tpu as pltpu

D = 1024
F = 4096
H = 8
HD = 128
G = 8
L = 128
CHUNK = 64
NSH = 4
NDEV = 8
ZW = 7 * D
RMS_EPS = 1e-6
LN_EPS = 1e-5
NEG = -1e30
SCALE = 1.0 / math.sqrt(HD)
LOG2E = math.log2(math.e)
LN2 = math.log(2.0)

ADAM_LR = 0.001
ADAM_B1 = 0.9
ADAM_B2 = 0.999
ADAM_EPS = 1e-08
ADAM_WD = 0.01
ADAM_STEP = 10

VMEM_LIMIT_BYTES = 56 * 1024 * 1024

WIN_SH = 1794
FFN_ROWS = 3 * D
MIX_ROWS = 256 + 2 * D
G2_ROWS = FFN_ROWS + MIX_ROWS
SMALL_STRIDE = 8
SMALL_ROWS = 10 * SMALL_STRIDE + L

BF = jnp.bfloat16
F32 = jnp.float32
MESH = pl.DeviceIdType.MESH


def _params(n_grid):
    return pltpu.CompilerParams(dimension_semantics=("arbitrary",) * n_grid,
                                vmem_limit_bytes=VMEM_LIMIT_BYTES)


def _dot(a, b):
    return jnp.dot(a, b, preferred_element_type=F32)


def _dot_nt(a, b):
    return lax.dot_general(a, b, (((1,), (1,)), ((), ())), preferred_element_type=F32)


def _dot_tn(a, b):
    return lax.dot_general(a, b, (((0,), (0,)), ((), ())), preferred_element_type=F32)


def _rms(x, g):
    r = lax.rsqrt(jnp.mean(x * x, axis=-1, keepdims=True) + RMS_EPS)
    return x * r * g


def _rms_bwd(dn, x, g):
    r = lax.rsqrt(jnp.mean(x * x, axis=-1, keepdims=True) + RMS_EPS)
    xr = x * r
    dg = jnp.sum(dn * xr, axis=0, keepdims=True)
    t = dn * g
    dx = r * (t - xr * jnp.mean(t * xr, axis=-1, keepdims=True))
    return dx, dg


def _gelu_parts(x):
    cdf = 0.5 * (1.0 + lax.erf(x * (1.0 / math.sqrt(2.0))))
    pdf = jnp.exp(-0.5 * x * x) * (1.0 / math.sqrt(2.0 * math.pi))
    return x * cdf, cdf + x * pdf


def _gelu(x):
    return x * (0.5 * (1.0 + lax.erf(x * (1.0 / math.sqrt(2.0)))))


def _sigmoid(x):
    return 1.0 / (1.0 + jnp.exp(-x))


def _ffn_fwd(x, g_pre, wpack, g_post, tm, gather=None):
    S = x.shape[0]
    nt, nf, tf = S // tm, NSH, D

    def body(x_ref, gpre_ref, wg_ref, wu_ref, wd_ref, gpost_ref, *rest):
        if gather is None:
            h_ref, a_ref, b_ref, y_ref, xo_ref, h_s, acc = rest
        else:
            wp_ref, h_ref, a_ref, b_ref, y_ref, xo_ref, g_ref, h_s, acc, send_sems, recv_sems = rest
        i = pl.program_id(0)
        j = pl.program_id(1)

        if gather is not None:
            @pl.when((i == 0) & (j == 0))
            def _():
                _gather_start(wp_ref, g_ref, send_sems, recv_sems)

        @pl.when(j == 0)
        def _():
            h = _rms(x_ref[...], gpre_ref[...]).astype(BF)
            h_s[...] = h
            h_ref[...] = h
            acc[...] = jnp.zeros_like(acc)

        h = h_s[...]
        a = _dot(h, wg_ref[...])
        b = _dot(h, wu_ref[...])
        a_ref[...] = a.astype(BF)
        b_ref[...] = b.astype(BF)
        act = (a * _sigmoid(a)) * b
        acc[...] += _dot(act.astype(BF), wd_ref[...])

        @pl.when(j == nf - 1)
        def _():
            y = acc[...]
            y_ref[...] = y
            xo_ref[...] = x_ref[...] + 0.5 * _rms(y, gpost_ref[...])

        if gather is not None:
            @pl.when((i == nt - 1) & (j == nf - 1))
            def _():
                _gather_finish(wp_ref, g_ref, send_sems, recv_sems)

    row = pl.BlockSpec((tm, D), lambda i, j: (i, 0))
    vec = pl.BlockSpec((1, D), lambda i, j: (0, 0))
    anywhere = pl.BlockSpec(memory_space=pl.ANY)
    in_specs = [row, vec,
                pl.BlockSpec((None, D, tf), lambda i, j: (j, 0, 0)),
                pl.BlockSpec((None, D, tf), lambda i, j: (j, 1, 0)),
                pl.BlockSpec((None, tf, D), lambda i, j: (j, 2, 0)),
                vec]
    out_specs = [row,
                 pl.BlockSpec((tm, tf), lambda i, j: (i, j)),
                 pl.BlockSpec((tm, tf), lambda i, j: (i, j)),
                 row, row]
    out_shape = [jax.ShapeDtypeStruct((S, D), BF),
                 jax.ShapeDtypeStruct((S, F), BF),
                 jax.ShapeDtypeStruct((S, F), BF),
                 jax.ShapeDtypeStruct((S, D), F32),
                 jax.ShapeDtypeStruct((S, D), F32)]
    scratch = [pltpu.VMEM((tm, D), BF), pltpu.VMEM((tm, D), F32)]
    args = [x, g_pre, wpack, wpack, wpack, g_post]
    if gather is not None:
        in_specs.append(anywhere)
        out_specs.append(anywhere)
        out_shape.append(jax.ShapeDtypeStruct((NSH,) + gather.shape, gather.dtype))
        scratch += [pltpu.SemaphoreType.DMA((6,)), pltpu.SemaphoreType.DMA((6,))]
        args.append(gather)
    res = list(pl.pallas_call(
        body, name="ffn_fwd" if gather is None else "ffn_fwd_gather",
        grid=(nt, nf),
        in_specs=in_specs, out_specs=out_specs, out_shape=out_shape, scratch_shapes=scratch,
        compiler_params=_params(2),
    )(*args))
    if gather is not None:
        res[5] = _place_own_shard(res[5], gather)
    return res


def _ffn_bwd(dxo, y, g_post, a, b, wpack, x_in, g_pre, tm, scatter=None):
    S = dxo.shape[0]
    nt, nf, tf = S // tm, NSH, D

    def body(dxo_ref, y_ref, gpost_ref, a_ref, b_ref, wg_ref, wu_ref, wd_ref, xin_ref, gpre_ref, *rest):
        if scatter is None:
            dy_ref, da_ref, db_ref, act_ref, dxin_ref, dgpost_ref, dgpre_ref, dy_s, acc = rest
        else:
            (q_ref, dy_ref, da_ref, db_ref, act_ref, dxin_ref, dgpost_ref, dgpre_ref, t_ref,
             dy_s, acc, send_sems, recv_sems) = rest
        i = pl.program_id(0)
        j = pl.program_id(1)

        @pl.when((i == 0) & (j == 0))
        def _():
            dgpost_ref[...] = jnp.zeros_like(dgpost_ref)
            dgpre_ref[...] = jnp.zeros_like(dgpre_ref)
            if scatter is not None:
                _scatter_start(q_ref, t_ref, send_sems, recv_sems)

        @pl.when(j == 0)
        def _():
            dy, dg = _rms_bwd(0.5 * dxo_ref[...], y_ref[...], gpost_ref[...])
            dyb = dy.astype(BF)
            dy_s[...] = dyb
            dy_ref[...] = dyb
            dgpost_ref[...] += dg
            acc[...] = jnp.zeros_like(acc)

        dact = _dot_nt(dy_s[...], wd_ref[...])
        av = a_ref[...].astype(F32)
        bv = b_ref[...].astype(F32)
        sig = _sigmoid(av)
        sl = av * sig
        act_ref[...] = (sl * bv).astype(BF)
        dbb = (dact * sl).astype(BF)
        dab = (dact * bv * (sig * (1.0 + av * (1.0 - sig)))).astype(BF)
        da_ref[...] = dab
        db_ref[...] = dbb
        acc[...] += _dot_nt(dab, wg_ref[...]) + _dot_nt(dbb, wu_ref[...])

        @pl.when(j == nf - 1)
        def _():
            dx, dg = _rms_bwd(acc[...], xin_ref[...], gpre_ref[...])
            dxin_ref[...] = dxo_ref[...] + dx
            dgpre_ref[...] += dg

        if scatter is not None:
            @pl.when((i == nt - 1) & (j == nf - 1))
            def _():
                _scatter_finish(q_ref, t_ref, send_sems, recv_sems)

    row = pl.BlockSpec((tm, D), lambda i, j: (i, 0))
    vec = pl.BlockSpec((1, D), lambda i, j: (0, 0))
    ff = pl.BlockSpec((tm, tf), lambda i, j: (i, j))
    anywhere = pl.BlockSpec(memory_space=pl.ANY)
    in_specs = [row, row, vec, ff, ff,
                pl.BlockSpec((None, D, tf), lambda i, j: (j, 0, 0)),
                pl.BlockSpec((None, D, tf), lambda i, j: (j, 1, 0)),
                pl.BlockSpec((None, tf, D), lambda i, j: (j, 2, 0)),
                row, vec]
    out_specs = [row, ff, ff, ff, row, vec, vec]
    out_shape = [jax.ShapeDtypeStruct((S, D), BF),
                 jax.ShapeDtypeStruct((S, F), BF),
                 jax.ShapeDtypeStruct((S, F), BF),
                 jax.ShapeDtypeStruct((S, F), BF),
                 jax.ShapeDtypeStruct((S, D), F32),
                 jax.ShapeDtypeStruct((1, D), F32),
                 jax.ShapeDtypeStruct((1, D), F32)]
    scratch = [pltpu.VMEM((tm, D), BF), pltpu.VMEM((tm, D), F32)]
    args = [dxo, y, g_post, a, b, wpack, wpack, wpack, x_in, g_pre]
    if scatter is not None:
        in_specs.append(anywhere)
        out_specs.append(anywhere)
        out_shape.append(jax.ShapeDtypeStruct((3,) + scatter.shape[1:], scatter.dtype))
        scratch += [pltpu.SemaphoreType.DMA((3,)), pltpu.SemaphoreType.DMA((3,))]
        args.append(scatter)
    return pl.pallas_call(
        body, name="ffn_bwd" if scatter is None else "ffn_bwd_scatter",
        grid=(nt, nf),
        in_specs=in_specs, out_specs=out_specs, out_shape=out_shape, scratch_shapes=scratch,
        compiler_params=_params(2),
    )(*args)


def _mm_tn(a, b, bm, bn, bt, col_blocked=False):
    S, M = a.shape
    N = b.shape[1]
    nt = S // bt

    def body(a_ref, b_ref, o_ref):
        t = pl.program_id(2)

        @pl.when(t == 0)
        def _():
            o_ref[...] = jnp.zeros_like(o_ref)

        o_ref[...] += _dot_tn(a_ref[...], b_ref[...])

    if col_blocked:
        out_spec = pl.BlockSpec((None, bm, bn), lambda m, n, t: (n, m, 0))
        out_shape = jax.ShapeDtypeStruct((N // bn, M, bn), F32)
    else:
        out_spec = pl.BlockSpec((bm, bn), lambda m, n, t: (m, n))
        out_shape = jax.ShapeDtypeStruct((M, N), F32)
    return pl.pallas_call(
        body, name="mm_tn",
        grid=(M // bm, N // bn, nt),
        in_specs=[pl.BlockSpec((bt, bm), lambda m, n, t: (t, m)),
                  pl.BlockSpec((bt, bn), lambda m, n, t: (t, n))],
        out_specs=out_spec, out_shape=out_shape,
        compiler_params=_params(3),
    )(a, b)


def _mm(a, w, tm, tn, out_dtype, first_block_scale=1.0):
    S, K = a.shape
    N = w.shape[1]

    def body(a_ref, w_ref, o_ref):
        r = _dot(a_ref[...], w_ref[...])
        if first_block_scale != 1.0:
            r = r * jnp.where(pl.program_id(1) == 0, first_block_scale, 1.0)
        o_ref[...] = r.astype(out_dtype)

    return pl.pallas_call(
        body, name="mm",
        grid=(S // tm, N // tn),
        in_specs=[pl.BlockSpec((tm, K), lambda i, j: (i, 0)),
                  pl.BlockSpec((K, tn), lambda i, j: (0, j))],
        out_specs=pl.BlockSpec((tm, tn), lambda i, j: (i, j)),
        out_shape=jax.ShapeDtypeStruct((S, N), out_dtype),
        compiler_params=_params(2),
    )(a, w)


def _norm_fwd(x, g, tm):
    S = x.shape[0]

    def body(x_ref, g_ref, h_ref):
        h_ref[...] = _rms(x_ref[...], g_ref[...]).astype(BF)

    return pl.pallas_call(
        body, name="norm_fwd",
        grid=(S // tm,),
        in_specs=[pl.BlockSpec((tm, D), lambda i: (i, 0)), pl.BlockSpec((1, D), lambda i: (0, 0))],
        out_specs=pl.BlockSpec((tm, D), lambda i: (i, 0)),
        out_shape=jax.ShapeDtypeStruct((S, D), BF),
        compiler_params=_params(1),
    )(x, g)


def _mix_in_bwd(dz, wcat, dzf, wf, x1, g, dx2, tm):
    S = dz.shape[0]
    nk = 2
    kb = ZW // nk

    def body(dz_ref, w_ref, dzf_ref, wf_ref, x_ref, g_ref, dx2_ref, dx1_ref, dg_ref, acc):
        i = pl.program_id(0)
        k = pl.program_id(1)

        @pl.when((i == 0) & (k == 0))
        def _():
            dg_ref[...] = jnp.zeros_like(dg_ref)

        @pl.when(k == 0)
        def _():
            acc[...] = _dot_nt(dzf_ref[...], wf_ref[...])

        acc[...] += _dot_nt(dz_ref[...], w_ref[...])

        @pl.when(k == nk - 1)
        def _():
            dx, dg = _rms_bwd(acc[...], x_ref[...], g_ref[...])
            dx1_ref[...] = dx2_ref[...] + dx
            dg_ref[...] += dg

    row = pl.BlockSpec((tm, D), lambda i, k: (i, 0))
    vec = pl.BlockSpec((1, D), lambda i, k: (0, 0))
    return pl.pallas_call(
        body, name="mix_in_bwd",
        grid=(S // tm, nk),
        in_specs=[pl.BlockSpec((tm, kb), lambda i, k: (i, k)),
                  pl.BlockSpec((D, kb), lambda i, k: (0, k)),
                  pl.BlockSpec((tm, HD), lambda i, k: (i, 0)),
                  pl.BlockSpec((D, HD), lambda i, k: (0, 0)),
                  row, vec, row],
        out_specs=[row, vec],
        out_shape=[jax.ShapeDtypeStruct((S, D), F32), jax.ShapeDtypeStruct((1, D), F32)],
        scratch_shapes=[pltpu.VMEM((tm, D), F32)],
        compiler_params=_params(2),
    )(dz, wcat, dzf, wf, x1, g, dx2)


def _scan_rows(blk, reverse):
    n = blk.shape[0]
    row = lax.broadcasted_iota(jnp.int32, blk.shape, 0)
    d = 1
    while d < n:
        if reverse:
            blk = blk + jnp.where(row < n - d, pltpu.roll(blk, n - d, 0), 0.0)
        else:
            blk = blk + jnp.where(row >= d, pltpu.roll(blk, d, 0), 0.0)
        d *= 2
    return blk


def _forget_cumsum(zf, bf, cs):
    S = zf.shape[0]

    def body(zf_ref, bf_ref, c_ref, cb_ref, carry):
        @pl.when(pl.program_id(0) == 0)
        def _():
            carry[...] = jnp.zeros_like(carry)

        x = zf_ref[...] + bf_ref[...]
        logf = jnp.minimum(x, 0.0) - jnp.log1p(jnp.exp(-jnp.abs(x)))
        sc = _scan_rows(logf, False) + carry[...]
        carry[...] = sc[cs - 1:cs, :]
        sc = sc * LOG2E
        c_ref[...] = sc
        for h in range(H):
            cb_ref[h] = jnp.broadcast_to(sc[:, h:h + 1], (cs, HD))

    return pl.pallas_call(
        body, name="forget_cumsum",
        grid=(S // cs,),
        in_specs=[pl.BlockSpec((cs, HD), lambda i: (i, 0)), pl.BlockSpec((1, HD), lambda i: (0, 0))],
        out_specs=[pl.BlockSpec((cs, HD), lambda i: (i, 0)),
                   pl.BlockSpec((H, cs, HD), lambda i: (0, i, 0))],
        out_shape=[jax.ShapeDtypeStruct((S, HD), F32), jax.ShapeDtypeStruct((H, S, HD), F32)],
        scratch_shapes=[pltpu.VMEM((1, HD), F32)],
        compiler_params=_params(1),
    )(zf, bf)


def _forget_bwd(dc, zf, bf, cs):
    S = dc.shape[0]
    nc = S // cs

    def body(dc_ref, zf_ref, bf_ref, dzf_ref, dbf_ref, carry):
        @pl.when(pl.program_id(0) == 0)
        def _():
            carry[...] = jnp.zeros_like(carry)
            dbf_ref[...] = jnp.zeros_like(dbf_ref)

        sc = _scan_rows(dc_ref[...], True) + carry[...]
        carry[...] = sc[0:1, :]
        x = zf_ref[...] + bf_ref[...]
        dz = sc * _sigmoid(-x)
        dzf_ref[...] = dz.astype(BF)
        dbf_ref[...] += jnp.sum(dz, axis=0, keepdims=True)

    rev = pl.BlockSpec((cs, HD), lambda i: (nc - 1 - i, 0))
    vec = pl.BlockSpec((1, HD), lambda i: (0, 0))
    return pl.pallas_call(
        body, name="forget_bwd",
        grid=(nc,),
        in_specs=[rev, rev, vec],
        out_specs=[rev, vec],
        out_shape=[jax.ShapeDtypeStruct((S, HD), BF), jax.ShapeDtypeStruct((1, HD), F32)],
        scratch_shapes=[pltpu.VMEM((1, HD), F32)],
        compiler_params=_params(1),
    )(dc, zf, bf)


def _lanes(x, n):
    return x if n == HD else jnp.concatenate([x] * (n // HD), axis=1)


def _causal_mask(i, j, t, rows_are_queries):
    r = lax.broadcasted_iota(jnp.int32, (t, t), 0)
    c = lax.broadcasted_iota(jnp.int32, (t, t), 1)
    if rows_are_queries:
        return (j * t + c) <= (i * t + r)
    return (j * t + r) <= (i * t + c)


def _fox_fwd(z, ccol_b, crow, t):
    S = z.shape[0]
    nq = S // t

    def body(q_ref, kv_ref, cc_ref, cr_ref, o_ref, lse_ref, m_s, acc_s):
        i = pl.program_id(1)
        ct = cc_ref[...]
        ones = jnp.ones((t, HD), BF)
        m_s[...] = jnp.full_like(m_s, NEG)
        acc_s[...] = jnp.zeros_like(acc_s)
        n_sub = 2 if t >= 256 else 1
        ts = t // n_sub

        def step(j, masked):
            off = pl.multiple_of(j * t, t)
            k = kv_ref[pl.ds(off, t), :HD]
            v1 = jnp.concatenate([kv_ref[pl.ds(off, t), HD:], ones], axis=1)
            cs = cr_ref[pl.ds(j, 1), :]
            mask = _causal_mask(i, j, t, True) if masked else None
            for r in range(n_sub):
                rows = slice(r * ts, (r + 1) * ts)
                s = _dot_nt(q_ref[rows, :], k) - cs
                if masked:
                    s = jnp.where(mask[rows, :], s, NEG)
                m_old = m_s[rows, :]
                m_new = jnp.maximum(m_old, jnp.max(s, axis=1, keepdims=True))
                p = jnp.exp2(s - _lanes(m_new, t))
                alpha = jnp.exp2(m_old - m_new)
                acc_s[rows, :] = _lanes(alpha, 2 * HD) * acc_s[rows, :] + _dot(p.astype(BF), v1)
                m_s[rows, :] = m_new

        def full_step(j, carry):
            step(j, False)
            return carry

        lax.fori_loop(0, i, full_step, 0)
        step(i, True)
        l = acc_s[:, HD:]
        o_ref[...] = (acc_s[:, :HD] / l).astype(BF)
        lse_ref[...] = m_s[...] + ct + jnp.log2(l)

    return pl.pallas_call(
        body, name="fox_fwd",
        grid=(H, nq),
        in_specs=[pl.BlockSpec((t, HD), lambda h, i: (i, h)),
                  pl.BlockSpec((S, 2 * HD), lambda h, i: (0, 4 + h)),
                  pl.BlockSpec((None, t, HD), lambda h, i: (h, i, 0)),
                  pl.BlockSpec((None, nq, t), lambda h, i: (h, 0, 0))],
        out_specs=[pl.BlockSpec((t, HD), lambda h, i: (i, h)),
                   pl.BlockSpec((None, t, HD), lambda h, i: (h, i, 0))],
        out_shape=[jax.ShapeDtypeStruct((S, D), BF), jax.ShapeDtypeStruct((H, S, HD), F32)],
        scratch_shapes=[pltpu.VMEM((t, HD), F32), pltpu.VMEM((t, 2 * HD), F32)],
        compiler_params=_params(2),
    )(z, z, ccol_b, crow)


def _fox_bwd(z, do, o, lse_b, ccol_b, crow, dz, t):
    S = z.shape[0]
    nq = S // t

    def body(q_ref, kv_ref, do_ref, o_ref, lse_ref, cc_ref, cr_ref, dz_in,
             dq_ref, dkt_ref, dvt_ref, dck_ref, dcq_ref, acc_s, r_s):
        del dz_in
        i = pl.program_id(1)

        @pl.when(i == 0)
        def _():
            dkt_ref[...] = jnp.zeros_like(dkt_ref)
            dvt_ref[...] = jnp.zeros_like(dvt_ref)
            dck_ref[...] = jnp.zeros_like(dck_ref)

        q = q_ref[...]
        dout = do_ref[...]
        qt = jnp.transpose(q.astype(F32)).astype(BF)
        dot_ = jnp.transpose(dout.astype(F32)).astype(BF)
        off_t = _lanes(lse_ref[...] - cc_ref[...], t)
        delta = jnp.sum(dout.astype(F32) * o_ref[...].astype(F32), axis=1, keepdims=True)
        delta = _lanes(jnp.broadcast_to(delta, (t, HD)), t)
        acc_s[...] = jnp.zeros_like(acc_s)
        r_s[...] = jnp.zeros_like(r_s)

        def step(j, masked):
            off = pl.multiple_of(j * t, t)
            k = kv_ref[pl.ds(off, t), :HD]
            v = kv_ref[pl.ds(off, t), HD:]
            p = jnp.exp2(_dot_nt(q, k) - cr_ref[pl.ds(j, 1), :] - off_t)
            if masked:
                p = jnp.where(_causal_mask(i, j, t, True), p, 0.0)
            ds = p * (_dot_nt(dout, v) - delta)
            dsb = ds.astype(BF)
            acc_s[...] += _dot(dsb, k)
            dkt_ref[j] += _dot(qt, dsb)
            dvt_ref[j] += _dot(dot_, p.astype(BF))
            dck_ref[pl.ds(j, 1), :] += jnp.sum(ds, axis=0, keepdims=True)
            r_s[...] += jnp.sum(ds, axis=1, keepdims=True)

        def full_step(j, carry):
            step(j, False)
            return carry

        lax.fori_loop(0, i, full_step, 0)
        step(i, True)
        dq_ref[...] = (acc_s[...] * SCALE).astype(BF)
        dcq_ref[...] = jnp.transpose(r_s[...])[0:1, :]

    qspec = pl.BlockSpec((t, HD), lambda h, i: (i, h))
    bspec = pl.BlockSpec((None, t, HD), lambda h, i: (h, i, 0))
    rows = pl.BlockSpec((None, nq, t), lambda h, i: (h, 0, 0))
    tspec = pl.BlockSpec((None, nq, HD, t), lambda h, i: (h, 0, 0, 0))
    tshape = jax.ShapeDtypeStruct((H, nq, HD, t), F32)
    return pl.pallas_call(
        body, name="fox_bwd",
        grid=(H, nq),
        in_specs=[qspec,
                  pl.BlockSpec((S, 2 * HD), lambda h, i: (0, 4 + h)),
                  qspec, qspec, bspec, bspec, rows,
                  pl.BlockSpec(memory_space=pl.ANY)],
        out_specs=[qspec, tspec, tspec, rows, pl.BlockSpec((None, None, 1, t), lambda h, i: (h, i, 0, 0))],
        out_shape=[jax.ShapeDtypeStruct((S, ZW), BF), tshape, tshape,
                   jax.ShapeDtypeStruct((H, nq, t), F32), jax.ShapeDtypeStruct((H, nq, 1, t), F32)],
        scratch_shapes=[pltpu.VMEM((t, HD), F32), pltpu.VMEM((t, HD), F32)],
        input_output_aliases={7: 0},
        compiler_params=_params(2),
    )(z, z, do, o, lse_b, ccol_b, crow, dz)


def _fox_bwd_finish(dkt, dvt, dz, t):
    nq = dkt.shape[1]
    S = nq * t

    def body(dkt_ref, dvt_ref, dz_in, dkv_ref):
        del dz_in
        dkv_ref[:, :HD] = (jnp.transpose(dkt_ref[...]) * LN2).astype(BF)
        dkv_ref[:, HD:] = jnp.transpose(dvt_ref[...]).astype(BF)

    tspec = pl.BlockSpec((None, None, HD, t), lambda h, j: (h, j, 0, 0))
    return pl.pallas_call(
        body, name="fox_bwd_finish",
        grid=(H, nq),
        in_specs=[tspec, tspec, pl.BlockSpec(memory_space=pl.ANY)],
        out_specs=pl.BlockSpec((t, 2 * HD), lambda h, j: (j, 4 + h)),
        out_shape=jax.ShapeDtypeStruct((S, ZW), BF),
        input_output_aliases={2: 0},
        compiler_params=_params(2),
    )(dkt, dvt, dz)


def _sgu_mask(transposed):
    r = lax.broadcasted_iota(jnp.int32, (L, L), 0)
    c = lax.broadcasted_iota(jnp.int32, (L, L), 1)
    if transposed:
        return (r // CHUNK) <= (c // CHUNK)
    return (c // CHUNK) <= (r // CHUNK)


def _ln_group(vs, lng, lnb):
    mu = jnp.mean(vs, axis=-1, keepdims=True)
    xc = vs - mu
    rstd = lax.rsqrt(jnp.mean(xc * xc, axis=-1, keepdims=True) + LN_EPS)
    xhat = xc * rstd
    return xhat, rstd, xhat * lng + lnb


def _mix_out_fwd(z, oa, ln_g, ln_b, ws, bst, wout, x1, g_post, tm):
    S = z.shape[0]
    nw = tm // L

    def body(u_ref, sv_ref, ga_ref, gb_ref, oa_ref, lng_ref, lnb_ref, ws_ref, bst_ref, wout_ref, x1_ref, gp_ref,
             mg_ref, y_ref, x2_ref, mg_s):
        mask = _sgu_mask(False)
        for g in range(G):
            cols = slice(g * L, (g + 1) * L)
            wm = jnp.where(mask, ws_ref[g], 0.0).astype(BF)
            bcol = bst_ref[:, g:g + 1]
            lng = lng_ref[:, cols]
            lnb = lnb_ref[:, cols]
            for w in range(nw):
                rows = slice(w * L, (w + 1) * L)
                vs = _gelu(sv_ref[rows, cols].astype(F32))
                _, _, vn = _ln_group(vs, lng, lnb)
                mixed = _dot(wm, vn.astype(BF)) + bcol
                ob = _gelu(u_ref[rows, cols].astype(F32)) * mixed
                mg = (_sigmoid(ga_ref[rows, cols].astype(F32)) * oa_ref[rows, cols].astype(F32)
                      + _sigmoid(gb_ref[rows, cols].astype(F32)) * ob)
                mg_s[rows, cols] = mg.astype(BF)
        mgb = mg_s[...]
        mg_ref[...] = mgb
        y = _dot(mgb, wout_ref[...])
        y_ref[...] = y
        x2_ref[...] = x1_ref[...] + _rms(y, gp_ref[...])

    row = pl.BlockSpec((tm, D), lambda i: (i, 0))
    vec = pl.BlockSpec((1, D), lambda i: (0, 0))

    def zcol(kb):
        return pl.BlockSpec((tm, D), lambda i: (i, kb))

    return pl.pallas_call(
        body, name="mix_out_fwd",
        grid=(S // tm,),
        in_specs=[zcol(3), zcol(4), zcol(5), zcol(6), row, vec, vec,
                  pl.BlockSpec((G, L, L), lambda i: (0, 0, 0)),
                  pl.BlockSpec((L, HD), lambda i: (0, 0)),
                  pl.BlockSpec((D, D), lambda i: (0, 0)),
                  row, vec],
        out_specs=[row, row, row],
        out_shape=[jax.ShapeDtypeStruct((S, D), BF),
                   jax.ShapeDtypeStruct((S, D), F32),
                   jax.ShapeDtypeStruct((S, D), F32)],
        scratch_shapes=[pltpu.VMEM((tm, D), BF)],
        compiler_params=_params(1),
    )(z, z, z, z, oa, ln_g, ln_b, ws, bst, wout, x1, g_post)


def _mix_out_bwd(dx2, y2, g_post, wout, z, oa, ln_g, ln_b, ws, wst, bst, tm, scatter):
    S = z.shape[0]
    nw = tm // L

    def body(dx2_ref, y_ref, gp_ref, wout_ref, u_ref, sv_ref, ga_ref, gb_ref, oa_ref, lng_ref, lnb_ref,
             ws_ref, wst_ref, bst_ref, q_ref,
             dz_ref, dy_ref, doa_ref, dgp_ref, dlng_ref, dlnb_ref, dws_ref, dbst_ref, t_ref,
             dzg_s, dm_s, send_sems, recv_sems):
        i = pl.program_id(0)
        c = pl.program_id(1)

        @pl.when((i == 0) & (c == 0))
        def _():
            _scatter_start(q_ref, t_ref, send_sems, recv_sems)
            dgp_ref[...] = jnp.zeros_like(dgp_ref)
            dlng_ref[...] = jnp.zeros_like(dlng_ref)
            dlnb_ref[...] = jnp.zeros_like(dlnb_ref)
            dws_ref[...] = jnp.zeros_like(dws_ref)
            dbst_ref[...] = jnp.zeros_like(dbst_ref)

        @pl.when(c == 0)
        def _():
            dy, dg = _rms_bwd(dx2_ref[...], y_ref[...], gp_ref[...])
            dyb = dy.astype(BF)
            dy_ref[...] = dyb
            dgp_ref[...] += dg
            dm_s[...] = _dot_nt(dyb, wout_ref[...])
            mask = _sgu_mask(False)
            mask_t = _sgu_mask(True)
            lane = lax.broadcasted_iota(jnp.int32, (L, HD), 1)
            for g in range(G):
                cols = slice(g * L, (g + 1) * L)
                wm = jnp.where(mask, ws_ref[g], 0.0).astype(BF)
                wmt = jnp.where(mask_t, wst_ref[g], 0.0).astype(BF)
                bcol = bst_ref[:, g:g + 1]
                lng = lng_ref[:, cols]
                lnb = lnb_ref[:, cols]
                dws_g = jnp.zeros((L, L), F32)
                dbs_g = jnp.zeros((L, 1), F32)
                dlng_g = jnp.zeros((1, L), F32)
                dlnb_g = jnp.zeros((1, L), F32)
                for w in range(nw):
                    rows = slice(w * L, (w + 1) * L)
                    dm = dm_s[rows, cols]
                    vs, dvs_dz = _gelu_parts(sv_ref[rows, cols].astype(F32))
                    xhat, rstd, vn = _ln_group(vs, lng, lnb)
                    vnb = vn.astype(BF)
                    mixed = _dot(wm, vnb) + bcol
                    u, du_dz = _gelu_parts(u_ref[rows, cols].astype(F32))
                    sga = _sigmoid(ga_ref[rows, cols].astype(F32))
                    sgb = _sigmoid(gb_ref[rows, cols].astype(F32))
                    oav = oa_ref[rows, cols].astype(F32)
                    ob = u * mixed
                    doa_ref[rows, cols] = (dm * sga).astype(BF)
                    dzg_s[2, rows, cols] = (dm * oav * sga * (1.0 - sga)).astype(BF)
                    dzg_s[3, rows, cols] = (dm * ob * sgb * (1.0 - sgb)).astype(BF)
                    dob = dm * sgb
                    dzg_s[0, rows, cols] = (dob * mixed * du_dz).astype(BF)
                    dmixed = dob * u
                    dmb = dmixed.astype(BF)
                    dbs_g += jnp.sum(dmixed, axis=1, keepdims=True)
                    dws_g += _dot_nt(dmb, vnb)
                    dvn = _dot(wmt, dmb)
                    dlng_g += jnp.sum(dvn * xhat, axis=0, keepdims=True)
                    dlnb_g += jnp.sum(dvn, axis=0, keepdims=True)
                    dxh = dvn * lng
                    dvs = rstd * (dxh - jnp.mean(dxh, axis=-1, keepdims=True)
                                  - xhat * jnp.mean(dxh * xhat, axis=-1, keepdims=True))
                    dzg_s[1, rows, cols] = (dvs * dvs_dz).astype(BF)
                dws_ref[g] += jnp.where(mask, dws_g, 0.0)
                dbst_ref[...] += jnp.where(lane == g, dbs_g, 0.0)
                dlng_ref[:, cols] += dlng_g
                dlnb_ref[:, cols] += dlnb_g

        dz_ref[...] = dzg_s[c]

        @pl.when((i == S // tm - 1) & (c == 3))
        def _():
            _scatter_finish(q_ref, t_ref, send_sems, recv_sems)

    row = pl.BlockSpec((tm, D), lambda i, c: (i, 0))
    vec = pl.BlockSpec((1, D), lambda i, c: (0, 0))
    wsspec = pl.BlockSpec((G, L, L), lambda i, c: (0, 0, 0))
    bspec = pl.BlockSpec((L, HD), lambda i, c: (0, 0))
    anywhere = pl.BlockSpec(memory_space=pl.ANY)

    def zcol(kb):
        return pl.BlockSpec((tm, D), lambda i, c: (i, kb))

    return pl.pallas_call(
        body, name="mix_out_bwd",
        grid=(S // tm, 4),
        in_specs=[row, row, vec, pl.BlockSpec((D, D), lambda i, c: (0, 0)),
                  zcol(3), zcol(4), zcol(5), zcol(6), row, vec, vec, wsspec, wsspec, bspec, anywhere],
        out_specs=[pl.BlockSpec((tm, D), lambda i, c: (i, 3 + c)),
                   row, row, vec, vec, vec, wsspec, bspec, anywhere],
        out_shape=[jax.ShapeDtypeStruct((S, ZW), BF),
                   jax.ShapeDtypeStruct((S, D), BF),
                   jax.ShapeDtypeStruct((S, D), BF),
                   jax.ShapeDtypeStruct((1, D), F32),
                   jax.ShapeDtypeStruct((1, D), F32),
                   jax.ShapeDtypeStruct((1, D), F32),
                   jax.ShapeDtypeStruct((G, L, L), F32),
                   jax.ShapeDtypeStruct((L, HD), F32),
                   jax.ShapeDtypeStruct((3,) + scatter.shape[1:], scatter.dtype)],
        scratch_shapes=[pltpu.VMEM((4, tm, D), BF), pltpu.VMEM((tm, D), F32),
                        pltpu.SemaphoreType.DMA((3,)), pltpu.SemaphoreType.DMA((3,))],
        compiler_params=_params(2),
    )(dx2, y2, g_post, wout, z, z, z, z, oa, ln_g, ln_b, ws, wst, bst, scatter)


def _loss_head(x3, target, tm):
    S = x3.shape[0]

    def body(x_ref, t_ref, dx_ref, loss_ref):
        @pl.when(pl.program_id(0) == 0)
        def _():
            loss_ref[...] = jnp.zeros_like(loss_ref)

        e = x_ref[...] - t_ref[...]
        dx_ref[...] = e * (1.0 / D)
        loss_ref[...] += jnp.sum(e * e) * (0.5 / D)

    row = pl.BlockSpec((tm, D), lambda i: (i, 0))
    return pl.pallas_call(
        body, name="loss_head",
        grid=(S // tm,),
        in_specs=[row, row],
        out_specs=[row, pl.BlockSpec((8, HD), lambda i: (0, 0))],
        out_shape=[jax.ShapeDtypeStruct((S, D), F32), jax.ShapeDtypeStruct((8, HD), F32)],
        compiler_params=_params(1),
    )(x3, target)


def _adamw(w, g, m, v, tr):
    R, C = w.shape

    def body(w_ref, g_ref, m_ref, v_ref, d_ref, nm_ref, nv_ref):
        gv = g_ref[...]
        m_new = ADAM_B1 * m_ref[...] + (1.0 - ADAM_B1) * gv
        v_new = ADAM_B2 * v_ref[...] + (1.0 - ADAM_B2) * (gv * gv)
        m_hat = m_new / (1.0 - ADAM_B1 ** ADAM_STEP)
        v_hat = v_new / (1.0 - ADAM_B2 ** ADAM_STEP)
        d_ref[...] = -ADAM_LR * (m_hat / (jnp.sqrt(v_hat) + ADAM_EPS) + ADAM_WD * w_ref[...])
        nm_ref[...] = m_new
        nv_ref[...] = v_new

    spec = pl.BlockSpec((tr, C), lambda i: (i, 0))
    shp = jax.ShapeDtypeStruct((R, C), F32)
    return pl.pallas_call(
        body, name="adamw",
        grid=(R // tr,),
        in_specs=[spec] * 4, out_specs=[spec] * 3, out_shape=[shp] * 3,
        compiler_params=_params(1),
    )(w, g, m, v)


def _mesh_pos():
    return lax.axis_index("x"), lax.axis_index("y"), lax.axis_index("c")


def _half(c, rows):
    return pl.ds(pl.multiple_of(c * rows, 16), rows)


def _other_chips(x, y):
    return [(1 - x, y), (x, 1 - y), (1 - x, 1 - y)]


def _remote(k, src, dst, to, send_sems, recv_sems):
    return pltpu.make_async_remote_copy(src_ref=src, dst_ref=dst, send_sem=send_sems.at[k],
                                        recv_sem=recv_sems.at[k], device_id=to, device_id_type=MESH)


def _gather_start(wp_ref, g_ref, send_sems, recv_sems):
    x, y, c = _mesh_pos()
    mine = _half(c, wp_ref.shape[0] // 2)
    for k, (px, py) in enumerate(_other_chips(x, y)):
        _remote(k, wp_ref.at[mine], g_ref.at[2 * x + y, mine], (px, py, c), send_sems, recv_sems).start()


def _gather_finish(wp_ref, g_ref, send_sems, recv_sems):
    x, y, c = _mesh_pos()
    sibling = (x, y, 1 - c)
    rows = wp_ref.shape[0] // 2
    mine, other = _half(c, rows), _half(1 - c, rows)
    chips = _other_chips(x, y)
    for k, (px, py) in enumerate(chips):
        land = g_ref.at[2 * px + py, mine]
        _remote(k, land, land, (px, py, c), send_sems, recv_sems).wait_recv()
        _remote(3 + k, land, land, sibling, send_sems, recv_sems).start()
    for k, (px, py) in enumerate(chips):
        land = g_ref.at[2 * px + py, other]
        _remote(3 + k, land, land, sibling, send_sems, recv_sems).wait_recv()
    for k, (px, py) in enumerate(chips):
        land = g_ref.at[2 * px + py, mine]
        _remote(k, wp_ref.at[mine], g_ref.at[2 * x + y, mine], (px, py, c), send_sems, recv_sems).wait_send()
        _remote(3 + k, land, land, sibling, send_sems, recv_sems).wait_send()


def _place_own_shard(g, wp):
    x, y, _ = _mesh_pos()
    return lax.dynamic_update_index_in_dim(g, wp, 2 * x + y, 0)


def _all_gather_weights(wp):
    def body(wp_ref, g_ref, send_sems, recv_sems):
        _gather_start(wp_ref, g_ref, send_sems, recv_sems)
        _gather_finish(wp_ref, g_ref, send_sems, recv_sems)

    g = pl.pallas_call(
        body, name="all_gather_weights",
        in_specs=[pl.BlockSpec(memory_space=pl.ANY)],
        out_specs=pl.BlockSpec(memory_space=pl.ANY),
        out_shape=jax.ShapeDtypeStruct((NSH,) + wp.shape, wp.dtype),
        scratch_shapes=[pltpu.SemaphoreType.DMA((6,)), pltpu.SemaphoreType.DMA((6,))],
        compiler_params=pltpu.CompilerParams(has_side_effects=True),
    )(wp)
    return _place_own_shard(g, wp)


def _scatter_copies(q_ref, t_ref, send_sems, recv_sems):
    x, y, c = _mesh_pos()
    return [_remote(k, q_ref.at[2 * px + py], t_ref.at[k], (px, py, c), send_sems, recv_sems)
            for k, (px, py) in enumerate(_other_chips(x, y))]


def _scatter_start(q_ref, t_ref, send_sems, recv_sems):
    for cp in _scatter_copies(q_ref, t_ref, send_sems, recv_sems):
        cp.start()


def _scatter_finish(q_ref, t_ref, send_sems, recv_sems):
    for cp in _scatter_copies(q_ref, t_ref, send_sems, recv_sems):
        cp.wait()


def _pair_exchange(p):
    rows = p.shape[1] // 2

    def body(p_ref, r_ref, send_sem, recv_sem):
        x, y, c = _mesh_pos()
        cp = pltpu.make_async_remote_copy(src_ref=p_ref.at[:, _half(1 - c, rows)], dst_ref=r_ref, send_sem=send_sem,
                                          recv_sem=recv_sem, device_id=(x, y, 1 - c), device_id_type=MESH)
        cp.start()
        cp.wait()

    return pl.pallas_call(
        body, name="pair_exchange",
        in_specs=[pl.BlockSpec(memory_space=pl.ANY)],
        out_specs=pl.BlockSpec(memory_space=pl.ANY),
        out_shape=jax.ShapeDtypeStruct((NSH, rows, D), F32),
        scratch_shapes=[pltpu.SemaphoreType.DMA, pltpu.SemaphoreType.DMA],
        compiler_params=pltpu.CompilerParams(has_side_effects=True),
    )(p)


def _pair_add(p, r, nb):
    rows = r.shape[1]
    tr = rows // nb

    def body(p_ref, r_ref, q_ref):
        q_ref[...] = (p_ref[...] + r_ref[...]).astype(BF)

    return pl.pallas_call(
        body, name="pair_add", grid=(NSH, nb),
        in_specs=[pl.BlockSpec((None, tr, D), lambda j, i: (j, lax.axis_index("c") * nb + i, 0)),
                  pl.BlockSpec((None, tr, D), lambda j, i: (j, i, 0))],
        out_specs=pl.BlockSpec((None, tr, D), lambda j, i: (j, i, 0)),
        out_shape=jax.ShapeDtypeStruct((NSH, rows, D), BF),
        compiler_params=_params(2),
    )(p, r)


def _chip_exchange(q, small):
    def body(q_ref, s_ref, t_ref, sm_ref, send_sems, recv_sems, ssend_sems, srecv_sems):
        x, y, c = _mesh_pos()
        me = 4 * x + 2 * y + c
        _scatter_start(q_ref, t_ref, send_sems, recv_sems)
        flips = [(fx, fy, fc) for fx in (0, 1) for fy in (0, 1) for fc in (0, 1)][1:]
        smalls = []
        for k, (fx, fy, fc) in enumerate(flips):
            cp = pltpu.make_async_remote_copy(src_ref=s_ref, dst_ref=sm_ref.at[me],
                                              send_sem=ssend_sems.at[k], recv_sem=srecv_sems.at[k],
                                              device_id=(x ^ fx, y ^ fy, c ^ fc), device_id_type=MESH)
            cp.start()
            smalls.append(cp)
        _scatter_finish(q_ref, t_ref, send_sems, recv_sems)
        for cp in smalls:
            cp.wait()

    t, sm = pl.pallas_call(
        body, name="chip_exchange",
        in_specs=[pl.BlockSpec(memory_space=pl.ANY), pl.BlockSpec(memory_space=pl.ANY)],
        out_specs=[pl.BlockSpec(memory_space=pl.ANY), pl.BlockSpec(memory_space=pl.ANY)],
        out_shape=[jax.ShapeDtypeStruct((3,) + q.shape[1:], BF),
                   jax.ShapeDtypeStruct((NDEV, SMALL_ROWS, D), F32)],
        scratch_shapes=[pltpu.SemaphoreType.DMA((3,)), pltpu.SemaphoreType.DMA((3,)),
                        pltpu.SemaphoreType.DMA((7,)), pltpu.SemaphoreType.DMA((7,))],
        compiler_params=pltpu.CompilerParams(has_side_effects=True),
    )(q, small)
    x, y, c = _mesh_pos()
    return t, lax.dynamic_update_index_in_dim(sm, small, 4 * x + 2 * y + c, 0)


def _shard_sum(p, r, t, nb):
    rows = r.shape[1]
    tr = rows // nb

    def shard():
        return 2 * lax.axis_index("x") + lax.axis_index("y")

    def body(p_ref, r_ref, t_ref, o_ref):
        s = p_ref[...] + r_ref[...]
        for k in range(3):
            s = s + t_ref[k].astype(F32)
        o_ref[...] = s

    return pl.pallas_call(
        body, name="shard_sum", grid=(nb,),
        in_specs=[pl.BlockSpec((None, tr, D), lambda i: (shard(), lax.axis_index("c") * nb + i, 0)),
                  pl.BlockSpec((None, tr, D), lambda i: (shard(), i, 0)),
                  pl.BlockSpec((3, tr, D), lambda i: (0, i, 0))],
        out_specs=pl.BlockSpec((tr, D), lambda i: (i, 0)),
        out_shape=jax.ShapeDtypeStruct((rows, D), F32),
        compiler_params=_params(1),
    )(p, r, t)


def _small_sum(sm):
    def body(sm_ref, o_ref):
        s = sm_ref[0]
        for k in range(1, NDEV):
            s = s + sm_ref[k]
        o_ref[...] = s

    return pl.pallas_call(
        body, name="small_sum",
        in_specs=[pl.BlockSpec(memory_space=pltpu.VMEM)],
        out_specs=pl.BlockSpec(memory_space=pltpu.VMEM),
        out_shape=jax.ShapeDtypeStruct((SMALL_ROWS, D), F32),
    )(sm)


def _pair_gather(halves):
    n = len(halves)

    def body(*refs):
        gh_refs, o_refs, (send_sems, recv_sems) = refs[:n], refs[n:2 * n], refs[2 * n:]
        x, y, c = _mesh_pos()
        sibling = (x, y, 1 - c)
        for g in range(n):
            rows = gh_refs[g].shape[0]
            _remote(g, gh_refs[g], o_refs[g].at[_half(c, rows)], sibling, send_sems, recv_sems).start()
        for g in range(n):
            rows = gh_refs[g].shape[0]
            _remote(g, gh_refs[g], o_refs[g].at[_half(c, rows)], sibling, send_sems, recv_sems).wait_send()
            _remote(g, gh_refs[g], o_refs[g].at[_half(1 - c, rows)], sibling, send_sems, recv_sems).wait_recv()

    anywhere = pl.BlockSpec(memory_space=pl.ANY)
    outs = pl.pallas_call(
        body, name="pair_gather",
        in_specs=[anywhere] * n, out_specs=[anywhere] * n,
        out_shape=[jax.ShapeDtypeStruct((2 * h.shape[0], D), F32) for h in halves],
        scratch_shapes=[pltpu.SemaphoreType.DMA((n,)), pltpu.SemaphoreType.DMA((n,))],
        compiler_params=pltpu.CompilerParams(has_side_effects=True),
    )(*halves)
    c = lax.axis_index("c")
    return [lax.dynamic_update_slice_in_dim(o, h, c * h.shape[0], 0) for o, h in zip(outs, halves)]


def _pad_cols(a, n):
    return jnp.pad(a, ((0, 0), (0, n - a.shape[1])))


def _split_w_in(w_in_full):
    q, k, v = w_in_full[:, :D], w_in_full[:, D:2 * D], w_in_full[:, 2 * D:3 * D]
    f = w_in_full[:, 3 * D:3 * D + H]
    gates = w_in_full[:, 3 * D + H:]
    kv = jnp.stack([k.reshape(D, H, HD), v.reshape(D, H, HD)], axis=2).reshape(D, 2 * D)
    return jnp.concatenate([q, kv, gates], axis=1), _pad_cols(f, HD)


def _merge_w_in_grad(dwcat, dwf):
    kv = dwcat[:, D:3 * D].reshape(D, H, 2, HD)
    return jnp.concatenate([dwcat[:, :D], kv[:, :, 0].reshape(D, D), kv[:, :, 1].reshape(D, D),
                            dwf[:, :H], dwcat[:, 3 * D:]], axis=1)


def _reduce_start(g):
    r = _pair_exchange(g)
    return g, r, _pair_add(g, r, 3)


def _train_step(x, target, wp1, wp2, small, tm, t_attn):
    S = x.shape[0]
    g1pre, g1post = small["ffn1_pre_g"], small["ffn1_post_g"]
    gmpre, gmpost = small["mix_pre_g"], small["mix_post_g"]
    g2pre, g2post = small["ffn2_pre_g"], small["ffn2_post_g"]
    ln_g, ln_b = small["sgu_ln_g"], small["sgu_ln_b"]
    ws = small["sgu_w_s"][0]
    wst = jnp.swapaxes(ws, 1, 2)
    bst = _pad_cols(small["sgu_b_s"][0].T, HD)
    bf = _pad_cols(small["b_forget"], HD)

    w1 = _all_gather_weights(wp1)
    h1, a1, b1, y1, x1, w2 = _ffn_fwd(x, g1pre, w1, g1post, tm, gather=wp2)
    wout = w2[:, FFN_ROWS:FFN_ROWS + 256, :].reshape(D, D)
    r0 = FFN_ROWS + 256
    w_in_full = jnp.concatenate(
        [blk for j in range(NSH) for blk in (w2[j, r0:r0 + D], w2[j, r0 + D:r0 + 2 * D, :WIN_SH - D])], axis=1)
    wcat, wf = _split_w_in(w_in_full)
    h2 = _norm_fwd(x1, gmpre, tm)
    z = _mm(h2, wcat, min(1024, S), D, BF, first_block_scale=SCALE * LOG2E)
    zf = _mm(h2, wf, tm, HD, F32)
    cs = min(512, S)
    c, ccol_b = _forget_cumsum(zf, bf, cs)
    crow = jnp.transpose(c[:, :H]).reshape(H, S // t_attn, t_attn)
    oa, lse_b = _fox_fwd(z, ccol_b, crow, t_attn)
    merged, y2, x2 = _mix_out_fwd(z, oa, ln_g, ln_b, ws, bst, wout, x1, gmpost, tm)
    h3, a3, b3, y3, x3 = _ffn_fwd(x2, g2pre, w2, g2post, tm)
    dx3, loss_acc = _loss_head(x3, target, tm)
    loss = loss_acc[0, 0]

    dy3, da3, db3, act3, dx2, dg2post, dg2pre = _ffn_bwd(dx3, y3, g2post, a3, b3, w2, x2, g2pre, tm)
    bt = min(2048, S)
    dwg2 = _mm_tn(h3, da3, D, D, bt, col_blocked=True)
    dwu2 = _mm_tn(h3, db3, D, D, bt, col_blocked=True)
    dwd2 = _mm_tn(act3, dy3, D, D, bt).reshape(NSH, D, D)
    g_ffn2, r_ffn2, q_ffn2 = _reduce_start(jnp.concatenate([dwg2, dwu2, dwd2], axis=1))

    dz, dy2, doa, dgmpost, dlng, dlnb, dws, dbst, t_ffn2 = _mix_out_bwd(
        dx2, y2, gmpost, wout, z, oa, ln_g, ln_b, ws, wst, bst, tm, q_ffn2)
    dwout = _mm_tn(merged, dy2, D, D, bt).reshape(NSH, 256, D)
    dz, dkt, dvt, dc_keys, dc_queries = _fox_bwd(z, doa, oa, lse_b, ccol_b, crow, dz, t_attn)
    dz = _fox_bwd_finish(dkt, dvt, dz, t_attn)
    dc = _pad_cols(jnp.transpose(dc_queries.reshape(H, S) - dc_keys.reshape(H, S)), HD)
    dzf, dbf = _forget_bwd(dc, zf, bf, cs)
    dwcat = _mm_tn(h2, dz, D, D, bt)
    dwf = _mm_tn(h2, dzf, D, HD, bt)
    dx1, dgmpre = _mix_in_bwd(dz, wcat, dzf, wf, x1, gmpre, dx2, tm)
    dwin = _merge_w_in_grad(dwcat, dwf)
    dwin_a = jnp.stack([dwin[:, j * WIN_SH:j * WIN_SH + D] for j in range(NSH)])
    dwin_b = jnp.stack([_pad_cols(dwin[:, j * WIN_SH + D:(j + 1) * WIN_SH], D) for j in range(NSH)])
    g_mix, r_mix, q_mix = _reduce_start(jnp.concatenate([dwout, dwin_a, dwin_b], axis=1))

    dy1, da1, db1, act1, dx, dg1post, dg1pre, t_mix = _ffn_bwd(dx1, y1, g1post, a1, b1, w1, x, g1pre, tm,
                                                               scatter=q_mix)
    dwg1 = _mm_tn(h1, da1, D, D, bt, col_blocked=True)
    dwu1 = _mm_tn(h1, db1, D, D, bt, col_blocked=True)
    dwd1 = _mm_tn(act1, dy1, D, D, bt).reshape(NSH, D, D)
    g_ffn1, r_ffn1, q_ffn1 = _reduce_start(jnp.concatenate([dwg1, dwu1, dwd1], axis=1))

    gsmall = {
        "ffn1_pre_g": dg1pre, "ffn1_post_g": dg1post, "mix_pre_g": dgmpre, "mix_post_g": dgmpost,
        "ffn2_pre_g": dg2pre, "ffn2_post_g": dg2post, "sgu_ln_g": dlng, "sgu_ln_b": dlnb,
        "sgu_w_s": dws[None], "sgu_b_s": jnp.transpose(dbst[:, :G])[None], "b_forget": dbf[:, :H],
    }
    t_ffn1, sm = _chip_exchange(q_ffn1, _pack_small(gsmall))
    halves = [_shard_sum(g, r, t, 3) for g, r, t in ((g_ffn1, r_ffn1, t_ffn1), (g_ffn2, r_ffn2, t_ffn2),
                                                     (g_mix, r_mix, t_mix))]
    f_ffn1, f_ffn2, f_mix = _pair_gather(halves)
    return loss, dx, f_ffn1, f_ffn2, f_mix, _small_sum(sm)


_SMALL_NAMES = ["ffn1_pre_g", "ffn1_post_g", "mix_pre_g", "mix_post_g", "ffn2_pre_g", "ffn2_post_g",
                "sgu_ln_g", "sgu_ln_b", "sgu_b_s", "b_forget", "sgu_w_s"]
_SMALL_SHAPES = {"sgu_b_s": (1, G, L), "b_forget": (1, H), "sgu_w_s": (1, G, L, L)}


def _pack_small(d):
    rows = []
    for n in _SMALL_NAMES:
        a = d[n].astype(F32)
        if n == "b_forget":
            a = _pad_cols(a, D)
        a = a.reshape(-1, D)
        rows.append(jnp.pad(a, ((0, -a.shape[0] % SMALL_STRIDE), (0, 0))))
    return jnp.concatenate(rows, axis=0)


def _unpack_small(p):
    out, r = {}, 0
    for n in _SMALL_NAMES:
        if n == "sgu_w_s":
            out[n] = p[r:r + L].reshape(1, G, L, L)
            r += L
        elif n == "b_forget":
            out[n] = p[r:r + 1, :H]
            r += SMALL_STRIDE
        elif n == "sgu_b_s":
            out[n] = p[r:r + 1].reshape(1, G, L)
            r += SMALL_STRIDE
        else:
            out[n] = p[r:r + 1]
            r += SMALL_STRIDE
    return out


_BIG_NAMES = ["ffn1_w_gate", "ffn1_w_up", "ffn1_w_down", "ffn2_w_gate", "ffn2_w_up", "ffn2_w_down", "w_out", "w_in"]
_WEIGHT_ORDER = ['ffn1_pre_g', 'ffn1_w_gate', 'ffn1_w_up', 'ffn1_w_down', 'ffn1_post_g', 'mix_pre_g', 'w_in', 'b_forget',
                 'sgu_ln_g', 'sgu_ln_b', 'sgu_w_s', 'sgu_b_s', 'w_out', 'mix_post_g', 'ffn2_pre_g', 'ffn2_w_gate',
                 'ffn2_w_up', 'ffn2_w_down', 'ffn2_post_g']


def _pack_ffn(w, name):
    return jnp.concatenate([w[name + "_w_gate"][0], w[name + "_w_up"][0], w[name + "_w_down"][0]], axis=0)


def _pack_mix(w):
    w_in = w["w_in"][0]
    return jnp.concatenate([w["w_out"][0], w_in[:, :D], _pad_cols(w_in[:, D:], D)], axis=0)


def _unpack_ffn(p, name):
    return {name + "_w_gate": p[:D][None], name + "_w_up": p[D:2 * D][None], name + "_w_down": p[2 * D:][None]}


def _unpack_mix(p):
    return {"w_out": p[:256][None],
            "w_in": jnp.concatenate([p[256:256 + D], p[256 + D:, :WIN_SH - D]], axis=1)[None]}


def _step(args, tm, t_attn):
    x = args["x"][0]
    target = args["loss_target"][0]
    weights = {n: args[n] for n in _WEIGHT_ORDER}
    small = {n: weights[n] for n in _SMALL_NAMES}

    wb = {n: weights[n].astype(BF) for n in _BIG_NAMES}
    wp1 = _pack_ffn(wb, "ffn1")
    wp2 = jnp.concatenate([_pack_ffn(wb, "ffn2"), _pack_mix(wb)], axis=0)
    loss_local, dx, f_ffn1, f_ffn2, f_mix, gsm = _train_step(x, target, wp1, wp2, small, tm, t_attn)
    loss = lax.psum(loss_local, ("x", "y", "c"))
    grads = {**_unpack_ffn(f_ffn1, "ffn1"), **_unpack_ffn(f_ffn2, "ffn2"), **_unpack_mix(f_mix),
             **_unpack_small(gsm)}

    delta, new_m, new_v = {}, {}, {}
    for n in _BIG_NAMES:
        shp = weights[n].shape
        w2 = weights[n].reshape(-1, shp[-1])
        rows = w2.shape[0]
        d, nm, nv = _adamw(w2, grads[n].reshape(w2.shape), args["m_" + n].reshape(w2.shape),
                           args["v_" + n].reshape(w2.shape), rows // 4)
        delta[n], new_m[n], new_v[n] = d.reshape(shp), nm.reshape(shp), nv.reshape(shp)
    ds, nms, nvs = _adamw(_pack_small(small), gsm, _pack_small({n: args["m_" + n] for n in _SMALL_NAMES}),
                          _pack_small({n: args["v_" + n] for n in _SMALL_NAMES}), SMALL_ROWS)
    delta.update(_unpack_small(ds))
    new_m.update(_unpack_small(nms))
    new_v.update(_unpack_small(nvs))

    return (loss, dx[None], *[grads[n] for n in _WEIGHT_ORDER], *[delta[n] for n in _WEIGHT_ORDER],
            *[new_m[n] for n in _WEIGHT_ORDER], *[new_v[n] for n in _WEIGHT_ORDER])


_ARG_NAMES = (["x"] + _WEIGHT_ORDER + ["loss_target"] + ["m_" + n for n in _WEIGHT_ORDER]
              + ["v_" + n for n in _WEIGHT_ORDER])


def kernel(x, ffn1_pre_g, ffn1_w_gate, ffn1_w_up, ffn1_w_down, ffn1_post_g, mix_pre_g, w_in, b_forget, sgu_ln_g, sgu_ln_b, sgu_w_s, sgu_b_s, w_out, mix_post_g, ffn2_pre_g, ffn2_w_gate, ffn2_w_up, ffn2_w_down, ffn2_post_g, loss_target, m_ffn1_pre_g, m_ffn1_w_gate, m_ffn1_w_up, m_ffn1_w_down, m_ffn1_post_g, m_mix_pre_g, m_w_in, m_b_forget, m_sgu_ln_g, m_sgu_ln_b, m_sgu_w_s, m_sgu_b_s, m_w_out, m_mix_post_g, m_ffn2_pre_g, m_ffn2_w_gate, m_ffn2_w_up, m_ffn2_w_down, m_ffn2_post_g, v_ffn1_pre_g, v_ffn1_w_gate, v_ffn1_w_up, v_ffn1_w_down, v_ffn1_post_g, v_mix_pre_g, v_w_in, v_b_forget, v_sgu_ln_g, v_sgu_ln_b, v_sgu_w_s, v_sgu_b_s, v_w_out, v_mix_post_g, v_ffn2_pre_g, v_ffn2_w_gate, v_ffn2_w_up, v_ffn2_w_down, v_ffn2_post_g):
    args = (x, ffn1_pre_g, ffn1_w_gate, ffn1_w_up, ffn1_w_down, ffn1_post_g, mix_pre_g, w_in, b_forget, sgu_ln_g, sgu_ln_b, sgu_w_s, sgu_b_s, w_out, mix_post_g, ffn2_pre_g, ffn2_w_gate, ffn2_w_up, ffn2_w_down, ffn2_post_g, loss_target, m_ffn1_pre_g, m_ffn1_w_gate, m_ffn1_w_up, m_ffn1_w_down, m_ffn1_post_g, m_mix_pre_g, m_w_in, m_b_forget, m_sgu_ln_g, m_sgu_ln_b, m_sgu_w_s, m_sgu_b_s, m_w_out, m_mix_post_g, m_ffn2_pre_g, m_ffn2_w_gate, m_ffn2_w_up, m_ffn2_w_down, m_ffn2_post_g, v_ffn1_pre_g, v_ffn1_w_gate, v_ffn1_w_up, v_ffn1_w_down, v_ffn1_post_g, v_mix_pre_g, v_w_in, v_b_forget, v_sgu_ln_g, v_sgu_ln_b, v_sgu_w_s, v_sgu_b_s, v_w_out, v_mix_post_g, v_ffn2_pre_g, v_ffn2_w_gate, v_ffn2_w_up, v_ffn2_w_down, v_ffn2_post_g)
    named = dict(zip(_ARG_NAMES, args))
    tile = min(512, x.shape[1])
    return _step(named, tile, tile)
```

```python
import functools
import math

import jax
import jax.numpy as jnp
from jax import lax
from jax.experimental import pallas as pl
from jax.experimental.pallas import tpu as pltpu

D = 1024
F = 4096
H = 8
HD = 128
G = 8
L = 128
CHUNK = 64
NSH = 4
NDEV = 8
ZW = 7 * D
RMS_EPS = 1e-6
LN_EPS = 1e-5
NEG = -1e30
SCALE = 1.0 / math.sqrt(HD)
LOG2E = math.log2(math.e)
LN2 = math.log(2.0)

ADAM_LR = 0.001
ADAM_B1 = 0.9
ADAM_B2 = 0.999
ADAM_EPS = 1e-08
ADAM_WD = 0.01
ADAM_STEP = 10

VMEM_LIMIT_BYTES = 56 * 1024 * 1024

WIN_SH = 1794
FFN_ROWS = 3 * D
MIX_ROWS = 256 + 2 * D
G2_ROWS = FFN_ROWS + MIX_ROWS
SMALL_STRIDE = 8
SMALL_ROWS = 10 * SMALL_STRIDE + L

BF = jnp.bfloat16
F32 = jnp.float32
MESH = pl.DeviceIdType.MESH


def _params(n_grid):
    return pltpu.CompilerParams(dimension_semantics=("arbitrary",) * n_grid,
                                vmem_limit_bytes=VMEM_LIMIT_BYTES)


def _dot(a, b):
    return jnp.dot(a, b, preferred_element_type=F32)


def _dot_nt(a, b):
    return lax.dot_general(a, b, (((1,), (1,)), ((), ())), preferred_element_type=F32)


def _dot_tn(a, b):
    return lax.dot_general(a, b, (((0,), (0,)), ((), ())), preferred_element_type=F32)


def _rms(x, g):
    r = lax.rsqrt(jnp.mean(x * x, axis=-1, keepdims=True) + RMS_EPS)
    return x * r * g


def _rms_bwd(dn, x, g):
    r = lax.rsqrt(jnp.mean(x * x, axis=-1, keepdims=True) + RMS_EPS)
    xr = x * r
    dg = jnp.sum(dn * xr, axis=0, keepdims=True)
    t = dn * g
    dx = r * (t - xr * jnp.mean(t * xr, axis=-1, keepdims=True))
    return dx, dg


def _gelu_parts(x):
    cdf = 0.5 * (1.0 + lax.erf(x * (1.0 / math.sqrt(2.0))))
    pdf = jnp.exp(-0.5 * x * x) * (1.0 / math.sqrt(2.0 * math.pi))
    return x * cdf, cdf + x * pdf


def _gelu(x):
    return x * (0.5 * (1.0 + lax.erf(x * (1.0 / math.sqrt(2.0)))))


def _sigmoid(x):
    return 1.0 / (1.0 + jnp.exp(-x))


def _ffn_fwd(x, g_pre, wpack, g_post, tm, gather=None):
    S = x.shape[0]
    nt, nf, tf = S // tm, NSH, D

    def body(x_ref, gpre_ref, wg_ref, wu_ref, wd_ref, gpost_ref, *rest):
        if gather is None:
            h_ref, a_ref, b_ref, y_ref, xo_ref, h_s, acc = rest
        else:
            wp_ref, h_ref, a_ref, b_ref, y_ref, xo_ref, g_ref, h_s, acc, send_sems, recv_sems = rest
        i = pl.program_id(0)
        j = pl.program_id(1)

        if gather is not None:
            @pl.when((i == 0) & (j == 0))
            def _():
                _gather_start(wp_ref, g_ref, send_sems, recv_sems)

        @pl.when(j == 0)
        def _():
            h = _rms(x_ref[...], gpre_ref[...]).astype(BF)
            h_s[...] = h
            h_ref[...] = h
            acc[...] = jnp.zeros_like(acc)

        h = h_s[...]
        a = _dot(h, wg_ref[...])
        b = _dot(h, wu_ref[...])
        a_ref[...] = a.astype(BF)
        b_ref[...] = b.astype(BF)
        act = (a * _sigmoid(a)) * b
        acc[...] += _dot(act.astype(BF), wd_ref[...])

        @pl.when(j == nf - 1)
        def _():
            y = acc[...]
            y_ref[...] = y
            xo_ref[...] = x_ref[...] + 0.5 * _rms(y, gpost_ref[...])

        if gather is not None:
            @pl.when((i == nt - 1) & (j == nf - 1))
            def _():
                _gather_finish(wp_ref, g_ref, send_sems, recv_sems)

    row = pl.BlockSpec((tm, D), lambda i, j: (i, 0))
    vec = pl.BlockSpec((1, D), lambda i, j: (0, 0))
    anywhere = pl.BlockSpec(memory_space=pl.ANY)
    in_specs = [row, vec,
                pl.BlockSpec((None, D, tf), lambda i, j: (j, 0, 0)),
                pl.BlockSpec((None, D, tf), lambda i, j: (j, 1, 0)),
                pl.BlockSpec((None, tf, D), lambda i, j: (j, 2, 0)),
                vec]
    out_specs = [row,
                 pl.BlockSpec((tm, tf), lambda i, j: (i, j)),
                 pl.BlockSpec((tm, tf), lambda i, j: (i, j)),
                 row, row]
    out_shape = [jax.ShapeDtypeStruct((S, D), BF),
                 jax.ShapeDtypeStruct((S, F), BF),
                 jax.ShapeDtypeStruct((S, F), BF),
                 jax.ShapeDtypeStruct((S, D), F32),
                 jax.ShapeDtypeStruct((S, D), F32)]
    scratch = [pltpu.VMEM((tm, D), BF), pltpu.VMEM((tm, D), F32)]
    args = [x, g_pre, wpack, wpack, wpack, g_post]
    if gather is not None:
        in_specs.append(anywhere)
        out_specs.append(anywhere)
        out_shape.append(jax.ShapeDtypeStruct((NSH,) + gather.shape, gather.dtype))
        scratch += [pltpu.SemaphoreType.DMA((6,)), pltpu.SemaphoreType.DMA((6,))]
        args.append(gather)
    res = list(pl.pallas_call(
        body, name="ffn_fwd" if gather is None else "ffn_fwd_gather",
        grid=(nt, nf),
        in_specs=in_specs, out_specs=out_specs, out_shape=out_shape, scratch_shapes=scratch,
        compiler_params=_params(2),
    )(*args))
    if gather is not None:
        res[5] = _place_own_shard(res[5], gather)
    return res


def _ffn_bwd(dxo, y, g_post, a, b, wpack, x_in, g_pre, tm, scatter=None):
    S = dxo.shape[0]
    nt, nf, tf = S // tm, NSH, D

    def body(dxo_ref, y_ref, gpost_ref, a_ref, b_ref, wg_ref, wu_ref, wd_ref, xin_ref, gpre_ref, *rest):
        if scatter is None:
            dy_ref, da_ref, db_ref, act_ref, dxin_ref, dgpost_ref, dgpre_ref, dy_s, acc = rest
        else:
            (q_ref, dy_ref, da_ref, db_ref, act_ref, dxin_ref, dgpost_ref, dgpre_ref, t_ref,
             dy_s, acc, send_sems, recv_sems) = rest
        i = pl.program_id(0)
        j = pl.program_id(1)

        @pl.when((i == 0) & (j == 0))
        def _():
            dgpost_ref[...] = jnp.zeros_like(dgpost_ref)
            dgpre_ref[...] = jnp.zeros_like(dgpre_ref)
            if scatter is not None:
                _scatter_start(q_ref, t_ref, send_sems, recv_sems)

        @pl.when(j == 0)
        def _():
            dy, dg = _rms_bwd(0.5 * dxo_ref[...], y_ref[...], gpost_ref[...])
            dyb = dy.astype(BF)
            dy_s[...] = dyb
            dy_ref[...] = dyb
            dgpost_ref[...] += dg
            acc[...] = jnp.zeros_like(acc)

        dact = _dot_nt(dy_s[...], wd_ref[...])
        av = a_ref[...].astype(F32)
        bv = b_ref[...].astype(F32)
        sig = _sigmoid(av)
        sl = av * sig
        act_ref[...] = (sl * bv).astype(BF)
        dbb = (dact * sl).astype(BF)
        dab = (dact * bv * (sig * (1.0 + av * (1.0 - sig)))).astype(BF)
        da_ref[...] = dab
        db_ref[...] = dbb
        acc[...] += _dot_nt(dab, wg_ref[...]) + _dot_nt(dbb, wu_ref[...])

        @pl.when(j == nf - 1)
        def _():
            dx, dg = _rms_bwd(acc[...], xin_ref[...], gpre_ref[...])
            dxin_ref[...] = dxo_ref[...] + dx
            dgpre_ref[...] += dg

        if scatter is not None:
            @pl.when((i == nt - 1) & (j == nf - 1))
            def _():
                _scatter_finish(q_ref, t_ref, send_sems, recv_sems)

    row = pl.BlockSpec((tm, D), lambda i, j: (i, 0))
    vec = pl.BlockSpec((1, D), lambda i, j: (0, 0))
    ff = pl.BlockSpec((tm, tf), lambda i, j: (i, j))
    anywhere = pl.BlockSpec(memory_space=pl.ANY)
    in_specs = [row, row, vec, ff, ff,
                pl.BlockSpec((None, D, tf), lambda i, j: (j, 0, 0)),
                pl.BlockSpec((None, D, tf), lambda i, j: (j, 1, 0)),
                pl.BlockSpec((None, tf, D), lambda i, j: (j, 2, 0)),
                row, vec]
    out_specs = [row, ff, ff, ff, row, vec, vec]
    out_shape = [jax.ShapeDtypeStruct((S, D), BF),
                 jax.ShapeDtypeStruct((S, F), BF),
                 jax.ShapeDtypeStruct((S, F), BF),
                 jax.ShapeDtypeStruct((S, F), BF),
                 jax.ShapeDtypeStruct((S, D), F32),
                 jax.ShapeDtypeStruct((1, D), F32),
                 jax.ShapeDtypeStruct((1, D), F32)]
    scratch = [pltpu.VMEM((tm, D), BF), pltpu.VMEM((tm, D), F32)]
    args = [dxo, y, g_post, a, b, wpack, wpack, wpack, x_in, g_pre]
    if scatter is not None:
        in_specs.append(anywhere)
        out_specs.append(anywhere)
        out_shape.append(jax.ShapeDtypeStruct((3,) + scatter.shape[1:], scatter.dtype))
        scratch += [pltpu.SemaphoreType.DMA((3,)), pltpu.SemaphoreType.DMA((3,))]
        args.append(scatter)
    return pl.pallas_call(
        body, name="ffn_bwd" if scatter is None else "ffn_bwd_scatter",
        grid=(nt, nf),
        in_specs=in_specs, out_specs=out_specs, out_shape=out_shape, scratch_shapes=scratch,
        compiler_params=_params(2),
    )(*args)


def _mm_tn(a, b, bm, bn, bt, col_blocked=False):
    S, M = a.shape
    N = b.shape[1]
    nt = S // bt

    def body(a_ref, b_ref, o_ref):
        t = pl.program_id(2)

        @pl.when(t == 0)
        def _():
            o_ref[...] = jnp.zeros_like(o_ref)

        o_ref[...] += _dot_tn(a_ref[...], b_ref[...])

    if col_blocked:
        out_spec = pl.BlockSpec((None, bm, bn), lambda m, n, t: (n, m, 0))
        out_shape = jax.ShapeDtypeStruct((N // bn, M, bn), F32)
    else:
        out_spec = pl.BlockSpec((bm, bn), lambda m, n, t: (m, n))
        out_shape = jax.ShapeDtypeStruct((M, N), F32)
    return pl.pallas_call(
        body, name="mm_tn",
        grid=(M // bm, N // bn, nt),
        in_specs=[pl.BlockSpec((bt, bm), lambda m, n, t: (t, m)),
                  pl.BlockSpec((bt, bn), lambda m, n, t: (t, n))],
        out_specs=out_spec, out_shape=out_shape,
        compiler_params=_params(3),
    )(a, b)


def _mm(a, w, tm, tn, out_dtype, first_block_scale=1.0):
    S, K = a.shape
    N = w.shape[1]

    def body(a_ref, w_ref, o_ref):
        r = _dot(a_ref[...], w_ref[...])
        if first_block_scale != 1.0:
            r = r * jnp.where(pl.program_id(1) == 0, first_block_scale, 1.0)
        o_ref[...] = r.astype(out_dtype)

    return pl.pallas_call(
        body, name="mm",
        grid=(S // tm, N // tn),
        in_specs=[pl.BlockSpec((tm, K), lambda i, j: (i, 0)),
                  pl.BlockSpec((K, tn), lambda i, j: (0, j))],
        out_specs=pl.BlockSpec((tm, tn), lambda i, j: (i, j)),
        out_shape=jax.ShapeDtypeStruct((S, N), out_dtype),
        compiler_params=_params(2),
    )(a, w)


def _norm_fwd(x, g, tm):
    S = x.shape[0]

    def body(x_ref, g_ref, h_ref):
        h_ref[...] = _rms(x_ref[...], g_ref[...]).astype(BF)

    return pl.pallas_call(
        body, name="norm_fwd",
        grid=(S // tm,),
        in_specs=[pl.BlockSpec((tm, D), lambda i: (i, 0)), pl.BlockSpec((1, D), lambda i: (0, 0))],
        out_specs=pl.BlockSpec((tm, D), lambda i: (i, 0)),
        out_shape=jax.ShapeDtypeStruct((S, D), BF),
        compiler_params=_params(1),
    )(x, g)


def _mix_in_bwd(dz, wcat, dzf, wf, x1, g, dx2, tm):
    S = dz.shape[0]
    nk = 2
    kb = ZW // nk

    def body(dz_ref, w_ref, dzf_ref, wf_ref, x_ref, g_ref, dx2_ref, dx1_ref, dg_ref, acc):
        i = pl.program_id(0)
        k = pl.program_id(1)

        @pl.when((i == 0) & (k == 0))
        def _():
            dg_ref[...] = jnp.zeros_like(dg_ref)

        @pl.when(k == 0)
        def _():
            acc[...] = _dot_nt(dzf_ref[...], wf_ref[...])

        acc[...] += _dot_nt(dz_ref[...], w_ref[...])

        @pl.when(k == nk - 1)
        def _():
            dx, dg = _rms_bwd(acc[...], x_ref[...], g_ref[...])
            dx1_ref[...] = dx2_ref[...] + dx
            dg_ref[...] += dg

    row = pl.BlockSpec((tm, D), lambda i, k: (i, 0))
    vec = pl.BlockSpec((1, D), lambda i, k: (0, 0))
    return pl.pallas_call(
        body, name="mix_in_bwd",
        grid=(S // tm, nk),
        in_specs=[pl.BlockSpec((tm, kb), lambda i, k: (i, k)),
                  pl.BlockSpec((D, kb), lambda i, k: (0, k)),
                  pl.BlockSpec((tm, HD), lambda i, k: (i, 0)),
                  pl.BlockSpec((D, HD), lambda i, k: (0, 0)),
                  row, vec, row],
        out_specs=[row, vec],
        out_shape=[jax.ShapeDtypeStruct((S, D), F32), jax.ShapeDtypeStruct((1, D), F32)],
        scratch_shapes=[pltpu.VMEM((tm, D), F32)],
        compiler_params=_params(2),
    )(dz, wcat, dzf, wf, x1, g, dx2)


def _scan_rows(blk, reverse):
    n = blk.shape[0]
    row = lax.broadcasted_iota(jnp.int32, blk.shape, 0)
    d = 1
    while d < n:
        if reverse:
            blk = blk + jnp.where(row < n - d, pltpu.roll(blk, n - d, 0), 0.0)
        else:
            blk = blk + jnp.where(row >= d, pltpu.roll(blk, d, 0), 0.0)
        d *= 2
    return blk


def _forget_cumsum(zf, bf, cs):
    S = zf.shape[0]

    def body(zf_ref, bf_ref, c_ref, cb_ref, carry):
        @pl.when(pl.program_id(0) == 0)
        def _():
            carry[...] = jnp.zeros_like(carry)

        x = zf_ref[...] + bf_ref[...]
        logf = jnp.minimum(x, 0.0) - jnp.log1p(jnp.exp(-jnp.abs(x)))
        sc = _scan_rows(logf, False) + carry[...]
        carry[...] = sc[cs - 1:cs, :]
        sc = sc * LOG2E
        c_ref[...] = sc
        for h in range(H):
            cb_ref[h] = jnp.broadcast_to(sc[:, h:h + 1], (cs, HD))

    return pl.pallas_call(
        body, name="forget_cumsum",
        grid=(S // cs,),
        in_specs=[pl.BlockSpec((cs, HD), lambda i: (i, 0)), pl.BlockSpec((1, HD), lambda i: (0, 0))],
        out_specs=[pl.BlockSpec((cs, HD), lambda i: (i, 0)),
                   pl.BlockSpec((H, cs, HD), lambda i: (0, i, 0))],
        out_shape=[jax.ShapeDtypeStruct((S, HD), F32), jax.ShapeDtypeStruct((H, S, HD), F32)],
        scratch_shapes=[pltpu.VMEM((1, HD), F32)],
        compiler_params=_params(1),
    )(zf, bf)


def _forget_bwd(dc, zf, bf, cs):
    S = dc.shape[0]
    nc = S // cs

    def body(dc_ref, zf_ref, bf_ref, dzf_ref, dbf_ref, carry):
        @pl.when(pl.program_id(0) == 0)
        def _():
            carry[...] = jnp.zeros_like(carry)
            dbf_ref[...] = jnp.zeros_like(dbf_ref)

        sc = _scan_rows(dc_ref[...], True) + carry[...]
        carry[...] = sc[0:1, :]
        x = zf_ref[...] + bf_ref[...]
        dz = sc * _sigmoid(-x)
        dzf_ref[...] = dz.astype(BF)
        dbf_ref[...] += jnp.sum(dz, axis=0, keepdims=True)

    rev = pl.BlockSpec((cs, HD), lambda i: (nc - 1 - i, 0))
    vec = pl.BlockSpec((1, HD), lambda i: (0, 0))
    return pl.pallas_call(
        body, name="forget_bwd",
        grid=(nc,),
        in_specs=[rev, rev, vec],
        out_specs=[rev, vec],
        out_shape=[jax.ShapeDtypeStruct((S, HD), BF), jax.ShapeDtypeStruct((1, HD), F32)],
        scratch_shapes=[pltpu.VMEM((1, HD), F32)],
        compiler_params=_params(1),
    )(dc, zf, bf)


def _lanes(x, n):
    return x if n == HD else jnp.concatenate([x] * (n // HD), axis=1)


def _causal_mask(i, j, t, rows_are_queries):
    r = lax.broadcasted_iota(jnp.int32, (t, t), 0)
    c = lax.broadcasted_iota(jnp.int32, (t, t), 1)
    if rows_are_queries:
        return (j * t + c) <= (i * t + r)
    return (j * t + r) <= (i * t + c)


def _fox_fwd(z, ccol_b, crow, t):
    S = z.shape[0]
    nq = S // t

    def body(q_ref, kv_ref, cc_ref, cr_ref, o_ref, lse_ref, m_s, acc_s, s_a, s_b):
        i = pl.program_id(1)
        ct = cc_ref[...]
        ones = jnp.ones((t, HD), BF)
        m_s[...] = jnp.full_like(m_s, NEG)
        acc_s[...] = jnp.zeros_like(acc_s)

        def scores(j, s_ref):
            off = pl.multiple_of(j * t, t)
            s_ref[...] = _dot_nt(q_ref[...], kv_ref[pl.ds(off, t), :HD]) - cr_ref[pl.ds(j, 1), :]

        def consume(j, s_ref, masked):
            off = pl.multiple_of(j * t, t)
            v1 = jnp.concatenate([kv_ref[pl.ds(off, t), HD:], ones], axis=1)
            s = s_ref[...]
            if masked:
                s = jnp.where(_causal_mask(i, j, t, True), s, NEG)
            m_old = m_s[...]
            m_new = jnp.maximum(m_old, jnp.max(s, axis=1, keepdims=True))
            p = jnp.exp2(s - _lanes(m_new, t))
            alpha = jnp.exp2(m_old - m_new)
            acc_s[...] = _lanes(alpha, 2 * HD) * acc_s[...] + _dot(p.astype(BF), v1)
            m_s[...] = m_new

        scores(0, s_a)

        def pair(jj, carry):
            j = 2 * jj
            scores(j + 1, s_b)
            consume(j, s_a, False)
            scores(j + 2, s_a)
            consume(j + 1, s_b, False)
            return carry

        lax.fori_loop(0, i // 2, pair, 0)

        @pl.when(i % 2 == 0)
        def _():
            consume(i, s_a, True)

        @pl.when(i % 2 == 1)
        def _():
            scores(i, s_b)
            consume(i - 1, s_a, False)
            consume(i, s_b, True)

        l = acc_s[:, HD:]
        o_ref[...] = (acc_s[:, :HD] / l).astype(BF)
        lse_ref[...] = m_s[...] + ct + jnp.log2(l)

    return pl.pallas_call(
        body, name="fox_fwd",
        grid=(H, nq),
        in_specs=[pl.BlockSpec((t, HD), lambda h, i: (i, h)),
                  pl.BlockSpec((S, 2 * HD), lambda h, i: (0, 4 + h)),
                  pl.BlockSpec((None, t, HD), lambda h, i: (h, i, 0)),
                  pl.BlockSpec((None, nq, t), lambda h, i: (h, 0, 0))],
        out_specs=[pl.BlockSpec((t, HD), lambda h, i: (i, h)),
                   pl.BlockSpec((None, t, HD), lambda h, i: (h, i, 0))],
        out_shape=[jax.ShapeDtypeStruct((S, D), BF), jax.ShapeDtypeStruct((H, S, HD), F32)],
        scratch_shapes=[pltpu.VMEM((t, HD), F32), pltpu.VMEM((t, 2 * HD), F32),
                        pltpu.VMEM((t, t), F32), pltpu.VMEM((t, t), F32)],
        compiler_params=_params(2),
    )(z, z, ccol_b, crow)


def _fox_bwd(z, do, o, lse_b, ccol_b, crow, dz, t):
    S = z.shape[0]
    nq = S // t

    def body(q_ref, kv_ref, do_ref, o_ref, lse_ref, cc_ref, cr_ref, dz_in,
             dq_ref, dkt_ref, dvt_ref, dck_ref, dcq_ref, acc_s, r_s):
        del dz_in
        i = pl.program_id(1)

        @pl.when(i == 0)
        def _():
            dkt_ref[...] = jnp.zeros_like(dkt_ref)
            dvt_ref[...] = jnp.zeros_like(dvt_ref)
            dck_ref[...] = jnp.zeros_like(dck_ref)

        q = q_ref[...]
        dout = do_ref[...]
        qt = jnp.transpose(q.astype(F32)).astype(BF)
        dot_ = jnp.transpose(dout.astype(F32)).astype(BF)
        off_t = _lanes(lse_ref[...] - cc_ref[...], t)
        delta = jnp.sum(dout.astype(F32) * o_ref[...].astype(F32), axis=1, keepdims=True)
        delta = _lanes(jnp.broadcast_to(delta, (t, HD)), t)
        acc_s[...] = jnp.zeros_like(acc_s)
        r_s[...] = jnp.zeros_like(r_s)

        def step(j, masked):
            off = pl.multiple_of(j * t, t)
            k = kv_ref[pl.ds(off, t), :HD]
            v = kv_ref[pl.ds(off, t), HD:]
            p = jnp.exp2(_dot_nt(q, k) - cr_ref[pl.ds(j, 1), :] - off_t)
            if masked:
                p = jnp.where(_causal_mask(i, j, t, True), p, 0.0)
            ds = p * (_dot_nt(dout, v) - delta)
            dsb = ds.astype(BF)
            acc_s[...] += _dot(dsb, k)
            dkt_ref[j] += _dot(qt, dsb)
            dvt_ref[j] += _dot(dot_, p.astype(BF))
            dck_ref[pl.ds(j, 1), :] += jnp.sum(ds, axis=0, keepdims=True)
            r_s[...] += jnp.sum(ds, axis=1, keepdims=True)

        def full_step(j, carry):
            step(j, False)
            return carry

        lax.fori_loop(0, i, full_step, 0)
        step(i, True)
        dq_ref[...] = (acc_s[...] * SCALE).astype(BF)
        dcq_ref[...] = jnp.transpose(r_s[...])[0:1, :]

    qspec = pl.BlockSpec((t, HD), lambda h, i: (i, h))
    bspec = pl.BlockSpec((None, t, HD), lambda h, i: (h, i, 0))
    rows = pl.BlockSpec((None, nq, t), lambda h, i: (h, 0, 0))
    tspec = pl.BlockSpec((None, nq, HD, t), lambda h, i: (h, 0, 0, 0))
    tshape = jax.ShapeDtypeStruct((H, nq, HD, t), F32)
    return pl.pallas_call(
        body, name="fox_bwd",
        grid=(H, nq),
        in_specs=[qspec,
                  pl.BlockSpec((S, 2 * HD), lambda h, i: (0, 4 + h)),
                  qspec, qspec, bspec, bspec, rows,
                  pl.BlockSpec(memory_space=pl.ANY)],
        out_specs=[qspec, tspec, tspec, rows, pl.BlockSpec((None, None, 1, t), lambda h, i: (h, i, 0, 0))],
        out_shape=[jax.ShapeDtypeStruct((S, ZW), BF), tshape, tshape,
                   jax.ShapeDtypeStruct((H, nq, t), F32), jax.ShapeDtypeStruct((H, nq, 1, t), F32)],
        scratch_shapes=[pltpu.VMEM((t, HD), F32), pltpu.VMEM((t, HD), F32)],
        input_output_aliases={7: 0},
        compiler_params=_params(2),
    )(z, z, do, o, lse_b, ccol_b, crow, dz)


def _fox_bwd_finish(dkt, dvt, dz, t):
    nq = dkt.shape[1]
    S = nq * t

    def body(dkt_ref, dvt_ref, dz_in, dkv_ref):
        del dz_in
        dkv_ref[:, :HD] = (jnp.transpose(dkt_ref[...]) * LN2).astype(BF)
        dkv_ref[:, HD:] = jnp.transpose(dvt_ref[...]).astype(BF)

    tspec = pl.BlockSpec((None, None, HD, t), lambda h, j: (h, j, 0, 0))
    return pl.pallas_call(
        body, name="fox_bwd_finish",
        grid=(H, nq),
        in_specs=[tspec, tspec, pl.BlockSpec(memory_space=pl.ANY)],
        out_specs=pl.BlockSpec((t, 2 * HD), lambda h, j: (j, 4 + h)),
        out_shape=jax.ShapeDtypeStruct((S, ZW), BF),
        input_output_aliases={2: 0},
        compiler_params=_params(2),
    )(dkt, dvt, dz)


def _sgu_mask(transposed):
    r = lax.broadcasted_iota(jnp.int32, (L, L), 0)
    c = lax.broadcasted_iota(jnp.int32, (L, L), 1)
    if transposed:
        return (r // CHUNK) <= (c // CHUNK)
    return (c // CHUNK) <= (r // CHUNK)


def _ln_group(vs, lng, lnb):
    mu = jnp.mean(vs, axis=-1, keepdims=True)
    xc = vs - mu
    rstd = lax.rsqrt(jnp.mean(xc * xc, axis=-1, keepdims=True) + LN_EPS)
    xhat = xc * rstd
    return xhat, rstd, xhat * lng + lnb


def _mix_out_fwd(z, oa, ln_g, ln_b, ws, bst, wout, x1, g_post, tm):
    S = z.shape[0]
    nw = tm // L

    def body(u_ref, sv_ref, ga_ref, gb_ref, oa_ref, lng_ref, lnb_ref, ws_ref, bst_ref, wout_ref, x1_ref, gp_ref,
             mg_ref, y_ref, x2_ref, mg_s):
        mask = _sgu_mask(False)
        for g in range(G):
            cols = slice(g * L, (g + 1) * L)
            wm = jnp.where(mask, ws_ref[g], 0.0).astype(BF)
            bcol = bst_ref[:, g:g + 1]
            lng = lng_ref[:, cols]
            lnb = lnb_ref[:, cols]
            for w in range(nw):
                rows = slice(w * L, (w + 1) * L)
                vs = _gelu(sv_ref[rows, cols].astype(F32))
                _, _, vn = _ln_group(vs, lng, lnb)
                mixed = _dot(wm, vn.astype(BF)) + bcol
                ob = _gelu(u_ref[rows, cols].astype(F32)) * mixed
                mg = (_sigmoid(ga_ref[rows, cols].astype(F32)) * oa_ref[rows, cols].astype(F32)
                      + _sigmoid(gb_ref[rows, cols].astype(F32)) * ob)
                mg_s[rows, cols] = mg.astype(BF)
        mgb = mg_s[...]
        mg_ref[...] = mgb
        y = _dot(mgb, wout_ref[...])
        y_ref[...] = y
        x2_ref[...] = x1_ref[...] + _rms(y, gp_ref[...])

    row = pl.BlockSpec((tm, D), lambda i: (i, 0))
    vec = pl.BlockSpec((1, D), lambda i: (0, 0))

    def zcol(kb):
        return pl.BlockSpec((tm, D), lambda i: (i, kb))

    return pl.pallas_call(
        body, name="mix_out_fwd",
        grid=(S // tm,),
        in_specs=[zcol(3), zcol(4), zcol(5), zcol(6), row, vec, vec,
                  pl.BlockSpec((G, L, L), lambda i: (0, 0, 0)),
                  pl.BlockSpec((L, HD), lambda i: (0, 0)),
                  pl.BlockSpec((D, D), lambda i: (0, 0)),
                  row, vec],
        out_specs=[row, row, row],
        out_shape=[jax.ShapeDtypeStruct((S, D), BF),
                   jax.ShapeDtypeStruct((S, D), F32),
                   jax.ShapeDtypeStruct((S, D), F32)],
        scratch_shapes=[pltpu.VMEM((tm, D), BF)],
        compiler_params=_params(1),
    )(z, z, z, z, oa, ln_g, ln_b, ws, bst, wout, x1, g_post)


def _mix_out_bwd(dx2, y2, g_post, wout, z, oa, ln_g, ln_b, ws, wst, bst, tm, scatter):
    S = z.shape[0]
    nw = tm // L

    def body(dx2_ref, y_ref, gp_ref, wout_ref, u_ref, sv_ref, ga_ref, gb_ref, oa_ref, lng_ref, lnb_ref,
             ws_ref, wst_ref, bst_ref, q_ref,
             dz_ref, dy_ref, doa_ref, dgp_ref, dlng_ref, dlnb_ref, dws_ref, dbst_ref, t_ref,
             dzg_s, dm_s, send_sems, recv_sems):
        i = pl.program_id(0)
        c = pl.program_id(1)

        @pl.when((i == 0) & (c == 0))
        def _():
            _scatter_start(q_ref, t_ref, send_sems, recv_sems)
            dgp_ref[...] = jnp.zeros_like(dgp_ref)
            dlng_ref[...] = jnp.zeros_like(dlng_ref)
            dlnb_ref[...] = jnp.zeros_like(dlnb_ref)
            dws_ref[...] = jnp.zeros_like(dws_ref)
            dbst_ref[...] = jnp.zeros_like(dbst_ref)

        @pl.when(c == 0)
        def _():
            dy, dg = _rms_bwd(dx2_ref[...], y_ref[...], gp_ref[...])
            dyb = dy.astype(BF)
            dy_ref[...] = dyb
            dgp_ref[...] += dg
            dm_s[...] = _dot_nt(dyb, wout_ref[...])
            mask = _sgu_mask(False)
            mask_t = _sgu_mask(True)
            lane = lax.broadcasted_iota(jnp.int32, (L, HD), 1)
            for g in range(G):
                cols = slice(g * L, (g + 1) * L)
                wm = jnp.where(mask, ws_ref[g], 0.0).astype(BF)
                wmt = jnp.where(mask_t, wst_ref[g], 0.0).astype(BF)
                bcol = bst_ref[:, g:g + 1]
                lng = lng_ref[:, cols]
                lnb = lnb_ref[:, cols]
                dws_g = jnp.zeros((L, L), F32)
                dbs_g = jnp.zeros((L, 1), F32)
                dlng_g = jnp.zeros((1, L), F32)
                dlnb_g = jnp.zeros((1, L), F32)
                for w in range(nw):
                    rows = slice(w * L, (w + 1) * L)
                    dm = dm_s[rows, cols]
                    vs, dvs_dz = _gelu_parts(sv_ref[rows, cols].astype(F32))
                    xhat, rstd, vn = _ln_group(vs, lng, lnb)
                    vnb = vn.astype(BF)
                    mixed = _dot(wm, vnb) + bcol
                    u, du_dz = _gelu_parts(u_ref[rows, cols].astype(F32))
                    sga = _sigmoid(ga_ref[rows, cols].astype(F32))
                    sgb = _sigmoid(gb_ref[rows, cols].astype(F32))
                    oav = oa_ref[rows, cols].astype(F32)
                    ob = u * mixed
                    doa_ref[rows, cols] = (dm * sga).astype(BF)
                    dzg_s[2, rows, cols] = (dm * oav * sga * (1.0 - sga)).astype(BF)
                    dzg_s[3, rows, cols] = (dm * ob * sgb * (1.0 - sgb)).astype(BF)
                    dob = dm * sgb
                    dzg_s[0, rows, cols] = (dob * mixed * du_dz).astype(BF)
                    dmixed = dob * u
                    dmb = dmixed.astype(BF)
                    dbs_g += jnp.sum(dmixed, axis=1, keepdims=True)
                    dws_g += _dot_nt(dmb, vnb)
                    dvn = _dot(wmt, dmb)
                    dlng_g += jnp.sum(dvn * xhat, axis=0, keepdims=True)
                    dlnb_g += jnp.sum(dvn, axis=0, keepdims=True)
                    dxh = dvn * lng
                    dvs = rstd * (dxh - jnp.mean(dxh, axis=-1, keepdims=True)
                                  - xhat * jnp.mean(dxh * xhat, axis=-1, keepdims=True))
                    dzg_s[1, rows, cols] = (dvs * dvs_dz).astype(BF)
                dws_ref[g] += jnp.where(mask, dws_g, 0.0)
                dbst_ref[...] += jnp.where(lane == g, dbs_g, 0.0)
                dlng_ref[:, cols] += dlng_g
                dlnb_ref[:, cols] += dlnb_g

        dz_ref[...] = dzg_s[c]

        @pl.when((i == S // tm - 1) & (c == 3))
        def _():
            _scatter_finish(q_ref, t_ref, send_sems, recv_sems)

    row = pl.BlockSpec((tm, D), lambda i, c: (i, 0))
    vec = pl.BlockSpec((1, D), lambda i, c: (0, 0))
    wsspec = pl.BlockSpec((G, L, L), lambda i, c: (0, 0, 0))
    bspec = pl.BlockSpec((L, HD), lambda i, c: (0, 0))
    anywhere = pl.BlockSpec(memory_space=pl.ANY)

    def zcol(kb):
        return pl.BlockSpec((tm, D), lambda i, c: (i, kb))

    return pl.pallas_call(
        body, name="mix_out_bwd",
        grid=(S // tm, 4),
        in_specs=[row, row, vec, pl.BlockSpec((D, D), lambda i, c: (0, 0)),
                  zcol(3), zcol(4), zcol(5), zcol(6), row, vec, vec, wsspec, wsspec, bspec, anywhere],
        out_specs=[pl.BlockSpec((tm, D), lambda i, c: (i, 3 + c)),
                   row, row, vec, vec, vec, wsspec, bspec, anywhere],
        out_shape=[jax.ShapeDtypeStruct((S, ZW), BF),
                   jax.ShapeDtypeStruct((S, D), BF),
                   jax.ShapeDtypeStruct((S, D), BF),
                   jax.ShapeDtypeStruct((1, D), F32),
                   jax.ShapeDtypeStruct((1, D), F32),
                   jax.ShapeDtypeStruct((1, D), F32),
                   jax.ShapeDtypeStruct((G, L, L), F32),
                   jax.ShapeDtypeStruct((L, HD), F32),
                   jax.ShapeDtypeStruct((3,) + scatter.shape[1:], scatter.dtype)],
        scratch_shapes=[pltpu.VMEM((4, tm, D), BF), pltpu.VMEM((tm, D), F32),
                        pltpu.SemaphoreType.DMA((3,)), pltpu.SemaphoreType.DMA((3,))],
        compiler_params=_params(2),
    )(dx2, y2, g_post, wout, z, z, z, z, oa, ln_g, ln_b, ws, wst, bst, scatter)


def _loss_head(x3, target, tm):
    S = x3.shape[0]

    def body(x_ref, t_ref, dx_ref, loss_ref):
        @pl.when(pl.program_id(0) == 0)
        def _():
            loss_ref[...] = jnp.zeros_like(loss_ref)

        e = x_ref[...] - t_ref[...]
        dx_ref[...] = e * (1.0 / D)
        loss_ref[...] += jnp.sum(e * e) * (0.5 / D)

    row = pl.BlockSpec((tm, D), lambda i: (i, 0))
    return pl.pallas_call(
        body, name="loss_head",
        grid=(S // tm,),
        in_specs=[row, row],
        out_specs=[row, pl.BlockSpec((8, HD), lambda i: (0, 0))],
        out_shape=[jax.ShapeDtypeStruct((S, D), F32), jax.ShapeDtypeStruct((8, HD), F32)],
        compiler_params=_params(1),
    )(x3, target)


def _adamw(w, g, m, v, tr):
    R, C = w.shape

    def body(w_ref, g_ref, m_ref, v_ref, d_ref, nm_ref, nv_ref):
        gv = g_ref[...]
        m_new = ADAM_B1 * m_ref[...] + (1.0 - ADAM_B1) * gv
        v_new = ADAM_B2 * v_ref[...] + (1.0 - ADAM_B2) * (gv * gv)
        m_hat = m_new / (1.0 - ADAM_B1 ** ADAM_STEP)
        v_hat = v_new / (1.0 - ADAM_B2 ** ADAM_STEP)
        d_ref[...] = -ADAM_LR * (m_hat / (jnp.sqrt(v_hat) + ADAM_EPS) + ADAM_WD * w_ref[...])
        nm_ref[...] = m_new
        nv_ref[...] = v_new

    spec = pl.BlockSpec((tr, C), lambda i: (i, 0))
    shp = jax.ShapeDtypeStruct((R, C), F32)
    return pl.pallas_call(
        body, name="adamw",
        grid=(R // tr,),
        in_specs=[spec] * 4, out_specs=[spec] * 3, out_shape=[shp] * 3,
        compiler_params=_params(1),
    )(w, g, m, v)


def _mesh_pos():
    return lax.axis_index("x"), lax.axis_index("y"), lax.axis_index("c")


def _half(c, rows):
    return pl.ds(pl.multiple_of(c * rows, 16), rows)


def _other_chips(x, y):
    return [(1 - x, y), (x, 1 - y), (1 - x, 1 - y)]


def _remote(k, src, dst, to, send_sems, recv_sems):
    return pltpu.make_async_remote_copy(src_ref=src, dst_ref=dst, send_sem=send_sems.at[k],
                                        recv_sem=recv_sems.at[k], device_id=to, device_id_type=MESH)


def _gather_start(wp_ref, g_ref, send_sems, recv_sems):
    x, y, c = _mesh_pos()
    mine = _half(c, wp_ref.shape[0] // 2)
    for k, (px, py) in enumerate(_other_chips(x, y)):
        _remote(k, wp_ref.at[mine], g_ref.at[2 * x + y, mine], (px, py, c), send_sems, recv_sems).start()


def _gather_finish(wp_ref, g_ref, send_sems, recv_sems):
    x, y, c = _mesh_pos()
    sibling = (x, y, 1 - c)
    rows = wp_ref.shape[0] // 2
    mine, other = _half(c, rows), _half(1 - c, rows)
    chips = _other_chips(x, y)
    for k, (px, py) in enumerate(chips):
        land = g_ref.at[2 * px + py, mine]
        _remote(k, land, land, (px, py, c), send_sems, recv_sems).wait_recv()
        _remote(3 + k, land, land, sibling, send_sems, recv_sems).start()
    for k, (px, py) in enumerate(chips):
        land = g_ref.at[2 * px + py, other]
        _remote(3 + k, land, land, sibling, send_sems, recv_sems).wait_recv()
    for k, (px, py) in enumerate(chips):
        land = g_ref.at[2 * px + py, mine]
        _remote(k, wp_ref.at[mine], g_ref.at[2 * x + y, mine], (px, py, c), send_sems, recv_sems).wait_send()
        _remote(3 + k, land, land, sibling, send_sems, recv_sems).wait_send()


def _place_own_shard(g, wp):
    x, y, _ = _mesh_pos()
    return lax.dynamic_update_index_in_dim(g, wp, 2 * x + y, 0)


def _all_gather_weights(wp):
    def body(wp_ref, g_ref, send_sems, recv_sems):
        _gather_start(wp_ref, g_ref, send_sems, recv_sems)
        _gather_finish(wp_ref, g_ref, send_sems, recv_sems)

    g = pl.pallas_call(
        body, name="all_gather_weights",
        in_specs=[pl.BlockSpec(memory_space=pl.ANY)],
        out_specs=pl.BlockSpec(memory_space=pl.ANY),
        out_shape=jax.ShapeDtypeStruct((NSH,) + wp.shape, wp.dtype),
        scratch_shapes=[pltpu.SemaphoreType.DMA((6,)), pltpu.SemaphoreType.DMA((6,))],
        compiler_params=pltpu.CompilerParams(has_side_effects=True),
    )(wp)
    return _place_own_shard(g, wp)


def _scatter_copies(q_ref, t_ref, send_sems, recv_sems):
    x, y, c = _mesh_pos()
    return [_remote(k, q_ref.at[2 * px + py], t_ref.at[k], (px, py, c), send_sems, recv_sems)
            for k, (px, py) in enumerate(_other_chips(x, y))]


def _scatter_start(q_ref, t_ref, send_sems, recv_sems):
    for cp in _scatter_copies(q_ref, t_ref, send_sems, recv_sems):
        cp.start()


def _scatter_finish(q_ref, t_ref, send_sems, recv_sems):
    for cp in _scatter_copies(q_ref, t_ref, send_sems, recv_sems):
        cp.wait()


def _pair_exchange(p):
    rows = p.shape[1] // 2

    def body(p_ref, r_ref, send_sem, recv_sem):
        x, y, c = _mesh_pos()
        cp = pltpu.make_async_remote_copy(src_ref=p_ref.at[:, _half(1 - c, rows)], dst_ref=r_ref, send_sem=send_sem,
                                          recv_sem=recv_sem, device_id=(x, y, 1 - c), device_id_type=MESH)
        cp.start()
        cp.wait()

    return pl.pallas_call(
        body, name="pair_exchange",
        in_specs=[pl.BlockSpec(memory_space=pl.ANY)],
        out_specs=pl.BlockSpec(memory_space=pl.ANY),
        out_shape=jax.ShapeDtypeStruct((NSH, rows, D), F32),
        scratch_shapes=[pltpu.SemaphoreType.DMA, pltpu.SemaphoreType.DMA],
        compiler_params=pltpu.CompilerParams(has_side_effects=True),
    )(p)


def _pair_add(p, r, nb):
    rows = r.shape[1]
    tr = rows // nb

    def body(p_ref, r_ref, q_ref):
        q_ref[...] = (p_ref[...] + r_ref[...]).astype(BF)

    return pl.pallas_call(
        body, name="pair_add", grid=(NSH, nb),
        in_specs=[pl.BlockSpec((None, tr, D), lambda j, i: (j, lax.axis_index("c") * nb + i, 0)),
                  pl.BlockSpec((None, tr, D), lambda j, i: (j, i, 0))],
        out_specs=pl.BlockSpec((None, tr, D), lambda j, i: (j, i, 0)),
        out_shape=jax.ShapeDtypeStruct((NSH, rows, D), BF),
        compiler_params=_params(2),
    )(p, r)


def _chip_exchange(q, small):
    def body(q_ref, s_ref, t_ref, sm_ref, send_sems, recv_sems, ssend_sems, srecv_sems):
        x, y, c = _mesh_pos()
        me = 4 * x + 2 * y + c
        _scatter_start(q_ref, t_ref, send_sems, recv_sems)
        flips = [(fx, fy, fc) for fx in (0, 1) for fy in (0, 1) for fc in (0, 1)][1:]
        smalls = []
        for k, (fx, fy, fc) in enumerate(flips):
            cp = pltpu.make_async_remote_copy(src_ref=s_ref, dst_ref=sm_ref.at[me],
                                              send_sem=ssend_sems.at[k], recv_sem=srecv_sems.at[k],
                                              device_id=(x ^ fx, y ^ fy, c ^ fc), device_id_type=MESH)
            cp.start()
            smalls.append(cp)
        _scatter_finish(q_ref, t_ref, send_sems, recv_sems)
        for cp in smalls:
            cp.wait()

    t, sm = pl.pallas_call(
        body, name="chip_exchange",
        in_specs=[pl.BlockSpec(memory_space=pl.ANY), pl.BlockSpec(memory_space=pl.ANY)],
        out_specs=[pl.BlockSpec(memory_space=pl.ANY), pl.BlockSpec(memory_space=pl.ANY)],
        out_shape=[jax.ShapeDtypeStruct((3,) + q.shape[1:], BF),
                   jax.ShapeDtypeStruct((NDEV, SMALL_ROWS, D), F32)],
        scratch_shapes=[pltpu.SemaphoreType.DMA((3,)), pltpu.SemaphoreType.DMA((3,)),
                        pltpu.SemaphoreType.DMA((7,)), pltpu.SemaphoreType.DMA((7,))],
        compiler_params=pltpu.CompilerParams(has_side_effects=True),
    )(q, small)
    x, y, c = _mesh_pos()
    return t, lax.dynamic_update_index_in_dim(sm, small, 4 * x + 2 * y + c, 0)


def _shard_sum(p, r, t, nb):
    rows = r.shape[1]
    tr = rows // nb

    def shard():
        return 2 * lax.axis_index("x") + lax.axis_index("y")

    def body(p_ref, r_ref, t_ref, o_ref):
        s = p_ref[...] + r_ref[...]
        for k in range(3):
            s = s + t_ref[k].astype(F32)
        o_ref[...] = s

    return pl.pallas_call(
        body, name="shard_sum", grid=(nb,),
        in_specs=[pl.BlockSpec((None, tr, D), lambda i: (shard(), lax.axis_index("c") * nb + i, 0)),
                  pl.BlockSpec((None, tr, D), lambda i: (shard(), i, 0)),
                  pl.BlockSpec((3, tr, D), lambda i: (0, i, 0))],
        out_specs=pl.BlockSpec((tr, D), lambda i: (i, 0)),
        out_shape=jax.ShapeDtypeStruct((rows, D), F32),
        compiler_params=_params(1),
    )(p, r, t)


def _small_sum(sm):
    def body(sm_ref, o_ref):
        s = sm_ref[0]
        for k in range(1, NDEV):
            s = s + sm_ref[k]
        o_ref[...] = s

    return pl.pallas_call(
        body, name="small_sum",
        in_specs=[pl.BlockSpec(memory_space=pltpu.VMEM)],
        out_specs=pl.BlockSpec(memory_space=pltpu.VMEM),
        out_shape=jax.ShapeDtypeStruct((SMALL_ROWS, D), F32),
    )(sm)


def _pair_gather(halves):
    n = len(halves)

    def body(*refs):
        gh_refs, o_refs, (send_sems, recv_sems) = refs[:n], refs[n:2 * n], refs[2 * n:]
        x, y, c = _mesh_pos()
        sibling = (x, y, 1 - c)
        for g in range(n):
            rows = gh_refs[g].shape[0]
            _remote(g, gh_refs[g], o_refs[g].at[_half(c, rows)], sibling, send_sems, recv_sems).start()
        for g in range(n):
            rows = gh_refs[g].shape[0]
            _remote(g, gh_refs[g], o_refs[g].at[_half(c, rows)], sibling, send_sems, recv_sems).wait_send()
            _remote(g, gh_refs[g], o_refs[g].at[_half(1 - c, rows)], sibling, send_sems, recv_sems).wait_recv()

    anywhere = pl.BlockSpec(memory_space=pl.ANY)
    outs = pl.pallas_call(
        body, name="pair_gather",
        in_specs=[anywhere] * n, out_specs=[anywhere] * n,
        out_shape=[jax.ShapeDtypeStruct((2 * h.shape[0], D), F32) for h in halves],
        scratch_shapes=[pltpu.SemaphoreType.DMA((n,)), pltpu.SemaphoreType.DMA((n,))],
        compiler_params=pltpu.CompilerParams(has_side_effects=True),
    )(*halves)
    c = lax.axis_index("c")
    return [lax.dynamic_update_slice_in_dim(o, h, c * h.shape[0], 0) for o, h in zip(outs, halves)]


def _pad_cols(a, n):
    return jnp.pad(a, ((0, 0), (0, n - a.shape[1])))


def _split_w_in(w_in_full):
    q, k, v = w_in_full[:, :D], w_in_full[:, D:2 * D], w_in_full[:, 2 * D:3 * D]
    f = w_in_full[:, 3 * D:3 * D + H]
    gates = w_in_full[:, 3 * D + H:]
    kv = jnp.stack([k.reshape(D, H, HD), v.reshape(D, H, HD)], axis=2).reshape(D, 2 * D)
    return jnp.concatenate([q, kv, gates], axis=1), _pad_cols(f, HD)


def _merge_w_in_grad(dwcat, dwf):
    kv = dwcat[:, D:3 * D].reshape(D, H, 2, HD)
    return jnp.concatenate([dwcat[:, :D], kv[:, :, 0].reshape(D, D), kv[:, :, 1].reshape(D, D),
                            dwf[:, :H], dwcat[:, 3 * D:]], axis=1)


def _reduce_start(g):
    r = _pair_exchange(g)
    return g, r, _pair_add(g, r, 3)


def _train_step(x, target, wp1, wp2, small, tm, t_attn):
    S = x.shape[0]
    g1pre, g1post = small["ffn1_pre_g"], small["ffn1_post_g"]
    gmpre, gmpost = small["mix_pre_g"], small["mix_post_g"]
    g2pre, g2post = small["ffn2_pre_g"], small["ffn2_post_g"]
    ln_g, ln_b = small["sgu_ln_g"], small["sgu_ln_b"]
    ws = small["sgu_w_s"][0]
    wst = jnp.swapaxes(ws, 1, 2)
    bst = _pad_cols(small["sgu_b_s"][0].T, HD)
    bf = _pad_cols(small["b_forget"], HD)

    w1 = _all_gather_weights(wp1)
    h1, a1, b1, y1, x1, w2 = _ffn_fwd(x, g1pre, w1, g1post, tm, gather=wp2)
    wout = w2[:, FFN_ROWS:FFN_ROWS + 256, :].reshape(D, D)
    r0 = FFN_ROWS + 256
    w_in_full = jnp.concatenate(
        [blk for j in range(NSH) for blk in (w2[j, r0:r0 + D], w2[j, r0 + D:r0 + 2 * D, :WIN_SH - D])], axis=1)
    wcat, wf = _split_w_in(w_in_full)
    h2 = _norm_fwd(x1, gmpre, tm)
    z = _mm(h2, wcat, min(1024, S), D, BF, first_block_scale=SCALE * LOG2E)
    zf = _mm(h2, wf, tm, HD, F32)
    cs = min(512, S)
    c, ccol_b = _forget_cumsum(zf, bf, cs)
    crow = jnp.transpose(c[:, :H]).reshape(H, S // t_attn, t_attn)
    oa, lse_b = _fox_fwd(z, ccol_b, crow, t_attn)
    merged, y2, x2 = _mix_out_fwd(z, oa, ln_g, ln_b, ws, bst, wout, x1, gmpost, tm)
    h3, a3, b3, y3, x3 = _ffn_fwd(x2, g2pre, w2, g2post, tm)
    dx3, loss_acc = _loss_head(x3, target, tm)
    loss = loss_acc[0, 0]

    dy3, da3, db3, act3, dx2, dg2post, dg2pre = _ffn_bwd(dx3, y3, g2post, a3, b3, w2, x2, g2pre, tm)
    bt = min(2048, S)
    dwg2 = _mm_tn(h3, da3, D, D, bt, col_blocked=True)
    dwu2 = _mm_tn(h3, db3, D, D, bt, col_blocked=True)
    dwd2 = _mm_tn(act3, dy3, D, D, bt).reshape(NSH, D, D)
    g_ffn2, r_ffn2, q_ffn2 = _reduce_start(jnp.concatenate([dwg2, dwu2, dwd2], axis=1))

    dz, dy2, doa, dgmpost, dlng, dlnb, dws, dbst, t_ffn2 = _mix_out_bwd(
        dx2, y2, gmpost, wout, z, oa, ln_g, ln_b, ws, wst, bst, tm, q_ffn2)
    dwout = _mm_tn(merged, dy2, D, D, bt).reshape(NSH, 256, D)
    dz, dkt, dvt, dc_keys, dc_queries = _fox_bwd(z, doa, oa, lse_b, ccol_b, crow, dz, t_attn)
    dz = _fox_bwd_finish(dkt, dvt, dz, t_attn)
    dc = _pad_cols(jnp.transpose(dc_queries.reshape(H, S) - dc_keys.reshape(H, S)), HD)
    dzf, dbf = _forget_bwd(dc, zf, bf, cs)
    dwcat = _mm_tn(h2, dz, D, D, bt)
    dwf = _mm_tn(h2, dzf, D, HD, bt)
    dx1, dgmpre = _mix_in_bwd(dz, wcat, dzf, wf, x1, gmpre, dx2, tm)
    dwin = _merge_w_in_grad(dwcat, dwf)
    dwin_a = jnp.stack([dwin[:, j * WIN_SH:j * WIN_SH + D] for j in range(NSH)])
    dwin_b = jnp.stack([_pad_cols(dwin[:, j * WIN_SH + D:(j + 1) * WIN_SH], D) for j in range(NSH)])
    g_mix, r_mix, q_mix = _reduce_start(jnp.concatenate([dwout, dwin_a, dwin_b], axis=1))

    dy1, da1, db1, act1, dx, dg1post, dg1pre, t_mix = _ffn_bwd(dx1, y1, g1post, a1, b1, w1, x, g1pre, tm,
                                                               scatter=q_mix)
    dwg1 = _mm_tn(h1, da1, D, D, bt, col_blocked=True)
    dwu1 = _mm_tn(h1, db1, D, D, bt, col_blocked=True)
    dwd1 = _mm_tn(act1, dy1, D, D, bt).reshape(NSH, D, D)
    g_ffn1, r_ffn1, q_ffn1 = _reduce_start(jnp.concatenate([dwg1, dwu1, dwd1], axis=1))

    gsmall = {
        "ffn1_pre_g": dg1pre, "ffn1_post_g": dg1post, "mix_pre_g": dgmpre, "mix_post_g": dgmpost,
        "ffn2_pre_g": dg2pre, "ffn2_post_g": dg2post, "sgu_ln_g": dlng, "sgu_ln_b": dlnb,
        "sgu_w_s": dws[None], "sgu_b_s": jnp.transpose(dbst[:, :G])[None], "b_forget": dbf[:, :H],
    }
    t_ffn1, sm = _chip_exchange(q_ffn1, _pack_small(gsmall))
    halves = [_shard_sum(g, r, t, 3) for g, r, t in ((g_ffn1, r_ffn1, t_ffn1), (g_ffn2, r_ffn2, t_ffn2),
                                                     (g_mix, r_mix, t_mix))]
    f_ffn1, f_ffn2, f_mix = _pair_gather(halves)
    return loss, dx, f_ffn1, f_ffn2, f_mix, _small_sum(sm)


_SMALL_NAMES = ["ffn1_pre_g", "ffn1_post_g", "mix_pre_g", "mix_post_g", "ffn2_pre_g", "ffn2_post_g",
                "sgu_ln_g", "sgu_ln_b", "sgu_b_s", "b_forget", "sgu_w_s"]
_SMALL_SHAPES = {"sgu_b_s": (1, G, L), "b_forget": (1, H), "sgu_w_s": (1, G, L, L)}


def _pack_small(d):
    rows = []
    for n in _SMALL_NAMES:
        a = d[n].astype(F32)
        if n == "b_forget":
            a = _pad_cols(a, D)
        a = a.reshape(-1, D)
        rows.append(jnp.pad(a, ((0, -a.shape[0] % SMALL_STRIDE), (0, 0))))
    return jnp.concatenate(rows, axis=0)


def _unpack_small(p):
    out, r = {}, 0
    for n in _SMALL_NAMES:
        if n == "sgu_w_s":
            out[n] = p[r:r + L].reshape(1, G, L, L)
            r += L
        elif n == "b_forget":
            out[n] = p[r:r + 1, :H]
            r += SMALL_STRIDE
        elif n == "sgu_b_s":
            out[n] = p[r:r + 1].reshape(1, G, L)
            r += SMALL_STRIDE
        else:
            out[n] = p[r:r + 1]
            r += SMALL_STRIDE
    return out


_BIG_NAMES = ["ffn1_w_gate", "ffn1_w_up", "ffn1_w_down", "ffn2_w_gate", "ffn2_w_up", "ffn2_w_down", "w_out", "w_in"]
_WEIGHT_ORDER = ['ffn1_pre_g', 'ffn1_w_gate', 'ffn1_w_up', 'ffn1_w_down', 'ffn1_post_g', 'mix_pre_g', 'w_in', 'b_forget',
                 'sgu_ln_g', 'sgu_ln_b', 'sgu_w_s', 'sgu_b_s', 'w_out', 'mix_post_g', 'ffn2_pre_g', 'ffn2_w_gate',
                 'ffn2_w_up', 'ffn2_w_down', 'ffn2_post_g']


def _pack_ffn(w, name):
    return jnp.concatenate([w[name + "_w_gate"][0], w[name + "_w_up"][0], w[name + "_w_down"][0]], axis=0)


def _pack_mix(w):
    w_in = w["w_in"][0]
    return jnp.concatenate([w["w_out"][0], w_in[:, :D], _pad_cols(w_in[:, D:], D)], axis=0)


def _unpack_ffn(p, name):
    return {name + "_w_gate": p[:D][None], name + "_w_up": p[D:2 * D][None], name + "_w_down": p[2 * D:][None]}


def _unpack_mix(p):
    return {"w_out": p[:256][None],
            "w_in": jnp.concatenate([p[256:256 + D], p[256 + D:, :WIN_SH - D]], axis=1)[None]}


def _step(args, tm, t_attn):
    x = args["x"][0]
    target = args["loss_target"][0]
    weights = {n: args[n] for n in _WEIGHT_ORDER}
    small = {n: weights[n] for n in _SMALL_NAMES}

    wb = {n: weights[n].astype(BF) for n in _BIG_NAMES}
    wp1 = _pack_ffn(wb, "ffn1")
    wp2 = jnp.concatenate([_pack_ffn(wb, "ffn2"), _pack_mix(wb)], axis=0)
    loss_local, dx, f_ffn1, f_ffn2, f_mix, gsm = _train_step(x, target, wp1, wp2, small, tm, t_attn)
    loss = lax.psum(loss_local, ("x", "y", "c"))
    grads = {**_unpack_ffn(f_ffn1, "ffn1"), **_unpack_ffn(f_ffn2, "ffn2"), **_unpack_mix(f_mix),
             **_unpack_small(gsm)}

    delta, new_m, new_v = {}, {}, {}
    for n in _BIG_NAMES:
        shp = weights[n].shape
        w2 = weights[n].reshape(-1, shp[-1])
        rows = w2.shape[0]
        d, nm, nv = _adamw(w2, grads[n].reshape(w2.shape), args["m_" + n].reshape(w2.shape),
                           args["v_" + n].reshape(w2.shape), rows // 4)
        delta[n], new_m[n], new_v[n] = d.reshape(shp), nm.reshape(shp), nv.reshape(shp)
    ds, nms, nvs = _adamw(_pack_small(small), gsm, _pack_small({n: args["m_" + n] for n in _SMALL_NAMES}),
                          _pack_small({n: args["v_" + n] for n in _SMALL_NAMES}), SMALL_ROWS)
    delta.update(_unpack_small(ds))
    new_m.update(_unpack_small(nms))
    new_v.update(_unpack_small(nvs))

    return (loss, dx[None], *[grads[n] for n in _WEIGHT_ORDER], *[delta[n] for n in _WEIGHT_ORDER],
            *[new_m[n] for n in _WEIGHT_ORDER], *[new_v[n] for n in _WEIGHT_ORDER])


_ARG_NAMES = (["x"] + _WEIGHT_ORDER + ["loss_target"] + ["m_" + n for n in _WEIGHT_ORDER]
              + ["v_" + n for n in _WEIGHT_ORDER])


def kernel(x, ffn1_pre_g, ffn1_w_gate, ffn1_w_up, ffn1_w_down, ffn1_post_g, mix_pre_g, w_in, b_forget, sgu_ln_g, sgu_ln_b, sgu_w_s, sgu_b_s, w_out, mix_post_g, ffn2_pre_g, ffn2_w_gate, ffn2_w_up, ffn2_w_down, ffn2_post_g, loss_target, m_ffn1_pre_g, m_ffn1_w_gate, m_ffn1_w_up, m_ffn1_w_down, m_ffn1_post_g, m_mix_pre_g, m_w_in, m_b_forget, m_sgu_ln_g, m_sgu_ln_b, m_sgu_w_s, m_sgu_b_s, m_w_out, m_mix_post_g, m_ffn2_pre_g, m_ffn2_w_gate, m_ffn2_w_up, m_ffn2_w_down, m_ffn2_post_g, v_ffn1_pre_g, v_ffn1_w_gate, v_ffn1_w_up, v_ffn1_w_down, v_ffn1_post_g, v_mix_pre_g, v_w_in, v_b_forget, v_sgu_ln_g, v_sgu_ln_b, v_sgu_w_s, v_sgu_b_s, v_w_out, v_mix_post_g, v_ffn2_pre_g, v_ffn2_w_gate, v_ffn2_w_up, v_ffn2_w_down, v_ffn2_post_g):
    args = (x, ffn1_pre_g, ffn1_w_gate, ffn1_w_up, ffn1_w_down, ffn1_post_g, mix_pre_g, w_in, b_forget, sgu_ln_g, sgu_ln_b, sgu_w_s, sgu_b_s, w_out, mix_post_g, ffn2_pre_g, ffn2_w_gate, ffn2_w_up, ffn2_w_down, ffn2_post_g, loss_target, m_ffn1_pre_g, m_ffn1_w_gate, m_ffn1_w_up, m_ffn1_w_down, m_ffn1_post_g, m_mix_pre_g, m_w_in, m_b_forget, m_sgu_ln_g, m_sgu_ln_b, m_sgu_w_s, m_sgu_b_s, m_w_out, m_mix_post_g, m_ffn2_pre_g, m_ffn2_w_gate, m_ffn2_w_up, m_ffn2_w_down, m_ffn2_post_g, v_ffn1_pre_g, v_ffn1_w_gate, v_ffn1_w_up, v_ffn1_w_down, v_ffn1_post_g, v_mix_pre_g, v_w_in, v_b_forget, v_sgu_ln_g, v_sgu_ln_b, v_sgu_w_s, v_sgu_b_s, v_w_out, v_mix_post_g, v_ffn2_pre_g, v_ffn2_w_gate, v_ffn2_w_up, v_ffn2_w_down, v_ffn2_post_g)
    named = dict(zip(_ARG_NAMES, args))
    tile = min(512, x.shape[1])
    return _step(named, tile, tile)
```

```python
import functools
import math

import jax
import jax.numpy as jnp
from jax import lax
from jax.experimental import pallas as pl
from jax.experimental.pallas import tpu as pltpu

D = 1024
F = 4096
H = 8
HD = 128
G = 8
L = 128
CHUNK = 64
NSH = 4
NDEV = 8
ZW = 7 * D
RMS_EPS = 1e-6
LN_EPS = 1e-5
NEG = -1e30
SCALE = 1.0 / math.sqrt(HD)
LOG2E = math.log2(math.e)
LN2 = math.log(2.0)

ADAM_LR = 0.001
ADAM_B1 = 0.9
ADAM_B2 = 0.999
ADAM_EPS = 1e-08
ADAM_WD = 0.01
ADAM_STEP = 10

VMEM_LIMIT_BYTES = 56 * 1024 * 1024

WIN_SH = 1794
FFN_ROWS = 3 * D
MIX_ROWS = 256 + 2 * D
G2_ROWS = FFN_ROWS + MIX_ROWS
SMALL_STRIDE = 8
SMALL_ROWS = 10 * SMALL_STRIDE + L

BF = jnp.bfloat16
F32 = jnp.float32
MESH = pl.DeviceIdType.MESH


def _params(n_grid):
    return pltpu.CompilerParams(dimension_semantics=("arbitrary",) * n_grid,
                                vmem_limit_bytes=VMEM_LIMIT_BYTES)


def _dot(a, b):
    return jnp.dot(a, b, preferred_element_type=F32)


def _dot_nt(a, b):
    return lax.dot_general(a, b, (((1,), (1,)), ((), ())), preferred_element_type=F32)


def _dot_tn(a, b):
    return lax.dot_general(a, b, (((0,), (0,)), ((), ())), preferred_element_type=F32)


def _rms(x, g):
    r = lax.rsqrt(jnp.mean(x * x, axis=-1, keepdims=True) + RMS_EPS)
    return x * r * g


def _rms_bwd(dn, x, g):
    r = lax.rsqrt(jnp.mean(x * x, axis=-1, keepdims=True) + RMS_EPS)
    xr = x * r
    dg = jnp.sum(dn * xr, axis=0, keepdims=True)
    t = dn * g
    dx = r * (t - xr * jnp.mean(t * xr, axis=-1, keepdims=True))
    return dx, dg


def _gelu_parts(x):
    cdf = 0.5 * (1.0 + lax.erf(x * (1.0 / math.sqrt(2.0))))
    pdf = jnp.exp(-0.5 * x * x) * (1.0 / math.sqrt(2.0 * math.pi))
    return x * cdf, cdf + x * pdf


def _gelu(x):
    return x * (0.5 * (1.0 + lax.erf(x * (1.0 / math.sqrt(2.0)))))


def _sigmoid(x):
    return 1.0 / (1.0 + jnp.exp(-x))


def _ffn_fwd(x, g_pre, wpack, g_post, tm, gather=None):
    S = x.shape[0]
    nt, nf, tf = S // tm, NSH, D

    def body(x_ref, gpre_ref, wg_ref, wu_ref, wd_ref, gpost_ref, *rest):
        if gather is None:
            h_ref, a_ref, b_ref, y_ref, xo_ref, h_s, acc = rest
        else:
            wp_ref, h_ref, a_ref, b_ref, y_ref, xo_ref, g_ref, h_s, acc, send_sems, recv_sems = rest
        i = pl.program_id(0)
        j = pl.program_id(1)

        if gather is not None:
            @pl.when((i == 0) & (j == 0))
            def _():
                _gather_start(wp_ref, g_ref, send_sems, recv_sems)

        @pl.when(j == 0)
        def _():
            h = _rms(x_ref[...], gpre_ref[...]).astype(BF)
            h_s[...] = h
            h_ref[...] = h
            acc[...] = jnp.zeros_like(acc)

        h = h_s[...]
        a = _dot(h, wg_ref[...])
        b = _dot(h, wu_ref[...])
        a_ref[...] = a.astype(BF)
        b_ref[...] = b.astype(BF)
        act = (a * _sigmoid(a)) * b
        acc[...] += _dot(act.astype(BF), wd_ref[...])

        @pl.when(j == nf - 1)
        def _():
            y = acc[...]
            y_ref[...] = y
            xo_ref[...] = x_ref[...] + 0.5 * _rms(y, gpost_ref[...])

        if gather is not None:
            @pl.when((i == nt - 1) & (j == nf - 1))
            def _():
                _gather_finish(wp_ref, g_ref, send_sems, recv_sems)

    row = pl.BlockSpec((tm, D), lambda i, j: (i, 0))
    vec = pl.BlockSpec((1, D), lambda i, j: (0, 0))
    anywhere = pl.BlockSpec(memory_space=pl.ANY)
    in_specs = [row, vec,
                pl.BlockSpec((None, D, tf), lambda i, j: (j, 0, 0)),
                pl.BlockSpec((None, D, tf), lambda i, j: (j, 1, 0)),
                pl.BlockSpec((None, tf, D), lambda i, j: (j, 2, 0)),
                vec]
    out_specs = [row,
                 pl.BlockSpec((tm, tf), lambda i, j: (i, j)),
                 pl.BlockSpec((tm, tf), lambda i, j: (i, j)),
                 row, row]
    out_shape = [jax.ShapeDtypeStruct((S, D), BF),
                 jax.ShapeDtypeStruct((S, F), BF),
                 jax.ShapeDtypeStruct((S, F), BF),
                 jax.ShapeDtypeStruct((S, D), F32),
                 jax.ShapeDtypeStruct((S, D), F32)]
    scratch = [pltpu.VMEM((tm, D), BF), pltpu.VMEM((tm, D), F32)]
    args = [x, g_pre, wpack, wpack, wpack, g_post]
    if gather is not None:
        in_specs.append(anywhere)
        out_specs.append(anywhere)
        out_shape.append(jax.ShapeDtypeStruct((NSH,) + gather.shape, gather.dtype))
        scratch += [pltpu.SemaphoreType.DMA((6,)), pltpu.SemaphoreType.DMA((6,))]
        args.append(gather)
    res = list(pl.pallas_call(
        body, name="ffn_fwd" if gather is None else "ffn_fwd_gather",
        grid=(nt, nf),
        in_specs=in_specs, out_specs=out_specs, out_shape=out_shape, scratch_shapes=scratch,
        compiler_params=_params(2),
    )(*args))
    if gather is not None:
        res[5] = _place_own_shard(res[5], gather)
    return res


def _ffn_bwd(dxo, y, g_post, a, b, wpack, x_in, g_pre, tm, scatter=None):
    S = dxo.shape[0]
    nt, nf, tf = S // tm, NSH, D

    def body(dxo_ref, y_ref, gpost_ref, a_ref, b_ref, wg_ref, wu_ref, wd_ref, xin_ref, gpre_ref, *rest):
        if scatter is None:
            dy_ref, da_ref, db_ref, act_ref, dxin_ref, dgpost_ref, dgpre_ref, dy_s, acc = rest
        else:
            (q_ref, dy_ref, da_ref, db_ref, act_ref, dxin_ref, dgpost_ref, dgpre_ref, t_ref,
             dy_s, acc, send_sems, recv_sems) = rest
        i = pl.program_id(0)
        j = pl.program_id(1)

        @pl.when((i == 0) & (j == 0))
        def _():
            dgpost_ref[...] = jnp.zeros_like(dgpost_ref)
            dgpre_ref[...] = jnp.zeros_like(dgpre_ref)
            if scatter is not None:
                _scatter_start(q_ref, t_ref, send_sems, recv_sems)

        @pl.when(j == 0)
        def _():
            dy, dg = _rms_bwd(0.5 * dxo_ref[...], y_ref[...], gpost_ref[...])
            dyb = dy.astype(BF)
            dy_s[...] = dyb
            dy_ref[...] = dyb
            dgpost_ref[...] += dg
            acc[...] = jnp.zeros_like(acc)

        dact = _dot_nt(dy_s[...], wd_ref[...])
        av = a_ref[...].astype(F32)
        bv = b_ref[...].astype(F32)
        sig = _sigmoid(av)
        sl = av * sig
        act_ref[...] = (sl * bv).astype(BF)
        dbb = (dact * sl).astype(BF)
        dab = (dact * bv * (sig * (1.0 + av * (1.0 - sig)))).astype(BF)
        da_ref[...] = dab
        db_ref[...] = dbb
        acc[...] += _dot_nt(dab, wg_ref[...]) + _dot_nt(dbb, wu_ref[...])

        @pl.when(j == nf - 1)
        def _():
            dx, dg = _rms_bwd(acc[...], xin_ref[...], gpre_ref[...])
            dxin_ref[...] = dxo_ref[...] + dx
            dgpre_ref[...] += dg

        if scatter is not None:
            @pl.when((i == nt - 1) & (j == nf - 1))
            def _():
                _scatter_finish(q_ref, t_ref, send_sems, recv_sems)

    row = pl.BlockSpec((tm, D), lambda i, j: (i, 0))
    vec = pl.BlockSpec((1, D), lambda i, j: (0, 0))
    ff = pl.BlockSpec((tm, tf), lambda i, j: (i, j))
    anywhere = pl.BlockSpec(memory_space=pl.ANY)
    in_specs = [row, row, vec, ff, ff,
                pl.BlockSpec((None, D, tf), lambda i, j: (j, 0, 0)),
                pl.BlockSpec((None, D, tf), lambda i, j: (j, 1, 0)),
                pl.BlockSpec((None, tf, D), lambda i, j: (j, 2, 0)),
                row, vec]
    out_specs = [row, ff, ff, ff, row, vec, vec]
    out_shape = [jax.ShapeDtypeStruct((S, D), BF),
                 jax.ShapeDtypeStruct((S, F), BF),
                 jax.ShapeDtypeStruct((S, F), BF),
                 jax.ShapeDtypeStruct((S, F), BF),
                 jax.ShapeDtypeStruct((S, D), F32),
                 jax.ShapeDtypeStruct((1, D), F32),
                 jax.ShapeDtypeStruct((1, D), F32)]
    scratch = [pltpu.VMEM((tm, D), BF), pltpu.VMEM((tm, D), F32)]
    args = [dxo, y, g_post, a, b, wpack, wpack, wpack, x_in, g_pre]
    if scatter is not None:
        in_specs.append(anywhere)
        out_specs.append(anywhere)
        out_shape.append(jax.ShapeDtypeStruct((3,) + scatter.shape[1:], scatter.dtype))
        scratch += [pltpu.SemaphoreType.DMA((3,)), pltpu.SemaphoreType.DMA((3,))]
        args.append(scatter)
    return pl.pallas_call(
        body, name="ffn_bwd" if scatter is None else "ffn_bwd_scatter",
        grid=(nt, nf),
        in_specs=in_specs, out_specs=out_specs, out_shape=out_shape, scratch_shapes=scratch,
        compiler_params=_params(2),
    )(*args)


def _mm_tn(a, b, bm, bn, bt, into=None):
    S, M = a.shape
    N = b.shape[1]
    nt = S // bt

    def body(a_ref, b_ref, *rest):
        o_ref = rest[-1]
        t = pl.program_id(2)

        @pl.when(t == 0)
        def _():
            o_ref[...] = jnp.zeros_like(o_ref)

        o_ref[...] += _dot_tn(a_ref[...], b_ref[...])

    in_specs = [pl.BlockSpec((bt, bm), lambda m, n, t: (t, m)),
                pl.BlockSpec((bt, bn), lambda m, n, t: (t, n))]
    args, aliases = [a, b], {}
    if into is None:
        out_spec = pl.BlockSpec((bm, bn), lambda m, n, t: (m, n))
        out_shape = jax.ShapeDtypeStruct((M, N), F32)
    else:
        buf, rows, rb, by = into
        assert bn == D and (M == bm if by == "cols" else (M == NSH * bm and N == D))
        if by == "cols":
            out_spec = pl.BlockSpec((None, bm, bn), lambda m, n, t: (n, rb, 0))
        else:
            out_spec = pl.BlockSpec((None, bm, bn), lambda m, n, t: (m, rb, 0))
        out_shape = jax.ShapeDtypeStruct((NSH, rows, D), F32)
        if buf is not None:
            in_specs.append(pl.BlockSpec(memory_space=pl.ANY))
            args.append(buf)
            aliases = {2: 0}
    return pl.pallas_call(
        body, name="mm_tn",
        grid=(M // bm, N // bn, nt),
        in_specs=in_specs, out_specs=out_spec, out_shape=out_shape,
        input_output_aliases=aliases,
        compiler_params=_params(3),
    )(*args)


def _mm(a, w, tm, tn, out_dtype, first_block_scale=1.0):
    S, K = a.shape
    N = w.shape[1]

    def body(a_ref, w_ref, o_ref):
        r = _dot(a_ref[...], w_ref[...])
        if first_block_scale != 1.0:
            r = r * jnp.where(pl.program_id(1) == 0, first_block_scale, 1.0)
        o_ref[...] = r.astype(out_dtype)

    return pl.pallas_call(
        body, name="mm",
        grid=(S // tm, N // tn),
        in_specs=[pl.BlockSpec((tm, K), lambda i, j: (i, 0)),
                  pl.BlockSpec((K, tn), lambda i, j: (0, j))],
        out_specs=pl.BlockSpec((tm, tn), lambda i, j: (i, j)),
        out_shape=jax.ShapeDtypeStruct((S, N), out_dtype),
        compiler_params=_params(2),
    )(a, w)


def _norm_fwd(x, g, tm):
    S = x.shape[0]

    def body(x_ref, g_ref, h_ref):
        h_ref[...] = _rms(x_ref[...], g_ref[...]).astype(BF)

    return pl.pallas_call(
        body, name="norm_fwd",
        grid=(S // tm,),
        in_specs=[pl.BlockSpec((tm, D), lambda i: (i, 0)), pl.BlockSpec((1, D), lambda i: (0, 0))],
        out_specs=pl.BlockSpec((tm, D), lambda i: (i, 0)),
        out_shape=jax.ShapeDtypeStruct((S, D), BF),
        compiler_params=_params(1),
    )(x, g)


def _mix_in_bwd(dz, wcat, dzf, wf, x1, g, dx2, tm, rider):
    S = dz.shape[0]
    nk = 2
    kb = ZW // nk

    def body(dz_ref, w_ref, dzf_ref, wf_ref, x_ref, g_ref, dx2_ref, rsrc_ref, dx1_ref, dg_ref, rdst_ref,
             acc, send_sems, recv_sems):
        i = pl.program_id(0)
        k = pl.program_id(1)

        @pl.when((i == 0) & (k == 0))
        def _():
            _rider_start(rider[0], rsrc_ref, rdst_ref, send_sems, recv_sems)
            dg_ref[...] = jnp.zeros_like(dg_ref)

        @pl.when(k == 0)
        def _():
            acc[...] = _dot_nt(dzf_ref[...], wf_ref[...])

        acc[...] += _dot_nt(dz_ref[...], w_ref[...])

        @pl.when(k == nk - 1)
        def _():
            dx, dg = _rms_bwd(acc[...], x_ref[...], g_ref[...])
            dx1_ref[...] = dx2_ref[...] + dx
            dg_ref[...] += dg

        @pl.when((i == S // tm - 1) & (k == nk - 1))
        def _():
            _rider_finish(rider[0], rsrc_ref, rdst_ref, send_sems, recv_sems)

    row = pl.BlockSpec((tm, D), lambda i, k: (i, 0))
    vec = pl.BlockSpec((1, D), lambda i, k: (0, 0))
    anywhere = pl.BlockSpec(memory_space=pl.ANY)
    return pl.pallas_call(
        body, name="mix_in_bwd",
        grid=(S // tm, nk),
        in_specs=[pl.BlockSpec((tm, kb), lambda i, k: (i, k)),
                  pl.BlockSpec((D, kb), lambda i, k: (0, k)),
                  pl.BlockSpec((tm, HD), lambda i, k: (i, 0)),
                  pl.BlockSpec((D, HD), lambda i, k: (0, 0)),
                  row, vec, row, anywhere],
        out_specs=[row, vec, anywhere],
        out_shape=[jax.ShapeDtypeStruct((S, D), F32), jax.ShapeDtypeStruct((1, D), F32), _rider_out(rider)],
        scratch_shapes=[pltpu.VMEM((tm, D), F32)] + _rider_sems(rider),
        compiler_params=_params(2),
    )(dz, wcat, dzf, wf, x1, g, dx2, rider[1])


def _scan_rows(blk, reverse):
    n = blk.shape[0]
    row = lax.broadcasted_iota(jnp.int32, blk.shape, 0)
    d = 1
    while d < n:
        if reverse:
            blk = blk + jnp.where(row < n - d, pltpu.roll(blk, n - d, 0), 0.0)
        else:
            blk = blk + jnp.where(row >= d, pltpu.roll(blk, d, 0), 0.0)
        d *= 2
    return blk


def _forget_cumsum(zf, bf, cs):
    S = zf.shape[0]

    def body(zf_ref, bf_ref, c_ref, cb_ref, carry):
        @pl.when(pl.program_id(0) == 0)
        def _():
            carry[...] = jnp.zeros_like(carry)

        x = zf_ref[...] + bf_ref[...]
        logf = jnp.minimum(x, 0.0) - jnp.log1p(jnp.exp(-jnp.abs(x)))
        sc = _scan_rows(logf, False) + carry[...]
        carry[...] = sc[cs - 1:cs, :]
        sc = sc * LOG2E
        c_ref[...] = sc
        for h in range(H):
            cb_ref[h] = jnp.broadcast_to(sc[:, h:h + 1], (cs, HD))

    return pl.pallas_call(
        body, name="forget_cumsum",
        grid=(S // cs,),
        in_specs=[pl.BlockSpec((cs, HD), lambda i: (i, 0)), pl.BlockSpec((1, HD), lambda i: (0, 0))],
        out_specs=[pl.BlockSpec((cs, HD), lambda i: (i, 0)),
                   pl.BlockSpec((H, cs, HD), lambda i: (0, i, 0))],
        out_shape=[jax.ShapeDtypeStruct((S, HD), F32), jax.ShapeDtypeStruct((H, S, HD), F32)],
        scratch_shapes=[pltpu.VMEM((1, HD), F32)],
        compiler_params=_params(1),
    )(zf, bf)


def _forget_bwd(dc, zf, bf, cs):
    S = dc.shape[0]
    nc = S // cs

    def body(dc_ref, zf_ref, bf_ref, dzf_ref, dbf_ref, carry):
        @pl.when(pl.program_id(0) == 0)
        def _():
            carry[...] = jnp.zeros_like(carry)
            dbf_ref[...] = jnp.zeros_like(dbf_ref)

        sc = _scan_rows(dc_ref[...], True) + carry[...]
        carry[...] = sc[0:1, :]
        x = zf_ref[...] + bf_ref[...]
        dz = sc * _sigmoid(-x)
        dzf_ref[...] = dz.astype(BF)
        dbf_ref[...] += jnp.sum(dz, axis=0, keepdims=True)

    rev = pl.BlockSpec((cs, HD), lambda i: (nc - 1 - i, 0))
    vec = pl.BlockSpec((1, HD), lambda i: (0, 0))
    return pl.pallas_call(
        body, name="forget_bwd",
        grid=(nc,),
        in_specs=[rev, rev, vec],
        out_specs=[rev, vec],
        out_shape=[jax.ShapeDtypeStruct((S, HD), BF), jax.ShapeDtypeStruct((1, HD), F32)],
        scratch_shapes=[pltpu.VMEM((1, HD), F32)],
        compiler_params=_params(1),
    )(dc, zf, bf)


def _lanes(x, n):
    return x if n == HD else jnp.concatenate([x] * (n // HD), axis=1)


def _causal_mask(i, j, t, rows_are_queries):
    r = lax.broadcasted_iota(jnp.int32, (t, t), 0)
    c = lax.broadcasted_iota(jnp.int32, (t, t), 1)
    if rows_are_queries:
        return (j * t + c) <= (i * t + r)
    return (j * t + r) <= (i * t + c)


def _fox_fwd(z, ccol_b, crow, t):
    S = z.shape[0]
    nq = S // t

    def body(q_ref, kv_ref, cc_ref, cr_ref, o_ref, lse_ref, m_s, acc_s, s_a, s_b):
        i = pl.program_id(1)
        ct = cc_ref[...]
        ones = jnp.ones((t, HD), BF)
        m_s[...] = jnp.full_like(m_s, NEG)
        acc_s[...] = jnp.zeros_like(acc_s)

        def scores(j, s_ref):
            off = pl.multiple_of(j * t, t)
            s_ref[...] = _dot_nt(q_ref[...], kv_ref[pl.ds(off, t), :HD]) - cr_ref[pl.ds(j, 1), :]

        def consume(j, s_ref, masked):
            off = pl.multiple_of(j * t, t)
            v1 = jnp.concatenate([kv_ref[pl.ds(off, t), HD:], ones], axis=1)
            s = s_ref[...]
            if masked:
                s = jnp.where(_causal_mask(i, j, t, True), s, NEG)
            m_old = m_s[...]
            m_new = jnp.maximum(m_old, jnp.max(s, axis=1, keepdims=True))
            p = jnp.exp2(s - _lanes(m_new, t))
            alpha = jnp.exp2(m_old - m_new)
            acc_s[...] = _lanes(alpha, 2 * HD) * acc_s[...] + _dot(p.astype(BF), v1)
            m_s[...] = m_new

        scores(0, s_a)

        def pair(jj, carry):
            j = 2 * jj
            scores(j + 1, s_b)
            consume(j, s_a, False)
            scores(j + 2, s_a)
            consume(j + 1, s_b, False)
            return carry

        lax.fori_loop(0, i // 2, pair, 0)

        @pl.when(i % 2 == 0)
        def _():
            consume(i, s_a, True)

        @pl.when(i % 2 == 1)
        def _():
            scores(i, s_b)
            consume(i - 1, s_a, False)
            consume(i, s_b, True)

        l = acc_s[:, HD:]
        o_ref[...] = (acc_s[:, :HD] / l).astype(BF)
        lse_ref[...] = m_s[...] + ct + jnp.log2(l)

    return pl.pallas_call(
        body, name="fox_fwd",
        grid=(H, nq),
        in_specs=[pl.BlockSpec((t, HD), lambda h, i: (i, h)),
                  pl.BlockSpec((S, 2 * HD), lambda h, i: (0, 4 + h)),
                  pl.BlockSpec((None, t, HD), lambda h, i: (h, i, 0)),
                  pl.BlockSpec((None, nq, t), lambda h, i: (h, 0, 0))],
        out_specs=[pl.BlockSpec((t, HD), lambda h, i: (i, h)),
                   pl.BlockSpec((None, t, HD), lambda h, i: (h, i, 0))],
        out_shape=[jax.ShapeDtypeStruct((S, D), BF), jax.ShapeDtypeStruct((H, S, HD), F32)],
        scratch_shapes=[pltpu.VMEM((t, HD), F32), pltpu.VMEM((t, 2 * HD), F32),
                        pltpu.VMEM((t, t), F32), pltpu.VMEM((t, t), F32)],
        compiler_params=_params(2),
    )(z, z, ccol_b, crow)


def _fox_bwd(z, do, o, lse_b, ccol_b, crow, dz, t, rider):
    S = z.shape[0]
    nq = S // t

    def body(q_ref, kv_ref, do_ref, o_ref, lse_ref, cc_ref, cr_ref, dz_in, rsrc_ref,
             dq_ref, dkt_ref, dvt_ref, dck_ref, dcq_ref, rdst_ref, acc_s, r_s, send_sems, recv_sems):
        del dz_in
        i = pl.program_id(1)

        @pl.when((pl.program_id(0) == 0) & (i == 0))
        def _():
            _rider_start(rider[0], rsrc_ref, rdst_ref, send_sems, recv_sems)

        @pl.when(i == 0)
        def _():
            dkt_ref[...] = jnp.zeros_like(dkt_ref)
            dvt_ref[...] = jnp.zeros_like(dvt_ref)
            dck_ref[...] = jnp.zeros_like(dck_ref)

        q = q_ref[...]
        dout = do_ref[...]
        qt = jnp.transpose(q.astype(F32)).astype(BF)
        dot_ = jnp.transpose(dout.astype(F32)).astype(BF)
        off_t = _lanes(lse_ref[...] - cc_ref[...], t)
        delta = jnp.sum(dout.astype(F32) * o_ref[...].astype(F32), axis=1, keepdims=True)
        delta = _lanes(jnp.broadcast_to(delta, (t, HD)), t)
        acc_s[...] = jnp.zeros_like(acc_s)
        r_s[...] = jnp.zeros_like(r_s)

        def step(j, masked):
            off = pl.multiple_of(j * t, t)
            k = kv_ref[pl.ds(off, t), :HD]
            v = kv_ref[pl.ds(off, t), HD:]
            p = jnp.exp2(_dot_nt(q, k) - cr_ref[pl.ds(j, 1), :] - off_t)
            if masked:
                p = jnp.where(_causal_mask(i, j, t, True), p, 0.0)
            ds = p * (_dot_nt(dout, v) - delta)
            dsb = ds.astype(BF)
            acc_s[...] += _dot(dsb, k)
            dkt_ref[j] += _dot(qt, dsb)
            dvt_ref[j] += _dot(dot_, p.astype(BF))
            dck_ref[pl.ds(j, 1), :] += jnp.sum(ds, axis=0, keepdims=True)
            r_s[...] += jnp.sum(ds, axis=1, keepdims=True)

        def full_step(j, carry):
            step(j, False)
            return carry

        lax.fori_loop(0, i, full_step, 0)
        step(i, True)
        dq_ref[...] = (acc_s[...] * SCALE).astype(BF)
        dcq_ref[...] = jnp.transpose(r_s[...])[0:1, :]

        @pl.when((pl.program_id(0) == H - 1) & (i == nq - 1))
        def _():
            _rider_finish(rider[0], rsrc_ref, rdst_ref, send_sems, recv_sems)

    qspec = pl.BlockSpec((t, HD), lambda h, i: (i, h))
    bspec = pl.BlockSpec((None, t, HD), lambda h, i: (h, i, 0))
    rows = pl.BlockSpec((None, nq, t), lambda h, i: (h, 0, 0))
    tspec = pl.BlockSpec((None, nq, HD, t), lambda h, i: (h, 0, 0, 0))
    tshape = jax.ShapeDtypeStruct((H, nq, HD, t), F32)
    anywhere = pl.BlockSpec(memory_space=pl.ANY)
    return pl.pallas_call(
        body, name="fox_bwd",
        grid=(H, nq),
        in_specs=[qspec,
                  pl.BlockSpec((S, 2 * HD), lambda h, i: (0, 4 + h)),
                  qspec, qspec, bspec, bspec, rows, anywhere, anywhere],
        out_specs=[qspec, tspec, tspec, rows, pl.BlockSpec((None, None, 1, t), lambda h, i: (h, i, 0, 0)),
                   anywhere],
        out_shape=[jax.ShapeDtypeStruct((S, ZW), BF), tshape, tshape,
                   jax.ShapeDtypeStruct((H, nq, t), F32), jax.ShapeDtypeStruct((H, nq, 1, t), F32),
                   _rider_out(rider)],
        scratch_shapes=[pltpu.VMEM((t, HD), F32), pltpu.VMEM((t, HD), F32)] + _rider_sems(rider),
        input_output_aliases={7: 0},
        compiler_params=_params(2),
    )(z, z, do, o, lse_b, ccol_b, crow, dz, rider[1])


def _fox_bwd_finish(dkt, dvt, dz, t):
    nq = dkt.shape[1]
    S = nq * t

    def body(dkt_ref, dvt_ref, dz_in, dkv_ref):
        del dz_in
        dkv_ref[:, :HD] = (jnp.transpose(dkt_ref[...]) * LN2).astype(BF)
        dkv_ref[:, HD:] = jnp.transpose(dvt_ref[...]).astype(BF)

    tspec = pl.BlockSpec((None, None, HD, t), lambda h, j: (h, j, 0, 0))
    return pl.pallas_call(
        body, name="fox_bwd_finish",
        grid=(H, nq),
        in_specs=[tspec, tspec, pl.BlockSpec(memory_space=pl.ANY)],
        out_specs=pl.BlockSpec((t, 2 * HD), lambda h, j: (j, 4 + h)),
        out_shape=jax.ShapeDtypeStruct((S, ZW), BF),
        input_output_aliases={2: 0},
        compiler_params=_params(2),
    )(dkt, dvt, dz)


def _sgu_mask(transposed):
    r = lax.broadcasted_iota(jnp.int32, (L, L), 0)
    c = lax.broadcasted_iota(jnp.int32, (L, L), 1)
    if transposed:
        return (r // CHUNK) <= (c // CHUNK)
    return (c // CHUNK) <= (r // CHUNK)


def _ln_group(vs, lng, lnb):
    mu = jnp.mean(vs, axis=-1, keepdims=True)
    xc = vs - mu
    rstd = lax.rsqrt(jnp.mean(xc * xc, axis=-1, keepdims=True) + LN_EPS)
    xhat = xc * rstd
    return xhat, rstd, xhat * lng + lnb


def _mix_out_fwd(z, oa, ln_g, ln_b, ws, bst, wout, x1, g_post, tm):
    S = z.shape[0]
    nw = tm // L

    def body(u_ref, sv_ref, ga_ref, gb_ref, oa_ref, lng_ref, lnb_ref, ws_ref, bst_ref, wout_ref, x1_ref, gp_ref,
             mg_ref, y_ref, x2_ref, mg_s):
        mask = _sgu_mask(False)
        for g in range(G):
            cols = slice(g * L, (g + 1) * L)
            wm = jnp.where(mask, ws_ref[g], 0.0).astype(BF)
            bcol = bst_ref[:, g:g + 1]
            lng = lng_ref[:, cols]
            lnb = lnb_ref[:, cols]
            for w in range(nw):
                rows = slice(w * L, (w + 1) * L)
                vs = _gelu(sv_ref[rows, cols].astype(F32))
                _, _, vn = _ln_group(vs, lng, lnb)
                mixed = _dot(wm, vn.astype(BF)) + bcol
                ob = _gelu(u_ref[rows, cols].astype(F32)) * mixed
                mg = (_sigmoid(ga_ref[rows, cols].astype(F32)) * oa_ref[rows, cols].astype(F32)
                      + _sigmoid(gb_ref[rows, cols].astype(F32)) * ob)
                mg_s[rows, cols] = mg.astype(BF)
        mgb = mg_s[...]
        mg_ref[...] = mgb
        y = _dot(mgb, wout_ref[...])
        y_ref[...] = y
        x2_ref[...] = x1_ref[...] + _rms(y, gp_ref[...])

    row = pl.BlockSpec((tm, D), lambda i: (i, 0))
    vec = pl.BlockSpec((1, D), lambda i: (0, 0))

    def zcol(kb):
        return pl.BlockSpec((tm, D), lambda i: (i, kb))

    return pl.pallas_call(
        body, name="mix_out_fwd",
        grid=(S // tm,),
        in_specs=[zcol(3), zcol(4), zcol(5), zcol(6), row, vec, vec,
                  pl.BlockSpec((G, L, L), lambda i: (0, 0, 0)),
                  pl.BlockSpec((L, HD), lambda i: (0, 0)),
                  pl.BlockSpec((D, D), lambda i: (0, 0)),
                  row, vec],
        out_specs=[row, row, row],
        out_shape=[jax.ShapeDtypeStruct((S, D), BF),
                   jax.ShapeDtypeStruct((S, D), F32),
                   jax.ShapeDtypeStruct((S, D), F32)],
        scratch_shapes=[pltpu.VMEM((tm, D), BF)],
        compiler_params=_params(1),
    )(z, z, z, z, oa, ln_g, ln_b, ws, bst, wout, x1, g_post)


def _mix_out_bwd(dx2, y2, g_post, wout, z, oa, ln_g, ln_b, ws, wst, bst, tm, rider):
    S = z.shape[0]
    nw = tm // L

    def body(dx2_ref, y_ref, gp_ref, wout_ref, u_ref, sv_ref, ga_ref, gb_ref, oa_ref, lng_ref, lnb_ref,
             ws_ref, wst_ref, bst_ref, q_ref,
             dz_ref, dy_ref, doa_ref, dgp_ref, dlng_ref, dlnb_ref, dws_ref, dbst_ref, t_ref,
             dzg_s, dm_s, send_sems, recv_sems):
        i = pl.program_id(0)
        c = pl.program_id(1)

        @pl.when((i == 0) & (c == 0))
        def _():
            _rider_start(rider[0], q_ref, t_ref, send_sems, recv_sems)
            dgp_ref[...] = jnp.zeros_like(dgp_ref)
            dlng_ref[...] = jnp.zeros_like(dlng_ref)
            dlnb_ref[...] = jnp.zeros_like(dlnb_ref)
            dws_ref[...] = jnp.zeros_like(dws_ref)
            dbst_ref[...] = jnp.zeros_like(dbst_ref)

        @pl.when(c == 0)
        def _():
            dy, dg = _rms_bwd(dx2_ref[...], y_ref[...], gp_ref[...])
            dyb = dy.astype(BF)
            dy_ref[...] = dyb
            dgp_ref[...] += dg
            dm_s[...] = _dot_nt(dyb, wout_ref[...])
            mask = _sgu_mask(False)
            mask_t = _sgu_mask(True)
            lane = lax.broadcasted_iota(jnp.int32, (L, HD), 1)
            for g in range(G):
                cols = slice(g * L, (g + 1) * L)
                wm = jnp.where(mask, ws_ref[g], 0.0).astype(BF)
                wmt = jnp.where(mask_t, wst_ref[g], 0.0).astype(BF)
                bcol = bst_ref[:, g:g + 1]
                lng = lng_ref[:, cols]
                lnb = lnb_ref[:, cols]
                dws_g = jnp.zeros((L, L), F32)
                dbs_g = jnp.zeros((L, 1), F32)
                dlng_g = jnp.zeros((1, L), F32)
                dlnb_g = jnp.zeros((1, L), F32)
                for w in range(nw):
                    rows = slice(w * L, (w + 1) * L)
                    dm = dm_s[rows, cols]
                    vs, dvs_dz = _gelu_parts(sv_ref[rows, cols].astype(F32))
                    xhat, rstd, vn = _ln_group(vs, lng, lnb)
                    vnb = vn.astype(BF)
                    mixed = _dot(wm, vnb) + bcol
                    u, du_dz = _gelu_parts(u_ref[rows, cols].astype(F32))
                    sga = _sigmoid(ga_ref[rows, cols].astype(F32))
                    sgb = _sigmoid(gb_ref[rows, cols].astype(F32))
                    oav = oa_ref[rows, cols].astype(F32)
                    ob = u * mixed
                    doa_ref[rows, cols] = (dm * sga).astype(BF)
                    dzg_s[2, rows, cols] = (dm * oav * sga * (1.0 - sga)).astype(BF)
                    dzg_s[3, rows, cols] = (dm * ob * sgb * (1.0 - sgb)).astype(BF)
                    dob = dm * sgb
                    dzg_s[0, rows, cols] = (dob * mixed * du_dz).astype(BF)
                    dmixed = dob * u
                    dmb = dmixed.astype(BF)
                    dbs_g += jnp.sum(dmixed, axis=1, keepdims=True)
                    dws_g += _dot_nt(dmb, vnb)
                    dvn = _dot(wmt, dmb)
                    dlng_g += jnp.sum(dvn * xhat, axis=0, keepdims=True)
                    dlnb_g += jnp.sum(dvn, axis=0, keepdims=True)
                    dxh = dvn * lng
                    dvs = rstd * (dxh - jnp.mean(dxh, axis=-1, keepdims=True)
                                  - xhat * jnp.mean(dxh * xhat, axis=-1, keepdims=True))
                    dzg_s[1, rows, cols] = (dvs * dvs_dz).astype(BF)
                dws_ref[g] += jnp.where(mask, dws_g, 0.0)
                dbst_ref[...] += jnp.where(lane == g, dbs_g, 0.0)
                dlng_ref[:, cols] += dlng_g
                dlnb_ref[:, cols] += dlnb_g

        dz_ref[...] = dzg_s[c]

        @pl.when((i == S // tm - 1) & (c == 3))
        def _():
            _rider_finish(rider[0], q_ref, t_ref, send_sems, recv_sems)

    row = pl.BlockSpec((tm, D), lambda i, c: (i, 0))
    vec = pl.BlockSpec((1, D), lambda i, c: (0, 0))
    wsspec = pl.BlockSpec((G, L, L), lambda i, c: (0, 0, 0))
    bspec = pl.BlockSpec((L, HD), lambda i, c: (0, 0))
    anywhere = pl.BlockSpec(memory_space=pl.ANY)

    def zcol(kb):
        return pl.BlockSpec((tm, D), lambda i, c: (i, kb))

    return pl.pallas_call(
        body, name="mix_out_bwd",
        grid=(S // tm, 4),
        in_specs=[row, row, vec, pl.BlockSpec((D, D), lambda i, c: (0, 0)),
                  zcol(3), zcol(4), zcol(5), zcol(6), row, vec, vec, wsspec, wsspec, bspec, anywhere],
        out_specs=[pl.BlockSpec((tm, D), lambda i, c: (i, 3 + c)),
                   row, row, vec, vec, vec, wsspec, bspec, anywhere],
        out_shape=[jax.ShapeDtypeStruct((S, ZW), BF),
                   jax.ShapeDtypeStruct((S, D), BF),
                   jax.ShapeDtypeStruct((S, D), BF),
                   jax.ShapeDtypeStruct((1, D), F32),
                   jax.ShapeDtypeStruct((1, D), F32),
                   jax.ShapeDtypeStruct((1, D), F32),
                   jax.ShapeDtypeStruct((G, L, L), F32),
                   jax.ShapeDtypeStruct((L, HD), F32),
                   _rider_out(rider)],
        scratch_shapes=[pltpu.VMEM((4, tm, D), BF), pltpu.VMEM((tm, D), F32)] + _rider_sems(rider),
        compiler_params=_params(2),
    )(dx2, y2, g_post, wout, z, z, z, z, oa, ln_g, ln_b, ws, wst, bst, rider[1])


def _loss_head(x3, target, tm):
    S = x3.shape[0]

    def body(x_ref, t_ref, dx_ref, loss_ref):
        @pl.when(pl.program_id(0) == 0)
        def _():
            loss_ref[...] = jnp.zeros_like(loss_ref)

        e = x_ref[...] - t_ref[...]
        dx_ref[...] = e * (1.0 / D)
        loss_ref[...] += jnp.sum(e * e) * (0.5 / D)

    row = pl.BlockSpec((tm, D), lambda i: (i, 0))
    return pl.pallas_call(
        body, name="loss_head",
        grid=(S // tm,),
        in_specs=[row, row],
        out_specs=[row, pl.BlockSpec((8, HD), lambda i: (0, 0))],
        out_shape=[jax.ShapeDtypeStruct((S, D), F32), jax.ShapeDtypeStruct((8, HD), F32)],
        compiler_params=_params(1),
    )(x3, target)


def _adamw(w, g, m, v, tr):
    R, C = w.shape

    def body(w_ref, g_ref, m_ref, v_ref, d_ref, nm_ref, nv_ref):
        gv = g_ref[...]
        m_new = ADAM_B1 * m_ref[...] + (1.0 - ADAM_B1) * gv
        v_new = ADAM_B2 * v_ref[...] + (1.0 - ADAM_B2) * (gv * gv)
        m_hat = m_new / (1.0 - ADAM_B1 ** ADAM_STEP)
        v_hat = v_new / (1.0 - ADAM_B2 ** ADAM_STEP)
        d_ref[...] = -ADAM_LR * (m_hat / (jnp.sqrt(v_hat) + ADAM_EPS) + ADAM_WD * w_ref[...])
        nm_ref[...] = m_new
        nv_ref[...] = v_new

    spec = pl.BlockSpec((tr, C), lambda i: (i, 0))
    shp = jax.ShapeDtypeStruct((R, C), F32)
    return pl.pallas_call(
        body, name="adamw",
        grid=(R // tr,),
        in_specs=[spec] * 4, out_specs=[spec] * 3, out_shape=[shp] * 3,
        compiler_params=_params(1),
    )(w, g, m, v)


def _mesh_pos():
    return lax.axis_index("x"), lax.axis_index("y"), lax.axis_index("c")


def _half(c, rows):
    return pl.ds(pl.multiple_of(c * rows, 16), rows)


def _other_chips(x, y):
    return [(1 - x, y), (x, 1 - y), (1 - x, 1 - y)]


def _remote(k, src, dst, to, send_sems, recv_sems):
    return pltpu.make_async_remote_copy(src_ref=src, dst_ref=dst, send_sem=send_sems.at[k],
                                        recv_sem=recv_sems.at[k], device_id=to, device_id_type=MESH)


def _gather_start(wp_ref, g_ref, send_sems, recv_sems):
    x, y, c = _mesh_pos()
    mine = _half(c, wp_ref.shape[0] // 2)
    for k, (px, py) in enumerate(_other_chips(x, y)):
        _remote(k, wp_ref.at[mine], g_ref.at[2 * x + y, mine], (px, py, c), send_sems, recv_sems).start()


def _gather_finish(wp_ref, g_ref, send_sems, recv_sems):
    x, y, c = _mesh_pos()
    sibling = (x, y, 1 - c)
    rows = wp_ref.shape[0] // 2
    mine, other = _half(c, rows), _half(1 - c, rows)
    chips = _other_chips(x, y)
    for k, (px, py) in enumerate(chips):
        land = g_ref.at[2 * px + py, mine]
        _remote(k, land, land, (px, py, c), send_sems, recv_sems).wait_recv()
        _remote(3 + k, land, land, sibling, send_sems, recv_sems).start()
    for k, (px, py) in enumerate(chips):
        land = g_ref.at[2 * px + py, other]
        _remote(3 + k, land, land, sibling, send_sems, recv_sems).wait_recv()
    for k, (px, py) in enumerate(chips):
        land = g_ref.at[2 * px + py, mine]
        _remote(k, wp_ref.at[mine], g_ref.at[2 * x + y, mine], (px, py, c), send_sems, recv_sems).wait_send()
        _remote(3 + k, land, land, sibling, send_sems, recv_sems).wait_send()


def _place_own_shard(g, wp):
    x, y, _ = _mesh_pos()
    return lax.dynamic_update_index_in_dim(g, wp, 2 * x + y, 0)


def _all_gather_weights(wp):
    def body(wp_ref, g_ref, send_sems, recv_sems):
        _gather_start(wp_ref, g_ref, send_sems, recv_sems)
        _gather_finish(wp_ref, g_ref, send_sems, recv_sems)

    g = pl.pallas_call(
        body, name="all_gather_weights",
        in_specs=[pl.BlockSpec(memory_space=pl.ANY)],
        out_specs=pl.BlockSpec(memory_space=pl.ANY),
        out_shape=jax.ShapeDtypeStruct((NSH,) + wp.shape, wp.dtype),
        scratch_shapes=[pltpu.SemaphoreType.DMA((6,)), pltpu.SemaphoreType.DMA((6,))],
        compiler_params=pltpu.CompilerParams(has_side_effects=True),
    )(wp)
    return _place_own_shard(g, wp)


def _scatter_copies(q_ref, t_ref, send_sems, recv_sems):
    x, y, c = _mesh_pos()
    return [_remote(k, q_ref.at[2 * px + py], t_ref.at[k], (px, py, c), send_sems, recv_sems)
            for k, (px, py) in enumerate(_other_chips(x, y))]


def _scatter_start(q_ref, t_ref, send_sems, recv_sems):
    for cp in _scatter_copies(q_ref, t_ref, send_sems, recv_sems):
        cp.start()


def _scatter_finish(q_ref, t_ref, send_sems, recv_sems):
    for cp in _scatter_copies(q_ref, t_ref, send_sems, recv_sems):
        cp.wait()


def _rider_copies(kind, src_ref, dst_ref, send_sems, recv_sems):
    if kind == "scatter":
        return _scatter_copies(src_ref, dst_ref, send_sems, recv_sems)
    x, y, c = _mesh_pos()
    rows = src_ref.shape[1] // 2
    return [_remote(0, src_ref.at[:, _half(1 - c, rows)], dst_ref, (x, y, 1 - c), send_sems, recv_sems)]


def _rider_start(kind, src_ref, dst_ref, send_sems, recv_sems):
    for cp in _rider_copies(kind, src_ref, dst_ref, send_sems, recv_sems):
        cp.start()


def _rider_finish(kind, src_ref, dst_ref, send_sems, recv_sems):
    for cp in _rider_copies(kind, src_ref, dst_ref, send_sems, recv_sems):
        cp.wait()


def _rider_out(rider):
    kind, a = rider
    if kind == "scatter":
        return jax.ShapeDtypeStruct((3,) + a.shape[1:], a.dtype)
    return jax.ShapeDtypeStruct((a.shape[0], a.shape[1] // 2) + a.shape[2:], a.dtype)


def _rider_sems(rider):
    n = 3 if rider[0] == "scatter" else 1
    return [pltpu.SemaphoreType.DMA((n,)), pltpu.SemaphoreType.DMA((n,))]


def _pair_exchange(p):
    rows = p.shape[1] // 2

    def body(p_ref, r_ref, send_sem, recv_sem):
        x, y, c = _mesh_pos()
        cp = pltpu.make_async_remote_copy(src_ref=p_ref.at[:, _half(1 - c, rows)], dst_ref=r_ref, send_sem=send_sem,
                                          recv_sem=recv_sem, device_id=(x, y, 1 - c), device_id_type=MESH)
        cp.start()
        cp.wait()

    return pl.pallas_call(
        body, name="pair_exchange",
        in_specs=[pl.BlockSpec(memory_space=pl.ANY)],
        out_specs=pl.BlockSpec(memory_space=pl.ANY),
        out_shape=jax.ShapeDtypeStruct((NSH, rows, D), F32),
        scratch_shapes=[pltpu.SemaphoreType.DMA, pltpu.SemaphoreType.DMA],
        compiler_params=pltpu.CompilerParams(has_side_effects=True),
    )(p)


def _pair_add(p, r, nb):
    rows = r.shape[1]
    tr = rows // nb

    def body(p_ref, r_ref, q_ref):
        q_ref[...] = (p_ref[...] + r_ref[...]).astype(BF)

    return pl.pallas_call(
        body, name="pair_add", grid=(NSH, nb),
        in_specs=[pl.BlockSpec((None, tr, D), lambda j, i: (j, lax.axis_index("c") * nb + i, 0)),
                  pl.BlockSpec((None, tr, D), lambda j, i: (j, i, 0))],
        out_specs=pl.BlockSpec((None, tr, D), lambda j, i: (j, i, 0)),
        out_shape=jax.ShapeDtypeStruct((NSH, rows, D), BF),
        compiler_params=_params(2),
    )(p, r)


def _chip_exchange(q, small):
    def body(q_ref, s_ref, t_ref, sm_ref, send_sems, recv_sems, ssend_sems, srecv_sems):
        x, y, c = _mesh_pos()
        me = 4 * x + 2 * y + c
        _scatter_start(q_ref, t_ref, send_sems, recv_sems)
        flips = [(fx, fy, fc) for fx in (0, 1) for fy in (0, 1) for fc in (0, 1)][1:]
        smalls = []
        for k, (fx, fy, fc) in enumerate(flips):
            cp = pltpu.make_async_remote_copy(src_ref=s_ref, dst_ref=sm_ref.at[me],
                                              send_sem=ssend_sems.at[k], recv_sem=srecv_sems.at[k],
                                              device_id=(x ^ fx, y ^ fy, c ^ fc), device_id_type=MESH)
            cp.start()
            smalls.append(cp)
        _scatter_finish(q_ref, t_ref, send_sems, recv_sems)
        for cp in smalls:
            cp.wait()

    t, sm = pl.pallas_call(
        body, name="chip_exchange",
        in_specs=[pl.BlockSpec(memory_space=pl.ANY), pl.BlockSpec(memory_space=pl.ANY)],
        out_specs=[pl.BlockSpec(memory_space=pl.ANY), pl.BlockSpec(memory_space=pl.ANY)],
        out_shape=[jax.ShapeDtypeStruct((3,) + q.shape[1:], BF),
                   jax.ShapeDtypeStruct((NDEV, SMALL_ROWS, D), F32)],
        scratch_shapes=[pltpu.SemaphoreType.DMA((3,)), pltpu.SemaphoreType.DMA((3,)),
                        pltpu.SemaphoreType.DMA((7,)), pltpu.SemaphoreType.DMA((7,))],
        compiler_params=pltpu.CompilerParams(has_side_effects=True),
    )(q, small)
    x, y, c = _mesh_pos()
    return t, lax.dynamic_update_index_in_dim(sm, small, 4 * x + 2 * y + c, 0)


def _shard_sum(p, r, t, nb):
    rows = r.shape[1]
    tr = rows // nb

    def shard():
        return 2 * lax.axis_index("x") + lax.axis_index("y")

    def body(p_ref, r_ref, t_ref, o_ref):
        s = p_ref[...] + r_ref[...]
        for k in range(3):
            s = s + t_ref[k].astype(F32)
        o_ref[...] = s

    return pl.pallas_call(
        body, name="shard_sum", grid=(nb,),
        in_specs=[pl.BlockSpec((None, tr, D), lambda i: (shard(), lax.axis_index("c") * nb + i, 0)),
                  pl.BlockSpec((None, tr, D), lambda i: (shard(), i, 0)),
                  pl.BlockSpec((3, tr, D), lambda i: (0, i, 0))],
        out_specs=pl.BlockSpec((tr, D), lambda i: (i, 0)),
        out_shape=jax.ShapeDtypeStruct((rows, D), F32),
        compiler_params=_params(1),
    )(p, r, t)


def _small_sum(sm):
    def body(sm_ref, o_ref):
        s = sm_ref[0]
        for k in range(1, NDEV):
            s = s + sm_ref[k]
        o_ref[...] = s

    return pl.pallas_call(
        body, name="small_sum",
        in_specs=[pl.BlockSpec(memory_space=pltpu.VMEM)],
        out_specs=pl.BlockSpec(memory_space=pltpu.VMEM),
        out_shape=jax.ShapeDtypeStruct((SMALL_ROWS, D), F32),
    )(sm)


def _pair_gather(halves):
    n = len(halves)

    def body(*refs):
        gh_refs, o_refs, (send_sems, recv_sems) = refs[:n], refs[n:2 * n], refs[2 * n:]
        x, y, c = _mesh_pos()
        sibling = (x, y, 1 - c)
        for g in range(n):
            rows = gh_refs[g].shape[0]
            _remote(g, gh_refs[g], o_refs[g].at[_half(c, rows)], sibling, send_sems, recv_sems).start()
        for g in range(n):
            rows = gh_refs[g].shape[0]
            _remote(g, gh_refs[g], o_refs[g].at[_half(c, rows)], sibling, send_sems, recv_sems).wait_send()
            _remote(g, gh_refs[g], o_refs[g].at[_half(1 - c, rows)], sibling, send_sems, recv_sems).wait_recv()

    anywhere = pl.BlockSpec(memory_space=pl.ANY)
    outs = pl.pallas_call(
        body, name="pair_gather",
        in_specs=[anywhere] * n, out_specs=[anywhere] * n,
        out_shape=[jax.ShapeDtypeStruct((2 * h.shape[0], D), F32) for h in halves],
        scratch_shapes=[pltpu.SemaphoreType.DMA((n,)), pltpu.SemaphoreType.DMA((n,))],
        compiler_params=pltpu.CompilerParams(has_side_effects=True),
    )(*halves)
    c = lax.axis_index("c")
    return [lax.dynamic_update_slice_in_dim(o, h, c * h.shape[0], 0) for o, h in zip(outs, halves)]


def _pad_cols(a, n):
    return jnp.pad(a, ((0, 0), (0, n - a.shape[1])))


def _split_w_in(w_in_full):
    q, k, v = w_in_full[:, :D], w_in_full[:, D:2 * D], w_in_full[:, 2 * D:3 * D]
    f = w_in_full[:, 3 * D:3 * D + H]
    gates = w_in_full[:, 3 * D + H:]
    kv = jnp.stack([k.reshape(D, H, HD), v.reshape(D, H, HD)], axis=2).reshape(D, 2 * D)
    return jnp.concatenate([q, kv, gates], axis=1), _pad_cols(f, HD)


def _merge_w_in_grad(dwcat, dwf):
    kv = dwcat[:, D:3 * D].reshape(D, H, 2, HD)
    return jnp.concatenate([dwcat[:, :D], kv[:, :, 0].reshape(D, D), kv[:, :, 1].reshape(D, D),
                            dwf[:, :H], dwcat[:, 3 * D:]], axis=1)


def _ffn_weight_grads(h, da, db, act, dy, bt):
    g = _mm_tn(h, da, D, D, bt, into=(None, FFN_ROWS, 0, "cols"))
    g = _mm_tn(h, db, D, D, bt, into=(g, FFN_ROWS, 1, "cols"))
    return _mm_tn(act, dy, D, D, bt, into=(g, FFN_ROWS, 2, "rows"))


def _train_step(x, target, wp1, wp2, small, tm, t_attn):
    S = x.shape[0]
    g1pre, g1post = small["ffn1_pre_g"], small["ffn1_post_g"]
    gmpre, gmpost = small["mix_pre_g"], small["mix_post_g"]
    g2pre, g2post = small["ffn2_pre_g"], small["ffn2_post_g"]
    ln_g, ln_b = small["sgu_ln_g"], small["sgu_ln_b"]
    ws = small["sgu_w_s"][0]
    wst = jnp.swapaxes(ws, 1, 2)
    bst = _pad_cols(small["sgu_b_s"][0].T, HD)
    bf = _pad_cols(small["b_forget"], HD)

    w1 = _all_gather_weights(wp1)
    h1, a1, b1, y1, x1, w2 = _ffn_fwd(x, g1pre, w1, g1post, tm, gather=wp2)
    wout = w2[:, FFN_ROWS:FFN_ROWS + 256, :].reshape(D, D)
    r0 = FFN_ROWS + 256
    w_in_full = jnp.concatenate(
        [blk for j in range(NSH) for blk in (w2[j, r0:r0 + D], w2[j, r0 + D:r0 + 2 * D, :WIN_SH - D])], axis=1)
    wcat, wf = _split_w_in(w_in_full)
    h2 = _norm_fwd(x1, gmpre, tm)
    z = _mm(h2, wcat, min(1024, S), D, BF, first_block_scale=SCALE * LOG2E)
    zf = _mm(h2, wf, tm, HD, F32)
    cs = min(512, S)
    c, ccol_b = _forget_cumsum(zf, bf, cs)
    crow = jnp.transpose(c[:, :H]).reshape(H, S // t_attn, t_attn)
    oa, lse_b = _fox_fwd(z, ccol_b, crow, t_attn)
    merged, y2, x2 = _mix_out_fwd(z, oa, ln_g, ln_b, ws, bst, wout, x1, gmpost, tm)
    h3, a3, b3, y3, x3 = _ffn_fwd(x2, g2pre, w2, g2post, tm)
    dx3, loss_acc = _loss_head(x3, target, tm)
    loss = loss_acc[0, 0]

    dy3, da3, db3, act3, dx2, dg2post, dg2pre = _ffn_bwd(dx3, y3, g2post, a3, b3, w2, x2, g2pre, tm)
    bt = min(2048, S)
    g_ffn2 = _ffn_weight_grads(h3, da3, db3, act3, dy3, bt)

    dz, dy2, doa, dgmpost, dlng, dlnb, dws, dbst, r_ffn2 = _mix_out_bwd(
        dx2, y2, gmpost, wout, z, oa, ln_g, ln_b, ws, wst, bst, tm, ("exchange", g_ffn2))
    q_ffn2 = _pair_add(g_ffn2, r_ffn2, 3)
    g_mix = _mm_tn(merged, dy2, 256, D, bt, into=(None, MIX_ROWS, 0, "rows"))
    dz, dkt, dvt, dc_keys, dc_queries, t_ffn2 = _fox_bwd(z, doa, oa, lse_b, ccol_b, crow, dz, t_attn,
                                                         ("scatter", q_ffn2))
    dz = _fox_bwd_finish(dkt, dvt, dz, t_attn)
    dc = _pad_cols(jnp.transpose(dc_queries.reshape(H, S) - dc_keys.reshape(H, S)), HD)
    dzf, dbf = _forget_bwd(dc, zf, bf, cs)
    dwcat = _mm_tn(h2, dz, D, D, bt)
    dwf = _mm_tn(h2, dzf, D, HD, bt)
    dwin = _merge_w_in_grad(dwcat, dwf)
    dwin_a = jnp.stack([dwin[:, j * WIN_SH:j * WIN_SH + D] for j in range(NSH)])
    dwin_b = jnp.stack([_pad_cols(dwin[:, j * WIN_SH + D:(j + 1) * WIN_SH], D) for j in range(NSH)])
    g_mix = lax.dynamic_update_slice(g_mix, jnp.concatenate([dwin_a, dwin_b], axis=1), (0, 256, 0))
    dx1, dgmpre, r_mix = _mix_in_bwd(dz, wcat, dzf, wf, x1, gmpre, dx2, tm, ("exchange", g_mix))
    q_mix = _pair_add(g_mix, r_mix, 3)

    dy1, da1, db1, act1, dx, dg1post, dg1pre, t_mix = _ffn_bwd(dx1, y1, g1post, a1, b1, w1, x, g1pre, tm,
                                                               scatter=q_mix)
    g_ffn1 = _ffn_weight_grads(h1, da1, db1, act1, dy1, bt)
    r_ffn1 = _pair_exchange(g_ffn1)
    q_ffn1 = _pair_add(g_ffn1, r_ffn1, 3)

    gsmall = {
        "ffn1_pre_g": dg1pre, "ffn1_post_g": dg1post, "mix_pre_g": dgmpre, "mix_post_g": dgmpost,
        "ffn2_pre_g": dg2pre, "ffn2_post_g": dg2post, "sgu_ln_g": dlng, "sgu_ln_b": dlnb,
        "sgu_w_s": dws[None], "sgu_b_s": jnp.transpose(dbst[:, :G])[None], "b_forget": dbf[:, :H],
    }
    t_ffn1, sm = _chip_exchange(q_ffn1, _pack_small(gsmall))
    halves = [_shard_sum(g, r, t, 3) for g, r, t in ((g_ffn1, r_ffn1, t_ffn1), (g_ffn2, r_ffn2, t_ffn2),
                                                     (g_mix, r_mix, t_mix))]
    f_ffn1, f_ffn2, f_mix = _pair_gather(halves)
    return loss, dx, f_ffn1, f_ffn2, f_mix, _small_sum(sm)


_SMALL_NAMES = ["ffn1_pre_g", "ffn1_post_g", "mix_pre_g", "mix_post_g", "ffn2_pre_g", "ffn2_post_g",
                "sgu_ln_g", "sgu_ln_b", "sgu_b_s", "b_forget", "sgu_w_s"]
_SMALL_SHAPES = {"sgu_b_s": (1, G, L), "b_forget": (1, H), "sgu_w_s": (1, G, L, L)}


def _pack_small(d):
    rows = []
    for n in _SMALL_NAMES:
        a = d[n].astype(F32)
        if n == "b_forget":
            a = _pad_cols(a, D)
        a = a.reshape(-1, D)
        rows.append(jnp.pad(a, ((0, -a.shape[0] % SMALL_STRIDE), (0, 0))))
    return jnp.concatenate(rows, axis=0)


def _unpack_small(p):
    out, r = {}, 0
    for n in _SMALL_NAMES:
        if n == "sgu_w_s":
            out[n] = p[r:r + L].reshape(1, G, L, L)
            r += L
        elif n == "b_forget":
            out[n] = p[r:r + 1, :H]
            r += SMALL_STRIDE
        elif n == "sgu_b_s":
            out[n] = p[r:r + 1].reshape(1, G, L)
            r += SMALL_STRIDE
        else:
            out[n] = p[r:r + 1]
            r += SMALL_STRIDE
    return out


_BIG_NAMES = ["ffn1_w_gate", "ffn1_w_up", "ffn1_w_down", "ffn2_w_gate", "ffn2_w_up", "ffn2_w_down", "w_out", "w_in"]
_WEIGHT_ORDER = ['ffn1_pre_g', 'ffn1_w_gate', 'ffn1_w_up', 'ffn1_w_down', 'ffn1_post_g', 'mix_pre_g', 'w_in', 'b_forget',
                 'sgu_ln_g', 'sgu_ln_b', 'sgu_w_s', 'sgu_b_s', 'w_out', 'mix_post_g', 'ffn2_pre_g', 'ffn2_w_gate',
                 'ffn2_w_up', 'ffn2_w_down', 'ffn2_post_g']


def _pack_ffn(w, name):
    return jnp.concatenate([w[name + "_w_gate"][0], w[name + "_w_up"][0], w[name + "_w_down"][0]], axis=0)


def _pack_mix(w):
    w_in = w["w_in"][0]
    return jnp.concatenate([w["w_out"][0], w_in[:, :D], _pad_cols(w_in[:, D:], D)], axis=0)


def _unpack_ffn(p, name):
    return {name + "_w_gate": p[:D][None], name + "_w_up": p[D:2 * D][None], name + "_w_down": p[2 * D:][None]}


def _unpack_mix(p):
    return {"w_out": p[:256][None],
            "w_in": jnp.concatenate([p[256:256 + D], p[256 + D:, :WIN_SH - D]], axis=1)[None]}


def _step(args, tm, t_attn):
    x = args["x"][0]
    target = args["loss_target"][0]
    weights = {n: args[n] for n in _WEIGHT_ORDER}
    small = {n: weights[n] for n in _SMALL_NAMES}

    wb = {n: weights[n].astype(BF) for n in _BIG_NAMES}
    wp1 = _pack_ffn(wb, "ffn1")
    wp2 = jnp.concatenate([_pack_ffn(wb, "ffn2"), _pack_mix(wb)], axis=0)
    loss_local, dx, f_ffn1, f_ffn2, f_mix, gsm = _train_step(x, target, wp1, wp2, small, tm, t_attn)
    loss = lax.psum(loss_local, ("x", "y", "c"))
    grads = {**_unpack_ffn(f_ffn1, "ffn1"), **_unpack_ffn(f_ffn2, "ffn2"), **_unpack_mix(f_mix),
             **_unpack_small(gsm)}

    delta, new_m, new_v = {}, {}, {}
    for n in _BIG_NAMES:
        shp = weights[n].shape
        w2 = weights[n].reshape(-1, shp[-1])
        rows = w2.shape[0]
        d, nm, nv = _adamw(w2, grads[n].reshape(w2.shape), args["m_" + n].reshape(w2.shape),
                           args["v_" + n].reshape(w2.shape), rows // 4)
        delta[n], new_m[n], new_v[n] = d.reshape(shp), nm.reshape(shp), nv.reshape(shp)
    ds, nms, nvs = _adamw(_pack_small(small), gsm, _pack_small({n: args["m_" + n] for n in _SMALL_NAMES}),
                          _pack_small({n: args["v_" + n] for n in _SMALL_NAMES}), SMALL_ROWS)
    delta.update(_unpack_small(ds))
    new_m.update(_unpack_small(nms))
    new_v.update(_unpack_small(nvs))

    return (loss, dx[None], *[grads[n] for n in _WEIGHT_ORDER], *[delta[n] for n in _WEIGHT_ORDER],
            *[new_m[n] for n in _WEIGHT_ORDER], *[new_v[n] for n in _WEIGHT_ORDER])


_ARG_NAMES = (["x"] + _WEIGHT_ORDER + ["loss_target"] + ["m_" + n for n in _WEIGHT_ORDER]
              + ["v_" + n for n in _WEIGHT_ORDER])


def kernel(x, ffn1_pre_g, ffn1_w_gate, ffn1_w_up, ffn1_w_down, ffn1_post_g, mix_pre_g, w_in, b_forget, sgu_ln_g, sgu_ln_b, sgu_w_s, sgu_b_s, w_out, mix_post_g, ffn2_pre_g, ffn2_w_gate, ffn2_w_up, ffn2_w_down, ffn2_post_g, loss_target, m_ffn1_pre_g, m_ffn1_w_gate, m_ffn1_w_up, m_ffn1_w_down, m_ffn1_post_g, m_mix_pre_g, m_w_in, m_b_forget, m_sgu_ln_g, m_sgu_ln_b, m_sgu_w_s, m_sgu_b_s, m_w_out, m_mix_post_g, m_ffn2_pre_g, m_ffn2_w_gate, m_ffn2_w_up, m_ffn2_w_down, m_ffn2_post_g, v_ffn1_pre_g, v_ffn1_w_gate, v_ffn1_w_up, v_ffn1_w_down, v_ffn1_post_g, v_mix_pre_g, v_w_in, v_b_forget, v_sgu_ln_g, v_sgu_ln_b, v_sgu_w_s, v_sgu_b_s, v_w_out, v_mix_post_g, v_ffn2_pre_g, v_ffn2_w_gate, v_ffn2_w_up, v_ffn2_w_down, v_ffn2_post_g):
    args = (x, ffn1_pre_g, ffn1_w_gate, ffn1_w_up, ffn1_w_down, ffn1_post_g, mix_pre_g, w_in, b_forget, sgu_ln_g, sgu_ln_b, sgu_w_s, sgu_b_s, w_out, mix_post_g, ffn2_pre_g, ffn2_w_gate, ffn2_w_up, ffn2_w_down, ffn2_post_g, loss_target, m_ffn1_pre_g, m_ffn1_w_gate, m_ffn1_w_up, m_ffn1_w_down, m_ffn1_post_g, m_mix_pre_g, m_w_in, m_b_forget, m_sgu_ln_g, m_sgu_ln_b, m_sgu_w_s, m_sgu_b_s, m_w_out, m_mix_post_g, m_ffn2_pre_g, m_ffn2_w_gate, m_ffn2_w_up, m_ffn2_w_down, m_ffn2_post_g, v_ffn1_pre_g, v_ffn1_w_gate, v_ffn1_w_up, v_ffn1_w_down, v_ffn1_post_g, v_mix_pre_g, v_w_in, v_b_forget, v_sgu_ln_g, v_sgu_ln_b, v_sgu_w_s, v_sgu_b_s, v_w_out, v_mix_post_g, v_ffn2_pre_g, v_ffn2_w_gate, v_ffn2_w_up, v_ffn2_w_down, v_ffn2_post_g)
    named = dict(zip(_ARG_NAMES, args))
    tile = min(512, x.shape[1])
    return _step(named, tile, tile)
```

```python
import functools
import math

import jax
import jax.numpy as jnp
from jax import lax
from jax.experimental import pallas as pl
from jax.experimental.pallas import tpu as pltpu

D = 1024
F = 4096
H = 8
HD = 128
G = 8
L = 128
CHUNK = 64
NSH = 4
NDEV = 8
ZW = 7 * D
RMS_EPS = 1e-6
LN_EPS = 1e-5
NEG = -1e30
SCALE = 1.0 / math.sqrt(HD)
LOG2E = math.log2(math.e)
LN2 = math.log(2.0)

ADAM_LR = 0.001
ADAM_B1 = 0.9
ADAM_B2 = 0.999
ADAM_EPS = 1e-08
ADAM_WD = 0.01
ADAM_STEP = 10

VMEM_LIMIT_BYTES = 56 * 1024 * 1024

WIN_SH = 1794
FFN_ROWS = 3 * D
MIX_ROWS = 256 + 2 * D
G2_ROWS = FFN_ROWS + MIX_ROWS
SMALL_STRIDE = 8
SMALL_ROWS = 10 * SMALL_STRIDE + L

BF = jnp.bfloat16
F32 = jnp.float32
MESH = pl.DeviceIdType.MESH


def _params(n_grid):
    return pltpu.CompilerParams(dimension_semantics=("arbitrary",) * n_grid,
                                vmem_limit_bytes=VMEM_LIMIT_BYTES)


def _dot(a, b):
    return jnp.dot(a, b, preferred_element_type=F32)


def _dot_nt(a, b):
    return lax.dot_general(a, b, (((1,), (1,)), ((), ())), preferred_element_type=F32)


def _dot_tn(a, b):
    return lax.dot_general(a, b, (((0,), (0,)), ((), ())), preferred_element_type=F32)


def _rms(x, g):
    r = lax.rsqrt(jnp.mean(x * x, axis=-1, keepdims=True) + RMS_EPS)
    return x * r * g


def _rms_bwd(dn, x, g):
    r = lax.rsqrt(jnp.mean(x * x, axis=-1, keepdims=True) + RMS_EPS)
    xr = x * r
    dg = jnp.sum(dn * xr, axis=0, keepdims=True)
    t = dn * g
    dx = r * (t - xr * jnp.mean(t * xr, axis=-1, keepdims=True))
    return dx, dg


def _gelu_parts(x):
    cdf = 0.5 * (1.0 + lax.erf(x * (1.0 / math.sqrt(2.0))))
    pdf = jnp.exp(-0.5 * x * x) * (1.0 / math.sqrt(2.0 * math.pi))
    return x * cdf, cdf + x * pdf


def _gelu(x):
    return x * (0.5 * (1.0 + lax.erf(x * (1.0 / math.sqrt(2.0)))))


def _sigmoid(x):
    return 1.0 / (1.0 + jnp.exp(-x))


def _ffn_fwd(x, g_pre, wpack, g_post, tm, gather=None):
    S = x.shape[0]
    nt, nf, tf = S // tm, NSH, D

    def body(x_ref, gpre_ref, wg_ref, wu_ref, wd_ref, gpost_ref, *rest):
        if gather is None:
            h_ref, a_ref, b_ref, y_ref, xo_ref, h_s, acc = rest
        else:
            wp_ref, h_ref, a_ref, b_ref, y_ref, xo_ref, g_ref, h_s, acc, send_sems, recv_sems = rest
        i = pl.program_id(0)
        j = pl.program_id(1)

        if gather is not None:
            @pl.when((i == 0) & (j == 0))
            def _():
                _gather_start(wp_ref, g_ref, send_sems, recv_sems)

        @pl.when(j == 0)
        def _():
            h = _rms(x_ref[...], gpre_ref[...]).astype(BF)
            h_s[...] = h
            h_ref[...] = h
            acc[...] = jnp.zeros_like(acc)

        h = h_s[...]
        a = _dot(h, wg_ref[...])
        b = _dot(h, wu_ref[...])
        a_ref[...] = a.astype(BF)
        b_ref[...] = b.astype(BF)
        act = (a * _sigmoid(a)) * b
        acc[...] += _dot(act.astype(BF), wd_ref[...])

        @pl.when(j == nf - 1)
        def _():
            y = acc[...]
            y_ref[...] = y
            xo_ref[...] = x_ref[...] + 0.5 * _rms(y, gpost_ref[...])

        if gather is not None:
            @pl.when((i == nt - 1) & (j == nf - 1))
            def _():
                _gather_finish(wp_ref, g_ref, send_sems, recv_sems)

    row = pl.BlockSpec((tm, D), lambda i, j: (i, 0))
    vec = pl.BlockSpec((1, D), lambda i, j: (0, 0))
    anywhere = pl.BlockSpec(memory_space=pl.ANY)
    in_specs = [row, vec,
                pl.BlockSpec((None, D, tf), lambda i, j: (j, 0, 0)),
                pl.BlockSpec((None, D, tf), lambda i, j: (j, 1, 0)),
                pl.BlockSpec((None, tf, D), lambda i, j: (j, 2, 0)),
                vec]
    out_specs = [row,
                 pl.BlockSpec((tm, tf), lambda i, j: (i, j)),
                 pl.BlockSpec((tm, tf), lambda i, j: (i, j)),
                 row, row]
    out_shape = [jax.ShapeDtypeStruct((S, D), BF),
                 jax.ShapeDtypeStruct((S, F), BF),
                 jax.ShapeDtypeStruct((S, F), BF),
                 jax.ShapeDtypeStruct((S, D), F32),
                 jax.ShapeDtypeStruct((S, D), F32)]
    scratch = [pltpu.VMEM((tm, D), BF), pltpu.VMEM((tm, D), F32)]
    args = [x, g_pre, wpack, wpack, wpack, g_post]
    if gather is not None:
        in_specs.append(anywhere)
        out_specs.append(anywhere)
        out_shape.append(jax.ShapeDtypeStruct((NSH,) + gather.shape, gather.dtype))
        scratch += [pltpu.SemaphoreType.DMA((6,)), pltpu.SemaphoreType.DMA((6,))]
        args.append(gather)
    res = list(pl.pallas_call(
        body, name="ffn_fwd" if gather is None else "ffn_fwd_gather",
        grid=(nt, nf),
        in_specs=in_specs, out_specs=out_specs, out_shape=out_shape, scratch_shapes=scratch,
        compiler_params=_params(2),
    )(*args))
    if gather is not None:
        res[5] = _place_own_shard(res[5], gather)
    return res


def _ffn_bwd(dxo, y, g_post, a, b, wpack, x_in, g_pre, tm, riders=()):
    S = dxo.shape[0]
    nt, nf, tf = S // tm, NSH, D
    nr = len(riders)

    def body(dxo_ref, y_ref, gpost_ref, a_ref, b_ref, wg_ref, wu_ref, wd_ref, xin_ref, gpre_ref, *rest):
        rsrc = rest[:nr]
        dy_ref, da_ref, db_ref, act_ref, dxin_ref, dgpost_ref, dgpre_ref = rest[nr:nr + 7]
        rdst = rest[nr + 7:2 * nr + 7]
        dy_s, acc = rest[2 * nr + 7:2 * nr + 9]
        sems = rest[2 * nr + 9:]
        i = pl.program_id(0)
        j = pl.program_id(1)

        @pl.when((i == 0) & (j == 0))
        def _():
            dgpost_ref[...] = jnp.zeros_like(dgpost_ref)
            dgpre_ref[...] = jnp.zeros_like(dgpre_ref)
            for k, (kind, _) in enumerate(riders):
                _rider_start(kind, rsrc[k], rdst[k], sems[2 * k], sems[2 * k + 1])

        @pl.when(j == 0)
        def _():
            dy, dg = _rms_bwd(0.5 * dxo_ref[...], y_ref[...], gpost_ref[...])
            dyb = dy.astype(BF)
            dy_s[...] = dyb
            dy_ref[...] = dyb
            dgpost_ref[...] += dg
            acc[...] = jnp.zeros_like(acc)

        dact = _dot_nt(dy_s[...], wd_ref[...])
        av = a_ref[...].astype(F32)
        bv = b_ref[...].astype(F32)
        sig = _sigmoid(av)
        sl = av * sig
        act_ref[...] = (sl * bv).astype(BF)
        dbb = (dact * sl).astype(BF)
        dab = (dact * bv * (sig * (1.0 + av * (1.0 - sig)))).astype(BF)
        da_ref[...] = dab
        db_ref[...] = dbb
        acc[...] += _dot_nt(dab, wg_ref[...]) + _dot_nt(dbb, wu_ref[...])

        @pl.when(j == nf - 1)
        def _():
            dx, dg = _rms_bwd(acc[...], xin_ref[...], gpre_ref[...])
            dxin_ref[...] = dxo_ref[...] + dx
            dgpre_ref[...] += dg

        if riders:
            @pl.when((i == nt - 1) & (j == nf - 1))
            def _():
                for k, (kind, _) in enumerate(riders):
                    _rider_finish(kind, rsrc[k], rdst[k], sems[2 * k], sems[2 * k + 1])

    row = pl.BlockSpec((tm, D), lambda i, j: (i, 0))
    vec = pl.BlockSpec((1, D), lambda i, j: (0, 0))
    ff = pl.BlockSpec((tm, tf), lambda i, j: (i, j))
    anywhere = pl.BlockSpec(memory_space=pl.ANY)
    in_specs = [row, row, vec, ff, ff,
                pl.BlockSpec((None, D, tf), lambda i, j: (j, 0, 0)),
                pl.BlockSpec((None, D, tf), lambda i, j: (j, 1, 0)),
                pl.BlockSpec((None, tf, D), lambda i, j: (j, 2, 0)),
                row, vec]
    out_specs = [row, ff, ff, ff, row, vec, vec]
    out_shape = [jax.ShapeDtypeStruct((S, D), BF),
                 jax.ShapeDtypeStruct((S, F), BF),
                 jax.ShapeDtypeStruct((S, F), BF),
                 jax.ShapeDtypeStruct((S, F), BF),
                 jax.ShapeDtypeStruct((S, D), F32),
                 jax.ShapeDtypeStruct((1, D), F32),
                 jax.ShapeDtypeStruct((1, D), F32)]
    scratch = [pltpu.VMEM((tm, D), BF), pltpu.VMEM((tm, D), F32)]
    args = [dxo, y, g_post, a, b, wpack, wpack, wpack, x_in, g_pre]
    for rider in riders:
        in_specs.append(anywhere)
        out_specs.append(anywhere)
        out_shape.append(_rider_out(rider))
        scratch += _rider_sems(rider)
        args.append(rider[1])
    return pl.pallas_call(
        body, name="ffn_bwd" if not riders else "ffn_bwd_riders",
        grid=(nt, nf),
        in_specs=in_specs, out_specs=out_specs, out_shape=out_shape, scratch_shapes=scratch,
        compiler_params=_params(2),
    )(*args)


def _mm_tn(a, b, bm, bn, bt, into=None, rider=None):
    S, M = a.shape
    N = b.shape[1]
    nt = S // bt
    n_in = 2 + (into is not None and into[0] is not None) + (rider is not None)

    def body(*refs):
        a_ref, b_ref, o_ref = refs[0], refs[1], refs[n_in]
        m, n, t = pl.program_id(0), pl.program_id(1), pl.program_id(2)

        if rider is not None:
            rsrc_ref, rdst_ref, send_sems, recv_sems = refs[n_in - 1], refs[n_in + 1], refs[-2], refs[-1]

            @pl.when((m == 0) & (n == 0) & (t == 0))
            def _():
                _rider_start(rider[0], rsrc_ref, rdst_ref, send_sems, recv_sems)

        @pl.when(t == 0)
        def _():
            o_ref[...] = jnp.zeros_like(o_ref)

        o_ref[...] += _dot_tn(a_ref[...], b_ref[...])

        if rider is not None:
            @pl.when((m == M // bm - 1) & (n == N // bn - 1) & (t == nt - 1))
            def _():
                _rider_finish(rider[0], rsrc_ref, rdst_ref, send_sems, recv_sems)

    in_specs = [pl.BlockSpec((bt, bm), lambda m, n, t: (t, m)),
                pl.BlockSpec((bt, bn), lambda m, n, t: (t, n))]
    args, aliases = [a, b], {}
    if into is None:
        out_spec = pl.BlockSpec((bm, bn), lambda m, n, t: (m, n))
        out_shape = jax.ShapeDtypeStruct((M, N), F32)
    else:
        buf, rows, rb, by = into
        assert bn == D and (M == bm if by == "cols" else (M == NSH * bm and N == D))
        if by == "cols":
            out_spec = pl.BlockSpec((None, bm, bn), lambda m, n, t: (n, rb, 0))
        else:
            out_spec = pl.BlockSpec((None, bm, bn), lambda m, n, t: (m, rb, 0))
        out_shape = jax.ShapeDtypeStruct((NSH, rows, D), F32)
        if buf is not None:
            in_specs.append(pl.BlockSpec(memory_space=pl.ANY))
            args.append(buf)
            aliases = {2: 0}
    if rider is None:
        return pl.pallas_call(
            body, name="mm_tn",
            grid=(M // bm, N // bn, nt),
            in_specs=in_specs, out_specs=out_spec, out_shape=out_shape,
            input_output_aliases=aliases,
            compiler_params=_params(3),
        )(*args)
    anywhere = pl.BlockSpec(memory_space=pl.ANY)
    return pl.pallas_call(
        body, name="mm_tn_rider",
        grid=(M // bm, N // bn, nt),
        in_specs=in_specs + [anywhere], out_specs=[out_spec, anywhere], out_shape=[out_shape, _rider_out(rider)],
        scratch_shapes=_rider_sems(rider),
        input_output_aliases=aliases,
        compiler_params=_params(3),
    )(*args, rider[1])


def _mm(a, w, tm, tn, out_dtype, first_block_scale=1.0):
    S, K = a.shape
    N = w.shape[1]

    def body(a_ref, w_ref, o_ref):
        r = _dot(a_ref[...], w_ref[...])
        if first_block_scale != 1.0:
            r = r * jnp.where(pl.program_id(1) == 0, first_block_scale, 1.0)
        o_ref[...] = r.astype(out_dtype)

    return pl.pallas_call(
        body, name="mm",
        grid=(S // tm, N // tn),
        in_specs=[pl.BlockSpec((tm, K), lambda i, j: (i, 0)),
                  pl.BlockSpec((K, tn), lambda i, j: (0, j))],
        out_specs=pl.BlockSpec((tm, tn), lambda i, j: (i, j)),
        out_shape=jax.ShapeDtypeStruct((S, N), out_dtype),
        compiler_params=_params(2),
    )(a, w)


def _norm_fwd(x, g, tm):
    S = x.shape[0]

    def body(x_ref, g_ref, h_ref):
        h_ref[...] = _rms(x_ref[...], g_ref[...]).astype(BF)

    return pl.pallas_call(
        body, name="norm_fwd",
        grid=(S // tm,),
        in_specs=[pl.BlockSpec((tm, D), lambda i: (i, 0)), pl.BlockSpec((1, D), lambda i: (0, 0))],
        out_specs=pl.BlockSpec((tm, D), lambda i: (i, 0)),
        out_shape=jax.ShapeDtypeStruct((S, D), BF),
        compiler_params=_params(1),
    )(x, g)


def _mix_in_bwd(dz, wcat, dzf, wf, x1, g, dx2, tm, rider):
    S = dz.shape[0]
    nk = 2
    kb = ZW // nk

    def body(dz_ref, w_ref, dzf_ref, wf_ref, x_ref, g_ref, dx2_ref, rsrc_ref, dx1_ref, dg_ref, rdst_ref,
             acc, send_sems, recv_sems):
        i = pl.program_id(0)
        k = pl.program_id(1)

        @pl.when((i == 0) & (k == 0))
        def _():
            _rider_start(rider[0], rsrc_ref, rdst_ref, send_sems, recv_sems)
            dg_ref[...] = jnp.zeros_like(dg_ref)

        @pl.when(k == 0)
        def _():
            acc[...] = _dot_nt(dzf_ref[...], wf_ref[...])

        acc[...] += _dot_nt(dz_ref[...], w_ref[...])

        @pl.when(k == nk - 1)
        def _():
            dx, dg = _rms_bwd(acc[...], x_ref[...], g_ref[...])
            dx1_ref[...] = dx2_ref[...] + dx
            dg_ref[...] += dg

        @pl.when((i == S // tm - 1) & (k == nk - 1))
        def _():
            _rider_finish(rider[0], rsrc_ref, rdst_ref, send_sems, recv_sems)

    row = pl.BlockSpec((tm, D), lambda i, k: (i, 0))
    vec = pl.BlockSpec((1, D), lambda i, k: (0, 0))
    anywhere = pl.BlockSpec(memory_space=pl.ANY)
    return pl.pallas_call(
        body, name="mix_in_bwd",
        grid=(S // tm, nk),
        in_specs=[pl.BlockSpec((tm, kb), lambda i, k: (i, k)),
                  pl.BlockSpec((D, kb), lambda i, k: (0, k)),
                  pl.BlockSpec((tm, HD), lambda i, k: (i, 0)),
                  pl.BlockSpec((D, HD), lambda i, k: (0, 0)),
                  row, vec, row, anywhere],
        out_specs=[row, vec, anywhere],
        out_shape=[jax.ShapeDtypeStruct((S, D), F32), jax.ShapeDtypeStruct((1, D), F32), _rider_out(rider)],
        scratch_shapes=[pltpu.VMEM((tm, D), F32)] + _rider_sems(rider),
        compiler_params=_params(2),
    )(dz, wcat, dzf, wf, x1, g, dx2, rider[1])


def _scan_rows(blk, reverse):
    n = blk.shape[0]
    row = lax.broadcasted_iota(jnp.int32, blk.shape, 0)
    d = 1
    while d < n:
        if reverse:
            blk = blk + jnp.where(row < n - d, pltpu.roll(blk, n - d, 0), 0.0)
        else:
            blk = blk + jnp.where(row >= d, pltpu.roll(blk, d, 0), 0.0)
        d *= 2
    return blk


def _forget_cumsum(zf, bf, cs):
    S = zf.shape[0]

    def body(zf_ref, bf_ref, c_ref, cb_ref, carry):
        @pl.when(pl.program_id(0) == 0)
        def _():
            carry[...] = jnp.zeros_like(carry)

        x = zf_ref[...] + bf_ref[...]
        logf = jnp.minimum(x, 0.0) - jnp.log1p(jnp.exp(-jnp.abs(x)))
        sc = _scan_rows(logf, False) + carry[...]
        carry[...] = sc[cs - 1:cs, :]
        sc = sc * LOG2E
        c_ref[...] = sc
        for h in range(H):
            cb_ref[h] = jnp.broadcast_to(sc[:, h:h + 1], (cs, HD))

    return pl.pallas_call(
        body, name="forget_cumsum",
        grid=(S // cs,),
        in_specs=[pl.BlockSpec((cs, HD), lambda i: (i, 0)), pl.BlockSpec((1, HD), lambda i: (0, 0))],
        out_specs=[pl.BlockSpec((cs, HD), lambda i: (i, 0)),
                   pl.BlockSpec((H, cs, HD), lambda i: (0, i, 0))],
        out_shape=[jax.ShapeDtypeStruct((S, HD), F32), jax.ShapeDtypeStruct((H, S, HD), F32)],
        scratch_shapes=[pltpu.VMEM((1, HD), F32)],
        compiler_params=_params(1),
    )(zf, bf)


def _forget_bwd(dc, zf, bf, cs):
    S = dc.shape[0]
    nc = S // cs

    def body(dc_ref, zf_ref, bf_ref, dzf_ref, dbf_ref, carry):
        @pl.when(pl.program_id(0) == 0)
        def _():
            carry[...] = jnp.zeros_like(carry)
            dbf_ref[...] = jnp.zeros_like(dbf_ref)

        sc = _scan_rows(dc_ref[...], True) + carry[...]
        carry[...] = sc[0:1, :]
        x = zf_ref[...] + bf_ref[...]
        dz = sc * _sigmoid(-x)
        dzf_ref[...] = dz.astype(BF)
        dbf_ref[...] += jnp.sum(dz, axis=0, keepdims=True)

    rev = pl.BlockSpec((cs, HD), lambda i: (nc - 1 - i, 0))
    vec = pl.BlockSpec((1, HD), lambda i: (0, 0))
    return pl.pallas_call(
        body, name="forget_bwd",
        grid=(nc,),
        in_specs=[rev, rev, vec],
        out_specs=[rev, vec],
        out_shape=[jax.ShapeDtypeStruct((S, HD), BF), jax.ShapeDtypeStruct((1, HD), F32)],
        scratch_shapes=[pltpu.VMEM((1, HD), F32)],
        compiler_params=_params(1),
    )(dc, zf, bf)


def _lanes(x, n):
    return x if n == HD else jnp.concatenate([x] * (n // HD), axis=1)


def _causal_mask(i, j, t, rows_are_queries):
    r = lax.broadcasted_iota(jnp.int32, (t, t), 0)
    c = lax.broadcasted_iota(jnp.int32, (t, t), 1)
    if rows_are_queries:
        return (j * t + c) <= (i * t + r)
    return (j * t + r) <= (i * t + c)


def _fox_fwd(z, ccol_b, crow, t):
    S = z.shape[0]
    nq = S // t

    def body(q_ref, kv_ref, cc_ref, cr_ref, o_ref, lse_ref, m_s, acc_s, s_a, s_b):
        i = pl.program_id(1)
        ct = cc_ref[...]
        ones = jnp.ones((t, HD), BF)
        m_s[...] = jnp.full_like(m_s, NEG)
        acc_s[...] = jnp.zeros_like(acc_s)

        def scores(j, s_ref):
            off = pl.multiple_of(j * t, t)
            s_ref[...] = _dot_nt(q_ref[...], kv_ref[pl.ds(off, t), :HD]) - cr_ref[pl.ds(j, 1), :]

        def consume(j, s_ref, masked):
            off = pl.multiple_of(j * t, t)
            v1 = jnp.concatenate([kv_ref[pl.ds(off, t), HD:], ones], axis=1)
            s = s_ref[...]
            if masked:
                s = jnp.where(_causal_mask(i, j, t, True), s, NEG)
            m_old = m_s[...]
            m_new = jnp.maximum(m_old, jnp.max(s, axis=1, keepdims=True))
            p = jnp.exp2(s - _lanes(m_new, t))
            alpha = jnp.exp2(m_old - m_new)
            acc_s[...] = _lanes(alpha, 2 * HD) * acc_s[...] + _dot(p.astype(BF), v1)
            m_s[...] = m_new

        scores(0, s_a)

        def pair(jj, carry):
            j = 2 * jj
            scores(j + 1, s_b)
            consume(j, s_a, False)
            scores(j + 2, s_a)
            consume(j + 1, s_b, False)
            return carry

        lax.fori_loop(0, i // 2, pair, 0)

        @pl.when(i % 2 == 0)
        def _():
            consume(i, s_a, True)

        @pl.when(i % 2 == 1)
        def _():
            scores(i, s_b)
            consume(i - 1, s_a, False)
            consume(i, s_b, True)

        l = acc_s[:, HD:]
        o_ref[...] = (acc_s[:, :HD] / l).astype(BF)
        lse_ref[...] = m_s[...] + ct + jnp.log2(l)

    return pl.pallas_call(
        body, name="fox_fwd",
        grid=(H, nq),
        in_specs=[pl.BlockSpec((t, HD), lambda h, i: (i, h)),
                  pl.BlockSpec((S, 2 * HD), lambda h, i: (0, 4 + h)),
                  pl.BlockSpec((None, t, HD), lambda h, i: (h, i, 0)),
                  pl.BlockSpec((None, nq, t), lambda h, i: (h, 0, 0))],
        out_specs=[pl.BlockSpec((t, HD), lambda h, i: (i, h)),
                   pl.BlockSpec((None, t, HD), lambda h, i: (h, i, 0))],
        out_shape=[jax.ShapeDtypeStruct((S, D), BF), jax.ShapeDtypeStruct((H, S, HD), F32)],
        scratch_shapes=[pltpu.VMEM((t, HD), F32), pltpu.VMEM((t, 2 * HD), F32),
                        pltpu.VMEM((t, t), F32), pltpu.VMEM((t, t), F32)],
        compiler_params=_params(2),
    )(z, z, ccol_b, crow)


def _fox_bwd(z, do, o, lse_b, ccol_b, crow, dz, t, rider):
    S = z.shape[0]
    nq = S // t

    def body(q_ref, kv_ref, do_ref, o_ref, lse_ref, cc_ref, cr_ref, dz_in, rsrc_ref,
             dq_ref, dkt_ref, dvt_ref, dck_ref, dcq_ref, rdst_ref, acc_s, r_s, send_sems, recv_sems):
        del dz_in
        i = pl.program_id(1)

        @pl.when((pl.program_id(0) == 0) & (i == 0))
        def _():
            _rider_start(rider[0], rsrc_ref, rdst_ref, send_sems, recv_sems)

        @pl.when(i == 0)
        def _():
            dkt_ref[...] = jnp.zeros_like(dkt_ref)
            dvt_ref[...] = jnp.zeros_like(dvt_ref)
            dck_ref[...] = jnp.zeros_like(dck_ref)

        q = q_ref[...]
        dout = do_ref[...]
        qt = jnp.transpose(q.astype(F32)).astype(BF)
        dot_ = jnp.transpose(dout.astype(F32)).astype(BF)
        off_t = _lanes(lse_ref[...] - cc_ref[...], t)
        delta = jnp.sum(dout.astype(F32) * o_ref[...].astype(F32), axis=1, keepdims=True)
        delta = _lanes(jnp.broadcast_to(delta, (t, HD)), t)
        acc_s[...] = jnp.zeros_like(acc_s)
        r_s[...] = jnp.zeros_like(r_s)

        def step(j, masked):
            off = pl.multiple_of(j * t, t)
            k = kv_ref[pl.ds(off, t), :HD]
            v = kv_ref[pl.ds(off, t), HD:]
            p = jnp.exp2(_dot_nt(q, k) - cr_ref[pl.ds(j, 1), :] - off_t)
            if masked:
                p = jnp.where(_causal_mask(i, j, t, True), p, 0.0)
            ds = p * (_dot_nt(dout, v) - delta)
            dsb = ds.astype(BF)
            acc_s[...] += _dot(dsb, k)
            dkt_ref[j] += _dot(qt, dsb)
            dvt_ref[j] += _dot(dot_, p.astype(BF))
            dck_ref[pl.ds(j, 1), :] += jnp.sum(ds, axis=0, keepdims=True)
            r_s[...] += jnp.sum(ds, axis=1, keepdims=True)

        def full_step(j, carry):
            step(j, False)
            return carry

        lax.fori_loop(0, i, full_step, 0)
        step(i, True)
        dq_ref[...] = (acc_s[...] * SCALE).astype(BF)
        dcq_ref[...] = jnp.transpose(r_s[...])[0:1, :]

        @pl.when((pl.program_id(0) == H - 1) & (i == nq - 1))
        def _():
            _rider_finish(rider[0], rsrc_ref, rdst_ref, send_sems, recv_sems)

    qspec = pl.BlockSpec((t, HD), lambda h, i: (i, h))
    bspec = pl.BlockSpec((None, t, HD), lambda h, i: (h, i, 0))
    rows = pl.BlockSpec((None, nq, t), lambda h, i: (h, 0, 0))
    tspec = pl.BlockSpec((None, nq, HD, t), lambda h, i: (h, 0, 0, 0))
    tshape = jax.ShapeDtypeStruct((H, nq, HD, t), F32)
    anywhere = pl.BlockSpec(memory_space=pl.ANY)
    return pl.pallas_call(
        body, name="fox_bwd",
        grid=(H, nq),
        in_specs=[qspec,
                  pl.BlockSpec((S, 2 * HD), lambda h, i: (0, 4 + h)),
                  qspec, qspec, bspec, bspec, rows, anywhere, anywhere],
        out_specs=[qspec, tspec, tspec, rows, pl.BlockSpec((None, None, 1, t), lambda h, i: (h, i, 0, 0)),
                   anywhere],
        out_shape=[jax.ShapeDtypeStruct((S, ZW), BF), tshape, tshape,
                   jax.ShapeDtypeStruct((H, nq, t), F32), jax.ShapeDtypeStruct((H, nq, 1, t), F32),
                   _rider_out(rider)],
        scratch_shapes=[pltpu.VMEM((t, HD), F32), pltpu.VMEM((t, HD), F32)] + _rider_sems(rider),
        input_output_aliases={7: 0},
        compiler_params=_params(2),
    )(z, z, do, o, lse_b, ccol_b, crow, dz, rider[1])


def _fox_bwd_finish(dkt, dvt, dz, t):
    nq = dkt.shape[1]
    S = nq * t

    def body(dkt_ref, dvt_ref, dz_in, dkv_ref):
        del dz_in
        dkv_ref[:, :HD] = (jnp.transpose(dkt_ref[...]) * LN2).astype(BF)
        dkv_ref[:, HD:] = jnp.transpose(dvt_ref[...]).astype(BF)

    tspec = pl.BlockSpec((None, None, HD, t), lambda h, j: (h, j, 0, 0))
    return pl.pallas_call(
        body, name="fox_bwd_finish",
        grid=(H, nq),
        in_specs=[tspec, tspec, pl.BlockSpec(memory_space=pl.ANY)],
        out_specs=pl.BlockSpec((t, 2 * HD), lambda h, j: (j, 4 + h)),
        out_shape=jax.ShapeDtypeStruct((S, ZW), BF),
        input_output_aliases={2: 0},
        compiler_params=_params(2),
    )(dkt, dvt, dz)


def _sgu_mask(transposed):
    r = lax.broadcasted_iota(jnp.int32, (L, L), 0)
    c = lax.broadcasted_iota(jnp.int32, (L, L), 1)
    if transposed:
        return (r // CHUNK) <= (c // CHUNK)
    return (c // CHUNK) <= (r // CHUNK)


def _ln_group(vs, lng, lnb):
    mu = jnp.mean(vs, axis=-1, keepdims=True)
    xc = vs - mu
    rstd = lax.rsqrt(jnp.mean(xc * xc, axis=-1, keepdims=True) + LN_EPS)
    xhat = xc * rstd
    return xhat, rstd, xhat * lng + lnb


def _mix_out_fwd(z, oa, ln_g, ln_b, ws, bst, wout, x1, g_post, tm):
    S = z.shape[0]
    nw = tm // L

    def body(u_ref, sv_ref, ga_ref, gb_ref, oa_ref, lng_ref, lnb_ref, ws_ref, bst_ref, wout_ref, x1_ref, gp_ref,
             mg_ref, y_ref, x2_ref, mg_s):
        mask = _sgu_mask(False)
        for g in range(G):
            cols = slice(g * L, (g + 1) * L)
            wm = jnp.where(mask, ws_ref[g], 0.0).astype(BF)
            bcol = bst_ref[:, g:g + 1]
            lng = lng_ref[:, cols]
            lnb = lnb_ref[:, cols]
            for w in range(nw):
                rows = slice(w * L, (w + 1) * L)
                vs = _gelu(sv_ref[rows, cols].astype(F32))
                _, _, vn = _ln_group(vs, lng, lnb)
                mixed = _dot(wm, vn.astype(BF)) + bcol
                ob = _gelu(u_ref[rows, cols].astype(F32)) * mixed
                mg = (_sigmoid(ga_ref[rows, cols].astype(F32)) * oa_ref[rows, cols].astype(F32)
                      + _sigmoid(gb_ref[rows, cols].astype(F32)) * ob)
                mg_s[rows, cols] = mg.astype(BF)
        mgb = mg_s[...]
        mg_ref[...] = mgb
        y = _dot(mgb, wout_ref[...])
        y_ref[...] = y
        x2_ref[...] = x1_ref[...] + _rms(y, gp_ref[...])

    row = pl.BlockSpec((tm, D), lambda i: (i, 0))
    vec = pl.BlockSpec((1, D), lambda i: (0, 0))

    def zcol(kb):
        return pl.BlockSpec((tm, D), lambda i: (i, kb))

    return pl.pallas_call(
        body, name="mix_out_fwd",
        grid=(S // tm,),
        in_specs=[zcol(3), zcol(4), zcol(5), zcol(6), row, vec, vec,
                  pl.BlockSpec((G, L, L), lambda i: (0, 0, 0)),
                  pl.BlockSpec((L, HD), lambda i: (0, 0)),
                  pl.BlockSpec((D, D), lambda i: (0, 0)),
                  row, vec],
        out_specs=[row, row, row],
        out_shape=[jax.ShapeDtypeStruct((S, D), BF),
                   jax.ShapeDtypeStruct((S, D), F32),
                   jax.ShapeDtypeStruct((S, D), F32)],
        scratch_shapes=[pltpu.VMEM((tm, D), BF)],
        compiler_params=_params(1),
    )(z, z, z, z, oa, ln_g, ln_b, ws, bst, wout, x1, g_post)


def _mix_out_bwd(dx2, y2, g_post, wout, z, oa, ln_g, ln_b, ws, wst, bst, tm, rider):
    S = z.shape[0]
    nw = tm // L

    def body(dx2_ref, y_ref, gp_ref, wout_ref, u_ref, sv_ref, ga_ref, gb_ref, oa_ref, lng_ref, lnb_ref,
             ws_ref, wst_ref, bst_ref, q_ref,
             dz_ref, dy_ref, doa_ref, dgp_ref, dlng_ref, dlnb_ref, dws_ref, dbst_ref, t_ref,
             dzg_s, dm_s, send_sems, recv_sems):
        i = pl.program_id(0)
        c = pl.program_id(1)

        @pl.when((i == 0) & (c == 0))
        def _():
            _rider_start(rider[0], q_ref, t_ref, send_sems, recv_sems)
            dgp_ref[...] = jnp.zeros_like(dgp_ref)
            dlng_ref[...] = jnp.zeros_like(dlng_ref)
            dlnb_ref[...] = jnp.zeros_like(dlnb_ref)
            dws_ref[...] = jnp.zeros_like(dws_ref)
            dbst_ref[...] = jnp.zeros_like(dbst_ref)

        @pl.when(c == 0)
        def _():
            dy, dg = _rms_bwd(dx2_ref[...], y_ref[...], gp_ref[...])
            dyb = dy.astype(BF)
            dy_ref[...] = dyb
            dgp_ref[...] += dg
            dm_s[...] = _dot_nt(dyb, wout_ref[...])
            mask = _sgu_mask(False)
            mask_t = _sgu_mask(True)
            lane = lax.broadcasted_iota(jnp.int32, (L, HD), 1)
            for g in range(G):
                cols = slice(g * L, (g + 1) * L)
                wm = jnp.where(mask, ws_ref[g], 0.0).astype(BF)
                wmt = jnp.where(mask_t, wst_ref[g], 0.0).astype(BF)
                bcol = bst_ref[:, g:g + 1]
                lng = lng_ref[:, cols]
                lnb = lnb_ref[:, cols]
                dws_g = jnp.zeros((L, L), F32)
                dbs_g = jnp.zeros((L, 1), F32)
                dlng_g = jnp.zeros((1, L), F32)
                dlnb_g = jnp.zeros((1, L), F32)
                for w in range(nw):
                    rows = slice(w * L, (w + 1) * L)
                    dm = dm_s[rows, cols]
                    vs, dvs_dz = _gelu_parts(sv_ref[rows, cols].astype(F32))
                    xhat, rstd, vn = _ln_group(vs, lng, lnb)
                    vnb = vn.astype(BF)
                    mixed = _dot(wm, vnb) + bcol
                    u, du_dz = _gelu_parts(u_ref[rows, cols].astype(F32))
                    sga = _sigmoid(ga_ref[rows, cols].astype(F32))
                    sgb = _sigmoid(gb_ref[rows, cols].astype(F32))
                    oav = oa_ref[rows, cols].astype(F32)
                    ob = u * mixed
                    doa_ref[rows, cols] = (dm * sga).astype(BF)
                    dzg_s[2, rows, cols] = (dm * oav * sga * (1.0 - sga)).astype(BF)
                    dzg_s[3, rows, cols] = (dm * ob * sgb * (1.0 - sgb)).astype(BF)
                    dob = dm * sgb
                    dzg_s[0, rows, cols] = (dob * mixed * du_dz).astype(BF)
                    dmixed = dob * u
                    dmb = dmixed.astype(BF)
                    dbs_g += jnp.sum(dmixed, axis=1, keepdims=True)
                    dws_g += _dot_nt(dmb, vnb)
                    dvn = _dot(wmt, dmb)
                    dlng_g += jnp.sum(dvn * xhat, axis=0, keepdims=True)
                    dlnb_g += jnp.sum(dvn, axis=0, keepdims=True)
                    dxh = dvn * lng
                    dvs = rstd * (dxh - jnp.mean(dxh, axis=-1, keepdims=True)
                                  - xhat * jnp.mean(dxh * xhat, axis=-1, keepdims=True))
                    dzg_s[1, rows, cols] = (dvs * dvs_dz).astype(BF)
                dws_ref[g] += jnp.where(mask, dws_g, 0.0)
                dbst_ref[...] += jnp.where(lane == g, dbs_g, 0.0)
                dlng_ref[:, cols] += dlng_g
                dlnb_ref[:, cols] += dlnb_g

        dz_ref[...] = dzg_s[c]

        @pl.when((i == S // tm - 1) & (c == 3))
        def _():
            _rider_finish(rider[0], q_ref, t_ref, send_sems, recv_sems)

    row = pl.BlockSpec((tm, D), lambda i, c: (i, 0))
    vec = pl.BlockSpec((1, D), lambda i, c: (0, 0))
    wsspec = pl.BlockSpec((G, L, L), lambda i, c: (0, 0, 0))
    bspec = pl.BlockSpec((L, HD), lambda i, c: (0, 0))
    anywhere = pl.BlockSpec(memory_space=pl.ANY)

    def zcol(kb):
        return pl.BlockSpec((tm, D), lambda i, c: (i, kb))

    return pl.pallas_call(
        body, name="mix_out_bwd",
        grid=(S // tm, 4),
        in_specs=[row, row, vec, pl.BlockSpec((D, D), lambda i, c: (0, 0)),
                  zcol(3), zcol(4), zcol(5), zcol(6), row, vec, vec, wsspec, wsspec, bspec, anywhere],
        out_specs=[pl.BlockSpec((tm, D), lambda i, c: (i, 3 + c)),
                   row, row, vec, vec, vec, wsspec, bspec, anywhere],
        out_shape=[jax.ShapeDtypeStruct((S, ZW), BF),
                   jax.ShapeDtypeStruct((S, D), BF),
                   jax.ShapeDtypeStruct((S, D), BF),
                   jax.ShapeDtypeStruct((1, D), F32),
                   jax.ShapeDtypeStruct((1, D), F32),
                   jax.ShapeDtypeStruct((1, D), F32),
                   jax.ShapeDtypeStruct((G, L, L), F32),
                   jax.ShapeDtypeStruct((L, HD), F32),
                   _rider_out(rider)],
        scratch_shapes=[pltpu.VMEM((4, tm, D), BF), pltpu.VMEM((tm, D), F32)] + _rider_sems(rider),
        compiler_params=_params(2),
    )(dx2, y2, g_post, wout, z, z, z, z, oa, ln_g, ln_b, ws, wst, bst, rider[1])


def _loss_head(x3, target, tm):
    S = x3.shape[0]

    def body(x_ref, t_ref, dx_ref, loss_ref):
        @pl.when(pl.program_id(0) == 0)
        def _():
            loss_ref[...] = jnp.zeros_like(loss_ref)

        e = x_ref[...] - t_ref[...]
        dx_ref[...] = e * (1.0 / D)
        loss_ref[...] += jnp.sum(e * e) * (0.5 / D)

    row = pl.BlockSpec((tm, D), lambda i: (i, 0))
    return pl.pallas_call(
        body, name="loss_head",
        grid=(S // tm,),
        in_specs=[row, row],
        out_specs=[row, pl.BlockSpec((8, HD), lambda i: (0, 0))],
        out_shape=[jax.ShapeDtypeStruct((S, D), F32), jax.ShapeDtypeStruct((8, HD), F32)],
        compiler_params=_params(1),
    )(x3, target)


def _adamw(w, g, m, v, tr):
    R, C = w.shape

    def body(w_ref, g_ref, m_ref, v_ref, d_ref, nm_ref, nv_ref):
        gv = g_ref[...]
        m_new = ADAM_B1 * m_ref[...] + (1.0 - ADAM_B1) * gv
        v_new = ADAM_B2 * v_ref[...] + (1.0 - ADAM_B2) * (gv * gv)
        m_hat = m_new / (1.0 - ADAM_B1 ** ADAM_STEP)
        v_hat = v_new / (1.0 - ADAM_B2 ** ADAM_STEP)
        d_ref[...] = -ADAM_LR * (m_hat / (jnp.sqrt(v_hat) + ADAM_EPS) + ADAM_WD * w_ref[...])
        nm_ref[...] = m_new
        nv_ref[...] = v_new

    spec = pl.BlockSpec((tr, C), lambda i: (i, 0))
    shp = jax.ShapeDtypeStruct((R, C), F32)
    return pl.pallas_call(
        body, name="adamw",
        grid=(R // tr,),
        in_specs=[spec] * 4, out_specs=[spec] * 3, out_shape=[shp] * 3,
        compiler_params=_params(1),
    )(w, g, m, v)


def _mesh_pos():
    return lax.axis_index("x"), lax.axis_index("y"), lax.axis_index("c")


def _half(c, rows):
    return pl.ds(pl.multiple_of(c * rows, 16), rows)


def _other_chips(x, y):
    return [(1 - x, y), (x, 1 - y), (1 - x, 1 - y)]


def _remote(k, src, dst, to, send_sems, recv_sems):
    return pltpu.make_async_remote_copy(src_ref=src, dst_ref=dst, send_sem=send_sems.at[k],
                                        recv_sem=recv_sems.at[k], device_id=to, device_id_type=MESH)


def _gather_start(wp_ref, g_ref, send_sems, recv_sems):
    x, y, c = _mesh_pos()
    mine = _half(c, wp_ref.shape[0] // 2)
    for k, (px, py) in enumerate(_other_chips(x, y)):
        _remote(k, wp_ref.at[mine], g_ref.at[2 * x + y, mine], (px, py, c), send_sems, recv_sems).start()


def _gather_finish(wp_ref, g_ref, send_sems, recv_sems):
    x, y, c = _mesh_pos()
    sibling = (x, y, 1 - c)
    rows = wp_ref.shape[0] // 2
    mine, other = _half(c, rows), _half(1 - c, rows)
    chips = _other_chips(x, y)
    for k, (px, py) in enumerate(chips):
        land = g_ref.at[2 * px + py, mine]
        _remote(k, land, land, (px, py, c), send_sems, recv_sems).wait_recv()
        _remote(3 + k, land, land, sibling, send_sems, recv_sems).start()
    for k, (px, py) in enumerate(chips):
        land = g_ref.at[2 * px + py, other]
        _remote(3 + k, land, land, sibling, send_sems, recv_sems).wait_recv()
    for k, (px, py) in enumerate(chips):
        land = g_ref.at[2 * px + py, mine]
        _remote(k, wp_ref.at[mine], g_ref.at[2 * x + y, mine], (px, py, c), send_sems, recv_sems).wait_send()
        _remote(3 + k, land, land, sibling, send_sems, recv_sems).wait_send()


def _place_own_shard(g, wp):
    x, y, _ = _mesh_pos()
    return lax.dynamic_update_index_in_dim(g, wp, 2 * x + y, 0)


def _all_gather_weights(wp):
    def body(wp_ref, g_ref, send_sems, recv_sems):
        _gather_start(wp_ref, g_ref, send_sems, recv_sems)
        _gather_finish(wp_ref, g_ref, send_sems, recv_sems)

    g = pl.pallas_call(
        body, name="all_gather_weights",
        in_specs=[pl.BlockSpec(memory_space=pl.ANY)],
        out_specs=pl.BlockSpec(memory_space=pl.ANY),
        out_shape=jax.ShapeDtypeStruct((NSH,) + wp.shape, wp.dtype),
        scratch_shapes=[pltpu.SemaphoreType.DMA((6,)), pltpu.SemaphoreType.DMA((6,))],
        compiler_params=pltpu.CompilerParams(has_side_effects=True),
    )(wp)
    return _place_own_shard(g, wp)


def _scatter_copies(q_ref, t_ref, send_sems, recv_sems):
    x, y, c = _mesh_pos()
    return [_remote(k, q_ref.at[2 * px + py], t_ref.at[k], (px, py, c), send_sems, recv_sems)
            for k, (px, py) in enumerate(_other_chips(x, y))]


def _scatter_start(q_ref, t_ref, send_sems, recv_sems):
    for cp in _scatter_copies(q_ref, t_ref, send_sems, recv_sems):
        cp.start()


def _scatter_finish(q_ref, t_ref, send_sems, recv_sems):
    for cp in _scatter_copies(q_ref, t_ref, send_sems, recv_sems):
        cp.wait()


_FLIPS = [(fx, fy, fc) for fx in (0, 1) for fy in (0, 1) for fc in (0, 1)][1:]


def _rider_copies(kind, src_ref, dst_ref, send_sems, recv_sems):
    if kind == "scatter":
        return _scatter_copies(src_ref, dst_ref, send_sems, recv_sems)
    x, y, c = _mesh_pos()
    if kind == "broadcast":
        return [_remote(k, src_ref, dst_ref.at[4 * x + 2 * y + c], (x ^ fx, y ^ fy, c ^ fc), send_sems, recv_sems)
                for k, (fx, fy, fc) in enumerate(_FLIPS)]
    rows = src_ref.shape[1] // 2
    return [_remote(0, src_ref.at[:, _half(1 - c, rows)], dst_ref, (x, y, 1 - c), send_sems, recv_sems)]


def _place_own_block(sm, block):
    x, y, c = _mesh_pos()
    return lax.dynamic_update_index_in_dim(sm, block, 4 * x + 2 * y + c, 0)


def _rider_start(kind, src_ref, dst_ref, send_sems, recv_sems):
    for cp in _rider_copies(kind, src_ref, dst_ref, send_sems, recv_sems):
        cp.start()


def _rider_finish(kind, src_ref, dst_ref, send_sems, recv_sems):
    for cp in _rider_copies(kind, src_ref, dst_ref, send_sems, recv_sems):
        cp.wait()


def _rider_out(rider):
    kind, a = rider
    if kind == "scatter":
        return jax.ShapeDtypeStruct((3,) + a.shape[1:], a.dtype)
    if kind == "broadcast":
        return jax.ShapeDtypeStruct((NDEV,) + a.shape, a.dtype)
    return jax.ShapeDtypeStruct((a.shape[0], a.shape[1] // 2) + a.shape[2:], a.dtype)


def _rider_sems(rider):
    n = {"scatter": 3, "broadcast": 7, "exchange": 1}[rider[0]]
    return [pltpu.SemaphoreType.DMA((n,)), pltpu.SemaphoreType.DMA((n,))]


def _pair_exchange(p):
    rows = p.shape[1] // 2

    def body(p_ref, r_ref, send_sem, recv_sem):
        x, y, c = _mesh_pos()
        cp = pltpu.make_async_remote_copy(src_ref=p_ref.at[:, _half(1 - c, rows)], dst_ref=r_ref, send_sem=send_sem,
                                          recv_sem=recv_sem, device_id=(x, y, 1 - c), device_id_type=MESH)
        cp.start()
        cp.wait()

    return pl.pallas_call(
        body, name="pair_exchange",
        in_specs=[pl.BlockSpec(memory_space=pl.ANY)],
        out_specs=pl.BlockSpec(memory_space=pl.ANY),
        out_shape=jax.ShapeDtypeStruct((NSH, rows, D), F32),
        scratch_shapes=[pltpu.SemaphoreType.DMA, pltpu.SemaphoreType.DMA],
        compiler_params=pltpu.CompilerParams(has_side_effects=True),
    )(p)


def _pair_add(p, r, nb):
    rows = r.shape[1]
    tr = rows // nb

    def body(p_ref, r_ref, q_ref):
        q_ref[...] = (p_ref[...] + r_ref[...]).astype(BF)

    return pl.pallas_call(
        body, name="pair_add", grid=(NSH, nb),
        in_specs=[pl.BlockSpec((None, tr, D), lambda j, i: (j, lax.axis_index("c") * nb + i, 0)),
                  pl.BlockSpec((None, tr, D), lambda j, i: (j, i, 0))],
        out_specs=pl.BlockSpec((None, tr, D), lambda j, i: (j, i, 0)),
        out_shape=jax.ShapeDtypeStruct((NSH, rows, D), BF),
        compiler_params=_params(2),
    )(p, r)


def _chip_exchange(q, small):
    riders = (("scatter", q), ("broadcast", small))

    def body(q_ref, s_ref, t_ref, sm_ref, send_sems, recv_sems, ssend_sems, srecv_sems):
        _rider_start("scatter", q_ref, t_ref, send_sems, recv_sems)
        _rider_start("broadcast", s_ref, sm_ref, ssend_sems, srecv_sems)
        _rider_finish("scatter", q_ref, t_ref, send_sems, recv_sems)
        _rider_finish("broadcast", s_ref, sm_ref, ssend_sems, srecv_sems)

    anywhere = pl.BlockSpec(memory_space=pl.ANY)
    t, sm = pl.pallas_call(
        body, name="chip_exchange",
        in_specs=[anywhere, anywhere], out_specs=[anywhere, anywhere],
        out_shape=[_rider_out(r) for r in riders],
        scratch_shapes=_rider_sems(riders[0]) + _rider_sems(riders[1]),
        compiler_params=pltpu.CompilerParams(has_side_effects=True),
    )(q, small)
    return t, _place_own_block(sm, small)


def _shard_sum(p, r, t, nb):
    rows = r.shape[1]
    tr = rows // nb

    def shard():
        return 2 * lax.axis_index("x") + lax.axis_index("y")

    def body(p_ref, r_ref, t_ref, o_ref):
        s = p_ref[...] + r_ref[...]
        for k in range(3):
            s = s + t_ref[k].astype(F32)
        o_ref[...] = s

    return pl.pallas_call(
        body, name="shard_sum", grid=(nb,),
        in_specs=[pl.BlockSpec((None, tr, D), lambda i: (shard(), lax.axis_index("c") * nb + i, 0)),
                  pl.BlockSpec((None, tr, D), lambda i: (shard(), i, 0)),
                  pl.BlockSpec((3, tr, D), lambda i: (0, i, 0))],
        out_specs=pl.BlockSpec((tr, D), lambda i: (i, 0)),
        out_shape=jax.ShapeDtypeStruct((rows, D), F32),
        compiler_params=_params(1),
    )(p, r, t)


def _small_sum(sm):
    def body(sm_ref, o_ref):
        s = sm_ref[0]
        for k in range(1, NDEV):
            s = s + sm_ref[k]
        o_ref[...] = s

    return pl.pallas_call(
        body, name="small_sum",
        in_specs=[pl.BlockSpec(memory_space=pltpu.VMEM)],
        out_specs=pl.BlockSpec(memory_space=pltpu.VMEM),
        out_shape=jax.ShapeDtypeStruct(sm.shape[1:], F32),
    )(sm)


def _pair_gather(halves):
    n = len(halves)

    def body(*refs):
        gh_refs, o_refs, (send_sems, recv_sems) = refs[:n], refs[n:2 * n], refs[2 * n:]
        x, y, c = _mesh_pos()
        sibling = (x, y, 1 - c)
        for g in range(n):
            rows = gh_refs[g].shape[0]
            _remote(g, gh_refs[g], o_refs[g].at[_half(c, rows)], sibling, send_sems, recv_sems).start()
        for g in range(n):
            rows = gh_refs[g].shape[0]
            _remote(g, gh_refs[g], o_refs[g].at[_half(c, rows)], sibling, send_sems, recv_sems).wait_send()
            _remote(g, gh_refs[g], o_refs[g].at[_half(1 - c, rows)], sibling, send_sems, recv_sems).wait_recv()

    anywhere = pl.BlockSpec(memory_space=pl.ANY)
    outs = pl.pallas_call(
        body, name="pair_gather",
        in_specs=[anywhere] * n, out_specs=[anywhere] * n,
        out_shape=[jax.ShapeDtypeStruct((2 * h.shape[0], D), F32) for h in halves],
        scratch_shapes=[pltpu.SemaphoreType.DMA((n,)), pltpu.SemaphoreType.DMA((n,))],
        compiler_params=pltpu.CompilerParams(has_side_effects=True),
    )(*halves)
    c = lax.axis_index("c")
    return [lax.dynamic_update_slice_in_dim(o, h, c * h.shape[0], 0) for o, h in zip(outs, halves)]


def _pad_cols(a, n):
    return jnp.pad(a, ((0, 0), (0, n - a.shape[1])))


def _split_w_in(w_in_full):
    q, k, v = w_in_full[:, :D], w_in_full[:, D:2 * D], w_in_full[:, 2 * D:3 * D]
    f = w_in_full[:, 3 * D:3 * D + H]
    gates = w_in_full[:, 3 * D + H:]
    kv = jnp.stack([k.reshape(D, H, HD), v.reshape(D, H, HD)], axis=2).reshape(D, 2 * D)
    return jnp.concatenate([q, kv, gates], axis=1), _pad_cols(f, HD)


def _merge_w_in_grad(dwcat, dwf):
    kv = dwcat[:, D:3 * D].reshape(D, H, 2, HD)
    return jnp.concatenate([dwcat[:, :D], kv[:, :, 0].reshape(D, D), kv[:, :, 1].reshape(D, D),
                            dwf[:, :H], dwcat[:, 3 * D:]], axis=1)


def _ffn_weight_grads(h, da, db, act, dy, bt):
    g = _mm_tn(h, da, D, D, bt, into=(None, FFN_ROWS, 0, "cols"))
    g = _mm_tn(h, db, D, D, bt, into=(g, FFN_ROWS, 1, "cols"))
    return _mm_tn(act, dy, D, D, bt, into=(g, FFN_ROWS, 2, "rows"))


def _train_step(x, target, wp1, wp2, small, tm, t_attn):
    S = x.shape[0]
    g1pre, g1post = small["ffn1_pre_g"], small["ffn1_post_g"]
    gmpre, gmpost = small["mix_pre_g"], small["mix_post_g"]
    g2pre, g2post = small["ffn2_pre_g"], small["ffn2_post_g"]
    ln_g, ln_b = small["sgu_ln_g"], small["sgu_ln_b"]
    ws = small["sgu_w_s"][0]
    wst = jnp.swapaxes(ws, 1, 2)
    bst = _pad_cols(small["sgu_b_s"][0].T, HD)
    bf = _pad_cols(small["b_forget"], HD)

    w1 = _all_gather_weights(wp1)
    h1, a1, b1, y1, x1, w2 = _ffn_fwd(x, g1pre, w1, g1post, tm, gather=wp2)
    wout = w2[:, FFN_ROWS:FFN_ROWS + 256, :].reshape(D, D)
    r0 = FFN_ROWS + 256
    w_in_full = jnp.concatenate(
        [blk for j in range(NSH) for blk in (w2[j, r0:r0 + D], w2[j, r0 + D:r0 + 2 * D, :WIN_SH - D])], axis=1)
    wcat, wf = _split_w_in(w_in_full)
    h2 = _norm_fwd(x1, gmpre, tm)
    z = _mm(h2, wcat, min(1024, S), D, BF, first_block_scale=SCALE * LOG2E)
    zf = _mm(h2, wf, tm, HD, F32)
    cs = min(512, S)
    c, ccol_b = _forget_cumsum(zf, bf, cs)
    crow = jnp.transpose(c[:, :H]).reshape(H, S // t_attn, t_attn)
    oa, lse_b = _fox_fwd(z, ccol_b, crow, t_attn)
    merged, y2, x2 = _mix_out_fwd(z, oa, ln_g, ln_b, ws, bst, wout, x1, gmpost, tm)
    h3, a3, b3, y3, x3 = _ffn_fwd(x2, g2pre, w2, g2post, tm)
    dx3, loss_acc = _loss_head(x3, target, tm)
    loss = loss_acc[0, 0]

    dy3, da3, db3, act3, dx2, dg2post, dg2pre = _ffn_bwd(dx3, y3, g2post, a3, b3, w2, x2, g2pre, tm)
    bt = min(2048, S)
    g_ffn2 = _ffn_weight_grads(h3, da3, db3, act3, dy3, bt)

    dz, dy2, doa, dgmpost, dlng, dlnb, dws, dbst, r_ffn2 = _mix_out_bwd(
        dx2, y2, gmpost, wout, z, oa, ln_g, ln_b, ws, wst, bst, tm, ("exchange", g_ffn2))
    q_ffn2 = _pair_add(g_ffn2, r_ffn2, 3)
    g_mix = _mm_tn(merged, dy2, 256, D, bt, into=(None, MIX_ROWS, 0, "rows"))
    dz, dkt, dvt, dc_keys, dc_queries, t_ffn2 = _fox_bwd(z, doa, oa, lse_b, ccol_b, crow, dz, t_attn,
                                                         ("scatter", q_ffn2))
    dz = _fox_bwd_finish(dkt, dvt, dz, t_attn)
    dc = _pad_cols(jnp.transpose(dc_queries.reshape(H, S) - dc_keys.reshape(H, S)), HD)
    dzf, dbf = _forget_bwd(dc, zf, bf, cs)
    dwcat = _mm_tn(h2, dz, D, D, bt)
    dwf = _mm_tn(h2, dzf, D, HD, bt)
    dwin = _merge_w_in_grad(dwcat, dwf)
    dwin_a = jnp.stack([dwin[:, j * WIN_SH:j * WIN_SH + D] for j in range(NSH)])
    dwin_b = jnp.stack([_pad_cols(dwin[:, j * WIN_SH + D:(j + 1) * WIN_SH], D) for j in range(NSH)])
    g_mix = lax.dynamic_update_slice(g_mix, jnp.concatenate([dwin_a, dwin_b], axis=1), (0, 256, 0))
    dx1, dgmpre, r_mix = _mix_in_bwd(dz, wcat, dzf, wf, x1, gmpre, dx2, tm, ("exchange", g_mix))
    q_mix = _pair_add(g_mix, r_mix, 3)

    small_early = _pack_small({
        "mix_pre_g": dgmpre, "mix_post_g": dgmpost, "ffn2_pre_g": dg2pre, "ffn2_post_g": dg2post,
        "sgu_ln_g": dlng, "sgu_ln_b": dlnb, "sgu_w_s": dws[None], "sgu_b_s": jnp.transpose(dbst[:, :G])[None],
        "b_forget": dbf[:, :H]}, _SMALL_EARLY)
    dy1, da1, db1, act1, dx, dg1post, dg1pre, t_mix, sm_early = _ffn_bwd(
        dx1, y1, g1post, a1, b1, w1, x, g1pre, tm, riders=(("scatter", q_mix), ("broadcast", small_early)))
    sm_early = _place_own_block(sm_early, small_early)
    g_gu = _mm_tn(h1, da1, D, D, bt, into=(None, 2 * D, 0, "cols"))
    g_gu = _mm_tn(h1, db1, D, D, bt, into=(g_gu, 2 * D, 1, "cols"))
    r_gu = _pair_exchange(g_gu)
    q_gu = _pair_add(g_gu, r_gu, 2)
    g_dn, t_gu = _mm_tn(act1, dy1, D, D, bt, into=(None, D, 0, "rows"), rider=("scatter", q_gu))
    r_dn = _pair_exchange(g_dn)
    q_dn = _pair_add(g_dn, r_dn, 2)
    small_late = _pack_small({"ffn1_pre_g": dg1pre, "ffn1_post_g": dg1post}, _SMALL_LATE)
    t_dn, sm_late = _chip_exchange(q_dn, small_late)

    halves = [_shard_sum(g_gu, r_gu, t_gu, 2), _shard_sum(g_dn, r_dn, t_dn, 2),
              _shard_sum(g_ffn2, r_ffn2, t_ffn2, 3), _shard_sum(g_mix, r_mix, t_mix, 3)]
    f_gu, f_dn, f_ffn2, f_mix = _pair_gather(halves)
    gsm = jnp.concatenate([_small_sum(sm_early), _small_sum(sm_late)], axis=0)
    return loss, dx, jnp.concatenate([f_gu, f_dn], axis=0), f_ffn2, f_mix, gsm


_SMALL_EARLY = ["mix_pre_g", "mix_post_g", "ffn2_pre_g", "ffn2_post_g", "sgu_ln_g", "sgu_ln_b", "sgu_b_s", "b_forget",
                "sgu_w_s"]
_SMALL_LATE = ["ffn1_pre_g", "ffn1_post_g"]
_SMALL_NAMES = _SMALL_EARLY + _SMALL_LATE


def _pack_small(d, names=None):
    rows = []
    for n in names or _SMALL_NAMES:
        a = d[n].astype(F32)
        if n == "b_forget":
            a = _pad_cols(a, D)
        a = a.reshape(-1, D)
        rows.append(jnp.pad(a, ((0, -a.shape[0] % SMALL_STRIDE), (0, 0))))
    return jnp.concatenate(rows, axis=0)


def _unpack_small(p):
    out, r = {}, 0
    for n in _SMALL_NAMES:
        if n == "sgu_w_s":
            out[n] = p[r:r + L].reshape(1, G, L, L)
            r += L
        elif n == "b_forget":
            out[n] = p[r:r + 1, :H]
            r += SMALL_STRIDE
        elif n == "sgu_b_s":
            out[n] = p[r:r + 1].reshape(1, G, L)
            r += SMALL_STRIDE
        else:
            out[n] = p[r:r + 1]
            r += SMALL_STRIDE
    return out


_BIG_NAMES = ["ffn1_w_gate", "ffn1_w_up", "ffn1_w_down", "ffn2_w_gate", "ffn2_w_up", "ffn2_w_down", "w_out", "w_in"]
_WEIGHT_ORDER = ['ffn1_pre_g', 'ffn1_w_gate', 'ffn1_w_up', 'ffn1_w_down', 'ffn1_post_g', 'mix_pre_g', 'w_in', 'b_forget',
                 'sgu_ln_g', 'sgu_ln_b', 'sgu_w_s', 'sgu_b_s', 'w_out', 'mix_post_g', 'ffn2_pre_g', 'ffn2_w_gate',
                 'ffn2_w_up', 'ffn2_w_down', 'ffn2_post_g']


def _pack_ffn(w, name):
    return jnp.concatenate([w[name + "_w_gate"][0], w[name + "_w_up"][0], w[name + "_w_down"][0]], axis=0)


def _pack_mix(w):
    w_in = w["w_in"][0]
    return jnp.concatenate([w["w_out"][0], w_in[:, :D], _pad_cols(w_in[:, D:], D)], axis=0)


def _unpack_ffn(p, name):
    return {name + "_w_gate": p[:D][None], name + "_w_up": p[D:2 * D][None], name + "_w_down": p[2 * D:][None]}


def _unpack_mix(p):
    return {"w_out": p[:256][None],
            "w_in": jnp.concatenate([p[256:256 + D], p[256 + D:, :WIN_SH - D]], axis=1)[None]}


def _step(args, tm, t_attn):
    x = args["x"][0]
    target = args["loss_target"][0]
    weights = {n: args[n] for n in _WEIGHT_ORDER}
    small = {n: weights[n] for n in _SMALL_NAMES}

    wb = {n: weights[n].astype(BF) for n in _BIG_NAMES}
    wp1 = _pack_ffn(wb, "ffn1")
    wp2 = jnp.concatenate([_pack_ffn(wb, "ffn2"), _pack_mix(wb)], axis=0)
    loss_local, dx, f_ffn1, f_ffn2, f_mix, gsm = _train_step(x, target, wp1, wp2, small, tm, t_attn)
    loss = lax.psum(loss_local, ("x", "y", "c"))
    grads = {**_unpack_ffn(f_ffn1, "ffn1"), **_unpack_ffn(f_ffn2, "ffn2"), **_unpack_mix(f_mix),
             **_unpack_small(gsm)}

    delta, new_m, new_v = {}, {}, {}
    for n in _BIG_NAMES:
        shp = weights[n].shape
        w2 = weights[n].reshape(-1, shp[-1])
        rows = w2.shape[0]
        d, nm, nv = _adamw(w2, grads[n].reshape(w2.shape), args["m_" + n].reshape(w2.shape),
                           args["v_" + n].reshape(w2.shape), rows // 4)
        delta[n], new_m[n], new_v[n] = d.reshape(shp), nm.reshape(shp), nv.reshape(shp)
    ds, nms, nvs = _adamw(_pack_small(small), gsm, _pack_small({n: args["m_" + n] for n in _SMALL_NAMES}),
                          _pack_small({n: args["v_" + n] for n in _SMALL_NAMES}), SMALL_ROWS)
    delta.update(_unpack_small(ds))
    new_m.update(_unpack_small(nms))
    new_v.update(_unpack_small(nvs))

    return (loss, dx[None], *[grads[n] for n in _WEIGHT_ORDER], *[delta[n] for n in _WEIGHT_ORDER],
            *[new_m[n] for n in _WEIGHT_ORDER], *[new_v[n] for n in _WEIGHT_ORDER])


_ARG_NAMES = (["x"] + _WEIGHT_ORDER + ["loss_target"] + ["m_" + n for n in _WEIGHT_ORDER]
              + ["v_" + n for n in _WEIGHT_ORDER])


def kernel(x, ffn1_pre_g, ffn1_w_gate, ffn1_w_up, ffn1_w_down, ffn1_post_g, mix_pre_g, w_in, b_forget, sgu_ln_g, sgu_ln_b, sgu_w_s, sgu_b_s, w_out, mix_post_g, ffn2_pre_g, ffn2_w_gate, ffn2_w_up, ffn2_w_down, ffn2_post_g, loss_target, m_ffn1_pre_g, m_ffn1_w_gate, m_ffn1_w_up, m_ffn1_w_down, m_ffn1_post_g, m_mix_pre_g, m_w_in, m_b_forget, m_sgu_ln_g, m_sgu_ln_b, m_sgu_w_s, m_sgu_b_s, m_w_out, m_mix_post_g, m_ffn2_pre_g, m_ffn2_w_gate, m_ffn2_w_up, m_ffn2_w_down, m_ffn2_post_g, v_ffn1_pre_g, v_ffn1_w_gate, v_ffn1_w_up, v_ffn1_w_down, v_ffn1_post_g, v_mix_pre_g, v_w_in, v_b_forget, v_sgu_ln_g, v_sgu_ln_b, v_sgu_w_s, v_sgu_b_s, v_w_out, v_mix_post_g, v_ffn2_pre_g, v_ffn2_w_gate, v_ffn2_w_up, v_ffn2_w_down, v_ffn2_post_g):
    args = (x, ffn1_pre_g, ffn1_w_gate, ffn1_w_up, ffn1_w_down, ffn1_post_g, mix_pre_g, w_in, b_forget, sgu_ln_g, sgu_ln_b, sgu_w_s, sgu_b_s, w_out, mix_post_g, ffn2_pre_g, ffn2_w_gate, ffn2_w_up, ffn2_w_down, ffn2_post_g, loss_target, m_ffn1_pre_g, m_ffn1_w_gate, m_ffn1_w_up, m_ffn1_w_down, m_ffn1_post_g, m_mix_pre_g, m_w_in, m_b_forget, m_sgu_ln_g, m_sgu_ln_b, m_sgu_w_s, m_sgu_b_s, m_w_out, m_mix_post_g, m_ffn2_pre_g, m_ffn2_w_gate, m_ffn2_w_up, m_ffn2_w_down, m_ffn2_post_g, v_ffn1_pre_g, v_ffn1_w_gate, v_ffn1_w_up, v_ffn1_w_down, v_ffn1_post_g, v_mix_pre_g, v_w_in, v_b_forget, v_sgu_ln_g, v_sgu_ln_b, v_sgu_w_s, v_sgu_b_s, v_w_out, v_mix_post_g, v_ffn2_pre_g, v_ffn2_w_gate, v_ffn2_w_up, v_ffn2_w_down, v_ffn2_post_g)
    named = dict(zip(_ARG_NAMES, args))
    tile = min(512, x.shape[1])
    return _step(named, tile, tile)
```

```python
import functools
import math

import jax
import jax.numpy as jnp
from jax import lax
from jax.experimental import pallas as pl
from jax.experimental.pallas import tpu as pltpu

D = 1024
F = 4096
H = 8
HD = 128
G = 8
L = 128
CHUNK = 64
NSH = 4
NDEV = 8
ZW = 7 * D
RMS_EPS = 1e-6
LN_EPS = 1e-5
NEG = -1e30
SCALE = 1.0 / math.sqrt(HD)
LOG2E = math.log2(math.e)
LN2 = math.log(2.0)

ADAM_LR = 0.001
ADAM_B1 = 0.9
ADAM_B2 = 0.999
ADAM_EPS = 1e-08
ADAM_WD = 0.01
ADAM_STEP = 10

VMEM_LIMIT_BYTES = 56 * 1024 * 1024

WIN_SH = 1794
FFN_ROWS = 3 * D
MIX_ROWS = 256 + 2 * D
G2_ROWS = FFN_ROWS + MIX_ROWS
SMALL_STRIDE = 8
SMALL_ROWS = 10 * SMALL_STRIDE + L

BF = jnp.bfloat16
F32 = jnp.float32
MESH = pl.DeviceIdType.MESH


def _params(n_grid):
    return pltpu.CompilerParams(dimension_semantics=("arbitrary",) * n_grid,
                                vmem_limit_bytes=VMEM_LIMIT_BYTES)


def _dot(a, b):
    return jnp.dot(a, b, preferred_element_type=F32)


def _dot_nt(a, b):
    return lax.dot_general(a, b, (((1,), (1,)), ((), ())), preferred_element_type=F32)


def _dot_tn(a, b):
    return lax.dot_general(a, b, (((0,), (0,)), ((), ())), preferred_element_type=F32)


def _rms(x, g):
    r = lax.rsqrt(jnp.mean(x * x, axis=-1, keepdims=True) + RMS_EPS)
    return x * r * g


def _rms_bwd(dn, x, g):
    r = lax.rsqrt(jnp.mean(x * x, axis=-1, keepdims=True) + RMS_EPS)
    xr = x * r
    dg = jnp.sum(dn * xr, axis=0, keepdims=True)
    t = dn * g
    dx = r * (t - xr * jnp.mean(t * xr, axis=-1, keepdims=True))
    return dx, dg


def _gelu_parts(x):
    cdf = 0.5 * (1.0 + lax.erf(x * (1.0 / math.sqrt(2.0))))
    pdf = jnp.exp(-0.5 * x * x) * (1.0 / math.sqrt(2.0 * math.pi))
    return x * cdf, cdf + x * pdf


def _gelu(x):
    return x * (0.5 * (1.0 + lax.erf(x * (1.0 / math.sqrt(2.0)))))


def _sigmoid(x):
    return 1.0 / (1.0 + jnp.exp(-x))


def _ffn_fwd(x, g_pre, wpack, g_post, tm, tail, gather=None):
    S = x.shape[0]
    nt, nf, tf = S // tm, NSH, D
    n_tail_out = 1 if tail[0] == "norm" else 2

    def body(x_ref, gpre_ref, wg_ref, wu_ref, wd_ref, gpost_ref, tail_ref, *rest):
        if gather is not None:
            wp_ref, rest = rest[0], rest[1:]
        h_ref, a_ref, b_ref, y_ref, xo_ref = rest[:5]
        tail_out = rest[5:5 + n_tail_out]
        rest = rest[5 + n_tail_out:]
        if gather is not None:
            g_ref, h_s, acc, send_sems, recv_sems = rest
        else:
            h_s, acc = rest
        i = pl.program_id(0)
        j = pl.program_id(1)

        if gather is not None:
            @pl.when((i == 0) & (j == 0))
            def _():
                _gather_start(wp_ref, g_ref, send_sems, recv_sems)

        @pl.when(j == 0)
        def _():
            h = _rms(x_ref[...], gpre_ref[...]).astype(BF)
            h_s[...] = h
            h_ref[...] = h
            acc[...] = jnp.zeros_like(acc)

        h = h_s[...]
        a = _dot(h, wg_ref[...])
        b = _dot(h, wu_ref[...])
        a_ref[...] = a.astype(BF)
        b_ref[...] = b.astype(BF)
        act = (a * _sigmoid(a)) * b
        acc[...] += _dot(act.astype(BF), wd_ref[...])

        if tail[0] == "loss":
            @pl.when((i == 0) & (j == 0))
            def _():
                tail_out[1][...] = jnp.zeros_like(tail_out[1])

        @pl.when(j == nf - 1)
        def _():
            y = acc[...]
            y_ref[...] = y
            xo = x_ref[...] + 0.5 * _rms(y, gpost_ref[...])
            xo_ref[...] = xo
            if tail[0] == "norm":
                tail_out[0][...] = _rms(xo, tail_ref[...]).astype(BF)
            else:
                e = xo - tail_ref[...]
                tail_out[0][...] = e * (1.0 / D)
                tail_out[1][...] += jnp.sum(e * e) * (0.5 / D)

        if gather is not None:
            @pl.when((i == nt - 1) & (j == nf - 1))
            def _():
                _gather_finish(wp_ref, g_ref, send_sems, recv_sems)

    row = pl.BlockSpec((tm, D), lambda i, j: (i, 0))
    vec = pl.BlockSpec((1, D), lambda i, j: (0, 0))
    anywhere = pl.BlockSpec(memory_space=pl.ANY)
    in_specs = [row, vec,
                pl.BlockSpec((None, D, tf), lambda i, j: (j, 0, 0)),
                pl.BlockSpec((None, D, tf), lambda i, j: (j, 1, 0)),
                pl.BlockSpec((None, tf, D), lambda i, j: (j, 2, 0)),
                vec, vec if tail[0] == "norm" else row]
    out_specs = [row,
                 pl.BlockSpec((tm, tf), lambda i, j: (i, j)),
                 pl.BlockSpec((tm, tf), lambda i, j: (i, j)),
                 row, row]
    out_shape = [jax.ShapeDtypeStruct((S, D), BF),
                 jax.ShapeDtypeStruct((S, F), BF),
                 jax.ShapeDtypeStruct((S, F), BF),
                 jax.ShapeDtypeStruct((S, D), F32),
                 jax.ShapeDtypeStruct((S, D), F32)]
    if tail[0] == "norm":
        out_specs.append(row)
        out_shape.append(jax.ShapeDtypeStruct((S, D), BF))
    else:
        out_specs += [row, pl.BlockSpec((8, HD), lambda i, j: (0, 0))]
        out_shape += [jax.ShapeDtypeStruct((S, D), F32), jax.ShapeDtypeStruct((8, HD), F32)]
    scratch = [pltpu.VMEM((tm, D), BF), pltpu.VMEM((tm, D), F32)]
    args = [x, g_pre, wpack, wpack, wpack, g_post, tail[1]]
    if gather is not None:
        in_specs.append(anywhere)
        out_specs.append(anywhere)
        out_shape.append(jax.ShapeDtypeStruct((NSH,) + gather.shape, gather.dtype))
        scratch += [pltpu.SemaphoreType.DMA((6,)), pltpu.SemaphoreType.DMA((6,))]
        args.append(gather)
    res = list(pl.pallas_call(
        body, name="ffn_fwd" if gather is None else "ffn_fwd_gather",
        grid=(nt, nf),
        in_specs=in_specs, out_specs=out_specs, out_shape=out_shape, scratch_shapes=scratch,
        compiler_params=_params(2),
    )(*args))
    if gather is not None:
        res[-1] = _place_own_shard(res[-1], gather)
    return res


def _ffn_bwd(dxo, y, g_post, a, b, wpack, x_in, g_pre, tm, riders=()):
    S = dxo.shape[0]
    nt, nf, tf = S // tm, NSH, D
    nr = len(riders)

    def body(dxo_ref, y_ref, gpost_ref, a_ref, b_ref, wg_ref, wu_ref, wd_ref, xin_ref, gpre_ref, *rest):
        rsrc = rest[:nr]
        dy_ref, da_ref, db_ref, act_ref, dxin_ref, dgpost_ref, dgpre_ref = rest[nr:nr + 7]
        rdst = rest[nr + 7:2 * nr + 7]
        dy_s, acc = rest[2 * nr + 7:2 * nr + 9]
        sems = rest[2 * nr + 9:]
        i = pl.program_id(0)
        j = pl.program_id(1)

        @pl.when((i == 0) & (j == 0))
        def _():
            dgpost_ref[...] = jnp.zeros_like(dgpost_ref)
            dgpre_ref[...] = jnp.zeros_like(dgpre_ref)
            for k, (kind, _) in enumerate(riders):
                _rider_start(kind, rsrc[k], rdst[k], sems[2 * k], sems[2 * k + 1])

        @pl.when(j == 0)
        def _():
            dy, dg = _rms_bwd(0.5 * dxo_ref[...], y_ref[...], gpost_ref[...])
            dyb = dy.astype(BF)
            dy_s[...] = dyb
            dy_ref[...] = dyb
            dgpost_ref[...] += dg
            acc[...] = jnp.zeros_like(acc)

        dact = _dot_nt(dy_s[...], wd_ref[...])
        av = a_ref[...].astype(F32)
        bv = b_ref[...].astype(F32)
        sig = _sigmoid(av)
        sl = av * sig
        act_ref[...] = (sl * bv).astype(BF)
        dbb = (dact * sl).astype(BF)
        dab = (dact * bv * (sig * (1.0 + av * (1.0 - sig)))).astype(BF)
        da_ref[...] = dab
        db_ref[...] = dbb
        acc[...] += _dot_nt(dab, wg_ref[...]) + _dot_nt(dbb, wu_ref[...])

        @pl.when(j == nf - 1)
        def _():
            dx, dg = _rms_bwd(acc[...], xin_ref[...], gpre_ref[...])
            dxin_ref[...] = dxo_ref[...] + dx
            dgpre_ref[...] += dg

        if riders:
            @pl.when((i == nt - 1) & (j == nf - 1))
            def _():
                for k, (kind, _) in enumerate(riders):
                    _rider_finish(kind, rsrc[k], rdst[k], sems[2 * k], sems[2 * k + 1])

    row = pl.BlockSpec((tm, D), lambda i, j: (i, 0))
    vec = pl.BlockSpec((1, D), lambda i, j: (0, 0))
    ff = pl.BlockSpec((tm, tf), lambda i, j: (i, j))
    anywhere = pl.BlockSpec(memory_space=pl.ANY)
    in_specs = [row, row, vec, ff, ff,
                pl.BlockSpec((None, D, tf), lambda i, j: (j, 0, 0)),
                pl.BlockSpec((None, D, tf), lambda i, j: (j, 1, 0)),
                pl.BlockSpec((None, tf, D), lambda i, j: (j, 2, 0)),
                row, vec]
    out_specs = [row, ff, ff, ff, row, vec, vec]
    out_shape = [jax.ShapeDtypeStruct((S, D), BF),
                 jax.ShapeDtypeStruct((S, F), BF),
                 jax.ShapeDtypeStruct((S, F), BF),
                 jax.ShapeDtypeStruct((S, F), BF),
                 jax.ShapeDtypeStruct((S, D), F32),
                 jax.ShapeDtypeStruct((1, D), F32),
                 jax.ShapeDtypeStruct((1, D), F32)]
    scratch = [pltpu.VMEM((tm, D), BF), pltpu.VMEM((tm, D), F32)]
    args = [dxo, y, g_post, a, b, wpack, wpack, wpack, x_in, g_pre]
    for rider in riders:
        in_specs.append(anywhere)
        out_specs.append(anywhere)
        out_shape.append(_rider_out(rider))
        scratch += _rider_sems(rider)
        args.append(rider[1])
    return pl.pallas_call(
        body, name="ffn_bwd" if not riders else "ffn_bwd_riders",
        grid=(nt, nf),
        in_specs=in_specs, out_specs=out_specs, out_shape=out_shape, scratch_shapes=scratch,
        compiler_params=_params(2),
    )(*args)


def _mm_tn(a, b, bm, bn, bt, into=None, rider=None):
    S, M = a.shape
    N = b.shape[1]
    nt = S // bt
    n_in = 2 + (into is not None and into[0] is not None) + (rider is not None)

    def body(*refs):
        a_ref, b_ref, o_ref = refs[0], refs[1], refs[n_in]
        m, n, t = pl.program_id(0), pl.program_id(1), pl.program_id(2)

        if rider is not None:
            rsrc_ref, rdst_ref, send_sems, recv_sems = refs[n_in - 1], refs[n_in + 1], refs[-2], refs[-1]

            @pl.when((m == 0) & (n == 0) & (t == 0))
            def _():
                _rider_start(rider[0], rsrc_ref, rdst_ref, send_sems, recv_sems)

        @pl.when(t == 0)
        def _():
            o_ref[...] = jnp.zeros_like(o_ref)

        o_ref[...] += _dot_tn(a_ref[...], b_ref[...])

        if rider is not None:
            @pl.when((m == M // bm - 1) & (n == N // bn - 1) & (t == nt - 1))
            def _():
                _rider_finish(rider[0], rsrc_ref, rdst_ref, send_sems, recv_sems)

    in_specs = [pl.BlockSpec((bt, bm), lambda m, n, t: (t, m)),
                pl.BlockSpec((bt, bn), lambda m, n, t: (t, n))]
    args, aliases = [a, b], {}
    if into is None:
        out_spec = pl.BlockSpec((bm, bn), lambda m, n, t: (m, n))
        out_shape = jax.ShapeDtypeStruct((M, N), F32)
    else:
        buf, rows, rb, by = into
        assert bn == D and (M == bm if by == "cols" else (M == NSH * bm and N == D))
        if by == "cols":
            out_spec = pl.BlockSpec((None, bm, bn), lambda m, n, t: (n, rb, 0))
        else:
            out_spec = pl.BlockSpec((None, bm, bn), lambda m, n, t: (m, rb, 0))
        out_shape = jax.ShapeDtypeStruct((NSH, rows, D), F32)
        if buf is not None:
            in_specs.append(pl.BlockSpec(memory_space=pl.ANY))
            args.append(buf)
            aliases = {2: 0}
    if rider is None:
        return pl.pallas_call(
            body, name="mm_tn",
            grid=(M // bm, N // bn, nt),
            in_specs=in_specs, out_specs=out_spec, out_shape=out_shape,
            input_output_aliases=aliases,
            compiler_params=_params(3),
        )(*args)
    anywhere = pl.BlockSpec(memory_space=pl.ANY)
    return pl.pallas_call(
        body, name="mm_tn_rider",
        grid=(M // bm, N // bn, nt),
        in_specs=in_specs + [anywhere], out_specs=[out_spec, anywhere], out_shape=[out_shape, _rider_out(rider)],
        scratch_shapes=_rider_sems(rider),
        input_output_aliases=aliases,
        compiler_params=_params(3),
    )(*args, rider[1])


def _mm(a, w, tm, tn, out_dtype, first_block_scale=1.0):
    S, K = a.shape
    N = w.shape[1]

    def body(a_ref, w_ref, o_ref):
        r = _dot(a_ref[...], w_ref[...])
        if first_block_scale != 1.0:
            r = r * jnp.where(pl.program_id(1) == 0, first_block_scale, 1.0)
        o_ref[...] = r.astype(out_dtype)

    return pl.pallas_call(
        body, name="mm",
        grid=(S // tm, N // tn),
        in_specs=[pl.BlockSpec((tm, K), lambda i, j: (i, 0)),
                  pl.BlockSpec((K, tn), lambda i, j: (0, j))],
        out_specs=pl.BlockSpec((tm, tn), lambda i, j: (i, j)),
        out_shape=jax.ShapeDtypeStruct((S, N), out_dtype),
        compiler_params=_params(2),
    )(a, w)


def _mix_in_bwd(dz, wcat, dzf, wf, x1, g, dx2, tm, rider):
    S = dz.shape[0]
    nk = 2
    kb = ZW // nk

    def body(dz_ref, w_ref, dzf_ref, wf_ref, x_ref, g_ref, dx2_ref, rsrc_ref, dx1_ref, dg_ref, rdst_ref,
             acc, send_sems, recv_sems):
        i = pl.program_id(0)
        k = pl.program_id(1)

        @pl.when((i == 0) & (k == 0))
        def _():
            _rider_start(rider[0], rsrc_ref, rdst_ref, send_sems, recv_sems)
            dg_ref[...] = jnp.zeros_like(dg_ref)

        @pl.when(k == 0)
        def _():
            acc[...] = _dot_nt(dzf_ref[...], wf_ref[...])

        acc[...] += _dot_nt(dz_ref[...], w_ref[...])

        @pl.when(k == nk - 1)
        def _():
            dx, dg = _rms_bwd(acc[...], x_ref[...], g_ref[...])
            dx1_ref[...] = dx2_ref[...] + dx
            dg_ref[...] += dg

        @pl.when((i == S // tm - 1) & (k == nk - 1))
        def _():
            _rider_finish(rider[0], rsrc_ref, rdst_ref, send_sems, recv_sems)

    row = pl.BlockSpec((tm, D), lambda i, k: (i, 0))
    vec = pl.BlockSpec((1, D), lambda i, k: (0, 0))
    anywhere = pl.BlockSpec(memory_space=pl.ANY)
    return pl.pallas_call(
        body, name="mix_in_bwd",
        grid=(S // tm, nk),
        in_specs=[pl.BlockSpec((tm, kb), lambda i, k: (i, k)),
                  pl.BlockSpec((D, kb), lambda i, k: (0, k)),
                  pl.BlockSpec((tm, HD), lambda i, k: (i, 0)),
                  pl.BlockSpec((D, HD), lambda i, k: (0, 0)),
                  row, vec, row, anywhere],
        out_specs=[row, vec, anywhere],
        out_shape=[jax.ShapeDtypeStruct((S, D), F32), jax.ShapeDtypeStruct((1, D), F32), _rider_out(rider)],
        scratch_shapes=[pltpu.VMEM((tm, D), F32)] + _rider_sems(rider),
        compiler_params=_params(2),
    )(dz, wcat, dzf, wf, x1, g, dx2, rider[1])


def _scan_rows(blk, reverse):
    n = blk.shape[0]
    row = lax.broadcasted_iota(jnp.int32, blk.shape, 0)
    d = 1
    while d < n:
        if reverse:
            blk = blk + jnp.where(row < n - d, pltpu.roll(blk, n - d, 0), 0.0)
        else:
            blk = blk + jnp.where(row >= d, pltpu.roll(blk, d, 0), 0.0)
        d *= 2
    return blk


def _forget_cumsum(zf, bf, cs):
    S = zf.shape[0]

    def body(zf_ref, bf_ref, c_ref, cb_ref, carry):
        @pl.when(pl.program_id(0) == 0)
        def _():
            carry[...] = jnp.zeros_like(carry)

        x = zf_ref[...] + bf_ref[...]
        logf = jnp.minimum(x, 0.0) - jnp.log1p(jnp.exp(-jnp.abs(x)))
        sc = _scan_rows(logf, False) + carry[...]
        carry[...] = sc[cs - 1:cs, :]
        sc = sc * LOG2E
        c_ref[...] = sc
        for h in range(H):
            cb_ref[h] = jnp.broadcast_to(sc[:, h:h + 1], (cs, HD))

    return pl.pallas_call(
        body, name="forget_cumsum",
        grid=(S // cs,),
        in_specs=[pl.BlockSpec((cs, HD), lambda i: (i, 0)), pl.BlockSpec((1, HD), lambda i: (0, 0))],
        out_specs=[pl.BlockSpec((cs, HD), lambda i: (i, 0)),
                   pl.BlockSpec((H, cs, HD), lambda i: (0, i, 0))],
        out_shape=[jax.ShapeDtypeStruct((S, HD), F32), jax.ShapeDtypeStruct((H, S, HD), F32)],
        scratch_shapes=[pltpu.VMEM((1, HD), F32)],
        compiler_params=_params(1),
    )(zf, bf)


def _forget_bwd(dc, zf, bf, cs):
    S = dc.shape[0]
    nc = S // cs

    def body(dc_ref, zf_ref, bf_ref, dzf_ref, dbf_ref, carry):
        @pl.when(pl.program_id(0) == 0)
        def _():
            carry[...] = jnp.zeros_like(carry)
            dbf_ref[...] = jnp.zeros_like(dbf_ref)

        sc = _scan_rows(dc_ref[...], True) + carry[...]
        carry[...] = sc[0:1, :]
        x = zf_ref[...] + bf_ref[...]
        dz = sc * _sigmoid(-x)
        dzf_ref[...] = dz.astype(BF)
        dbf_ref[...] += jnp.sum(dz, axis=0, keepdims=True)

    rev = pl.BlockSpec((cs, HD), lambda i: (nc - 1 - i, 0))
    vec = pl.BlockSpec((1, HD), lambda i: (0, 0))
    return pl.pallas_call(
        body, name="forget_bwd",
        grid=(nc,),
        in_specs=[rev, rev, vec],
        out_specs=[rev, vec],
        out_shape=[jax.ShapeDtypeStruct((S, HD), BF), jax.ShapeDtypeStruct((1, HD), F32)],
        scratch_shapes=[pltpu.VMEM((1, HD), F32)],
        compiler_params=_params(1),
    )(dc, zf, bf)


def _lanes(x, n):
    return x if n == HD else jnp.concatenate([x] * (n // HD), axis=1)


def _causal_mask(i, j, t, rows_are_queries):
    r = lax.broadcasted_iota(jnp.int32, (t, t), 0)
    c = lax.broadcasted_iota(jnp.int32, (t, t), 1)
    if rows_are_queries:
        return (j * t + c) <= (i * t + r)
    return (j * t + r) <= (i * t + c)


def _fox_fwd(z, ccol_b, crow, t):
    S = z.shape[0]
    nq = S // t

    def body(q_ref, kv_ref, cc_ref, cr_ref, o_ref, lse_ref, m_s, acc_s, s_a, s_b):
        i = pl.program_id(1)
        ct = cc_ref[...]
        ones = jnp.ones((t, HD), BF)
        m_s[...] = jnp.full_like(m_s, NEG)
        acc_s[...] = jnp.zeros_like(acc_s)

        def scores(j, s_ref):
            off = pl.multiple_of(j * t, t)
            s_ref[...] = _dot_nt(q_ref[...], kv_ref[pl.ds(off, t), :HD]) - cr_ref[pl.ds(j, 1), :]

        def consume(j, s_ref, masked):
            off = pl.multiple_of(j * t, t)
            v1 = jnp.concatenate([kv_ref[pl.ds(off, t), HD:], ones], axis=1)
            s = s_ref[...]
            if masked:
                s = jnp.where(_causal_mask(i, j, t, True), s, NEG)
            m_old = m_s[...]
            m_new = jnp.maximum(m_old, jnp.max(s, axis=1, keepdims=True))
            p = jnp.exp2(s - _lanes(m_new, t))
            alpha = jnp.exp2(m_old - m_new)
            acc_s[...] = _lanes(alpha, 2 * HD) * acc_s[...] + _dot(p.astype(BF), v1)
            m_s[...] = m_new

        scores(0, s_a)

        def pair(jj, carry):
            j = 2 * jj
            scores(j + 1, s_b)
            consume(j, s_a, False)
            scores(j + 2, s_a)
            consume(j + 1, s_b, False)
            return carry

        lax.fori_loop(0, i // 2, pair, 0)

        @pl.when(i % 2 == 0)
        def _():
            consume(i, s_a, True)

        @pl.when(i % 2 == 1)
        def _():
            scores(i, s_b)
            consume(i - 1, s_a, False)
            consume(i, s_b, True)

        l = acc_s[:, HD:]
        o_ref[...] = (acc_s[:, :HD] / l).astype(BF)
        lse_ref[...] = m_s[...] + ct + jnp.log2(l)

    return pl.pallas_call(
        body, name="fox_fwd",
        grid=(H, nq),
        in_specs=[pl.BlockSpec((t, HD), lambda h, i: (i, h)),
                  pl.BlockSpec((S, 2 * HD), lambda h, i: (0, 4 + h)),
                  pl.BlockSpec((None, t, HD), lambda h, i: (h, i, 0)),
                  pl.BlockSpec((None, nq, t), lambda h, i: (h, 0, 0))],
        out_specs=[pl.BlockSpec((t, HD), lambda h, i: (i, h)),
                   pl.BlockSpec((None, t, HD), lambda h, i: (h, i, 0))],
        out_shape=[jax.ShapeDtypeStruct((S, D), BF), jax.ShapeDtypeStruct((H, S, HD), F32)],
        scratch_shapes=[pltpu.VMEM((t, HD), F32), pltpu.VMEM((t, 2 * HD), F32),
                        pltpu.VMEM((t, t), F32), pltpu.VMEM((t, t), F32)],
        compiler_params=_params(2),
    )(z, z, ccol_b, crow)


def _fox_bwd(z, do, o, lse_b, ccol_b, crow, dz, t, rider):
    S = z.shape[0]
    nq = S // t

    def body(q_ref, kv_ref, do_ref, o_ref, lse_ref, cc_ref, cr_ref, dz_in, rsrc_ref,
             dq_ref, dkt_ref, dvt_ref, dck_ref, dcq_ref, rdst_ref, acc_s, r_s, send_sems, recv_sems):
        del dz_in
        i = pl.program_id(1)

        @pl.when((pl.program_id(0) == 0) & (i == 0))
        def _():
            _rider_start(rider[0], rsrc_ref, rdst_ref, send_sems, recv_sems)

        @pl.when(i == 0)
        def _():
            dkt_ref[...] = jnp.zeros_like(dkt_ref)
            dvt_ref[...] = jnp.zeros_like(dvt_ref)
            dck_ref[...] = jnp.zeros_like(dck_ref)

        q = q_ref[...]
        dout = do_ref[...]
        qt = jnp.transpose(q.astype(F32)).astype(BF)
        dot_ = jnp.transpose(dout.astype(F32)).astype(BF)
        off_t = _lanes(lse_ref[...] - cc_ref[...], t)
        delta = jnp.sum(dout.astype(F32) * o_ref[...].astype(F32), axis=1, keepdims=True)
        delta = _lanes(jnp.broadcast_to(delta, (t, HD)), t)
        acc_s[...] = jnp.zeros_like(acc_s)
        r_s[...] = jnp.zeros_like(r_s)

        def step(j, masked):
            off = pl.multiple_of(j * t, t)
            k = kv_ref[pl.ds(off, t), :HD]
            v = kv_ref[pl.ds(off, t), HD:]
            p = jnp.exp2(_dot_nt(q, k) - cr_ref[pl.ds(j, 1), :] - off_t)
            if masked:
                p = jnp.where(_causal_mask(i, j, t, True), p, 0.0)
            ds = p * (_dot_nt(dout, v) - delta)
            dsb = ds.astype(BF)
            acc_s[...] += _dot(dsb, k)
            dkt_ref[j] += _dot(qt, dsb)
            dvt_ref[j] += _dot(dot_, p.astype(BF))
            dck_ref[pl.ds(j, 1), :] += jnp.sum(ds, axis=0, keepdims=True)
            r_s[...] += jnp.sum(ds, axis=1, keepdims=True)

        def full_step(j, carry):
            step(j, False)
            return carry

        lax.fori_loop(0, i, full_step, 0)
        step(i, True)
        dq_ref[...] = (acc_s[...] * SCALE).astype(BF)
        dcq_ref[...] = jnp.transpose(r_s[...])[0:1, :]

        @pl.when((pl.program_id(0) == H - 1) & (i == nq - 1))
        def _():
            _rider_finish(rider[0], rsrc_ref, rdst_ref, send_sems, recv_sems)

    qspec = pl.BlockSpec((t, HD), lambda h, i: (i, h))
    bspec = pl.BlockSpec((None, t, HD), lambda h, i: (h, i, 0))
    rows = pl.BlockSpec((None, nq, t), lambda h, i: (h, 0, 0))
    tspec = pl.BlockSpec((None, nq, HD, t), lambda h, i: (h, 0, 0, 0))
    tshape = jax.ShapeDtypeStruct((H, nq, HD, t), F32)
    anywhere = pl.BlockSpec(memory_space=pl.ANY)
    return pl.pallas_call(
        body, name="fox_bwd",
        grid=(H, nq),
        in_specs=[qspec,
                  pl.BlockSpec((S, 2 * HD), lambda h, i: (0, 4 + h)),
                  qspec, qspec, bspec, bspec, rows, anywhere, anywhere],
        out_specs=[qspec, tspec, tspec, rows, pl.BlockSpec((None, None, 1, t), lambda h, i: (h, i, 0, 0)),
                   anywhere],
        out_shape=[jax.ShapeDtypeStruct((S, ZW), BF), tshape, tshape,
                   jax.ShapeDtypeStruct((H, nq, t), F32), jax.ShapeDtypeStruct((H, nq, 1, t), F32),
                   _rider_out(rider)],
        scratch_shapes=[pltpu.VMEM((t, HD), F32), pltpu.VMEM((t, HD), F32)] + _rider_sems(rider),
        input_output_aliases={7: 0},
        compiler_params=_params(2),
    )(z, z, do, o, lse_b, ccol_b, crow, dz, rider[1])


def _fox_bwd_finish(dkt, dvt, dz, t):
    nq = dkt.shape[1]
    S = nq * t

    def body(dkt_ref, dvt_ref, dz_in, dkv_ref):
        del dz_in
        for j in range(nq):
            rows = slice(j * t, (j + 1) * t)
            dkv_ref[rows, :HD] = (jnp.transpose(dkt_ref[j]) * LN2).astype(BF)
            dkv_ref[rows, HD:] = jnp.transpose(dvt_ref[j]).astype(BF)

    tspec = pl.BlockSpec((None, nq, HD, t), lambda h: (h, 0, 0, 0))
    return pl.pallas_call(
        body, name="fox_bwd_finish",
        grid=(H,),
        in_specs=[tspec, tspec, pl.BlockSpec(memory_space=pl.ANY)],
        out_specs=pl.BlockSpec((S, 2 * HD), lambda h: (0, 4 + h)),
        out_shape=jax.ShapeDtypeStruct((S, ZW), BF),
        input_output_aliases={2: 0},
        compiler_params=_params(1),
    )(dkt, dvt, dz)


def _sgu_mask(transposed):
    r = lax.broadcasted_iota(jnp.int32, (L, L), 0)
    c = lax.broadcasted_iota(jnp.int32, (L, L), 1)
    if transposed:
        return (r // CHUNK) <= (c // CHUNK)
    return (c // CHUNK) <= (r // CHUNK)


def _ln_group(vs, lng, lnb):
    mu = jnp.mean(vs, axis=-1, keepdims=True)
    xc = vs - mu
    rstd = lax.rsqrt(jnp.mean(xc * xc, axis=-1, keepdims=True) + LN_EPS)
    xhat = xc * rstd
    return xhat, rstd, xhat * lng + lnb


def _mix_out_fwd(z, oa, ln_g, ln_b, ws, bst, wout, x1, g_post, tm):
    S = z.shape[0]
    nw = tm // L

    def body(u_ref, sv_ref, ga_ref, gb_ref, oa_ref, lng_ref, lnb_ref, ws_ref, bst_ref, wout_ref, x1_ref, gp_ref,
             mg_ref, y_ref, x2_ref, mg_s):
        mask = _sgu_mask(False)
        for g in range(G):
            cols = slice(g * L, (g + 1) * L)
            wm = jnp.where(mask, ws_ref[g], 0.0).astype(BF)
            bcol = bst_ref[:, g:g + 1]
            lng = lng_ref[:, cols]
            lnb = lnb_ref[:, cols]
            for w in range(nw):
                rows = slice(w * L, (w + 1) * L)
                vs = _gelu(sv_ref[rows, cols].astype(F32))
                _, _, vn = _ln_group(vs, lng, lnb)
                mixed = _dot(wm, vn.astype(BF)) + bcol
                ob = _gelu(u_ref[rows, cols].astype(F32)) * mixed
                mg = (_sigmoid(ga_ref[rows, cols].astype(F32)) * oa_ref[rows, cols].astype(F32)
                      + _sigmoid(gb_ref[rows, cols].astype(F32)) * ob)
                mg_s[rows, cols] = mg.astype(BF)
        mgb = mg_s[...]
        mg_ref[...] = mgb
        y = _dot(mgb, wout_ref[...])
        y_ref[...] = y
        x2_ref[...] = x1_ref[...] + _rms(y, gp_ref[...])

    row = pl.BlockSpec((tm, D), lambda i: (i, 0))
    vec = pl.BlockSpec((1, D), lambda i: (0, 0))

    def zcol(kb):
        return pl.BlockSpec((tm, D), lambda i: (i, kb))

    return pl.pallas_call(
        body, name="mix_out_fwd",
        grid=(S // tm,),
        in_specs=[zcol(3), zcol(4), zcol(5), zcol(6), row, vec, vec,
                  pl.BlockSpec((G, L, L), lambda i: (0, 0, 0)),
                  pl.BlockSpec((L, HD), lambda i: (0, 0)),
                  pl.BlockSpec((D, D), lambda i: (0, 0)),
                  row, vec],
        out_specs=[row, row, row],
        out_shape=[jax.ShapeDtypeStruct((S, D), BF),
                   jax.ShapeDtypeStruct((S, D), F32),
                   jax.ShapeDtypeStruct((S, D), F32)],
        scratch_shapes=[pltpu.VMEM((tm, D), BF)],
        compiler_params=_params(1),
    )(z, z, z, z, oa, ln_g, ln_b, ws, bst, wout, x1, g_post)


def _mix_out_bwd(dx2, y2, g_post, wout, z, oa, ln_g, ln_b, ws, wst, bst, tm, rider):
    S = z.shape[0]
    nw = tm // L

    def body(dx2_ref, y_ref, gp_ref, wout_ref, u_ref, sv_ref, ga_ref, gb_ref, oa_ref, lng_ref, lnb_ref,
             ws_ref, wst_ref, bst_ref, q_ref,
             dz_ref, dy_ref, doa_ref, dgp_ref, dlng_ref, dlnb_ref, dws_ref, dbst_ref, t_ref,
             dzg_s, dm_s, send_sems, recv_sems):
        i = pl.program_id(0)
        c = pl.program_id(1)

        @pl.when((i == 0) & (c == 0))
        def _():
            _rider_start(rider[0], q_ref, t_ref, send_sems, recv_sems)
            dgp_ref[...] = jnp.zeros_like(dgp_ref)
            dlng_ref[...] = jnp.zeros_like(dlng_ref)
            dlnb_ref[...] = jnp.zeros_like(dlnb_ref)
            dws_ref[...] = jnp.zeros_like(dws_ref)
            dbst_ref[...] = jnp.zeros_like(dbst_ref)

        @pl.when(c == 0)
        def _():
            dy, dg = _rms_bwd(dx2_ref[...], y_ref[...], gp_ref[...])
            dyb = dy.astype(BF)
            dy_ref[...] = dyb
            dgp_ref[...] += dg
            dm_s[...] = _dot_nt(dyb, wout_ref[...])
            mask = _sgu_mask(False)
            mask_t = _sgu_mask(True)
            lane = lax.broadcasted_iota(jnp.int32, (L, HD), 1)
            for g in range(G):
                cols = slice(g * L, (g + 1) * L)
                wm = jnp.where(mask, ws_ref[g], 0.0).astype(BF)
                wmt = jnp.where(mask_t, wst_ref[g], 0.0).astype(BF)
                bcol = bst_ref[:, g:g + 1]
                lng = lng_ref[:, cols]
                lnb = lnb_ref[:, cols]
                dws_g = jnp.zeros((L, L), F32)
                dbs_g = jnp.zeros((L, 1), F32)
                dlng_g = jnp.zeros((1, L), F32)
                dlnb_g = jnp.zeros((1, L), F32)
                for w in range(nw):
                    rows = slice(w * L, (w + 1) * L)
                    dm = dm_s[rows, cols]
                    vs, dvs_dz = _gelu_parts(sv_ref[rows, cols].astype(F32))
                    xhat, rstd, vn = _ln_group(vs, lng, lnb)
                    vnb = vn.astype(BF)
                    mixed = _dot(wm, vnb) + bcol
                    u, du_dz = _gelu_parts(u_ref[rows, cols].astype(F32))
                    sga = _sigmoid(ga_ref[rows, cols].astype(F32))
                    sgb = _sigmoid(gb_ref[rows, cols].astype(F32))
                    oav = oa_ref[rows, cols].astype(F32)
                    ob = u * mixed
                    doa_ref[rows, cols] = (dm * sga).astype(BF)
                    dzg_s[2, rows, cols] = (dm * oav * sga * (1.0 - sga)).astype(BF)
                    dzg_s[3, rows, cols] = (dm * ob * sgb * (1.0 - sgb)).astype(BF)
                    dob = dm * sgb
                    dzg_s[0, rows, cols] = (dob * mixed * du_dz).astype(BF)
                    dmixed = dob * u
                    dmb = dmixed.astype(BF)
                    dbs_g += jnp.sum(dmixed, axis=1, keepdims=True)
                    dws_g += _dot_nt(dmb, vnb)
                    dvn = _dot(wmt, dmb)
                    dlng_g += jnp.sum(dvn * xhat, axis=0, keepdims=True)
                    dlnb_g += jnp.sum(dvn, axis=0, keepdims=True)
                    dxh = dvn * lng
                    dvs = rstd * (dxh - jnp.mean(dxh, axis=-1, keepdims=True)
                                  - xhat * jnp.mean(dxh * xhat, axis=-1, keepdims=True))
                    dzg_s[1, rows, cols] = (dvs * dvs_dz).astype(BF)
                dws_ref[g] += jnp.where(mask, dws_g, 0.0)
                dbst_ref[...] += jnp.where(lane == g, dbs_g, 0.0)
                dlng_ref[:, cols] += dlng_g
                dlnb_ref[:, cols] += dlnb_g

        dz_ref[...] = dzg_s[c]

        @pl.when((i == S // tm - 1) & (c == 3))
        def _():
            _rider_finish(rider[0], q_ref, t_ref, send_sems, recv_sems)

    row = pl.BlockSpec((tm, D), lambda i, c: (i, 0))
    vec = pl.BlockSpec((1, D), lambda i, c: (0, 0))
    wsspec = pl.BlockSpec((G, L, L), lambda i, c: (0, 0, 0))
    bspec = pl.BlockSpec((L, HD), lambda i, c: (0, 0))
    anywhere = pl.BlockSpec(memory_space=pl.ANY)

    def zcol(kb):
        return pl.BlockSpec((tm, D), lambda i, c: (i, kb))

    return pl.pallas_call(
        body, name="mix_out_bwd",
        grid=(S // tm, 4),
        in_specs=[row, row, vec, pl.BlockSpec((D, D), lambda i, c: (0, 0)),
                  zcol(3), zcol(4), zcol(5), zcol(6), row, vec, vec, wsspec, wsspec, bspec, anywhere],
        out_specs=[pl.BlockSpec((tm, D), lambda i, c: (i, 3 + c)),
                   row, row, vec, vec, vec, wsspec, bspec, anywhere],
        out_shape=[jax.ShapeDtypeStruct((S, ZW), BF),
                   jax.ShapeDtypeStruct((S, D), BF),
                   jax.ShapeDtypeStruct((S, D), BF),
                   jax.ShapeDtypeStruct((1, D), F32),
                   jax.ShapeDtypeStruct((1, D), F32),
                   jax.ShapeDtypeStruct((1, D), F32),
                   jax.ShapeDtypeStruct((G, L, L), F32),
                   jax.ShapeDtypeStruct((L, HD), F32),
                   _rider_out(rider)],
        scratch_shapes=[pltpu.VMEM((4, tm, D), BF), pltpu.VMEM((tm, D), F32)] + _rider_sems(rider),
        compiler_params=_params(2),
    )(dx2, y2, g_post, wout, z, z, z, z, oa, ln_g, ln_b, ws, wst, bst, rider[1])


def _adamw(w, g, m, v, tr):
    R, C = w.shape

    def body(w_ref, g_ref, m_ref, v_ref, d_ref, nm_ref, nv_ref):
        gv = g_ref[...]
        m_new = ADAM_B1 * m_ref[...] + (1.0 - ADAM_B1) * gv
        v_new = ADAM_B2 * v_ref[...] + (1.0 - ADAM_B2) * (gv * gv)
        m_hat = m_new / (1.0 - ADAM_B1 ** ADAM_STEP)
        v_hat = v_new / (1.0 - ADAM_B2 ** ADAM_STEP)
        d_ref[...] = -ADAM_LR * (m_hat / (jnp.sqrt(v_hat) + ADAM_EPS) + ADAM_WD * w_ref[...])
        nm_ref[...] = m_new
        nv_ref[...] = v_new

    spec = pl.BlockSpec((tr, C), lambda i: (i, 0))
    shp = jax.ShapeDtypeStruct((R, C), F32)
    return pl.pallas_call(
        body, name="adamw",
        grid=(R // tr,),
        in_specs=[spec] * 4, out_specs=[spec] * 3, out_shape=[shp] * 3,
        compiler_params=_params(1),
    )(w, g, m, v)


def _mesh_pos():
    return lax.axis_index("x"), lax.axis_index("y"), lax.axis_index("c")


def _half(c, rows):
    return pl.ds(pl.multiple_of(c * rows, 16), rows)


def _other_chips(x, y):
    return [(1 - x, y), (x, 1 - y), (1 - x, 1 - y)]


def _remote(k, src, dst, to, send_sems, recv_sems):
    return pltpu.make_async_remote_copy(src_ref=src, dst_ref=dst, send_sem=send_sems.at[k],
                                        recv_sem=recv_sems.at[k], device_id=to, device_id_type=MESH)


def _gather_start(wp_ref, g_ref, send_sems, recv_sems):
    x, y, c = _mesh_pos()
    mine = _half(c, wp_ref.shape[0] // 2)
    for k, (px, py) in enumerate(_other_chips(x, y)):
        _remote(k, wp_ref.at[mine], g_ref.at[2 * x + y, mine], (px, py, c), send_sems, recv_sems).start()


def _gather_finish(wp_ref, g_ref, send_sems, recv_sems):
    x, y, c = _mesh_pos()
    sibling = (x, y, 1 - c)
    rows = wp_ref.shape[0] // 2
    mine, other = _half(c, rows), _half(1 - c, rows)
    chips = _other_chips(x, y)
    for k, (px, py) in enumerate(chips):
        land = g_ref.at[2 * px + py, mine]
        _remote(k, land, land, (px, py, c), send_sems, recv_sems).wait_recv()
        _remote(3 + k, land, land, sibling, send_sems, recv_sems).start()
    for k, (px, py) in enumerate(chips):
        land = g_ref.at[2 * px + py, other]
        _remote(3 + k, land, land, sibling, send_sems, recv_sems).wait_recv()
    for k, (px, py) in enumerate(chips):
        land = g_ref.at[2 * px + py, mine]
        _remote(k, wp_ref.at[mine], g_ref.at[2 * x + y, mine], (px, py, c), send_sems, recv_sems).wait_send()
        _remote(3 + k, land, land, sibling, send_sems, recv_sems).wait_send()


def _place_own_shard(g, wp):
    x, y, _ = _mesh_pos()
    return lax.dynamic_update_index_in_dim(g, wp, 2 * x + y, 0)


def _all_gather_weights(wp):
    def body(wp_ref, g_ref, send_sems, recv_sems):
        _gather_start(wp_ref, g_ref, send_sems, recv_sems)
        _gather_finish(wp_ref, g_ref, send_sems, recv_sems)

    g = pl.pallas_call(
        body, name="all_gather_weights",
        in_specs=[pl.BlockSpec(memory_space=pl.ANY)],
        out_specs=pl.BlockSpec(memory_space=pl.ANY),
        out_shape=jax.ShapeDtypeStruct((NSH,) + wp.shape, wp.dtype),
        scratch_shapes=[pltpu.SemaphoreType.DMA((6,)), pltpu.SemaphoreType.DMA((6,))],
        compiler_params=pltpu.CompilerParams(has_side_effects=True),
    )(wp)
    return _place_own_shard(g, wp)


def _scatter_copies(q_ref, t_ref, send_sems, recv_sems):
    x, y, c = _mesh_pos()
    return [_remote(k, q_ref.at[2 * px + py], t_ref.at[k], (px, py, c), send_sems, recv_sems)
            for k, (px, py) in enumerate(_other_chips(x, y))]


_FLIPS = [(fx, fy, fc) for fx in (0, 1) for fy in (0, 1) for fc in (0, 1)][1:]


def _rider_copies(kind, src_ref, dst_ref, send_sems, recv_sems):
    if kind == "scatter":
        return _scatter_copies(src_ref, dst_ref, send_sems, recv_sems)
    x, y, c = _mesh_pos()
    if kind == "broadcast":
        return [_remote(k, src_ref, dst_ref.at[4 * x + 2 * y + c], (x ^ fx, y ^ fy, c ^ fc), send_sems, recv_sems)
                for k, (fx, fy, fc) in enumerate(_FLIPS)]
    rows = src_ref.shape[1] // 2
    return [_remote(0, src_ref.at[:, _half(1 - c, rows)], dst_ref, (x, y, 1 - c), send_sems, recv_sems)]


def _place_own_block(sm, block):
    x, y, c = _mesh_pos()
    return lax.dynamic_update_index_in_dim(sm, block, 4 * x + 2 * y + c, 0)


def _rider_start(kind, src_ref, dst_ref, send_sems, recv_sems):
    for cp in _rider_copies(kind, src_ref, dst_ref, send_sems, recv_sems):
        cp.start()


def _rider_finish(kind, src_ref, dst_ref, send_sems, recv_sems):
    for cp in _rider_copies(kind, src_ref, dst_ref, send_sems, recv_sems):
        cp.wait()


def _rider_out(rider):
    kind, a = rider
    if kind == "scatter":
        return jax.ShapeDtypeStruct((3,) + a.shape[1:], a.dtype)
    if kind == "broadcast":
        return jax.ShapeDtypeStruct((NDEV,) + a.shape, a.dtype)
    return jax.ShapeDtypeStruct((a.shape[0], a.shape[1] // 2) + a.shape[2:], a.dtype)


def _rider_sems(rider):
    n = {"scatter": 3, "broadcast": 7, "exchange": 1}[rider[0]]
    return [pltpu.SemaphoreType.DMA((n,)), pltpu.SemaphoreType.DMA((n,))]


def _pair_exchange(p):
    rows = p.shape[1] // 2

    def body(p_ref, r_ref, send_sem, recv_sem):
        x, y, c = _mesh_pos()
        cp = pltpu.make_async_remote_copy(src_ref=p_ref.at[:, _half(1 - c, rows)], dst_ref=r_ref, send_sem=send_sem,
                                          recv_sem=recv_sem, device_id=(x, y, 1 - c), device_id_type=MESH)
        cp.start()
        cp.wait()

    return pl.pallas_call(
        body, name="pair_exchange",
        in_specs=[pl.BlockSpec(memory_space=pl.ANY)],
        out_specs=pl.BlockSpec(memory_space=pl.ANY),
        out_shape=jax.ShapeDtypeStruct((NSH, rows, D), F32),
        scratch_shapes=[pltpu.SemaphoreType.DMA, pltpu.SemaphoreType.DMA],
        compiler_params=pltpu.CompilerParams(has_side_effects=True),
    )(p)


def _pair_add(p, r, nb):
    rows = r.shape[1]
    tr = rows // nb

    def body(p_ref, r_ref, q_ref):
        q_ref[...] = (p_ref[...] + r_ref[...]).astype(BF)

    return pl.pallas_call(
        body, name="pair_add", grid=(NSH, nb),
        in_specs=[pl.BlockSpec((None, tr, D), lambda j, i: (j, lax.axis_index("c") * nb + i, 0)),
                  pl.BlockSpec((None, tr, D), lambda j, i: (j, i, 0))],
        out_specs=pl.BlockSpec((None, tr, D), lambda j, i: (j, i, 0)),
        out_shape=jax.ShapeDtypeStruct((NSH, rows, D), BF),
        compiler_params=_params(2),
    )(p, r)


def _chip_exchange(q, small):
    riders = (("scatter", q), ("broadcast", small))

    def body(q_ref, s_ref, t_ref, sm_ref, send_sems, recv_sems, ssend_sems, srecv_sems):
        _rider_start("scatter", q_ref, t_ref, send_sems, recv_sems)
        _rider_start("broadcast", s_ref, sm_ref, ssend_sems, srecv_sems)
        _rider_finish("scatter", q_ref, t_ref, send_sems, recv_sems)
        _rider_finish("broadcast", s_ref, sm_ref, ssend_sems, srecv_sems)

    anywhere = pl.BlockSpec(memory_space=pl.ANY)
    t, sm = pl.pallas_call(
        body, name="chip_exchange",
        in_specs=[anywhere, anywhere], out_specs=[anywhere, anywhere],
        out_shape=[_rider_out(r) for r in riders],
        scratch_shapes=_rider_sems(riders[0]) + _rider_sems(riders[1]),
        compiler_params=pltpu.CompilerParams(has_side_effects=True),
    )(q, small)
    return t, _place_own_block(sm, small)


def _shard_sum(p, r, t, nb):
    rows = r.shape[1]
    tr = rows // nb

    def shard():
        return 2 * lax.axis_index("x") + lax.axis_index("y")

    def body(p_ref, r_ref, t_ref, o_ref):
        s = p_ref[...] + r_ref[...]
        for k in range(3):
            s = s + t_ref[k].astype(F32)
        o_ref[...] = s

    return pl.pallas_call(
        body, name="shard_sum", grid=(nb,),
        in_specs=[pl.BlockSpec((None, tr, D), lambda i: (shard(), lax.axis_index("c") * nb + i, 0)),
                  pl.BlockSpec((None, tr, D), lambda i: (shard(), i, 0)),
                  pl.BlockSpec((3, tr, D), lambda i: (0, i, 0))],
        out_specs=pl.BlockSpec((tr, D), lambda i: (i, 0)),
        out_shape=jax.ShapeDtypeStruct((rows, D), F32),
        compiler_params=_params(1),
    )(p, r, t)


def _small_sum(sm):
    def body(sm_ref, o_ref):
        s = sm_ref[0]
        for k in range(1, NDEV):
            s = s + sm_ref[k]
        o_ref[...] = s

    return pl.pallas_call(
        body, name="small_sum",
        in_specs=[pl.BlockSpec(memory_space=pltpu.VMEM)],
        out_specs=pl.BlockSpec(memory_space=pltpu.VMEM),
        out_shape=jax.ShapeDtypeStruct(sm.shape[1:], F32),
    )(sm)


def _pair_gather(halves):
    n = len(halves)

    def body(*refs):
        gh_refs, o_refs, (send_sems, recv_sems) = refs[:n], refs[n:2 * n], refs[2 * n:]
        x, y, c = _mesh_pos()
        sibling = (x, y, 1 - c)
        for g in range(n):
            rows = gh_refs[g].shape[0]
            _remote(g, gh_refs[g], o_refs[g].at[_half(c, rows)], sibling, send_sems, recv_sems).start()
        for g in range(n):
            rows = gh_refs[g].shape[0]
            _remote(g, gh_refs[g], o_refs[g].at[_half(c, rows)], sibling, send_sems, recv_sems).wait_send()
            _remote(g, gh_refs[g], o_refs[g].at[_half(1 - c, rows)], sibling, send_sems, recv_sems).wait_recv()

    anywhere = pl.BlockSpec(memory_space=pl.ANY)
    outs = pl.pallas_call(
        body, name="pair_gather",
        in_specs=[anywhere] * n, out_specs=[anywhere] * n,
        out_shape=[jax.ShapeDtypeStruct((2 * h.shape[0], D), F32) for h in halves],
        scratch_shapes=[pltpu.SemaphoreType.DMA((n,)), pltpu.SemaphoreType.DMA((n,))],
        compiler_params=pltpu.CompilerParams(has_side_effects=True),
    )(*halves)
    c = lax.axis_index("c")
    return [lax.dynamic_update_slice_in_dim(o, h, c * h.shape[0], 0) for o, h in zip(outs, halves)]


def _pad_cols(a, n):
    return jnp.pad(a, ((0, 0), (0, n - a.shape[1])))


def _split_w_in(w_in_full):
    q, k, v = w_in_full[:, :D], w_in_full[:, D:2 * D], w_in_full[:, 2 * D:3 * D]
    f = w_in_full[:, 3 * D:3 * D + H]
    gates = w_in_full[:, 3 * D + H:]
    kv = jnp.stack([k.reshape(D, H, HD), v.reshape(D, H, HD)], axis=2).reshape(D, 2 * D)
    return jnp.concatenate([q, kv, gates], axis=1), _pad_cols(f, HD)


def _merge_w_in_grad(dwcat, dwf):
    kv = dwcat[:, D:3 * D].reshape(D, H, 2, HD)
    return jnp.concatenate([dwcat[:, :D], kv[:, :, 0].reshape(D, D), kv[:, :, 1].reshape(D, D),
                            dwf[:, :H], dwcat[:, 3 * D:]], axis=1)


def _ffn_weight_grads(h, da, db, act, dy, bt):
    g = _mm_tn(h, da, D, D, bt, into=(None, FFN_ROWS, 0, "cols"))
    g = _mm_tn(h, db, D, D, bt, into=(g, FFN_ROWS, 1, "cols"))
    return _mm_tn(act, dy, D, D, bt, into=(g, FFN_ROWS, 2, "rows"))


def _train_step(x, target, wp1, wp2, small, tm, t_attn):
    S = x.shape[0]
    g1pre, g1post = small["ffn1_pre_g"], small["ffn1_post_g"]
    gmpre, gmpost = small["mix_pre_g"], small["mix_post_g"]
    g2pre, g2post = small["ffn2_pre_g"], small["ffn2_post_g"]
    ln_g, ln_b = small["sgu_ln_g"], small["sgu_ln_b"]
    ws = small["sgu_w_s"][0]
    wst = jnp.swapaxes(ws, 1, 2)
    bst = _pad_cols(small["sgu_b_s"][0].T, HD)
    bf = _pad_cols(small["b_forget"], HD)

    w1 = _all_gather_weights(wp1)
    h1, a1, b1, y1, x1, h2, w2 = _ffn_fwd(x, g1pre, w1, g1post, tm, ("norm", gmpre), gather=wp2)
    wout = w2[:, FFN_ROWS:FFN_ROWS + 256, :].reshape(D, D)
    r0 = FFN_ROWS + 256
    w_in_full = jnp.concatenate(
        [blk for j in range(NSH) for blk in (w2[j, r0:r0 + D], w2[j, r0 + D:r0 + 2 * D, :WIN_SH - D])], axis=1)
    wcat, wf = _split_w_in(w_in_full)
    z = _mm(h2, wcat, min(1024, S), D, BF, first_block_scale=SCALE * LOG2E)
    zf = _mm(h2, wf, tm, HD, F32)
    cs = min(512, S)
    c, ccol_b = _forget_cumsum(zf, bf, cs)
    crow = jnp.transpose(c[:, :H]).reshape(H, S // t_attn, t_attn)
    oa, lse_b = _fox_fwd(z, ccol_b, crow, t_attn)
    merged, y2, x2 = _mix_out_fwd(z, oa, ln_g, ln_b, ws, bst, wout, x1, gmpost, tm)
    h3, a3, b3, y3, _, dx3, loss_acc = _ffn_fwd(x2, g2pre, w2, g2post, tm, ("loss", target))
    loss = loss_acc[0, 0]

    dy3, da3, db3, act3, dx2, dg2post, dg2pre = _ffn_bwd(dx3, y3, g2post, a3, b3, w2, x2, g2pre, tm)
    bt = min(2048, S)
    g_ffn2 = _ffn_weight_grads(h3, da3, db3, act3, dy3, bt)

    dz, dy2, doa, dgmpost, dlng, dlnb, dws, dbst, r_ffn2 = _mix_out_bwd(
        dx2, y2, gmpost, wout, z, oa, ln_g, ln_b, ws, wst, bst, tm, ("exchange", g_ffn2))
    q_ffn2 = _pair_add(g_ffn2, r_ffn2, 3)
    g_mix = _mm_tn(merged, dy2, 256, D, bt, into=(None, MIX_ROWS, 0, "rows"))
    dz, dkt, dvt, dc_keys, dc_queries, t_ffn2 = _fox_bwd(z, doa, oa, lse_b, ccol_b, crow, dz, t_attn,
                                                         ("scatter", q_ffn2))
    dz = _fox_bwd_finish(dkt, dvt, dz, t_attn)
    dc = _pad_cols(jnp.transpose(dc_queries.reshape(H, S) - dc_keys.reshape(H, S)), HD)
    dzf, dbf = _forget_bwd(dc, zf, bf, cs)
    dwcat = _mm_tn(h2, dz, D, D, bt)
    dwf = _mm_tn(h2, dzf, D, HD, bt)
    dwin = _merge_w_in_grad(dwcat, dwf)
    dwin_a = jnp.stack([dwin[:, j * WIN_SH:j * WIN_SH + D] for j in range(NSH)])
    dwin_b = jnp.stack([_pad_cols(dwin[:, j * WIN_SH + D:(j + 1) * WIN_SH], D) for j in range(NSH)])
    g_mix = lax.dynamic_update_slice(g_mix, jnp.concatenate([dwin_a, dwin_b], axis=1), (0, 256, 0))
    dx1, dgmpre, r_mix = _mix_in_bwd(dz, wcat, dzf, wf, x1, gmpre, dx2, tm, ("exchange", g_mix))
    q_mix = _pair_add(g_mix, r_mix, 3)

    small_early = _pack_small({
        "mix_pre_g": dgmpre, "mix_post_g": dgmpost, "ffn2_pre_g": dg2pre, "ffn2_post_g": dg2post,
        "sgu_ln_g": dlng, "sgu_ln_b": dlnb, "sgu_w_s": dws[None], "sgu_b_s": jnp.transpose(dbst[:, :G])[None],
        "b_forget": dbf[:, :H]}, _SMALL_EARLY)
    dy1, da1, db1, act1, dx, dg1post, dg1pre, t_mix, sm_early = _ffn_bwd(
        dx1, y1, g1post, a1, b1, w1, x, g1pre, tm, riders=(("scatter", q_mix), ("broadcast", small_early)))
    sm_early = _place_own_block(sm_early, small_early)
    g_gu = _mm_tn(h1, da1, D, D, bt, into=(None, 2 * D, 0, "cols"))
    g_gu = _mm_tn(h1, db1, D, D, bt, into=(g_gu, 2 * D, 1, "cols"))
    r_gu = _pair_exchange(g_gu)
    q_gu = _pair_add(g_gu, r_gu, 2)
    g_dn, t_gu = _mm_tn(act1, dy1, D, D, bt, into=(None, D, 0, "rows"), rider=("scatter", q_gu))
    r_dn = _pair_exchange(g_dn)
    q_dn = _pair_add(g_dn, r_dn, 2)
    small_late = _pack_small({"ffn1_pre_g": dg1pre, "ffn1_post_g": dg1post}, _SMALL_LATE)
    t_dn, sm_late = _chip_exchange(q_dn, small_late)

    halves = [_shard_sum(g_gu, r_gu, t_gu, 2), _shard_sum(g_dn, r_dn, t_dn, 2),
              _shard_sum(g_ffn2, r_ffn2, t_ffn2, 3), _shard_sum(g_mix, r_mix, t_mix, 3)]
    f_gu, f_dn, f_ffn2, f_mix = _pair_gather(halves)
    gsm = jnp.concatenate([_small_sum(sm_early), _small_sum(sm_late)], axis=0)
    return loss, dx, jnp.concatenate([f_gu, f_dn], axis=0), f_ffn2, f_mix, gsm


_SMALL_EARLY = ["mix_pre_g", "mix_post_g", "ffn2_pre_g", "ffn2_post_g", "sgu_ln_g", "sgu_ln_b", "sgu_b_s", "b_forget",
                "sgu_w_s"]
_SMALL_LATE = ["ffn1_pre_g", "ffn1_post_g"]
_SMALL_NAMES = _SMALL_EARLY + _SMALL_LATE


def _pack_small(d, names=None):
    rows = []
    for n in names or _SMALL_NAMES:
        a = d[n].astype(F32)
        if n == "b_forget":
            a = _pad_cols(a, D)
        a = a.reshape(-1, D)
        rows.append(jnp.pad(a, ((0, -a.shape[0] % SMALL_STRIDE), (0, 0))))
    return jnp.concatenate(rows, axis=0)


def _unpack_small(p):
    out, r = {}, 0
    for n in _SMALL_NAMES:
        if n == "sgu_w_s":
            out[n] = p[r:r + L].reshape(1, G, L, L)
            r += L
        elif n == "b_forget":
            out[n] = p[r:r + 1, :H]
            r += SMALL_STRIDE
        elif n == "sgu_b_s":
            out[n] = p[r:r + 1].reshape(1, G, L)
            r += SMALL_STRIDE
        else:
            out[n] = p[r:r + 1]
            r += SMALL_STRIDE
    return out


_BIG_NAMES = ["ffn1_w_gate", "ffn1_w_up", "ffn1_w_down", "ffn2_w_gate", "ffn2_w_up", "ffn2_w_down", "w_out", "w_in"]
_WEIGHT_ORDER = ['ffn1_pre_g', 'ffn1_w_gate', 'ffn1_w_up', 'ffn1_w_down', 'ffn1_post_g', 'mix_pre_g', 'w_in', 'b_forget',
                 'sgu_ln_g', 'sgu_ln_b', 'sgu_w_s', 'sgu_b_s', 'w_out', 'mix_post_g', 'ffn2_pre_g', 'ffn2_w_gate',
                 'ffn2_w_up', 'ffn2_w_down', 'ffn2_post_g']


def _pack_ffn(w, name):
    return jnp.concatenate([w[name + "_w_gate"][0], w[name + "_w_up"][0], w[name + "_w_down"][0]], axis=0)


def _pack_mix(w):
    w_in = w["w_in"][0]
    return jnp.concatenate([w["w_out"][0], w_in[:, :D], _pad_cols(w_in[:, D:], D)], axis=0)


def _unpack_ffn(p, name):
    return {name + "_w_gate": p[:D][None], name + "_w_up": p[D:2 * D][None], name + "_w_down": p[2 * D:][None]}


def _unpack_mix(p):
    return {"w_out": p[:256][None],
            "w_in": jnp.concatenate([p[256:256 + D], p[256 + D:, :WIN_SH - D]], axis=1)[None]}


def _step(args, tm, t_attn):
    x = args["x"][0]
    target = args["loss_target"][0]
    weights = {n: args[n] for n in _WEIGHT_ORDER}
    small = {n: weights[n] for n in _SMALL_NAMES}

    wb = {n: weights[n].astype(BF) for n in _BIG_NAMES}
    wp1 = _pack_ffn(wb, "ffn1")
    wp2 = jnp.concatenate([_pack_ffn(wb, "ffn2"), _pack_mix(wb)], axis=0)
    loss_local, dx, f_ffn1, f_ffn2, f_mix, gsm = _train_step(x, target, wp1, wp2, small, tm, t_attn)
    loss = lax.psum(loss_local, ("x", "y", "c"))
    grads = {**_unpack_ffn(f_ffn1, "ffn1"), **_unpack_ffn(f_ffn2, "ffn2"), **_unpack_mix(f_mix),
             **_unpack_small(gsm)}

    delta, new_m, new_v = {}, {}, {}
    for n in _BIG_NAMES:
        shp = weights[n].shape
        w2 = weights[n].reshape(-1, shp[-1])
        rows = w2.shape[0]
        d, nm, nv = _adamw(w2, grads[n].reshape(w2.shape), args["m_" + n].reshape(w2.shape),
                           args["v_" + n].reshape(w2.shape), rows // 4)
        delta[n], new_m[n], new_v[n] = d.reshape(shp), nm.reshape(shp), nv.reshape(shp)
    ds, nms, nvs = _adamw(_pack_small(small), gsm, _pack_small({n: args["m_" + n] for n in _SMALL_NAMES}),
                          _pack_small({n: args["v_" + n] for n in _SMALL_NAMES}), SMALL_ROWS)
    delta.update(_unpack_small(ds))
    new_m.update(_unpack_small(nms))
    new_v.update(_unpack_small(nvs))

    return (loss, dx[None], *[grads[n] for n in _WEIGHT_ORDER], *[delta[n] for n in _WEIGHT_ORDER],
            *[new_m[n] for n in _WEIGHT_ORDER], *[new_v[n] for n in _WEIGHT_ORDER])


_ARG_NAMES = (["x"] + _WEIGHT_ORDER + ["loss_target"] + ["m_" + n for n in _WEIGHT_ORDER]
              + ["v_" + n for n in _WEIGHT_ORDER])


def kernel(x, ffn1_pre_g, ffn1_w_gate, ffn1_w_up, ffn1_w_down, ffn1_post_g, mix_pre_g, w_in, b_forget, sgu_ln_g, sgu_ln_b, sgu_w_s, sgu_b_s, w_out, mix_post_g, ffn2_pre_g, ffn2_w_gate, ffn2_w_up, ffn2_w_down, ffn2_post_g, loss_target, m_ffn1_pre_g, m_ffn1_w_gate, m_ffn1_w_up, m_ffn1_w_down, m_ffn1_post_g, m_mix_pre_g, m_w_in, m_b_forget, m_sgu_ln_g, m_sgu_ln_b, m_sgu_w_s, m_sgu_b_s, m_w_out, m_mix_post_g, m_ffn2_pre_g, m_ffn2_w_gate, m_ffn2_w_up, m_ffn2_w_down, m_ffn2_post_g, v_ffn1_pre_g, v_ffn1_w_gate, v_ffn1_w_up, v_ffn1_w_down, v_ffn1_post_g, v_mix_pre_g, v_w_in, v_b_forget, v_sgu_ln_g, v_sgu_ln_b, v_sgu_w_s, v_sgu_b_s, v_w_out, v_mix_post_g, v_ffn2_pre_g, v_ffn2_w_gate, v_ffn2_w_up, v_ffn2_w_down, v_ffn2_post_g):
    args = (x, ffn1_pre_g, ffn1_w_gate, ffn1_w_up, ffn1_w_down, ffn1_post_g, mix_pre_g, w_in, b_forget, sgu_ln_g, sgu_ln_b, sgu_w_s, sgu_b_s, w_out, mix_post_g, ffn2_pre_g, ffn2_w_gate, ffn2_w_up, ffn2_w_down, ffn2_post_g, loss_target, m_ffn1_pre_g, m_ffn1_w_gate, m_ffn1_w_up, m_ffn1_w_down, m_ffn1_post_g, m_mix_pre_g, m_w_in, m_b_forget, m_sgu_ln_g, m_sgu_ln_b, m_sgu_w_s, m_sgu_b_s, m_w_out, m_mix_post_g, m_ffn2_pre_g, m_ffn2_w_gate, m_ffn2_w_up, m_ffn2_w_down, m_ffn2_post_g, v_ffn1_pre_g, v_ffn1_w_gate, v_ffn1_w_up, v_ffn1_w_down, v_ffn1_post_g, v_mix_pre_g, v_w_in, v_b_forget, v_sgu_ln_g, v_sgu_ln_b, v_sgu_w_s, v_sgu_b_s, v_w_out, v_mix_post_g, v_ffn2_pre_g, v_ffn2_w_gate, v_ffn2_w_up, v_ffn2_w_down, v_ffn2_post_g)
    named = dict(zip(_ARG_NAMES, args))
    tile = min(512, x.shape[1])
    return _step(named, tile, tile)
```

```python
import functools
import math

import jax
import jax.numpy as jnp
from jax import lax
from jax.experimental import pallas as pl
from jax.experimental.pallas import tpu as pltpu

D = 1024
F = 4096
H = 8
HD = 128
G = 8
L = 128
CHUNK = 64
NSH = 4
NDEV = 8
ZW = 7 * D
RMS_EPS = 1e-6
LN_EPS = 1e-5
NEG = -1e30
SCALE = 1.0 / math.sqrt(HD)
LOG2E = math.log2(math.e)
LN2 = math.log(2.0)

ADAM_LR = 0.001
ADAM_B1 = 0.9
ADAM_B2 = 0.999
ADAM_EPS = 1e-08
ADAM_WD = 0.01
ADAM_STEP = 10

VMEM_LIMIT_BYTES = 56 * 1024 * 1024

WIN_SH = 1794
FFN_ROWS = 3 * D
MIX_ROWS = 256 + 2 * D
G2_ROWS = FFN_ROWS + MIX_ROWS
SMALL_STRIDE = 8
SMALL_ROWS = 10 * SMALL_STRIDE + L

BF = jnp.bfloat16
F32 = jnp.float32
MESH = pl.DeviceIdType.MESH


def _params(n_grid):
    return pltpu.CompilerParams(dimension_semantics=("arbitrary",) * n_grid,
                                vmem_limit_bytes=VMEM_LIMIT_BYTES)


def _dot(a, b):
    return jnp.dot(a, b, preferred_element_type=F32)


def _dot_nt(a, b):
    return lax.dot_general(a, b, (((1,), (1,)), ((), ())), preferred_element_type=F32)


def _dot_tn(a, b):
    return lax.dot_general(a, b, (((0,), (0,)), ((), ())), preferred_element_type=F32)


def _rms(x, g):
    r = lax.rsqrt(jnp.mean(x * x, axis=-1, keepdims=True) + RMS_EPS)
    return x * r * g


def _rms_bwd(dn, x, g):
    r = lax.rsqrt(jnp.mean(x * x, axis=-1, keepdims=True) + RMS_EPS)
    xr = x * r
    dg = jnp.sum(dn * xr, axis=0, keepdims=True)
    t = dn * g
    dx = r * (t - xr * jnp.mean(t * xr, axis=-1, keepdims=True))
    return dx, dg


def _gelu_parts(x):
    cdf = 0.5 * (1.0 + lax.erf(x * (1.0 / math.sqrt(2.0))))
    pdf = jnp.exp(-0.5 * x * x) * (1.0 / math.sqrt(2.0 * math.pi))
    return x * cdf, cdf + x * pdf


def _gelu(x):
    return x * (0.5 * (1.0 + lax.erf(x * (1.0 / math.sqrt(2.0)))))


def _sigmoid(x):
    return 1.0 / (1.0 + jnp.exp(-x))


def _ffn_fwd(x, g_pre, wpack, g_post, tm, tail, gather=None):
    S = x.shape[0]
    nt, nf, tf = S // tm, NSH, D
    n_tail_out = 1 if tail[0] == "norm" else 2

    def body(x_ref, gpre_ref, wg_ref, wu_ref, wd_ref, gpost_ref, tail_ref, *rest):
        if gather is not None:
            wp_ref, rest = rest[0], rest[1:]
        h_ref, a_ref, b_ref, y_ref, xo_ref = rest[:5]
        tail_out = rest[5:5 + n_tail_out]
        rest = rest[5 + n_tail_out:]
        if gather is not None:
            g_ref, h_s, acc, send_sems, recv_sems = rest
        else:
            h_s, acc = rest
        i = pl.program_id(0)
        j = pl.program_id(1)

        if gather is not None:
            @pl.when((i == 0) & (j == 0))
            def _():
                _gather_start(wp_ref, g_ref, send_sems, recv_sems)

        @pl.when(j == 0)
        def _():
            h = _rms(x_ref[...], gpre_ref[...]).astype(BF)
            h_s[...] = h
            h_ref[...] = h
            acc[...] = jnp.zeros_like(acc)

        h = h_s[...]
        a = _dot(h, wg_ref[...])
        b = _dot(h, wu_ref[...])
        a_ref[...] = a.astype(BF)
        b_ref[...] = b.astype(BF)
        act = (a * _sigmoid(a)) * b
        acc[...] += _dot(act.astype(BF), wd_ref[...])

        if tail[0] == "loss":
            @pl.when((i == 0) & (j == 0))
            def _():
                tail_out[1][...] = jnp.zeros_like(tail_out[1])

        @pl.when(j == nf - 1)
        def _():
            y = acc[...]
            y_ref[...] = y
            xo = x_ref[...] + 0.5 * _rms(y, gpost_ref[...])
            xo_ref[...] = xo
            if tail[0] == "norm":
                tail_out[0][...] = _rms(xo, tail_ref[...]).astype(BF)
            else:
                e = xo - tail_ref[...]
                tail_out[0][...] = e * (1.0 / D)
                tail_out[1][...] += jnp.sum(e * e) * (0.5 / D)

        if gather is not None:
            @pl.when((i == max(nt - 2, 0)) & (j == 0))
            def _():
                _gather_forward(wp_ref, g_ref, send_sems, recv_sems)

            @pl.when((i == nt - 1) & (j == nf - 1))
            def _():
                _gather_finish(wp_ref, g_ref, send_sems, recv_sems)

    row = pl.BlockSpec((tm, D), lambda i, j: (i, 0))
    vec = pl.BlockSpec((1, D), lambda i, j: (0, 0))
    anywhere = pl.BlockSpec(memory_space=pl.ANY)
    in_specs = [row, vec,
                pl.BlockSpec((None, D, tf), lambda i, j: (j, 0, 0)),
                pl.BlockSpec((None, D, tf), lambda i, j: (j, 1, 0)),
                pl.BlockSpec((None, tf, D), lambda i, j: (j, 2, 0)),
                vec, vec if tail[0] == "norm" else row]
    out_specs = [row,
                 pl.BlockSpec((tm, tf), lambda i, j: (i, j)),
                 pl.BlockSpec((tm, tf), lambda i, j: (i, j)),
                 row, row]
    out_shape = [jax.ShapeDtypeStruct((S, D), BF),
                 jax.ShapeDtypeStruct((S, F), BF),
                 jax.ShapeDtypeStruct((S, F), BF),
                 jax.ShapeDtypeStruct((S, D), F32),
                 jax.ShapeDtypeStruct((S, D), F32)]
    if tail[0] == "norm":
        out_specs.append(row)
        out_shape.append(jax.ShapeDtypeStruct((S, D), BF))
    else:
        out_specs += [row, pl.BlockSpec((8, HD), lambda i, j: (0, 0))]
        out_shape += [jax.ShapeDtypeStruct((S, D), F32), jax.ShapeDtypeStruct((8, HD), F32)]
    scratch = [pltpu.VMEM((tm, D), BF), pltpu.VMEM((tm, D), F32)]
    args = [x, g_pre, wpack, wpack, wpack, g_post, tail[1]]
    if gather is not None:
        in_specs.append(anywhere)
        out_specs.append(anywhere)
        out_shape.append(jax.ShapeDtypeStruct((NSH,) + gather.shape, gather.dtype))
        scratch += [pltpu.SemaphoreType.DMA((6,)), pltpu.SemaphoreType.DMA((6,))]
        args.append(gather)
    res = list(pl.pallas_call(
        body, name="ffn_fwd" if gather is None else "ffn_fwd_gather",
        grid=(nt, nf),
        in_specs=in_specs, out_specs=out_specs, out_shape=out_shape, scratch_shapes=scratch,
        compiler_params=_params(2),
    )(*args))
    if gather is not None:
        res[-1] = _place_own_shard(res[-1], gather)
    return res


def _ffn_bwd(dxo, y, g_post, a, b, wpack, x_in, g_pre, tm, riders=()):
    S = dxo.shape[0]
    nt, nf, tf = S // tm, NSH, D
    nr = len(riders)

    def body(dxo_ref, y_ref, gpost_ref, a_ref, b_ref, wg_ref, wu_ref, wd_ref, xin_ref, gpre_ref, *rest):
        rsrc = rest[:nr]
        dy_ref, da_ref, db_ref, act_ref, dxin_ref, dgpost_ref, dgpre_ref = rest[nr:nr + 7]
        rdst = rest[nr + 7:2 * nr + 7]
        dy_s, acc = rest[2 * nr + 7:2 * nr + 9]
        sems = rest[2 * nr + 9:]
        i = pl.program_id(0)
        j = pl.program_id(1)

        @pl.when((i == 0) & (j == 0))
        def _():
            dgpost_ref[...] = jnp.zeros_like(dgpost_ref)
            dgpre_ref[...] = jnp.zeros_like(dgpre_ref)
            for k, (kind, _) in enumerate(riders):
                _rider_start(kind, rsrc[k], rdst[k], sems[2 * k], sems[2 * k + 1])

        @pl.when(j == 0)
        def _():
            dy, dg = _rms_bwd(0.5 * dxo_ref[...], y_ref[...], gpost_ref[...])
            dyb = dy.astype(BF)
            dy_s[...] = dyb
            dy_ref[...] = dyb
            dgpost_ref[...] += dg
            acc[...] = jnp.zeros_like(acc)

        dact = _dot_nt(dy_s[...], wd_ref[...])
        av = a_ref[...].astype(F32)
        bv = b_ref[...].astype(F32)
        sig = _sigmoid(av)
        sl = av * sig
        act_ref[...] = (sl * bv).astype(BF)
        dbb = (dact * sl).astype(BF)
        dab = (dact * bv * (sig * (1.0 + av * (1.0 - sig)))).astype(BF)
        da_ref[...] = dab
        db_ref[...] = dbb
        acc[...] += _dot_nt(dab, wg_ref[...]) + _dot_nt(dbb, wu_ref[...])

        @pl.when(j == nf - 1)
        def _():
            dx, dg = _rms_bwd(acc[...], xin_ref[...], gpre_ref[...])
            dxin_ref[...] = dxo_ref[...] + dx
            dgpre_ref[...] += dg

        if riders:
            @pl.when((i == nt - 1) & (j == nf - 1))
            def _():
                for k, (kind, _) in enumerate(riders):
                    _rider_finish(kind, rsrc[k], rdst[k], sems[2 * k], sems[2 * k + 1])

    row = pl.BlockSpec((tm, D), lambda i, j: (i, 0))
    vec = pl.BlockSpec((1, D), lambda i, j: (0, 0))
    ff = pl.BlockSpec((tm, tf), lambda i, j: (i, j))
    anywhere = pl.BlockSpec(memory_space=pl.ANY)
    in_specs = [row, row, vec, ff, ff,
                pl.BlockSpec((None, D, tf), lambda i, j: (j, 0, 0)),
                pl.BlockSpec((None, D, tf), lambda i, j: (j, 1, 0)),
                pl.BlockSpec((None, tf, D), lambda i, j: (j, 2, 0)),
                row, vec]
    out_specs = [row, ff, ff, ff, row, vec, vec]
    out_shape = [jax.ShapeDtypeStruct((S, D), BF),
                 jax.ShapeDtypeStruct((S, F), BF),
                 jax.ShapeDtypeStruct((S, F), BF),
                 jax.ShapeDtypeStruct((S, F), BF),
                 jax.ShapeDtypeStruct((S, D), F32),
                 jax.ShapeDtypeStruct((1, D), F32),
                 jax.ShapeDtypeStruct((1, D), F32)]
    scratch = [pltpu.VMEM((tm, D), BF), pltpu.VMEM((tm, D), F32)]
    args = [dxo, y, g_post, a, b, wpack, wpack, wpack, x_in, g_pre]
    for rider in riders:
        in_specs.append(anywhere)
        out_specs.append(anywhere)
        out_shape.append(_rider_out(rider))
        scratch += _rider_sems(rider)
        args.append(rider[1])
    return pl.pallas_call(
        body, name="ffn_bwd" if not riders else "ffn_bwd_riders",
        grid=(nt, nf),
        in_specs=in_specs, out_specs=out_specs, out_shape=out_shape, scratch_shapes=scratch,
        compiler_params=_params(2),
    )(*args)


def _mm_tn(a, b, bm, bn, bt, into=None, rider=None):
    S, M = a.shape
    N = b.shape[1]
    nt = S // bt
    n_in = 2 + (into is not None and into[0] is not None) + (rider is not None)

    def body(*refs):
        a_ref, b_ref, o_ref = refs[0], refs[1], refs[n_in]
        m, n, t = pl.program_id(0), pl.program_id(1), pl.program_id(2)

        if rider is not None:
            rsrc_ref, rdst_ref, send_sems, recv_sems = refs[n_in - 1], refs[n_in + 1], refs[-2], refs[-1]

            @pl.when((m == 0) & (n == 0) & (t == 0))
            def _():
                _rider_start(rider[0], rsrc_ref, rdst_ref, send_sems, recv_sems)

        @pl.when(t == 0)
        def _():
            o_ref[...] = jnp.zeros_like(o_ref)

        o_ref[...] += _dot_tn(a_ref[...], b_ref[...])

        if rider is not None:
            @pl.when((m == M // bm - 1) & (n == N // bn - 1) & (t == nt - 1))
            def _():
                _rider_finish(rider[0], rsrc_ref, rdst_ref, send_sems, recv_sems)

    in_specs = [pl.BlockSpec((bt, bm), lambda m, n, t: (t, m)),
                pl.BlockSpec((bt, bn), lambda m, n, t: (t, n))]
    args, aliases = [a, b], {}
    if into is None:
        out_spec = pl.BlockSpec((bm, bn), lambda m, n, t: (m, n))
        out_shape = jax.ShapeDtypeStruct((M, N), F32)
    else:
        buf, rows, rb, by = into
        assert bn == D and (M == bm if by == "cols" else (M == NSH * bm and N == D))
        if by == "cols":
            out_spec = pl.BlockSpec((None, bm, bn), lambda m, n, t: (n, rb, 0))
        else:
            out_spec = pl.BlockSpec((None, bm, bn), lambda m, n, t: (m, rb, 0))
        out_shape = jax.ShapeDtypeStruct((NSH, rows, D), F32)
        if buf is not None:
            in_specs.append(pl.BlockSpec(memory_space=pl.ANY))
            args.append(buf)
            aliases = {2: 0}
    if rider is None:
        return pl.pallas_call(
            body, name="mm_tn",
            grid=(M // bm, N // bn, nt),
            in_specs=in_specs, out_specs=out_spec, out_shape=out_shape,
            input_output_aliases=aliases,
            compiler_params=_params(3),
        )(*args)
    anywhere = pl.BlockSpec(memory_space=pl.ANY)
    return pl.pallas_call(
        body, name="mm_tn_rider",
        grid=(M // bm, N // bn, nt),
        in_specs=in_specs + [anywhere], out_specs=[out_spec, anywhere], out_shape=[out_shape, _rider_out(rider)],
        scratch_shapes=_rider_sems(rider),
        input_output_aliases=aliases,
        compiler_params=_params(3),
    )(*args, rider[1])


def _mm(a, w, tm, tn, out_dtype, first_block_scale=1.0):
    S, K = a.shape
    N = w.shape[1]

    def body(a_ref, w_ref, o_ref):
        r = _dot(a_ref[...], w_ref[...])
        if first_block_scale != 1.0:
            r = r * jnp.where(pl.program_id(1) == 0, first_block_scale, 1.0)
        o_ref[...] = r.astype(out_dtype)

    return pl.pallas_call(
        body, name="mm",
        grid=(S // tm, N // tn),
        in_specs=[pl.BlockSpec((tm, K), lambda i, j: (i, 0)),
                  pl.BlockSpec((K, tn), lambda i, j: (0, j))],
        out_specs=pl.BlockSpec((tm, tn), lambda i, j: (i, j)),
        out_shape=jax.ShapeDtypeStruct((S, N), out_dtype),
        compiler_params=_params(2),
    )(a, w)


def _mix_in_bwd(dz, wcat, dzf, wf, x1, g, dx2, tm, rider):
    S = dz.shape[0]
    nk = 2
    kb = ZW // nk

    def body(dz_ref, w_ref, dzf_ref, wf_ref, x_ref, g_ref, dx2_ref, rsrc_ref, dx1_ref, dg_ref, rdst_ref,
             acc, send_sems, recv_sems):
        i = pl.program_id(0)
        k = pl.program_id(1)

        @pl.when((i == 0) & (k == 0))
        def _():
            _rider_start(rider[0], rsrc_ref, rdst_ref, send_sems, recv_sems)
            dg_ref[...] = jnp.zeros_like(dg_ref)

        @pl.when(k == 0)
        def _():
            acc[...] = _dot_nt(dzf_ref[...], wf_ref[...])

        acc[...] += _dot_nt(dz_ref[...], w_ref[...])

        @pl.when(k == nk - 1)
        def _():
            dx, dg = _rms_bwd(acc[...], x_ref[...], g_ref[...])
            dx1_ref[...] = dx2_ref[...] + dx
            dg_ref[...] += dg

        @pl.when((i == S // tm - 1) & (k == nk - 1))
        def _():
            _rider_finish(rider[0], rsrc_ref, rdst_ref, send_sems, recv_sems)

    row = pl.BlockSpec((tm, D), lambda i, k: (i, 0))
    vec = pl.BlockSpec((1, D), lambda i, k: (0, 0))
    anywhere = pl.BlockSpec(memory_space=pl.ANY)
    return pl.pallas_call(
        body, name="mix_in_bwd",
        grid=(S // tm, nk),
        in_specs=[pl.BlockSpec((tm, kb), lambda i, k: (i, k)),
                  pl.BlockSpec((D, kb), lambda i, k: (0, k)),
                  pl.BlockSpec((tm, HD), lambda i, k: (i, 0)),
                  pl.BlockSpec((D, HD), lambda i, k: (0, 0)),
                  row, vec, row, anywhere],
        out_specs=[row, vec, anywhere],
        out_shape=[jax.ShapeDtypeStruct((S, D), F32), jax.ShapeDtypeStruct((1, D), F32), _rider_out(rider)],
        scratch_shapes=[pltpu.VMEM((tm, D), F32)] + _rider_sems(rider),
        compiler_params=_params(2),
    )(dz, wcat, dzf, wf, x1, g, dx2, rider[1])


def _scan_rows(blk, reverse):
    n = blk.shape[0]
    row = lax.broadcasted_iota(jnp.int32, blk.shape, 0)
    d = 1
    while d < n:
        if reverse:
            blk = blk + jnp.where(row < n - d, pltpu.roll(blk, n - d, 0), 0.0)
        else:
            blk = blk + jnp.where(row >= d, pltpu.roll(blk, d, 0), 0.0)
        d *= 2
    return blk


def _forget_cumsum(zf, bf, cs):
    S = zf.shape[0]

    def body(zf_ref, bf_ref, c_ref, cb_ref, carry):
        @pl.when(pl.program_id(0) == 0)
        def _():
            carry[...] = jnp.zeros_like(carry)

        x = zf_ref[...] + bf_ref[...]
        logf = jnp.minimum(x, 0.0) - jnp.log1p(jnp.exp(-jnp.abs(x)))
        sc = _scan_rows(logf, False) + carry[...]
        carry[...] = sc[cs - 1:cs, :]
        sc = sc * LOG2E
        c_ref[...] = sc
        for h in range(H):
            cb_ref[h] = jnp.broadcast_to(sc[:, h:h + 1], (cs, HD))

    return pl.pallas_call(
        body, name="forget_cumsum",
        grid=(S // cs,),
        in_specs=[pl.BlockSpec((cs, HD), lambda i: (i, 0)), pl.BlockSpec((1, HD), lambda i: (0, 0))],
        out_specs=[pl.BlockSpec((cs, HD), lambda i: (i, 0)),
                   pl.BlockSpec((H, cs, HD), lambda i: (0, i, 0))],
        out_shape=[jax.ShapeDtypeStruct((S, HD), F32), jax.ShapeDtypeStruct((H, S, HD), F32)],
        scratch_shapes=[pltpu.VMEM((1, HD), F32)],
        compiler_params=_params(1),
    )(zf, bf)


def _forget_bwd(dc, zf, bf, cs):
    S = dc.shape[0]
    nc = S // cs

    def body(dc_ref, zf_ref, bf_ref, dzf_ref, dbf_ref, carry):
        @pl.when(pl.program_id(0) == 0)
        def _():
            carry[...] = jnp.zeros_like(carry)
            dbf_ref[...] = jnp.zeros_like(dbf_ref)

        sc = _scan_rows(dc_ref[...], True) + carry[...]
        carry[...] = sc[0:1, :]
        x = zf_ref[...] + bf_ref[...]
        dz = sc * _sigmoid(-x)
        dzf_ref[...] = dz.astype(BF)
        dbf_ref[...] += jnp.sum(dz, axis=0, keepdims=True)

    rev = pl.BlockSpec((cs, HD), lambda i: (nc - 1 - i, 0))
    vec = pl.BlockSpec((1, HD), lambda i: (0, 0))
    return pl.pallas_call(
        body, name="forget_bwd",
        grid=(nc,),
        in_specs=[rev, rev, vec],
        out_specs=[rev, vec],
        out_shape=[jax.ShapeDtypeStruct((S, HD), BF), jax.ShapeDtypeStruct((1, HD), F32)],
        scratch_shapes=[pltpu.VMEM((1, HD), F32)],
        compiler_params=_params(1),
    )(dc, zf, bf)


def _lanes(x, n):
    return x if n == HD else jnp.concatenate([x] * (n // HD), axis=1)


def _causal_mask(i, j, t, rows_are_queries):
    r = lax.broadcasted_iota(jnp.int32, (t, t), 0)
    c = lax.broadcasted_iota(jnp.int32, (t, t), 1)
    if rows_are_queries:
        return (j * t + c) <= (i * t + r)
    return (j * t + r) <= (i * t + c)


def _fox_fwd(z, ccol_b, crow, t):
    S = z.shape[0]
    nq = S // t

    def body(q_ref, kv_ref, cc_ref, cr_ref, o_ref, lse_ref, m_s, acc_s, s_a, s_b):
        i = pl.program_id(1)
        ct = cc_ref[...]
        ones = jnp.ones((t, HD), BF)
        m_s[...] = jnp.full_like(m_s, NEG)
        acc_s[...] = jnp.zeros_like(acc_s)

        def scores(j, s_ref):
            off = pl.multiple_of(j * t, t)
            s_ref[...] = _dot_nt(q_ref[...], kv_ref[pl.ds(off, t), :HD]) - cr_ref[pl.ds(j, 1), :]

        def consume(j, s_ref, masked):
            off = pl.multiple_of(j * t, t)
            v1 = jnp.concatenate([kv_ref[pl.ds(off, t), HD:], ones], axis=1)
            s = s_ref[...]
            if masked:
                s = jnp.where(_causal_mask(i, j, t, True), s, NEG)
            m_old = m_s[...]
            m_new = jnp.maximum(m_old, jnp.max(s, axis=1, keepdims=True))
            p = jnp.exp2(s - _lanes(m_new, t))
            alpha = jnp.exp2(m_old - m_new)
            acc_s[...] = _lanes(alpha, 2 * HD) * acc_s[...] + _dot(p.astype(BF), v1)
            m_s[...] = m_new

        scores(0, s_a)

        def pair(jj, carry):
            j = 2 * jj
            scores(j + 1, s_b)
            consume(j, s_a, False)
            scores(j + 2, s_a)
            consume(j + 1, s_b, False)
            return carry

        lax.fori_loop(0, i // 2, pair, 0)

        @pl.when(i % 2 == 0)
        def _():
            consume(i, s_a, True)

        @pl.when(i % 2 == 1)
        def _():
            scores(i, s_b)
            consume(i - 1, s_a, False)
            consume(i, s_b, True)

        l = acc_s[:, HD:]
        o_ref[...] = (acc_s[:, :HD] / l).astype(BF)
        lse_ref[...] = m_s[...] + ct + jnp.log2(l)

    return pl.pallas_call(
        body, name="fox_fwd",
        grid=(H, nq),
        in_specs=[pl.BlockSpec((t, HD), lambda h, i: (i, h)),
                  pl.BlockSpec((S, 2 * HD), lambda h, i: (0, 4 + h)),
                  pl.BlockSpec((None, t, HD), lambda h, i: (h, i, 0)),
                  pl.BlockSpec((None, nq, t), lambda h, i: (h, 0, 0))],
        out_specs=[pl.BlockSpec((t, HD), lambda h, i: (i, h)),
                   pl.BlockSpec((None, t, HD), lambda h, i: (h, i, 0))],
        out_shape=[jax.ShapeDtypeStruct((S, D), BF), jax.ShapeDtypeStruct((H, S, HD), F32)],
        scratch_shapes=[pltpu.VMEM((t, HD), F32), pltpu.VMEM((t, 2 * HD), F32),
                        pltpu.VMEM((t, t), F32), pltpu.VMEM((t, t), F32)],
        compiler_params=_params(2),
    )(z, z, ccol_b, crow)


def _fox_bwd(z, do, o, lse_b, ccol_b, crow, dz, t, rider):
    S = z.shape[0]
    nq = S // t

    def body(q_ref, kv_ref, do_ref, o_ref, lse_ref, cc_ref, cr_ref, dz_in, rsrc_ref,
             dq_ref, dkt_ref, dvt_ref, dck_ref, dcq_ref, rdst_ref, acc_s, r_s, send_sems, recv_sems):
        del dz_in
        i = pl.program_id(1)

        @pl.when((pl.program_id(0) == 0) & (i == 0))
        def _():
            _rider_start(rider[0], rsrc_ref, rdst_ref, send_sems, recv_sems)

        @pl.when(i == 0)
        def _():
            dkt_ref[...] = jnp.zeros_like(dkt_ref)
            dvt_ref[...] = jnp.zeros_like(dvt_ref)
            dck_ref[...] = jnp.zeros_like(dck_ref)

        q = q_ref[...]
        dout = do_ref[...]
        qt = jnp.transpose(q.astype(F32)).astype(BF)
        dot_ = jnp.transpose(dout.astype(F32)).astype(BF)
        off_t = _lanes(lse_ref[...] - cc_ref[...], t)
        delta = jnp.sum(dout.astype(F32) * o_ref[...].astype(F32), axis=1, keepdims=True)
        delta = _lanes(jnp.broadcast_to(delta, (t, HD)), t)
        acc_s[...] = jnp.zeros_like(acc_s)
        r_s[...] = jnp.zeros_like(r_s)

        def step(j, masked):
            off = pl.multiple_of(j * t, t)
            k = kv_ref[pl.ds(off, t), :HD]
            v = kv_ref[pl.ds(off, t), HD:]
            p = jnp.exp2(_dot_nt(q, k) - cr_ref[pl.ds(j, 1), :] - off_t)
            if masked:
                p = jnp.where(_causal_mask(i, j, t, True), p, 0.0)
            ds = p * (_dot_nt(dout, v) - delta)
            dsb = ds.astype(BF)
            acc_s[...] += _dot(dsb, k)
            dkt_ref[j] += _dot(qt, dsb)
            dvt_ref[j] += _dot(dot_, p.astype(BF))
            dck_ref[pl.ds(j, 1), :] += jnp.sum(ds, axis=0, keepdims=True)
            r_s[...] += jnp.sum(ds, axis=1, keepdims=True)

        def full_step(j, carry):
            step(j, False)
            return carry

        lax.fori_loop(0, i, full_step, 0)
        step(i, True)
        dq_ref[...] = (acc_s[...] * SCALE).astype(BF)
        dcq_ref[...] = jnp.transpose(r_s[...])[0:1, :]

        @pl.when((pl.program_id(0) == H - 1) & (i == nq - 1))
        def _():
            _rider_finish(rider[0], rsrc_ref, rdst_ref, send_sems, recv_sems)

    qspec = pl.BlockSpec((t, HD), lambda h, i: (i, h))
    bspec = pl.BlockSpec((None, t, HD), lambda h, i: (h, i, 0))
    rows = pl.BlockSpec((None, nq, t), lambda h, i: (h, 0, 0))
    tspec = pl.BlockSpec((None, nq, HD, t), lambda h, i: (h, 0, 0, 0))
    tshape = jax.ShapeDtypeStruct((H, nq, HD, t), F32)
    anywhere = pl.BlockSpec(memory_space=pl.ANY)
    return pl.pallas_call(
        body, name="fox_bwd",
        grid=(H, nq),
        in_specs=[qspec,
                  pl.BlockSpec((S, 2 * HD), lambda h, i: (0, 4 + h)),
                  qspec, qspec, bspec, bspec, rows, anywhere, anywhere],
        out_specs=[qspec, tspec, tspec, rows, pl.BlockSpec((None, None, 1, t), lambda h, i: (h, i, 0, 0)),
                   anywhere],
        out_shape=[jax.ShapeDtypeStruct((S, ZW), BF), tshape, tshape,
                   jax.ShapeDtypeStruct((H, nq, t), F32), jax.ShapeDtypeStruct((H, nq, 1, t), F32),
                   _rider_out(rider)],
        scratch_shapes=[pltpu.VMEM((t, HD), F32), pltpu.VMEM((t, HD), F32)] + _rider_sems(rider),
        input_output_aliases={7: 0},
        compiler_params=_params(2),
    )(z, z, do, o, lse_b, ccol_b, crow, dz, rider[1])


def _fox_bwd_finish(dkt, dvt, dz, t):
    nq = dkt.shape[1]
    S = nq * t

    def body(dkt_ref, dvt_ref, dz_in, dkv_ref):
        del dz_in
        for j in range(nq):
            rows = slice(j * t, (j + 1) * t)
            dkv_ref[rows, :HD] = (jnp.transpose(dkt_ref[j]) * LN2).astype(BF)
            dkv_ref[rows, HD:] = jnp.transpose(dvt_ref[j]).astype(BF)

    tspec = pl.BlockSpec((None, nq, HD, t), lambda h: (h, 0, 0, 0))
    return pl.pallas_call(
        body, name="fox_bwd_finish",
        grid=(H,),
        in_specs=[tspec, tspec, pl.BlockSpec(memory_space=pl.ANY)],
        out_specs=pl.BlockSpec((S, 2 * HD), lambda h: (0, 4 + h)),
        out_shape=jax.ShapeDtypeStruct((S, ZW), BF),
        input_output_aliases={2: 0},
        compiler_params=_params(1),
    )(dkt, dvt, dz)


def _sgu_mask(transposed):
    r = lax.broadcasted_iota(jnp.int32, (L, L), 0)
    c = lax.broadcasted_iota(jnp.int32, (L, L), 1)
    if transposed:
        return (r // CHUNK) <= (c // CHUNK)
    return (c // CHUNK) <= (r // CHUNK)


def _ln_group(vs, lng, lnb):
    mu = jnp.mean(vs, axis=-1, keepdims=True)
    xc = vs - mu
    rstd = lax.rsqrt(jnp.mean(xc * xc, axis=-1, keepdims=True) + LN_EPS)
    xhat = xc * rstd
    return xhat, rstd, xhat * lng + lnb


def _mix_out_fwd(z, oa, ln_g, ln_b, ws, bst, wout, x1, g_post, tm):
    S = z.shape[0]
    nw = tm // L

    def body(u_ref, sv_ref, ga_ref, gb_ref, oa_ref, lng_ref, lnb_ref, ws_ref, bst_ref, wout_ref, x1_ref, gp_ref,
             mg_ref, y_ref, x2_ref, mg_s):
        mask = _sgu_mask(False)
        for g in range(G):
            cols = slice(g * L, (g + 1) * L)
            wm = jnp.where(mask, ws_ref[g], 0.0).astype(BF)
            bcol = bst_ref[:, g:g + 1]
            lng = lng_ref[:, cols]
            lnb = lnb_ref[:, cols]
            for w in range(nw):
                rows = slice(w * L, (w + 1) * L)
                vs = _gelu(sv_ref[rows, cols].astype(F32))
                _, _, vn = _ln_group(vs, lng, lnb)
                mixed = _dot(wm, vn.astype(BF)) + bcol
                ob = _gelu(u_ref[rows, cols].astype(F32)) * mixed
                mg = (_sigmoid(ga_ref[rows, cols].astype(F32)) * oa_ref[rows, cols].astype(F32)
                      + _sigmoid(gb_ref[rows, cols].astype(F32)) * ob)
                mg_s[rows, cols] = mg.astype(BF)
        mgb = mg_s[...]
        mg_ref[...] = mgb
        y = _dot(mgb, wout_ref[...])
        y_ref[...] = y
        x2_ref[...] = x1_ref[...] + _rms(y, gp_ref[...])

    row = pl.BlockSpec((tm, D), lambda i: (i, 0))
    vec = pl.BlockSpec((1, D), lambda i: (0, 0))

    def zcol(kb):
        return pl.BlockSpec((tm, D), lambda i: (i, kb))

    return pl.pallas_call(
        body, name="mix_out_fwd",
        grid=(S // tm,),
        in_specs=[zcol(3), zcol(4), zcol(5), zcol(6), row, vec, vec,
                  pl.BlockSpec((G, L, L), lambda i: (0, 0, 0)),
                  pl.BlockSpec((L, HD), lambda i: (0, 0)),
                  pl.BlockSpec((D, D), lambda i: (0, 0)),
                  row, vec],
        out_specs=[row, row, row],
        out_shape=[jax.ShapeDtypeStruct((S, D), BF),
                   jax.ShapeDtypeStruct((S, D), F32),
                   jax.ShapeDtypeStruct((S, D), F32)],
        scratch_shapes=[pltpu.VMEM((tm, D), BF)],
        compiler_params=_params(1),
    )(z, z, z, z, oa, ln_g, ln_b, ws, bst, wout, x1, g_post)


def _mix_out_bwd(dx2, y2, g_post, wout, z, oa, ln_g, ln_b, ws, wst, bst, tm, rider):
    S = z.shape[0]
    nw = tm // L

    def body(dx2_ref, y_ref, gp_ref, wout_ref, u_ref, sv_ref, ga_ref, gb_ref, oa_ref, lng_ref, lnb_ref,
             ws_ref, wst_ref, bst_ref, q_ref,
             dz_ref, dy_ref, doa_ref, dgp_ref, dlng_ref, dlnb_ref, dws_ref, dbst_ref, t_ref,
             dzg_s, dm_s, send_sems, recv_sems):
        i = pl.program_id(0)
        c = pl.program_id(1)

        @pl.when((i == 0) & (c == 0))
        def _():
            _rider_start(rider[0], q_ref, t_ref, send_sems, recv_sems)
            dgp_ref[...] = jnp.zeros_like(dgp_ref)
            dlng_ref[...] = jnp.zeros_like(dlng_ref)
            dlnb_ref[...] = jnp.zeros_like(dlnb_ref)
            dws_ref[...] = jnp.zeros_like(dws_ref)
            dbst_ref[...] = jnp.zeros_like(dbst_ref)

        @pl.when(c == 0)
        def _():
            dy, dg = _rms_bwd(dx2_ref[...], y_ref[...], gp_ref[...])
            dyb = dy.astype(BF)
            dy_ref[...] = dyb
            dgp_ref[...] += dg
            dm_s[...] = _dot_nt(dyb, wout_ref[...])
            mask = _sgu_mask(False)
            mask_t = _sgu_mask(True)
            lane = lax.broadcasted_iota(jnp.int32, (L, HD), 1)
            for g in range(G):
                cols = slice(g * L, (g + 1) * L)
                wm = jnp.where(mask, ws_ref[g], 0.0).astype(BF)
                wmt = jnp.where(mask_t, wst_ref[g], 0.0).astype(BF)
                bcol = bst_ref[:, g:g + 1]
                lng = lng_ref[:, cols]
                lnb = lnb_ref[:, cols]
                dws_g = jnp.zeros((L, L), F32)
                dbs_g = jnp.zeros((L, 1), F32)
                dlng_g = jnp.zeros((1, L), F32)
                dlnb_g = jnp.zeros((1, L), F32)
                for w in range(nw):
                    rows = slice(w * L, (w + 1) * L)
                    dm = dm_s[rows, cols]
                    vs, dvs_dz = _gelu_parts(sv_ref[rows, cols].astype(F32))
                    xhat, rstd, vn = _ln_group(vs, lng, lnb)
                    vnb = vn.astype(BF)
                    mixed = _dot(wm, vnb) + bcol
                    u, du_dz = _gelu_parts(u_ref[rows, cols].astype(F32))
                    sga = _sigmoid(ga_ref[rows, cols].astype(F32))
                    sgb = _sigmoid(gb_ref[rows, cols].astype(F32))
                    oav = oa_ref[rows, cols].astype(F32)
                    ob = u * mixed
                    doa_ref[rows, cols] = (dm * sga).astype(BF)
                    dzg_s[2, rows, cols] = (dm * oav * sga * (1.0 - sga)).astype(BF)
                    dzg_s[3, rows, cols] = (dm * ob * sgb * (1.0 - sgb)).astype(BF)
                    dob = dm * sgb
                    dzg_s[0, rows, cols] = (dob * mixed * du_dz).astype(BF)
                    dmixed = dob * u
                    dmb = dmixed.astype(BF)
                    dbs_g += jnp.sum(dmixed, axis=1, keepdims=True)
                    dws_g += _dot_nt(dmb, vnb)
                    dvn = _dot(wmt, dmb)
                    dlng_g += jnp.sum(dvn * xhat, axis=0, keepdims=True)
                    dlnb_g += jnp.sum(dvn, axis=0, keepdims=True)
                    dxh = dvn * lng
                    dvs = rstd * (dxh - jnp.mean(dxh, axis=-1, keepdims=True)
                                  - xhat * jnp.mean(dxh * xhat, axis=-1, keepdims=True))
                    dzg_s[1, rows, cols] = (dvs * dvs_dz).astype(BF)
                dws_ref[g] += jnp.where(mask, dws_g, 0.0)
                dbst_ref[...] += jnp.where(lane == g, dbs_g, 0.0)
                dlng_ref[:, cols] += dlng_g
                dlnb_ref[:, cols] += dlnb_g

        dz_ref[...] = dzg_s[c]

        @pl.when((i == S // tm - 1) & (c == 3))
        def _():
            _rider_finish(rider[0], q_ref, t_ref, send_sems, recv_sems)

    row = pl.BlockSpec((tm, D), lambda i, c: (i, 0))
    vec = pl.BlockSpec((1, D), lambda i, c: (0, 0))
    wsspec = pl.BlockSpec((G, L, L), lambda i, c: (0, 0, 0))
    bspec = pl.BlockSpec((L, HD), lambda i, c: (0, 0))
    anywhere = pl.BlockSpec(memory_space=pl.ANY)

    def zcol(kb):
        return pl.BlockSpec((tm, D), lambda i, c: (i, kb))

    return pl.pallas_call(
        body, name="mix_out_bwd",
        grid=(S // tm, 4),
        in_specs=[row, row, vec, pl.BlockSpec((D, D), lambda i, c: (0, 0)),
                  zcol(3), zcol(4), zcol(5), zcol(6), row, vec, vec, wsspec, wsspec, bspec, anywhere],
        out_specs=[pl.BlockSpec((tm, D), lambda i, c: (i, 3 + c)),
                   row, row, vec, vec, vec, wsspec, bspec, anywhere],
        out_shape=[jax.ShapeDtypeStruct((S, ZW), BF),
                   jax.ShapeDtypeStruct((S, D), BF),
                   jax.ShapeDtypeStruct((S, D), BF),
                   jax.ShapeDtypeStruct((1, D), F32),
                   jax.ShapeDtypeStruct((1, D), F32),
                   jax.ShapeDtypeStruct((1, D), F32),
                   jax.ShapeDtypeStruct((G, L, L), F32),
                   jax.ShapeDtypeStruct((L, HD), F32),
                   _rider_out(rider)],
        scratch_shapes=[pltpu.VMEM((4, tm, D), BF), pltpu.VMEM((tm, D), F32)] + _rider_sems(rider),
        compiler_params=_params(2),
    )(dx2, y2, g_post, wout, z, z, z, z, oa, ln_g, ln_b, ws, wst, bst, rider[1])


def _adamw(w, g, m, v, tr):
    R, C = w.shape

    def body(w_ref, g_ref, m_ref, v_ref, d_ref, nm_ref, nv_ref):
        gv = g_ref[...]
        m_new = ADAM_B1 * m_ref[...] + (1.0 - ADAM_B1) * gv
        v_new = ADAM_B2 * v_ref[...] + (1.0 - ADAM_B2) * (gv * gv)
        m_hat = m_new / (1.0 - ADAM_B1 ** ADAM_STEP)
        v_hat = v_new / (1.0 - ADAM_B2 ** ADAM_STEP)
        d_ref[...] = -ADAM_LR * (m_hat / (jnp.sqrt(v_hat) + ADAM_EPS) + ADAM_WD * w_ref[...])
        nm_ref[...] = m_new
        nv_ref[...] = v_new

    spec = pl.BlockSpec((tr, C), lambda i: (i, 0))
    shp = jax.ShapeDtypeStruct((R, C), F32)
    return pl.pallas_call(
        body, name="adamw",
        grid=(R // tr,),
        in_specs=[spec] * 4, out_specs=[spec] * 3, out_shape=[shp] * 3,
        compiler_params=_params(1),
    )(w, g, m, v)


def _mesh_pos():
    return lax.axis_index("x"), lax.axis_index("y"), lax.axis_index("c")


def _half(c, rows):
    return pl.ds(pl.multiple_of(c * rows, 16), rows)


def _other_chips(x, y):
    return [(1 - x, y), (x, 1 - y), (1 - x, 1 - y)]


def _remote(k, src, dst, to, send_sems, recv_sems):
    return pltpu.make_async_remote_copy(src_ref=src, dst_ref=dst, send_sem=send_sems.at[k],
                                        recv_sem=recv_sems.at[k], device_id=to, device_id_type=MESH)


def _gather_start(wp_ref, g_ref, send_sems, recv_sems):
    x, y, c = _mesh_pos()
    mine = _half(c, wp_ref.shape[0] // 2)
    for k, (px, py) in enumerate(_other_chips(x, y)):
        _remote(k, wp_ref.at[mine], g_ref.at[2 * x + y, mine], (px, py, c), send_sems, recv_sems).start()


def _gather_forward(wp_ref, g_ref, send_sems, recv_sems):
    x, y, c = _mesh_pos()
    sibling = (x, y, 1 - c)
    mine = _half(c, wp_ref.shape[0] // 2)
    for k, (px, py) in enumerate(_other_chips(x, y)):
        land = g_ref.at[2 * px + py, mine]
        _remote(k, land, land, (px, py, c), send_sems, recv_sems).wait_recv()
        _remote(3 + k, land, land, sibling, send_sems, recv_sems).start()


def _gather_finish(wp_ref, g_ref, send_sems, recv_sems):
    x, y, c = _mesh_pos()
    sibling = (x, y, 1 - c)
    rows = wp_ref.shape[0] // 2
    mine, other = _half(c, rows), _half(1 - c, rows)
    chips = _other_chips(x, y)
    for k, (px, py) in enumerate(chips):
        land = g_ref.at[2 * px + py, other]
        _remote(3 + k, land, land, sibling, send_sems, recv_sems).wait_recv()
    for k, (px, py) in enumerate(chips):
        land = g_ref.at[2 * px + py, mine]
        _remote(k, wp_ref.at[mine], g_ref.at[2 * x + y, mine], (px, py, c), send_sems, recv_sems).wait_send()
        _remote(3 + k, land, land, sibling, send_sems, recv_sems).wait_send()


def _place_own_shard(g, wp):
    x, y, _ = _mesh_pos()
    return lax.dynamic_update_index_in_dim(g, wp, 2 * x + y, 0)


def _all_gather_weights(wp):
    def body(wp_ref, g_ref, send_sems, recv_sems):
        _gather_start(wp_ref, g_ref, send_sems, recv_sems)
        _gather_forward(wp_ref, g_ref, send_sems, recv_sems)
        _gather_finish(wp_ref, g_ref, send_sems, recv_sems)

    g = pl.pallas_call(
        body, name="all_gather_weights",
        in_specs=[pl.BlockSpec(memory_space=pl.ANY)],
        out_specs=pl.BlockSpec(memory_space=pl.ANY),
        out_shape=jax.ShapeDtypeStruct((NSH,) + wp.shape, wp.dtype),
        scratch_shapes=[pltpu.SemaphoreType.DMA((6,)), pltpu.SemaphoreType.DMA((6,))],
        compiler_params=pltpu.CompilerParams(has_side_effects=True),
    )(wp)
    return _place_own_shard(g, wp)


def _scatter_copies(q_ref, t_ref, send_sems, recv_sems):
    x, y, c = _mesh_pos()
    return [_remote(k, q_ref.at[2 * px + py], t_ref.at[k], (px, py, c), send_sems, recv_sems)
            for k, (px, py) in enumerate(_other_chips(x, y))]


_FLIPS = [(fx, fy, fc) for fx in (0, 1) for fy in (0, 1) for fc in (0, 1)][1:]


def _rider_copies(kind, src_ref, dst_ref, send_sems, recv_sems):
    if kind == "scatter":
        return _scatter_copies(src_ref, dst_ref, send_sems, recv_sems)
    x, y, c = _mesh_pos()
    if kind == "broadcast":
        return [_remote(k, src_ref, dst_ref.at[4 * x + 2 * y + c], (x ^ fx, y ^ fy, c ^ fc), send_sems, recv_sems)
                for k, (fx, fy, fc) in enumerate(_FLIPS)]
    rows = src_ref.shape[1] // 2
    return [_remote(0, src_ref.at[:, _half(1 - c, rows)], dst_ref, (x, y, 1 - c), send_sems, recv_sems)]


def _place_own_block(sm, block):
    x, y, c = _mesh_pos()
    return lax.dynamic_update_index_in_dim(sm, block, 4 * x + 2 * y + c, 0)


def _rider_start(kind, src_ref, dst_ref, send_sems, recv_sems):
    for cp in _rider_copies(kind, src_ref, dst_ref, send_sems, recv_sems):
        cp.start()


def _rider_finish(kind, src_ref, dst_ref, send_sems, recv_sems):
    for cp in _rider_copies(kind, src_ref, dst_ref, send_sems, recv_sems):
        cp.wait()


def _rider_out(rider):
    kind, a = rider
    if kind == "scatter":
        return jax.ShapeDtypeStruct((3,) + a.shape[1:], a.dtype)
    if kind == "broadcast":
        return jax.ShapeDtypeStruct((NDEV,) + a.shape, a.dtype)
    return jax.ShapeDtypeStruct((a.shape[0], a.shape[1] // 2) + a.shape[2:], a.dtype)


def _rider_sems(rider):
    n = {"scatter": 3, "broadcast": 7, "exchange": 1}[rider[0]]
    return [pltpu.SemaphoreType.DMA((n,)), pltpu.SemaphoreType.DMA((n,))]


def _pair_exchange(p):
    rows = p.shape[1] // 2

    def body(p_ref, r_ref, send_sem, recv_sem):
        x, y, c = _mesh_pos()
        cp = pltpu.make_async_remote_copy(src_ref=p_ref.at[:, _half(1 - c, rows)], dst_ref=r_ref, send_sem=send_sem,
                                          recv_sem=recv_sem, device_id=(x, y, 1 - c), device_id_type=MESH)
        cp.start()
        cp.wait()

    return pl.pallas_call(
        body, name="pair_exchange",
        in_specs=[pl.BlockSpec(memory_space=pl.ANY)],
        out_specs=pl.BlockSpec(memory_space=pl.ANY),
        out_shape=jax.ShapeDtypeStruct((NSH, rows, D), F32),
        scratch_shapes=[pltpu.SemaphoreType.DMA, pltpu.SemaphoreType.DMA],
        compiler_params=pltpu.CompilerParams(has_side_effects=True),
    )(p)


def _pair_add(p, r, nb):
    rows = r.shape[1]
    tr = rows // nb

    def body(p_ref, r_ref, q_ref):
        q_ref[...] = (p_ref[...] + r_ref[...]).astype(BF)

    return pl.pallas_call(
        body, name="pair_add", grid=(NSH, nb),
        in_specs=[pl.BlockSpec((None, tr, D), lambda j, i: (j, lax.axis_index("c") * nb + i, 0)),
                  pl.BlockSpec((None, tr, D), lambda j, i: (j, i, 0))],
        out_specs=pl.BlockSpec((None, tr, D), lambda j, i: (j, i, 0)),
        out_shape=jax.ShapeDtypeStruct((NSH, rows, D), BF),
        compiler_params=_params(2),
    )(p, r)


def _chip_exchange(q, small):
    riders = (("scatter", q), ("broadcast", small))

    def body(q_ref, s_ref, t_ref, sm_ref, send_sems, recv_sems, ssend_sems, srecv_sems):
        _rider_start("scatter", q_ref, t_ref, send_sems, recv_sems)
        _rider_start("broadcast", s_ref, sm_ref, ssend_sems, srecv_sems)
        _rider_finish("scatter", q_ref, t_ref, send_sems, recv_sems)
        _rider_finish("broadcast", s_ref, sm_ref, ssend_sems, srecv_sems)

    anywhere = pl.BlockSpec(memory_space=pl.ANY)
    t, sm = pl.pallas_call(
        body, name="chip_exchange",
        in_specs=[anywhere, anywhere], out_specs=[anywhere, anywhere],
        out_shape=[_rider_out(r) for r in riders],
        scratch_shapes=_rider_sems(riders[0]) + _rider_sems(riders[1]),
        compiler_params=pltpu.CompilerParams(has_side_effects=True),
    )(q, small)
    return t, _place_own_block(sm, small)


def _shard_sum(p, r, t, nb):
    rows = r.shape[1]
    tr = rows // nb

    def shard():
        return 2 * lax.axis_index("x") + lax.axis_index("y")

    def body(p_ref, r_ref, t_ref, o_ref):
        s = p_ref[...] + r_ref[...]
        for k in range(3):
            s = s + t_ref[k].astype(F32)
        o_ref[...] = s

    return pl.pallas_call(
        body, name="shard_sum", grid=(nb,),
        in_specs=[pl.BlockSpec((None, tr, D), lambda i: (shard(), lax.axis_index("c") * nb + i, 0)),
                  pl.BlockSpec((None, tr, D), lambda i: (shard(), i, 0)),
                  pl.BlockSpec((3, tr, D), lambda i: (0, i, 0))],
        out_specs=pl.BlockSpec((tr, D), lambda i: (i, 0)),
        out_shape=jax.ShapeDtypeStruct((rows, D), F32),
        compiler_params=_params(1),
    )(p, r, t)


def _small_sum(sm):
    def body(sm_ref, o_ref):
        s = sm_ref[0]
        for k in range(1, NDEV):
            s = s + sm_ref[k]
        o_ref[...] = s

    return pl.pallas_call(
        body, name="small_sum",
        in_specs=[pl.BlockSpec(memory_space=pltpu.VMEM)],
        out_specs=pl.BlockSpec(memory_space=pltpu.VMEM),
        out_shape=jax.ShapeDtypeStruct(sm.shape[1:], F32),
    )(sm)


def _pair_gather(halves):
    n = len(halves)

    def body(*refs):
        gh_refs, o_refs, (send_sems, recv_sems) = refs[:n], refs[n:2 * n], refs[2 * n:]
        x, y, c = _mesh_pos()
        sibling = (x, y, 1 - c)
        for g in range(n):
            rows = gh_refs[g].shape[0]
            _remote(g, gh_refs[g], o_refs[g].at[_half(c, rows)], sibling, send_sems, recv_sems).start()
        for g in range(n):
            rows = gh_refs[g].shape[0]
            _remote(g, gh_refs[g], o_refs[g].at[_half(c, rows)], sibling, send_sems, recv_sems).wait_send()
            _remote(g, gh_refs[g], o_refs[g].at[_half(1 - c, rows)], sibling, send_sems, recv_sems).wait_recv()

    anywhere = pl.BlockSpec(memory_space=pl.ANY)
    outs = pl.pallas_call(
        body, name="pair_gather",
        in_specs=[anywhere] * n, out_specs=[anywhere] * n,
        out_shape=[jax.ShapeDtypeStruct((2 * h.shape[0], D), F32) for h in halves],
        scratch_shapes=[pltpu.SemaphoreType.DMA((n,)), pltpu.SemaphoreType.DMA((n,))],
        compiler_params=pltpu.CompilerParams(has_side_effects=True),
    )(*halves)
    c = lax.axis_index("c")
    return [lax.dynamic_update_slice_in_dim(o, h, c * h.shape[0], 0) for o, h in zip(outs, halves)]


def _pad_cols(a, n):
    return jnp.pad(a, ((0, 0), (0, n - a.shape[1])))


def _split_w_in(w_in_full):
    q, k, v = w_in_full[:, :D], w_in_full[:, D:2 * D], w_in_full[:, 2 * D:3 * D]
    f = w_in_full[:, 3 * D:3 * D + H]
    gates = w_in_full[:, 3 * D + H:]
    kv = jnp.stack([k.reshape(D, H, HD), v.reshape(D, H, HD)], axis=2).reshape(D, 2 * D)
    return jnp.concatenate([q, kv, gates], axis=1), _pad_cols(f, HD)


def _merge_w_in_grad(dwcat, dwf):
    kv = dwcat[:, D:3 * D].reshape(D, H, 2, HD)
    return jnp.concatenate([dwcat[:, :D], kv[:, :, 0].reshape(D, D), kv[:, :, 1].reshape(D, D),
                            dwf[:, :H], dwcat[:, 3 * D:]], axis=1)


def _ffn_weight_grads(h, da, db, act, dy, bt):
    g = _mm_tn(h, da, D, D, bt, into=(None, FFN_ROWS, 0, "cols"))
    g = _mm_tn(h, db, D, D, bt, into=(g, FFN_ROWS, 1, "cols"))
    return _mm_tn(act, dy, D, D, bt, into=(g, FFN_ROWS, 2, "rows"))


def _train_step(x, target, wp1, wp2, small, tm, t_attn):
    S = x.shape[0]
    g1pre, g1post = small["ffn1_pre_g"], small["ffn1_post_g"]
    gmpre, gmpost = small["mix_pre_g"], small["mix_post_g"]
    g2pre, g2post = small["ffn2_pre_g"], small["ffn2_post_g"]
    ln_g, ln_b = small["sgu_ln_g"], small["sgu_ln_b"]
    ws = small["sgu_w_s"][0]
    wst = jnp.swapaxes(ws, 1, 2)
    bst = _pad_cols(small["sgu_b_s"][0].T, HD)
    bf = _pad_cols(small["b_forget"], HD)

    w1 = _all_gather_weights(wp1)
    h1, a1, b1, y1, x1, h2, w2 = _ffn_fwd(x, g1pre, w1, g1post, tm, ("norm", gmpre), gather=wp2)
    wout = w2[:, FFN_ROWS:FFN_ROWS + 256, :].reshape(D, D)
    r0 = FFN_ROWS + 256
    w_in_full = jnp.concatenate(
        [blk for j in range(NSH) for blk in (w2[j, r0:r0 + D], w2[j, r0 + D:r0 + 2 * D, :WIN_SH - D])], axis=1)
    wcat, wf = _split_w_in(w_in_full)
    z = _mm(h2, wcat, min(1024, S), D, BF, first_block_scale=SCALE * LOG2E)
    zf = _mm(h2, wf, tm, HD, F32)
    cs = min(512, S)
    c, ccol_b = _forget_cumsum(zf, bf, cs)
    crow = jnp.transpose(c[:, :H]).reshape(H, S // t_attn, t_attn)
    oa, lse_b = _fox_fwd(z, ccol_b, crow, t_attn)
    merged, y2, x2 = _mix_out_fwd(z, oa, ln_g, ln_b, ws, bst, wout, x1, gmpost, tm)
    h3, a3, b3, y3, _, dx3, loss_acc = _ffn_fwd(x2, g2pre, w2, g2post, tm, ("loss", target))
    loss = loss_acc[0, 0]

    dy3, da3, db3, act3, dx2, dg2post, dg2pre = _ffn_bwd(dx3, y3, g2post, a3, b3, w2, x2, g2pre, tm)
    bt = min(2048, S)
    g_ffn2 = _ffn_weight_grads(h3, da3, db3, act3, dy3, bt)

    dz, dy2, doa, dgmpost, dlng, dlnb, dws, dbst, r_ffn2 = _mix_out_bwd(
        dx2, y2, gmpost, wout, z, oa, ln_g, ln_b, ws, wst, bst, tm, ("exchange", g_ffn2))
    q_ffn2 = _pair_add(g_ffn2, r_ffn2, 3)
    g_mix = _mm_tn(merged, dy2, 256, D, bt, into=(None, MIX_ROWS, 0, "rows"))
    dz, dkt, dvt, dc_keys, dc_queries, t_ffn2 = _fox_bwd(z, doa, oa, lse_b, ccol_b, crow, dz, t_attn,
                                                         ("scatter", q_ffn2))
    dz = _fox_bwd_finish(dkt, dvt, dz, t_attn)
    dc = _pad_cols(jnp.transpose(dc_queries.reshape(H, S) - dc_keys.reshape(H, S)), HD)
    dzf, dbf = _forget_bwd(dc, zf, bf, cs)
    dwcat = _mm_tn(h2, dz, D, D, bt)
    dwf = _mm_tn(h2, dzf, D, HD, bt)
    dwin = _merge_w_in_grad(dwcat, dwf)
    dwin_a = jnp.stack([dwin[:, j * WIN_SH:j * WIN_SH + D] for j in range(NSH)])
    dwin_b = jnp.stack([_pad_cols(dwin[:, j * WIN_SH + D:(j + 1) * WIN_SH], D) for j in range(NSH)])
    g_mix = lax.dynamic_update_slice(g_mix, jnp.concatenate([dwin_a, dwin_b], axis=1), (0, 256, 0))
    dx1, dgmpre, r_mix = _mix_in_bwd(dz, wcat, dzf, wf, x1, gmpre, dx2, tm, ("exchange", g_mix))
    q_mix = _pair_add(g_mix, r_mix, 3)

    small_early = _pack_small({
        "mix_pre_g": dgmpre, "mix_post_g": dgmpost, "ffn2_pre_g": dg2pre, "ffn2_post_g": dg2post,
        "sgu_ln_g": dlng, "sgu_ln_b": dlnb, "sgu_w_s": dws[None], "sgu_b_s": jnp.transpose(dbst[:, :G])[None],
        "b_forget": dbf[:, :H]}, _SMALL_EARLY)
    dy1, da1, db1, act1, dx, dg1post, dg1pre, t_mix, sm_early = _ffn_bwd(
        dx1, y1, g1post, a1, b1, w1, x, g1pre, tm, riders=(("scatter", q_mix), ("broadcast", small_early)))
    sm_early = _place_own_block(sm_early, small_early)
    g_gu = _mm_tn(h1, da1, D, D, bt, into=(None, 2 * D, 0, "cols"))
    g_gu = _mm_tn(h1, db1, D, D, bt, into=(g_gu, 2 * D, 1, "cols"))
    r_gu = _pair_exchange(g_gu)
    q_gu = _pair_add(g_gu, r_gu, 2)
    g_dn, t_gu = _mm_tn(act1, dy1, D, D, bt, into=(None, D, 0, "rows"), rider=("scatter", q_gu))
    r_dn = _pair_exchange(g_dn)
    q_dn = _pair_add(g_dn, r_dn, 2)
    small_late = _pack_small({"ffn1_pre_g": dg1pre, "ffn1_post_g": dg1post}, _SMALL_LATE)
    t_dn, sm_late = _chip_exchange(q_dn, small_late)

    halves = [_shard_sum(g_gu, r_gu, t_gu, 2), _shard_sum(g_dn, r_dn, t_dn, 2),
              _shard_sum(g_ffn2, r_ffn2, t_ffn2, 3), _shard_sum(g_mix, r_mix, t_mix, 3)]
    f_gu, f_dn, f_ffn2, f_mix = _pair_gather(halves)
    gsm = jnp.concatenate([_small_sum(sm_early), _small_sum(sm_late)], axis=0)
    return loss, dx, jnp.concatenate([f_gu, f_dn], axis=0), f_ffn2, f_mix, gsm


_SMALL_EARLY = ["mix_pre_g", "mix_post_g", "ffn2_pre_g", "ffn2_post_g", "sgu_ln_g", "sgu_ln_b", "sgu_b_s", "b_forget",
                "sgu_w_s"]
_SMALL_LATE = ["ffn1_pre_g", "ffn1_post_g"]
_SMALL_NAMES = _SMALL_EARLY + _SMALL_LATE


def _pack_small(d, names=None):
    rows = []
    for n in names or _SMALL_NAMES:
        a = d[n].astype(F32)
        if n == "b_forget":
            a = _pad_cols(a, D)
        a = a.reshape(-1, D)
        rows.append(jnp.pad(a, ((0, -a.shape[0] % SMALL_STRIDE), (0, 0))))
    return jnp.concatenate(rows, axis=0)


def _unpack_small(p):
    out, r = {}, 0
    for n in _SMALL_NAMES:
        if n == "sgu_w_s":
            out[n] = p[r:r + L].reshape(1, G, L, L)
            r += L
        elif n == "b_forget":
            out[n] = p[r:r + 1, :H]
            r += SMALL_STRIDE
        elif n == "sgu_b_s":
            out[n] = p[r:r + 1].reshape(1, G, L)
            r += SMALL_STRIDE
        else:
            out[n] = p[r:r + 1]
            r += SMALL_STRIDE
    return out


_BIG_NAMES = ["ffn1_w_gate", "ffn1_w_up", "ffn1_w_down", "ffn2_w_gate", "ffn2_w_up", "ffn2_w_down", "w_out", "w_in"]
_WEIGHT_ORDER = ['ffn1_pre_g', 'ffn1_w_gate', 'ffn1_w_up', 'ffn1_w_down', 'ffn1_post_g', 'mix_pre_g', 'w_in', 'b_forget',
                 'sgu_ln_g', 'sgu_ln_b', 'sgu_w_s', 'sgu_b_s', 'w_out', 'mix_post_g', 'ffn2_pre_g', 'ffn2_w_gate',
                 'ffn2_w_up', 'ffn2_w_down', 'ffn2_post_g']


def _pack_ffn(w, name):
    return jnp.concatenate([w[name + "_w_gate"][0], w[name + "_w_up"][0], w[name + "_w_down"][0]], axis=0)


def _pack_mix(w):
    w_in = w["w_in"][0]
    return jnp.concatenate([w["w_out"][0], w_in[:, :D], _pad_cols(w_in[:, D:], D)], axis=0)


def _unpack_ffn(p, name):
    return {name + "_w_gate": p[:D][None], name + "_w_up": p[D:2 * D][None], name + "_w_down": p[2 * D:][None]}


def _unpack_mix(p):
    return {"w_out": p[:256][None],
            "w_in": jnp.concatenate([p[256:256 + D], p[256 + D:, :WIN_SH - D]], axis=1)[None]}


def _step(args, tm, t_attn):
    x = args["x"][0]
    target = args["loss_target"][0]
    weights = {n: args[n] for n in _WEIGHT_ORDER}
    small = {n: weights[n] for n in _SMALL_NAMES}

    wb = {n: weights[n].astype(BF) for n in _BIG_NAMES}
    wp1 = _pack_ffn(wb, "ffn1")
    wp2 = jnp.concatenate([_pack_ffn(wb, "ffn2"), _pack_mix(wb)], axis=0)
    loss_local, dx, f_ffn1, f_ffn2, f_mix, gsm = _train_step(x, target, wp1, wp2, small, tm, t_attn)
    loss = lax.psum(loss_local, ("x", "y", "c"))
    grads = {**_unpack_ffn(f_ffn1, "ffn1"), **_unpack_ffn(f_ffn2, "ffn2"), **_unpack_mix(f_mix),
             **_unpack_small(gsm)}

    delta, new_m, new_v = {}, {}, {}
    for n in _BIG_NAMES:
        shp = weights[n].shape
        w2 = weights[n].reshape(-1, shp[-1])
        rows = w2.shape[0]
        d, nm, nv = _adamw(w2, grads[n].reshape(w2.shape), args["m_" + n].reshape(w2.shape),
                           args["v_" + n].reshape(w2.shape), rows // 4)
        delta[n], new_m[n], new_v[n] = d.reshape(shp), nm.reshape(shp), nv.reshape(shp)
    ds, nms, nvs = _adamw(_pack_small(small), gsm, _pack_small({n: args["m_" + n] for n in _SMALL_NAMES}),
                          _pack_small({n: args["v_" + n] for n in _SMALL_NAMES}), SMALL_ROWS)
    delta.update(_unpack_small(ds))
    new_m.update(_unpack_small(nms))
    new_v.update(_unpack_small(nvs))

    return (loss, dx[None], *[grads[n] for n in _WEIGHT_ORDER], *[delta[n] for n in _WEIGHT_ORDER],
            *[new_m[n] for n in _WEIGHT_ORDER], *[new_v[n] for n in _WEIGHT_ORDER])


_ARG_NAMES = (["x"] + _WEIGHT_ORDER + ["loss_target"] + ["m_" + n for n in _WEIGHT_ORDER]
              + ["v_" + n for n in _WEIGHT_ORDER])


def kernel(x, ffn1_pre_g, ffn1_w_gate, ffn1_w_up, ffn1_w_down, ffn1_post_g, mix_pre_g, w_in, b_forget, sgu_ln_g, sgu_ln_b, sgu_w_s, sgu_b_s, w_out, mix_post_g, ffn2_pre_g, ffn2_w_gate, ffn2_w_up, ffn2_w_down, ffn2_post_g, loss_target, m_ffn1_pre_g, m_ffn1_w_gate, m_ffn1_w_up, m_ffn1_w_down, m_ffn1_post_g, m_mix_pre_g, m_w_in, m_b_forget, m_sgu_ln_g, m_sgu_ln_b, m_sgu_w_s, m_sgu_b_s, m_w_out, m_mix_post_g, m_ffn2_pre_g, m_ffn2_w_gate, m_ffn2_w_up, m_ffn2_w_down, m_ffn2_post_g, v_ffn1_pre_g, v_ffn1_w_gate, v_ffn1_w_up, v_ffn1_w_down, v_ffn1_post_g, v_mix_pre_g, v_w_in, v_b_forget, v_sgu_ln_g, v_sgu_ln_b, v_sgu_w_s, v_sgu_b_s, v_w_out, v_mix_post_g, v_ffn2_pre_g, v_ffn2_w_gate, v_ffn2_w_up, v_ffn2_w_down, v_ffn2_post_g):
    args = (x, ffn1_pre_g, ffn1_w_gate, ffn1_w_up, ffn1_w_down, ffn1_post_g, mix_pre_g, w_in, b_forget, sgu_ln_g, sgu_ln_b, sgu_w_s, sgu_b_s, w_out, mix_post_g, ffn2_pre_g, ffn2_w_gate, ffn2_w_up, ffn2_w_down, ffn2_post_g, loss_target, m_ffn1_pre_g, m_ffn1_w_gate, m_ffn1_w_up, m_ffn1_w_down, m_ffn1_post_g, m_mix_pre_g, m_w_in, m_b_forget, m_sgu_ln_g, m_sgu_ln_b, m_sgu_w_s, m_sgu_b_s, m_w_out, m_mix_post_g, m_ffn2_pre_g, m_ffn2_w_gate, m_ffn2_w_up, m_ffn2_w_down, m_ffn2_post_g, v_ffn1_pre_g, v_ffn1_w_gate, v_ffn1_w_up, v_ffn1_w_down, v_ffn1_post_g, v_mix_pre_g, v_w_in, v_b_forget, v_sgu_ln_g, v_sgu_ln_b, v_sgu_w_s, v_sgu_b_s, v_w_out, v_mix_post_g, v_ffn2_pre_g, v_ffn2_w_gate, v_ffn2_w_up, v_ffn2_w_down, v_ffn2_post_g)
    named = dict(zip(_ARG_NAMES, args))
    tile = min(512, x.shape[1])
    return _step(named, tile, tile)
```

```python
import functools
import math

import jax
import jax.numpy as jnp
from jax import lax
from jax.experimental import pallas as pl
from jax.experimental.pallas import tpu as pltpu

D = 1024
F = 4096
H = 8
HD = 128
G = 8
L = 128
CHUNK = 64
NSH = 4
NDEV = 8
ZW = 7 * D
RMS_EPS = 1e-6
LN_EPS = 1e-5
NEG = -1e30
SCALE = 1.0 / math.sqrt(HD)
LOG2E = math.log2(math.e)
LN2 = math.log(2.0)

ADAM_LR = 0.001
ADAM_B1 = 0.9
ADAM_B2 = 0.999
ADAM_EPS = 1e-08
ADAM_WD = 0.01
ADAM_STEP = 10

VMEM_LIMIT_BYTES = 56 * 1024 * 1024

WIN_SH = 1794
FFN_ROWS = 3 * D
MIX_ROWS = 256 + 2 * D
G2_ROWS = FFN_ROWS + MIX_ROWS
SMALL_STRIDE = 8
SMALL_ROWS = 10 * SMALL_STRIDE + L

BF = jnp.bfloat16
F32 = jnp.float32
MESH = pl.DeviceIdType.MESH


def _params(n_grid):
    return pltpu.CompilerParams(dimension_semantics=("arbitrary",) * n_grid,
                                vmem_limit_bytes=VMEM_LIMIT_BYTES)


def _dot(a, b):
    return jnp.dot(a, b, preferred_element_type=F32)


def _dot_nt(a, b):
    return lax.dot_general(a, b, (((1,), (1,)), ((), ())), preferred_element_type=F32)


def _dot_tn(a, b):
    return lax.dot_general(a, b, (((0,), (0,)), ((), ())), preferred_element_type=F32)


def _rms(x, g):
    r = lax.rsqrt(jnp.mean(x * x, axis=-1, keepdims=True) + RMS_EPS)
    return x * r * g


def _rms_bwd(dn, x, g):
    r = lax.rsqrt(jnp.mean(x * x, axis=-1, keepdims=True) + RMS_EPS)
    xr = x * r
    dg = jnp.sum(dn * xr, axis=0, keepdims=True)
    t = dn * g
    dx = r * (t - xr * jnp.mean(t * xr, axis=-1, keepdims=True))
    return dx, dg


def _gelu_parts(x):
    cdf = 0.5 * (1.0 + lax.erf(x * (1.0 / math.sqrt(2.0))))
    pdf = jnp.exp(-0.5 * x * x) * (1.0 / math.sqrt(2.0 * math.pi))
    return x * cdf, cdf + x * pdf


def _gelu(x):
    return x * (0.5 * (1.0 + lax.erf(x * (1.0 / math.sqrt(2.0)))))


def _sigmoid(x):
    return 0.5 * jnp.tanh(0.5 * x) + 0.5


def _ffn_fwd(x, g_pre, wpack, g_post, tm, tail, gather=None):
    S = x.shape[0]
    nt, nf, tf = S // tm, NSH, D
    n_tail_out = 1 if tail[0] == "norm" else 2

    def body(x_ref, gpre_ref, wg_ref, wu_ref, wd_ref, gpost_ref, tail_ref, *rest):
        if gather is not None:
            wp_ref, rest = rest[0], rest[1:]
        h_ref, a_ref, b_ref, y_ref, xo_ref = rest[:5]
        tail_out = rest[5:5 + n_tail_out]
        rest = rest[5 + n_tail_out:]
        if gather is not None:
            g_ref, h_s, acc, send_sems, recv_sems = rest
        else:
            h_s, acc = rest
        i = pl.program_id(0)
        j = pl.program_id(1)

        if gather is not None:
            @pl.when((i == 0) & (j == 0))
            def _():
                _gather_start(wp_ref, g_ref, send_sems, recv_sems)

        @pl.when(j == 0)
        def _():
            h = _rms(x_ref[...], gpre_ref[...]).astype(BF)
            h_s[...] = h
            h_ref[...] = h
            acc[...] = jnp.zeros_like(acc)

        h = h_s[...]
        a = _dot(h, wg_ref[...])
        b = _dot(h, wu_ref[...])
        a_ref[...] = a.astype(BF)
        b_ref[...] = b.astype(BF)
        act = (a * _sigmoid(a)) * b
        acc[...] += _dot(act.astype(BF), wd_ref[...])

        if tail[0] == "loss":
            @pl.when((i == 0) & (j == 0))
            def _():
                tail_out[1][...] = jnp.zeros_like(tail_out[1])

        @pl.when(j == nf - 1)
        def _():
            y = acc[...]
            y_ref[...] = y
            xo = x_ref[...] + 0.5 * _rms(y, gpost_ref[...])
            xo_ref[...] = xo
            if tail[0] == "norm":
                tail_out[0][...] = _rms(xo, tail_ref[...]).astype(BF)
            else:
                e = xo - tail_ref[...]
                tail_out[0][...] = e * (1.0 / D)
                tail_out[1][...] += jnp.sum(e * e) * (0.5 / D)

        if gather is not None:
            @pl.when((i == max(nt - 2, 0)) & (j == 0))
            def _():
                _gather_forward(wp_ref, g_ref, send_sems, recv_sems)

            @pl.when((i == nt - 1) & (j == nf - 1))
            def _():
                _gather_finish(wp_ref, g_ref, send_sems, recv_sems)

    row = pl.BlockSpec((tm, D), lambda i, j: (i, 0))
    vec = pl.BlockSpec((1, D), lambda i, j: (0, 0))
    anywhere = pl.BlockSpec(memory_space=pl.ANY)
    in_specs = [row, vec,
                pl.BlockSpec((None, D, tf), lambda i, j: (j, 0, 0)),
                pl.BlockSpec((None, D, tf), lambda i, j: (j, 1, 0)),
                pl.BlockSpec((None, tf, D), lambda i, j: (j, 2, 0)),
                vec, vec if tail[0] == "norm" else row]
    out_specs = [row,
                 pl.BlockSpec((tm, tf), lambda i, j: (i, j)),
                 pl.BlockSpec((tm, tf), lambda i, j: (i, j)),
                 row, row]
    out_shape = [jax.ShapeDtypeStruct((S, D), BF),
                 jax.ShapeDtypeStruct((S, F), BF),
                 jax.ShapeDtypeStruct((S, F), BF),
                 jax.ShapeDtypeStruct((S, D), F32),
                 jax.ShapeDtypeStruct((S, D), F32)]
    if tail[0] == "norm":
        out_specs.append(row)
        out_shape.append(jax.ShapeDtypeStruct((S, D), BF))
    else:
        out_specs += [row, pl.BlockSpec((8, HD), lambda i, j: (0, 0))]
        out_shape += [jax.ShapeDtypeStruct((S, D), F32), jax.ShapeDtypeStruct((8, HD), F32)]
    scratch = [pltpu.VMEM((tm, D), BF), pltpu.VMEM((tm, D), F32)]
    args = [x, g_pre, wpack, wpack, wpack, g_post, tail[1]]
    if gather is not None:
        in_specs.append(anywhere)
        out_specs.append(anywhere)
        out_shape.append(jax.ShapeDtypeStruct((NSH,) + gather.shape, gather.dtype))
        scratch += [pltpu.SemaphoreType.DMA((6,)), pltpu.SemaphoreType.DMA((6,))]
        args.append(gather)
    res = list(pl.pallas_call(
        body, name="ffn_fwd" if gather is None else "ffn_fwd_gather",
        grid=(nt, nf),
        in_specs=in_specs, out_specs=out_specs, out_shape=out_shape, scratch_shapes=scratch,
        compiler_params=_params(2),
    )(*args))
    if gather is not None:
        res[-1] = _place_own_shard(res[-1], gather)
    return res


def _ffn_bwd(dxo, y, g_post, a, b, wpack, x_in, g_pre, tm, riders=()):
    S = dxo.shape[0]
    nt, nf, tf = S // tm, NSH, D
    nr = len(riders)

    def body(dxo_ref, y_ref, gpost_ref, a_ref, b_ref, wg_ref, wu_ref, wd_ref, xin_ref, gpre_ref, *rest):
        rsrc = rest[:nr]
        dy_ref, da_ref, db_ref, act_ref, dxin_ref, dgpost_ref, dgpre_ref = rest[nr:nr + 7]
        rdst = rest[nr + 7:2 * nr + 7]
        dy_s, acc = rest[2 * nr + 7:2 * nr + 9]
        sems = rest[2 * nr + 9:]
        i = pl.program_id(0)
        j = pl.program_id(1)

        @pl.when((i == 0) & (j == 0))
        def _():
            dgpost_ref[...] = jnp.zeros_like(dgpost_ref)
            dgpre_ref[...] = jnp.zeros_like(dgpre_ref)
            for k, (kind, _) in enumerate(riders):
                _rider_start(kind, rsrc[k], rdst[k], sems[2 * k], sems[2 * k + 1])

        @pl.when(j == 0)
        def _():
            dy, dg = _rms_bwd(0.5 * dxo_ref[...], y_ref[...], gpost_ref[...])
            dyb = dy.astype(BF)
            dy_s[...] = dyb
            dy_ref[...] = dyb
            dgpost_ref[...] += dg
            acc[...] = jnp.zeros_like(acc)

        dact = _dot_nt(dy_s[...], wd_ref[...])
        av = a_ref[...].astype(F32)
        bv = b_ref[...].astype(F32)
        sig = _sigmoid(av)
        sl = av * sig
        act_ref[...] = (sl * bv).astype(BF)
        dbb = (dact * sl).astype(BF)
        dab = (dact * bv * (sig * (1.0 + av * (1.0 - sig)))).astype(BF)
        da_ref[...] = dab
        db_ref[...] = dbb
        acc[...] += _dot_nt(dab, wg_ref[...]) + _dot_nt(dbb, wu_ref[...])

        @pl.when(j == nf - 1)
        def _():
            dx, dg = _rms_bwd(acc[...], xin_ref[...], gpre_ref[...])
            dxin_ref[...] = dxo_ref[...] + dx
            dgpre_ref[...] += dg

        if riders:
            @pl.when((i == nt - 1) & (j == nf - 1))
            def _():
                for k, (kind, _) in enumerate(riders):
                    _rider_finish(kind, rsrc[k], rdst[k], sems[2 * k], sems[2 * k + 1])

    row = pl.BlockSpec((tm, D), lambda i, j: (i, 0))
    vec = pl.BlockSpec((1, D), lambda i, j: (0, 0))
    ff = pl.BlockSpec((tm, tf), lambda i, j: (i, j))
    anywhere = pl.BlockSpec(memory_space=pl.ANY)
    in_specs = [row, row, vec, ff, ff,
                pl.BlockSpec((None, D, tf), lambda i, j: (j, 0, 0)),
                pl.BlockSpec((None, D, tf), lambda i, j: (j, 1, 0)),
                pl.BlockSpec((None, tf, D), lambda i, j: (j, 2, 0)),
                row, vec]
    out_specs = [row, ff, ff, ff, row, vec, vec]
    out_shape = [jax.ShapeDtypeStruct((S, D), BF),
                 jax.ShapeDtypeStruct((S, F), BF),
                 jax.ShapeDtypeStruct((S, F), BF),
                 jax.ShapeDtypeStruct((S, F), BF),
                 jax.ShapeDtypeStruct((S, D), F32),
                 jax.ShapeDtypeStruct((1, D), F32),
                 jax.ShapeDtypeStruct((1, D), F32)]
    scratch = [pltpu.VMEM((tm, D), BF), pltpu.VMEM((tm, D), F32)]
    args = [dxo, y, g_post, a, b, wpack, wpack, wpack, x_in, g_pre]
    for rider in riders:
        in_specs.append(anywhere)
        out_specs.append(anywhere)
        out_shape.append(_rider_out(rider))
        scratch += _rider_sems(rider)
        args.append(rider[1])
    return pl.pallas_call(
        body, name="ffn_bwd" if not riders else "ffn_bwd_riders",
        grid=(nt, nf),
        in_specs=in_specs, out_specs=out_specs, out_shape=out_shape, scratch_shapes=scratch,
        compiler_params=_params(2),
    )(*args)


def _mm_tn(a, b, bm, bn, bt, into=None, rider=None):
    S, M = a.shape
    N = b.shape[1]
    nt = S // bt
    n_in = 2 + (into is not None and into[0] is not None) + (rider is not None)

    def body(*refs):
        a_ref, b_ref, o_ref = refs[0], refs[1], refs[n_in]
        m, n, t = pl.program_id(0), pl.program_id(1), pl.program_id(2)

        if rider is not None:
            rsrc_ref, rdst_ref, send_sems, recv_sems = refs[n_in - 1], refs[n_in + 1], refs[-2], refs[-1]

            @pl.when((m == 0) & (n == 0) & (t == 0))
            def _():
                _rider_start(rider[0], rsrc_ref, rdst_ref, send_sems, recv_sems)

        @pl.when(t == 0)
        def _():
            o_ref[...] = jnp.zeros_like(o_ref)

        o_ref[...] += _dot_tn(a_ref[...], b_ref[...])

        if rider is not None:
            @pl.when((m == M // bm - 1) & (n == N // bn - 1) & (t == nt - 1))
            def _():
                _rider_finish(rider[0], rsrc_ref, rdst_ref, send_sems, recv_sems)

    in_specs = [pl.BlockSpec((bt, bm), lambda m, n, t: (t, m)),
                pl.BlockSpec((bt, bn), lambda m, n, t: (t, n))]
    args, aliases = [a, b], {}
    if into is None:
        out_spec = pl.BlockSpec((bm, bn), lambda m, n, t: (m, n))
        out_shape = jax.ShapeDtypeStruct((M, N), F32)
    else:
        buf, rows, rb, by = into
        assert bn == D and (M == bm if by == "cols" else (M == NSH * bm and N == D))
        if by == "cols":
            out_spec = pl.BlockSpec((None, bm, bn), lambda m, n, t: (n, rb, 0))
        else:
            out_spec = pl.BlockSpec((None, bm, bn), lambda m, n, t: (m, rb, 0))
        out_shape = jax.ShapeDtypeStruct((NSH, rows, D), F32)
        if buf is not None:
            in_specs.append(pl.BlockSpec(memory_space=pl.ANY))
            args.append(buf)
            aliases = {2: 0}
    if rider is None:
        return pl.pallas_call(
            body, name="mm_tn",
            grid=(M // bm, N // bn, nt),
            in_specs=in_specs, out_specs=out_spec, out_shape=out_shape,
            input_output_aliases=aliases,
            compiler_params=_params(3),
        )(*args)
    anywhere = pl.BlockSpec(memory_space=pl.ANY)
    return pl.pallas_call(
        body, name="mm_tn_rider",
        grid=(M // bm, N // bn, nt),
        in_specs=in_specs + [anywhere], out_specs=[out_spec, anywhere], out_shape=[out_shape, _rider_out(rider)],
        scratch_shapes=_rider_sems(rider),
        input_output_aliases=aliases,
        compiler_params=_params(3),
    )(*args, rider[1])


def _mm(a, w, tm, tn, out_dtype, first_block_scale=1.0):
    S, K = a.shape
    N = w.shape[1]

    def body(a_ref, w_ref, o_ref):
        r = _dot(a_ref[...], w_ref[...])
        if first_block_scale != 1.0:
            r = r * jnp.where(pl.program_id(1) == 0, first_block_scale, 1.0)
        o_ref[...] = r.astype(out_dtype)

    return pl.pallas_call(
        body, name="mm",
        grid=(S // tm, N // tn),
        in_specs=[pl.BlockSpec((tm, K), lambda i, j: (i, 0)),
                  pl.BlockSpec((K, tn), lambda i, j: (0, j))],
        out_specs=pl.BlockSpec((tm, tn), lambda i, j: (i, j)),
        out_shape=jax.ShapeDtypeStruct((S, N), out_dtype),
        compiler_params=_params(2),
    )(a, w)


def _mix_in_bwd(dz, wcat, dzf, wf, x1, g, dx2, tm, rider):
    S = dz.shape[0]
    nk = 2
    kb = ZW // nk

    def body(dz_ref, w_ref, dzf_ref, wf_ref, x_ref, g_ref, dx2_ref, rsrc_ref, dx1_ref, dg_ref, rdst_ref,
             acc, send_sems, recv_sems):
        i = pl.program_id(0)
        k = pl.program_id(1)

        @pl.when((i == 0) & (k == 0))
        def _():
            _rider_start(rider[0], rsrc_ref, rdst_ref, send_sems, recv_sems)
            dg_ref[...] = jnp.zeros_like(dg_ref)

        @pl.when(k == 0)
        def _():
            acc[...] = _dot_nt(dzf_ref[...], wf_ref[...])

        acc[...] += _dot_nt(dz_ref[...], w_ref[...])

        @pl.when(k == nk - 1)
        def _():
            dx, dg = _rms_bwd(acc[...], x_ref[...], g_ref[...])
            dx1_ref[...] = dx2_ref[...] + dx
            dg_ref[...] += dg

        @pl.when((i == S // tm - 1) & (k == nk - 1))
        def _():
            _rider_finish(rider[0], rsrc_ref, rdst_ref, send_sems, recv_sems)

    row = pl.BlockSpec((tm, D), lambda i, k: (i, 0))
    vec = pl.BlockSpec((1, D), lambda i, k: (0, 0))
    anywhere = pl.BlockSpec(memory_space=pl.ANY)
    return pl.pallas_call(
        body, name="mix_in_bwd",
        grid=(S // tm, nk),
        in_specs=[pl.BlockSpec((tm, kb), lambda i, k: (i, k)),
                  pl.BlockSpec((D, kb), lambda i, k: (0, k)),
                  pl.BlockSpec((tm, HD), lambda i, k: (i, 0)),
                  pl.BlockSpec((D, HD), lambda i, k: (0, 0)),
                  row, vec, row, anywhere],
        out_specs=[row, vec, anywhere],
        out_shape=[jax.ShapeDtypeStruct((S, D), F32), jax.ShapeDtypeStruct((1, D), F32), _rider_out(rider)],
        scratch_shapes=[pltpu.VMEM((tm, D), F32)] + _rider_sems(rider),
        compiler_params=_params(2),
    )(dz, wcat, dzf, wf, x1, g, dx2, rider[1])


def _scan_rows(blk, reverse):
    n = blk.shape[0]
    row = lax.broadcasted_iota(jnp.int32, blk.shape, 0)
    d = 1
    while d < n:
        if reverse:
            blk = blk + jnp.where(row < n - d, pltpu.roll(blk, n - d, 0), 0.0)
        else:
            blk = blk + jnp.where(row >= d, pltpu.roll(blk, d, 0), 0.0)
        d *= 2
    return blk


def _forget_cumsum(zf, bf, cs):
    S = zf.shape[0]

    def body(zf_ref, bf_ref, c_ref, cb_ref, carry):
        @pl.when(pl.program_id(0) == 0)
        def _():
            carry[...] = jnp.zeros_like(carry)

        x = zf_ref[...] + bf_ref[...]
        logf = jnp.minimum(x, 0.0) - jnp.log1p(jnp.exp(-jnp.abs(x)))
        sc = _scan_rows(logf, False) + carry[...]
        carry[...] = sc[cs - 1:cs, :]
        sc = sc * LOG2E
        c_ref[...] = sc
        for h in range(H):
            cb_ref[h] = jnp.broadcast_to(sc[:, h:h + 1], (cs, HD))

    return pl.pallas_call(
        body, name="forget_cumsum",
        grid=(S // cs,),
        in_specs=[pl.BlockSpec((cs, HD), lambda i: (i, 0)), pl.BlockSpec((1, HD), lambda i: (0, 0))],
        out_specs=[pl.BlockSpec((cs, HD), lambda i: (i, 0)),
                   pl.BlockSpec((H, cs, HD), lambda i: (0, i, 0))],
        out_shape=[jax.ShapeDtypeStruct((S, HD), F32), jax.ShapeDtypeStruct((H, S, HD), F32)],
        scratch_shapes=[pltpu.VMEM((1, HD), F32)],
        compiler_params=_params(1),
    )(zf, bf)


def _forget_bwd(dc, zf, bf, cs):
    S = dc.shape[0]
    nc = S // cs

    def body(dc_ref, zf_ref, bf_ref, dzf_ref, dbf_ref, carry):
        @pl.when(pl.program_id(0) == 0)
        def _():
            carry[...] = jnp.zeros_like(carry)
            dbf_ref[...] = jnp.zeros_like(dbf_ref)

        sc = _scan_rows(dc_ref[...], True) + carry[...]
        carry[...] = sc[0:1, :]
        x = zf_ref[...] + bf_ref[...]
        dz = sc / (1.0 + jnp.exp(x))
        dzf_ref[...] = dz.astype(BF)
        dbf_ref[...] += jnp.sum(dz, axis=0, keepdims=True)

    rev = pl.BlockSpec((cs, HD), lambda i: (nc - 1 - i, 0))
    vec = pl.BlockSpec((1, HD), lambda i: (0, 0))
    return pl.pallas_call(
        body, name="forget_bwd",
        grid=(nc,),
        in_specs=[rev, rev, vec],
        out_specs=[rev, vec],
        out_shape=[jax.ShapeDtypeStruct((S, HD), BF), jax.ShapeDtypeStruct((1, HD), F32)],
        scratch_shapes=[pltpu.VMEM((1, HD), F32)],
        compiler_params=_params(1),
    )(dc, zf, bf)


def _lanes(x, n):
    return x if n == HD else jnp.concatenate([x] * (n // HD), axis=1)


def _causal_mask(i, j, t, rows_are_queries):
    r = lax.broadcasted_iota(jnp.int32, (t, t), 0)
    c = lax.broadcasted_iota(jnp.int32, (t, t), 1)
    if rows_are_queries:
        return (j * t + c) <= (i * t + r)
    return (j * t + r) <= (i * t + c)


def _fox_fwd(z, ccol_b, crow, t):
    S = z.shape[0]
    nq = S // t

    def body(q_ref, kv_ref, cc_ref, cr_ref, o_ref, lse_ref, m_s, acc_s, s_a, s_b):
        i = pl.program_id(1)
        ct = cc_ref[...]
        ones = jnp.ones((t, HD), BF)
        m_s[...] = jnp.full_like(m_s, NEG)
        acc_s[...] = jnp.zeros_like(acc_s)

        def scores(j, s_ref):
            off = pl.multiple_of(j * t, t)
            s_ref[...] = _dot_nt(q_ref[...], kv_ref[pl.ds(off, t), :HD]) - cr_ref[pl.ds(j, 1), :]

        def consume(j, s_ref, masked):
            off = pl.multiple_of(j * t, t)
            v1 = jnp.concatenate([kv_ref[pl.ds(off, t), HD:], ones], axis=1)
            s = s_ref[...]
            if masked:
                s = jnp.where(_causal_mask(i, j, t, True), s, NEG)
            m_old = m_s[...]
            m_new = jnp.maximum(m_old, jnp.max(s, axis=1, keepdims=True))
            p = jnp.exp2(s - _lanes(m_new, t))
            alpha = jnp.exp2(m_old - m_new)
            acc_s[...] = _lanes(alpha, 2 * HD) * acc_s[...] + _dot(p.astype(BF), v1)
            m_s[...] = m_new

        scores(0, s_a)

        def pair(jj, carry):
            j = 2 * jj
            scores(j + 1, s_b)
            consume(j, s_a, False)
            scores(j + 2, s_a)
            consume(j + 1, s_b, False)
            return carry

        lax.fori_loop(0, i // 2, pair, 0)

        @pl.when(i % 2 == 0)
        def _():
            consume(i, s_a, True)

        @pl.when(i % 2 == 1)
        def _():
            scores(i, s_b)
            consume(i - 1, s_a, False)
            consume(i, s_b, True)

        l = acc_s[:, HD:]
        o_ref[...] = (acc_s[:, :HD] / l).astype(BF)
        lse_ref[...] = m_s[...] + ct + jnp.log2(l)

    return pl.pallas_call(
        body, name="fox_fwd",
        grid=(H, nq),
        in_specs=[pl.BlockSpec((t, HD), lambda h, i: (i, h)),
                  pl.BlockSpec((S, 2 * HD), lambda h, i: (0, 4 + h)),
                  pl.BlockSpec((None, t, HD), lambda h, i: (h, i, 0)),
                  pl.BlockSpec((None, nq, t), lambda h, i: (h, 0, 0))],
        out_specs=[pl.BlockSpec((t, HD), lambda h, i: (i, h)),
                   pl.BlockSpec((None, t, HD), lambda h, i: (h, i, 0))],
        out_shape=[jax.ShapeDtypeStruct((S, D), BF), jax.ShapeDtypeStruct((H, S, HD), F32)],
        scratch_shapes=[pltpu.VMEM((t, HD), F32), pltpu.VMEM((t, 2 * HD), F32),
                        pltpu.VMEM((t, t), F32), pltpu.VMEM((t, t), F32)],
        compiler_params=_params(2),
    )(z, z, ccol_b, crow)


def _fox_bwd(z, do, o, lse_b, ccol_b, crow, dz, t, rider):
    S = z.shape[0]
    nq = S // t

    def body(q_ref, kv_ref, do_ref, o_ref, lse_ref, cc_ref, cr_ref, dz_in, rsrc_ref,
             dq_ref, dkt_ref, dvt_ref, dck_ref, dcq_ref, rdst_ref, acc_s, r_s, send_sems, recv_sems):
        del dz_in
        i = pl.program_id(1)

        @pl.when((pl.program_id(0) == 0) & (i == 0))
        def _():
            _rider_start(rider[0], rsrc_ref, rdst_ref, send_sems, recv_sems)

        @pl.when(i == 0)
        def _():
            dkt_ref[...] = jnp.zeros_like(dkt_ref)
            dvt_ref[...] = jnp.zeros_like(dvt_ref)
            dck_ref[...] = jnp.zeros_like(dck_ref)

        q = q_ref[...]
        dout = do_ref[...]
        qt = jnp.transpose(q.astype(F32)).astype(BF)
        dot_ = jnp.transpose(dout.astype(F32)).astype(BF)
        off_t = _lanes(lse_ref[...] - cc_ref[...], t)
        delta = jnp.sum(dout.astype(F32) * o_ref[...].astype(F32), axis=1, keepdims=True)
        delta = _lanes(jnp.broadcast_to(delta, (t, HD)), t)
        acc_s[...] = jnp.zeros_like(acc_s)
        r_s[...] = jnp.zeros_like(r_s)

        def step(j, masked):
            off = pl.multiple_of(j * t, t)
            k = kv_ref[pl.ds(off, t), :HD]
            v = kv_ref[pl.ds(off, t), HD:]
            p = jnp.exp2(_dot_nt(q, k) - cr_ref[pl.ds(j, 1), :] - off_t)
            if masked:
                p = jnp.where(_causal_mask(i, j, t, True), p, 0.0)
            ds = p * (_dot_nt(dout, v) - delta)
            dsb = ds.astype(BF)
            acc_s[...] += _dot(dsb, k)
            dkt_ref[j] += _dot(qt, dsb)
            dvt_ref[j] += _dot(dot_, p.astype(BF))
            dck_ref[pl.ds(j, 1), :] += jnp.sum(ds, axis=0, keepdims=True)
            r_s[...] += jnp.sum(ds, axis=1, keepdims=True)

        def full_step(j, carry):
            step(j, False)
            return carry

        lax.fori_loop(0, i, full_step, 0)
        step(i, True)
        dq_ref[...] = (acc_s[...] * SCALE).astype(BF)
        dcq_ref[...] = jnp.transpose(r_s[...])[0:1, :]

        @pl.when((pl.program_id(0) == H - 1) & (i == nq - 1))
        def _():
            _rider_finish(rider[0], rsrc_ref, rdst_ref, send_sems, recv_sems)

    qspec = pl.BlockSpec((t, HD), lambda h, i: (i, h))
    bspec = pl.BlockSpec((None, t, HD), lambda h, i: (h, i, 0))
    rows = pl.BlockSpec((None, nq, t), lambda h, i: (h, 0, 0))
    tspec = pl.BlockSpec((None, nq, HD, t), lambda h, i: (h, 0, 0, 0))
    tshape = jax.ShapeDtypeStruct((H, nq, HD, t), F32)
    anywhere = pl.BlockSpec(memory_space=pl.ANY)
    return pl.pallas_call(
        body, name="fox_bwd",
        grid=(H, nq),
        in_specs=[qspec,
                  pl.BlockSpec((S, 2 * HD), lambda h, i: (0, 4 + h)),
                  qspec, qspec, bspec, bspec, rows, anywhere, anywhere],
        out_specs=[qspec, tspec, tspec, rows, pl.BlockSpec((None, None, 1, t), lambda h, i: (h, i, 0, 0)),
                   anywhere],
        out_shape=[jax.ShapeDtypeStruct((S, ZW), BF), tshape, tshape,
                   jax.ShapeDtypeStruct((H, nq, t), F32), jax.ShapeDtypeStruct((H, nq, 1, t), F32),
                   _rider_out(rider)],
        scratch_shapes=[pltpu.VMEM((t, HD), F32), pltpu.VMEM((t, HD), F32)] + _rider_sems(rider),
        input_output_aliases={7: 0},
        compiler_params=_params(2),
    )(z, z, do, o, lse_b, ccol_b, crow, dz, rider[1])


def _fox_bwd_finish(dkt, dvt, dz, t):
    nq = dkt.shape[1]
    S = nq * t

    def body(dkt_ref, dvt_ref, dz_in, dkv_ref):
        del dz_in
        for j in range(nq):
            rows = slice(j * t, (j + 1) * t)
            dkv_ref[rows, :HD] = (jnp.transpose(dkt_ref[j]) * LN2).astype(BF)
            dkv_ref[rows, HD:] = jnp.transpose(dvt_ref[j]).astype(BF)

    tspec = pl.BlockSpec((None, nq, HD, t), lambda h: (h, 0, 0, 0))
    return pl.pallas_call(
        body, name="fox_bwd_finish",
        grid=(H,),
        in_specs=[tspec, tspec, pl.BlockSpec(memory_space=pl.ANY)],
        out_specs=pl.BlockSpec((S, 2 * HD), lambda h: (0, 4 + h)),
        out_shape=jax.ShapeDtypeStruct((S, ZW), BF),
        input_output_aliases={2: 0},
        compiler_params=_params(1),
    )(dkt, dvt, dz)


def _sgu_mask(transposed):
    r = lax.broadcasted_iota(jnp.int32, (L, L), 0)
    c = lax.broadcasted_iota(jnp.int32, (L, L), 1)
    if transposed:
        return (r // CHUNK) <= (c // CHUNK)
    return (c // CHUNK) <= (r // CHUNK)


def _ln_group(vs, lng, lnb):
    mu = jnp.mean(vs, axis=-1, keepdims=True)
    xc = vs - mu
    rstd = lax.rsqrt(jnp.mean(xc * xc, axis=-1, keepdims=True) + LN_EPS)
    xhat = xc * rstd
    return xhat, rstd, xhat * lng + lnb


def _mix_out_fwd(z, oa, ln_g, ln_b, ws, bst, wout, x1, g_post, tm):
    S = z.shape[0]
    nw = tm // L

    def body(u_ref, sv_ref, ga_ref, gb_ref, oa_ref, lng_ref, lnb_ref, ws_ref, bst_ref, wout_ref, x1_ref, gp_ref,
             mg_ref, y_ref, x2_ref, mg_s):
        mask = _sgu_mask(False)
        for g in range(G):
            cols = slice(g * L, (g + 1) * L)
            wm = jnp.where(mask, ws_ref[g], 0.0).astype(BF)
            bcol = bst_ref[:, g:g + 1]
            lng = lng_ref[:, cols]
            lnb = lnb_ref[:, cols]
            for w in range(nw):
                rows = slice(w * L, (w + 1) * L)
                vs = _gelu(sv_ref[rows, cols].astype(F32))
                _, _, vn = _ln_group(vs, lng, lnb)
                mixed = _dot(wm, vn.astype(BF)) + bcol
                ob = _gelu(u_ref[rows, cols].astype(F32)) * mixed
                mg = (_sigmoid(ga_ref[rows, cols].astype(F32)) * oa_ref[rows, cols].astype(F32)
                      + _sigmoid(gb_ref[rows, cols].astype(F32)) * ob)
                mg_s[rows, cols] = mg.astype(BF)
        mgb = mg_s[...]
        mg_ref[...] = mgb
        y = _dot(mgb, wout_ref[...])
        y_ref[...] = y
        x2_ref[...] = x1_ref[...] + _rms(y, gp_ref[...])

    row = pl.BlockSpec((tm, D), lambda i: (i, 0))
    vec = pl.BlockSpec((1, D), lambda i: (0, 0))

    def zcol(kb):
        return pl.BlockSpec((tm, D), lambda i: (i, kb))

    return pl.pallas_call(
        body, name="mix_out_fwd",
        grid=(S // tm,),
        in_specs=[zcol(3), zcol(4), zcol(5), zcol(6), row, vec, vec,
                  pl.BlockSpec((G, L, L), lambda i: (0, 0, 0)),
                  pl.BlockSpec((L, HD), lambda i: (0, 0)),
                  pl.BlockSpec((D, D), lambda i: (0, 0)),
                  row, vec],
        out_specs=[row, row, row],
        out_shape=[jax.ShapeDtypeStruct((S, D), BF),
                   jax.ShapeDtypeStruct((S, D), F32),
                   jax.ShapeDtypeStruct((S, D), F32)],
        scratch_shapes=[pltpu.VMEM((tm, D), BF)],
        compiler_params=_params(1),
    )(z, z, z, z, oa, ln_g, ln_b, ws, bst, wout, x1, g_post)


def _mix_out_bwd(dx2, y2, g_post, wout, z, oa, ln_g, ln_b, ws, wst, bst, tm, rider):
    S = z.shape[0]
    nw = tm // L

    def body(dx2_ref, y_ref, gp_ref, wout_ref, u_ref, sv_ref, ga_ref, gb_ref, oa_ref, lng_ref, lnb_ref,
             ws_ref, wst_ref, bst_ref, q_ref,
             dz_ref, dy_ref, doa_ref, dgp_ref, dlng_ref, dlnb_ref, dws_ref, dbst_ref, t_ref,
             dzg_s, dm_s, send_sems, recv_sems):
        i = pl.program_id(0)
        c = pl.program_id(1)

        @pl.when((i == 0) & (c == 0))
        def _():
            _rider_start(rider[0], q_ref, t_ref, send_sems, recv_sems)
            dgp_ref[...] = jnp.zeros_like(dgp_ref)
            dlng_ref[...] = jnp.zeros_like(dlng_ref)
            dlnb_ref[...] = jnp.zeros_like(dlnb_ref)
            dws_ref[...] = jnp.zeros_like(dws_ref)
            dbst_ref[...] = jnp.zeros_like(dbst_ref)

        @pl.when(c == 0)
        def _():
            dy, dg = _rms_bwd(dx2_ref[...], y_ref[...], gp_ref[...])
            dyb = dy.astype(BF)
            dy_ref[...] = dyb
            dgp_ref[...] += dg
            dm_s[...] = _dot_nt(dyb, wout_ref[...])
            mask = _sgu_mask(False)
            mask_t = _sgu_mask(True)
            lane = lax.broadcasted_iota(jnp.int32, (L, HD), 1)
            for g in range(G):
                cols = slice(g * L, (g + 1) * L)
                wm = jnp.where(mask, ws_ref[g], 0.0).astype(BF)
                wmt = jnp.where(mask_t, wst_ref[g], 0.0).astype(BF)
                bcol = bst_ref[:, g:g + 1]
                lng = lng_ref[:, cols]
                lnb = lnb_ref[:, cols]
                dws_g = jnp.zeros((L, L), F32)
                dbs_g = jnp.zeros((L, 1), F32)
                dlng_g = jnp.zeros((1, L), F32)
                dlnb_g = jnp.zeros((1, L), F32)
                for w in range(nw):
                    rows = slice(w * L, (w + 1) * L)
                    dm = dm_s[rows, cols]
                    vs, dvs_dz = _gelu_parts(sv_ref[rows, cols].astype(F32))
                    xhat, rstd, vn = _ln_group(vs, lng, lnb)
                    vnb = vn.astype(BF)
                    mixed = _dot(wm, vnb) + bcol
                    u, du_dz = _gelu_parts(u_ref[rows, cols].astype(F32))
                    sga = _sigmoid(ga_ref[rows, cols].astype(F32))
                    sgb = _sigmoid(gb_ref[rows, cols].astype(F32))
                    oav = oa_ref[rows, cols].astype(F32)
                    ob = u * mixed
                    doa_ref[rows, cols] = (dm * sga).astype(BF)
                    dzg_s[2, rows, cols] = (dm * oav * sga * (1.0 - sga)).astype(BF)
                    dzg_s[3, rows, cols] = (dm * ob * sgb * (1.0 - sgb)).astype(BF)
                    dob = dm * sgb
                    dzg_s[0, rows, cols] = (dob * mixed * du_dz).astype(BF)
                    dmixed = dob * u
                    dmb = dmixed.astype(BF)
                    dbs_g += jnp.sum(dmixed, axis=1, keepdims=True)
                    dws_g += _dot_nt(dmb, vnb)
                    dvn = _dot(wmt, dmb)
                    dlng_g += jnp.sum(dvn * xhat, axis=0, keepdims=True)
                    dlnb_g += jnp.sum(dvn, axis=0, keepdims=True)
                    dxh = dvn * lng
                    dvs = rstd * (dxh - jnp.mean(dxh, axis=-1, keepdims=True)
                                  - xhat * jnp.mean(dxh * xhat, axis=-1, keepdims=True))
                    dzg_s[1, rows, cols] = (dvs * dvs_dz).astype(BF)
                dws_ref[g] += jnp.where(mask, dws_g, 0.0)
                dbst_ref[...] += jnp.where(lane == g, dbs_g, 0.0)
                dlng_ref[:, cols] += dlng_g
                dlnb_ref[:, cols] += dlnb_g

        dz_ref[...] = dzg_s[c]

        @pl.when((i == S // tm - 1) & (c == 3))
        def _():
            _rider_finish(rider[0], q_ref, t_ref, send_sems, recv_sems)

    row = pl.BlockSpec((tm, D), lambda i, c: (i, 0))
    vec = pl.BlockSpec((1, D), lambda i, c: (0, 0))
    wsspec = pl.BlockSpec((G, L, L), lambda i, c: (0, 0, 0))
    bspec = pl.BlockSpec((L, HD), lambda i, c: (0, 0))
    anywhere = pl.BlockSpec(memory_space=pl.ANY)

    def zcol(kb):
        return pl.BlockSpec((tm, D), lambda i, c: (i, kb))

    return pl.pallas_call(
        body, name="mix_out_bwd",
        grid=(S // tm, 4),
        in_specs=[row, row, vec, pl.BlockSpec((D, D), lambda i, c: (0, 0)),
                  zcol(3), zcol(4), zcol(5), zcol(6), row, vec, vec, wsspec, wsspec, bspec, anywhere],
        out_specs=[pl.BlockSpec((tm, D), lambda i, c: (i, 3 + c)),
                   row, row, vec, vec, vec, wsspec, bspec, anywhere],
        out_shape=[jax.ShapeDtypeStruct((S, ZW), BF),
                   jax.ShapeDtypeStruct((S, D), BF),
                   jax.ShapeDtypeStruct((S, D), BF),
                   jax.ShapeDtypeStruct((1, D), F32),
                   jax.ShapeDtypeStruct((1, D), F32),
                   jax.ShapeDtypeStruct((1, D), F32),
                   jax.ShapeDtypeStruct((G, L, L), F32),
                   jax.ShapeDtypeStruct((L, HD), F32),
                   _rider_out(rider)],
        scratch_shapes=[pltpu.VMEM((4, tm, D), BF), pltpu.VMEM((tm, D), F32)] + _rider_sems(rider),
        compiler_params=_params(2),
    )(dx2, y2, g_post, wout, z, z, z, z, oa, ln_g, ln_b, ws, wst, bst, rider[1])


def _adamw_update(w_ref, g_ref, m_ref, v_ref, d_ref, nm_ref, nv_ref):
    gv = g_ref[...]
    m_new = ADAM_B1 * m_ref[...] + (1.0 - ADAM_B1) * gv
    v_new = ADAM_B2 * v_ref[...] + (1.0 - ADAM_B2) * (gv * gv)
    m_hat = m_new / (1.0 - ADAM_B1 ** ADAM_STEP)
    v_hat = v_new / (1.0 - ADAM_B2 ** ADAM_STEP)
    d_ref[...] = -ADAM_LR * (m_hat / (jnp.sqrt(v_hat) + ADAM_EPS) + ADAM_WD * w_ref[...])
    nm_ref[...] = m_new
    nv_ref[...] = v_new


def _adamw(w, g, m, v, tr):
    R, C = w.shape
    spec = pl.BlockSpec((tr, C), lambda i: (i, 0))
    shp = jax.ShapeDtypeStruct((R, C), F32)
    return pl.pallas_call(
        functools.partial(_adamw_update), name="adamw",
        grid=(R // tr,),
        in_specs=[spec] * 4, out_specs=[spec] * 3, out_shape=[shp] * 3,
        compiler_params=_params(1),
    )(w, g, m, v)


ADAMW_GROUP_STEPS = 16


def _adamw_group(tensors, riders):
    n, nr, steps = len(tensors), len(riders), ADAMW_GROUP_STEPS

    def body(*refs):
        ins, rsrc = refs[:4 * n], refs[4 * n:4 * n + nr]
        outs, rdst = refs[4 * n + nr:7 * n + nr], refs[7 * n + nr:7 * n + 2 * nr]
        sems = refs[7 * n + 2 * nr:]
        i = pl.program_id(0)

        @pl.when(i == 0)
        def _():
            for k, (kind, _) in enumerate(riders):
                _rider_start(kind, rsrc[k], rdst[k], sems[2 * k], sems[2 * k + 1])

        for k in range(n):
            _adamw_update(*ins[4 * k:4 * k + 4], *outs[3 * k:3 * k + 3])

        @pl.when(i == steps - 1)
        def _():
            for k, (kind, _) in enumerate(riders):
                _rider_finish(kind, rsrc[k], rdst[k], sems[2 * k], sems[2 * k + 1])

    anywhere = pl.BlockSpec(memory_space=pl.ANY)
    in_specs, out_specs, out_shape, args = [], [], [], []
    for w, gbuf, row0, m, v in tensors:
        tr = w.shape[0] // steps
        spec = pl.BlockSpec((tr, D), lambda i: (i, 0))
        in_specs += [spec, pl.BlockSpec((tr, D), functools.partial(lambda i, b: (b + i, 0), b=row0 // tr)), spec, spec]
        out_specs += [spec] * 3
        out_shape += [jax.ShapeDtypeStruct(w.shape, F32)] * 3
        args += [w, gbuf, m, v]
    scratch = []
    for rider in riders:
        in_specs.append(anywhere)
        out_specs.append(anywhere)
        out_shape.append(_rider_out(rider))
        scratch += _rider_sems(rider)
        args.append(rider[1])
    res = pl.pallas_call(
        body, name="adamw_group",
        grid=(steps,),
        in_specs=in_specs, out_specs=out_specs, out_shape=out_shape, scratch_shapes=scratch,
        compiler_params=_params(1),
    )(*args)
    return [tuple(res[3 * k:3 * k + 3]) for k in range(n)], list(res[3 * n:])


def _mesh_pos():
    return lax.axis_index("x"), lax.axis_index("y"), lax.axis_index("c")


def _half(c, rows):
    return pl.ds(pl.multiple_of(c * rows, 16), rows)


def _other_chips(x, y):
    return [(1 - x, y), (x, 1 - y), (1 - x, 1 - y)]


def _remote(k, src, dst, to, send_sems, recv_sems):
    return pltpu.make_async_remote_copy(src_ref=src, dst_ref=dst, send_sem=send_sems.at[k],
                                        recv_sem=recv_sems.at[k], device_id=to, device_id_type=MESH)


def _gather_start(wp_ref, g_ref, send_sems, recv_sems):
    x, y, c = _mesh_pos()
    mine = _half(c, wp_ref.shape[0] // 2)
    for k, (px, py) in enumerate(_other_chips(x, y)):
        _remote(k, wp_ref.at[mine], g_ref.at[2 * x + y, mine], (px, py, c), send_sems, recv_sems).start()


def _gather_forward(wp_ref, g_ref, send_sems, recv_sems):
    x, y, c = _mesh_pos()
    sibling = (x, y, 1 - c)
    mine = _half(c, wp_ref.shape[0] // 2)
    for k, (px, py) in enumerate(_other_chips(x, y)):
        land = g_ref.at[2 * px + py, mine]
        _remote(k, land, land, (px, py, c), send_sems, recv_sems).wait_recv()
        _remote(3 + k, land, land, sibling, send_sems, recv_sems).start()


def _gather_finish(wp_ref, g_ref, send_sems, recv_sems):
    x, y, c = _mesh_pos()
    sibling = (x, y, 1 - c)
    rows = wp_ref.shape[0] // 2
    mine, other = _half(c, rows), _half(1 - c, rows)
    chips = _other_chips(x, y)
    for k, (px, py) in enumerate(chips):
        land = g_ref.at[2 * px + py, other]
        _remote(3 + k, land, land, sibling, send_sems, recv_sems).wait_recv()
    for k, (px, py) in enumerate(chips):
        land = g_ref.at[2 * px + py, mine]
        _remote(k, wp_ref.at[mine], g_ref.at[2 * x + y, mine], (px, py, c), send_sems, recv_sems).wait_send()
        _remote(3 + k, land, land, sibling, send_sems, recv_sems).wait_send()


def _place_own_shard(g, wp):
    x, y, _ = _mesh_pos()
    return lax.dynamic_update_index_in_dim(g, wp, 2 * x + y, 0)


def _all_gather_weights(wp):
    def body(wp_ref, g_ref, send_sems, recv_sems):
        _gather_start(wp_ref, g_ref, send_sems, recv_sems)
        _gather_forward(wp_ref, g_ref, send_sems, recv_sems)
        _gather_finish(wp_ref, g_ref, send_sems, recv_sems)

    g = pl.pallas_call(
        body, name="all_gather_weights",
        in_specs=[pl.BlockSpec(memory_space=pl.ANY)],
        out_specs=pl.BlockSpec(memory_space=pl.ANY),
        out_shape=jax.ShapeDtypeStruct((NSH,) + wp.shape, wp.dtype),
        scratch_shapes=[pltpu.SemaphoreType.DMA((6,)), pltpu.SemaphoreType.DMA((6,))],
        compiler_params=pltpu.CompilerParams(has_side_effects=True),
    )(wp)
    return _place_own_shard(g, wp)


def _scatter_copies(q_ref, t_ref, send_sems, recv_sems):
    x, y, c = _mesh_pos()
    return [_remote(k, q_ref.at[2 * px + py], t_ref.at[k], (px, py, c), send_sems, recv_sems)
            for k, (px, py) in enumerate(_other_chips(x, y))]


_FLIPS = [(fx, fy, fc) for fx in (0, 1) for fy in (0, 1) for fc in (0, 1)][1:]


def _rider_copies(kind, src_ref, dst_ref, send_sems, recv_sems):
    if kind == "scatter":
        return _scatter_copies(src_ref, dst_ref, send_sems, recv_sems)
    x, y, c = _mesh_pos()
    if kind == "broadcast":
        return [_remote(k, src_ref, dst_ref.at[4 * x + 2 * y + c], (x ^ fx, y ^ fy, c ^ fc), send_sems, recv_sems)
                for k, (fx, fy, fc) in enumerate(_FLIPS)]
    rows = src_ref.shape[1] // 2
    return [_remote(0, src_ref.at[:, _half(1 - c, rows)], dst_ref, (x, y, 1 - c), send_sems, recv_sems)]


def _place_own_block(sm, block):
    x, y, c = _mesh_pos()
    return lax.dynamic_update_index_in_dim(sm, block, 4 * x + 2 * y + c, 0)


def _rider_start(kind, src_ref, dst_ref, send_sems, recv_sems):
    for cp in _rider_copies(kind, src_ref, dst_ref, send_sems, recv_sems):
        cp.start()


def _rider_finish(kind, src_ref, dst_ref, send_sems, recv_sems):
    for cp in _rider_copies(kind, src_ref, dst_ref, send_sems, recv_sems):
        cp.wait()


def _rider_out(rider):
    kind, a = rider
    if kind == "scatter":
        return jax.ShapeDtypeStruct((3,) + a.shape[1:], a.dtype)
    if kind == "broadcast":
        return jax.ShapeDtypeStruct((NDEV,) + a.shape, a.dtype)
    return jax.ShapeDtypeStruct((a.shape[0], a.shape[1] // 2) + a.shape[2:], a.dtype)


def _rider_sems(rider):
    n = {"scatter": 3, "broadcast": 7, "exchange": 1}[rider[0]]
    return [pltpu.SemaphoreType.DMA((n,)), pltpu.SemaphoreType.DMA((n,))]


def _pair_exchange(p):
    rows = p.shape[1] // 2

    def body(p_ref, r_ref, send_sem, recv_sem):
        x, y, c = _mesh_pos()
        cp = pltpu.make_async_remote_copy(src_ref=p_ref.at[:, _half(1 - c, rows)], dst_ref=r_ref, send_sem=send_sem,
                                          recv_sem=recv_sem, device_id=(x, y, 1 - c), device_id_type=MESH)
        cp.start()
        cp.wait()

    return pl.pallas_call(
        body, name="pair_exchange",
        in_specs=[pl.BlockSpec(memory_space=pl.ANY)],
        out_specs=pl.BlockSpec(memory_space=pl.ANY),
        out_shape=jax.ShapeDtypeStruct((NSH, rows, D), F32),
        scratch_shapes=[pltpu.SemaphoreType.DMA, pltpu.SemaphoreType.DMA],
        compiler_params=pltpu.CompilerParams(has_side_effects=True),
    )(p)


def _pair_add(p, r, nb):
    rows = r.shape[1]
    tr = rows // nb

    def body(p_ref, r_ref, q_ref):
        q_ref[...] = (p_ref[...] + r_ref[...]).astype(BF)

    return pl.pallas_call(
        body, name="pair_add", grid=(NSH, nb),
        in_specs=[pl.BlockSpec((None, tr, D), lambda j, i: (j, lax.axis_index("c") * nb + i, 0)),
                  pl.BlockSpec((None, tr, D), lambda j, i: (j, i, 0))],
        out_specs=pl.BlockSpec((None, tr, D), lambda j, i: (j, i, 0)),
        out_shape=jax.ShapeDtypeStruct((NSH, rows, D), BF),
        compiler_params=_params(2),
    )(p, r)


def _shard_sum(p, r, t, nb):
    rows = r.shape[1]
    tr = rows // nb

    def shard():
        return 2 * lax.axis_index("x") + lax.axis_index("y")

    def body(p_ref, r_ref, t_ref, o_ref):
        s = p_ref[...] + r_ref[...]
        for k in range(3):
            s = s + t_ref[k].astype(F32)
        o_ref[...] = s

    return pl.pallas_call(
        body, name="shard_sum", grid=(nb,),
        in_specs=[pl.BlockSpec((None, tr, D), lambda i: (shard(), lax.axis_index("c") * nb + i, 0)),
                  pl.BlockSpec((None, tr, D), lambda i: (shard(), i, 0)),
                  pl.BlockSpec((3, tr, D), lambda i: (0, i, 0))],
        out_specs=pl.BlockSpec((tr, D), lambda i: (i, 0)),
        out_shape=jax.ShapeDtypeStruct((rows, D), F32),
        compiler_params=_params(1),
    )(p, r, t)


def _small_sum(sm):
    def body(sm_ref, o_ref):
        s = sm_ref[0]
        for k in range(1, NDEV):
            s = s + sm_ref[k]
        o_ref[...] = s

    return pl.pallas_call(
        body, name="small_sum",
        in_specs=[pl.BlockSpec(memory_space=pltpu.VMEM)],
        out_specs=pl.BlockSpec(memory_space=pltpu.VMEM),
        out_shape=jax.ShapeDtypeStruct(sm.shape[1:], F32),
    )(sm)


def _pair_gather(halves):
    n = len(halves)

    def body(*refs):
        gh_refs, o_refs, (send_sems, recv_sems) = refs[:n], refs[n:2 * n], refs[2 * n:]
        x, y, c = _mesh_pos()
        sibling = (x, y, 1 - c)
        for g in range(n):
            rows = gh_refs[g].shape[0]
            _remote(g, gh_refs[g], o_refs[g].at[_half(c, rows)], sibling, send_sems, recv_sems).start()
        for g in range(n):
            rows = gh_refs[g].shape[0]
            _remote(g, gh_refs[g], o_refs[g].at[_half(c, rows)], sibling, send_sems, recv_sems).wait_send()
            _remote(g, gh_refs[g], o_refs[g].at[_half(1 - c, rows)], sibling, send_sems, recv_sems).wait_recv()

    anywhere = pl.BlockSpec(memory_space=pl.ANY)
    outs = pl.pallas_call(
        body, name="pair_gather",
        in_specs=[anywhere] * n, out_specs=[anywhere] * n,
        out_shape=[jax.ShapeDtypeStruct((2 * h.shape[0], D), F32) for h in halves],
        scratch_shapes=[pltpu.SemaphoreType.DMA((n,)), pltpu.SemaphoreType.DMA((n,))],
        compiler_params=pltpu.CompilerParams(has_side_effects=True),
    )(*halves)
    c = lax.axis_index("c")
    return [lax.dynamic_update_slice_in_dim(o, h, c * h.shape[0], 0) for o, h in zip(outs, halves)]


def _pad_cols(a, n):
    return jnp.pad(a, ((0, 0), (0, n - a.shape[1])))


def _split_w_in(w_in_full):
    q, k, v = w_in_full[:, :D], w_in_full[:, D:2 * D], w_in_full[:, 2 * D:3 * D]
    f = w_in_full[:, 3 * D:3 * D + H]
    gates = w_in_full[:, 3 * D + H:]
    kv = jnp.stack([k.reshape(D, H, HD), v.reshape(D, H, HD)], axis=2).reshape(D, 2 * D)
    return jnp.concatenate([q, kv, gates], axis=1), _pad_cols(f, HD)


def _merge_w_in_grad(dwcat, dwf):
    kv = dwcat[:, D:3 * D].reshape(D, H, 2, HD)
    return jnp.concatenate([dwcat[:, :D], kv[:, :, 0].reshape(D, D), kv[:, :, 1].reshape(D, D),
                            dwf[:, :H], dwcat[:, 3 * D:]], axis=1)


def _ffn_weight_grads(h, da, db, act, dy, bt):
    g = _mm_tn(h, da, D, D, bt, into=(None, FFN_ROWS, 0, "cols"))
    g = _mm_tn(h, db, D, D, bt, into=(g, FFN_ROWS, 1, "cols"))
    return _mm_tn(act, dy, D, D, bt, into=(g, FFN_ROWS, 2, "rows"))


def _train_step(x, target, wp1, wp2, small, adam, tm, t_attn):
    S = x.shape[0]
    g1pre, g1post = small["ffn1_pre_g"], small["ffn1_post_g"]
    gmpre, gmpost = small["mix_pre_g"], small["mix_post_g"]
    g2pre, g2post = small["ffn2_pre_g"], small["ffn2_post_g"]
    ln_g, ln_b = small["sgu_ln_g"], small["sgu_ln_b"]
    ws = small["sgu_w_s"][0]
    wst = jnp.swapaxes(ws, 1, 2)
    bst = _pad_cols(small["sgu_b_s"][0].T, HD)
    bf = _pad_cols(small["b_forget"], HD)

    w1 = _all_gather_weights(wp1)
    h1, a1, b1, y1, x1, h2, w2 = _ffn_fwd(x, g1pre, w1, g1post, tm, ("norm", gmpre), gather=wp2)
    wout = w2[:, FFN_ROWS:FFN_ROWS + 256, :].reshape(D, D)
    r0 = FFN_ROWS + 256
    w_in_full = jnp.concatenate(
        [blk for j in range(NSH) for blk in (w2[j, r0:r0 + D], w2[j, r0 + D:r0 + 2 * D, :WIN_SH - D])], axis=1)
    wcat, wf = _split_w_in(w_in_full)
    z = _mm(h2, wcat, min(1024, S), D, BF, first_block_scale=SCALE * LOG2E)
    zf = _mm(h2, wf, tm, HD, F32)
    cs = min(512, S)
    c, ccol_b = _forget_cumsum(zf, bf, cs)
    crow = jnp.transpose(c[:, :H]).reshape(H, S // t_attn, t_attn)
    oa, lse_b = _fox_fwd(z, ccol_b, crow, t_attn)
    merged, y2, x2 = _mix_out_fwd(z, oa, ln_g, ln_b, ws, bst, wout, x1, gmpost, tm)
    h3, a3, b3, y3, _, dx3, loss_acc = _ffn_fwd(x2, g2pre, w2, g2post, tm, ("loss", target))
    loss = loss_acc[0, 0]

    dy3, da3, db3, act3, dx2, dg2post, dg2pre = _ffn_bwd(dx3, y3, g2post, a3, b3, w2, x2, g2pre, tm)
    bt = min(2048, S)
    g_ffn2 = _ffn_weight_grads(h3, da3, db3, act3, dy3, bt)

    dz, dy2, doa, dgmpost, dlng, dlnb, dws, dbst, r_ffn2 = _mix_out_bwd(
        dx2, y2, gmpost, wout, z, oa, ln_g, ln_b, ws, wst, bst, tm, ("exchange", g_ffn2))
    q_ffn2 = _pair_add(g_ffn2, r_ffn2, 3)
    g_mix = _mm_tn(merged, dy2, 256, D, bt, into=(None, MIX_ROWS, 0, "rows"))
    dz, dkt, dvt, dc_keys, dc_queries, t_ffn2 = _fox_bwd(z, doa, oa, lse_b, ccol_b, crow, dz, t_attn,
                                                         ("scatter", q_ffn2))
    dz = _fox_bwd_finish(dkt, dvt, dz, t_attn)
    dc = _pad_cols(jnp.transpose(dc_queries.reshape(H, S) - dc_keys.reshape(H, S)), HD)
    dzf, dbf = _forget_bwd(dc, zf, bf, cs)
    dwcat = _mm_tn(h2, dz, D, D, bt)
    dwf = _mm_tn(h2, dzf, D, HD, bt)
    dwin = _merge_w_in_grad(dwcat, dwf)
    dwin_a = jnp.stack([dwin[:, j * WIN_SH:j * WIN_SH + D] for j in range(NSH)])
    dwin_b = jnp.stack([_pad_cols(dwin[:, j * WIN_SH + D:(j + 1) * WIN_SH], D) for j in range(NSH)])
    g_mix = lax.dynamic_update_slice(g_mix, jnp.concatenate([dwin_a, dwin_b], axis=1), (0, 256, 0))
    dx1, dgmpre, r_mix = _mix_in_bwd(dz, wcat, dzf, wf, x1, gmpre, dx2, tm, ("exchange", g_mix))
    q_mix = _pair_add(g_mix, r_mix, 3)

    small_early = _pack_small({
        "mix_pre_g": dgmpre, "mix_post_g": dgmpost, "ffn2_pre_g": dg2pre, "ffn2_post_g": dg2post,
        "sgu_ln_g": dlng, "sgu_ln_b": dlnb, "sgu_w_s": dws[None], "sgu_b_s": jnp.transpose(dbst[:, :G])[None],
        "b_forget": dbf[:, :H]}, _SMALL_EARLY)
    dy1, da1, db1, act1, dx, dg1post, dg1pre, t_mix, sm_early = _ffn_bwd(
        dx1, y1, g1post, a1, b1, w1, x, g1pre, tm, riders=(("scatter", q_mix), ("broadcast", small_early)))
    sm_early = _place_own_block(sm_early, small_early)
    g_gu = _mm_tn(h1, da1, D, D, bt, into=(None, 2 * D, 0, "cols"))
    g_gu = _mm_tn(h1, db1, D, D, bt, into=(g_gu, 2 * D, 1, "cols"))
    r_gu = _pair_exchange(g_gu)
    q_gu = _pair_add(g_gu, r_gu, 2)
    g_dn, t_gu = _mm_tn(act1, dy1, D, D, bt, into=(None, D, 0, "rows"), rider=("scatter", q_gu))
    r_dn = _pair_exchange(g_dn)
    q_dn = _pair_add(g_dn, r_dn, 2)
    small_late = _pack_small({"ffn1_pre_g": dg1pre, "ffn1_post_g": dg1post}, _SMALL_LATE)

    f_gu, f_ffn2, f_mix = _pair_gather([_shard_sum(g_gu, r_gu, t_gu, 2), _shard_sum(g_ffn2, r_ffn2, t_ffn2, 3),
                                        _shard_sum(g_mix, r_mix, t_mix, 3)])
    group = [("ffn1_w_gate", f_gu, 0), ("ffn1_w_up", f_gu, D), ("ffn2_w_gate", f_ffn2, 0), ("ffn2_w_up", f_ffn2, D),
             ("ffn2_w_down", f_ffn2, 2 * D), ("w_out", f_mix, 0)]
    updates, (t_dn, sm_late) = _adamw_group(
        [(adam[n][0], buf, row0, adam[n][1], adam[n][2]) for n, buf, row0 in group],
        (("scatter", q_dn), ("broadcast", small_late)))
    sm_late = _place_own_block(sm_late, small_late)
    (f_dn,) = _pair_gather([_shard_sum(g_dn, r_dn, t_dn, 2)])
    gsm = jnp.concatenate([_small_sum(sm_early), _small_sum(sm_late)], axis=0)
    updated = {n: u for (n, _, _), u in zip(group, updates)}
    return loss, dx, jnp.concatenate([f_gu, f_dn], axis=0), f_ffn2, f_mix, gsm, updated


_SMALL_EARLY = ["mix_pre_g", "mix_post_g", "ffn2_pre_g", "ffn2_post_g", "sgu_ln_g", "sgu_ln_b", "sgu_b_s", "b_forget",
                "sgu_w_s"]
_SMALL_LATE = ["ffn1_pre_g", "ffn1_post_g"]
_SMALL_NAMES = _SMALL_EARLY + _SMALL_LATE


def _pack_small(d, names=None):
    rows = []
    for n in names or _SMALL_NAMES:
        a = d[n].astype(F32)
        if n == "b_forget":
            a = _pad_cols(a, D)
        a = a.reshape(-1, D)
        rows.append(jnp.pad(a, ((0, -a.shape[0] % SMALL_STRIDE), (0, 0))))
    return jnp.concatenate(rows, axis=0)


def _unpack_small(p):
    out, r = {}, 0
    for n in _SMALL_NAMES:
        if n == "sgu_w_s":
            out[n] = p[r:r + L].reshape(1, G, L, L)
            r += L
        elif n == "b_forget":
            out[n] = p[r:r + 1, :H]
            r += SMALL_STRIDE
        elif n == "sgu_b_s":
            out[n] = p[r:r + 1].reshape(1, G, L)
            r += SMALL_STRIDE
        else:
            out[n] = p[r:r + 1]
            r += SMALL_STRIDE
    return out


_BIG_NAMES = ["ffn1_w_gate", "ffn1_w_up", "ffn1_w_down", "ffn2_w_gate", "ffn2_w_up", "ffn2_w_down", "w_out", "w_in"]
_WEIGHT_ORDER = ['ffn1_pre_g', 'ffn1_w_gate', 'ffn1_w_up', 'ffn1_w_down', 'ffn1_post_g', 'mix_pre_g', 'w_in', 'b_forget',
                 'sgu_ln_g', 'sgu_ln_b', 'sgu_w_s', 'sgu_b_s', 'w_out', 'mix_post_g', 'ffn2_pre_g', 'ffn2_w_gate',
                 'ffn2_w_up', 'ffn2_w_down', 'ffn2_post_g']


def _pack_ffn(w, name):
    return jnp.concatenate([w[name + "_w_gate"][0], w[name + "_w_up"][0], w[name + "_w_down"][0]], axis=0)


def _pack_mix(w):
    w_in = w["w_in"][0]
    return jnp.concatenate([w["w_out"][0], w_in[:, :D], _pad_cols(w_in[:, D:], D)], axis=0)


def _unpack_ffn(p, name):
    return {name + "_w_gate": p[:D][None], name + "_w_up": p[D:2 * D][None], name + "_w_down": p[2 * D:][None]}


def _unpack_mix(p):
    return {"w_out": p[:256][None],
            "w_in": jnp.concatenate([p[256:256 + D], p[256 + D:, :WIN_SH - D]], axis=1)[None]}


def _step(args, tm, t_attn):
    x = args["x"][0]
    target = args["loss_target"][0]
    weights = {n: args[n] for n in _WEIGHT_ORDER}
    small = {n: weights[n] for n in _SMALL_NAMES}

    wb = {n: weights[n].astype(BF) for n in _BIG_NAMES}
    wp1 = _pack_ffn(wb, "ffn1")
    wp2 = jnp.concatenate([_pack_ffn(wb, "ffn2"), _pack_mix(wb)], axis=0)
    early = ["ffn1_w_gate", "ffn1_w_up", "ffn2_w_gate", "ffn2_w_up", "ffn2_w_down", "w_out"]
    adam = {n: tuple(a.reshape(-1, D) for a in (weights[n], args["m_" + n], args["v_" + n])) for n in early}
    loss_local, dx, f_ffn1, f_ffn2, f_mix, gsm, updated = _train_step(x, target, wp1, wp2, small, adam, tm, t_attn)
    loss = lax.psum(loss_local, ("x", "y", "c"))
    grads = {**_unpack_ffn(f_ffn1, "ffn1"), **_unpack_ffn(f_ffn2, "ffn2"), **_unpack_mix(f_mix),
             **_unpack_small(gsm)}

    delta, new_m, new_v = {}, {}, {}
    for n in _BIG_NAMES:
        shp = weights[n].shape
        if n in updated:
            d, nm, nv = updated[n]
        else:
            w2 = weights[n].reshape(-1, shp[-1])
            d, nm, nv = _adamw(w2, grads[n].reshape(w2.shape), args["m_" + n].reshape(w2.shape),
                               args["v_" + n].reshape(w2.shape), w2.shape[0] // 4)
        delta[n], new_m[n], new_v[n] = d.reshape(shp), nm.reshape(shp), nv.reshape(shp)
    ds, nms, nvs = _adamw(_pack_small(small), gsm, _pack_small({n: args["m_" + n] for n in _SMALL_NAMES}),
                          _pack_small({n: args["v_" + n] for n in _SMALL_NAMES}), SMALL_ROWS)
    delta.update(_unpack_small(ds))
    new_m.update(_unpack_small(nms))
    new_v.update(_unpack_small(nvs))

    return (loss, dx[None], *[grads[n] for n in _WEIGHT_ORDER], *[delta[n] for n in _WEIGHT_ORDER],
            *[new_m[n] for n in _WEIGHT_ORDER], *[new_v[n] for n in _WEIGHT_ORDER])


_ARG_NAMES = (["x"] + _WEIGHT_ORDER + ["loss_target"] + ["m_" + n for n in _WEIGHT_ORDER]
              + ["v_" + n for n in _WEIGHT_ORDER])


def kernel(x, ffn1_pre_g, ffn1_w_gate, ffn1_w_up, ffn1_w_down, ffn1_post_g, mix_pre_g, w_in, b_forget, sgu_ln_g, sgu_ln_b, sgu_w_s, sgu_b_s, w_out, mix_post_g, ffn2_pre_g, ffn2_w_gate, ffn2_w_up, ffn2_w_down, ffn2_post_g, loss_target, m_ffn1_pre_g, m_ffn1_w_gate, m_ffn1_w_up, m_ffn1_w_down, m_ffn1_post_g, m_mix_pre_g, m_w_in, m_b_forget, m_sgu_ln_g, m_sgu_ln_b, m_sgu_w_s, m_sgu_b_s, m_w_out, m_mix_post_g, m_ffn2_pre_g, m_ffn2_w_gate, m_ffn2_w_up, m_ffn2_w_down, m_ffn2_post_g, v_ffn1_pre_g, v_ffn1_w_gate, v_ffn1_w_up, v_ffn1_w_down, v_ffn1_post_g, v_mix_pre_g, v_w_in, v_b_forget, v_sgu_ln_g, v_sgu_ln_b, v_sgu_w_s, v_sgu_b_s, v_w_out, v_mix_post_g, v_ffn2_pre_g, v_ffn2_w_gate, v_ffn2_w_up, v_ffn2_w_down, v_ffn2_post_g):
    args = (x, ffn1_pre_g, ffn1_w_gate, ffn1_w_up, ffn1_w_down, ffn1_post_g, mix_pre_g, w_in, b_forget, sgu_ln_g, sgu_ln_b, sgu_w_s, sgu_b_s, w_out, mix_post_g, ffn2_pre_g, ffn2_w_gate, ffn2_w_up, ffn2_w_down, ffn2_post_g, loss_target, m_ffn1_pre_g, m_ffn1_w_gate, m_ffn1_w_up, m_ffn1_w_down, m_ffn1_post_g, m_mix_pre_g, m_w_in, m_b_forget, m_sgu_ln_g, m_sgu_ln_b, m_sgu_w_s, m_sgu_b_s, m_w_out, m_mix_post_g, m_ffn2_pre_g, m_ffn2_w_gate, m_ffn2_w_up, m_ffn2_w_down, m_ffn2_post_g, v_ffn1_pre_g, v_ffn1_w_gate, v_ffn1_w_up, v_ffn1_w_down, v_ffn1_post_g, v_mix_pre_g, v_w_in, v_b_forget, v_sgu_ln_g, v_sgu_ln_b, v_sgu_w_s, v_sgu_b_s, v_w_out, v_mix_post_g, v_ffn2_pre_g, v_ffn2_w_gate, v_ffn2_w_up, v_ffn2_w_down, v_ffn2_post_g)
    named = dict(zip(_ARG_NAMES, args))
    tile = min(512, x.shape[1])
    return _step(named, tile, tile)
```

```python
import functools
import math

import jax
import jax.numpy as jnp
from jax import lax
from jax.experimental import pallas as pl
from jax.experimental.pallas import tpu as pltpu

D = 1024
F = 4096
H = 8
HD = 128
G = 8
L = 128
CHUNK = 64
NSH = 4
NDEV = 8
ZW = 7 * D
RMS_EPS = 1e-6
LN_EPS = 1e-5
NEG = -1e30
SCALE = 1.0 / math.sqrt(HD)
LOG2E = math.log2(math.e)
LN2 = math.log(2.0)

ADAM_LR = 0.001
ADAM_B1 = 0.9
ADAM_B2 = 0.999
ADAM_EPS = 1e-08
ADAM_WD = 0.01
ADAM_STEP = 10

VMEM_LIMIT_BYTES = 56 * 1024 * 1024

WIN_SH = 1794
FFN_ROWS = 3 * D
MIX_ROWS = 256 + 2 * D
G2_ROWS = FFN_ROWS + MIX_ROWS
SMALL_STRIDE = 8
SMALL_ROWS = 10 * SMALL_STRIDE + L

BF = jnp.bfloat16
F32 = jnp.float32
MESH = pl.DeviceIdType.MESH


def _params(n_grid):
    return pltpu.CompilerParams(dimension_semantics=("arbitrary",) * n_grid,
                                vmem_limit_bytes=VMEM_LIMIT_BYTES)


def _dot(a, b):
    return jnp.dot(a, b, preferred_element_type=F32)


def _dot_nt(a, b):
    return lax.dot_general(a, b, (((1,), (1,)), ((), ())), preferred_element_type=F32)


def _dot_tn(a, b):
    return lax.dot_general(a, b, (((0,), (0,)), ((), ())), preferred_element_type=F32)


def _rms(x, g):
    r = lax.rsqrt(jnp.mean(x * x, axis=-1, keepdims=True) + RMS_EPS)
    return x * r * g


def _rms_bwd(dn, x, g):
    r = lax.rsqrt(jnp.mean(x * x, axis=-1, keepdims=True) + RMS_EPS)
    xr = x * r
    dg = jnp.sum(dn * xr, axis=0, keepdims=True)
    t = dn * g
    dx = r * (t - xr * jnp.mean(t * xr, axis=-1, keepdims=True))
    return dx, dg


def _gelu_parts(x):
    cdf = 0.5 * (1.0 + lax.erf(x * (1.0 / math.sqrt(2.0))))
    pdf = jnp.exp(-0.5 * x * x) * (1.0 / math.sqrt(2.0 * math.pi))
    return x * cdf, cdf + x * pdf


def _gelu(x):
    return x * (0.5 * (1.0 + lax.erf(x * (1.0 / math.sqrt(2.0)))))


def _sigmoid(x):
    return 0.5 * jnp.tanh(0.5 * x) + 0.5


def _ffn_fwd(x, g_pre, wpack, g_post, tm, tail, gather=None):
    S = x.shape[0]
    nt, nf, tf = S // tm, NSH, D
    n_tail_out = 1 if tail[0] == "norm" else 2

    def body(x_ref, gpre_ref, wg_ref, wu_ref, wd_ref, gpost_ref, tail_ref, *rest):
        if gather is not None:
            wp_ref, rest = rest[0], rest[1:]
        h_ref, a_ref, b_ref, y_ref, xo_ref = rest[:5]
        tail_out = rest[5:5 + n_tail_out]
        rest = rest[5 + n_tail_out:]
        if gather is not None:
            g_ref, h_s, acc, send_sems, recv_sems = rest
        else:
            h_s, acc = rest
        i = pl.program_id(0)
        j = pl.program_id(1)

        if gather is not None:
            @pl.when((i == 0) & (j == 0))
            def _():
                _gather_start(wp_ref, g_ref, send_sems, recv_sems)

        @pl.when(j == 0)
        def _():
            h = _rms(x_ref[...], gpre_ref[...]).astype(BF)
            h_s[...] = h
            h_ref[...] = h
            acc[...] = jnp.zeros_like(acc)

        h = h_s[...]
        a = _dot(h, wg_ref[...])
        b = _dot(h, wu_ref[...])
        a_ref[...] = a.astype(BF)
        b_ref[...] = b.astype(BF)
        act = (a * _sigmoid(a)) * b
        acc[...] += _dot(act.astype(BF), wd_ref[...])

        if tail[0] == "loss":
            @pl.when((i == 0) & (j == 0))
            def _():
                tail_out[1][...] = jnp.zeros_like(tail_out[1])

        @pl.when(j == nf - 1)
        def _():
            y = acc[...]
            y_ref[...] = y
            xo = x_ref[...] + 0.5 * _rms(y, gpost_ref[...])
            xo_ref[...] = xo
            if tail[0] == "norm":
                tail_out[0][...] = _rms(xo, tail_ref[...]).astype(BF)
            else:
                e = xo - tail_ref[...]
                tail_out[0][...] = e * (1.0 / D)
                tail_out[1][...] += jnp.sum(e * e) * (0.5 / D)

        if gather is not None:
            @pl.when((i == max(nt - 2, 0)) & (j == 0))
            def _():
                _gather_forward(wp_ref, g_ref, send_sems, recv_sems)

            @pl.when((i == nt - 1) & (j == nf - 1))
            def _():
                _gather_finish(wp_ref, g_ref, send_sems, recv_sems)

    row = pl.BlockSpec((tm, D), lambda i, j: (i, 0))
    vec = pl.BlockSpec((1, D), lambda i, j: (0, 0))
    anywhere = pl.BlockSpec(memory_space=pl.ANY)
    in_specs = [row, vec,
                pl.BlockSpec((None, D, tf), lambda i, j: (j, 0, 0)),
                pl.BlockSpec((None, D, tf), lambda i, j: (j, 1, 0)),
                pl.BlockSpec((None, tf, D), lambda i, j: (j, 2, 0)),
                vec, vec if tail[0] == "norm" else row]
    out_specs = [row,
                 pl.BlockSpec((tm, tf), lambda i, j: (i, j)),
                 pl.BlockSpec((tm, tf), lambda i, j: (i, j)),
                 row, row]
    out_shape = [jax.ShapeDtypeStruct((S, D), BF),
                 jax.ShapeDtypeStruct((S, F), BF),
                 jax.ShapeDtypeStruct((S, F), BF),
                 jax.ShapeDtypeStruct((S, D), F32),
                 jax.ShapeDtypeStruct((S, D), F32)]
    if tail[0] == "norm":
        out_specs.append(row)
        out_shape.append(jax.ShapeDtypeStruct((S, D), BF))
    else:
        out_specs += [row, pl.BlockSpec((8, HD), lambda i, j: (0, 0))]
        out_shape += [jax.ShapeDtypeStruct((S, D), F32), jax.ShapeDtypeStruct((8, HD), F32)]
    scratch = [pltpu.VMEM((tm, D), BF), pltpu.VMEM((tm, D), F32)]
    args = [x, g_pre, wpack, wpack, wpack, g_post, tail[1]]
    if gather is not None:
        in_specs.append(anywhere)
        out_specs.append(anywhere)
        out_shape.append(jax.ShapeDtypeStruct((NSH,) + gather.shape, gather.dtype))
        scratch += [pltpu.SemaphoreType.DMA((6,)), pltpu.SemaphoreType.DMA((6,))]
        args.append(gather)
    res = list(pl.pallas_call(
        body, name="ffn_fwd" if gather is None else "ffn_fwd_gather",
        grid=(nt, nf),
        in_specs=in_specs, out_specs=out_specs, out_shape=out_shape, scratch_shapes=scratch,
        compiler_params=_params(2),
    )(*args))
    if gather is not None:
        res[-1] = _place_own_shard(res[-1], gather)
    return res


def _ffn_bwd(dxo, y, g_post, a, b, wpack, x_in, g_pre, tm, riders=()):
    S = dxo.shape[0]
    nt, nf, tf = S // tm, NSH, D
    nr = len(riders)

    def body(dxo_ref, y_ref, gpost_ref, a_ref, b_ref, wg_ref, wu_ref, wd_ref, xin_ref, gpre_ref, *rest):
        rsrc = rest[:nr]
        dy_ref, da_ref, db_ref, act_ref, dxin_ref, dgpost_ref, dgpre_ref = rest[nr:nr + 7]
        rdst = rest[nr + 7:2 * nr + 7]
        dy_s, acc = rest[2 * nr + 7:2 * nr + 9]
        sems = rest[2 * nr + 9:]
        i = pl.program_id(0)
        j = pl.program_id(1)

        @pl.when((i == 0) & (j == 0))
        def _():
            dgpost_ref[...] = jnp.zeros_like(dgpost_ref)
            dgpre_ref[...] = jnp.zeros_like(dgpre_ref)
            for k, (kind, _) in enumerate(riders):
                _rider_start(kind, rsrc[k], rdst[k], sems[2 * k], sems[2 * k + 1])

        @pl.when(j == 0)
        def _():
            dy, dg = _rms_bwd(0.5 * dxo_ref[...], y_ref[...], gpost_ref[...])
            dyb = dy.astype(BF)
            dy_s[...] = dyb
            dy_ref[...] = dyb
            dgpost_ref[...] += dg
            acc[...] = jnp.zeros_like(acc)

        dact = _dot_nt(dy_s[...], wd_ref[...])
        av = a_ref[...].astype(F32)
        bv = b_ref[...].astype(F32)
        sig = _sigmoid(av)
        sl = av * sig
        act_ref[...] = (sl * bv).astype(BF)
        dbb = (dact * sl).astype(BF)
        dab = (dact * bv * (sig * (1.0 + av * (1.0 - sig)))).astype(BF)
        da_ref[...] = dab
        db_ref[...] = dbb
        acc[...] += _dot_nt(dab, wg_ref[...]) + _dot_nt(dbb, wu_ref[...])

        @pl.when(j == nf - 1)
        def _():
            dx, dg = _rms_bwd(acc[...], xin_ref[...], gpre_ref[...])
            dxin_ref[...] = dxo_ref[...] + dx
            dgpre_ref[...] += dg

        if riders:
            @pl.when((i == nt - 1) & (j == nf - 1))
            def _():
                for k, (kind, _) in enumerate(riders):
                    _rider_finish(kind, rsrc[k], rdst[k], sems[2 * k], sems[2 * k + 1])

    row = pl.BlockSpec((tm, D), lambda i, j: (i, 0))
    vec = pl.BlockSpec((1, D), lambda i, j: (0, 0))
    ff = pl.BlockSpec((tm, tf), lambda i, j: (i, j))
    anywhere = pl.BlockSpec(memory_space=pl.ANY)
    in_specs = [row, row, vec, ff, ff,
                pl.BlockSpec((None, D, tf), lambda i, j: (j, 0, 0)),
                pl.BlockSpec((None, D, tf), lambda i, j: (j, 1, 0)),
                pl.BlockSpec((None, tf, D), lambda i, j: (j, 2, 0)),
                row, vec]
    out_specs = [row, ff, ff, ff, row, vec, vec]
    out_shape = [jax.ShapeDtypeStruct((S, D), BF),
                 jax.ShapeDtypeStruct((S, F), BF),
                 jax.ShapeDtypeStruct((S, F), BF),
                 jax.ShapeDtypeStruct((S, F), BF),
                 jax.ShapeDtypeStruct((S, D), F32),
                 jax.ShapeDtypeStruct((1, D), F32),
                 jax.ShapeDtypeStruct((1, D), F32)]
    scratch = [pltpu.VMEM((tm, D), BF), pltpu.VMEM((tm, D), F32)]
    args = [dxo, y, g_post, a, b, wpack, wpack, wpack, x_in, g_pre]
    for rider in riders:
        in_specs.append(anywhere)
        out_specs.append(anywhere)
        out_shape.append(_rider_out(rider))
        scratch += _rider_sems(rider)
        args.append(rider[1])
    return pl.pallas_call(
        body, name="ffn_bwd" if not riders else "ffn_bwd_riders",
        grid=(nt, nf),
        in_specs=in_specs, out_specs=out_specs, out_shape=out_shape, scratch_shapes=scratch,
        compiler_params=_params(2),
    )(*args)


def _mm_tn(a, b, bm, bn, bt, into=None, rider=None):
    S, M = a.shape
    N = b.shape[1]
    nt = S // bt
    n_in = 2 + (into is not None and into[0] is not None) + (rider is not None)

    def body(*refs):
        a_ref, b_ref, o_ref = refs[0], refs[1], refs[n_in]
        m, n, t = pl.program_id(0), pl.program_id(1), pl.program_id(2)

        if rider is not None:
            rsrc_ref, rdst_ref, send_sems, recv_sems = refs[n_in - 1], refs[n_in + 1], refs[-2], refs[-1]

            @pl.when((m == 0) & (n == 0) & (t == 0))
            def _():
                _rider_start(rider[0], rsrc_ref, rdst_ref, send_sems, recv_sems)

        @pl.when(t == 0)
        def _():
            o_ref[...] = jnp.zeros_like(o_ref)

        o_ref[...] += _dot_tn(a_ref[...], b_ref[...])

        if rider is not None:
            @pl.when((m == M // bm - 1) & (n == N // bn - 1) & (t == nt - 1))
            def _():
                _rider_finish(rider[0], rsrc_ref, rdst_ref, send_sems, recv_sems)

    in_specs = [pl.BlockSpec((bt, bm), lambda m, n, t: (t, m)),
                pl.BlockSpec((bt, bn), lambda m, n, t: (t, n))]
    args, aliases = [a, b], {}
    if into is None:
        out_spec = pl.BlockSpec((bm, bn), lambda m, n, t: (m, n))
        out_shape = jax.ShapeDtypeStruct((M, N), F32)
    else:
        buf, rows, rb, by = into
        assert bn == D and (M == bm if by == "cols" else (M == NSH * bm and N == D))
        if by == "cols":
            out_spec = pl.BlockSpec((None, bm, bn), lambda m, n, t: (n, rb, 0))
        else:
            out_spec = pl.BlockSpec((None, bm, bn), lambda m, n, t: (m, rb, 0))
        out_shape = jax.ShapeDtypeStruct((NSH, rows, D), F32)
        if buf is not None:
            in_specs.append(pl.BlockSpec(memory_space=pl.ANY))
            args.append(buf)
            aliases = {2: 0}
    if rider is None:
        return pl.pallas_call(
            body, name="mm_tn",
            grid=(M // bm, N // bn, nt),
            in_specs=in_specs, out_specs=out_spec, out_shape=out_shape,
            input_output_aliases=aliases,
            compiler_params=_params(3),
        )(*args)
    anywhere = pl.BlockSpec(memory_space=pl.ANY)
    return pl.pallas_call(
        body, name="mm_tn_rider",
        grid=(M // bm, N // bn, nt),
        in_specs=in_specs + [anywhere], out_specs=[out_spec, anywhere], out_shape=[out_shape, _rider_out(rider)],
        scratch_shapes=_rider_sems(rider),
        input_output_aliases=aliases,
        compiler_params=_params(3),
    )(*args, rider[1])


def _mm(a, w, tm, tn, out_dtype, first_block_scale=1.0):
    S, K = a.shape
    N = w.shape[1]

    def body(a_ref, w_ref, o_ref):
        r = _dot(a_ref[...], w_ref[...])
        if first_block_scale != 1.0:
            r = r * jnp.where(pl.program_id(1) == 0, first_block_scale, 1.0)
        o_ref[...] = r.astype(out_dtype)

    return pl.pallas_call(
        body, name="mm",
        grid=(S // tm, N // tn),
        in_specs=[pl.BlockSpec((tm, K), lambda i, j: (i, 0)),
                  pl.BlockSpec((K, tn), lambda i, j: (0, j))],
        out_specs=pl.BlockSpec((tm, tn), lambda i, j: (i, j)),
        out_shape=jax.ShapeDtypeStruct((S, N), out_dtype),
        compiler_params=_params(2),
    )(a, w)


def _mix_in_bwd(dz, wcat, dzf, wf, x1, g, dx2, tm, rider):
    S = dz.shape[0]
    nk = 1
    kb = ZW // nk

    def body(dz_ref, w_ref, dzf_ref, wf_ref, x_ref, g_ref, dx2_ref, rsrc_ref, dx1_ref, dg_ref, rdst_ref,
             acc, send_sems, recv_sems):
        i = pl.program_id(0)
        k = pl.program_id(1)

        @pl.when((i == 0) & (k == 0))
        def _():
            _rider_start(rider[0], rsrc_ref, rdst_ref, send_sems, recv_sems)
            dg_ref[...] = jnp.zeros_like(dg_ref)

        @pl.when(k == 0)
        def _():
            acc[...] = _dot_nt(dzf_ref[...], wf_ref[...])

        acc[...] += _dot_nt(dz_ref[...], w_ref[...])

        @pl.when(k == nk - 1)
        def _():
            dx, dg = _rms_bwd(acc[...], x_ref[...], g_ref[...])
            dx1_ref[...] = dx2_ref[...] + dx
            dg_ref[...] += dg

        @pl.when((i == S // tm - 1) & (k == nk - 1))
        def _():
            _rider_finish(rider[0], rsrc_ref, rdst_ref, send_sems, recv_sems)

    row = pl.BlockSpec((tm, D), lambda i, k: (i, 0))
    vec = pl.BlockSpec((1, D), lambda i, k: (0, 0))
    anywhere = pl.BlockSpec(memory_space=pl.ANY)
    return pl.pallas_call(
        body, name="mix_in_bwd",
        grid=(S // tm, nk),
        in_specs=[pl.BlockSpec((tm, kb), lambda i, k: (i, k)),
                  pl.BlockSpec((D, kb), lambda i, k: (0, k)),
                  pl.BlockSpec((tm, HD), lambda i, k: (i, 0)),
                  pl.BlockSpec((D, HD), lambda i, k: (0, 0)),
                  row, vec, row, anywhere],
        out_specs=[row, vec, anywhere],
        out_shape=[jax.ShapeDtypeStruct((S, D), F32), jax.ShapeDtypeStruct((1, D), F32), _rider_out(rider)],
        scratch_shapes=[pltpu.VMEM((tm, D), F32)] + _rider_sems(rider),
        compiler_params=_params(2),
    )(dz, wcat, dzf, wf, x1, g, dx2, rider[1])


def _scan_rows(blk, reverse):
    n = blk.shape[0]
    row = lax.broadcasted_iota(jnp.int32, blk.shape, 0)
    d = 1
    while d < n:
        if reverse:
            blk = blk + jnp.where(row < n - d, pltpu.roll(blk, n - d, 0), 0.0)
        else:
            blk = blk + jnp.where(row >= d, pltpu.roll(blk, d, 0), 0.0)
        d *= 2
    return blk


def _forget_cumsum(zf, bf, cs):
    S = zf.shape[0]

    def body(zf_ref, bf_ref, c_ref, cb_ref, carry):
        @pl.when(pl.program_id(0) == 0)
        def _():
            carry[...] = jnp.zeros_like(carry)

        x = zf_ref[...] + bf_ref[...]
        logf = jnp.minimum(x, 0.0) - jnp.log1p(jnp.exp(-jnp.abs(x)))
        sc = _scan_rows(logf, False) + carry[...]
        carry[...] = sc[cs - 1:cs, :]
        sc = sc * LOG2E
        c_ref[...] = sc
        for h in range(H):
            cb_ref[h] = jnp.broadcast_to(sc[:, h:h + 1], (cs, HD))

    return pl.pallas_call(
        body, name="forget_cumsum",
        grid=(S // cs,),
        in_specs=[pl.BlockSpec((cs, HD), lambda i: (i, 0)), pl.BlockSpec((1, HD), lambda i: (0, 0))],
        out_specs=[pl.BlockSpec((cs, HD), lambda i: (i, 0)),
                   pl.BlockSpec((H, cs, HD), lambda i: (0, i, 0))],
        out_shape=[jax.ShapeDtypeStruct((S, HD), F32), jax.ShapeDtypeStruct((H, S, HD), F32)],
        scratch_shapes=[pltpu.VMEM((1, HD), F32)],
        compiler_params=_params(1),
    )(zf, bf)


def _forget_bwd(dc, zf, bf, cs):
    S = dc.shape[0]
    nc = S // cs

    def body(dc_ref, zf_ref, bf_ref, dzf_ref, dbf_ref, carry):
        @pl.when(pl.program_id(0) == 0)
        def _():
            carry[...] = jnp.zeros_like(carry)
            dbf_ref[...] = jnp.zeros_like(dbf_ref)

        sc = _scan_rows(dc_ref[...], True) + carry[...]
        carry[...] = sc[0:1, :]
        x = zf_ref[...] + bf_ref[...]
        dz = sc / (1.0 + jnp.exp(x))
        dzf_ref[...] = dz.astype(BF)
        dbf_ref[...] += jnp.sum(dz, axis=0, keepdims=True)

    rev = pl.BlockSpec((cs, HD), lambda i: (nc - 1 - i, 0))
    vec = pl.BlockSpec((1, HD), lambda i: (0, 0))
    return pl.pallas_call(
        body, name="forget_bwd",
        grid=(nc,),
        in_specs=[rev, rev, vec],
        out_specs=[rev, vec],
        out_shape=[jax.ShapeDtypeStruct((S, HD), BF), jax.ShapeDtypeStruct((1, HD), F32)],
        scratch_shapes=[pltpu.VMEM((1, HD), F32)],
        compiler_params=_params(1),
    )(dc, zf, bf)


def _lanes(x, n):
    return x if n == HD else jnp.concatenate([x] * (n // HD), axis=1)


def _causal_mask(i, j, t, rows_are_queries):
    r = lax.broadcasted_iota(jnp.int32, (t, t), 0)
    c = lax.broadcasted_iota(jnp.int32, (t, t), 1)
    if rows_are_queries:
        return (j * t + c) <= (i * t + r)
    return (j * t + r) <= (i * t + c)


def _fox_fwd(z, ccol_b, crow, t):
    S = z.shape[0]
    nq = S // t

    def body(q_ref, kv_ref, cc_ref, cr_ref, o_ref, lse_ref, m_s, acc_s, s_a, s_b):
        i = pl.program_id(1)
        ct = cc_ref[...]
        ones = jnp.ones((t, HD), BF)
        m_s[...] = jnp.full_like(m_s, NEG)
        acc_s[...] = jnp.zeros_like(acc_s)

        def scores(j, s_ref):
            off = pl.multiple_of(j * t, t)
            s_ref[...] = _dot_nt(q_ref[...], kv_ref[pl.ds(off, t), :HD]) - cr_ref[pl.ds(j, 1), :]

        def consume(j, s_ref, masked):
            off = pl.multiple_of(j * t, t)
            v1 = jnp.concatenate([kv_ref[pl.ds(off, t), HD:], ones], axis=1)
            s = s_ref[...]
            if masked:
                s = jnp.where(_causal_mask(i, j, t, True), s, NEG)
            m_old = m_s[...]
            m_new = jnp.maximum(m_old, jnp.max(s, axis=1, keepdims=True))
            p = jnp.exp2(s - _lanes(m_new, t))
            alpha = jnp.exp2(m_old - m_new)
            acc_s[...] = _lanes(alpha, 2 * HD) * acc_s[...] + _dot(p.astype(BF), v1)
            m_s[...] = m_new

        scores(0, s_a)

        def pair(jj, carry):
            j = 2 * jj
            scores(j + 1, s_b)
            consume(j, s_a, False)
            scores(j + 2, s_a)
            consume(j + 1, s_b, False)
            return carry

        lax.fori_loop(0, i // 2, pair, 0)

        @pl.when(i % 2 == 0)
        def _():
            consume(i, s_a, True)

        @pl.when(i % 2 == 1)
        def _():
            scores(i, s_b)
            consume(i - 1, s_a, False)
            consume(i, s_b, True)

        l = acc_s[:, HD:]
        o_ref[...] = (acc_s[:, :HD] / l).astype(BF)
        lse_ref[...] = m_s[...] + ct + jnp.log2(l)

    return pl.pallas_call(
        body, name="fox_fwd",
        grid=(H, nq),
        in_specs=[pl.BlockSpec((t, HD), lambda h, i: (i, h)),
                  pl.BlockSpec((S, 2 * HD), lambda h, i: (0, 4 + h)),
                  pl.BlockSpec((None, t, HD), lambda h, i: (h, i, 0)),
                  pl.BlockSpec((None, nq, t), lambda h, i: (h, 0, 0))],
        out_specs=[pl.BlockSpec((t, HD), lambda h, i: (i, h)),
                   pl.BlockSpec((None, t, HD), lambda h, i: (h, i, 0))],
        out_shape=[jax.ShapeDtypeStruct((S, D), BF), jax.ShapeDtypeStruct((H, S, HD), F32)],
        scratch_shapes=[pltpu.VMEM((t, HD), F32), pltpu.VMEM((t, 2 * HD), F32),
                        pltpu.VMEM((t, t), F32), pltpu.VMEM((t, t), F32)],
        compiler_params=_params(2),
    )(z, z, ccol_b, crow)


def _fox_bwd(z, do, o, lse_b, ccol_b, crow, dz, t, rider):
    S = z.shape[0]
    nq = S // t

    def body(q_ref, kv_ref, do_ref, o_ref, lse_ref, cc_ref, cr_ref, dz_in, rsrc_ref,
             dq_ref, dkt_ref, dvt_ref, dck_ref, dcq_ref, rdst_ref, acc_s, r_s, send_sems, recv_sems):
        del dz_in
        i = pl.program_id(1)

        @pl.when((pl.program_id(0) == 0) & (i == 0))
        def _():
            _rider_start(rider[0], rsrc_ref, rdst_ref, send_sems, recv_sems)

        @pl.when(i == 0)
        def _():
            dkt_ref[...] = jnp.zeros_like(dkt_ref)
            dvt_ref[...] = jnp.zeros_like(dvt_ref)
            dck_ref[...] = jnp.zeros_like(dck_ref)

        q = q_ref[...]
        dout = do_ref[...]
        qt = jnp.transpose(q.astype(F32)).astype(BF)
        dot_ = jnp.transpose(dout.astype(F32)).astype(BF)
        off_t = _lanes(lse_ref[...] - cc_ref[...], t)
        delta = jnp.sum(dout.astype(F32) * o_ref[...].astype(F32), axis=1, keepdims=True)
        delta = _lanes(jnp.broadcast_to(delta, (t, HD)), t)
        acc_s[...] = jnp.zeros_like(acc_s)
        r_s[...] = jnp.zeros_like(r_s)

        def step(j, masked):
            off = pl.multiple_of(j * t, t)
            k = kv_ref[pl.ds(off, t), :HD]
            v = kv_ref[pl.ds(off, t), HD:]
            p = jnp.exp2(_dot_nt(q, k) - cr_ref[pl.ds(j, 1), :] - off_t)
            if masked:
                p = jnp.where(_causal_mask(i, j, t, True), p, 0.0)
            ds = p * (_dot_nt(dout, v) - delta)
            dsb = ds.astype(BF)
            acc_s[...] += _dot(dsb, k)
            dkt_ref[j] += _dot(qt, dsb)
            dvt_ref[j] += _dot(dot_, p.astype(BF))
            dck_ref[pl.ds(j, 1), :] += jnp.sum(ds, axis=0, keepdims=True)
            r_s[...] += jnp.sum(ds, axis=1, keepdims=True)

        def full_step(j, carry):
            step(j, False)
            return carry

        lax.fori_loop(0, i, full_step, 0)
        step(i, True)
        dq_ref[...] = (acc_s[...] * SCALE).astype(BF)
        dcq_ref[...] = jnp.transpose(r_s[...])[0:1, :]

        @pl.when((pl.program_id(0) == H - 1) & (i == nq - 1))
        def _():
            _rider_finish(rider[0], rsrc_ref, rdst_ref, send_sems, recv_sems)

    qspec = pl.BlockSpec((t, HD), lambda h, i: (i, h))
    bspec = pl.BlockSpec((None, t, HD), lambda h, i: (h, i, 0))
    rows = pl.BlockSpec((None, nq, t), lambda h, i: (h, 0, 0))
    tspec = pl.BlockSpec((None, nq, HD, t), lambda h, i: (h, 0, 0, 0))
    tshape = jax.ShapeDtypeStruct((H, nq, HD, t), F32)
    anywhere = pl.BlockSpec(memory_space=pl.ANY)
    return pl.pallas_call(
        body, name="fox_bwd",
        grid=(H, nq),
        in_specs=[qspec,
                  pl.BlockSpec((S, 2 * HD), lambda h, i: (0, 4 + h)),
                  qspec, qspec, bspec, bspec, rows, anywhere, anywhere],
        out_specs=[qspec, tspec, tspec, rows, pl.BlockSpec((None, None, 1, t), lambda h, i: (h, i, 0, 0)),
                   anywhere],
        out_shape=[jax.ShapeDtypeStruct((S, ZW), BF), tshape, tshape,
                   jax.ShapeDtypeStruct((H, nq, t), F32), jax.ShapeDtypeStruct((H, nq, 1, t), F32),
                   _rider_out(rider)],
        scratch_shapes=[pltpu.VMEM((t, HD), F32), pltpu.VMEM((t, HD), F32)] + _rider_sems(rider),
        input_output_aliases={7: 0},
        compiler_params=_params(2),
    )(z, z, do, o, lse_b, ccol_b, crow, dz, rider[1])


def _fox_bwd_finish(dkt, dvt, dz, t):
    nq = dkt.shape[1]
    S = nq * t

    def body(dkt_ref, dvt_ref, dz_in, dkv_ref):
        del dz_in
        for j in range(nq):
            rows = slice(j * t, (j + 1) * t)
            dkv_ref[rows, :HD] = (jnp.transpose(dkt_ref[j]) * LN2).astype(BF)
            dkv_ref[rows, HD:] = jnp.transpose(dvt_ref[j]).astype(BF)

    tspec = pl.BlockSpec((None, nq, HD, t), lambda h: (h, 0, 0, 0))
    return pl.pallas_call(
        body, name="fox_bwd_finish",
        grid=(H,),
        in_specs=[tspec, tspec, pl.BlockSpec(memory_space=pl.ANY)],
        out_specs=pl.BlockSpec((S, 2 * HD), lambda h: (0, 4 + h)),
        out_shape=jax.ShapeDtypeStruct((S, ZW), BF),
        input_output_aliases={2: 0},
        compiler_params=_params(1),
    )(dkt, dvt, dz)


def _sgu_mask(transposed):
    r = lax.broadcasted_iota(jnp.int32, (L, L), 0)
    c = lax.broadcasted_iota(jnp.int32, (L, L), 1)
    if transposed:
        return (r // CHUNK) <= (c // CHUNK)
    return (c // CHUNK) <= (r // CHUNK)


def _ln_group(vs, lng, lnb):
    mu = jnp.mean(vs, axis=-1, keepdims=True)
    xc = vs - mu
    rstd = lax.rsqrt(jnp.mean(xc * xc, axis=-1, keepdims=True) + LN_EPS)
    xhat = xc * rstd
    return xhat, rstd, xhat * lng + lnb


def _mix_out_fwd(z, oa, ln_g, ln_b, ws, bst, wout, x1, g_post, tm):
    S = z.shape[0]
    nw = tm // L

    def body(u_ref, sv_ref, ga_ref, gb_ref, oa_ref, lng_ref, lnb_ref, ws_ref, bst_ref, wout_ref, x1_ref, gp_ref,
             mg_ref, y_ref, x2_ref, mg_s):
        mask = _sgu_mask(False)
        for g in range(G):
            cols = slice(g * L, (g + 1) * L)
            wm = jnp.where(mask, ws_ref[g], 0.0).astype(BF)
            bcol = bst_ref[:, g:g + 1]
            lng = lng_ref[:, cols]
            lnb = lnb_ref[:, cols]
            for w in range(nw):
                rows = slice(w * L, (w + 1) * L)
                vs = _gelu(sv_ref[rows, cols].astype(F32))
                _, _, vn = _ln_group(vs, lng, lnb)
                mixed = _dot(wm, vn.astype(BF)) + bcol
                ob = _gelu(u_ref[rows, cols].astype(F32)) * mixed
                mg = (_sigmoid(ga_ref[rows, cols].astype(F32)) * oa_ref[rows, cols].astype(F32)
                      + _sigmoid(gb_ref[rows, cols].astype(F32)) * ob)
                mg_s[rows, cols] = mg.astype(BF)
        mgb = mg_s[...]
        mg_ref[...] = mgb
        y = _dot(mgb, wout_ref[...])
        y_ref[...] = y
        x2_ref[...] = x1_ref[...] + _rms(y, gp_ref[...])

    row = pl.BlockSpec((tm, D), lambda i: (i, 0))
    vec = pl.BlockSpec((1, D), lambda i: (0, 0))

    def zcol(kb):
        return pl.BlockSpec((tm, D), lambda i: (i, kb))

    return pl.pallas_call(
        body, name="mix_out_fwd",
        grid=(S // tm,),
        in_specs=[zcol(3), zcol(4), zcol(5), zcol(6), row, vec, vec,
                  pl.BlockSpec((G, L, L), lambda i: (0, 0, 0)),
                  pl.BlockSpec((L, HD), lambda i: (0, 0)),
                  pl.BlockSpec((D, D), lambda i: (0, 0)),
                  row, vec],
        out_specs=[row, row, row],
        out_shape=[jax.ShapeDtypeStruct((S, D), BF),
                   jax.ShapeDtypeStruct((S, D), F32),
                   jax.ShapeDtypeStruct((S, D), F32)],
        scratch_shapes=[pltpu.VMEM((tm, D), BF)],
        compiler_params=_params(1),
    )(z, z, z, z, oa, ln_g, ln_b, ws, bst, wout, x1, g_post)


def _mix_out_bwd(dx2, y2, g_post, wout, z, oa, ln_g, ln_b, ws, wst, bst, tm, rider):
    S = z.shape[0]
    nw = tm // L

    def body(dx2_ref, y_ref, gp_ref, wout_ref, u_ref, sv_ref, ga_ref, gb_ref, oa_ref, lng_ref, lnb_ref,
             ws_ref, wst_ref, bst_ref, q_ref,
             dz_ref, dy_ref, doa_ref, dgp_ref, dlng_ref, dlnb_ref, dws_ref, dbst_ref, t_ref,
             dzg_s, dm_s, send_sems, recv_sems):
        i = pl.program_id(0)
        c = pl.program_id(1)

        @pl.when((i == 0) & (c == 0))
        def _():
            _rider_start(rider[0], q_ref, t_ref, send_sems, recv_sems)
            dgp_ref[...] = jnp.zeros_like(dgp_ref)
            dlng_ref[...] = jnp.zeros_like(dlng_ref)
            dlnb_ref[...] = jnp.zeros_like(dlnb_ref)
            dws_ref[...] = jnp.zeros_like(dws_ref)
            dbst_ref[...] = jnp.zeros_like(dbst_ref)

        @pl.when(c == 0)
        def _():
            dy, dg = _rms_bwd(dx2_ref[...], y_ref[...], gp_ref[...])
            dyb = dy.astype(BF)
            dy_ref[...] = dyb
            dgp_ref[...] += dg
            dm_s[...] = _dot_nt(dyb, wout_ref[...])
            mask = _sgu_mask(False)
            mask_t = _sgu_mask(True)
            lane = lax.broadcasted_iota(jnp.int32, (L, HD), 1)
            for g in range(G):
                cols = slice(g * L, (g + 1) * L)
                wm = jnp.where(mask, ws_ref[g], 0.0).astype(BF)
                wmt = jnp.where(mask_t, wst_ref[g], 0.0).astype(BF)
                bcol = bst_ref[:, g:g + 1]
                lng = lng_ref[:, cols]
                lnb = lnb_ref[:, cols]
                dws_g = jnp.zeros((L, L), F32)
                dbs_g = jnp.zeros((L, 1), F32)
                dlng_g = jnp.zeros((1, L), F32)
                dlnb_g = jnp.zeros((1, L), F32)
                for w in range(nw):
                    rows = slice(w * L, (w + 1) * L)
                    dm = dm_s[rows, cols]
                    vs, dvs_dz = _gelu_parts(sv_ref[rows, cols].astype(F32))
                    xhat, rstd, vn = _ln_group(vs, lng, lnb)
                    vnb = vn.astype(BF)
                    mixed = _dot(wm, vnb) + bcol
                    u, du_dz = _gelu_parts(u_ref[rows, cols].astype(F32))
                    sga = _sigmoid(ga_ref[rows, cols].astype(F32))
                    sgb = _sigmoid(gb_ref[rows, cols].astype(F32))
                    oav = oa_ref[rows, cols].astype(F32)
                    ob = u * mixed
                    doa_ref[rows, cols] = (dm * sga).astype(BF)
                    dzg_s[2, rows, cols] = (dm * oav * sga * (1.0 - sga)).astype(BF)
                    dzg_s[3, rows, cols] = (dm * ob * sgb * (1.0 - sgb)).astype(BF)
                    dob = dm * sgb
                    dzg_s[0, rows, cols] = (dob * mixed * du_dz).astype(BF)
                    dmixed = dob * u
                    dmb = dmixed.astype(BF)
                    dbs_g += jnp.sum(dmixed, axis=1, keepdims=True)
                    dws_g += _dot_nt(dmb, vnb)
                    dvn = _dot(wmt, dmb)
                    dlng_g += jnp.sum(dvn * xhat, axis=0, keepdims=True)
                    dlnb_g += jnp.sum(dvn, axis=0, keepdims=True)
                    dxh = dvn * lng
                    dvs = rstd * (dxh - jnp.mean(dxh, axis=-1, keepdims=True)
                                  - xhat * jnp.mean(dxh * xhat, axis=-1, keepdims=True))
                    dzg_s[1, rows, cols] = (dvs * dvs_dz).astype(BF)
                dws_ref[g] += jnp.where(mask, dws_g, 0.0)
                dbst_ref[...] += jnp.where(lane == g, dbs_g, 0.0)
                dlng_ref[:, cols] += dlng_g
                dlnb_ref[:, cols] += dlnb_g

        dz_ref[...] = dzg_s[c]

        @pl.when((i == S // tm - 1) & (c == 3))
        def _():
            _rider_finish(rider[0], q_ref, t_ref, send_sems, recv_sems)

    row = pl.BlockSpec((tm, D), lambda i, c: (i, 0))
    vec = pl.BlockSpec((1, D), lambda i, c: (0, 0))
    wsspec = pl.BlockSpec((G, L, L), lambda i, c: (0, 0, 0))
    bspec = pl.BlockSpec((L, HD), lambda i, c: (0, 0))
    anywhere = pl.BlockSpec(memory_space=pl.ANY)

    def zcol(kb):
        return pl.BlockSpec((tm, D), lambda i, c: (i, kb))

    return pl.pallas_call(
        body, name="mix_out_bwd",
        grid=(S // tm, 4),
        in_specs=[row, row, vec, pl.BlockSpec((D, D), lambda i, c: (0, 0)),
                  zcol(3), zcol(4), zcol(5), zcol(6), row, vec, vec, wsspec, wsspec, bspec, anywhere],
        out_specs=[pl.BlockSpec((tm, D), lambda i, c: (i, 3 + c)),
                   row, row, vec, vec, vec, wsspec, bspec, anywhere],
        out_shape=[jax.ShapeDtypeStruct((S, ZW), BF),
                   jax.ShapeDtypeStruct((S, D), BF),
                   jax.ShapeDtypeStruct((S, D), BF),
                   jax.ShapeDtypeStruct((1, D), F32),
                   jax.ShapeDtypeStruct((1, D), F32),
                   jax.ShapeDtypeStruct((1, D), F32),
                   jax.ShapeDtypeStruct((G, L, L), F32),
                   jax.ShapeDtypeStruct((L, HD), F32),
                   _rider_out(rider)],
        scratch_shapes=[pltpu.VMEM((4, tm, D), BF), pltpu.VMEM((tm, D), F32)] + _rider_sems(rider),
        compiler_params=_params(2),
    )(dx2, y2, g_post, wout, z, z, z, z, oa, ln_g, ln_b, ws, wst, bst, rider[1])


def _adamw_update(w_ref, g_ref, m_ref, v_ref, d_ref, nm_ref, nv_ref):
    gv = g_ref[...]
    m_new = ADAM_B1 * m_ref[...] + (1.0 - ADAM_B1) * gv
    v_new = ADAM_B2 * v_ref[...] + (1.0 - ADAM_B2) * (gv * gv)
    m_hat = m_new / (1.0 - ADAM_B1 ** ADAM_STEP)
    v_hat = v_new / (1.0 - ADAM_B2 ** ADAM_STEP)
    d_ref[...] = -ADAM_LR * (m_hat / (jnp.sqrt(v_hat) + ADAM_EPS) + ADAM_WD * w_ref[...])
    nm_ref[...] = m_new
    nv_ref[...] = v_new


def _adamw(w, g, m, v, tr):
    R, C = w.shape
    spec = pl.BlockSpec((tr, C), lambda i: (i, 0))
    shp = jax.ShapeDtypeStruct((R, C), F32)
    return pl.pallas_call(
        functools.partial(_adamw_update), name="adamw",
        grid=(R // tr,),
        in_specs=[spec] * 4, out_specs=[spec] * 3, out_shape=[shp] * 3,
        compiler_params=_params(1),
    )(w, g, m, v)


ADAMW_GROUP_STEPS = 16


def _adamw_group(tensors, riders):
    n, nr, steps = len(tensors), len(riders), ADAMW_GROUP_STEPS

    def body(*refs):
        ins, rsrc = refs[:4 * n], refs[4 * n:4 * n + nr]
        outs, rdst = refs[4 * n + nr:7 * n + nr], refs[7 * n + nr:7 * n + 2 * nr]
        sems = refs[7 * n + 2 * nr:]
        i = pl.program_id(0)

        @pl.when(i == 0)
        def _():
            for k, (kind, _) in enumerate(riders):
                _rider_start(kind, rsrc[k], rdst[k], sems[2 * k], sems[2 * k + 1])

        for k in range(n):
            _adamw_update(*ins[4 * k:4 * k + 4], *outs[3 * k:3 * k + 3])

        @pl.when(i == steps - 1)
        def _():
            for k, (kind, _) in enumerate(riders):
                _rider_finish(kind, rsrc[k], rdst[k], sems[2 * k], sems[2 * k + 1])

    anywhere = pl.BlockSpec(memory_space=pl.ANY)
    in_specs, out_specs, out_shape, args = [], [], [], []
    for w, gbuf, row0, m, v in tensors:
        tr = w.shape[0] // steps
        spec = pl.BlockSpec((tr, D), lambda i: (i, 0))
        in_specs += [spec, pl.BlockSpec((tr, D), functools.partial(lambda i, b: (b + i, 0), b=row0 // tr)), spec, spec]
        out_specs += [spec] * 3
        out_shape += [jax.ShapeDtypeStruct(w.shape, F32)] * 3
        args += [w, gbuf, m, v]
    scratch = []
    for rider in riders:
        in_specs.append(anywhere)
        out_specs.append(anywhere)
        out_shape.append(_rider_out(rider))
        scratch += _rider_sems(rider)
        args.append(rider[1])
    res = pl.pallas_call(
        body, name="adamw_group",
        grid=(steps,),
        in_specs=in_specs, out_specs=out_specs, out_shape=out_shape, scratch_shapes=scratch,
        compiler_params=_params(1),
    )(*args)
    return [tuple(res[3 * k:3 * k + 3]) for k in range(n)], list(res[3 * n:])


def _mesh_pos():
    return lax.axis_index("x"), lax.axis_index("y"), lax.axis_index("c")


def _half(c, rows):
    return pl.ds(pl.multiple_of(c * rows, 16), rows)


def _other_chips(x, y):
    return [(1 - x, y), (x, 1 - y), (1 - x, 1 - y)]


def _remote(k, src, dst, to, send_sems, recv_sems):
    return pltpu.make_async_remote_copy(src_ref=src, dst_ref=dst, send_sem=send_sems.at[k],
                                        recv_sem=recv_sems.at[k], device_id=to, device_id_type=MESH)


def _gather_start(wp_ref, g_ref, send_sems, recv_sems):
    x, y, c = _mesh_pos()
    mine = _half(c, wp_ref.shape[0] // 2)
    for k, (px, py) in enumerate(_other_chips(x, y)):
        _remote(k, wp_ref.at[mine], g_ref.at[2 * x + y, mine], (px, py, c), send_sems, recv_sems).start()


def _gather_forward(wp_ref, g_ref, send_sems, recv_sems):
    x, y, c = _mesh_pos()
    sibling = (x, y, 1 - c)
    mine = _half(c, wp_ref.shape[0] // 2)
    for k, (px, py) in enumerate(_other_chips(x, y)):
        land = g_ref.at[2 * px + py, mine]
        _remote(k, land, land, (px, py, c), send_sems, recv_sems).wait_recv()
        _remote(3 + k, land, land, sibling, send_sems, recv_sems).start()


def _gather_finish(wp_ref, g_ref, send_sems, recv_sems):
    x, y, c = _mesh_pos()
    sibling = (x, y, 1 - c)
    rows = wp_ref.shape[0] // 2
    mine, other = _half(c, rows), _half(1 - c, rows)
    chips = _other_chips(x, y)
    for k, (px, py) in enumerate(chips):
        land = g_ref.at[2 * px + py, other]
        _remote(3 + k, land, land, sibling, send_sems, recv_sems).wait_recv()
    for k, (px, py) in enumerate(chips):
        land = g_ref.at[2 * px + py, mine]
        _remote(k, wp_ref.at[mine], g_ref.at[2 * x + y, mine], (px, py, c), send_sems, recv_sems).wait_send()
        _remote(3 + k, land, land, sibling, send_sems, recv_sems).wait_send()


def _place_own_shard(g, wp):
    x, y, _ = _mesh_pos()
    return lax.dynamic_update_index_in_dim(g, wp, 2 * x + y, 0)


def _all_gather_weights(wp):
    def body(wp_ref, g_ref, send_sems, recv_sems):
        _gather_start(wp_ref, g_ref, send_sems, recv_sems)
        _gather_forward(wp_ref, g_ref, send_sems, recv_sems)
        _gather_finish(wp_ref, g_ref, send_sems, recv_sems)

    g = pl.pallas_call(
        body, name="all_gather_weights",
        in_specs=[pl.BlockSpec(memory_space=pl.ANY)],
        out_specs=pl.BlockSpec(memory_space=pl.ANY),
        out_shape=jax.ShapeDtypeStruct((NSH,) + wp.shape, wp.dtype),
        scratch_shapes=[pltpu.SemaphoreType.DMA((6,)), pltpu.SemaphoreType.DMA((6,))],
        compiler_params=pltpu.CompilerParams(has_side_effects=True),
    )(wp)
    return _place_own_shard(g, wp)


def _scatter_copies(q_ref, t_ref, send_sems, recv_sems):
    x, y, c = _mesh_pos()
    return [_remote(k, q_ref.at[2 * px + py], t_ref.at[k], (px, py, c), send_sems, recv_sems)
            for k, (px, py) in enumerate(_other_chips(x, y))]


_FLIPS = [(fx, fy, fc) for fx in (0, 1) for fy in (0, 1) for fc in (0, 1)][1:]


def _rider_copies(kind, src_ref, dst_ref, send_sems, recv_sems):
    if kind == "scatter":
        return _scatter_copies(src_ref, dst_ref, send_sems, recv_sems)
    x, y, c = _mesh_pos()
    if kind == "broadcast":
        return [_remote(k, src_ref, dst_ref.at[4 * x + 2 * y + c], (x ^ fx, y ^ fy, c ^ fc), send_sems, recv_sems)
                for k, (fx, fy, fc) in enumerate(_FLIPS)]
    rows = src_ref.shape[1] // 2
    return [_remote(0, src_ref.at[:, _half(1 - c, rows)], dst_ref, (x, y, 1 - c), send_sems, recv_sems)]


def _place_own_block(sm, block):
    x, y, c = _mesh_pos()
    return lax.dynamic_update_index_in_dim(sm, block, 4 * x + 2 * y + c, 0)


def _rider_start(kind, src_ref, dst_ref, send_sems, recv_sems):
    for cp in _rider_copies(kind, src_ref, dst_ref, send_sems, recv_sems):
        cp.start()


def _rider_finish(kind, src_ref, dst_ref, send_sems, recv_sems):
    for cp in _rider_copies(kind, src_ref, dst_ref, send_sems, recv_sems):
        cp.wait()


def _rider_out(rider):
    kind, a = rider
    if kind == "scatter":
        return jax.ShapeDtypeStruct((3,) + a.shape[1:], a.dtype)
    if kind == "broadcast":
        return jax.ShapeDtypeStruct((NDEV,) + a.shape, a.dtype)
    return jax.ShapeDtypeStruct((a.shape[0], a.shape[1] // 2) + a.shape[2:], a.dtype)


def _rider_sems(rider):
    n = {"scatter": 3, "broadcast": 7, "exchange": 1}[rider[0]]
    return [pltpu.SemaphoreType.DMA((n,)), pltpu.SemaphoreType.DMA((n,))]


def _pair_exchange(p):
    rows = p.shape[1] // 2

    def body(p_ref, r_ref, send_sem, recv_sem):
        x, y, c = _mesh_pos()
        cp = pltpu.make_async_remote_copy(src_ref=p_ref.at[:, _half(1 - c, rows)], dst_ref=r_ref, send_sem=send_sem,
                                          recv_sem=recv_sem, device_id=(x, y, 1 - c), device_id_type=MESH)
        cp.start()
        cp.wait()

    return pl.pallas_call(
        body, name="pair_exchange",
        in_specs=[pl.BlockSpec(memory_space=pl.ANY)],
        out_specs=pl.BlockSpec(memory_space=pl.ANY),
        out_shape=jax.ShapeDtypeStruct((NSH, rows, D), F32),
        scratch_shapes=[pltpu.SemaphoreType.DMA, pltpu.SemaphoreType.DMA],
        compiler_params=pltpu.CompilerParams(has_side_effects=True),
    )(p)


def _pair_add(p, r, nb):
    rows = r.shape[1]
    tr = rows // nb

    def body(p_ref, r_ref, q_ref):
        q_ref[...] = (p_ref[...] + r_ref[...]).astype(BF)

    return pl.pallas_call(
        body, name="pair_add", grid=(NSH, nb),
        in_specs=[pl.BlockSpec((None, tr, D), lambda j, i: (j, lax.axis_index("c") * nb + i, 0)),
                  pl.BlockSpec((None, tr, D), lambda j, i: (j, i, 0))],
        out_specs=pl.BlockSpec((None, tr, D), lambda j, i: (j, i, 0)),
        out_shape=jax.ShapeDtypeStruct((NSH, rows, D), BF),
        compiler_params=_params(2),
    )(p, r)


def _shard_sum(p, r, t, nb):
    rows = r.shape[1]
    tr = rows // nb

    def shard():
        return 2 * lax.axis_index("x") + lax.axis_index("y")

    def body(p_ref, r_ref, t_ref, o_ref):
        s = p_ref[...] + r_ref[...]
        for k in range(3):
            s = s + t_ref[k].astype(F32)
        o_ref[...] = s

    return pl.pallas_call(
        body, name="shard_sum", grid=(nb,),
        in_specs=[pl.BlockSpec((None, tr, D), lambda i: (shard(), lax.axis_index("c") * nb + i, 0)),
                  pl.BlockSpec((None, tr, D), lambda i: (shard(), i, 0)),
                  pl.BlockSpec((3, tr, D), lambda i: (0, i, 0))],
        out_specs=pl.BlockSpec((tr, D), lambda i: (i, 0)),
        out_shape=jax.ShapeDtypeStruct((rows, D), F32),
        compiler_params=_params(1),
    )(p, r, t)


def _small_sum(sm):
    def body(sm_ref, o_ref):
        s = sm_ref[0]
        for k in range(1, NDEV):
            s = s + sm_ref[k]
        o_ref[...] = s

    return pl.pallas_call(
        body, name="small_sum",
        in_specs=[pl.BlockSpec(memory_space=pltpu.VMEM)],
        out_specs=pl.BlockSpec(memory_space=pltpu.VMEM),
        out_shape=jax.ShapeDtypeStruct(sm.shape[1:], F32),
    )(sm)


def _pair_gather(halves):
    n = len(halves)

    def body(*refs):
        gh_refs, o_refs, (send_sems, recv_sems) = refs[:n], refs[n:2 * n], refs[2 * n:]
        x, y, c = _mesh_pos()
        sibling = (x, y, 1 - c)
        for g in range(n):
            rows = gh_refs[g].shape[0]
            _remote(g, gh_refs[g], o_refs[g].at[_half(c, rows)], sibling, send_sems, recv_sems).start()
        for g in range(n):
            rows = gh_refs[g].shape[0]
            _remote(g, gh_refs[g], o_refs[g].at[_half(c, rows)], sibling, send_sems, recv_sems).wait_send()
            _remote(g, gh_refs[g], o_refs[g].at[_half(1 - c, rows)], sibling, send_sems, recv_sems).wait_recv()

    anywhere = pl.BlockSpec(memory_space=pl.ANY)
    outs = pl.pallas_call(
        body, name="pair_gather",
        in_specs=[anywhere] * n, out_specs=[anywhere] * n,
        out_shape=[jax.ShapeDtypeStruct((2 * h.shape[0], D), F32) for h in halves],
        scratch_shapes=[pltpu.SemaphoreType.DMA((n,)), pltpu.SemaphoreType.DMA((n,))],
        compiler_params=pltpu.CompilerParams(has_side_effects=True),
    )(*halves)
    c = lax.axis_index("c")
    return [lax.dynamic_update_slice_in_dim(o, h, c * h.shape[0], 0) for o, h in zip(outs, halves)]


def _pad_cols(a, n):
    return jnp.pad(a, ((0, 0), (0, n - a.shape[1])))


def _split_w_in(w_in_full):
    q, k, v = w_in_full[:, :D], w_in_full[:, D:2 * D], w_in_full[:, 2 * D:3 * D]
    f = w_in_full[:, 3 * D:3 * D + H]
    gates = w_in_full[:, 3 * D + H:]
    kv = jnp.stack([k.reshape(D, H, HD), v.reshape(D, H, HD)], axis=2).reshape(D, 2 * D)
    return jnp.concatenate([q, kv, gates], axis=1), _pad_cols(f, HD)


def _merge_w_in_grad(dwcat, dwf):
    kv = dwcat[:, D:3 * D].reshape(D, H, 2, HD)
    return jnp.concatenate([dwcat[:, :D], kv[:, :, 0].reshape(D, D), kv[:, :, 1].reshape(D, D),
                            dwf[:, :H], dwcat[:, 3 * D:]], axis=1)


def _ffn_weight_grads(h, da, db, act, dy, bt):
    g = _mm_tn(h, da, D, D, bt, into=(None, FFN_ROWS, 0, "cols"))
    g = _mm_tn(h, db, D, D, bt, into=(g, FFN_ROWS, 1, "cols"))
    return _mm_tn(act, dy, D, D, bt, into=(g, FFN_ROWS, 2, "rows"))


def _train_step(x, target, wp1, wp2, small, adam, tm, t_attn):
    S = x.shape[0]
    g1pre, g1post = small["ffn1_pre_g"], small["ffn1_post_g"]
    gmpre, gmpost = small["mix_pre_g"], small["mix_post_g"]
    g2pre, g2post = small["ffn2_pre_g"], small["ffn2_post_g"]
    ln_g, ln_b = small["sgu_ln_g"], small["sgu_ln_b"]
    ws = small["sgu_w_s"][0]
    wst = jnp.swapaxes(ws, 1, 2)
    bst = _pad_cols(small["sgu_b_s"][0].T, HD)
    bf = _pad_cols(small["b_forget"], HD)

    w1 = _all_gather_weights(wp1)
    h1, a1, b1, y1, x1, h2, w2 = _ffn_fwd(x, g1pre, w1, g1post, tm, ("norm", gmpre), gather=wp2)
    wout = w2[:, FFN_ROWS:FFN_ROWS + 256, :].reshape(D, D)
    r0 = FFN_ROWS + 256
    w_in_full = jnp.concatenate(
        [blk for j in range(NSH) for blk in (w2[j, r0:r0 + D], w2[j, r0 + D:r0 + 2 * D, :WIN_SH - D])], axis=1)
    wcat, wf = _split_w_in(w_in_full)
    z = _mm(h2, wcat, min(1024, S), D, BF, first_block_scale=SCALE * LOG2E)
    zf = _mm(h2, wf, tm, HD, F32)
    cs = min(512, S)
    c, ccol_b = _forget_cumsum(zf, bf, cs)
    crow = jnp.transpose(c[:, :H]).reshape(H, S // t_attn, t_attn)
    oa, lse_b = _fox_fwd(z, ccol_b, crow, t_attn)
    merged, y2, x2 = _mix_out_fwd(z, oa, ln_g, ln_b, ws, bst, wout, x1, gmpost, tm)
    h3, a3, b3, y3, _, dx3, loss_acc = _ffn_fwd(x2, g2pre, w2, g2post, tm, ("loss", target))
    loss = loss_acc[0, 0]

    dy3, da3, db3, act3, dx2, dg2post, dg2pre = _ffn_bwd(dx3, y3, g2post, a3, b3, w2, x2, g2pre, tm)
    bt = min(2048, S)
    g_ffn2 = _ffn_weight_grads(h3, da3, db3, act3, dy3, bt)

    dz, dy2, doa, dgmpost, dlng, dlnb, dws, dbst, r_ffn2 = _mix_out_bwd(
        dx2, y2, gmpost, wout, z, oa, ln_g, ln_b, ws, wst, bst, tm, ("exchange", g_ffn2))
    q_ffn2 = _pair_add(g_ffn2, r_ffn2, 3)
    g_mix = _mm_tn(merged, dy2, 256, D, bt, into=(None, MIX_ROWS, 0, "rows"))
    dz, dkt, dvt, dc_keys, dc_queries, t_ffn2 = _fox_bwd(z, doa, oa, lse_b, ccol_b, crow, dz, t_attn,
                                                         ("scatter", q_ffn2))
    dz = _fox_bwd_finish(dkt, dvt, dz, t_attn)
    dc = _pad_cols(jnp.transpose(dc_queries.reshape(H, S) - dc_keys.reshape(H, S)), HD)
    dzf, dbf = _forget_bwd(dc, zf, bf, cs)
    dwcat = _mm_tn(h2, dz, D, D, bt)
    dwf = _mm_tn(h2, dzf, D, HD, bt)
    dwin = _merge_w_in_grad(dwcat, dwf)
    dwin_a = jnp.stack([dwin[:, j * WIN_SH:j * WIN_SH + D] for j in range(NSH)])
    dwin_b = jnp.stack([_pad_cols(dwin[:, j * WIN_SH + D:(j + 1) * WIN_SH], D) for j in range(NSH)])
    g_mix = lax.dynamic_update_slice(g_mix, jnp.concatenate([dwin_a, dwin_b], axis=1), (0, 256, 0))
    dx1, dgmpre, r_mix = _mix_in_bwd(dz, wcat, dzf, wf, x1, gmpre, dx2, min(256, S), ("exchange", g_mix))
    q_mix = _pair_add(g_mix, r_mix, 3)

    small_early = _pack_small({
        "mix_pre_g": dgmpre, "mix_post_g": dgmpost, "ffn2_pre_g": dg2pre, "ffn2_post_g": dg2post,
        "sgu_ln_g": dlng, "sgu_ln_b": dlnb, "sgu_w_s": dws[None], "sgu_b_s": jnp.transpose(dbst[:, :G])[None],
        "b_forget": dbf[:, :H]}, _SMALL_EARLY)
    dy1, da1, db1, act1, dx, dg1post, dg1pre, t_mix, sm_early = _ffn_bwd(
        dx1, y1, g1post, a1, b1, w1, x, g1pre, tm, riders=(("scatter", q_mix), ("broadcast", small_early)))
    sm_early = _place_own_block(sm_early, small_early)
    g_gu = _mm_tn(h1, da1, D, D, bt, into=(None, 2 * D, 0, "cols"))
    g_gu = _mm_tn(h1, db1, D, D, bt, into=(g_gu, 2 * D, 1, "cols"))
    r_gu = _pair_exchange(g_gu)
    q_gu = _pair_add(g_gu, r_gu, 2)
    g_dn, t_gu = _mm_tn(act1, dy1, D, D, bt, into=(None, D, 0, "rows"), rider=("scatter", q_gu))
    r_dn = _pair_exchange(g_dn)
    q_dn = _pair_add(g_dn, r_dn, 2)
    small_late = _pack_small({"ffn1_pre_g": dg1pre, "ffn1_post_g": dg1post}, _SMALL_LATE)

    f_gu, f_ffn2, f_mix = _pair_gather([_shard_sum(g_gu, r_gu, t_gu, 2), _shard_sum(g_ffn2, r_ffn2, t_ffn2, 3),
                                        _shard_sum(g_mix, r_mix, t_mix, 3)])
    group = [("ffn1_w_gate", f_gu, 0), ("ffn1_w_up", f_gu, D), ("ffn2_w_gate", f_ffn2, 0), ("ffn2_w_up", f_ffn2, D),
             ("ffn2_w_down", f_ffn2, 2 * D), ("w_out", f_mix, 0)]
    updates, (t_dn, sm_late) = _adamw_group(
        [(adam[n][0], buf, row0, adam[n][1], adam[n][2]) for n, buf, row0 in group],
        (("scatter", q_dn), ("broadcast", small_late)))
    sm_late = _place_own_block(sm_late, small_late)
    (f_dn,) = _pair_gather([_shard_sum(g_dn, r_dn, t_dn, 2)])
    gsm = jnp.concatenate([_small_sum(sm_early), _small_sum(sm_late)], axis=0)
    updated = {n: u for (n, _, _), u in zip(group, updates)}
    return loss, dx, jnp.concatenate([f_gu, f_dn], axis=0), f_ffn2, f_mix, gsm, updated


_SMALL_EARLY = ["mix_pre_g", "mix_post_g", "ffn2_pre_g", "ffn2_post_g", "sgu_ln_g", "sgu_ln_b", "sgu_b_s", "b_forget",
                "sgu_w_s"]
_SMALL_LATE = ["ffn1_pre_g", "ffn1_post_g"]
_SMALL_NAMES = _SMALL_EARLY + _SMALL_LATE


def _pack_small(d, names=None):
    rows = []
    for n in names or _SMALL_NAMES:
        a = d[n].astype(F32)
        if n == "b_forget":
            a = _pad_cols(a, D)
        a = a.reshape(-1, D)
        rows.append(jnp.pad(a, ((0, -a.shape[0] % SMALL_STRIDE), (0, 0))))
    return jnp.concatenate(rows, axis=0)


def _unpack_small(p):
    out, r = {}, 0
    for n in _SMALL_NAMES:
        if n == "sgu_w_s":
            out[n] = p[r:r + L].reshape(1, G, L, L)
            r += L
        elif n == "b_forget":
            out[n] = p[r:r + 1, :H]
            r += SMALL_STRIDE
        elif n == "sgu_b_s":
            out[n] = p[r:r + 1].reshape(1, G, L)
            r += SMALL_STRIDE
        else:
            out[n] = p[r:r + 1]
            r += SMALL_STRIDE
    return out


_BIG_NAMES = ["ffn1_w_gate", "ffn1_w_up", "ffn1_w_down", "ffn2_w_gate", "ffn2_w_up", "ffn2_w_down", "w_out", "w_in"]
_WEIGHT_ORDER = ['ffn1_pre_g', 'ffn1_w_gate', 'ffn1_w_up', 'ffn1_w_down', 'ffn1_post_g', 'mix_pre_g', 'w_in', 'b_forget',
                 'sgu_ln_g', 'sgu_ln_b', 'sgu_w_s', 'sgu_b_s', 'w_out', 'mix_post_g', 'ffn2_pre_g', 'ffn2_w_gate',
                 'ffn2_w_up', 'ffn2_w_down', 'ffn2_post_g']


def _pack_ffn(w, name):
    return jnp.concatenate([w[name + "_w_gate"][0], w[name + "_w_up"][0], w[name + "_w_down"][0]], axis=0)


def _pack_mix(w):
    w_in = w["w_in"][0]
    return jnp.concatenate([w["w_out"][0], w_in[:, :D], _pad_cols(w_in[:, D:], D)], axis=0)


def _unpack_ffn(p, name):
    return {name + "_w_gate": p[:D][None], name + "_w_up": p[D:2 * D][None], name + "_w_down": p[2 * D:][None]}


def _unpack_mix(p):
    return {"w_out": p[:256][None],
            "w_in": jnp.concatenate([p[256:256 + D], p[256 + D:, :WIN_SH - D]], axis=1)[None]}


def _step(args, tm, t_attn):
    x = args["x"][0]
    target = args["loss_target"][0]
    weights = {n: args[n] for n in _WEIGHT_ORDER}
    small = {n: weights[n] for n in _SMALL_NAMES}

    wb = {n: weights[n].astype(BF) for n in _BIG_NAMES}
    wp1 = _pack_ffn(wb, "ffn1")
    wp2 = jnp.concatenate([_pack_ffn(wb, "ffn2"), _pack_mix(wb)], axis=0)
    early = ["ffn1_w_gate", "ffn1_w_up", "ffn2_w_gate", "ffn2_w_up", "ffn2_w_down", "w_out"]
    adam = {n: tuple(a.reshape(-1, D) for a in (weights[n], args["m_" + n], args["v_" + n])) for n in early}
    loss_local, dx, f_ffn1, f_ffn2, f_mix, gsm, updated = _train_step(x, target, wp1, wp2, small, adam, tm, t_attn)
    loss = lax.psum(loss_local, ("x", "y", "c"))
    grads = {**_unpack_ffn(f_ffn1, "ffn1"), **_unpack_ffn(f_ffn2, "ffn2"), **_unpack_mix(f_mix),
             **_unpack_small(gsm)}

    delta, new_m, new_v = {}, {}, {}
    for n in _BIG_NAMES:
        shp = weights[n].shape
        if n in updated:
            d, nm, nv = updated[n]
        else:
            w2 = weights[n].reshape(-1, shp[-1])
            d, nm, nv = _adamw(w2, grads[n].reshape(w2.shape), args["m_" + n].reshape(w2.shape),
                               args["v_" + n].reshape(w2.shape), w2.shape[0] // 4)
        delta[n], new_m[n], new_v[n] = d.reshape(shp), nm.reshape(shp), nv.reshape(shp)
    ds, nms, nvs = _adamw(_pack_small(small), gsm, _pack_small({n: args["m_" + n] for n in _SMALL_NAMES}),
                          _pack_small({n: args["v_" + n] for n in _SMALL_NAMES}), SMALL_ROWS)
    delta.update(_unpack_small(ds))
    new_m.update(_unpack_small(nms))
    new_v.update(_unpack_small(nvs))

    return (loss, dx[None], *[grads[n] for n in _WEIGHT_ORDER], *[delta[n] for n in _WEIGHT_ORDER],
            *[new_m[n] for n in _WEIGHT_ORDER], *[new_v[n] for n in _WEIGHT_ORDER])


_ARG_NAMES = (["x"] + _WEIGHT_ORDER + ["loss_target"] + ["m_" + n for n in _WEIGHT_ORDER]
              + ["v_" + n for n in _WEIGHT_ORDER])


def kernel(x, ffn1_pre_g, ffn1_w_gate, ffn1_w_up, ffn1_w_down, ffn1_post_g, mix_pre_g, w_in, b_forget, sgu_ln_g, sgu_ln_b, sgu_w_s, sgu_b_s, w_out, mix_post_g, ffn2_pre_g, ffn2_w_gate, ffn2_w_up, ffn2_w_down, ffn2_post_g, loss_target, m_ffn1_pre_g, m_ffn1_w_gate, m_ffn1_w_up, m_ffn1_w_down, m_ffn1_post_g, m_mix_pre_g, m_w_in, m_b_forget, m_sgu_ln_g, m_sgu_ln_b, m_sgu_w_s, m_sgu_b_s, m_w_out, m_mix_post_g, m_ffn2_pre_g, m_ffn2_w_gate, m_ffn2_w_up, m_ffn2_w_down, m_ffn2_post_g, v_ffn1_pre_g, v_ffn1_w_gate, v_ffn1_w_up, v_ffn1_w_down, v_ffn1_post_g, v_mix_pre_g, v_w_in, v_b_forget, v_sgu_ln_g, v_sgu_ln_b, v_sgu_w_s, v_sgu_b_s, v_w_out, v_mix_post_g, v_ffn2_pre_g, v_ffn2_w_gate, v_ffn2_w_up, v_ffn2_w_down, v_ffn2_post_g):
    args = (x, ffn1_pre_g, ffn1_w_gate, ffn1_w_up, ffn1_w_down, ffn1_post_g, mix_pre_g, w_in, b_forget, sgu_ln_g, sgu_ln_b, sgu_w_s, sgu_b_s, w_out, mix_post_g, ffn2_pre_g, ffn2_w_gate, ffn2_w_up, ffn2_w_down, ffn2_post_g, loss_target, m_ffn1_pre_g, m_ffn1_w_gate, m_ffn1_w_up, m_ffn1_w_down, m_ffn1_post_g, m_mix_pre_g, m_w_in, m_b_forget, m_sgu_ln_g, m_sgu_ln_b, m_sgu_w_s, m_sgu_b_s, m_w_out, m_mix_post_g, m_ffn2_pre_g, m_ffn2_w_gate, m_ffn2_w_up, m_ffn2_w_down, m_ffn2_post_g, v_ffn1_pre_g, v_ffn1_w_gate, v_ffn1_w_up, v_ffn1_w_down, v_ffn1_post_g, v_mix_pre_g, v_w_in, v_b_forget, v_sgu_ln_g, v_sgu_ln_b, v_sgu_w_s, v_sgu_b_s, v_w_out, v_mix_post_g, v_ffn2_pre_g, v_ffn2_w_gate, v_ffn2_w_up, v_ffn2_w_down, v_ffn2_post_g)
    named = dict(zip(_ARG_NAMES, args))
    tile = min(512, x.shape[1])
    return _step(named, tile, tile)
```

```python
import functools
import math

import jax
import jax.numpy as jnp
from jax import lax
from jax.experimental import pallas as pl
from jax.experimental.pallas import tpu as pltpu

D = 1024
F = 4096
H = 8
HD = 128
G = 8
L = 128
CHUNK = 64
NSH = 4
NDEV = 8
ZW = 7 * D
RMS_EPS = 1e-6
LN_EPS = 1e-5
NEG = -1e30
SCALE = 1.0 / math.sqrt(HD)
LOG2E = math.log2(math.e)
LN2 = math.log(2.0)

ADAM_LR = 0.001
ADAM_B1 = 0.9
ADAM_B2 = 0.999
ADAM_EPS = 1e-08
ADAM_WD = 0.01
ADAM_STEP = 10

VMEM_LIMIT_BYTES = 56 * 1024 * 1024

WIN_SH = 1794
FFN_ROWS = 3 * D
MIX_ROWS = 256 + 2 * D
G2_ROWS = FFN_ROWS + MIX_ROWS
SMALL_STRIDE = 8
SMALL_ROWS = 10 * SMALL_STRIDE + L

BF = jnp.bfloat16
F32 = jnp.float32
MESH = pl.DeviceIdType.MESH


def _params(n_grid):
    return pltpu.CompilerParams(dimension_semantics=("arbitrary",) * n_grid,
                                vmem_limit_bytes=VMEM_LIMIT_BYTES)


def _dot(a, b):
    return jnp.dot(a, b, preferred_element_type=F32)


def _dot_nt(a, b):
    return lax.dot_general(a, b, (((1,), (1,)), ((), ())), preferred_element_type=F32)


def _dot_tn(a, b):
    return lax.dot_general(a, b, (((0,), (0,)), ((), ())), preferred_element_type=F32)


def _rms(x, g):
    r = lax.rsqrt(jnp.mean(x * x, axis=-1, keepdims=True) + RMS_EPS)
    return x * r * g


def _rms_bwd(dn, x, g):
    r = lax.rsqrt(jnp.mean(x * x, axis=-1, keepdims=True) + RMS_EPS)
    xr = x * r
    dg = jnp.sum(dn * xr, axis=0, keepdims=True)
    t = dn * g
    dx = r * (t - xr * jnp.mean(t * xr, axis=-1, keepdims=True))
    return dx, dg


def _gelu_parts(x):
    cdf = 0.5 * (1.0 + lax.erf(x * (1.0 / math.sqrt(2.0))))
    pdf = jnp.exp(-0.5 * x * x) * (1.0 / math.sqrt(2.0 * math.pi))
    return x * cdf, cdf + x * pdf


def _gelu(x):
    return x * (0.5 * (1.0 + lax.erf(x * (1.0 / math.sqrt(2.0)))))


def _sigmoid(x):
    return 0.5 * jnp.tanh(0.5 * x) + 0.5


def _ffn_fwd(x, g_pre, wpack, g_post, tm, tail, gather=None):
    S = x.shape[0]
    nt, nf, tf = S // tm, NSH, D
    n_tail_out = 1 if tail[0] == "norm" else 2

    def body(x_ref, gpre_ref, wg_ref, wu_ref, wd_ref, gpost_ref, tail_ref, *rest):
        if gather is not None:
            wp_ref, rest = rest[0], rest[1:]
        h_ref, a_ref, b_ref, y_ref, xo_ref = rest[:5]
        tail_out = rest[5:5 + n_tail_out]
        rest = rest[5 + n_tail_out:]
        if gather is not None:
            g_ref, h_s, acc, send_sems, recv_sems = rest
        else:
            h_s, acc = rest
        i = pl.program_id(0)
        j = pl.program_id(1)

        if gather is not None:
            @pl.when((i == 0) & (j == 0))
            def _():
                _gather_start(wp_ref, g_ref, send_sems, recv_sems)

        @pl.when(j == 0)
        def _():
            h = _rms(x_ref[...], gpre_ref[...]).astype(BF)
            h_s[...] = h
            h_ref[...] = h
            acc[...] = jnp.zeros_like(acc)

        h = h_s[...]
        a = _dot(h, wg_ref[...])
        b = _dot(h, wu_ref[...])
        a_ref[...] = a.astype(BF)
        b_ref[...] = b.astype(BF)
        act = (a * _sigmoid(a)) * b
        acc[...] += _dot(act.astype(BF), wd_ref[...])

        if tail[0] == "loss":
            @pl.when((i == 0) & (j == 0))
            def _():
                tail_out[1][...] = jnp.zeros_like(tail_out[1])

        @pl.when(j == nf - 1)
        def _():
            y = acc[...]
            y_ref[...] = y
            xo = x_ref[...] + 0.5 * _rms(y, gpost_ref[...])
            xo_ref[...] = xo
            if tail[0] == "norm":
                tail_out[0][...] = _rms(xo, tail_ref[...]).astype(BF)
            else:
                e = xo - tail_ref[...]
                tail_out[0][...] = e * (1.0 / D)
                tail_out[1][...] += jnp.sum(e * e) * (0.5 / D)

        if gather is not None:
            @pl.when((i == max(nt - 2, 0)) & (j == 0))
            def _():
                _gather_forward(wp_ref, g_ref, send_sems, recv_sems)

            @pl.when((i == nt - 1) & (j == nf - 1))
            def _():
                _gather_finish(wp_ref, g_ref, send_sems, recv_sems)

    row = pl.BlockSpec((tm, D), lambda i, j: (i, 0))
    vec = pl.BlockSpec((1, D), lambda i, j: (0, 0))
    anywhere = pl.BlockSpec(memory_space=pl.ANY)
    in_specs = [row, vec,
                pl.BlockSpec((None, D, tf), lambda i, j: (j, 0, 0)),
                pl.BlockSpec((None, D, tf), lambda i, j: (j, 1, 0)),
                pl.BlockSpec((None, tf, D), lambda i, j: (j, 2, 0)),
                vec, vec if tail[0] == "norm" else row]
    out_specs = [row,
                 pl.BlockSpec((tm, tf), lambda i, j: (i, j)),
                 pl.BlockSpec((tm, tf), lambda i, j: (i, j)),
                 row, row]
    out_shape = [jax.ShapeDtypeStruct((S, D), BF),
                 jax.ShapeDtypeStruct((S, F), BF),
                 jax.ShapeDtypeStruct((S, F), BF),
                 jax.ShapeDtypeStruct((S, D), F32),
                 jax.ShapeDtypeStruct((S, D), F32)]
    if tail[0] == "norm":
        out_specs.append(row)
        out_shape.append(jax.ShapeDtypeStruct((S, D), BF))
    else:
        out_specs += [row, pl.BlockSpec((8, HD), lambda i, j: (0, 0))]
        out_shape += [jax.ShapeDtypeStruct((S, D), F32), jax.ShapeDtypeStruct((8, HD), F32)]
    scratch = [pltpu.VMEM((tm, D), BF), pltpu.VMEM((tm, D), F32)]
    args = [x, g_pre, wpack, wpack, wpack, g_post, tail[1]]
    if gather is not None:
        in_specs.append(anywhere)
        out_specs.append(anywhere)
        out_shape.append(jax.ShapeDtypeStruct((NSH,) + gather.shape, gather.dtype))
        scratch += [pltpu.SemaphoreType.DMA((6,)), pltpu.SemaphoreType.DMA((6,))]
        args.append(gather)
    res = list(pl.pallas_call(
        body, name="ffn_fwd" if gather is None else "ffn_fwd_gather",
        grid=(nt, nf),
        in_specs=in_specs, out_specs=out_specs, out_shape=out_shape, scratch_shapes=scratch,
        compiler_params=_params(2),
    )(*args))
    if gather is not None:
        res[-1] = _place_own_shard(res[-1], gather)
    return res


def _ffn_bwd(dxo, y, g_post, a, b, wpack, x_in, g_pre, tm, riders=()):
    S = dxo.shape[0]
    nt, nf, tf = S // tm, NSH, D
    nr = len(riders)

    def body(dxo_ref, y_ref, gpost_ref, a_ref, b_ref, wg_ref, wu_ref, wd_ref, xin_ref, gpre_ref, *rest):
        rsrc = rest[:nr]
        dy_ref, da_ref, db_ref, act_ref, dxin_ref, dgpost_ref, dgpre_ref = rest[nr:nr + 7]
        rdst = rest[nr + 7:2 * nr + 7]
        dy_s, acc = rest[2 * nr + 7:2 * nr + 9]
        sems = rest[2 * nr + 9:]
        i = pl.program_id(0)
        j = pl.program_id(1)

        @pl.when((i == 0) & (j == 0))
        def _():
            dgpost_ref[...] = jnp.zeros_like(dgpost_ref)
            dgpre_ref[...] = jnp.zeros_like(dgpre_ref)
            for k, (kind, _) in enumerate(riders):
                _rider_start(kind, rsrc[k], rdst[k], sems[2 * k], sems[2 * k + 1])

        @pl.when(j == 0)
        def _():
            dy, dg = _rms_bwd(0.5 * dxo_ref[...], y_ref[...], gpost_ref[...])
            dyb = dy.astype(BF)
            dy_s[...] = dyb
            dy_ref[...] = dyb
            dgpost_ref[...] += dg
            acc[...] = jnp.zeros_like(acc)

        dact = _dot_nt(dy_s[...], wd_ref[...])
        av = a_ref[...].astype(F32)
        bv = b_ref[...].astype(F32)
        sig = _sigmoid(av)
        sl = av * sig
        act_ref[...] = (sl * bv).astype(BF)
        dbb = (dact * sl).astype(BF)
        dab = (dact * bv * (sig * (1.0 + av * (1.0 - sig)))).astype(BF)
        da_ref[...] = dab
        db_ref[...] = dbb
        acc[...] += _dot_nt(dab, wg_ref[...]) + _dot_nt(dbb, wu_ref[...])

        @pl.when(j == nf - 1)
        def _():
            dx, dg = _rms_bwd(acc[...], xin_ref[...], gpre_ref[...])
            dxin_ref[...] = dxo_ref[...] + dx
            dgpre_ref[...] += dg

        if riders:
            @pl.when((i == nt - 1) & (j == nf - 1))
            def _():
                for k, (kind, _) in enumerate(riders):
                    _rider_finish(kind, rsrc[k], rdst[k], sems[2 * k], sems[2 * k + 1])

    row = pl.BlockSpec((tm, D), lambda i, j: (i, 0))
    vec = pl.BlockSpec((1, D), lambda i, j: (0, 0))
    ff = pl.BlockSpec((tm, tf), lambda i, j: (i, j))
    anywhere = pl.BlockSpec(memory_space=pl.ANY)
    in_specs = [row, row, vec, ff, ff,
                pl.BlockSpec((None, D, tf), lambda i, j: (j, 0, 0)),
                pl.BlockSpec((None, D, tf), lambda i, j: (j, 1, 0)),
                pl.BlockSpec((None, tf, D), lambda i, j: (j, 2, 0)),
                row, vec]
    out_specs = [row, ff, ff, ff, row, vec, vec]
    out_shape = [jax.ShapeDtypeStruct((S, D), BF),
                 jax.ShapeDtypeStruct((S, F), BF),
                 jax.ShapeDtypeStruct((S, F), BF),
                 jax.ShapeDtypeStruct((S, F), BF),
                 jax.ShapeDtypeStruct((S, D), F32),
                 jax.ShapeDtypeStruct((1, D), F32),
                 jax.ShapeDtypeStruct((1, D), F32)]
    scratch = [pltpu.VMEM((tm, D), BF), pltpu.VMEM((tm, D), F32)]
    args = [dxo, y, g_post, a, b, wpack, wpack, wpack, x_in, g_pre]
    for rider in riders:
        in_specs.append(anywhere)
        out_specs.append(anywhere)
        out_shape.append(_rider_out(rider))
        scratch += _rider_sems(rider)
        args.append(rider[1])
    return pl.pallas_call(
        body, name="ffn_bwd" if not riders else "ffn_bwd_riders",
        grid=(nt, nf),
        in_specs=in_specs, out_specs=out_specs, out_shape=out_shape, scratch_shapes=scratch,
        compiler_params=_params(2),
    )(*args)


def _mm_tn(a, b, bm, bn, bt, into=None, rider=None):
    S, M = a.shape
    N = b.shape[1]
    nt = S // bt
    n_in = 2 + (into is not None and into[0] is not None) + (rider is not None)

    def body(*refs):
        a_ref, b_ref, o_ref = refs[0], refs[1], refs[n_in]
        m, n, t = pl.program_id(0), pl.program_id(1), pl.program_id(2)

        if rider is not None:
            rsrc_ref, rdst_ref, send_sems, recv_sems = refs[n_in - 1], refs[n_in + 1], refs[-2], refs[-1]

            @pl.when((m == 0) & (n == 0) & (t == 0))
            def _():
                _rider_start(rider[0], rsrc_ref, rdst_ref, send_sems, recv_sems)

        @pl.when(t == 0)
        def _():
            o_ref[...] = jnp.zeros_like(o_ref)

        o_ref[...] += _dot_tn(a_ref[...], b_ref[...])

        if rider is not None:
            @pl.when((m == M // bm - 1) & (n == N // bn - 1) & (t == nt - 1))
            def _():
                _rider_finish(rider[0], rsrc_ref, rdst_ref, send_sems, recv_sems)

    in_specs = [pl.BlockSpec((bt, bm), lambda m, n, t: (t, m)),
                pl.BlockSpec((bt, bn), lambda m, n, t: (t, n))]
    args, aliases = [a, b], {}
    if into is None:
        out_spec = pl.BlockSpec((bm, bn), lambda m, n, t: (m, n))
        out_shape = jax.ShapeDtypeStruct((M, N), F32)
    else:
        buf, rows, rb, by = into
        assert bn == D and (M == bm if by == "cols" else (M == NSH * bm and N == D))
        if by == "cols":
            out_spec = pl.BlockSpec((None, bm, bn), lambda m, n, t: (n, rb, 0))
        else:
            out_spec = pl.BlockSpec((None, bm, bn), lambda m, n, t: (m, rb, 0))
        out_shape = jax.ShapeDtypeStruct((NSH, rows, D), F32)
        if buf is not None:
            in_specs.append(pl.BlockSpec(memory_space=pl.ANY))
            args.append(buf)
            aliases = {2: 0}
    if rider is None:
        return pl.pallas_call(
            body, name="mm_tn",
            grid=(M // bm, N // bn, nt),
            in_specs=in_specs, out_specs=out_spec, out_shape=out_shape,
            input_output_aliases=aliases,
            compiler_params=_params(3),
        )(*args)
    anywhere = pl.BlockSpec(memory_space=pl.ANY)
    return pl.pallas_call(
        body, name="mm_tn_rider",
        grid=(M // bm, N // bn, nt),
        in_specs=in_specs + [anywhere], out_specs=[out_spec, anywhere], out_shape=[out_shape, _rider_out(rider)],
        scratch_shapes=_rider_sems(rider),
        input_output_aliases=aliases,
        compiler_params=_params(3),
    )(*args, rider[1])


def _mm(a, w, tm, tn, out_dtype, first_block_scale=1.0):
    S, K = a.shape
    N = w.shape[1]

    def body(a_ref, w_ref, o_ref):
        r = _dot(a_ref[...], w_ref[...])
        if first_block_scale == 1.0:
            o_ref[...] = r.astype(out_dtype)
        elif tn == N:
            o_ref[:, :D] = (r[:, :D] * first_block_scale).astype(out_dtype)
            o_ref[:, D:] = r[:, D:].astype(out_dtype)
        else:
            o_ref[...] = (r * jnp.where(pl.program_id(1) == 0, first_block_scale, 1.0)).astype(out_dtype)

    return pl.pallas_call(
        body, name="mm",
        grid=(S // tm, N // tn),
        in_specs=[pl.BlockSpec((tm, K), lambda i, j: (i, 0)),
                  pl.BlockSpec((K, tn), lambda i, j: (0, j))],
        out_specs=pl.BlockSpec((tm, tn), lambda i, j: (i, j)),
        out_shape=jax.ShapeDtypeStruct((S, N), out_dtype),
        compiler_params=_params(2),
    )(a, w)


def _mix_in_bwd(dz, wcat, dzf, wf, x1, g, dx2, tm, rider):
    S = dz.shape[0]
    nk = 1
    kb = ZW // nk

    def body(dz_ref, w_ref, dzf_ref, wf_ref, x_ref, g_ref, dx2_ref, rsrc_ref, dx1_ref, dg_ref, rdst_ref,
             acc, send_sems, recv_sems):
        i = pl.program_id(0)
        k = pl.program_id(1)

        @pl.when((i == 0) & (k == 0))
        def _():
            _rider_start(rider[0], rsrc_ref, rdst_ref, send_sems, recv_sems)
            dg_ref[...] = jnp.zeros_like(dg_ref)

        @pl.when(k == 0)
        def _():
            acc[...] = _dot_nt(dzf_ref[...], wf_ref[...])

        acc[...] += _dot_nt(dz_ref[...], w_ref[...])

        @pl.when(k == nk - 1)
        def _():
            dx, dg = _rms_bwd(acc[...], x_ref[...], g_ref[...])
            dx1_ref[...] = dx2_ref[...] + dx
            dg_ref[...] += dg

        @pl.when((i == S // tm - 1) & (k == nk - 1))
        def _():
            _rider_finish(rider[0], rsrc_ref, rdst_ref, send_sems, recv_sems)

    row = pl.BlockSpec((tm, D), lambda i, k: (i, 0))
    vec = pl.BlockSpec((1, D), lambda i, k: (0, 0))
    anywhere = pl.BlockSpec(memory_space=pl.ANY)
    return pl.pallas_call(
        body, name="mix_in_bwd",
        grid=(S // tm, nk),
        in_specs=[pl.BlockSpec((tm, kb), lambda i, k: (i, k)),
                  pl.BlockSpec((D, kb), lambda i, k: (0, k)),
                  pl.BlockSpec((tm, HD), lambda i, k: (i, 0)),
                  pl.BlockSpec((D, HD), lambda i, k: (0, 0)),
                  row, vec, row, anywhere],
        out_specs=[row, vec, anywhere],
        out_shape=[jax.ShapeDtypeStruct((S, D), F32), jax.ShapeDtypeStruct((1, D), F32), _rider_out(rider)],
        scratch_shapes=[pltpu.VMEM((tm, D), F32)] + _rider_sems(rider),
        compiler_params=_params(2),
    )(dz, wcat, dzf, wf, x1, g, dx2, rider[1])


def _scan_rows(blk, reverse):
    n = blk.shape[0]
    row = lax.broadcasted_iota(jnp.int32, blk.shape, 0)
    d = 1
    while d < n:
        if reverse:
            blk = blk + jnp.where(row < n - d, pltpu.roll(blk, n - d, 0), 0.0)
        else:
            blk = blk + jnp.where(row >= d, pltpu.roll(blk, d, 0), 0.0)
        d *= 2
    return blk


def _forget_cumsum(zf, bf, cs):
    S = zf.shape[0]

    def body(zf_ref, bf_ref, c_ref, cb_ref, carry):
        @pl.when(pl.program_id(0) == 0)
        def _():
            carry[...] = jnp.zeros_like(carry)

        x = zf_ref[...] + bf_ref[...]
        logf = jnp.minimum(x, 0.0) - jnp.log1p(jnp.exp(-jnp.abs(x)))
        sc = _scan_rows(logf, False) + carry[...]
        carry[...] = sc[cs - 1:cs, :]
        sc = sc * LOG2E
        c_ref[...] = sc
        for h in range(H):
            cb_ref[h] = jnp.broadcast_to(sc[:, h:h + 1], (cs, HD))

    return pl.pallas_call(
        body, name="forget_cumsum",
        grid=(S // cs,),
        in_specs=[pl.BlockSpec((cs, HD), lambda i: (i, 0)), pl.BlockSpec((1, HD), lambda i: (0, 0))],
        out_specs=[pl.BlockSpec((cs, HD), lambda i: (i, 0)),
                   pl.BlockSpec((H, cs, HD), lambda i: (0, i, 0))],
        out_shape=[jax.ShapeDtypeStruct((S, HD), F32), jax.ShapeDtypeStruct((H, S, HD), F32)],
        scratch_shapes=[pltpu.VMEM((1, HD), F32)],
        compiler_params=_params(1),
    )(zf, bf)


def _forget_bwd(dc, zf, bf, cs):
    S = dc.shape[0]
    nc = S // cs

    def body(dc_ref, zf_ref, bf_ref, dzf_ref, dbf_ref, carry):
        @pl.when(pl.program_id(0) == 0)
        def _():
            carry[...] = jnp.zeros_like(carry)
            dbf_ref[...] = jnp.zeros_like(dbf_ref)

        sc = _scan_rows(dc_ref[...], True) + carry[...]
        carry[...] = sc[0:1, :]
        x = zf_ref[...] + bf_ref[...]
        dz = sc / (1.0 + jnp.exp(x))
        dzf_ref[...] = dz.astype(BF)
        dbf_ref[...] += jnp.sum(dz, axis=0, keepdims=True)

    rev = pl.BlockSpec((cs, HD), lambda i: (nc - 1 - i, 0))
    vec = pl.BlockSpec((1, HD), lambda i: (0, 0))
    return pl.pallas_call(
        body, name="forget_bwd",
        grid=(nc,),
        in_specs=[rev, rev, vec],
        out_specs=[rev, vec],
        out_shape=[jax.ShapeDtypeStruct((S, HD), BF), jax.ShapeDtypeStruct((1, HD), F32)],
        scratch_shapes=[pltpu.VMEM((1, HD), F32)],
        compiler_params=_params(1),
    )(dc, zf, bf)


def _lanes(x, n):
    return x if n == HD else jnp.concatenate([x] * (n // HD), axis=1)


def _causal_mask(i, j, t, rows_are_queries):
    r = lax.broadcasted_iota(jnp.int32, (t, t), 0)
    c = lax.broadcasted_iota(jnp.int32, (t, t), 1)
    if rows_are_queries:
        return (j * t + c) <= (i * t + r)
    return (j * t + r) <= (i * t + c)


def _fox_fwd(z, ccol_b, crow, t):
    S = z.shape[0]
    nq = S // t

    def body(q_ref, kv_ref, cc_ref, cr_ref, o_ref, lse_ref, m_s, acc_s, s_a, s_b):
        i = pl.program_id(1)
        ct = cc_ref[...]
        ones = jnp.ones((t, HD), BF)
        m_s[...] = jnp.full_like(m_s, NEG)
        acc_s[...] = jnp.zeros_like(acc_s)

        def scores(j, s_ref):
            off = pl.multiple_of(j * t, t)
            s_ref[...] = _dot_nt(q_ref[...], kv_ref[pl.ds(off, t), :HD]) - cr_ref[pl.ds(j, 1), :]

        def consume(j, s_ref, masked):
            off = pl.multiple_of(j * t, t)
            v1 = jnp.concatenate([kv_ref[pl.ds(off, t), HD:], ones], axis=1)
            s = s_ref[...]
            if masked:
                s = jnp.where(_causal_mask(i, j, t, True), s, NEG)
            m_old = m_s[...]
            m_new = jnp.maximum(m_old, jnp.max(s, axis=1, keepdims=True))
            p = jnp.exp2(s - _lanes(m_new, t))
            alpha = jnp.exp2(m_old - m_new)
            acc_s[...] = _lanes(alpha, 2 * HD) * acc_s[...] + _dot(p.astype(BF), v1)
            m_s[...] = m_new

        scores(0, s_a)

        def pair(jj, carry):
            j = 2 * jj
            scores(j + 1, s_b)
            consume(j, s_a, False)
            scores(j + 2, s_a)
            consume(j + 1, s_b, False)
            return carry

        lax.fori_loop(0, i // 2, pair, 0)

        @pl.when(i % 2 == 0)
        def _():
            consume(i, s_a, True)

        @pl.when(i % 2 == 1)
        def _():
            scores(i, s_b)
            consume(i - 1, s_a, False)
            consume(i, s_b, True)

        l = acc_s[:, HD:]
        o_ref[...] = (acc_s[:, :HD] / l).astype(BF)
        lse_ref[...] = m_s[...] + ct + jnp.log2(l)

    return pl.pallas_call(
        body, name="fox_fwd",
        grid=(H, nq),
        in_specs=[pl.BlockSpec((t, HD), lambda h, i: (i, h)),
                  pl.BlockSpec((S, 2 * HD), lambda h, i: (0, 4 + h)),
                  pl.BlockSpec((None, t, HD), lambda h, i: (h, i, 0)),
                  pl.BlockSpec((None, nq, t), lambda h, i: (h, 0, 0))],
        out_specs=[pl.BlockSpec((t, HD), lambda h, i: (i, h)),
                   pl.BlockSpec((None, t, HD), lambda h, i: (h, i, 0))],
        out_shape=[jax.ShapeDtypeStruct((S, D), BF), jax.ShapeDtypeStruct((H, S, HD), F32)],
        scratch_shapes=[pltpu.VMEM((t, HD), F32), pltpu.VMEM((t, 2 * HD), F32),
                        pltpu.VMEM((t, t), F32), pltpu.VMEM((t, t), F32)],
        compiler_params=_params(2),
    )(z, z, ccol_b, crow)


def _fox_bwd(z, do, o, lse_b, ccol_b, crow, dz, t, rider):
    S = z.shape[0]
    nq = S // t

    def body(q_ref, kv_ref, do_ref, o_ref, lse_ref, cc_ref, cr_ref, dz_in, rsrc_ref,
             dq_ref, dkt_ref, dvt_ref, dck_ref, dcq_ref, rdst_ref, acc_s, r_s, send_sems, recv_sems):
        del dz_in
        i = pl.program_id(1)

        @pl.when((pl.program_id(0) == 0) & (i == 0))
        def _():
            _rider_start(rider[0], rsrc_ref, rdst_ref, send_sems, recv_sems)

        @pl.when(i == 0)
        def _():
            dkt_ref[...] = jnp.zeros_like(dkt_ref)
            dvt_ref[...] = jnp.zeros_like(dvt_ref)
            dck_ref[...] = jnp.zeros_like(dck_ref)

        q = q_ref[...]
        dout = do_ref[...]
        qt = jnp.transpose(q.astype(F32)).astype(BF)
        dot_ = jnp.transpose(dout.astype(F32)).astype(BF)
        off_t = _lanes(lse_ref[...] - cc_ref[...], t)
        delta = jnp.sum(dout.astype(F32) * o_ref[...].astype(F32), axis=1, keepdims=True)
        delta = _lanes(jnp.broadcast_to(delta, (t, HD)), t)
        acc_s[...] = jnp.zeros_like(acc_s)
        r_s[...] = jnp.zeros_like(r_s)

        def step(j, masked):
            off = pl.multiple_of(j * t, t)
            k = kv_ref[pl.ds(off, t), :HD]
            v = kv_ref[pl.ds(off, t), HD:]
            p = jnp.exp2(_dot_nt(q, k) - cr_ref[pl.ds(j, 1), :] - off_t)
            if masked:
                p = jnp.where(_causal_mask(i, j, t, True), p, 0.0)
            ds = p * (_dot_nt(dout, v) - delta)
            dsb = ds.astype(BF)
            acc_s[...] += _dot(dsb, k)
            dkt_ref[j] += _dot(qt, dsb)
            dvt_ref[j] += _dot(dot_, p.astype(BF))
            dck_ref[pl.ds(j, 1), :] += jnp.sum(ds, axis=0, keepdims=True)
            r_s[...] += jnp.sum(ds, axis=1, keepdims=True)

        def full_step(j, carry):
            step(j, False)
            return carry

        lax.fori_loop(0, i, full_step, 0)
        step(i, True)
        dq_ref[...] = (acc_s[...] * SCALE).astype(BF)
        dcq_ref[...] = jnp.transpose(r_s[...])[0:1, :]

        @pl.when((pl.program_id(0) == H - 1) & (i == nq - 1))
        def _():
            _rider_finish(rider[0], rsrc_ref, rdst_ref, send_sems, recv_sems)

    qspec = pl.BlockSpec((t, HD), lambda h, i: (i, h))
    bspec = pl.BlockSpec((None, t, HD), lambda h, i: (h, i, 0))
    rows = pl.BlockSpec((None, nq, t), lambda h, i: (h, 0, 0))
    tspec = pl.BlockSpec((None, nq, HD, t), lambda h, i: (h, 0, 0, 0))
    tshape = jax.ShapeDtypeStruct((H, nq, HD, t), F32)
    anywhere = pl.BlockSpec(memory_space=pl.ANY)
    return pl.pallas_call(
        body, name="fox_bwd",
        grid=(H, nq),
        in_specs=[qspec,
                  pl.BlockSpec((S, 2 * HD), lambda h, i: (0, 4 + h)),
                  qspec, qspec, bspec, bspec, rows, anywhere, anywhere],
        out_specs=[qspec, tspec, tspec, rows, pl.BlockSpec((None, None, 1, t), lambda h, i: (h, i, 0, 0)),
                   anywhere],
        out_shape=[jax.ShapeDtypeStruct((S, ZW), BF), tshape, tshape,
                   jax.ShapeDtypeStruct((H, nq, t), F32), jax.ShapeDtypeStruct((H, nq, 1, t), F32),
                   _rider_out(rider)],
        scratch_shapes=[pltpu.VMEM((t, HD), F32), pltpu.VMEM((t, HD), F32)] + _rider_sems(rider),
        input_output_aliases={7: 0},
        compiler_params=_params(2),
    )(z, z, do, o, lse_b, ccol_b, crow, dz, rider[1])


def _fox_bwd_finish(dkt, dvt, dz, t):
    nq = dkt.shape[1]
    S = nq * t

    def body(dkt_ref, dvt_ref, dz_in, dkv_ref):
        del dz_in
        for j in range(nq):
            rows = slice(j * t, (j + 1) * t)
            dkv_ref[rows, :HD] = (jnp.transpose(dkt_ref[j]) * LN2).astype(BF)
            dkv_ref[rows, HD:] = jnp.transpose(dvt_ref[j]).astype(BF)

    tspec = pl.BlockSpec((None, nq, HD, t), lambda h: (h, 0, 0, 0))
    return pl.pallas_call(
        body, name="fox_bwd_finish",
        grid=(H,),
        in_specs=[tspec, tspec, pl.BlockSpec(memory_space=pl.ANY)],
        out_specs=pl.BlockSpec((S, 2 * HD), lambda h: (0, 4 + h)),
        out_shape=jax.ShapeDtypeStruct((S, ZW), BF),
        input_output_aliases={2: 0},
        compiler_params=_params(1),
    )(dkt, dvt, dz)


def _sgu_mask(transposed):
    r = lax.broadcasted_iota(jnp.int32, (L, L), 0)
    c = lax.broadcasted_iota(jnp.int32, (L, L), 1)
    if transposed:
        return (r // CHUNK) <= (c // CHUNK)
    return (c // CHUNK) <= (r // CHUNK)


def _ln_group(vs, lng, lnb):
    mu = jnp.mean(vs, axis=-1, keepdims=True)
    xc = vs - mu
    rstd = lax.rsqrt(jnp.mean(xc * xc, axis=-1, keepdims=True) + LN_EPS)
    xhat = xc * rstd
    return xhat, rstd, xhat * lng + lnb


def _mix_out_fwd(z, oa, ln_g, ln_b, ws, bst, wout, x1, g_post, tm):
    S = z.shape[0]
    nw = tm // L

    def body(u_ref, sv_ref, ga_ref, gb_ref, oa_ref, lng_ref, lnb_ref, ws_ref, bst_ref, wout_ref, x1_ref, gp_ref,
             mg_ref, y_ref, x2_ref, mg_s):
        mask = _sgu_mask(False)
        for g in range(G):
            cols = slice(g * L, (g + 1) * L)
            wm = jnp.where(mask, ws_ref[g], 0.0).astype(BF)
            bcol = bst_ref[:, g:g + 1]
            lng = lng_ref[:, cols]
            lnb = lnb_ref[:, cols]
            for w in range(nw):
                rows = slice(w * L, (w + 1) * L)
                vs = _gelu(sv_ref[rows, cols].astype(F32))
                _, _, vn = _ln_group(vs, lng, lnb)
                mixed = _dot(wm, vn.astype(BF)) + bcol
                ob = _gelu(u_ref[rows, cols].astype(F32)) * mixed
                mg = (_sigmoid(ga_ref[rows, cols].astype(F32)) * oa_ref[rows, cols].astype(F32)
                      + _sigmoid(gb_ref[rows, cols].astype(F32)) * ob)
                mg_s[rows, cols] = mg.astype(BF)
        mgb = mg_s[...]
        mg_ref[...] = mgb
        y = _dot(mgb, wout_ref[...])
        y_ref[...] = y
        x2_ref[...] = x1_ref[...] + _rms(y, gp_ref[...])

    row = pl.BlockSpec((tm, D), lambda i: (i, 0))
    vec = pl.BlockSpec((1, D), lambda i: (0, 0))

    def zcol(kb):
        return pl.BlockSpec((tm, D), lambda i: (i, kb))

    return pl.pallas_call(
        body, name="mix_out_fwd",
        grid=(S // tm,),
        in_specs=[zcol(3), zcol(4), zcol(5), zcol(6), row, vec, vec,
                  pl.BlockSpec((G, L, L), lambda i: (0, 0, 0)),
                  pl.BlockSpec((L, HD), lambda i: (0, 0)),
                  pl.BlockSpec((D, D), lambda i: (0, 0)),
                  row, vec],
        out_specs=[row, row, row],
        out_shape=[jax.ShapeDtypeStruct((S, D), BF),
                   jax.ShapeDtypeStruct((S, D), F32),
                   jax.ShapeDtypeStruct((S, D), F32)],
        scratch_shapes=[pltpu.VMEM((tm, D), BF)],
        compiler_params=_params(1),
    )(z, z, z, z, oa, ln_g, ln_b, ws, bst, wout, x1, g_post)


def _mix_out_bwd(dx2, y2, g_post, wout, z, oa, ln_g, ln_b, ws, wst, bst, tm, rider):
    S = z.shape[0]
    nw = tm // L

    def body(dx2_ref, y_ref, gp_ref, wout_ref, u_ref, sv_ref, ga_ref, gb_ref, oa_ref, lng_ref, lnb_ref,
             ws_ref, wst_ref, bst_ref, q_ref,
             dz_ref, dy_ref, doa_ref, dgp_ref, dlng_ref, dlnb_ref, dws_ref, dbst_ref, t_ref,
             dzg_s, dm_s, send_sems, recv_sems):
        i = pl.program_id(0)
        c = pl.program_id(1)

        @pl.when((i == 0) & (c == 0))
        def _():
            _rider_start(rider[0], q_ref, t_ref, send_sems, recv_sems)
            dgp_ref[...] = jnp.zeros_like(dgp_ref)
            dlng_ref[...] = jnp.zeros_like(dlng_ref)
            dlnb_ref[...] = jnp.zeros_like(dlnb_ref)
            dws_ref[...] = jnp.zeros_like(dws_ref)
            dbst_ref[...] = jnp.zeros_like(dbst_ref)

        @pl.when(c == 0)
        def _():
            dy, dg = _rms_bwd(dx2_ref[...], y_ref[...], gp_ref[...])
            dyb = dy.astype(BF)
            dy_ref[...] = dyb
            dgp_ref[...] += dg
            dm_s[...] = _dot_nt(dyb, wout_ref[...])
            mask = _sgu_mask(False)
            mask_t = _sgu_mask(True)
            lane = lax.broadcasted_iota(jnp.int32, (L, HD), 1)
            for g in range(G):
                cols = slice(g * L, (g + 1) * L)
                wm = jnp.where(mask, ws_ref[g], 0.0).astype(BF)
                wmt = jnp.where(mask_t, wst_ref[g], 0.0).astype(BF)
                bcol = bst_ref[:, g:g + 1]
                lng = lng_ref[:, cols]
                lnb = lnb_ref[:, cols]
                dws_g = jnp.zeros((L, L), F32)
                dbs_g = jnp.zeros((L, 1), F32)
                dlng_g = jnp.zeros((1, L), F32)
                dlnb_g = jnp.zeros((1, L), F32)
                for w in range(nw):
                    rows = slice(w * L, (w + 1) * L)
                    dm = dm_s[rows, cols]
                    vs, dvs_dz = _gelu_parts(sv_ref[rows, cols].astype(F32))
                    xhat, rstd, vn = _ln_group(vs, lng, lnb)
                    vnb = vn.astype(BF)
                    mixed = _dot(wm, vnb) + bcol
                    u, du_dz = _gelu_parts(u_ref[rows, cols].astype(F32))
                    sga = _sigmoid(ga_ref[rows, cols].astype(F32))
                    sgb = _sigmoid(gb_ref[rows, cols].astype(F32))
                    oav = oa_ref[rows, cols].astype(F32)
                    ob = u * mixed
                    doa_ref[rows, cols] = (dm * sga).astype(BF)
                    dzg_s[2, rows, cols] = (dm * oav * sga * (1.0 - sga)).astype(BF)
                    dzg_s[3, rows, cols] = (dm * ob * sgb * (1.0 - sgb)).astype(BF)
                    dob = dm * sgb
                    dzg_s[0, rows, cols] = (dob * mixed * du_dz).astype(BF)
                    dmixed = dob * u
                    dmb = dmixed.astype(BF)
                    dbs_g += jnp.sum(dmixed, axis=1, keepdims=True)
                    dws_g += _dot_nt(dmb, vnb)
                    dvn = _dot(wmt, dmb)
                    dlng_g += jnp.sum(dvn * xhat, axis=0, keepdims=True)
                    dlnb_g += jnp.sum(dvn, axis=0, keepdims=True)
                    dxh = dvn * lng
                    dvs = rstd * (dxh - jnp.mean(dxh, axis=-1, keepdims=True)
                                  - xhat * jnp.mean(dxh * xhat, axis=-1, keepdims=True))
                    dzg_s[1, rows, cols] = (dvs * dvs_dz).astype(BF)
                dws_ref[g] += jnp.where(mask, dws_g, 0.0)
                dbst_ref[...] += jnp.where(lane == g, dbs_g, 0.0)
                dlng_ref[:, cols] += dlng_g
                dlnb_ref[:, cols] += dlnb_g

        dz_ref[...] = dzg_s[c]

        @pl.when((i == S // tm - 1) & (c == 3))
        def _():
            _rider_finish(rider[0], q_ref, t_ref, send_sems, recv_sems)

    row = pl.BlockSpec((tm, D), lambda i, c: (i, 0))
    vec = pl.BlockSpec((1, D), lambda i, c: (0, 0))
    wsspec = pl.BlockSpec((G, L, L), lambda i, c: (0, 0, 0))
    bspec = pl.BlockSpec((L, HD), lambda i, c: (0, 0))
    anywhere = pl.BlockSpec(memory_space=pl.ANY)

    def zcol(kb):
        return pl.BlockSpec((tm, D), lambda i, c: (i, kb))

    return pl.pallas_call(
        body, name="mix_out_bwd",
        grid=(S // tm, 4),
        in_specs=[row, row, vec, pl.BlockSpec((D, D), lambda i, c: (0, 0)),
                  zcol(3), zcol(4), zcol(5), zcol(6), row, vec, vec, wsspec, wsspec, bspec, anywhere],
        out_specs=[pl.BlockSpec((tm, D), lambda i, c: (i, 3 + c)),
                   row, row, vec, vec, vec, wsspec, bspec, anywhere],
        out_shape=[jax.ShapeDtypeStruct((S, ZW), BF),
                   jax.ShapeDtypeStruct((S, D), BF),
                   jax.ShapeDtypeStruct((S, D), BF),
                   jax.ShapeDtypeStruct((1, D), F32),
                   jax.ShapeDtypeStruct((1, D), F32),
                   jax.ShapeDtypeStruct((1, D), F32),
                   jax.ShapeDtypeStruct((G, L, L), F32),
                   jax.ShapeDtypeStruct((L, HD), F32),
                   _rider_out(rider)],
        scratch_shapes=[pltpu.VMEM((4, tm, D), BF), pltpu.VMEM((tm, D), F32)] + _rider_sems(rider),
        compiler_params=_params(2),
    )(dx2, y2, g_post, wout, z, z, z, z, oa, ln_g, ln_b, ws, wst, bst, rider[1])


def _adamw_update(w_ref, g_ref, m_ref, v_ref, d_ref, nm_ref, nv_ref):
    gv = g_ref[...]
    m_new = ADAM_B1 * m_ref[...] + (1.0 - ADAM_B1) * gv
    v_new = ADAM_B2 * v_ref[...] + (1.0 - ADAM_B2) * (gv * gv)
    m_hat = m_new / (1.0 - ADAM_B1 ** ADAM_STEP)
    v_hat = v_new / (1.0 - ADAM_B2 ** ADAM_STEP)
    d_ref[...] = -ADAM_LR * (m_hat / (jnp.sqrt(v_hat) + ADAM_EPS) + ADAM_WD * w_ref[...])
    nm_ref[...] = m_new
    nv_ref[...] = v_new


def _adamw(w, g, m, v, tr):
    R, C = w.shape
    spec = pl.BlockSpec((tr, C), lambda i: (i, 0))
    shp = jax.ShapeDtypeStruct((R, C), F32)
    return pl.pallas_call(
        functools.partial(_adamw_update), name="adamw",
        grid=(R // tr,),
        in_specs=[spec] * 4, out_specs=[spec] * 3, out_shape=[shp] * 3,
        compiler_params=_params(1),
    )(w, g, m, v)


ADAMW_GROUP_STEPS = 16


def _adamw_group(tensors, riders):
    n, nr, steps = len(tensors), len(riders), ADAMW_GROUP_STEPS

    def body(*refs):
        ins, rsrc = refs[:4 * n], refs[4 * n:4 * n + nr]
        outs, rdst = refs[4 * n + nr:7 * n + nr], refs[7 * n + nr:7 * n + 2 * nr]
        sems = refs[7 * n + 2 * nr:]
        i = pl.program_id(0)

        @pl.when(i == 0)
        def _():
            for k, (kind, _) in enumerate(riders):
                _rider_start(kind, rsrc[k], rdst[k], sems[2 * k], sems[2 * k + 1])

        for k in range(n):
            _adamw_update(*ins[4 * k:4 * k + 4], *outs[3 * k:3 * k + 3])

        @pl.when(i == steps - 1)
        def _():
            for k, (kind, _) in enumerate(riders):
                _rider_finish(kind, rsrc[k], rdst[k], sems[2 * k], sems[2 * k + 1])

    anywhere = pl.BlockSpec(memory_space=pl.ANY)
    in_specs, out_specs, out_shape, args = [], [], [], []
    for w, gbuf, row0, m, v in tensors:
        tr = w.shape[0] // steps
        spec = pl.BlockSpec((tr, D), lambda i: (i, 0))
        in_specs += [spec, pl.BlockSpec((tr, D), functools.partial(lambda i, b: (b + i, 0), b=row0 // tr)), spec, spec]
        out_specs += [spec] * 3
        out_shape += [jax.ShapeDtypeStruct(w.shape, F32)] * 3
        args += [w, gbuf, m, v]
    scratch = []
    for rider in riders:
        in_specs.append(anywhere)
        out_specs.append(anywhere)
        out_shape.append(_rider_out(rider))
        scratch += _rider_sems(rider)
        args.append(rider[1])
    res = pl.pallas_call(
        body, name="adamw_group",
        grid=(steps,),
        in_specs=in_specs, out_specs=out_specs, out_shape=out_shape, scratch_shapes=scratch,
        compiler_params=_params(1),
    )(*args)
    return [tuple(res[3 * k:3 * k + 3]) for k in range(n)], list(res[3 * n:])


def _mesh_pos():
    return lax.axis_index("x"), lax.axis_index("y"), lax.axis_index("c")


def _half(c, rows):
    return pl.ds(pl.multiple_of(c * rows, 16), rows)


def _other_chips(x, y):
    return [(1 - x, y), (x, 1 - y), (1 - x, 1 - y)]


def _remote(k, src, dst, to, send_sems, recv_sems):
    return pltpu.make_async_remote_copy(src_ref=src, dst_ref=dst, send_sem=send_sems.at[k],
                                        recv_sem=recv_sems.at[k], device_id=to, device_id_type=MESH)


def _gather_start(wp_ref, g_ref, send_sems, recv_sems):
    x, y, c = _mesh_pos()
    mine = _half(c, wp_ref.shape[0] // 2)
    for k, (px, py) in enumerate(_other_chips(x, y)):
        _remote(k, wp_ref.at[mine], g_ref.at[2 * x + y, mine], (px, py, c), send_sems, recv_sems).start()


def _gather_forward(wp_ref, g_ref, send_sems, recv_sems):
    x, y, c = _mesh_pos()
    sibling = (x, y, 1 - c)
    mine = _half(c, wp_ref.shape[0] // 2)
    for k, (px, py) in enumerate(_other_chips(x, y)):
        land = g_ref.at[2 * px + py, mine]
        _remote(k, land, land, (px, py, c), send_sems, recv_sems).wait_recv()
        _remote(3 + k, land, land, sibling, send_sems, recv_sems).start()


def _gather_finish(wp_ref, g_ref, send_sems, recv_sems):
    x, y, c = _mesh_pos()
    sibling = (x, y, 1 - c)
    rows = wp_ref.shape[0] // 2
    mine, other = _half(c, rows), _half(1 - c, rows)
    chips = _other_chips(x, y)
    for k, (px, py) in enumerate(chips):
        land = g_ref.at[2 * px + py, other]
        _remote(3 + k, land, land, sibling, send_sems, recv_sems).wait_recv()
    for k, (px, py) in enumerate(chips):
        land = g_ref.at[2 * px + py, mine]
        _remote(k, wp_ref.at[mine], g_ref.at[2 * x + y, mine], (px, py, c), send_sems, recv_sems).wait_send()
        _remote(3 + k, land, land, sibling, send_sems, recv_sems).wait_send()


def _place_own_shard(g, wp):
    x, y, _ = _mesh_pos()
    return lax.dynamic_update_index_in_dim(g, wp, 2 * x + y, 0)


def _all_gather_weights(wp):
    def body(wp_ref, g_ref, send_sems, recv_sems):
        _gather_start(wp_ref, g_ref, send_sems, recv_sems)
        _gather_forward(wp_ref, g_ref, send_sems, recv_sems)
        _gather_finish(wp_ref, g_ref, send_sems, recv_sems)

    g = pl.pallas_call(
        body, name="all_gather_weights",
        in_specs=[pl.BlockSpec(memory_space=pl.ANY)],
        out_specs=pl.BlockSpec(memory_space=pl.ANY),
        out_shape=jax.ShapeDtypeStruct((NSH,) + wp.shape, wp.dtype),
        scratch_shapes=[pltpu.SemaphoreType.DMA((6,)), pltpu.SemaphoreType.DMA((6,))],
        compiler_params=pltpu.CompilerParams(has_side_effects=True),
    )(wp)
    return _place_own_shard(g, wp)


def _scatter_copies(q_ref, t_ref, send_sems, recv_sems):
    x, y, c = _mesh_pos()
    return [_remote(k, q_ref.at[2 * px + py], t_ref.at[k], (px, py, c), send_sems, recv_sems)
            for k, (px, py) in enumerate(_other_chips(x, y))]


_FLIPS = [(fx, fy, fc) for fx in (0, 1) for fy in (0, 1) for fc in (0, 1)][1:]


def _rider_copies(kind, src_ref, dst_ref, send_sems, recv_sems):
    if kind == "scatter":
        return _scatter_copies(src_ref, dst_ref, send_sems, recv_sems)
    x, y, c = _mesh_pos()
    if kind == "broadcast":
        return [_remote(k, src_ref, dst_ref.at[4 * x + 2 * y + c], (x ^ fx, y ^ fy, c ^ fc), send_sems, recv_sems)
                for k, (fx, fy, fc) in enumerate(_FLIPS)]
    rows = src_ref.shape[1] // 2
    return [_remote(0, src_ref.at[:, _half(1 - c, rows)], dst_ref, (x, y, 1 - c), send_sems, recv_sems)]


def _place_own_block(sm, block):
    x, y, c = _mesh_pos()
    return lax.dynamic_update_index_in_dim(sm, block, 4 * x + 2 * y + c, 0)


def _rider_start(kind, src_ref, dst_ref, send_sems, recv_sems):
    for cp in _rider_copies(kind, src_ref, dst_ref, send_sems, recv_sems):
        cp.start()


def _rider_finish(kind, src_ref, dst_ref, send_sems, recv_sems):
    for cp in _rider_copies(kind, src_ref, dst_ref, send_sems, recv_sems):
        cp.wait()


def _rider_out(rider):
    kind, a = rider
    if kind == "scatter":
        return jax.ShapeDtypeStruct((3,) + a.shape[1:], a.dtype)
    if kind == "broadcast":
        return jax.ShapeDtypeStruct((NDEV,) + a.shape, a.dtype)
    return jax.ShapeDtypeStruct((a.shape[0], a.shape[1] // 2) + a.shape[2:], a.dtype)


def _rider_sems(rider):
    n = {"scatter": 3, "broadcast": 7, "exchange": 1}[rider[0]]
    return [pltpu.SemaphoreType.DMA((n,)), pltpu.SemaphoreType.DMA((n,))]


def _pair_exchange(p):
    rows = p.shape[1] // 2

    def body(p_ref, r_ref, send_sem, recv_sem):
        x, y, c = _mesh_pos()
        cp = pltpu.make_async_remote_copy(src_ref=p_ref.at[:, _half(1 - c, rows)], dst_ref=r_ref, send_sem=send_sem,
                                          recv_sem=recv_sem, device_id=(x, y, 1 - c), device_id_type=MESH)
        cp.start()
        cp.wait()

    return pl.pallas_call(
        body, name="pair_exchange",
        in_specs=[pl.BlockSpec(memory_space=pl.ANY)],
        out_specs=pl.BlockSpec(memory_space=pl.ANY),
        out_shape=jax.ShapeDtypeStruct((NSH, rows, D), F32),
        scratch_shapes=[pltpu.SemaphoreType.DMA, pltpu.SemaphoreType.DMA],
        compiler_params=pltpu.CompilerParams(has_side_effects=True),
    )(p)


def _pair_add(p, r, nb):
    rows = r.shape[1]
    tr = rows // nb

    def body(p_ref, r_ref, q_ref):
        q_ref[...] = (p_ref[...] + r_ref[...]).astype(BF)

    return pl.pallas_call(
        body, name="pair_add", grid=(NSH, nb),
        in_specs=[pl.BlockSpec((None, tr, D), lambda j, i: (j, lax.axis_index("c") * nb + i, 0)),
                  pl.BlockSpec((None, tr, D), lambda j, i: (j, i, 0))],
        out_specs=pl.BlockSpec((None, tr, D), lambda j, i: (j, i, 0)),
        out_shape=jax.ShapeDtypeStruct((NSH, rows, D), BF),
        compiler_params=_params(2),
    )(p, r)


def _shard_sum(p, r, t, nb):
    rows = r.shape[1]
    tr = rows // nb

    def shard():
        return 2 * lax.axis_index("x") + lax.axis_index("y")

    def body(p_ref, r_ref, t_ref, o_ref):
        s = p_ref[...] + r_ref[...]
        for k in range(3):
            s = s + t_ref[k].astype(F32)
        o_ref[...] = s

    return pl.pallas_call(
        body, name="shard_sum", grid=(nb,),
        in_specs=[pl.BlockSpec((None, tr, D), lambda i: (shard(), lax.axis_index("c") * nb + i, 0)),
                  pl.BlockSpec((None, tr, D), lambda i: (shard(), i, 0)),
                  pl.BlockSpec((3, tr, D), lambda i: (0, i, 0))],
        out_specs=pl.BlockSpec((tr, D), lambda i: (i, 0)),
        out_shape=jax.ShapeDtypeStruct((rows, D), F32),
        compiler_params=_params(1),
    )(p, r, t)


def _small_sum(sm):
    def body(sm_ref, o_ref):
        s = sm_ref[0]
        for k in range(1, NDEV):
            s = s + sm_ref[k]
        o_ref[...] = s

    return pl.pallas_call(
        body, name="small_sum",
        in_specs=[pl.BlockSpec(memory_space=pltpu.VMEM)],
        out_specs=pl.BlockSpec(memory_space=pltpu.VMEM),
        out_shape=jax.ShapeDtypeStruct(sm.shape[1:], F32),
    )(sm)


def _pair_gather(halves):
    n = len(halves)

    def body(*refs):
        gh_refs, o_refs, (send_sems, recv_sems) = refs[:n], refs[n:2 * n], refs[2 * n:]
        x, y, c = _mesh_pos()
        sibling = (x, y, 1 - c)
        for g in range(n):
            rows = gh_refs[g].shape[0]
            _remote(g, gh_refs[g], o_refs[g].at[_half(c, rows)], sibling, send_sems, recv_sems).start()
        for g in range(n):
            rows = gh_refs[g].shape[0]
            _remote(g, gh_refs[g], o_refs[g].at[_half(c, rows)], sibling, send_sems, recv_sems).wait_send()
            _remote(g, gh_refs[g], o_refs[g].at[_half(1 - c, rows)], sibling, send_sems, recv_sems).wait_recv()

    anywhere = pl.BlockSpec(memory_space=pl.ANY)
    outs = pl.pallas_call(
        body, name="pair_gather",
        in_specs=[anywhere] * n, out_specs=[anywhere] * n,
        out_shape=[jax.ShapeDtypeStruct((2 * h.shape[0], D), F32) for h in halves],
        scratch_shapes=[pltpu.SemaphoreType.DMA((n,)), pltpu.SemaphoreType.DMA((n,))],
        compiler_params=pltpu.CompilerParams(has_side_effects=True),
    )(*halves)
    c = lax.axis_index("c")
    return [lax.dynamic_update_slice_in_dim(o, h, c * h.shape[0], 0) for o, h in zip(outs, halves)]


def _pad_cols(a, n):
    return jnp.pad(a, ((0, 0), (0, n - a.shape[1])))


def _split_w_in(w_in_full):
    q, k, v = w_in_full[:, :D], w_in_full[:, D:2 * D], w_in_full[:, 2 * D:3 * D]
    f = w_in_full[:, 3 * D:3 * D + H]
    gates = w_in_full[:, 3 * D + H:]
    kv = jnp.stack([k.reshape(D, H, HD), v.reshape(D, H, HD)], axis=2).reshape(D, 2 * D)
    return jnp.concatenate([q, kv, gates], axis=1), _pad_cols(f, HD)


def _merge_w_in_grad(dwcat, dwf):
    kv = dwcat[:, D:3 * D].reshape(D, H, 2, HD)
    return jnp.concatenate([dwcat[:, :D], kv[:, :, 0].reshape(D, D), kv[:, :, 1].reshape(D, D),
                            dwf[:, :H], dwcat[:, 3 * D:]], axis=1)


def _ffn_weight_grads(h, da, db, act, dy, bt):
    g = _mm_tn(h, da, D, D, bt, into=(None, FFN_ROWS, 0, "cols"))
    g = _mm_tn(h, db, D, D, bt, into=(g, FFN_ROWS, 1, "cols"))
    return _mm_tn(act, dy, D, D, bt, into=(g, FFN_ROWS, 2, "rows"))


def _train_step(x, target, wp1, wp2, small, adam, tm, t_attn):
    S = x.shape[0]
    g1pre, g1post = small["ffn1_pre_g"], small["ffn1_post_g"]
    gmpre, gmpost = small["mix_pre_g"], small["mix_post_g"]
    g2pre, g2post = small["ffn2_pre_g"], small["ffn2_post_g"]
    ln_g, ln_b = small["sgu_ln_g"], small["sgu_ln_b"]
    ws = small["sgu_w_s"][0]
    wst = jnp.swapaxes(ws, 1, 2)
    bst = _pad_cols(small["sgu_b_s"][0].T, HD)
    bf = _pad_cols(small["b_forget"], HD)

    w1 = _all_gather_weights(wp1)
    h1, a1, b1, y1, x1, h2, w2 = _ffn_fwd(x, g1pre, w1, g1post, tm, ("norm", gmpre), gather=wp2)
    wout = w2[:, FFN_ROWS:FFN_ROWS + 256, :].reshape(D, D)
    r0 = FFN_ROWS + 256
    w_in_full = jnp.concatenate(
        [blk for j in range(NSH) for blk in (w2[j, r0:r0 + D], w2[j, r0 + D:r0 + 2 * D, :WIN_SH - D])], axis=1)
    wcat, wf = _split_w_in(w_in_full)
    z = _mm(h2, wcat, min(256, S), ZW, BF, first_block_scale=SCALE * LOG2E)
    zf = _mm(h2, wf, tm, HD, F32)
    cs = min(512, S)
    c, ccol_b = _forget_cumsum(zf, bf, cs)
    crow = jnp.transpose(c[:, :H]).reshape(H, S // t_attn, t_attn)
    oa, lse_b = _fox_fwd(z, ccol_b, crow, t_attn)
    merged, y2, x2 = _mix_out_fwd(z, oa, ln_g, ln_b, ws, bst, wout, x1, gmpost, tm)
    h3, a3, b3, y3, _, dx3, loss_acc = _ffn_fwd(x2, g2pre, w2, g2post, tm, ("loss", target))
    loss = loss_acc[0, 0]

    dy3, da3, db3, act3, dx2, dg2post, dg2pre = _ffn_bwd(dx3, y3, g2post, a3, b3, w2, x2, g2pre, tm)
    bt = min(2048, S)
    g_ffn2 = _ffn_weight_grads(h3, da3, db3, act3, dy3, bt)

    dz, dy2, doa, dgmpost, dlng, dlnb, dws, dbst, r_ffn2 = _mix_out_bwd(
        dx2, y2, gmpost, wout, z, oa, ln_g, ln_b, ws, wst, bst, tm, ("exchange", g_ffn2))
    q_ffn2 = _pair_add(g_ffn2, r_ffn2, 3)
    g_mix = _mm_tn(merged, dy2, 256, D, bt, into=(None, MIX_ROWS, 0, "rows"))
    dz, dkt, dvt, dc_keys, dc_queries, t_ffn2 = _fox_bwd(z, doa, oa, lse_b, ccol_b, crow, dz, t_attn,
                                                         ("scatter", q_ffn2))
    dz = _fox_bwd_finish(dkt, dvt, dz, t_attn)
    dc = _pad_cols(jnp.transpose(dc_queries.reshape(H, S) - dc_keys.reshape(H, S)), HD)
    dzf, dbf = _forget_bwd(dc, zf, bf, cs)
    dwcat = _mm_tn(h2, dz, D, D, bt)
    dwf = _mm_tn(h2, dzf, D, HD, bt)
    dwin = _merge_w_in_grad(dwcat, dwf)
    dwin_a = jnp.stack([dwin[:, j * WIN_SH:j * WIN_SH + D] for j in range(NSH)])
    dwin_b = jnp.stack([_pad_cols(dwin[:, j * WIN_SH + D:(j + 1) * WIN_SH], D) for j in range(NSH)])
    g_mix = lax.dynamic_update_slice(g_mix, jnp.concatenate([dwin_a, dwin_b], axis=1), (0, 256, 0))
    dx1, dgmpre, r_mix = _mix_in_bwd(dz, wcat, dzf, wf, x1, gmpre, dx2, min(256, S), ("exchange", g_mix))
    q_mix = _pair_add(g_mix, r_mix, 3)

    small_early = _pack_small({
        "mix_pre_g": dgmpre, "mix_post_g": dgmpost, "ffn2_pre_g": dg2pre, "ffn2_post_g": dg2post,
        "sgu_ln_g": dlng, "sgu_ln_b": dlnb, "sgu_w_s": dws[None], "sgu_b_s": jnp.transpose(dbst[:, :G])[None],
        "b_forget": dbf[:, :H]}, _SMALL_EARLY)
    dy1, da1, db1, act1, dx, dg1post, dg1pre, t_mix, sm_early = _ffn_bwd(
        dx1, y1, g1post, a1, b1, w1, x, g1pre, tm, riders=(("scatter", q_mix), ("broadcast", small_early)))
    sm_early = _place_own_block(sm_early, small_early)
    g_gu = _mm_tn(h1, da1, D, D, bt, into=(None, 2 * D, 0, "cols"))
    g_gu = _mm_tn(h1, db1, D, D, bt, into=(g_gu, 2 * D, 1, "cols"))
    r_gu = _pair_exchange(g_gu)
    q_gu = _pair_add(g_gu, r_gu, 2)
    g_dn, t_gu = _mm_tn(act1, dy1, D, D, bt, into=(None, D, 0, "rows"), rider=("scatter", q_gu))
    r_dn = _pair_exchange(g_dn)
    q_dn = _pair_add(g_dn, r_dn, 2)
    small_late = _pack_small({"ffn1_pre_g": dg1pre, "ffn1_post_g": dg1post}, _SMALL_LATE)

    f_gu, f_ffn2, f_mix = _pair_gather([_shard_sum(g_gu, r_gu, t_gu, 2), _shard_sum(g_ffn2, r_ffn2, t_ffn2, 3),
                                        _shard_sum(g_mix, r_mix, t_mix, 3)])
    group = [("ffn1_w_gate", f_gu, 0), ("ffn1_w_up", f_gu, D), ("ffn2_w_gate", f_ffn2, 0), ("ffn2_w_up", f_ffn2, D),
             ("ffn2_w_down", f_ffn2, 2 * D), ("w_out", f_mix, 0)]
    updates, (t_dn, sm_late) = _adamw_group(
        [(adam[n][0], buf, row0, adam[n][1], adam[n][2]) for n, buf, row0 in group],
        (("scatter", q_dn), ("broadcast", small_late)))
    sm_late = _place_own_block(sm_late, small_late)
    (f_dn,) = _pair_gather([_shard_sum(g_dn, r_dn, t_dn, 2)])
    gsm = jnp.concatenate([_small_sum(sm_early), _small_sum(sm_late)], axis=0)
    updated = {n: u for (n, _, _), u in zip(group, updates)}
    return loss, dx, jnp.concatenate([f_gu, f_dn], axis=0), f_ffn2, f_mix, gsm, updated


_SMALL_EARLY = ["mix_pre_g", "mix_post_g", "ffn2_pre_g", "ffn2_post_g", "sgu_ln_g", "sgu_ln_b", "sgu_b_s", "b_forget",
                "sgu_w_s"]
_SMALL_LATE = ["ffn1_pre_g", "ffn1_post_g"]
_SMALL_NAMES = _SMALL_EARLY + _SMALL_LATE


def _pack_small(d, names=None):
    rows = []
    for n in names or _SMALL_NAMES:
        a = d[n].astype(F32)
        if n == "b_forget":
            a = _pad_cols(a, D)
        a = a.reshape(-1, D)
        rows.append(jnp.pad(a, ((0, -a.shape[0] % SMALL_STRIDE), (0, 0))))
    return jnp.concatenate(rows, axis=0)


def _unpack_small(p):
    out, r = {}, 0
    for n in _SMALL_NAMES:
        if n == "sgu_w_s":
            out[n] = p[r:r + L].reshape(1, G, L, L)
            r += L
        elif n == "b_forget":
            out[n] = p[r:r + 1, :H]
            r += SMALL_STRIDE
        elif n == "sgu_b_s":
            out[n] = p[r:r + 1].reshape(1, G, L)
            r += SMALL_STRIDE
        else:
            out[n] = p[r:r + 1]
            r += SMALL_STRIDE
    return out


_BIG_NAMES = ["ffn1_w_gate", "ffn1_w_up", "ffn1_w_down", "ffn2_w_gate", "ffn2_w_up", "ffn2_w_down", "w_out", "w_in"]
_WEIGHT_ORDER = ['ffn1_pre_g', 'ffn1_w_gate', 'ffn1_w_up', 'ffn1_w_down', 'ffn1_post_g', 'mix_pre_g', 'w_in', 'b_forget',
                 'sgu_ln_g', 'sgu_ln_b', 'sgu_w_s', 'sgu_b_s', 'w_out', 'mix_post_g', 'ffn2_pre_g', 'ffn2_w_gate',
                 'ffn2_w_up', 'ffn2_w_down', 'ffn2_post_g']


def _pack_ffn(w, name):
    return jnp.concatenate([w[name + "_w_gate"][0], w[name + "_w_up"][0], w[name + "_w_down"][0]], axis=0)


def _pack_mix(w):
    w_in = w["w_in"][0]
    return jnp.concatenate([w["w_out"][0], w_in[:, :D], _pad_cols(w_in[:, D:], D)], axis=0)


def _unpack_ffn(p, name):
    return {name + "_w_gate": p[:D][None], name + "_w_up": p[D:2 * D][None], name + "_w_down": p[2 * D:][None]}


def _unpack_mix(p):
    return {"w_out": p[:256][None],
            "w_in": jnp.concatenate([p[256:256 + D], p[256 + D:, :WIN_SH - D]], axis=1)[None]}


def _step(args, tm, t_attn):
    x = args["x"][0]
    target = args["loss_target"][0]
    weights = {n: args[n] for n in _WEIGHT_ORDER}
    small = {n: weights[n] for n in _SMALL_NAMES}

    wb = {n: weights[n].astype(BF) for n in _BIG_NAMES}
    wp1 = _pack_ffn(wb, "ffn1")
    wp2 = jnp.concatenate([_pack_ffn(wb, "ffn2"), _pack_mix(wb)], axis=0)
    early = ["ffn1_w_gate", "ffn1_w_up", "ffn2_w_gate", "ffn2_w_up", "ffn2_w_down", "w_out"]
    adam = {n: tuple(a.reshape(-1, D) for a in (weights[n], args["m_" + n], args["v_" + n])) for n in early}
    loss_local, dx, f_ffn1, f_ffn2, f_mix, gsm, updated = _train_step(x, target, wp1, wp2, small, adam, tm, t_attn)
    loss = lax.psum(loss_local, ("x", "y", "c"))
    grads = {**_unpack_ffn(f_ffn1, "ffn1"), **_unpack_ffn(f_ffn2, "ffn2"), **_unpack_mix(f_mix),
             **_unpack_small(gsm)}

    delta, new_m, new_v = {}, {}, {}
    for n in _BIG_NAMES:
        shp = weights[n].shape
        if n in updated:
            d, nm, nv = updated[n]
        else:
            w2 = weights[n].reshape(-1, shp[-1])
            d, nm, nv = _adamw(w2, grads[n].reshape(w2.shape), args["m_" + n].reshape(w2.shape),
                               args["v_" + n].reshape(w2.shape), w2.shape[0] // 4)
        delta[n], new_m[n], new_v[n] = d.reshape(shp), nm.reshape(shp), nv.reshape(shp)
    ds, nms, nvs = _adamw(_pack_small(small), gsm, _pack_small({n: args["m_" + n] for n in _SMALL_NAMES}),
                          _pack_small({n: args["v_" + n] for n in _SMALL_NAMES}), SMALL_ROWS)
    delta.update(_unpack_small(ds))
    new_m.update(_unpack_small(nms))
    new_v.update(_unpack_small(nvs))

    return (loss, dx[None], *[grads[n] for n in _WEIGHT_ORDER], *[delta[n] for n in _WEIGHT_ORDER],
            *[new_m[n] for n in _WEIGHT_ORDER], *[new_v[n] for n in _WEIGHT_ORDER])


_ARG_NAMES = (["x"] + _WEIGHT_ORDER + ["loss_target"] + ["m_" + n for n in _WEIGHT_ORDER]
              + ["v_" + n for n in _WEIGHT_ORDER])


def kernel(x, ffn1_pre_g, ffn1_w_gate, ffn1_w_up, ffn1_w_down, ffn1_post_g, mix_pre_g, w_in, b_forget, sgu_ln_g, sgu_ln_b, sgu_w_s, sgu_b_s, w_out, mix_post_g, ffn2_pre_g, ffn2_w_gate, ffn2_w_up, ffn2_w_down, ffn2_post_g, loss_target, m_ffn1_pre_g, m_ffn1_w_gate, m_ffn1_w_up, m_ffn1_w_down, m_ffn1_post_g, m_mix_pre_g, m_w_in, m_b_forget, m_sgu_ln_g, m_sgu_ln_b, m_sgu_w_s, m_sgu_b_s, m_w_out, m_mix_post_g, m_ffn2_pre_g, m_ffn2_w_gate, m_ffn2_w_up, m_ffn2_w_down, m_ffn2_post_g, v_ffn1_pre_g, v_ffn1_w_gate, v_ffn1_w_up, v_ffn1_w_down, v_ffn1_post_g, v_mix_pre_g, v_w_in, v_b_forget, v_sgu_ln_g, v_sgu_ln_b, v_sgu_w_s, v_sgu_b_s, v_w_out, v_mix_post_g, v_ffn2_pre_g, v_ffn2_w_gate, v_ffn2_w_up, v_ffn2_w_down, v_ffn2_post_g):
    args = (x, ffn1_pre_g, ffn1_w_gate, ffn1_w_up, ffn1_w_down, ffn1_post_g, mix_pre_g, w_in, b_forget, sgu_ln_g, sgu_ln_b, sgu_w_s, sgu_b_s, w_out, mix_post_g, ffn2_pre_g, ffn2_w_gate, ffn2_w_up, ffn2_w_down, ffn2_post_g, loss_target, m_ffn1_pre_g, m_ffn1_w_gate, m_ffn1_w_up, m_ffn1_w_down, m_ffn1_post_g, m_mix_pre_g, m_w_in, m_b_forget, m_sgu_ln_g, m_sgu_ln_b, m_sgu_w_s, m_sgu_b_s, m_w_out, m_mix_post_g, m_ffn2_pre_g, m_ffn2_w_gate, m_ffn2_w_up, m_ffn2_w_down, m_ffn2_post_g, v_ffn1_pre_g, v_ffn1_w_gate, v_ffn1_w_up, v_ffn1_w_down, v_ffn1_post_g, v_mix_pre_g, v_w_in, v_b_forget, v_sgu_ln_g, v_sgu_ln_b, v_sgu_w_s, v_sgu_b_s, v_w_out, v_mix_post_g, v_ffn2_pre_g, v_ffn2_w_gate, v_ffn2_w_up, v_ffn2_w_down, v_ffn2_post_g)
    named = dict(zip(_ARG_NAMES, args))
    tile = min(512, x.shape[1])
    return _step(named, tile, tile)
```

```python
import functools
import math

import jax
import jax.numpy as jnp
from jax import lax
from jax.experimental import pallas as pl
from jax.experimental.pallas import tpu as pltpu

D = 1024
F = 4096
H = 8
HD = 128
G = 8
L = 128
CHUNK = 64
NSH = 4
NDEV = 8
ZW = 7 * D
RMS_EPS = 1e-6
LN_EPS = 1e-5
NEG = -1e30
SCALE = 1.0 / math.sqrt(HD)
LOG2E = math.log2(math.e)
LN2 = math.log(2.0)

ADAM_LR = 0.001
ADAM_B1 = 0.9
ADAM_B2 = 0.999
ADAM_EPS = 1e-08
ADAM_WD = 0.01
ADAM_STEP = 10

VMEM_LIMIT_BYTES = 56 * 1024 * 1024

WIN_SH = 1794
FFN_ROWS = 3 * D
MIX_ROWS = 256 + 2 * D
G2_ROWS = FFN_ROWS + MIX_ROWS
SMALL_STRIDE = 8
SMALL_ROWS = 10 * SMALL_STRIDE + L

BF = jnp.bfloat16
F32 = jnp.float32
MESH = pl.DeviceIdType.MESH


def _params(n_grid):
    return pltpu.CompilerParams(dimension_semantics=("arbitrary",) * n_grid,
                                vmem_limit_bytes=VMEM_LIMIT_BYTES)


def _dot(a, b):
    return jnp.dot(a, b, preferred_element_type=F32)


def _dot_nt(a, b):
    return lax.dot_general(a, b, (((1,), (1,)), ((), ())), preferred_element_type=F32)


def _dot_tn(a, b):
    return lax.dot_general(a, b, (((0,), (0,)), ((), ())), preferred_element_type=F32)


def _rms(x, g):
    r = lax.rsqrt(jnp.mean(x * x, axis=-1, keepdims=True) + RMS_EPS)
    return x * r * g


def _rms_bwd(dn, x, g):
    r = lax.rsqrt(jnp.mean(x * x, axis=-1, keepdims=True) + RMS_EPS)
    xr = x * r
    dg = jnp.sum(dn * xr, axis=0, keepdims=True)
    t = dn * g
    dx = r * (t - xr * jnp.mean(t * xr, axis=-1, keepdims=True))
    return dx, dg


def _gelu_parts(x):
    cdf = 0.5 * (1.0 + lax.erf(x * (1.0 / math.sqrt(2.0))))
    pdf = jnp.exp(-0.5 * x * x) * (1.0 / math.sqrt(2.0 * math.pi))
    return x * cdf, cdf + x * pdf


def _gelu(x):
    return x * (0.5 * (1.0 + lax.erf(x * (1.0 / math.sqrt(2.0)))))


def _sigmoid(x):
    return 0.5 * jnp.tanh(0.5 * x) + 0.5


def _ffn_fwd(x, g_pre, wpack, g_post, tm, tail, gather=None):
    S = x.shape[0]
    nt, nf, tf = S // tm, NSH, D
    n_tail_out = 1 if tail[0] == "norm" else 2

    def body(x_ref, gpre_ref, wg_ref, wu_ref, wd_ref, gpost_ref, tail_ref, *rest):
        if gather is not None:
            wp_ref, rest = rest[0], rest[1:]
        h_ref, a_ref, b_ref, y_ref, xo_ref = rest[:5]
        tail_out = rest[5:5 + n_tail_out]
        rest = rest[5 + n_tail_out:]
        if gather is not None:
            g_ref, h_s, acc, send_sems, recv_sems = rest
        else:
            h_s, acc = rest
        i = pl.program_id(0)
        j = pl.program_id(1)

        if gather is not None:
            @pl.when((i == 0) & (j == 0))
            def _():
                _gather_start(wp_ref, g_ref, send_sems, recv_sems)

        @pl.when(j == 0)
        def _():
            h = _rms(x_ref[...], gpre_ref[...]).astype(BF)
            h_s[...] = h
            h_ref[...] = h
            acc[...] = jnp.zeros_like(acc)

        h = h_s[...]
        a = _dot(h, wg_ref[...])
        b = _dot(h, wu_ref[...])
        a_ref[...] = a.astype(BF)
        b_ref[...] = b.astype(BF)
        act = (a * _sigmoid(a)) * b
        acc[...] += _dot(act.astype(BF), wd_ref[...])

        if tail[0] == "loss":
            @pl.when((i == 0) & (j == 0))
            def _():
                tail_out[1][...] = jnp.zeros_like(tail_out[1])

        @pl.when(j == nf - 1)
        def _():
            y = acc[...]
            y_ref[...] = y
            xo = x_ref[...] + 0.5 * _rms(y, gpost_ref[...])
            xo_ref[...] = xo
            if tail[0] == "norm":
                tail_out[0][...] = _rms(xo, tail_ref[...]).astype(BF)
            else:
                e = xo - tail_ref[...]
                tail_out[0][...] = e * (1.0 / D)
                tail_out[1][...] += jnp.sum(e * e) * (0.5 / D)

        if gather is not None:
            @pl.when((i == max(nt - 2, 0)) & (j == 0))
            def _():
                _gather_forward(wp_ref, g_ref, send_sems, recv_sems)

            @pl.when((i == nt - 1) & (j == nf - 1))
            def _():
                _gather_finish(wp_ref, g_ref, send_sems, recv_sems)

    row = pl.BlockSpec((tm, D), lambda i, j: (i, 0))
    vec = pl.BlockSpec((1, D), lambda i, j: (0, 0))
    anywhere = pl.BlockSpec(memory_space=pl.ANY)
    in_specs = [row, vec,
                pl.BlockSpec((None, D, tf), lambda i, j: (j, 0, 0)),
                pl.BlockSpec((None, D, tf), lambda i, j: (j, 1, 0)),
                pl.BlockSpec((None, tf, D), lambda i, j: (j, 2, 0)),
                vec, vec if tail[0] == "norm" else row]
    out_specs = [row,
                 pl.BlockSpec((tm, tf), lambda i, j: (i, j)),
                 pl.BlockSpec((tm, tf), lambda i, j: (i, j)),
                 row, row]
    out_shape = [jax.ShapeDtypeStruct((S, D), BF),
                 jax.ShapeDtypeStruct((S, F), BF),
                 jax.ShapeDtypeStruct((S, F), BF),
                 jax.ShapeDtypeStruct((S, D), F32),
                 jax.ShapeDtypeStruct((S, D), F32)]
    if tail[0] == "norm":
        out_specs.append(row)
        out_shape.append(jax.ShapeDtypeStruct((S, D), BF))
    else:
        out_specs += [row, pl.BlockSpec((8, HD), lambda i, j: (0, 0))]
        out_shape += [jax.ShapeDtypeStruct((S, D), F32), jax.ShapeDtypeStruct((8, HD), F32)]
    scratch = [pltpu.VMEM((tm, D), BF), pltpu.VMEM((tm, D), F32)]
    args = [x, g_pre, wpack, wpack, wpack, g_post, tail[1]]
    if gather is not None:
        in_specs.append(anywhere)
        out_specs.append(anywhere)
        out_shape.append(jax.ShapeDtypeStruct((NSH,) + gather.shape, gather.dtype))
        scratch += [pltpu.SemaphoreType.DMA((6,)), pltpu.SemaphoreType.DMA((6,))]
        args.append(gather)
    res = list(pl.pallas_call(
        body, name="ffn_fwd" if gather is None else "ffn_fwd_gather",
        grid=(nt, nf),
        in_specs=in_specs, out_specs=out_specs, out_shape=out_shape, scratch_shapes=scratch,
        compiler_params=_params(2),
    )(*args))
    if gather is not None:
        res[-1] = _place_own_shard(res[-1], gather)
    return res


def _ffn_bwd(dxo, y, g_post, a, b, wpack, x_in, g_pre, tm, riders=()):
    S = dxo.shape[0]
    nt, nf, tf = S // tm, NSH, D
    nr = len(riders)

    def body(dxo_ref, y_ref, gpost_ref, a_ref, b_ref, wg_ref, wu_ref, wd_ref, xin_ref, gpre_ref, *rest):
        rsrc = rest[:nr]
        dy_ref, da_ref, db_ref, act_ref, dxin_ref, dgpost_ref, dgpre_ref = rest[nr:nr + 7]
        rdst = rest[nr + 7:2 * nr + 7]
        dy_s, acc = rest[2 * nr + 7:2 * nr + 9]
        sems = rest[2 * nr + 9:]
        i = pl.program_id(0)
        j = pl.program_id(1)

        @pl.when((i == 0) & (j == 0))
        def _():
            dgpost_ref[...] = jnp.zeros_like(dgpost_ref)
            dgpre_ref[...] = jnp.zeros_like(dgpre_ref)
            for k, (kind, _) in enumerate(riders):
                _rider_start(kind, rsrc[k], rdst[k], sems[2 * k], sems[2 * k + 1])

        @pl.when(j == 0)
        def _():
            dy, dg = _rms_bwd(0.5 * dxo_ref[...], y_ref[...], gpost_ref[...])
            dyb = dy.astype(BF)
            dy_s[...] = dyb
            dy_ref[...] = dyb
            dgpost_ref[...] += dg
            acc[...] = jnp.zeros_like(acc)

        dact = _dot_nt(dy_s[...], wd_ref[...])
        av = a_ref[...].astype(F32)
        bv = b_ref[...].astype(F32)
        sig = _sigmoid(av)
        sl = av * sig
        act_ref[...] = (sl * bv).astype(BF)
        dbb = (dact * sl).astype(BF)
        dab = (dact * bv * (sig * (1.0 + av * (1.0 - sig)))).astype(BF)
        da_ref[...] = dab
        db_ref[...] = dbb
        acc[...] += _dot_nt(dab, wg_ref[...]) + _dot_nt(dbb, wu_ref[...])

        @pl.when(j == nf - 1)
        def _():
            dx, dg = _rms_bwd(acc[...], xin_ref[...], gpre_ref[...])
            dxin_ref[...] = dxo_ref[...] + dx
            dgpre_ref[...] += dg

        if riders:
            @pl.when((i == nt - 1) & (j == nf - 1))
            def _():
                for k, (kind, _) in enumerate(riders):
                    _rider_finish(kind, rsrc[k], rdst[k], sems[2 * k], sems[2 * k + 1])

    row = pl.BlockSpec((tm, D), lambda i, j: (i, 0))
    vec = pl.BlockSpec((1, D), lambda i, j: (0, 0))
    ff = pl.BlockSpec((tm, tf), lambda i, j: (i, j))
    anywhere = pl.BlockSpec(memory_space=pl.ANY)
    in_specs = [row, row, vec, ff, ff,
                pl.BlockSpec((None, D, tf), lambda i, j: (j, 0, 0)),
                pl.BlockSpec((None, D, tf), lambda i, j: (j, 1, 0)),
                pl.BlockSpec((None, tf, D), lambda i, j: (j, 2, 0)),
                row, vec]
    out_specs = [row, ff, ff, ff, row, vec, vec]
    out_shape = [jax.ShapeDtypeStruct((S, D), BF),
                 jax.ShapeDtypeStruct((S, F), BF),
                 jax.ShapeDtypeStruct((S, F), BF),
                 jax.ShapeDtypeStruct((S, F), BF),
                 jax.ShapeDtypeStruct((S, D), F32),
                 jax.ShapeDtypeStruct((1, D), F32),
                 jax.ShapeDtypeStruct((1, D), F32)]
    scratch = [pltpu.VMEM((tm, D), BF), pltpu.VMEM((tm, D), F32)]
    args = [dxo, y, g_post, a, b, wpack, wpack, wpack, x_in, g_pre]
    for rider in riders:
        in_specs.append(anywhere)
        out_specs.append(anywhere)
        out_shape.append(_rider_out(rider))
        scratch += _rider_sems(rider)
        args.append(rider[1])
    return pl.pallas_call(
        body, name="ffn_bwd" if not riders else "ffn_bwd_riders",
        grid=(nt, nf),
        in_specs=in_specs, out_specs=out_specs, out_shape=out_shape, scratch_shapes=scratch,
        compiler_params=_params(2),
    )(*args)


def _mm_tn(a, b, bm, bn, bt, into=None, rider=None):
    S, M = a.shape
    N = b.shape[1]
    nt = S // bt
    n_in = 2 + (into is not None and into[0] is not None) + (rider is not None)

    def body(*refs):
        a_ref, b_ref, o_ref = refs[0], refs[1], refs[n_in]
        m, n, t = pl.program_id(0), pl.program_id(1), pl.program_id(2)

        if rider is not None:
            rsrc_ref, rdst_ref, send_sems, recv_sems = refs[n_in - 1], refs[n_in + 1], refs[-2], refs[-1]

            @pl.when((m == 0) & (n == 0) & (t == 0))
            def _():
                _rider_start(rider[0], rsrc_ref, rdst_ref, send_sems, recv_sems)

        @pl.when(t == 0)
        def _():
            o_ref[...] = jnp.zeros_like(o_ref)

        o_ref[...] += _dot_tn(a_ref[...], b_ref[...])

        if rider is not None:
            @pl.when((m == M // bm - 1) & (n == N // bn - 1) & (t == nt - 1))
            def _():
                _rider_finish(rider[0], rsrc_ref, rdst_ref, send_sems, recv_sems)

    in_specs = [pl.BlockSpec((bt, bm), lambda m, n, t: (t, m)),
                pl.BlockSpec((bt, bn), lambda m, n, t: (t, n))]
    args, aliases = [a, b], {}
    if into is None:
        out_spec = pl.BlockSpec((bm, bn), lambda m, n, t: (m, n))
        out_shape = jax.ShapeDtypeStruct((M, N), F32)
    else:
        buf, rows, rb, by = into
        assert bn == D and (M == bm if by == "cols" else (M == NSH * bm and N == D))
        if by == "cols":
            out_spec = pl.BlockSpec((None, bm, bn), lambda m, n, t: (n, rb, 0))
        else:
            out_spec = pl.BlockSpec((None, bm, bn), lambda m, n, t: (m, rb, 0))
        out_shape = jax.ShapeDtypeStruct((NSH, rows, D), F32)
        if buf is not None:
            in_specs.append(pl.BlockSpec(memory_space=pl.ANY))
            args.append(buf)
            aliases = {2: 0}
    if rider is None:
        return pl.pallas_call(
            body, name="mm_tn",
            grid=(M // bm, N // bn, nt),
            in_specs=in_specs, out_specs=out_spec, out_shape=out_shape,
            input_output_aliases=aliases,
            compiler_params=_params(3),
        )(*args)
    anywhere = pl.BlockSpec(memory_space=pl.ANY)
    return pl.pallas_call(
        body, name="mm_tn_rider",
        grid=(M // bm, N // bn, nt),
        in_specs=in_specs + [anywhere], out_specs=[out_spec, anywhere], out_shape=[out_shape, _rider_out(rider)],
        scratch_shapes=_rider_sems(rider),
        input_output_aliases=aliases,
        compiler_params=_params(3),
    )(*args, rider[1])


def _mix_in_fwd(h, wcat, wf, tm):
    S = h.shape[0]

    def body(h_ref, w_ref, wf_ref, z_ref, zf_ref):
        hv = h_ref[...]
        r = _dot(hv, w_ref[...])
        z_ref[:, :D] = (r[:, :D] * (SCALE * LOG2E)).astype(BF)
        z_ref[:, D:] = r[:, D:].astype(BF)
        zf_ref[...] = _dot(hv, wf_ref[...])

    return pl.pallas_call(
        body, name="mix_in_fwd",
        grid=(S // tm,),
        in_specs=[pl.BlockSpec((tm, D), lambda i: (i, 0)),
                  pl.BlockSpec((D, ZW), lambda i: (0, 0)),
                  pl.BlockSpec((D, HD), lambda i: (0, 0))],
        out_specs=[pl.BlockSpec((tm, ZW), lambda i: (i, 0)), pl.BlockSpec((tm, HD), lambda i: (i, 0))],
        out_shape=[jax.ShapeDtypeStruct((S, ZW), BF), jax.ShapeDtypeStruct((S, HD), F32)],
        compiler_params=_params(1),
    )(h, wcat, wf)


def _mix_in_bwd(dz, wcat, dzf, wf, x1, g, dx2, tm, rider):
    S = dz.shape[0]
    nk = 1
    kb = ZW // nk

    def body(dz_ref, w_ref, dzf_ref, wf_ref, x_ref, g_ref, dx2_ref, rsrc_ref, dx1_ref, dg_ref, rdst_ref,
             acc, send_sems, recv_sems):
        i = pl.program_id(0)
        k = pl.program_id(1)

        @pl.when((i == 0) & (k == 0))
        def _():
            _rider_start(rider[0], rsrc_ref, rdst_ref, send_sems, recv_sems)
            dg_ref[...] = jnp.zeros_like(dg_ref)

        @pl.when(k == 0)
        def _():
            acc[...] = _dot_nt(dzf_ref[...], wf_ref[...])

        acc[...] += _dot_nt(dz_ref[...], w_ref[...])

        @pl.when(k == nk - 1)
        def _():
            dx, dg = _rms_bwd(acc[...], x_ref[...], g_ref[...])
            dx1_ref[...] = dx2_ref[...] + dx
            dg_ref[...] += dg

        @pl.when((i == S // tm - 1) & (k == nk - 1))
        def _():
            _rider_finish(rider[0], rsrc_ref, rdst_ref, send_sems, recv_sems)

    row = pl.BlockSpec((tm, D), lambda i, k: (i, 0))
    vec = pl.BlockSpec((1, D), lambda i, k: (0, 0))
    anywhere = pl.BlockSpec(memory_space=pl.ANY)
    return pl.pallas_call(
        body, name="mix_in_bwd",
        grid=(S // tm, nk),
        in_specs=[pl.BlockSpec((tm, kb), lambda i, k: (i, k)),
                  pl.BlockSpec((D, kb), lambda i, k: (0, k)),
                  pl.BlockSpec((tm, HD), lambda i, k: (i, 0)),
                  pl.BlockSpec((D, HD), lambda i, k: (0, 0)),
                  row, vec, row, anywhere],
        out_specs=[row, vec, anywhere],
        out_shape=[jax.ShapeDtypeStruct((S, D), F32), jax.ShapeDtypeStruct((1, D), F32), _rider_out(rider)],
        scratch_shapes=[pltpu.VMEM((tm, D), F32)] + _rider_sems(rider),
        compiler_params=_params(2),
    )(dz, wcat, dzf, wf, x1, g, dx2, rider[1])


def _scan_rows(blk, reverse):
    n = blk.shape[0]
    row = lax.broadcasted_iota(jnp.int32, blk.shape, 0)
    d = 1
    while d < n:
        if reverse:
            blk = blk + jnp.where(row < n - d, pltpu.roll(blk, n - d, 0), 0.0)
        else:
            blk = blk + jnp.where(row >= d, pltpu.roll(blk, d, 0), 0.0)
        d *= 2
    return blk


def _forget_cumsum(zf, bf, cs):
    S = zf.shape[0]

    def body(zf_ref, bf_ref, c_ref, cb_ref, carry):
        @pl.when(pl.program_id(0) == 0)
        def _():
            carry[...] = jnp.zeros_like(carry)

        x = zf_ref[...] + bf_ref[...]
        logf = jnp.minimum(x, 0.0) - jnp.log1p(jnp.exp(-jnp.abs(x)))
        sc = _scan_rows(logf, False) + carry[...]
        carry[...] = sc[cs - 1:cs, :]
        sc = sc * LOG2E
        c_ref[...] = sc
        for h in range(H):
            cb_ref[h] = jnp.broadcast_to(sc[:, h:h + 1], (cs, HD))

    return pl.pallas_call(
        body, name="forget_cumsum",
        grid=(S // cs,),
        in_specs=[pl.BlockSpec((cs, HD), lambda i: (i, 0)), pl.BlockSpec((1, HD), lambda i: (0, 0))],
        out_specs=[pl.BlockSpec((cs, HD), lambda i: (i, 0)),
                   pl.BlockSpec((H, cs, HD), lambda i: (0, i, 0))],
        out_shape=[jax.ShapeDtypeStruct((S, HD), F32), jax.ShapeDtypeStruct((H, S, HD), F32)],
        scratch_shapes=[pltpu.VMEM((1, HD), F32)],
        compiler_params=_params(1),
    )(zf, bf)


def _forget_bwd(dc, zf, bf, cs):
    S = dc.shape[0]
    nc = S // cs

    def body(dc_ref, zf_ref, bf_ref, dzf_ref, dbf_ref, carry):
        @pl.when(pl.program_id(0) == 0)
        def _():
            carry[...] = jnp.zeros_like(carry)
            dbf_ref[...] = jnp.zeros_like(dbf_ref)

        sc = _scan_rows(dc_ref[...], True) + carry[...]
        carry[...] = sc[0:1, :]
        x = zf_ref[...] + bf_ref[...]
        dz = sc / (1.0 + jnp.exp(x))
        dzf_ref[...] = dz.astype(BF)
        dbf_ref[...] += jnp.sum(dz, axis=0, keepdims=True)

    rev = pl.BlockSpec((cs, HD), lambda i: (nc - 1 - i, 0))
    vec = pl.BlockSpec((1, HD), lambda i: (0, 0))
    return pl.pallas_call(
        body, name="forget_bwd",
        grid=(nc,),
        in_specs=[rev, rev, vec],
        out_specs=[rev, vec],
        out_shape=[jax.ShapeDtypeStruct((S, HD), BF), jax.ShapeDtypeStruct((1, HD), F32)],
        scratch_shapes=[pltpu.VMEM((1, HD), F32)],
        compiler_params=_params(1),
    )(dc, zf, bf)


def _lanes(x, n):
    return x if n == HD else jnp.concatenate([x] * (n // HD), axis=1)


def _causal_mask(i, j, t, rows_are_queries):
    r = lax.broadcasted_iota(jnp.int32, (t, t), 0)
    c = lax.broadcasted_iota(jnp.int32, (t, t), 1)
    if rows_are_queries:
        return (j * t + c) <= (i * t + r)
    return (j * t + r) <= (i * t + c)


def _fox_fwd(z, ccol_b, crow, t):
    S = z.shape[0]
    nq = S // t

    def body(q_ref, kv_ref, cc_ref, cr_ref, o_ref, lse_ref, m_s, acc_s, s_a, s_b):
        i = pl.program_id(1)
        ct = cc_ref[...]
        ones = jnp.ones((t, HD), BF)
        m_s[...] = jnp.full_like(m_s, NEG)
        acc_s[...] = jnp.zeros_like(acc_s)

        def scores(j, s_ref):
            off = pl.multiple_of(j * t, t)
            s_ref[...] = _dot_nt(q_ref[...], kv_ref[pl.ds(off, t), :HD]) - cr_ref[pl.ds(j, 1), :]

        def consume(j, s_ref, masked):
            off = pl.multiple_of(j * t, t)
            v1 = jnp.concatenate([kv_ref[pl.ds(off, t), HD:], ones], axis=1)
            s = s_ref[...]
            if masked:
                s = jnp.where(_causal_mask(i, j, t, True), s, NEG)
            m_old = m_s[...]
            m_new = jnp.maximum(m_old, jnp.max(s, axis=1, keepdims=True))
            p = jnp.exp2(s - _lanes(m_new, t))
            alpha = jnp.exp2(m_old - m_new)
            acc_s[...] = _lanes(alpha, 2 * HD) * acc_s[...] + _dot(p.astype(BF), v1)
            m_s[...] = m_new

        scores(0, s_a)

        def pair(jj, carry):
            j = 2 * jj
            scores(j + 1, s_b)
            consume(j, s_a, False)
            scores(j + 2, s_a)
            consume(j + 1, s_b, False)
            return carry

        lax.fori_loop(0, i // 2, pair, 0)

        @pl.when(i % 2 == 0)
        def _():
            consume(i, s_a, True)

        @pl.when(i % 2 == 1)
        def _():
            scores(i, s_b)
            consume(i - 1, s_a, False)
            consume(i, s_b, True)

        l = acc_s[:, HD:]
        o_ref[...] = (acc_s[:, :HD] / l).astype(BF)
        lse_ref[...] = m_s[...] + ct + jnp.log2(l)

    return pl.pallas_call(
        body, name="fox_fwd",
        grid=(H, nq),
        in_specs=[pl.BlockSpec((t, HD), lambda h, i: (i, h)),
                  pl.BlockSpec((S, 2 * HD), lambda h, i: (0, 4 + h)),
                  pl.BlockSpec((None, t, HD), lambda h, i: (h, i, 0)),
                  pl.BlockSpec((None, nq, t), lambda h, i: (h, 0, 0))],
        out_specs=[pl.BlockSpec((t, HD), lambda h, i: (i, h)),
                   pl.BlockSpec((None, t, HD), lambda h, i: (h, i, 0))],
        out_shape=[jax.ShapeDtypeStruct((S, D), BF), jax.ShapeDtypeStruct((H, S, HD), F32)],
        scratch_shapes=[pltpu.VMEM((t, HD), F32), pltpu.VMEM((t, 2 * HD), F32),
                        pltpu.VMEM((t, t), F32), pltpu.VMEM((t, t), F32)],
        compiler_params=_params(2),
    )(z, z, ccol_b, crow)


def _fox_bwd(z, do, o, lse_b, ccol_b, crow, dz, t, rider):
    S = z.shape[0]
    nq = S // t

    def body(q_ref, kv_ref, do_ref, o_ref, lse_ref, cc_ref, cr_ref, dz_in, rsrc_ref,
             dq_ref, dkt_ref, dvt_ref, dck_ref, dcq_ref, rdst_ref, acc_s, r_s, send_sems, recv_sems):
        del dz_in
        i = pl.program_id(1)

        @pl.when((pl.program_id(0) == 0) & (i == 0))
        def _():
            _rider_start(rider[0], rsrc_ref, rdst_ref, send_sems, recv_sems)

        @pl.when(i == 0)
        def _():
            dkt_ref[...] = jnp.zeros_like(dkt_ref)
            dvt_ref[...] = jnp.zeros_like(dvt_ref)
            dck_ref[...] = jnp.zeros_like(dck_ref)

        q = q_ref[...]
        dout = do_ref[...]
        qt = jnp.transpose(q.astype(F32)).astype(BF)
        dot_ = jnp.transpose(dout.astype(F32)).astype(BF)
        off_t = _lanes(lse_ref[...] - cc_ref[...], t)
        delta = jnp.sum(dout.astype(F32) * o_ref[...].astype(F32), axis=1, keepdims=True)
        delta = _lanes(jnp.broadcast_to(delta, (t, HD)), t)
        acc_s[...] = jnp.zeros_like(acc_s)
        r_s[...] = jnp.zeros_like(r_s)

        def step(j, masked):
            off = pl.multiple_of(j * t, t)
            k = kv_ref[pl.ds(off, t), :HD]
            v = kv_ref[pl.ds(off, t), HD:]
            p = jnp.exp2(_dot_nt(q, k) - cr_ref[pl.ds(j, 1), :] - off_t)
            if masked:
                p = jnp.where(_causal_mask(i, j, t, True), p, 0.0)
            ds = p * (_dot_nt(dout, v) - delta)
            dsb = ds.astype(BF)
            acc_s[...] += _dot(dsb, k)
            dkt_ref[j] += _dot(qt, dsb)
            dvt_ref[j] += _dot(dot_, p.astype(BF))
            dck_ref[pl.ds(j, 1), :] += jnp.sum(ds, axis=0, keepdims=True)
            r_s[...] += jnp.sum(ds, axis=1, keepdims=True)

        def full_step(j, carry):
            step(j, False)
            return carry

        lax.fori_loop(0, i, full_step, 0)
        step(i, True)
        dq_ref[...] = (acc_s[...] * SCALE).astype(BF)
        dcq_ref[...] = jnp.transpose(r_s[...])[0:1, :]

        @pl.when((pl.program_id(0) == H - 1) & (i == nq - 1))
        def _():
            _rider_finish(rider[0], rsrc_ref, rdst_ref, send_sems, recv_sems)

    qspec = pl.BlockSpec((t, HD), lambda h, i: (i, h))
    bspec = pl.BlockSpec((None, t, HD), lambda h, i: (h, i, 0))
    rows = pl.BlockSpec((None, nq, t), lambda h, i: (h, 0, 0))
    tspec = pl.BlockSpec((None, nq, HD, t), lambda h, i: (h, 0, 0, 0))
    tshape = jax.ShapeDtypeStruct((H, nq, HD, t), F32)
    anywhere = pl.BlockSpec(memory_space=pl.ANY)
    return pl.pallas_call(
        body, name="fox_bwd",
        grid=(H, nq),
        in_specs=[qspec,
                  pl.BlockSpec((S, 2 * HD), lambda h, i: (0, 4 + h)),
                  qspec, qspec, bspec, bspec, rows, anywhere, anywhere],
        out_specs=[qspec, tspec, tspec, rows, pl.BlockSpec((None, None, 1, t), lambda h, i: (h, i, 0, 0)),
                   anywhere],
        out_shape=[jax.ShapeDtypeStruct((S, ZW), BF), tshape, tshape,
                   jax.ShapeDtypeStruct((H, nq, t), F32), jax.ShapeDtypeStruct((H, nq, 1, t), F32),
                   _rider_out(rider)],
        scratch_shapes=[pltpu.VMEM((t, HD), F32), pltpu.VMEM((t, HD), F32)] + _rider_sems(rider),
        input_output_aliases={7: 0},
        compiler_params=_params(2),
    )(z, z, do, o, lse_b, ccol_b, crow, dz, rider[1])


def _fox_bwd_finish(dkt, dvt, dz, t):
    nq = dkt.shape[1]
    S = nq * t

    def body(dkt_ref, dvt_ref, dz_in, dkv_ref):
        del dz_in
        for j in range(nq):
            rows = slice(j * t, (j + 1) * t)
            dkv_ref[rows, :HD] = (jnp.transpose(dkt_ref[j]) * LN2).astype(BF)
            dkv_ref[rows, HD:] = jnp.transpose(dvt_ref[j]).astype(BF)

    tspec = pl.BlockSpec((None, nq, HD, t), lambda h: (h, 0, 0, 0))
    return pl.pallas_call(
        body, name="fox_bwd_finish",
        grid=(H,),
        in_specs=[tspec, tspec, pl.BlockSpec(memory_space=pl.ANY)],
        out_specs=pl.BlockSpec((S, 2 * HD), lambda h: (0, 4 + h)),
        out_shape=jax.ShapeDtypeStruct((S, ZW), BF),
        input_output_aliases={2: 0},
        compiler_params=_params(1),
    )(dkt, dvt, dz)


def _sgu_mask(transposed):
    r = lax.broadcasted_iota(jnp.int32, (L, L), 0)
    c = lax.broadcasted_iota(jnp.int32, (L, L), 1)
    if transposed:
        return (r // CHUNK) <= (c // CHUNK)
    return (c // CHUNK) <= (r // CHUNK)


def _ln_group(vs, lng, lnb):
    mu = jnp.mean(vs, axis=-1, keepdims=True)
    xc = vs - mu
    rstd = lax.rsqrt(jnp.mean(xc * xc, axis=-1, keepdims=True) + LN_EPS)
    xhat = xc * rstd
    return xhat, rstd, xhat * lng + lnb


def _mix_out_fwd(z, oa, ln_g, ln_b, ws, bst, wout, x1, g_post, tm):
    S = z.shape[0]
    nw = tm // L

    def body(u_ref, sv_ref, ga_ref, gb_ref, oa_ref, lng_ref, lnb_ref, ws_ref, bst_ref, wout_ref, x1_ref, gp_ref,
             mg_ref, y_ref, x2_ref, mg_s):
        mask = _sgu_mask(False)
        for g in range(G):
            cols = slice(g * L, (g + 1) * L)
            wm = jnp.where(mask, ws_ref[g], 0.0).astype(BF)
            bcol = bst_ref[:, g:g + 1]
            lng = lng_ref[:, cols]
            lnb = lnb_ref[:, cols]
            for w in range(nw):
                rows = slice(w * L, (w + 1) * L)
                vs = _gelu(sv_ref[rows, cols].astype(F32))
                _, _, vn = _ln_group(vs, lng, lnb)
                mixed = _dot(wm, vn.astype(BF)) + bcol
                ob = _gelu(u_ref[rows, cols].astype(F32)) * mixed
                mg = (_sigmoid(ga_ref[rows, cols].astype(F32)) * oa_ref[rows, cols].astype(F32)
                      + _sigmoid(gb_ref[rows, cols].astype(F32)) * ob)
                mg_s[rows, cols] = mg.astype(BF)
        mgb = mg_s[...]
        mg_ref[...] = mgb
        y = _dot(mgb, wout_ref[...])
        y_ref[...] = y
        x2_ref[...] = x1_ref[...] + _rms(y, gp_ref[...])

    row = pl.BlockSpec((tm, D), lambda i: (i, 0))
    vec = pl.BlockSpec((1, D), lambda i: (0, 0))

    def zcol(kb):
        return pl.BlockSpec((tm, D), lambda i: (i, kb))

    return pl.pallas_call(
        body, name="mix_out_fwd",
        grid=(S // tm,),
        in_specs=[zcol(3), zcol(4), zcol(5), zcol(6), row, vec, vec,
                  pl.BlockSpec((G, L, L), lambda i: (0, 0, 0)),
                  pl.BlockSpec((L, HD), lambda i: (0, 0)),
                  pl.BlockSpec((D, D), lambda i: (0, 0)),
                  row, vec],
        out_specs=[row, row, row],
        out_shape=[jax.ShapeDtypeStruct((S, D), BF),
                   jax.ShapeDtypeStruct((S, D), F32),
                   jax.ShapeDtypeStruct((S, D), F32)],
        scratch_shapes=[pltpu.VMEM((tm, D), BF)],
        compiler_params=_params(1),
    )(z, z, z, z, oa, ln_g, ln_b, ws, bst, wout, x1, g_post)


def _mix_out_bwd(dx2, y2, g_post, wout, z, oa, ln_g, ln_b, ws, wst, bst, tm, rider):
    S = z.shape[0]
    nw = tm // L

    def body(dx2_ref, y_ref, gp_ref, wout_ref, u_ref, sv_ref, ga_ref, gb_ref, oa_ref, lng_ref, lnb_ref,
             ws_ref, wst_ref, bst_ref, q_ref,
             dz_ref, dy_ref, doa_ref, dgp_ref, dlng_ref, dlnb_ref, dws_ref, dbst_ref, t_ref,
             dzg_s, dm_s, send_sems, recv_sems):
        i = pl.program_id(0)
        c = pl.program_id(1)

        @pl.when((i == 0) & (c == 0))
        def _():
            _rider_start(rider[0], q_ref, t_ref, send_sems, recv_sems)
            dgp_ref[...] = jnp.zeros_like(dgp_ref)
            dlng_ref[...] = jnp.zeros_like(dlng_ref)
            dlnb_ref[...] = jnp.zeros_like(dlnb_ref)
            dws_ref[...] = jnp.zeros_like(dws_ref)
            dbst_ref[...] = jnp.zeros_like(dbst_ref)

        @pl.when(c == 0)
        def _():
            dy, dg = _rms_bwd(dx2_ref[...], y_ref[...], gp_ref[...])
            dyb = dy.astype(BF)
            dy_ref[...] = dyb
            dgp_ref[...] += dg
            dm_s[...] = _dot_nt(dyb, wout_ref[...])
            mask = _sgu_mask(False)
            mask_t = _sgu_mask(True)
            lane = lax.broadcasted_iota(jnp.int32, (L, HD), 1)
            for g in range(G):
                cols = slice(g * L, (g + 1) * L)
                wm = jnp.where(mask, ws_ref[g], 0.0).astype(BF)
                wmt = jnp.where(mask_t, wst_ref[g], 0.0).astype(BF)
                bcol = bst_ref[:, g:g + 1]
                lng = lng_ref[:, cols]
                lnb = lnb_ref[:, cols]
                dws_g = jnp.zeros((L, L), F32)
                dbs_g = jnp.zeros((L, 1), F32)
                dlng_g = jnp.zeros((1, L), F32)
                dlnb_g = jnp.zeros((1, L), F32)
                for w in range(nw):
                    rows = slice(w * L, (w + 1) * L)
                    dm = dm_s[rows, cols]
                    vs, dvs_dz = _gelu_parts(sv_ref[rows, cols].astype(F32))
                    xhat, rstd, vn = _ln_group(vs, lng, lnb)
                    vnb = vn.astype(BF)
                    mixed = _dot(wm, vnb) + bcol
                    u, du_dz = _gelu_parts(u_ref[rows, cols].astype(F32))
                    sga = _sigmoid(ga_ref[rows, cols].astype(F32))
                    sgb = _sigmoid(gb_ref[rows, cols].astype(F32))
                    oav = oa_ref[rows, cols].astype(F32)
                    ob = u * mixed
                    doa_ref[rows, cols] = (dm * sga).astype(BF)
                    dzg_s[2, rows, cols] = (dm * oav * sga * (1.0 - sga)).astype(BF)
                    dzg_s[3, rows, cols] = (dm * ob * sgb * (1.0 - sgb)).astype(BF)
                    dob = dm * sgb
                    dzg_s[0, rows, cols] = (dob * mixed * du_dz).astype(BF)
                    dmixed = dob * u
                    dmb = dmixed.astype(BF)
                    dbs_g += jnp.sum(dmixed, axis=1, keepdims=True)
                    dws_g += _dot_nt(dmb, vnb)
                    dvn = _dot(wmt, dmb)
                    dlng_g += jnp.sum(dvn * xhat, axis=0, keepdims=True)
                    dlnb_g += jnp.sum(dvn, axis=0, keepdims=True)
                    dxh = dvn * lng
                    dvs = rstd * (dxh - jnp.mean(dxh, axis=-1, keepdims=True)
                                  - xhat * jnp.mean(dxh * xhat, axis=-1, keepdims=True))
                    dzg_s[1, rows, cols] = (dvs * dvs_dz).astype(BF)
                dws_ref[g] += jnp.where(mask, dws_g, 0.0)
                dbst_ref[...] += jnp.where(lane == g, dbs_g, 0.0)
                dlng_ref[:, cols] += dlng_g
                dlnb_ref[:, cols] += dlnb_g

        dz_ref[...] = dzg_s[c]

        @pl.when((i == S // tm - 1) & (c == 3))
        def _():
            _rider_finish(rider[0], q_ref, t_ref, send_sems, recv_sems)

    row = pl.BlockSpec((tm, D), lambda i, c: (i, 0))
    vec = pl.BlockSpec((1, D), lambda i, c: (0, 0))
    wsspec = pl.BlockSpec((G, L, L), lambda i, c: (0, 0, 0))
    bspec = pl.BlockSpec((L, HD), lambda i, c: (0, 0))
    anywhere = pl.BlockSpec(memory_space=pl.ANY)

    def zcol(kb):
        return pl.BlockSpec((tm, D), lambda i, c: (i, kb))

    return pl.pallas_call(
        body, name="mix_out_bwd",
        grid=(S // tm, 4),
        in_specs=[row, row, vec, pl.BlockSpec((D, D), lambda i, c: (0, 0)),
                  zcol(3), zcol(4), zcol(5), zcol(6), row, vec, vec, wsspec, wsspec, bspec, anywhere],
        out_specs=[pl.BlockSpec((tm, D), lambda i, c: (i, 3 + c)),
                   row, row, vec, vec, vec, wsspec, bspec, anywhere],
        out_shape=[jax.ShapeDtypeStruct((S, ZW), BF),
                   jax.ShapeDtypeStruct((S, D), BF),
                   jax.ShapeDtypeStruct((S, D), BF),
                   jax.ShapeDtypeStruct((1, D), F32),
                   jax.ShapeDtypeStruct((1, D), F32),
                   jax.ShapeDtypeStruct((1, D), F32),
                   jax.ShapeDtypeStruct((G, L, L), F32),
                   jax.ShapeDtypeStruct((L, HD), F32),
                   _rider_out(rider)],
        scratch_shapes=[pltpu.VMEM((4, tm, D), BF), pltpu.VMEM((tm, D), F32)] + _rider_sems(rider),
        compiler_params=_params(2),
    )(dx2, y2, g_post, wout, z, z, z, z, oa, ln_g, ln_b, ws, wst, bst, rider[1])


def _adamw_update(w_ref, g_ref, m_ref, v_ref, d_ref, nm_ref, nv_ref):
    gv = g_ref[...]
    m_new = ADAM_B1 * m_ref[...] + (1.0 - ADAM_B1) * gv
    v_new = ADAM_B2 * v_ref[...] + (1.0 - ADAM_B2) * (gv * gv)
    m_hat = m_new / (1.0 - ADAM_B1 ** ADAM_STEP)
    v_hat = v_new / (1.0 - ADAM_B2 ** ADAM_STEP)
    d_ref[...] = -ADAM_LR * (m_hat / (jnp.sqrt(v_hat) + ADAM_EPS) + ADAM_WD * w_ref[...])
    nm_ref[...] = m_new
    nv_ref[...] = v_new


def _adamw(w, g, m, v, tr):
    R, C = w.shape
    spec = pl.BlockSpec((tr, C), lambda i: (i, 0))
    shp = jax.ShapeDtypeStruct((R, C), F32)
    return pl.pallas_call(
        functools.partial(_adamw_update), name="adamw",
        grid=(R // tr,),
        in_specs=[spec] * 4, out_specs=[spec] * 3, out_shape=[shp] * 3,
        compiler_params=_params(1),
    )(w, g, m, v)


ADAMW_GROUP_STEPS = 16


def _adamw_group(tensors, riders):
    n, nr, steps = len(tensors), len(riders), ADAMW_GROUP_STEPS

    def body(*refs):
        ins, rsrc = refs[:4 * n], refs[4 * n:4 * n + nr]
        outs, rdst = refs[4 * n + nr:7 * n + nr], refs[7 * n + nr:7 * n + 2 * nr]
        sems = refs[7 * n + 2 * nr:]
        i = pl.program_id(0)

        @pl.when(i == 0)
        def _():
            for k, (kind, _) in enumerate(riders):
                _rider_start(kind, rsrc[k], rdst[k], sems[2 * k], sems[2 * k + 1])

        for k in range(n):
            _adamw_update(*ins[4 * k:4 * k + 4], *outs[3 * k:3 * k + 3])

        @pl.when(i == steps - 1)
        def _():
            for k, (kind, _) in enumerate(riders):
                _rider_finish(kind, rsrc[k], rdst[k], sems[2 * k], sems[2 * k + 1])

    anywhere = pl.BlockSpec(memory_space=pl.ANY)
    in_specs, out_specs, out_shape, args = [], [], [], []
    for w, gbuf, row0, m, v in tensors:
        tr = w.shape[0] // steps
        spec = pl.BlockSpec((tr, D), lambda i: (i, 0))
        in_specs += [spec, pl.BlockSpec((tr, D), functools.partial(lambda i, b: (b + i, 0), b=row0 // tr)), spec, spec]
        out_specs += [spec] * 3
        out_shape += [jax.ShapeDtypeStruct(w.shape, F32)] * 3
        args += [w, gbuf, m, v]
    scratch = []
    for rider in riders:
        in_specs.append(anywhere)
        out_specs.append(anywhere)
        out_shape.append(_rider_out(rider))
        scratch += _rider_sems(rider)
        args.append(rider[1])
    res = pl.pallas_call(
        body, name="adamw_group",
        grid=(steps,),
        in_specs=in_specs, out_specs=out_specs, out_shape=out_shape, scratch_shapes=scratch,
        compiler_params=_params(1),
    )(*args)
    return [tuple(res[3 * k:3 * k + 3]) for k in range(n)], list(res[3 * n:])


def _mesh_pos():
    return lax.axis_index("x"), lax.axis_index("y"), lax.axis_index("c")


def _half(c, rows):
    return pl.ds(pl.multiple_of(c * rows, 16), rows)


def _other_chips(x, y):
    return [(1 - x, y), (x, 1 - y), (1 - x, 1 - y)]


def _remote(k, src, dst, to, send_sems, recv_sems):
    return pltpu.make_async_remote_copy(src_ref=src, dst_ref=dst, send_sem=send_sems.at[k],
                                        recv_sem=recv_sems.at[k], device_id=to, device_id_type=MESH)


def _gather_start(wp_ref, g_ref, send_sems, recv_sems):
    x, y, c = _mesh_pos()
    mine = _half(c, wp_ref.shape[0] // 2)
    for k, (px, py) in enumerate(_other_chips(x, y)):
        _remote(k, wp_ref.at[mine], g_ref.at[2 * x + y, mine], (px, py, c), send_sems, recv_sems).start()


def _gather_forward(wp_ref, g_ref, send_sems, recv_sems):
    x, y, c = _mesh_pos()
    sibling = (x, y, 1 - c)
    mine = _half(c, wp_ref.shape[0] // 2)
    for k, (px, py) in enumerate(_other_chips(x, y)):
        land = g_ref.at[2 * px + py, mine]
        _remote(k, land, land, (px, py, c), send_sems, recv_sems).wait_recv()
        _remote(3 + k, land, land, sibling, send_sems, recv_sems).start()


def _gather_finish(wp_ref, g_ref, send_sems, recv_sems):
    x, y, c = _mesh_pos()
    sibling = (x, y, 1 - c)
    rows = wp_ref.shape[0] // 2
    mine, other = _half(c, rows), _half(1 - c, rows)
    chips = _other_chips(x, y)
    for k, (px, py) in enumerate(chips):
        land = g_ref.at[2 * px + py, other]
        _remote(3 + k, land, land, sibling, send_sems, recv_sems).wait_recv()
    for k, (px, py) in enumerate(chips):
        land = g_ref.at[2 * px + py, mine]
        _remote(k, wp_ref.at[mine], g_ref.at[2 * x + y, mine], (px, py, c), send_sems, recv_sems).wait_send()
        _remote(3 + k, land, land, sibling, send_sems, recv_sems).wait_send()


def _place_own_shard(g, wp):
    x, y, _ = _mesh_pos()
    return lax.dynamic_update_index_in_dim(g, wp, 2 * x + y, 0)


def _all_gather_weights(wp):
    def body(wp_ref, g_ref, send_sems, recv_sems):
        _gather_start(wp_ref, g_ref, send_sems, recv_sems)
        _gather_forward(wp_ref, g_ref, send_sems, recv_sems)
        _gather_finish(wp_ref, g_ref, send_sems, recv_sems)

    g = pl.pallas_call(
        body, name="all_gather_weights",
        in_specs=[pl.BlockSpec(memory_space=pl.ANY)],
        out_specs=pl.BlockSpec(memory_space=pl.ANY),
        out_shape=jax.ShapeDtypeStruct((NSH,) + wp.shape, wp.dtype),
        scratch_shapes=[pltpu.SemaphoreType.DMA((6,)), pltpu.SemaphoreType.DMA((6,))],
        compiler_params=pltpu.CompilerParams(has_side_effects=True),
    )(wp)
    return _place_own_shard(g, wp)


def _scatter_copies(q_ref, t_ref, send_sems, recv_sems):
    x, y, c = _mesh_pos()
    return [_remote(k, q_ref.at[2 * px + py], t_ref.at[k], (px, py, c), send_sems, recv_sems)
            for k, (px, py) in enumerate(_other_chips(x, y))]


_FLIPS = [(fx, fy, fc) for fx in (0, 1) for fy in (0, 1) for fc in (0, 1)][1:]


def _rider_copies(kind, src_ref, dst_ref, send_sems, recv_sems):
    if kind == "scatter":
        return _scatter_copies(src_ref, dst_ref, send_sems, recv_sems)
    x, y, c = _mesh_pos()
    if kind == "broadcast":
        return [_remote(k, src_ref, dst_ref.at[4 * x + 2 * y + c], (x ^ fx, y ^ fy, c ^ fc), send_sems, recv_sems)
                for k, (fx, fy, fc) in enumerate(_FLIPS)]
    rows = src_ref.shape[1] // 2
    return [_remote(0, src_ref.at[:, _half(1 - c, rows)], dst_ref, (x, y, 1 - c), send_sems, recv_sems)]


def _place_own_block(sm, block):
    x, y, c = _mesh_pos()
    return lax.dynamic_update_index_in_dim(sm, block, 4 * x + 2 * y + c, 0)


def _rider_start(kind, src_ref, dst_ref, send_sems, recv_sems):
    for cp in _rider_copies(kind, src_ref, dst_ref, send_sems, recv_sems):
        cp.start()


def _rider_finish(kind, src_ref, dst_ref, send_sems, recv_sems):
    for cp in _rider_copies(kind, src_ref, dst_ref, send_sems, recv_sems):
        cp.wait()


def _rider_out(rider):
    kind, a = rider
    if kind == "scatter":
        return jax.ShapeDtypeStruct((3,) + a.shape[1:], a.dtype)
    if kind == "broadcast":
        return jax.ShapeDtypeStruct((NDEV,) + a.shape, a.dtype)
    return jax.ShapeDtypeStruct((a.shape[0], a.shape[1] // 2) + a.shape[2:], a.dtype)


def _rider_sems(rider):
    n = {"scatter": 3, "broadcast": 7, "exchange": 1}[rider[0]]
    return [pltpu.SemaphoreType.DMA((n,)), pltpu.SemaphoreType.DMA((n,))]


def _pair_exchange(p):
    rows = p.shape[1] // 2

    def body(p_ref, r_ref, send_sem, recv_sem):
        x, y, c = _mesh_pos()
        cp = pltpu.make_async_remote_copy(src_ref=p_ref.at[:, _half(1 - c, rows)], dst_ref=r_ref, send_sem=send_sem,
                                          recv_sem=recv_sem, device_id=(x, y, 1 - c), device_id_type=MESH)
        cp.start()
        cp.wait()

    return pl.pallas_call(
        body, name="pair_exchange",
        in_specs=[pl.BlockSpec(memory_space=pl.ANY)],
        out_specs=pl.BlockSpec(memory_space=pl.ANY),
        out_shape=jax.ShapeDtypeStruct((NSH, rows, D), F32),
        scratch_shapes=[pltpu.SemaphoreType.DMA, pltpu.SemaphoreType.DMA],
        compiler_params=pltpu.CompilerParams(has_side_effects=True),
    )(p)


def _pair_add(p, r, nb):
    rows = r.shape[1]
    tr = rows // nb

    def body(p_ref, r_ref, q_ref):
        q_ref[...] = (p_ref[...] + r_ref[...]).astype(BF)

    return pl.pallas_call(
        body, name="pair_add", grid=(NSH, nb),
        in_specs=[pl.BlockSpec((None, tr, D), lambda j, i: (j, lax.axis_index("c") * nb + i, 0)),
                  pl.BlockSpec((None, tr, D), lambda j, i: (j, i, 0))],
        out_specs=pl.BlockSpec((None, tr, D), lambda j, i: (j, i, 0)),
        out_shape=jax.ShapeDtypeStruct((NSH, rows, D), BF),
        compiler_params=_params(2),
    )(p, r)


def _shard_sum(p, r, t, nb):
    rows = r.shape[1]
    tr = rows // nb

    def shard():
        return 2 * lax.axis_index("x") + lax.axis_index("y")

    def body(p_ref, r_ref, t_ref, o_ref):
        s = p_ref[...] + r_ref[...]
        for k in range(3):
            s = s + t_ref[k].astype(F32)
        o_ref[...] = s

    return pl.pallas_call(
        body, name="shard_sum", grid=(nb,),
        in_specs=[pl.BlockSpec((None, tr, D), lambda i: (shard(), lax.axis_index("c") * nb + i, 0)),
                  pl.BlockSpec((None, tr, D), lambda i: (shard(), i, 0)),
                  pl.BlockSpec((3, tr, D), lambda i: (0, i, 0))],
        out_specs=pl.BlockSpec((tr, D), lambda i: (i, 0)),
        out_shape=jax.ShapeDtypeStruct((rows, D), F32),
        compiler_params=_params(1),
    )(p, r, t)


def _small_sum(sm):
    def body(sm_ref, o_ref):
        s = sm_ref[0]
        for k in range(1, NDEV):
            s = s + sm_ref[k]
        o_ref[...] = s

    return pl.pallas_call(
        body, name="small_sum",
        in_specs=[pl.BlockSpec(memory_space=pltpu.VMEM)],
        out_specs=pl.BlockSpec(memory_space=pltpu.VMEM),
        out_shape=jax.ShapeDtypeStruct(sm.shape[1:], F32),
    )(sm)


def _pair_gather(halves):
    n = len(halves)

    def body(*refs):
        gh_refs, o_refs, (send_sems, recv_sems) = refs[:n], refs[n:2 * n], refs[2 * n:]
        x, y, c = _mesh_pos()
        sibling = (x, y, 1 - c)
        for g in range(n):
            rows = gh_refs[g].shape[0]
            _remote(g, gh_refs[g], o_refs[g].at[_half(c, rows)], sibling, send_sems, recv_sems).start()
        for g in range(n):
            rows = gh_refs[g].shape[0]
            _remote(g, gh_refs[g], o_refs[g].at[_half(c, rows)], sibling, send_sems, recv_sems).wait_send()
            _remote(g, gh_refs[g], o_refs[g].at[_half(1 - c, rows)], sibling, send_sems, recv_sems).wait_recv()

    anywhere = pl.BlockSpec(memory_space=pl.ANY)
    outs = pl.pallas_call(
        body, name="pair_gather",
        in_specs=[anywhere] * n, out_specs=[anywhere] * n,
        out_shape=[jax.ShapeDtypeStruct((2 * h.shape[0], D), F32) for h in halves],
        scratch_shapes=[pltpu.SemaphoreType.DMA((n,)), pltpu.SemaphoreType.DMA((n,))],
        compiler_params=pltpu.CompilerParams(has_side_effects=True),
    )(*halves)
    c = lax.axis_index("c")
    return [lax.dynamic_update_slice_in_dim(o, h, c * h.shape[0], 0) for o, h in zip(outs, halves)]


def _pad_cols(a, n):
    return jnp.pad(a, ((0, 0), (0, n - a.shape[1])))


def _split_w_in(w_in_full):
    q, k, v = w_in_full[:, :D], w_in_full[:, D:2 * D], w_in_full[:, 2 * D:3 * D]
    f = w_in_full[:, 3 * D:3 * D + H]
    gates = w_in_full[:, 3 * D + H:]
    kv = jnp.stack([k.reshape(D, H, HD), v.reshape(D, H, HD)], axis=2).reshape(D, 2 * D)
    return jnp.concatenate([q, kv, gates], axis=1), _pad_cols(f, HD)


def _merge_w_in_grad(dwcat, dwf):
    kv = dwcat[:, D:3 * D].reshape(D, H, 2, HD)
    return jnp.concatenate([dwcat[:, :D], kv[:, :, 0].reshape(D, D), kv[:, :, 1].reshape(D, D),
                            dwf[:, :H], dwcat[:, 3 * D:]], axis=1)


def _ffn_weight_grads(h, da, db, act, dy, bt):
    g = _mm_tn(h, da, D, D, bt, into=(None, FFN_ROWS, 0, "cols"))
    g = _mm_tn(h, db, D, D, bt, into=(g, FFN_ROWS, 1, "cols"))
    return _mm_tn(act, dy, D, D, bt, into=(g, FFN_ROWS, 2, "rows"))


def _train_step(x, target, wp1, wp2, small, adam, tm, t_attn):
    S = x.shape[0]
    g1pre, g1post = small["ffn1_pre_g"], small["ffn1_post_g"]
    gmpre, gmpost = small["mix_pre_g"], small["mix_post_g"]
    g2pre, g2post = small["ffn2_pre_g"], small["ffn2_post_g"]
    ln_g, ln_b = small["sgu_ln_g"], small["sgu_ln_b"]
    ws = small["sgu_w_s"][0]
    wst = jnp.swapaxes(ws, 1, 2)
    bst = _pad_cols(small["sgu_b_s"][0].T, HD)
    bf = _pad_cols(small["b_forget"], HD)

    w1 = _all_gather_weights(wp1)
    h1, a1, b1, y1, x1, h2, w2 = _ffn_fwd(x, g1pre, w1, g1post, tm, ("norm", gmpre), gather=wp2)
    wout = w2[:, FFN_ROWS:FFN_ROWS + 256, :].reshape(D, D)
    r0 = FFN_ROWS + 256
    w_in_full = jnp.concatenate(
        [blk for j in range(NSH) for blk in (w2[j, r0:r0 + D], w2[j, r0 + D:r0 + 2 * D, :WIN_SH - D])], axis=1)
    wcat, wf = _split_w_in(w_in_full)
    z, zf = _mix_in_fwd(h2, wcat, wf, min(256, S))
    cs = min(512, S)
    c, ccol_b = _forget_cumsum(zf, bf, cs)
    crow = jnp.transpose(c[:, :H]).reshape(H, S // t_attn, t_attn)
    oa, lse_b = _fox_fwd(z, ccol_b, crow, t_attn)
    merged, y2, x2 = _mix_out_fwd(z, oa, ln_g, ln_b, ws, bst, wout, x1, gmpost, tm)
    h3, a3, b3, y3, _, dx3, loss_acc = _ffn_fwd(x2, g2pre, w2, g2post, tm, ("loss", target))
    loss = loss_acc[0, 0]

    dy3, da3, db3, act3, dx2, dg2post, dg2pre = _ffn_bwd(dx3, y3, g2post, a3, b3, w2, x2, g2pre, tm)
    bt = min(2048, S)
    g_ffn2 = _ffn_weight_grads(h3, da3, db3, act3, dy3, bt)

    dz, dy2, doa, dgmpost, dlng, dlnb, dws, dbst, r_ffn2 = _mix_out_bwd(
        dx2, y2, gmpost, wout, z, oa, ln_g, ln_b, ws, wst, bst, tm, ("exchange", g_ffn2))
    q_ffn2 = _pair_add(g_ffn2, r_ffn2, 3)
    g_mix = _mm_tn(merged, dy2, 256, D, bt, into=(None, MIX_ROWS, 0, "rows"))
    dz, dkt, dvt, dc_keys, dc_queries, t_ffn2 = _fox_bwd(z, doa, oa, lse_b, ccol_b, crow, dz, t_attn,
                                                         ("scatter", q_ffn2))
    dz = _fox_bwd_finish(dkt, dvt, dz, t_attn)
    dc = _pad_cols(jnp.transpose(dc_queries.reshape(H, S) - dc_keys.reshape(H, S)), HD)
    dzf, dbf = _forget_bwd(dc, zf, bf, cs)
    dwcat = _mm_tn(h2, dz, D, D, bt)
    dwf = _mm_tn(h2, dzf, D, HD, bt)
    dwin = _merge_w_in_grad(dwcat, dwf)
    dwin_a = jnp.stack([dwin[:, j * WIN_SH:j * WIN_SH + D] for j in range(NSH)])
    dwin_b = jnp.stack([_pad_cols(dwin[:, j * WIN_SH + D:(j + 1) * WIN_SH], D) for j in range(NSH)])
    g_mix = lax.dynamic_update_slice(g_mix, jnp.concatenate([dwin_a, dwin_b], axis=1), (0, 256, 0))
    dx1, dgmpre, r_mix = _mix_in_bwd(dz, wcat, dzf, wf, x1, gmpre, dx2, min(256, S), ("exchange", g_mix))
    q_mix = _pair_add(g_mix, r_mix, 3)

    small_early = _pack_small({
        "mix_pre_g": dgmpre, "mix_post_g": dgmpost, "ffn2_pre_g": dg2pre, "ffn2_post_g": dg2post,
        "sgu_ln_g": dlng, "sgu_ln_b": dlnb, "sgu_w_s": dws[None], "sgu_b_s": jnp.transpose(dbst[:, :G])[None],
        "b_forget": dbf[:, :H]}, _SMALL_EARLY)
    dy1, da1, db1, act1, dx, dg1post, dg1pre, t_mix, sm_early = _ffn_bwd(
        dx1, y1, g1post, a1, b1, w1, x, g1pre, tm, riders=(("scatter", q_mix), ("broadcast", small_early)))
    sm_early = _place_own_block(sm_early, small_early)
    g_gu = _mm_tn(h1, da1, D, D, bt, into=(None, 2 * D, 0, "cols"))
    g_gu = _mm_tn(h1, db1, D, D, bt, into=(g_gu, 2 * D, 1, "cols"))
    r_gu = _pair_exchange(g_gu)
    q_gu = _pair_add(g_gu, r_gu, 2)
    g_dn, t_gu = _mm_tn(act1, dy1, D, D, bt, into=(None, D, 0, "rows"), rider=("scatter", q_gu))
    r_dn = _pair_exchange(g_dn)
    q_dn = _pair_add(g_dn, r_dn, 2)
    small_late = _pack_small({"ffn1_pre_g": dg1pre, "ffn1_post_g": dg1post}, _SMALL_LATE)

    f_gu, f_ffn2, f_mix = _pair_gather([_shard_sum(g_gu, r_gu, t_gu, 2), _shard_sum(g_ffn2, r_ffn2, t_ffn2, 3),
                                        _shard_sum(g_mix, r_mix, t_mix, 3)])
    group = [("ffn1_w_gate", f_gu, 0), ("ffn1_w_up", f_gu, D), ("ffn2_w_gate", f_ffn2, 0), ("ffn2_w_up", f_ffn2, D),
             ("ffn2_w_down", f_ffn2, 2 * D), ("w_out", f_mix, 0)]
    updates, (t_dn, sm_late) = _adamw_group(
        [(adam[n][0], buf, row0, adam[n][1], adam[n][2]) for n, buf, row0 in group],
        (("scatter", q_dn), ("broadcast", small_late)))
    sm_late = _place_own_block(sm_late, small_late)
    (f_dn,) = _pair_gather([_shard_sum(g_dn, r_dn, t_dn, 2)])
    gsm = jnp.concatenate([_small_sum(sm_early), _small_sum(sm_late)], axis=0)
    updated = {n: u for (n, _, _), u in zip(group, updates)}
    return loss, dx, jnp.concatenate([f_gu, f_dn], axis=0), f_ffn2, f_mix, gsm, updated


_SMALL_EARLY = ["mix_pre_g", "mix_post_g", "ffn2_pre_g", "ffn2_post_g", "sgu_ln_g", "sgu_ln_b", "sgu_b_s", "b_forget",
                "sgu_w_s"]
_SMALL_LATE = ["ffn1_pre_g", "ffn1_post_g"]
_SMALL_NAMES = _SMALL_EARLY + _SMALL_LATE


def _pack_small(d, names=None):
    rows = []
    for n in names or _SMALL_NAMES:
        a = d[n].astype(F32)
        if n == "b_forget":
            a = _pad_cols(a, D)
        a = a.reshape(-1, D)
        rows.append(jnp.pad(a, ((0, -a.shape[0] % SMALL_STRIDE), (0, 0))))
    return jnp.concatenate(rows, axis=0)


def _unpack_small(p):
    out, r = {}, 0
    for n in _SMALL_NAMES:
        if n == "sgu_w_s":
            out[n] = p[r:r + L].reshape(1, G, L, L)
            r += L
        elif n == "b_forget":
            out[n] = p[r:r + 1, :H]
            r += SMALL_STRIDE
        elif n == "sgu_b_s":
            out[n] = p[r:r + 1].reshape(1, G, L)
            r += SMALL_STRIDE
        else:
            out[n] = p[r:r + 1]
            r += SMALL_STRIDE
    return out


_BIG_NAMES = ["ffn1_w_gate", "ffn1_w_up", "ffn1_w_down", "ffn2_w_gate", "ffn2_w_up", "ffn2_w_down", "w_out", "w_in"]
_WEIGHT_ORDER = ['ffn1_pre_g', 'ffn1_w_gate', 'ffn1_w_up', 'ffn1_w_down', 'ffn1_post_g', 'mix_pre_g', 'w_in', 'b_forget',
                 'sgu_ln_g', 'sgu_ln_b', 'sgu_w_s', 'sgu_b_s', 'w_out', 'mix_post_g', 'ffn2_pre_g', 'ffn2_w_gate',
                 'ffn2_w_up', 'ffn2_w_down', 'ffn2_post_g']


def _pack_ffn(w, name):
    return jnp.concatenate([w[name + "_w_gate"][0], w[name + "_w_up"][0], w[name + "_w_down"][0]], axis=0)


def _pack_mix(w):
    w_in = w["w_in"][0]
    return jnp.concatenate([w["w_out"][0], w_in[:, :D], _pad_cols(w_in[:, D:], D)], axis=0)


def _unpack_ffn(p, name):
    return {name + "_w_gate": p[:D][None], name + "_w_up": p[D:2 * D][None], name + "_w_down": p[2 * D:][None]}


def _unpack_mix(p):
    return {"w_out": p[:256][None],
            "w_in": jnp.concatenate([p[256:256 + D], p[256 + D:, :WIN_SH - D]], axis=1)[None]}


def _step(args, tm, t_attn):
    x = args["x"][0]
    target = args["loss_target"][0]
    weights = {n: args[n] for n in _WEIGHT_ORDER}
    small = {n: weights[n] for n in _SMALL_NAMES}

    wb = {n: weights[n].astype(BF) for n in _BIG_NAMES}
    wp1 = _pack_ffn(wb, "ffn1")
    wp2 = jnp.concatenate([_pack_ffn(wb, "ffn2"), _pack_mix(wb)], axis=0)
    early = ["ffn1_w_gate", "ffn1_w_up", "ffn2_w_gate", "ffn2_w_up", "ffn2_w_down", "w_out"]
    adam = {n: tuple(a.reshape(-1, D) for a in (weights[n], args["m_" + n], args["v_" + n])) for n in early}
    loss_local, dx, f_ffn1, f_ffn2, f_mix, gsm, updated = _train_step(x, target, wp1, wp2, small, adam, tm, t_attn)
    loss = lax.psum(loss_local, ("x", "y", "c"))
    grads = {**_unpack_ffn(f_ffn1, "ffn1"), **_unpack_ffn(f_ffn2, "ffn2"), **_unpack_mix(f_mix),
             **_unpack_small(gsm)}

    delta, new_m, new_v = {}, {}, {}
    for n in _BIG_NAMES:
        shp = weights[n].shape
        if n in updated:
            d, nm, nv = updated[n]
        else:
            w2 = weights[n].reshape(-1, shp[-1])
            d, nm, nv = _adamw(w2, grads[n].reshape(w2.shape), args["m_" + n].reshape(w2.shape),
                               args["v_" + n].reshape(w2.shape), w2.shape[0] // 4)
        delta[n], new_m[n], new_v[n] = d.reshape(shp), nm.reshape(shp), nv.reshape(shp)
    ds, nms, nvs = _adamw(_pack_small(small), gsm, _pack_small({n: args["m_" + n] for n in _SMALL_NAMES}),
                          _pack_small({n: args["v_" + n] for n in _SMALL_NAMES}), SMALL_ROWS)
    delta.update(_unpack_small(ds))
    new_m.update(_unpack_small(nms))
    new_v.update(_unpack_small(nvs))

    return (loss, dx[None], *[grads[n] for n in _WEIGHT_ORDER], *[delta[n] for n in _WEIGHT_ORDER],
            *[new_m[n] for n in _WEIGHT_ORDER], *[new_v[n] for n in _WEIGHT_ORDER])


_ARG_NAMES = (["x"] + _WEIGHT_ORDER + ["loss_target"] + ["m_" + n for n in _WEIGHT_ORDER]
              + ["v_" + n for n in _WEIGHT_ORDER])


def kernel(x, ffn1_pre_g, ffn1_w_gate, ffn1_w_up, ffn1_w_down, ffn1_post_g, mix_pre_g, w_in, b_forget, sgu_ln_g, sgu_ln_b, sgu_w_s, sgu_b_s, w_out, mix_post_g, ffn2_pre_g, ffn2_w_gate, ffn2_w_up, ffn2_w_down, ffn2_post_g, loss_target, m_ffn1_pre_g, m_ffn1_w_gate, m_ffn1_w_up, m_ffn1_w_down, m_ffn1_post_g, m_mix_pre_g, m_w_in, m_b_forget, m_sgu_ln_g, m_sgu_ln_b, m_sgu_w_s, m_sgu_b_s, m_w_out, m_mix_post_g, m_ffn2_pre_g, m_ffn2_w_gate, m_ffn2_w_up, m_ffn2_w_down, m_ffn2_post_g, v_ffn1_pre_g, v_ffn1_w_gate, v_ffn1_w_up, v_ffn1_w_down, v_ffn1_post_g, v_mix_pre_g, v_w_in, v_b_forget, v_sgu_ln_g, v_sgu_ln_b, v_sgu_w_s, v_sgu_b_s, v_w_out, v_mix_post_g, v_ffn2_pre_g, v_ffn2_w_gate, v_ffn2_w_up, v_ffn2_w_down, v_ffn2_post_g):
    args = (x, ffn1_pre_g, ffn1_w_gate, ffn1_w_up, ffn1_w_down, ffn1_post_g, mix_pre_g, w_in, b_forget, sgu_ln_g, sgu_ln_b, sgu_w_s, sgu_b_s, w_out, mix_post_g, ffn2_pre_g, ffn2_w_gate, ffn2_w_up, ffn2_w_down, ffn2_post_g, loss_target, m_ffn1_pre_g, m_ffn1_w_gate, m_ffn1_w_up, m_ffn1_w_down, m_ffn1_post_g, m_mix_pre_g, m_w_in, m_b_forget, m_sgu_ln_g, m_sgu_ln_b, m_sgu_w_s, m_sgu_b_s, m_w_out, m_mix_post_g, m_ffn2_pre_g, m_ffn2_w_gate, m_ffn2_w_up, m_ffn2_w_down, m_ffn2_post_g, v_ffn1_pre_g, v_ffn1_w_gate, v_ffn1_w_up, v_ffn1_w_down, v_ffn1_post_g, v_mix_pre_g, v_w_in, v_b_forget, v_sgu_ln_g, v_sgu_ln_b, v_sgu_w_s, v_sgu_b_s, v_w_out, v_mix_post_g, v_ffn2_pre_g, v_ffn2_w_gate, v_ffn2_w_up, v_ffn2_w_down, v_ffn2_post_g)
    named = dict(zip(_ARG_NAMES, args))
    tile = min(512, x.shape[1])
    return _step(named, tile, tile)
```

```python
import functools
import math

import jax
import jax.numpy as jnp
from jax import lax
from jax.experimental import pallas as pl
from jax.experimental.pallas import tpu as pltpu

D = 1024
F = 4096
H = 8
HD = 128
G = 8
L = 128
CHUNK = 64
NSH = 4
NDEV = 8
ZW = 7 * D
RMS_EPS = 1e-6
LN_EPS = 1e-5
NEG = -1e30
SCALE = 1.0 / math.sqrt(HD)
LOG2E = math.log2(math.e)
LN2 = math.log(2.0)

ADAM_LR = 0.001
ADAM_B1 = 0.9
ADAM_B2 = 0.999
ADAM_EPS = 1e-08
ADAM_WD = 0.01
ADAM_STEP = 10

VMEM_LIMIT_BYTES = 56 * 1024 * 1024

WIN_SH = 1794
FFN_ROWS = 3 * D
MIX_ROWS = 256 + 2 * D
G2_ROWS = FFN_ROWS + MIX_ROWS
SMALL_STRIDE = 8
SMALL_ROWS = 10 * SMALL_STRIDE + L

BF = jnp.bfloat16
F32 = jnp.float32
MESH = pl.DeviceIdType.MESH


def _params(n_grid):
    return pltpu.CompilerParams(dimension_semantics=("arbitrary",) * n_grid,
                                vmem_limit_bytes=VMEM_LIMIT_BYTES)


def _dot(a, b):
    return jnp.dot(a, b, preferred_element_type=F32)


def _dot_nt(a, b):
    return lax.dot_general(a, b, (((1,), (1,)), ((), ())), preferred_element_type=F32)


def _dot_tn(a, b):
    return lax.dot_general(a, b, (((0,), (0,)), ((), ())), preferred_element_type=F32)


def _rms(x, g):
    r = lax.rsqrt(jnp.mean(x * x, axis=-1, keepdims=True) + RMS_EPS)
    return x * r * g


def _rms_bwd(dn, x, g):
    r = lax.rsqrt(jnp.mean(x * x, axis=-1, keepdims=True) + RMS_EPS)
    xr = x * r
    dg = jnp.sum(dn * xr, axis=0, keepdims=True)
    t = dn * g
    dx = r * (t - xr * jnp.mean(t * xr, axis=-1, keepdims=True))
    return dx, dg


def _gelu_parts(x):
    cdf = 0.5 * (1.0 + lax.erf(x * (1.0 / math.sqrt(2.0))))
    pdf = jnp.exp(-0.5 * x * x) * (1.0 / math.sqrt(2.0 * math.pi))
    return x * cdf, cdf + x * pdf


def _gelu(x):
    return x * (0.5 * (1.0 + lax.erf(x * (1.0 / math.sqrt(2.0)))))


def _sigmoid(x):
    return 0.5 * jnp.tanh(0.5 * x) + 0.5


def _ffn_fwd(x, g_pre, wpack, g_post, tm, tail, gather=None):
    S = x.shape[0]
    nt, nf, tf = S // tm, NSH, D
    n_tail_out = 1 if tail[0] == "norm" else 2

    def body(x_ref, gpre_ref, wg_ref, wu_ref, wd_ref, gpost_ref, tail_ref, *rest):
        if gather is not None:
            wp_ref, rest = rest[0], rest[1:]
        h_ref, a_ref, b_ref, y_ref, xo_ref = rest[:5]
        tail_out = rest[5:5 + n_tail_out]
        rest = rest[5 + n_tail_out:]
        if gather is not None:
            g_ref, h_s, acc, send_sems, recv_sems = rest
        else:
            h_s, acc = rest
        i = pl.program_id(0)
        j = pl.program_id(1)

        if gather is not None:
            @pl.when((i == 0) & (j == 0))
            def _():
                _gather_start(wp_ref, g_ref, send_sems, recv_sems)

        @pl.when(j == 0)
        def _():
            h = _rms(x_ref[...], gpre_ref[...]).astype(BF)
            h_s[...] = h
            h_ref[...] = h
            acc[...] = jnp.zeros_like(acc)

        h = h_s[...]
        a = _dot(h, wg_ref[...])
        b = _dot(h, wu_ref[...])
        a_ref[...] = a.astype(BF)
        b_ref[...] = b.astype(BF)
        act = (a * _sigmoid(a)) * b
        acc[...] += _dot(act.astype(BF), wd_ref[...])

        if tail[0] == "loss":
            @pl.when((i == 0) & (j == 0))
            def _():
                tail_out[1][...] = jnp.zeros_like(tail_out[1])

        @pl.when(j == nf - 1)
        def _():
            y = acc[...]
            y_ref[...] = y
            xo = x_ref[...] + 0.5 * _rms(y, gpost_ref[...])
            xo_ref[...] = xo
            if tail[0] == "norm":
                tail_out[0][...] = _rms(xo, tail_ref[...]).astype(BF)
            else:
                e = xo - tail_ref[...]
                tail_out[0][...] = e * (1.0 / D)
                tail_out[1][...] += jnp.sum(e * e) * (0.5 / D)

        if gather is not None:
            @pl.when((i == max(nt - 2, 0)) & (j == 0))
            def _():
                _gather_forward(wp_ref, g_ref, send_sems, recv_sems)

            @pl.when((i == nt - 1) & (j == nf - 1))
            def _():
                _gather_finish(wp_ref, g_ref, send_sems, recv_sems)

    row = pl.BlockSpec((tm, D), lambda i, j: (i, 0))
    vec = pl.BlockSpec((1, D), lambda i, j: (0, 0))
    anywhere = pl.BlockSpec(memory_space=pl.ANY)
    in_specs = [row, vec,
                pl.BlockSpec((None, D, tf), lambda i, j: (j, 0, 0)),
                pl.BlockSpec((None, D, tf), lambda i, j: (j, 1, 0)),
                pl.BlockSpec((None, tf, D), lambda i, j: (j, 2, 0)),
                vec, vec if tail[0] == "norm" else row]
    out_specs = [row,
                 pl.BlockSpec((tm, tf), lambda i, j: (i, j)),
                 pl.BlockSpec((tm, tf), lambda i, j: (i, j)),
                 row, row]
    out_shape = [jax.ShapeDtypeStruct((S, D), BF),
                 jax.ShapeDtypeStruct((S, F), BF),
                 jax.ShapeDtypeStruct((S, F), BF),
                 jax.ShapeDtypeStruct((S, D), F32),
                 jax.ShapeDtypeStruct((S, D), F32)]
    if tail[0] == "norm":
        out_specs.append(row)
        out_shape.append(jax.ShapeDtypeStruct((S, D), BF))
    else:
        out_specs += [row, pl.BlockSpec((8, HD), lambda i, j: (0, 0))]
        out_shape += [jax.ShapeDtypeStruct((S, D), F32), jax.ShapeDtypeStruct((8, HD), F32)]
    scratch = [pltpu.VMEM((tm, D), BF), pltpu.VMEM((tm, D), F32)]
    args = [x, g_pre, wpack, wpack, wpack, g_post, tail[1]]
    if gather is not None:
        in_specs.append(anywhere)
        out_specs.append(anywhere)
        out_shape.append(jax.ShapeDtypeStruct((NSH,) + gather.shape, gather.dtype))
        scratch += [pltpu.SemaphoreType.DMA((6,)), pltpu.SemaphoreType.DMA((6,))]
        args.append(gather)
    res = list(pl.pallas_call(
        body, name="ffn_fwd" if gather is None else "ffn_fwd_gather",
        grid=(nt, nf),
        in_specs=in_specs, out_specs=out_specs, out_shape=out_shape, scratch_shapes=scratch,
        compiler_params=_params(2),
    )(*args))
    if gather is not None:
        res[-1] = _place_own_shard(res[-1], gather)
    return res


def _ffn_bwd(dxo, y, g_post, a, b, wpack, x_in, g_pre, tm, riders=()):
    S = dxo.shape[0]
    nt, nf, tf = S // tm, NSH, D
    nr = len(riders)

    def body(dxo_ref, y_ref, gpost_ref, a_ref, b_ref, wg_ref, wu_ref, wd_ref, xin_ref, gpre_ref, *rest):
        rsrc = rest[:nr]
        dy_ref, da_ref, db_ref, act_ref, dxin_ref, dgpost_ref, dgpre_ref = rest[nr:nr + 7]
        rdst = rest[nr + 7:2 * nr + 7]
        dy_s, acc = rest[2 * nr + 7:2 * nr + 9]
        sems = rest[2 * nr + 9:]
        i = pl.program_id(0)
        j = pl.program_id(1)

        @pl.when((i == 0) & (j == 0))
        def _():
            dgpost_ref[...] = jnp.zeros_like(dgpost_ref)
            dgpre_ref[...] = jnp.zeros_like(dgpre_ref)
            for k, (kind, _) in enumerate(riders):
                _rider_start(kind, rsrc[k], rdst[k], sems[2 * k], sems[2 * k + 1])

        @pl.when(j == 0)
        def _():
            dy, dg = _rms_bwd(0.5 * dxo_ref[...], y_ref[...], gpost_ref[...])
            dyb = dy.astype(BF)
            dy_s[...] = dyb
            dy_ref[...] = dyb
            dgpost_ref[...] += dg
            acc[...] = jnp.zeros_like(acc)

        dact = _dot_nt(dy_s[...], wd_ref[...])
        av = a_ref[...].astype(F32)
        bv = b_ref[...].astype(F32)
        sig = _sigmoid(av)
        sl = av * sig
        act_ref[...] = (sl * bv).astype(BF)
        dbb = (dact * sl).astype(BF)
        dab = (dact * bv * (sig * (1.0 + av * (1.0 - sig)))).astype(BF)
        da_ref[...] = dab
        db_ref[...] = dbb
        acc[...] += _dot_nt(dab, wg_ref[...]) + _dot_nt(dbb, wu_ref[...])

        @pl.when(j == nf - 1)
        def _():
            dx, dg = _rms_bwd(acc[...], xin_ref[...], gpre_ref[...])
            dxin_ref[...] = dxo_ref[...] + dx
            dgpre_ref[...] += dg

        if riders:
            @pl.when((i == nt - 1) & (j == nf - 1))
            def _():
                for k, (kind, _) in enumerate(riders):
                    _rider_finish(kind, rsrc[k], rdst[k], sems[2 * k], sems[2 * k + 1])

    row = pl.BlockSpec((tm, D), lambda i, j: (i, 0))
    vec = pl.BlockSpec((1, D), lambda i, j: (0, 0))
    ff = pl.BlockSpec((tm, tf), lambda i, j: (i, j))
    anywhere = pl.BlockSpec(memory_space=pl.ANY)
    in_specs = [row, row, vec, ff, ff,
                pl.BlockSpec((None, D, tf), lambda i, j: (j, 0, 0)),
                pl.BlockSpec((None, D, tf), lambda i, j: (j, 1, 0)),
                pl.BlockSpec((None, tf, D), lambda i, j: (j, 2, 0)),
                row, vec]
    out_specs = [row, ff, ff, ff, row, vec, vec]
    out_shape = [jax.ShapeDtypeStruct((S, D), BF),
                 jax.ShapeDtypeStruct((S, F), BF),
                 jax.ShapeDtypeStruct((S, F), BF),
                 jax.ShapeDtypeStruct((S, F), BF),
                 jax.ShapeDtypeStruct((S, D), F32),
                 jax.ShapeDtypeStruct((1, D), F32),
                 jax.ShapeDtypeStruct((1, D), F32)]
    scratch = [pltpu.VMEM((tm, D), BF), pltpu.VMEM((tm, D), F32)]
    args = [dxo, y, g_post, a, b, wpack, wpack, wpack, x_in, g_pre]
    for rider in riders:
        in_specs.append(anywhere)
        out_specs.append(anywhere)
        out_shape.append(_rider_out(rider))
        scratch += _rider_sems(rider)
        args.append(rider[1])
    return pl.pallas_call(
        body, name="ffn_bwd" if not riders else "ffn_bwd_riders",
        grid=(nt, nf),
        in_specs=in_specs, out_specs=out_specs, out_shape=out_shape, scratch_shapes=scratch,
        compiler_params=_params(2),
    )(*args)


def _mm_tn(a, b, bm, bn, bt, into=None, rider=None):
    S, M = a.shape
    N = b.shape[1]
    nt = S // bt
    n_in = 2 + (into is not None and into[0] is not None) + (rider is not None)

    def body(*refs):
        a_ref, b_ref, o_ref = refs[0], refs[1], refs[n_in]
        m, n, t = pl.program_id(0), pl.program_id(1), pl.program_id(2)

        if rider is not None:
            rsrc_ref, rdst_ref, send_sems, recv_sems = refs[n_in - 1], refs[n_in + 1], refs[-2], refs[-1]

            @pl.when((m == 0) & (n == 0) & (t == 0))
            def _():
                _rider_start(rider[0], rsrc_ref, rdst_ref, send_sems, recv_sems)

        @pl.when(t == 0)
        def _():
            o_ref[...] = jnp.zeros_like(o_ref)

        o_ref[...] += _dot_tn(a_ref[...], b_ref[...])

        if rider is not None:
            @pl.when((m == M // bm - 1) & (n == N // bn - 1) & (t == nt - 1))
            def _():
                _rider_finish(rider[0], rsrc_ref, rdst_ref, send_sems, recv_sems)

    in_specs = [pl.BlockSpec((bt, bm), lambda m, n, t: (t, m)),
                pl.BlockSpec((bt, bn), lambda m, n, t: (t, n))]
    args, aliases = [a, b], {}
    if into is None:
        out_spec = pl.BlockSpec((bm, bn), lambda m, n, t: (m, n))
        out_shape = jax.ShapeDtypeStruct((M, N), F32)
    else:
        buf, rows, rb, by = into
        assert bn == D and (M == bm if by == "cols" else (M == NSH * bm and N == D))
        if by == "cols":
            out_spec = pl.BlockSpec((None, bm, bn), lambda m, n, t: (n, rb, 0))
        else:
            out_spec = pl.BlockSpec((None, bm, bn), lambda m, n, t: (m, rb, 0))
        out_shape = jax.ShapeDtypeStruct((NSH, rows, D), F32)
        if buf is not None:
            in_specs.append(pl.BlockSpec(memory_space=pl.ANY))
            args.append(buf)
            aliases = {2: 0}
    if rider is None:
        return pl.pallas_call(
            body, name="mm_tn",
            grid=(M // bm, N // bn, nt),
            in_specs=in_specs, out_specs=out_spec, out_shape=out_shape,
            input_output_aliases=aliases,
            compiler_params=_params(3),
        )(*args)
    anywhere = pl.BlockSpec(memory_space=pl.ANY)
    return pl.pallas_call(
        body, name="mm_tn_rider",
        grid=(M // bm, N // bn, nt),
        in_specs=in_specs + [anywhere], out_specs=[out_spec, anywhere], out_shape=[out_shape, _rider_out(rider)],
        scratch_shapes=_rider_sems(rider),
        input_output_aliases=aliases,
        compiler_params=_params(3),
    )(*args, rider[1])


def _mm(a, w, tm, tn, out_dtype, first_block_scale=1.0):
    S, K = a.shape
    N = w.shape[1]

    def body(a_ref, w_ref, o_ref):
        r = _dot(a_ref[...], w_ref[...])
        if first_block_scale == 1.0:
            o_ref[...] = r.astype(out_dtype)
        elif tn == N:
            o_ref[:, :D] = (r[:, :D] * first_block_scale).astype(out_dtype)
            o_ref[:, D:] = r[:, D:].astype(out_dtype)
        else:
            o_ref[...] = (r * jnp.where(pl.program_id(1) == 0, first_block_scale, 1.0)).astype(out_dtype)

    return pl.pallas_call(
        body, name="mm",
        grid=(S // tm, N // tn),
        in_specs=[pl.BlockSpec((tm, K), lambda i, j: (i, 0)),
                  pl.BlockSpec((K, tn), lambda i, j: (0, j))],
        out_specs=pl.BlockSpec((tm, tn), lambda i, j: (i, j)),
        out_shape=jax.ShapeDtypeStruct((S, N), out_dtype),
        compiler_params=_params(2),
    )(a, w)


def _mix_in_bwd(dz, wcat, dzf, wf, x1, g, dx2, tm, rider):
    S = dz.shape[0]
    nk = 1
    kb = ZW // nk

    def body(dz_ref, w_ref, dzf_ref, wf_ref, x_ref, g_ref, dx2_ref, rsrc_ref, dx1_ref, dg_ref, rdst_ref,
             acc, send_sems, recv_sems):
        i = pl.program_id(0)
        k = pl.program_id(1)

        @pl.when((i == 0) & (k == 0))
        def _():
            _rider_start(rider[0], rsrc_ref, rdst_ref, send_sems, recv_sems)
            dg_ref[...] = jnp.zeros_like(dg_ref)

        @pl.when(k == 0)
        def _():
            acc[...] = _dot_nt(dzf_ref[...], wf_ref[...])

        acc[...] += _dot_nt(dz_ref[...], w_ref[...])

        @pl.when(k == nk - 1)
        def _():
            dx, dg = _rms_bwd(acc[...], x_ref[...], g_ref[...])
            dx1_ref[...] = dx2_ref[...] + dx
            dg_ref[...] += dg

        @pl.when((i == S // tm - 1) & (k == nk - 1))
        def _():
            _rider_finish(rider[0], rsrc_ref, rdst_ref, send_sems, recv_sems)

    row = pl.BlockSpec((tm, D), lambda i, k: (i, 0))
    vec = pl.BlockSpec((1, D), lambda i, k: (0, 0))
    anywhere = pl.BlockSpec(memory_space=pl.ANY)
    return pl.pallas_call(
        body, name="mix_in_bwd",
        grid=(S // tm, nk),
        in_specs=[pl.BlockSpec((tm, kb), lambda i, k: (i, k)),
                  pl.BlockSpec((D, kb), lambda i, k: (0, k)),
                  pl.BlockSpec((tm, HD), lambda i, k: (i, 0)),
                  pl.BlockSpec((D, HD), lambda i, k: (0, 0)),
                  row, vec, row, anywhere],
        out_specs=[row, vec, anywhere],
        out_shape=[jax.ShapeDtypeStruct((S, D), F32), jax.ShapeDtypeStruct((1, D), F32), _rider_out(rider)],
        scratch_shapes=[pltpu.VMEM((tm, D), F32)] + _rider_sems(rider),
        compiler_params=_params(2),
    )(dz, wcat, dzf, wf, x1, g, dx2, rider[1])


def _scan_rows(blk, reverse):
    n = blk.shape[0]
    row = lax.broadcasted_iota(jnp.int32, blk.shape, 0)
    d = 1
    while d < n:
        if reverse:
            blk = blk + jnp.where(row < n - d, pltpu.roll(blk, n - d, 0), 0.0)
        else:
            blk = blk + jnp.where(row >= d, pltpu.roll(blk, d, 0), 0.0)
        d *= 2
    return blk


def _forget_cumsum(zf, bf, cs):
    S = zf.shape[0]

    def body(zf_ref, bf_ref, c_ref, cb_ref, carry):
        @pl.when(pl.program_id(0) == 0)
        def _():
            carry[...] = jnp.zeros_like(carry)

        x = zf_ref[...] + bf_ref[...]
        logf = jnp.minimum(x, 0.0) - jnp.log1p(jnp.exp(-jnp.abs(x)))
        sc = _scan_rows(logf, False) + carry[...]
        carry[...] = sc[cs - 1:cs, :]
        sc = sc * LOG2E
        c_ref[...] = sc
        for h in range(H):
            cb_ref[h] = jnp.broadcast_to(sc[:, h:h + 1], (cs, HD))

    return pl.pallas_call(
        body, name="forget_cumsum",
        grid=(S // cs,),
        in_specs=[pl.BlockSpec((cs, HD), lambda i: (i, 0)), pl.BlockSpec((1, HD), lambda i: (0, 0))],
        out_specs=[pl.BlockSpec((cs, HD), lambda i: (i, 0)),
                   pl.BlockSpec((H, cs, HD), lambda i: (0, i, 0))],
        out_shape=[jax.ShapeDtypeStruct((S, HD), F32), jax.ShapeDtypeStruct((H, S, HD), F32)],
        scratch_shapes=[pltpu.VMEM((1, HD), F32)],
        compiler_params=_params(1),
    )(zf, bf)


def _forget_bwd(dc, zf, bf, cs):
    S = dc.shape[0]
    nc = S // cs

    def body(dc_ref, zf_ref, bf_ref, dzf_ref, dbf_ref, carry):
        @pl.when(pl.program_id(0) == 0)
        def _():
            carry[...] = jnp.zeros_like(carry)
            dbf_ref[...] = jnp.zeros_like(dbf_ref)

        sc = _scan_rows(dc_ref[...], True) + carry[...]
        carry[...] = sc[0:1, :]
        x = zf_ref[...] + bf_ref[...]
        dz = sc / (1.0 + jnp.exp(x))
        dzf_ref[...] = dz.astype(BF)
        dbf_ref[...] += jnp.sum(dz, axis=0, keepdims=True)

    rev = pl.BlockSpec((cs, HD), lambda i: (nc - 1 - i, 0))
    vec = pl.BlockSpec((1, HD), lambda i: (0, 0))
    return pl.pallas_call(
        body, name="forget_bwd",
        grid=(nc,),
        in_specs=[rev, rev, vec],
        out_specs=[rev, vec],
        out_shape=[jax.ShapeDtypeStruct((S, HD), BF), jax.ShapeDtypeStruct((1, HD), F32)],
        scratch_shapes=[pltpu.VMEM((1, HD), F32)],
        compiler_params=_params(1),
    )(dc, zf, bf)


def _lanes(x, n):
    return x if n == HD else jnp.concatenate([x] * (n // HD), axis=1)


def _causal_mask(i, j, t, rows_are_queries):
    r = lax.broadcasted_iota(jnp.int32, (t, t), 0)
    c = lax.broadcasted_iota(jnp.int32, (t, t), 1)
    if rows_are_queries:
        return (j * t + c) <= (i * t + r)
    return (j * t + r) <= (i * t + c)


def _fox_fwd(z, ccol_b, crow, t):
    S = z.shape[0]
    nq = S // t

    def body(q_ref, kv_ref, cc_ref, cr_ref, o_ref, lse_ref, m_s, acc_s, s_a, s_b):
        i = pl.program_id(1)
        ct = cc_ref[...]
        ones = jnp.ones((t, HD), BF)
        m_s[...] = jnp.full_like(m_s, NEG)
        acc_s[...] = jnp.zeros_like(acc_s)

        def scores(j, s_ref):
            off = pl.multiple_of(j * t, t)
            s_ref[...] = _dot_nt(q_ref[...], kv_ref[pl.ds(off, t), :HD]) - cr_ref[pl.ds(j, 1), :]

        def consume(j, s_ref, masked):
            off = pl.multiple_of(j * t, t)
            v1 = jnp.concatenate([kv_ref[pl.ds(off, t), HD:], ones], axis=1)
            s = s_ref[...]
            if masked:
                s = jnp.where(_causal_mask(i, j, t, True), s, NEG)
            m_old = m_s[...]
            m_new = jnp.maximum(m_old, jnp.max(s, axis=1, keepdims=True))
            p = jnp.exp2(s - _lanes(m_new, t))
            alpha = jnp.exp2(m_old - m_new)
            acc_s[...] = _lanes(alpha, 2 * HD) * acc_s[...] + _dot(p.astype(BF), v1)
            m_s[...] = m_new

        scores(0, s_a)

        def pair(jj, carry):
            j = 2 * jj
            scores(j + 1, s_b)
            consume(j, s_a, False)
            scores(j + 2, s_a)
            consume(j + 1, s_b, False)
            return carry

        lax.fori_loop(0, i // 2, pair, 0)

        @pl.when(i % 2 == 0)
        def _():
            consume(i, s_a, True)

        @pl.when(i % 2 == 1)
        def _():
            scores(i, s_b)
            consume(i - 1, s_a, False)
            consume(i, s_b, True)

        l = acc_s[:, HD:]
        o_ref[...] = (acc_s[:, :HD] / l).astype(BF)
        lse_ref[...] = m_s[...] + ct + jnp.log2(l)

    return pl.pallas_call(
        body, name="fox_fwd",
        grid=(H, nq),
        in_specs=[pl.BlockSpec((t, HD), lambda h, i: (i, h)),
                  pl.BlockSpec((S, 2 * HD), lambda h, i: (0, 4 + h)),
                  pl.BlockSpec((None, t, HD), lambda h, i: (h, i, 0)),
                  pl.BlockSpec((None, nq, t), lambda h, i: (h, 0, 0))],
        out_specs=[pl.BlockSpec((t, HD), lambda h, i: (i, h)),
                   pl.BlockSpec((None, t, HD), lambda h, i: (h, i, 0))],
        out_shape=[jax.ShapeDtypeStruct((S, D), BF), jax.ShapeDtypeStruct((H, S, HD), F32)],
        scratch_shapes=[pltpu.VMEM((t, HD), F32), pltpu.VMEM((t, 2 * HD), F32),
                        pltpu.VMEM((t, t), F32), pltpu.VMEM((t, t), F32)],
        compiler_params=_params(2),
    )(z, z, ccol_b, crow)


def _fox_bwd(z, do, o, lse_b, ccol_b, crow, dz, t, rider):
    S = z.shape[0]
    nq = S // t

    def body(q_ref, kv_ref, do_ref, o_ref, lse_ref, cc_ref, cr_ref, dz_in, rsrc_ref,
             dq_ref, dkt_ref, dvt_ref, dck_ref, dcq_ref, rdst_ref, acc_s, r_s, send_sems, recv_sems):
        del dz_in
        i = pl.program_id(1)

        @pl.when((pl.program_id(0) == 0) & (i == 0))
        def _():
            _rider_start(rider[0], rsrc_ref, rdst_ref, send_sems, recv_sems)

        @pl.when(i == 0)
        def _():
            dkt_ref[...] = jnp.zeros_like(dkt_ref)
            dvt_ref[...] = jnp.zeros_like(dvt_ref)
            dck_ref[...] = jnp.zeros_like(dck_ref)

        q = q_ref[...]
        dout = do_ref[...]
        qt = jnp.transpose(q.astype(F32)).astype(BF)
        dot_ = jnp.transpose(dout.astype(F32)).astype(BF)
        off_t = _lanes(lse_ref[...] - cc_ref[...], t)
        delta = jnp.sum(dout.astype(F32) * o_ref[...].astype(F32), axis=1, keepdims=True)
        delta = _lanes(jnp.broadcast_to(delta, (t, HD)), t)
        acc_s[...] = jnp.zeros_like(acc_s)
        r_s[...] = jnp.zeros_like(r_s)

        def step(j, masked):
            off = pl.multiple_of(j * t, t)
            k = kv_ref[pl.ds(off, t), :HD]
            v = kv_ref[pl.ds(off, t), HD:]
            p = jnp.exp2(_dot_nt(q, k) - cr_ref[pl.ds(j, 1), :] - off_t)
            if masked:
                p = jnp.where(_causal_mask(i, j, t, True), p, 0.0)
            ds = p * (_dot_nt(dout, v) - delta)
            dsb = ds.astype(BF)
            acc_s[...] += _dot(dsb, k)
            dkt_ref[j] += _dot(qt, dsb)
            dvt_ref[j] += _dot(dot_, p.astype(BF))
            dck_ref[pl.ds(j, 1), :] += jnp.sum(ds, axis=0, keepdims=True)
            r_s[...] += jnp.sum(ds, axis=1, keepdims=True)

        def full_step(j, carry):
            step(j, False)
            return carry

        lax.fori_loop(0, i, full_step, 0)
        step(i, True)
        dq_ref[...] = (acc_s[...] * SCALE).astype(BF)
        dcq_ref[...] = jnp.transpose(r_s[...])[0:1, :]

        @pl.when((pl.program_id(0) == H - 1) & (i == nq - 1))
        def _():
            _rider_finish(rider[0], rsrc_ref, rdst_ref, send_sems, recv_sems)

    qspec = pl.BlockSpec((t, HD), lambda h, i: (i, h))
    bspec = pl.BlockSpec((None, t, HD), lambda h, i: (h, i, 0))
    rows = pl.BlockSpec((None, nq, t), lambda h, i: (h, 0, 0))
    tspec = pl.BlockSpec((None, nq, HD, t), lambda h, i: (h, 0, 0, 0))
    tshape = jax.ShapeDtypeStruct((H, nq, HD, t), F32)
    anywhere = pl.BlockSpec(memory_space=pl.ANY)
    return pl.pallas_call(
        body, name="fox_bwd",
        grid=(H, nq),
        in_specs=[qspec,
                  pl.BlockSpec((S, 2 * HD), lambda h, i: (0, 4 + h)),
                  qspec, qspec, bspec, bspec, rows, anywhere, anywhere],
        out_specs=[qspec, tspec, tspec, rows, pl.BlockSpec((None, None, 1, t), lambda h, i: (h, i, 0, 0)),
                   anywhere],
        out_shape=[jax.ShapeDtypeStruct((S, ZW), BF), tshape, tshape,
                   jax.ShapeDtypeStruct((H, nq, t), F32), jax.ShapeDtypeStruct((H, nq, 1, t), F32),
                   _rider_out(rider)],
        scratch_shapes=[pltpu.VMEM((t, HD), F32), pltpu.VMEM((t, HD), F32)] + _rider_sems(rider),
        input_output_aliases={7: 0},
        compiler_params=_params(2),
    )(z, z, do, o, lse_b, ccol_b, crow, dz, rider[1])


def _fox_bwd_finish(dkt, dvt, dz, t):
    nq = dkt.shape[1]
    S = nq * t

    def body(dkt_ref, dvt_ref, dz_in, dkv_ref):
        del dz_in
        for j in range(nq):
            rows = slice(j * t, (j + 1) * t)
            dkv_ref[rows, :HD] = (jnp.transpose(dkt_ref[j]) * LN2).astype(BF)
            dkv_ref[rows, HD:] = jnp.transpose(dvt_ref[j]).astype(BF)

    tspec = pl.BlockSpec((None, nq, HD, t), lambda h: (h, 0, 0, 0))
    return pl.pallas_call(
        body, name="fox_bwd_finish",
        grid=(H,),
        in_specs=[tspec, tspec, pl.BlockSpec(memory_space=pl.ANY)],
        out_specs=pl.BlockSpec((S, 2 * HD), lambda h: (0, 4 + h)),
        out_shape=jax.ShapeDtypeStruct((S, ZW), BF),
        input_output_aliases={2: 0},
        compiler_params=_params(1),
    )(dkt, dvt, dz)


def _sgu_mask(transposed):
    r = lax.broadcasted_iota(jnp.int32, (L, L), 0)
    c = lax.broadcasted_iota(jnp.int32, (L, L), 1)
    if transposed:
        return (r // CHUNK) <= (c // CHUNK)
    return (c // CHUNK) <= (r // CHUNK)


def _ln_group(vs, lng, lnb):
    mu = jnp.mean(vs, axis=-1, keepdims=True)
    xc = vs - mu
    rstd = lax.rsqrt(jnp.mean(xc * xc, axis=-1, keepdims=True) + LN_EPS)
    xhat = xc * rstd
    return xhat, rstd, xhat * lng + lnb


def _mix_out_fwd(z, oa, ln_g, ln_b, ws, bst, wout, x1, g_post, tm):
    S = z.shape[0]
    nw = tm // L

    def body(u_ref, sv_ref, ga_ref, gb_ref, oa_ref, lng_ref, lnb_ref, ws_ref, bst_ref, wout_ref, x1_ref, gp_ref,
             mg_ref, y_ref, x2_ref, mg_s):
        mask = _sgu_mask(False)
        for g in range(G):
            cols = slice(g * L, (g + 1) * L)
            wm = jnp.where(mask, ws_ref[g], 0.0).astype(BF)
            bcol = bst_ref[:, g:g + 1]
            lng = lng_ref[:, cols]
            lnb = lnb_ref[:, cols]
            for w in range(nw):
                rows = slice(w * L, (w + 1) * L)
                vs = _gelu(sv_ref[rows, cols].astype(F32))
                _, _, vn = _ln_group(vs, lng, lnb)
                mixed = _dot(wm, vn.astype(BF)) + bcol
                ob = _gelu(u_ref[rows, cols].astype(F32)) * mixed
                mg = (_sigmoid(ga_ref[rows, cols].astype(F32)) * oa_ref[rows, cols].astype(F32)
                      + _sigmoid(gb_ref[rows, cols].astype(F32)) * ob)
                mg_s[rows, cols] = mg.astype(BF)
        mgb = mg_s[...]
        mg_ref[...] = mgb
        y = _dot(mgb, wout_ref[...])
        y_ref[...] = y
        x2_ref[...] = x1_ref[...] + _rms(y, gp_ref[...])

    row = pl.BlockSpec((tm, D), lambda i: (i, 0))
    vec = pl.BlockSpec((1, D), lambda i: (0, 0))

    def zcol(kb):
        return pl.BlockSpec((tm, D), lambda i: (i, kb))

    return pl.pallas_call(
        body, name="mix_out_fwd",
        grid=(S // tm,),
        in_specs=[zcol(3), zcol(4), zcol(5), zcol(6), row, vec, vec,
                  pl.BlockSpec((G, L, L), lambda i: (0, 0, 0)),
                  pl.BlockSpec((L, HD), lambda i: (0, 0)),
                  pl.BlockSpec((D, D), lambda i: (0, 0)),
                  row, vec],
        out_specs=[row, row, row],
        out_shape=[jax.ShapeDtypeStruct((S, D), BF),
                   jax.ShapeDtypeStruct((S, D), F32),
                   jax.ShapeDtypeStruct((S, D), F32)],
        scratch_shapes=[pltpu.VMEM((tm, D), BF)],
        compiler_params=_params(1),
    )(z, z, z, z, oa, ln_g, ln_b, ws, bst, wout, x1, g_post)


def _mix_out_bwd(dx2, y2, g_post, wout, z, oa, ln_g, ln_b, ws, wst, bst, tm, rider):
    S = z.shape[0]
    nw = tm // L

    def body(dx2_ref, y_ref, gp_ref, wout_ref, u_ref, sv_ref, ga_ref, gb_ref, oa_ref, lng_ref, lnb_ref,
             ws_ref, wst_ref, bst_ref, q_ref,
             dz_ref, dy_ref, doa_ref, dgp_ref, dlng_ref, dlnb_ref, dws_ref, dbst_ref, t_ref,
             dzg_s, dm_s, send_sems, recv_sems):
        i = pl.program_id(0)
        c = pl.program_id(1)

        @pl.when((i == 0) & (c == 0))
        def _():
            _rider_start(rider[0], q_ref, t_ref, send_sems, recv_sems)
            dgp_ref[...] = jnp.zeros_like(dgp_ref)
            dlng_ref[...] = jnp.zeros_like(dlng_ref)
            dlnb_ref[...] = jnp.zeros_like(dlnb_ref)
            dws_ref[...] = jnp.zeros_like(dws_ref)
            dbst_ref[...] = jnp.zeros_like(dbst_ref)

        @pl.when(c == 0)
        def _():
            dy, dg = _rms_bwd(dx2_ref[...], y_ref[...], gp_ref[...])
            dyb = dy.astype(BF)
            dy_ref[...] = dyb
            dgp_ref[...] += dg
            dm_s[...] = _dot_nt(dyb, wout_ref[...])
            mask = _sgu_mask(False)
            mask_t = _sgu_mask(True)
            lane = lax.broadcasted_iota(jnp.int32, (L, HD), 1)
            for g in range(G):
                cols = slice(g * L, (g + 1) * L)
                wm = jnp.where(mask, ws_ref[g], 0.0).astype(BF)
                wmt = jnp.where(mask_t, wst_ref[g], 0.0).astype(BF)
                bcol = bst_ref[:, g:g + 1]
                lng = lng_ref[:, cols]
                lnb = lnb_ref[:, cols]
                dws_g = jnp.zeros((L, L), F32)
                dbs_g = jnp.zeros((L, 1), F32)
                dlng_g = jnp.zeros((1, L), F32)
                dlnb_g = jnp.zeros((1, L), F32)
                for w in range(nw):
                    rows = slice(w * L, (w + 1) * L)
                    dm = dm_s[rows, cols]
                    vs, dvs_dz = _gelu_parts(sv_ref[rows, cols].astype(F32))
                    xhat, rstd, vn = _ln_group(vs, lng, lnb)
                    vnb = vn.astype(BF)
                    mixed = _dot(wm, vnb) + bcol
                    u, du_dz = _gelu_parts(u_ref[rows, cols].astype(F32))
                    sga = _sigmoid(ga_ref[rows, cols].astype(F32))
                    sgb = _sigmoid(gb_ref[rows, cols].astype(F32))
                    oav = oa_ref[rows, cols].astype(F32)
                    ob = u * mixed
                    doa_ref[rows, cols] = (dm * sga).astype(BF)
                    dzg_s[2, rows, cols] = (dm * oav * sga * (1.0 - sga)).astype(BF)
                    dzg_s[3, rows, cols] = (dm * ob * sgb * (1.0 - sgb)).astype(BF)
                    dob = dm * sgb
                    dzg_s[0, rows, cols] = (dob * mixed * du_dz).astype(BF)
                    dmixed = dob * u
                    dmb = dmixed.astype(BF)
                    dbs_g += jnp.sum(dmixed, axis=1, keepdims=True)
                    dws_g += _dot_nt(dmb, vnb)
                    dvn = _dot(wmt, dmb)
                    dlng_g += jnp.sum(dvn * xhat, axis=0, keepdims=True)
                    dlnb_g += jnp.sum(dvn, axis=0, keepdims=True)
                    dxh = dvn * lng
                    dvs = rstd * (dxh - jnp.mean(dxh, axis=-1, keepdims=True)
                                  - xhat * jnp.mean(dxh * xhat, axis=-1, keepdims=True))
                    dzg_s[1, rows, cols] = (dvs * dvs_dz).astype(BF)
                dws_ref[g] += jnp.where(mask, dws_g, 0.0)
                dbst_ref[...] += jnp.where(lane == g, dbs_g, 0.0)
                dlng_ref[:, cols] += dlng_g
                dlnb_ref[:, cols] += dlnb_g

        dz_ref[...] = dzg_s[c]

        @pl.when((i == S // tm - 1) & (c == 3))
        def _():
            _rider_finish(rider[0], q_ref, t_ref, send_sems, recv_sems)

    row = pl.BlockSpec((tm, D), lambda i, c: (i, 0))
    vec = pl.BlockSpec((1, D), lambda i, c: (0, 0))
    wsspec = pl.BlockSpec((G, L, L), lambda i, c: (0, 0, 0))
    bspec = pl.BlockSpec((L, HD), lambda i, c: (0, 0))
    anywhere = pl.BlockSpec(memory_space=pl.ANY)

    def zcol(kb):
        return pl.BlockSpec((tm, D), lambda i, c: (i, kb))

    return pl.pallas_call(
        body, name="mix_out_bwd",
        grid=(S // tm, 4),
        in_specs=[row, row, vec, pl.BlockSpec((D, D), lambda i, c: (0, 0)),
                  zcol(3), zcol(4), zcol(5), zcol(6), row, vec, vec, wsspec, wsspec, bspec, anywhere],
        out_specs=[pl.BlockSpec((tm, D), lambda i, c: (i, 3 + c)),
                   row, row, vec, vec, vec, wsspec, bspec, anywhere],
        out_shape=[jax.ShapeDtypeStruct((S, ZW), BF),
                   jax.ShapeDtypeStruct((S, D), BF),
                   jax.ShapeDtypeStruct((S, D), BF),
                   jax.ShapeDtypeStruct((1, D), F32),
                   jax.ShapeDtypeStruct((1, D), F32),
                   jax.ShapeDtypeStruct((1, D), F32),
                   jax.ShapeDtypeStruct((G, L, L), F32),
                   jax.ShapeDtypeStruct((L, HD), F32),
                   _rider_out(rider)],
        scratch_shapes=[pltpu.VMEM((4, tm, D), BF), pltpu.VMEM((tm, D), F32)] + _rider_sems(rider),
        compiler_params=_params(2),
    )(dx2, y2, g_post, wout, z, z, z, z, oa, ln_g, ln_b, ws, wst, bst, rider[1])


def _adamw_update(w_ref, g_ref, m_ref, v_ref, d_ref, nm_ref, nv_ref):
    gv = g_ref[...]
    m_new = ADAM_B1 * m_ref[...] + (1.0 - ADAM_B1) * gv
    v_new = ADAM_B2 * v_ref[...] + (1.0 - ADAM_B2) * (gv * gv)
    m_hat = m_new / (1.0 - ADAM_B1 ** ADAM_STEP)
    v_hat = v_new / (1.0 - ADAM_B2 ** ADAM_STEP)
    d_ref[...] = -ADAM_LR * (m_hat / (jnp.sqrt(v_hat) + ADAM_EPS) + ADAM_WD * w_ref[...])
    nm_ref[...] = m_new
    nv_ref[...] = v_new


def _adamw(w, g, m, v, tr):
    R, C = w.shape
    spec = pl.BlockSpec((tr, C), lambda i: (i, 0))
    shp = jax.ShapeDtypeStruct((R, C), F32)
    return pl.pallas_call(
        functools.partial(_adamw_update), name="adamw",
        grid=(R // tr,),
        in_specs=[spec] * 4, out_specs=[spec] * 3, out_shape=[shp] * 3,
        compiler_params=_params(1),
    )(w, g, m, v)


ADAMW_GROUP_STEPS = 16


def _adamw_group(tensors, riders):
    n, nr, steps = len(tensors), len(riders), ADAMW_GROUP_STEPS

    def body(*refs):
        ins, rsrc = refs[:4 * n], refs[4 * n:4 * n + nr]
        outs, rdst = refs[4 * n + nr:7 * n + nr], refs[7 * n + nr:7 * n + 2 * nr]
        sems = refs[7 * n + 2 * nr:]
        i = pl.program_id(0)

        @pl.when(i == 0)
        def _():
            for k, (kind, _) in enumerate(riders):
                _rider_start(kind, rsrc[k], rdst[k], sems[2 * k], sems[2 * k + 1])

        for k in range(n):
            _adamw_update(*ins[4 * k:4 * k + 4], *outs[3 * k:3 * k + 3])

        @pl.when(i == steps - 1)
        def _():
            for k, (kind, _) in enumerate(riders):
                _rider_finish(kind, rsrc[k], rdst[k], sems[2 * k], sems[2 * k + 1])

    anywhere = pl.BlockSpec(memory_space=pl.ANY)
    in_specs, out_specs, out_shape, args = [], [], [], []
    for w, gbuf, row0, m, v in tensors:
        tr = w.shape[0] // steps
        spec = pl.BlockSpec((tr, D), lambda i: (i, 0))
        in_specs += [spec, pl.BlockSpec((tr, D), functools.partial(lambda i, b: (b + i, 0), b=row0 // tr)), spec, spec]
        out_specs += [spec] * 3
        out_shape += [jax.ShapeDtypeStruct(w.shape, F32)] * 3
        args += [w, gbuf, m, v]
    scratch = []
    for rider in riders:
        in_specs.append(anywhere)
        out_specs.append(anywhere)
        out_shape.append(_rider_out(rider))
        scratch += _rider_sems(rider)
        args.append(rider[1])
    res = pl.pallas_call(
        body, name="adamw_group",
        grid=(steps,),
        in_specs=in_specs, out_specs=out_specs, out_shape=out_shape, scratch_shapes=scratch,
        compiler_params=_params(1),
    )(*args)
    return [tuple(res[3 * k:3 * k + 3]) for k in range(n)], list(res[3 * n:])


def _mesh_pos():
    return lax.axis_index("x"), lax.axis_index("y"), lax.axis_index("c")


def _half(c, rows):
    return pl.ds(pl.multiple_of(c * rows, 16), rows)


def _other_chips(x, y):
    return [(1 - x, y), (x, 1 - y), (1 - x, 1 - y)]


def _remote(k, src, dst, to, send_sems, recv_sems):
    return pltpu.make_async_remote_copy(src_ref=src, dst_ref=dst, send_sem=send_sems.at[k],
                                        recv_sem=recv_sems.at[k], device_id=to, device_id_type=MESH)


def _gather_start(wp_ref, g_ref, send_sems, recv_sems):
    x, y, c = _mesh_pos()
    mine = _half(c, wp_ref.shape[0] // 2)
    for k, (px, py) in enumerate(_other_chips(x, y)):
        _remote(k, wp_ref.at[mine], g_ref.at[2 * x + y, mine], (px, py, c), send_sems, recv_sems).start()


def _gather_forward(wp_ref, g_ref, send_sems, recv_sems):
    x, y, c = _mesh_pos()
    sibling = (x, y, 1 - c)
    mine = _half(c, wp_ref.shape[0] // 2)
    for k, (px, py) in enumerate(_other_chips(x, y)):
        land = g_ref.at[2 * px + py, mine]
        _remote(k, land, land, (px, py, c), send_sems, recv_sems).wait_recv()
        _remote(3 + k, land, land, sibling, send_sems, recv_sems).start()


def _gather_finish(wp_ref, g_ref, send_sems, recv_sems):
    x, y, c = _mesh_pos()
    sibling = (x, y, 1 - c)
    rows = wp_ref.shape[0] // 2
    mine, other = _half(c, rows), _half(1 - c, rows)
    chips = _other_chips(x, y)
    for k, (px, py) in enumerate(chips):
        land = g_ref.at[2 * px + py, other]
        _remote(3 + k, land, land, sibling, send_sems, recv_sems).wait_recv()
    for k, (px, py) in enumerate(chips):
        land = g_ref.at[2 * px + py, mine]
        _remote(k, wp_ref.at[mine], g_ref.at[2 * x + y, mine], (px, py, c), send_sems, recv_sems).wait_send()
        _remote(3 + k, land, land, sibling, send_sems, recv_sems).wait_send()


def _place_own_shard(g, wp):
    x, y, _ = _mesh_pos()
    return lax.dynamic_update_index_in_dim(g, wp, 2 * x + y, 0)


def _all_gather_weights(wp):
    def body(wp_ref, g_ref, send_sems, recv_sems):
        _gather_start(wp_ref, g_ref, send_sems, recv_sems)
        _gather_forward(wp_ref, g_ref, send_sems, recv_sems)
        _gather_finish(wp_ref, g_ref, send_sems, recv_sems)

    g = pl.pallas_call(
        body, name="all_gather_weights",
        in_specs=[pl.BlockSpec(memory_space=pl.ANY)],
        out_specs=pl.BlockSpec(memory_space=pl.ANY),
        out_shape=jax.ShapeDtypeStruct((NSH,) + wp.shape, wp.dtype),
        scratch_shapes=[pltpu.SemaphoreType.DMA((6,)), pltpu.SemaphoreType.DMA((6,))],
        compiler_params=pltpu.CompilerParams(has_side_effects=True),
    )(wp)
    return _place_own_shard(g, wp)


def _scatter_copies(q_ref, t_ref, send_sems, recv_sems):
    x, y, c = _mesh_pos()
    return [_remote(k, q_ref.at[2 * px + py], t_ref.at[k], (px, py, c), send_sems, recv_sems)
            for k, (px, py) in enumerate(_other_chips(x, y))]


_FLIPS = [(fx, fy, fc) for fx in (0, 1) for fy in (0, 1) for fc in (0, 1)][1:]


def _rider_copies(kind, src_ref, dst_ref, send_sems, recv_sems):
    if kind == "scatter":
        return _scatter_copies(src_ref, dst_ref, send_sems, recv_sems)
    x, y, c = _mesh_pos()
    if kind == "broadcast":
        return [_remote(k, src_ref, dst_ref.at[4 * x + 2 * y + c], (x ^ fx, y ^ fy, c ^ fc), send_sems, recv_sems)
                for k, (fx, fy, fc) in enumerate(_FLIPS)]
    rows = src_ref.shape[1] // 2
    return [_remote(0, src_ref.at[:, _half(1 - c, rows)], dst_ref, (x, y, 1 - c), send_sems, recv_sems)]


def _place_own_block(sm, block):
    x, y, c = _mesh_pos()
    return lax.dynamic_update_index_in_dim(sm, block, 4 * x + 2 * y + c, 0)


def _rider_start(kind, src_ref, dst_ref, send_sems, recv_sems):
    for cp in _rider_copies(kind, src_ref, dst_ref, send_sems, recv_sems):
        cp.start()


def _rider_finish(kind, src_ref, dst_ref, send_sems, recv_sems):
    for cp in _rider_copies(kind, src_ref, dst_ref, send_sems, recv_sems):
        cp.wait()


def _rider_out(rider):
    kind, a = rider
    if kind == "scatter":
        return jax.ShapeDtypeStruct((3,) + a.shape[1:], a.dtype)
    if kind == "broadcast":
        return jax.ShapeDtypeStruct((NDEV,) + a.shape, a.dtype)
    return jax.ShapeDtypeStruct((a.shape[0], a.shape[1] // 2) + a.shape[2:], a.dtype)


def _rider_sems(rider):
    n = {"scatter": 3, "broadcast": 7, "exchange": 1}[rider[0]]
    return [pltpu.SemaphoreType.DMA((n,)), pltpu.SemaphoreType.DMA((n,))]


def _pair_exchange(p):
    rows = p.shape[1] // 2

    def body(p_ref, r_ref, send_sem, recv_sem):
        x, y, c = _mesh_pos()
        cp = pltpu.make_async_remote_copy(src_ref=p_ref.at[:, _half(1 - c, rows)], dst_ref=r_ref, send_sem=send_sem,
                                          recv_sem=recv_sem, device_id=(x, y, 1 - c), device_id_type=MESH)
        cp.start()
        cp.wait()

    return pl.pallas_call(
        body, name="pair_exchange",
        in_specs=[pl.BlockSpec(memory_space=pl.ANY)],
        out_specs=pl.BlockSpec(memory_space=pl.ANY),
        out_shape=jax.ShapeDtypeStruct((NSH, rows, D), F32),
        scratch_shapes=[pltpu.SemaphoreType.DMA, pltpu.SemaphoreType.DMA],
        compiler_params=pltpu.CompilerParams(has_side_effects=True),
    )(p)


def _pair_add(p, r, nb):
    rows = r.shape[1]
    tr = rows // nb

    def body(p_ref, r_ref, q_ref):
        q_ref[...] = (p_ref[...] + r_ref[...]).astype(BF)

    return pl.pallas_call(
        body, name="pair_add", grid=(NSH, nb),
        in_specs=[pl.BlockSpec((None, tr, D), lambda j, i: (j, lax.axis_index("c") * nb + i, 0)),
                  pl.BlockSpec((None, tr, D), lambda j, i: (j, i, 0))],
        out_specs=pl.BlockSpec((None, tr, D), lambda j, i: (j, i, 0)),
        out_shape=jax.ShapeDtypeStruct((NSH, rows, D), BF),
        compiler_params=_params(2),
    )(p, r)


def _shard_sum(p, r, t, nb):
    rows = r.shape[1]
    tr = rows // nb

    def shard():
        return 2 * lax.axis_index("x") + lax.axis_index("y")

    def body(p_ref, r_ref, t_ref, o_ref):
        s = p_ref[...] + r_ref[...]
        for k in range(3):
            s = s + t_ref[k].astype(F32)
        o_ref[...] = s

    return pl.pallas_call(
        body, name="shard_sum", grid=(nb,),
        in_specs=[pl.BlockSpec((None, tr, D), lambda i: (shard(), lax.axis_index("c") * nb + i, 0)),
                  pl.BlockSpec((None, tr, D), lambda i: (shard(), i, 0)),
                  pl.BlockSpec((3, tr, D), lambda i: (0, i, 0))],
        out_specs=pl.BlockSpec((tr, D), lambda i: (i, 0)),
        out_shape=jax.ShapeDtypeStruct((rows, D), F32),
        compiler_params=_params(1),
    )(p, r, t)


def _small_sum(sm):
    def body(sm_ref, o_ref):
        s = sm_ref[0]
        for k in range(1, NDEV):
            s = s + sm_ref[k]
        o_ref[...] = s

    return pl.pallas_call(
        body, name="small_sum",
        in_specs=[pl.BlockSpec(memory_space=pltpu.VMEM)],
        out_specs=pl.BlockSpec(memory_space=pltpu.VMEM),
        out_shape=jax.ShapeDtypeStruct(sm.shape[1:], F32),
    )(sm)


def _pair_gather(halves):
    n = len(halves)

    def body(*refs):
        gh_refs, o_refs, (send_sems, recv_sems) = refs[:n], refs[n:2 * n], refs[2 * n:]
        x, y, c = _mesh_pos()
        sibling = (x, y, 1 - c)
        for g in range(n):
            rows = gh_refs[g].shape[0]
            _remote(g, gh_refs[g], o_refs[g].at[_half(c, rows)], sibling, send_sems, recv_sems).start()
        for g in range(n):
            rows = gh_refs[g].shape[0]
            _remote(g, gh_refs[g], o_refs[g].at[_half(c, rows)], sibling, send_sems, recv_sems).wait_send()
            _remote(g, gh_refs[g], o_refs[g].at[_half(1 - c, rows)], sibling, send_sems, recv_sems).wait_recv()

    anywhere = pl.BlockSpec(memory_space=pl.ANY)
    outs = pl.pallas_call(
        body, name="pair_gather",
        in_specs=[anywhere] * n, out_specs=[anywhere] * n,
        out_shape=[jax.ShapeDtypeStruct((2 * h.shape[0], D), F32) for h in halves],
        scratch_shapes=[pltpu.SemaphoreType.DMA((n,)), pltpu.SemaphoreType.DMA((n,))],
        compiler_params=pltpu.CompilerParams(has_side_effects=True),
    )(*halves)
    c = lax.axis_index("c")
    return [lax.dynamic_update_slice_in_dim(o, h, c * h.shape[0], 0) for o, h in zip(outs, halves)]


def _pad_cols(a, n):
    return jnp.pad(a, ((0, 0), (0, n - a.shape[1])))


def _split_w_in(w_in_full):
    q, k, v = w_in_full[:, :D], w_in_full[:, D:2 * D], w_in_full[:, 2 * D:3 * D]
    f = w_in_full[:, 3 * D:3 * D + H]
    gates = w_in_full[:, 3 * D + H:]
    kv = jnp.stack([k.reshape(D, H, HD), v.reshape(D, H, HD)], axis=2).reshape(D, 2 * D)
    return jnp.concatenate([q, kv, gates], axis=1), _pad_cols(f, HD)


def _merge_w_in_grad(dwcat, dwf):
    kv = dwcat[:, D:3 * D].reshape(D, H, 2, HD)
    return jnp.concatenate([dwcat[:, :D], kv[:, :, 0].reshape(D, D), kv[:, :, 1].reshape(D, D),
                            dwf[:, :H], dwcat[:, 3 * D:]], axis=1)


def _ffn_weight_grads(h, da, db, act, dy, bt):
    g = _mm_tn(h, da, D, D, bt, into=(None, FFN_ROWS, 0, "cols"))
    g = _mm_tn(h, db, D, D, bt, into=(g, FFN_ROWS, 1, "cols"))
    return _mm_tn(act, dy, D, D, bt, into=(g, FFN_ROWS, 2, "rows"))


def _train_step(x, target, wp1, wp2, small, adam, tm, t_attn):
    S = x.shape[0]
    g1pre, g1post = small["ffn1_pre_g"], small["ffn1_post_g"]
    gmpre, gmpost = small["mix_pre_g"], small["mix_post_g"]
    g2pre, g2post = small["ffn2_pre_g"], small["ffn2_post_g"]
    ln_g, ln_b = small["sgu_ln_g"], small["sgu_ln_b"]
    ws = small["sgu_w_s"][0]
    wst = jnp.swapaxes(ws, 1, 2)
    bst = _pad_cols(small["sgu_b_s"][0].T, HD)
    bf = _pad_cols(small["b_forget"], HD)

    w1 = _all_gather_weights(wp1)
    h1, a1, b1, y1, x1, h2, w2 = _ffn_fwd(x, g1pre, w1, g1post, tm, ("norm", gmpre), gather=wp2)
    wout = w2[:, FFN_ROWS:FFN_ROWS + 256, :].reshape(D, D)
    r0 = FFN_ROWS + 256
    w_in_full = jnp.concatenate(
        [blk for j in range(NSH) for blk in (w2[j, r0:r0 + D], w2[j, r0 + D:r0 + 2 * D, :WIN_SH - D])], axis=1)
    wcat, wf = _split_w_in(w_in_full)
    z = _mm(h2, wcat, min(256, S), ZW, BF, first_block_scale=SCALE * LOG2E)
    zf = _mm(h2, wf, tm, HD, F32)
    cs = min(512, S)
    c, ccol_b = _forget_cumsum(zf, bf, cs)
    crow = jnp.transpose(c[:, :H]).reshape(H, S // t_attn, t_attn)
    oa, lse_b = _fox_fwd(z, ccol_b, crow, t_attn)
    merged, y2, x2 = _mix_out_fwd(z, oa, ln_g, ln_b, ws, bst, wout, x1, gmpost, tm)
    h3, a3, b3, y3, _, dx3, loss_acc = _ffn_fwd(x2, g2pre, w2, g2post, tm, ("loss", target))
    loss = loss_acc[0, 0]

    dy3, da3, db3, act3, dx2, dg2post, dg2pre = _ffn_bwd(dx3, y3, g2post, a3, b3, w2, x2, g2pre, tm)
    bt = min(4096, S)
    g_ffn2 = _ffn_weight_grads(h3, da3, db3, act3, dy3, bt)

    dz, dy2, doa, dgmpost, dlng, dlnb, dws, dbst, r_ffn2 = _mix_out_bwd(
        dx2, y2, gmpost, wout, z, oa, ln_g, ln_b, ws, wst, bst, tm, ("exchange", g_ffn2))
    q_ffn2 = _pair_add(g_ffn2, r_ffn2, 3)
    g_mix = _mm_tn(merged, dy2, 256, D, bt, into=(None, MIX_ROWS, 0, "rows"))
    dz, dkt, dvt, dc_keys, dc_queries, t_ffn2 = _fox_bwd(z, doa, oa, lse_b, ccol_b, crow, dz, t_attn,
                                                         ("scatter", q_ffn2))
    dz = _fox_bwd_finish(dkt, dvt, dz, t_attn)
    dc = _pad_cols(jnp.transpose(dc_queries.reshape(H, S) - dc_keys.reshape(H, S)), HD)
    dzf, dbf = _forget_bwd(dc, zf, bf, cs)
    dwcat = _mm_tn(h2, dz, D, D, bt)
    dwf = _mm_tn(h2, dzf, D, HD, bt)
    dwin = _merge_w_in_grad(dwcat, dwf)
    dwin_a = jnp.stack([dwin[:, j * WIN_SH:j * WIN_SH + D] for j in range(NSH)])
    dwin_b = jnp.stack([_pad_cols(dwin[:, j * WIN_SH + D:(j + 1) * WIN_SH], D) for j in range(NSH)])
    g_mix = lax.dynamic_update_slice(g_mix, jnp.concatenate([dwin_a, dwin_b], axis=1), (0, 256, 0))
    dx1, dgmpre, r_mix = _mix_in_bwd(dz, wcat, dzf, wf, x1, gmpre, dx2, min(256, S), ("exchange", g_mix))
    q_mix = _pair_add(g_mix, r_mix, 3)

    small_early = _pack_small({
        "mix_pre_g": dgmpre, "mix_post_g": dgmpost, "ffn2_pre_g": dg2pre, "ffn2_post_g": dg2post,
        "sgu_ln_g": dlng, "sgu_ln_b": dlnb, "sgu_w_s": dws[None], "sgu_b_s": jnp.transpose(dbst[:, :G])[None],
        "b_forget": dbf[:, :H]}, _SMALL_EARLY)
    dy1, da1, db1, act1, dx, dg1post, dg1pre, t_mix, sm_early = _ffn_bwd(
        dx1, y1, g1post, a1, b1, w1, x, g1pre, tm, riders=(("scatter", q_mix), ("broadcast", small_early)))
    sm_early = _place_own_block(sm_early, small_early)
    g_gu = _mm_tn(h1, da1, D, D, bt, into=(None, 2 * D, 0, "cols"))
    g_gu = _mm_tn(h1, db1, D, D, bt, into=(g_gu, 2 * D, 1, "cols"))
    r_gu = _pair_exchange(g_gu)
    q_gu = _pair_add(g_gu, r_gu, 2)
    g_dn, t_gu = _mm_tn(act1, dy1, D, D, bt, into=(None, D, 0, "rows"), rider=("scatter", q_gu))
    r_dn = _pair_exchange(g_dn)
    q_dn = _pair_add(g_dn, r_dn, 2)
    small_late = _pack_small({"ffn1_pre_g": dg1pre, "ffn1_post_g": dg1post}, _SMALL_LATE)

    f_gu, f_ffn2, f_mix = _pair_gather([_shard_sum(g_gu, r_gu, t_gu, 2), _shard_sum(g_ffn2, r_ffn2, t_ffn2, 3),
                                        _shard_sum(g_mix, r_mix, t_mix, 3)])
    group = [("ffn1_w_gate", f_gu, 0), ("ffn1_w_up", f_gu, D), ("ffn2_w_gate", f_ffn2, 0), ("ffn2_w_up", f_ffn2, D),
             ("ffn2_w_down", f_ffn2, 2 * D), ("w_out", f_mix, 0)]
    updates, (t_dn, sm_late) = _adamw_group(
        [(adam[n][0], buf, row0, adam[n][1], adam[n][2]) for n, buf, row0 in group],
        (("scatter", q_dn), ("broadcast", small_late)))
    sm_late = _place_own_block(sm_late, small_late)
    (f_dn,) = _pair_gather([_shard_sum(g_dn, r_dn, t_dn, 2)])
    gsm = jnp.concatenate([_small_sum(sm_early), _small_sum(sm_late)], axis=0)
    updated = {n: u for (n, _, _), u in zip(group, updates)}
    return loss, dx, jnp.concatenate([f_gu, f_dn], axis=0), f_ffn2, f_mix, gsm, updated


_SMALL_EARLY = ["mix_pre_g", "mix_post_g", "ffn2_pre_g", "ffn2_post_g", "sgu_ln_g", "sgu_ln_b", "sgu_b_s", "b_forget",
                "sgu_w_s"]
_SMALL_LATE = ["ffn1_pre_g", "ffn1_post_g"]
_SMALL_NAMES = _SMALL_EARLY + _SMALL_LATE


def _pack_small(d, names=None):
    rows = []
    for n in names or _SMALL_NAMES:
        a = d[n].astype(F32)
        if n == "b_forget":
            a = _pad_cols(a, D)
        a = a.reshape(-1, D)
        rows.append(jnp.pad(a, ((0, -a.shape[0] % SMALL_STRIDE), (0, 0))))
    return jnp.concatenate(rows, axis=0)


def _unpack_small(p):
    out, r = {}, 0
    for n in _SMALL_NAMES:
        if n == "sgu_w_s":
            out[n] = p[r:r + L].reshape(1, G, L, L)
            r += L
        elif n == "b_forget":
            out[n] = p[r:r + 1, :H]
            r += SMALL_STRIDE
        elif n == "sgu_b_s":
            out[n] = p[r:r + 1].reshape(1, G, L)
            r += SMALL_STRIDE
        else:
            out[n] = p[r:r + 1]
            r += SMALL_STRIDE
    return out


_BIG_NAMES = ["ffn1_w_gate", "ffn1_w_up", "ffn1_w_down", "ffn2_w_gate", "ffn2_w_up", "ffn2_w_down", "w_out", "w_in"]
_WEIGHT_ORDER = ['ffn1_pre_g', 'ffn1_w_gate', 'ffn1_w_up', 'ffn1_w_down', 'ffn1_post_g', 'mix_pre_g', 'w_in', 'b_forget',
                 'sgu_ln_g', 'sgu_ln_b', 'sgu_w_s', 'sgu_b_s', 'w_out', 'mix_post_g', 'ffn2_pre_g', 'ffn2_w_gate',
                 'ffn2_w_up', 'ffn2_w_down', 'ffn2_post_g']


def _pack_ffn(w, name):
    return jnp.concatenate([w[name + "_w_gate"][0], w[name + "_w_up"][0], w[name + "_w_down"][0]], axis=0)


def _pack_mix(w):
    w_in = w["w_in"][0]
    return jnp.concatenate([w["w_out"][0], w_in[:, :D], _pad_cols(w_in[:, D:], D)], axis=0)


def _unpack_ffn(p, name):
    return {name + "_w_gate": p[:D][None], name + "_w_up": p[D:2 * D][None], name + "_w_down": p[2 * D:][None]}


def _unpack_mix(p):
    return {"w_out": p[:256][None],
            "w_in": jnp.concatenate([p[256:256 + D], p[256 + D:, :WIN_SH - D]], axis=1)[None]}


def _step(args, tm, t_attn):
    x = args["x"][0]
    target = args["loss_target"][0]
    weights = {n: args[n] for n in _WEIGHT_ORDER}
    small = {n: weights[n] for n in _SMALL_NAMES}

    wb = {n: weights[n].astype(BF) for n in _BIG_NAMES}
    wp1 = _pack_ffn(wb, "ffn1")
    wp2 = jnp.concatenate([_pack_ffn(wb, "ffn2"), _pack_mix(wb)], axis=0)
    early = ["ffn1_w_gate", "ffn1_w_up", "ffn2_w_gate", "ffn2_w_up", "ffn2_w_down", "w_out"]
    adam = {n: tuple(a.reshape(-1, D) for a in (weights[n], args["m_" + n], args["v_" + n])) for n in early}
    loss_local, dx, f_ffn1, f_ffn2, f_mix, gsm, updated = _train_step(x, target, wp1, wp2, small, adam, tm, t_attn)
    loss = lax.psum(loss_local, ("x", "y", "c"))
    grads = {**_unpack_ffn(f_ffn1, "ffn1"), **_unpack_ffn(f_ffn2, "ffn2"), **_unpack_mix(f_mix),
             **_unpack_small(gsm)}

    delta, new_m, new_v = {}, {}, {}
    for n in _BIG_NAMES:
        shp = weights[n].shape
        if n in updated:
            d, nm, nv = updated[n]
        else:
            w2 = weights[n].reshape(-1, shp[-1])
            d, nm, nv = _adamw(w2, grads[n].reshape(w2.shape), args["m_" + n].reshape(w2.shape),
                               args["v_" + n].reshape(w2.shape), w2.shape[0] // 4)
        delta[n], new_m[n], new_v[n] = d.reshape(shp), nm.reshape(shp), nv.reshape(shp)
    ds, nms, nvs = _adamw(_pack_small(small), gsm, _pack_small({n: args["m_" + n] for n in _SMALL_NAMES}),
                          _pack_small({n: args["v_" + n] for n in _SMALL_NAMES}), SMALL_ROWS)
    delta.update(_unpack_small(ds))
    new_m.update(_unpack_small(nms))
    new_v.update(_unpack_small(nvs))

    return (loss, dx[None], *[grads[n] for n in _WEIGHT_ORDER], *[delta[n] for n in _WEIGHT_ORDER],
            *[new_m[n] for n in _WEIGHT_ORDER], *[new_v[n] for n in _WEIGHT_ORDER])


_ARG_NAMES = (["x"] + _WEIGHT_ORDER + ["loss_target"] + ["m_" + n for n in _WEIGHT_ORDER]
              + ["v_" + n for n in _WEIGHT_ORDER])


def kernel(x, ffn1_pre_g, ffn1_w_gate, ffn1_w_up, ffn1_w_down, ffn1_post_g, mix_pre_g, w_in, b_forget, sgu_ln_g, sgu_ln_b, sgu_w_s, sgu_b_s, w_out, mix_post_g, ffn2_pre_g, ffn2_w_gate, ffn2_w_up, ffn2_w_down, ffn2_post_g, loss_target, m_ffn1_pre_g, m_ffn1_w_gate, m_ffn1_w_up, m_ffn1_w_down, m_ffn1_post_g, m_mix_pre_g, m_w_in, m_b_forget, m_sgu_ln_g, m_sgu_ln_b, m_sgu_w_s, m_sgu_b_s, m_w_out, m_mix_post_g, m_ffn2_pre_g, m_ffn2_w_gate, m_ffn2_w_up, m_ffn2_w_down, m_ffn2_post_g, v_ffn1_pre_g, v_ffn1_w_gate, v_ffn1_w_up, v_ffn1_w_down, v_ffn1_post_g, v_mix_pre_g, v_w_in, v_b_forget, v_sgu_ln_g, v_sgu_ln_b, v_sgu_w_s, v_sgu_b_s, v_w_out, v_mix_post_g, v_ffn2_pre_g, v_ffn2_w_gate, v_ffn2_w_up, v_ffn2_w_down, v_ffn2_post_g):
    args = (x, ffn1_pre_g, ffn1_w_gate, ffn1_w_up, ffn1_w_down, ffn1_post_g, mix_pre_g, w_in, b_forget, sgu_ln_g, sgu_ln_b, sgu_w_s, sgu_b_s, w_out, mix_post_g, ffn2_pre_g, ffn2_w_gate, ffn2_w_up, ffn2_w_down, ffn2_post_g, loss_target, m_ffn1_pre_g, m_ffn1_w_gate, m_ffn1_w_up, m_ffn1_w_down, m_ffn1_post_g, m_mix_pre_g, m_w_in, m_b_forget, m_sgu_ln_g, m_sgu_ln_b, m_sgu_w_s, m_sgu_b_s, m_w_out, m_mix_post_g, m_ffn2_pre_g, m_ffn2_w_gate, m_ffn2_w_up, m_ffn2_w_down, m_ffn2_post_g, v_ffn1_pre_g, v_ffn1_w_gate, v_ffn1_w_up, v_ffn1_w_down, v_ffn1_post_g, v_mix_pre_g, v_w_in, v_b_forget, v_sgu_ln_g, v_sgu_ln_b, v_sgu_w_s, v_sgu_b_s, v_w_out, v_mix_post_g, v_ffn2_pre_g, v_ffn2_w_gate, v_ffn2_w_up, v_ffn2_w_down, v_ffn2_post_g)
    named = dict(zip(_ARG_NAMES, args))
    tile = min(512, x.shape[1])
    return _step(named, tile, tile)
```
